```python
import jax, jax.numpy as jnp
from jax import lax
import numpy as np

D_MODEL = 2048
BATCH = 8
SEQ = 4096
DEPTH = 2

N_EVEN = (DEPTH + 1) // 2
N_ODD = DEPTH // 2
W_A = D_MODEL // 2
W_B = D_MODEL // 2
HEAD_DIM = 64
CONV_A = 3
CONV_B = 31
W_C = D_MODEL
POOL_WINDOWS = (2, 4, 8, 16)
N_POOL_GROUPS = len(POOL_WINDOWS)
G_C = W_C // N_POOL_GROUPS
EVEN_IN = 4 * W_A + 3 * W_B
ODD_IN = 2 * W_C
EPS = 1e-6

kernel_name = "hybrid_shortconv_conformer_pool_sandwich"


def _rmsnorm(x, g):
    xf = x.astype(jnp.float32)
    r = lax.rsqrt(jnp.mean(xf * xf, axis=-1, keepdims=True) + EPS)
    return (xf * r).astype(x.dtype) * g


def _layernorm(x, g, b):
    xf = x.astype(jnp.float32)
    mu = jnp.mean(xf, axis=-1, keepdims=True)
    var = jnp.mean(jnp.square(xf - mu), axis=-1, keepdims=True)
    return ((xf - mu) * lax.rsqrt(var + EPS)).astype(x.dtype) * g + b


def _causal_dwconv(x, w):
    k, c = w.shape
    return lax.conv_general_dilated(
        x, w[:, None, :].astype(x.dtype), window_strides=(1,),
        padding=[(k - 1, 0)], dimension_numbers=("NWC", "WIO", "NWC"),
        feature_group_count=c)


def _causal_pool_means(v):
    s = v.shape[1]
    cs = jnp.cumsum(v.astype(jnp.float32), axis=1)
    pos = jnp.arange(1, s + 1, dtype=jnp.int32)
    outs = []
    for win, c in zip(POOL_WINDOWS, jnp.split(cs, N_POOL_GROUPS, axis=-1)):
        shifted = jnp.pad(c, ((0, 0), (win, 0), (0, 0)))[:, :s]
        cnt = jnp.minimum(pos, win).astype(jnp.float32)[None, :, None]
        outs.append((c - shifted) / cnt)
    return jnp.concatenate(outs, axis=-1).astype(v.dtype)


def _even_mixer(h, w_in, a_conv, b_conv, b_conv_bias, b_ln_g, b_ln_b, w_out):
    p = h @ w_in
    a_x, a_b, a_c, a_z, b_val, b_gate, b_z = jnp.split(p, 7, axis=-1)
    ya = a_b * _causal_dwconv(a_c * a_x, a_conv)
    yb = b_val * jax.nn.sigmoid(b_gate)
    yb = _causal_dwconv(yb, b_conv) + b_conv_bias
    yb = jax.nn.silu(_layernorm(yb, b_ln_g, b_ln_b))
    u = jnp.concatenate([ya * jax.nn.silu(a_z), yb * jax.nn.silu(b_z)], axis=-1)
    return u @ w_out


def _odd_mixer(h, w_in, c_w, c_b, c_scale, w_out):
    p = h @ w_in
    v, z = jnp.split(p, 2, axis=-1)
    pooled = _causal_pool_means(v) - v
    bsz, s, _ = v.shape
    g = pooled.reshape(bsz, s, N_POOL_GROUPS, G_C)
    g = jnp.einsum("bsgc,gcd->bsgd", g, c_w) + c_b
    y = g.reshape(bsz, s, W_C) * c_scale
    return (y * jax.nn.silu(z)) @ w_out


def _fwd_setup_inputs(seed: int = 0) -> dict:
    key = jax.random.key(seed)
    ks = jax.random.split(key, 24)
    f32 = jnp.float32

    def nrm(k, shape, scale):
        return jax.random.normal(k, shape, f32) * scale

    def gain(k, shape):
        return 1.0 + 0.05 * jax.random.normal(k, shape, f32)

    return {
        "x": jax.random.normal(ks[0], (BATCH, SEQ, D_MODEL), f32),
        "e_norm_pre": gain(ks[1], (N_EVEN, D_MODEL)),
        "e_norm_post": gain(ks[2], (N_EVEN, D_MODEL)),
        "e_w_in": nrm(ks[3], (N_EVEN, D_MODEL, EVEN_IN), D_MODEL ** -0.5),
        "e_a_conv": nrm(ks[4], (N_EVEN, CONV_A, W_A), CONV_A ** -0.5),
        "e_b_conv": nrm(ks[5], (N_EVEN, CONV_B, W_B), CONV_B ** -0.5),
        "e_b_conv_bias": nrm(ks[6], (N_EVEN, W_B), 0.02),
        "e_b_ln_g": gain(ks[7], (N_EVEN, W_B)),
        "e_b_ln_b": nrm(ks[8], (N_EVEN, W_B), 0.02),
        "e_w_out": nrm(ks[9], (N_EVEN, W_A + W_B, D_MODEL), (W_A + W_B) ** -0.5),
        "o_norm_pre": gain(ks[10], (N_ODD, D_MODEL)),
        "o_norm_post": gain(ks[11], (N_ODD, D_MODEL)),
        "o_w_in": nrm(ks[12], (N_ODD, D_MODEL, ODD_IN), D_MODEL ** -0.5),
        "o_c_w": nrm(ks[13], (N_ODD, N_POOL_GROUPS, G_C, G_C), G_C ** -0.5),
        "o_c_b": nrm(ks[14], (N_ODD, N_POOL_GROUPS, G_C), 0.02),
        "o_c_scale": gain(ks[15], (N_ODD, W_C)),
        "o_w_out": nrm(ks[16], (N_ODD, W_C, D_MODEL), W_C ** -0.5),
    }


def _fwd_reference(x, e_norm_pre, e_norm_post, e_w_in, e_a_conv, e_b_conv, e_b_conv_bias,
              e_b_ln_g, e_b_ln_b, e_w_out, o_norm_pre, o_norm_post, o_w_in, o_c_w,
              o_c_b, o_c_scale, o_w_out):
    for layer in range(DEPTH):
        i = layer // 2
        if layer % 2 == 0:
            h = _rmsnorm(x, e_norm_pre[i])
            y = _even_mixer(h, e_w_in[i], e_a_conv[i], e_b_conv[i], e_b_conv_bias[i],
                            e_b_ln_g[i], e_b_ln_b[i], e_w_out[i])
            x = x + _rmsnorm(y, e_norm_post[i])
        else:
            h = _rmsnorm(x, o_norm_pre[i])
            y = _odd_mixer(h, o_w_in[i], o_c_w[i], o_c_b[i], o_c_scale[i], o_w_out[i])
            x = x + _rmsnorm(y, o_norm_post[i])
    return x


import jax as _jax
import jax.numpy as _jnp

TWIN_FORMAT = 'train_step'
FWD_PARAMS = ['x', 'e_norm_pre', 'e_norm_post', 'e_w_in', 'e_a_conv', 'e_b_conv', 'e_b_conv_bias', 'e_b_ln_g', 'e_b_ln_b', 'e_w_out', 'o_norm_pre', 'o_norm_post', 'o_w_in', 'o_c_w', 'o_c_b', 'o_c_scale', 'o_w_out']
TWIN_WEIGHTS = ['e_norm_pre', 'e_norm_post', 'e_w_in', 'e_a_conv', 'e_b_conv', 'e_b_conv_bias', 'e_b_ln_g', 'e_b_ln_b', 'e_w_out', 'o_norm_pre', 'o_norm_post', 'o_w_in', 'o_c_w', 'o_c_b', 'o_c_scale', 'o_w_out']
TWIN_DIFF_INPUT = 'x'
TWIN_INPUTS = ['x', 'e_norm_pre', 'e_norm_post', 'e_w_in', 'e_a_conv', 'e_b_conv', 'e_b_conv_bias', 'e_b_ln_g', 'e_b_ln_b', 'e_w_out', 'o_norm_pre', 'o_norm_post', 'o_w_in', 'o_c_w', 'o_c_b', 'o_c_scale', 'o_w_out', 'loss_target', 'm_e_norm_pre', 'm_e_norm_post', 'm_e_w_in', 'm_e_a_conv', 'm_e_b_conv', 'm_e_b_conv_bias', 'm_e_b_ln_g', 'm_e_b_ln_b', 'm_e_w_out', 'm_o_norm_pre', 'm_o_norm_post', 'm_o_w_in', 'm_o_c_w', 'm_o_c_b', 'm_o_c_scale', 'm_o_w_out', 'v_e_norm_pre', 'v_e_norm_post', 'v_e_w_in', 'v_e_a_conv', 'v_e_b_conv', 'v_e_b_conv_bias', 'v_e_b_ln_g', 'v_e_b_ln_b', 'v_e_w_out', 'v_o_norm_pre', 'v_o_norm_post', 'v_o_w_in', 'v_o_c_w', 'v_o_c_b', 'v_o_c_scale', 'v_o_w_out']
TWIN_OUTPUTS = ['loss', 'grad_x', 'grad_e_norm_pre', 'grad_e_norm_post', 'grad_e_w_in', 'grad_e_a_conv', 'grad_e_b_conv', 'grad_e_b_conv_bias', 'grad_e_b_ln_g', 'grad_e_b_ln_b', 'grad_e_w_out', 'grad_o_norm_pre', 'grad_o_norm_post', 'grad_o_w_in', 'grad_o_c_w', 'grad_o_c_b', 'grad_o_c_scale', 'grad_o_w_out', 'delta_e_norm_pre', 'delta_e_norm_post', 'delta_e_w_in', 'delta_e_a_conv', 'delta_e_b_conv', 'delta_e_b_conv_bias', 'delta_e_b_ln_g', 'delta_e_b_ln_b', 'delta_e_w_out', 'delta_o_norm_pre', 'delta_o_norm_post', 'delta_o_w_in', 'delta_o_c_w', 'delta_o_c_b', 'delta_o_c_scale', 'delta_o_w_out', 'new_m_e_norm_pre', 'new_m_e_norm_post', 'new_m_e_w_in', 'new_m_e_a_conv', 'new_m_e_b_conv', 'new_m_e_b_conv_bias', 'new_m_e_b_ln_g', 'new_m_e_b_ln_b', 'new_m_e_w_out', 'new_m_o_norm_pre', 'new_m_o_norm_post', 'new_m_o_w_in', 'new_m_o_c_w', 'new_m_o_c_b', 'new_m_o_c_scale', 'new_m_o_w_out', 'new_v_e_norm_pre', 'new_v_e_norm_post', 'new_v_e_w_in', 'new_v_e_a_conv', 'new_v_e_b_conv', 'new_v_e_b_conv_bias', 'new_v_e_b_ln_g', 'new_v_e_b_ln_b', 'new_v_e_w_out', 'new_v_o_norm_pre', 'new_v_o_norm_post', 'new_v_o_w_in', 'new_v_o_c_w', 'new_v_o_c_b', 'new_v_o_c_scale', 'new_v_o_w_out']
TWIN_LEAF_KINDS = {'loss': 'loss', 'grad_x': 'grad_x', 'grad_e_norm_pre': 'grad_w', 'grad_e_norm_post': 'grad_w', 'grad_e_w_in': 'grad_w', 'grad_e_a_conv': 'grad_w', 'grad_e_b_conv': 'grad_w', 'grad_e_b_conv_bias': 'grad_w', 'grad_e_b_ln_g': 'grad_w', 'grad_e_b_ln_b': 'grad_w', 'grad_e_w_out': 'grad_w', 'grad_o_norm_pre': 'grad_w', 'grad_o_norm_post': 'grad_w', 'grad_o_w_in': 'grad_w', 'grad_o_c_w': 'grad_w', 'grad_o_c_b': 'grad_w', 'grad_o_c_scale': 'grad_w', 'grad_o_w_out': 'grad_w', 'delta_e_norm_pre': 'delta_w', 'delta_e_norm_post': 'delta_w', 'delta_e_w_in': 'delta_w', 'delta_e_a_conv': 'delta_w', 'delta_e_b_conv': 'delta_w', 'delta_e_b_conv_bias': 'delta_w', 'delta_e_b_ln_g': 'delta_w', 'delta_e_b_ln_b': 'delta_w', 'delta_e_w_out': 'delta_w', 'delta_o_norm_pre': 'delta_w', 'delta_o_norm_post': 'delta_w', 'delta_o_w_in': 'delta_w', 'delta_o_c_w': 'delta_w', 'delta_o_c_b': 'delta_w', 'delta_o_c_scale': 'delta_w', 'delta_o_w_out': 'delta_w', 'new_m_e_norm_pre': 'new_m', 'new_m_e_norm_post': 'new_m', 'new_m_e_w_in': 'new_m', 'new_m_e_a_conv': 'new_m', 'new_m_e_b_conv': 'new_m', 'new_m_e_b_conv_bias': 'new_m', 'new_m_e_b_ln_g': 'new_m', 'new_m_e_b_ln_b': 'new_m', 'new_m_e_w_out': 'new_m', 'new_m_o_norm_pre': 'new_m', 'new_m_o_norm_post': 'new_m', 'new_m_o_w_in': 'new_m', 'new_m_o_c_w': 'new_m', 'new_m_o_c_b': 'new_m', 'new_m_o_c_scale': 'new_m', 'new_m_o_w_out': 'new_m', 'new_v_e_norm_pre': 'new_v', 'new_v_e_norm_post': 'new_v', 'new_v_e_w_in': 'new_v', 'new_v_e_a_conv': 'new_v', 'new_v_e_b_conv': 'new_v', 'new_v_e_b_conv_bias': 'new_v', 'new_v_e_b_ln_g': 'new_v', 'new_v_e_b_ln_b': 'new_v', 'new_v_e_w_out': 'new_v', 'new_v_o_norm_pre': 'new_v', 'new_v_o_norm_post': 'new_v', 'new_v_o_w_in': 'new_v', 'new_v_o_c_w': 'new_v', 'new_v_o_c_b': 'new_v', 'new_v_o_c_scale': 'new_v', 'new_v_o_w_out': 'new_v'}


def _forward(args):
    return _fwd_reference(*[args[k] for k in FWD_PARAMS])


def _output_shape():
    def fwd():
        inp = _fwd_setup_inputs(0)
        return _fwd_reference(*[inp[k] for k in FWD_PARAMS])
    out = _jax.eval_shape(fwd)
    return out.shape, out.dtype

N_MICROBATCH = 1
ADAM_LR = 0.001
ADAM_B1 = 0.9
ADAM_B2 = 0.999
ADAM_EPS = 1e-08
ADAM_WD = 0.01
ADAM_STEP = 10
PER_EXAMPLE_BATCH_AXIS = {'x': 0, 'loss_target': 0}
SHARED_INPUTS = []
_WEIGHT_DTYPES = {'e_norm_pre': _jnp.float32, 'e_norm_post': _jnp.float32, 'e_w_in': _jnp.float32, 'e_a_conv': _jnp.float32, 'e_b_conv': _jnp.float32, 'e_b_conv_bias': _jnp.float32, 'e_b_ln_g': _jnp.float32, 'e_b_ln_b': _jnp.float32, 'e_w_out': _jnp.float32, 'o_norm_pre': _jnp.float32, 'o_norm_post': _jnp.float32, 'o_w_in': _jnp.float32, 'o_c_w': _jnp.float32, 'o_c_b': _jnp.float32, 'o_c_scale': _jnp.float32, 'o_w_out': _jnp.float32}
MOMENT_SCALE = {'e_norm_pre': 4.215920e-01, 'e_norm_post': 1.604113e+01, 'e_w_in': 2.255502e-01, 'e_a_conv': 2.713630e-01, 'e_b_conv': 1.732893e-01, 'e_b_conv_bias': 4.034038e-01, 'e_b_ln_g': 2.499983e-01, 'e_b_ln_b': 2.357291e-01, 'e_w_out': 2.480195e-01, 'o_norm_pre': 2.544326e-01, 'o_norm_post': 1.601853e+01, 'o_w_in': 1.822913e-01, 'o_c_w': 2.038958e-01, 'o_c_b': 5.552491e-01, 'o_c_scale': 2.010905e-01, 'o_w_out': 2.051954e-01}


def _to_microbatches(a, axis):
    t = _jnp.moveaxis(a, axis, 0)
    t = t.reshape((N_MICROBATCH, t.shape[0] // N_MICROBATCH) + t.shape[1:])
    return _jnp.moveaxis(t, 1, axis + 1)


def setup_inputs(seed: int = 0) -> dict:
    inp = _fwd_setup_inputs(seed)
    key = _jax.random.fold_in(_jax.random.key(seed), 7919)
    shape, _ = _output_shape()
    out = dict(inp)
    out["loss_target"] = _jax.random.normal(_jax.random.fold_in(key, 0), shape, _jnp.float32)
    for i, name in enumerate(TWIN_WEIGHTS):
        w = inp[name].astype(_jnp.float32)
        if MOMENT_SCALE is None:
            s = _jnp.sqrt(_jnp.mean(_jnp.square(w)) + 1e-30)
        else:
            s = MOMENT_SCALE[name]
        km, kv = _jax.random.split(_jax.random.fold_in(key, i + 1))
        out[name] = w
        out["m_" + name] = s * _jax.random.normal(km, w.shape, _jnp.float32)
        out["v_" + name] = (s * s) * _jax.random.uniform(kv, w.shape, _jnp.float32, 0.5, 1.5)
    if N_MICROBATCH > 1:
        for name, axis in PER_EXAMPLE_BATCH_AXIS.items():
            out[name] = _to_microbatches(out[name], axis)
    return {'x': out['x'], 'e_norm_pre': out['e_norm_pre'], 'e_norm_post': out['e_norm_post'], 'e_w_in': out['e_w_in'], 'e_a_conv': out['e_a_conv'], 'e_b_conv': out['e_b_conv'], 'e_b_conv_bias': out['e_b_conv_bias'], 'e_b_ln_g': out['e_b_ln_g'], 'e_b_ln_b': out['e_b_ln_b'], 'e_w_out': out['e_w_out'], 'o_norm_pre': out['o_norm_pre'], 'o_norm_post': out['o_norm_post'], 'o_w_in': out['o_w_in'], 'o_c_w': out['o_c_w'], 'o_c_b': out['o_c_b'], 'o_c_scale': out['o_c_scale'], 'o_w_out': out['o_w_out'], 'loss_target': out['loss_target'], 'm_e_norm_pre': out['m_e_norm_pre'], 'm_e_norm_post': out['m_e_norm_post'], 'm_e_w_in': out['m_e_w_in'], 'm_e_a_conv': out['m_e_a_conv'], 'm_e_b_conv': out['m_e_b_conv'], 'm_e_b_conv_bias': out['m_e_b_conv_bias'], 'm_e_b_ln_g': out['m_e_b_ln_g'], 'm_e_b_ln_b': out['m_e_b_ln_b'], 'm_e_w_out': out['m_e_w_out'], 'm_o_norm_pre': out['m_o_norm_pre'], 'm_o_norm_post': out['m_o_norm_post'], 'm_o_w_in': out['m_o_w_in'], 'm_o_c_w': out['m_o_c_w'], 'm_o_c_b': out['m_o_c_b'], 'm_o_c_scale': out['m_o_c_scale'], 'm_o_w_out': out['m_o_w_out'], 'v_e_norm_pre': out['v_e_norm_pre'], 'v_e_norm_post': out['v_e_norm_post'], 'v_e_w_in': out['v_e_w_in'], 'v_e_a_conv': out['v_e_a_conv'], 'v_e_b_conv': out['v_e_b_conv'], 'v_e_b_conv_bias': out['v_e_b_conv_bias'], 'v_e_b_ln_g': out['v_e_b_ln_g'], 'v_e_b_ln_b': out['v_e_b_ln_b'], 'v_e_w_out': out['v_e_w_out'], 'v_o_norm_pre': out['v_o_norm_pre'], 'v_o_norm_post': out['v_o_norm_post'], 'v_o_w_in': out['v_o_w_in'], 'v_o_c_w': out['v_o_c_w'], 'v_o_c_b': out['v_o_c_b'], 'v_o_c_scale': out['v_o_c_scale'], 'v_o_w_out': out['v_o_w_out']}


def _loss(weights, diff, rest, loss_target):
    with _jax.named_scope("forward"):
        args = {**rest, TWIN_DIFF_INPUT: diff, **{k: w.astype(_WEIGHT_DTYPES[k]) for k, w in weights.items()}}
        y = _forward(args)
    with _jax.named_scope("loss_head"):
        err = _jnp.square(y.astype(_jnp.float32) - loss_target)
        return 0.5 * _jnp.sum(_jnp.mean(err, axis=-1)) if err.ndim else 0.5 * err


def _adamw(w, g, m, v):
    m = ADAM_B1 * m + (1.0 - ADAM_B1) * g
    v = ADAM_B2 * v + (1.0 - ADAM_B2) * _jnp.square(g)
    m_hat = m / (1.0 - ADAM_B1 ** ADAM_STEP)
    v_hat = v / (1.0 - ADAM_B2 ** ADAM_STEP)
    delta = -ADAM_LR * (m_hat / (_jnp.sqrt(v_hat) + ADAM_EPS) + ADAM_WD * w)
    return delta, m, v


def reference(x, e_norm_pre, e_norm_post, e_w_in, e_a_conv, e_b_conv, e_b_conv_bias, e_b_ln_g, e_b_ln_b, e_w_out, o_norm_pre, o_norm_post, o_w_in, o_c_w, o_c_b, o_c_scale, o_w_out, loss_target, m_e_norm_pre, m_e_norm_post, m_e_w_in, m_e_a_conv, m_e_b_conv, m_e_b_conv_bias, m_e_b_ln_g, m_e_b_ln_b, m_e_w_out, m_o_norm_pre, m_o_norm_post, m_o_w_in, m_o_c_w, m_o_c_b, m_o_c_scale, m_o_w_out, v_e_norm_pre, v_e_norm_post, v_e_w_in, v_e_a_conv, v_e_b_conv, v_e_b_conv_bias, v_e_b_ln_g, v_e_b_ln_b, v_e_w_out, v_o_norm_pre, v_o_norm_post, v_o_w_in, v_o_c_w, v_o_c_b, v_o_c_scale, v_o_w_out):
    given = dict(x=x, e_norm_pre=e_norm_pre, e_norm_post=e_norm_post, e_w_in=e_w_in, e_a_conv=e_a_conv, e_b_conv=e_b_conv, e_b_conv_bias=e_b_conv_bias, e_b_ln_g=e_b_ln_g, e_b_ln_b=e_b_ln_b, e_w_out=e_w_out, o_norm_pre=o_norm_pre, o_norm_post=o_norm_post, o_w_in=o_w_in, o_c_w=o_c_w, o_c_b=o_c_b, o_c_scale=o_c_scale, o_w_out=o_w_out, loss_target=loss_target, m_e_norm_pre=m_e_norm_pre, m_e_norm_post=m_e_norm_post, m_e_w_in=m_e_w_in, m_e_a_conv=m_e_a_conv, m_e_b_conv=m_e_b_conv, m_e_b_conv_bias=m_e_b_conv_bias, m_e_b_ln_g=m_e_b_ln_g, m_e_b_ln_b=m_e_b_ln_b, m_e_w_out=m_e_w_out, m_o_norm_pre=m_o_norm_pre, m_o_norm_post=m_o_norm_post, m_o_w_in=m_o_w_in, m_o_c_w=m_o_c_w, m_o_c_b=m_o_c_b, m_o_c_scale=m_o_c_scale, m_o_w_out=m_o_w_out, v_e_norm_pre=v_e_norm_pre, v_e_norm_post=v_e_norm_post, v_e_w_in=v_e_w_in, v_e_a_conv=v_e_a_conv, v_e_b_conv=v_e_b_conv, v_e_b_conv_bias=v_e_b_conv_bias, v_e_b_ln_g=v_e_b_ln_g, v_e_b_ln_b=v_e_b_ln_b, v_e_w_out=v_e_w_out, v_o_norm_pre=v_o_norm_pre, v_o_norm_post=v_o_norm_post, v_o_w_in=v_o_w_in, v_o_c_w=v_o_c_w, v_o_c_b=v_o_c_b, v_o_c_scale=v_o_c_scale, v_o_w_out=v_o_w_out)
    weights = {n: given[n] for n in TWIN_WEIGHTS}
    shared = {n: given[n] for n in SHARED_INPUTS}
    per_example = {n: given[n] for n in ['x']}
    grad_fn = _jax.value_and_grad(_loss, argnums=(0, 1))

    def one_microbatch(ex, loss_target):
        ex = dict(ex)
        diff = ex.pop(TWIN_DIFF_INPUT)
        return grad_fn(weights, diff, {**shared, **ex}, loss_target)

    if N_MICROBATCH == 1:
        loss, (grad_w, grad_x) = one_microbatch(per_example, given["loss_target"])
    else:
        def body(carry, xs):
            loss_sum, grad_sum = carry
            l_k, (gw_k, gx_k) = one_microbatch(xs[0], xs[1])
            with _jax.named_scope("update"):
                return (loss_sum + l_k, _jax.tree.map(_jnp.add, grad_sum, gw_k)), gx_k

        init = (_jnp.zeros((), _jnp.float32), _jax.tree.map(_jnp.zeros_like, weights))
        (loss, grad_w), grad_x = _jax.lax.scan(body, init, (per_example, given["loss_target"]))
    with _jax.named_scope("update"):
        delta_w, new_m, new_v = {}, {}, {}
        for n in TWIN_WEIGHTS:
            delta_w[n], new_m[n], new_v[n] = _adamw(weights[n], grad_w[n], given["m_" + n], given["v_" + n])
    return (loss, grad_x, *[grad_w[n] for n in TWIN_WEIGHTS], *[delta_w[n] for n in TWIN_WEIGHTS],
            *[new_m[n] for n in TWIN_WEIGHTS], *[new_v[n] for n in TWIN_WEIGHTS])
```

```python
import jax
import jax.numpy as jnp
from jax import lax
from jax.experimental import pallas as pl
from jax.experimental.pallas import tpu as pltpu

F32 = jnp.float32
BF16 = jnp.bfloat16
EPS = 1e-6
MESH = pl.DeviceIdType.MESH
ANY = pl.BlockSpec(memory_space=pl.ANY)

N_DEV = 8
HALO = 32
PHALO = 16
CONV_A = 3
CONV_B = 31
POOL_WINDOWS = (2, 4, 8, 16)
LANES = 128
MIB = 1024 * 1024

ADAM_LR = 0.001
ADAM_B1 = 0.9
ADAM_B2 = 0.999
ADAM_EPS = 1e-08
ADAM_WD = 0.01
ADAM_STEP = 10

TM_MM = 512
TM_NT = 1024
TM_MIX = 256


def _sds(shape, dtype):
    return jax.ShapeDtypeStruct(tuple(shape), dtype)


def _params(sem, vmem_mib):
    return pltpu.CompilerParams(dimension_semantics=sem, vmem_limit_bytes=vmem_mib * MIB)


def _const(shape, single=False):
    n = len(shape)
    if single:
        return pl.BlockSpec(shape, lambda *_: (0,) * n, pipeline_mode=pl.Buffered(1))
    return pl.BlockSpec(shape, lambda *_: (0,) * n)


def _sig(v):
    return jax.nn.sigmoid(v)


def _dsilu(v, s):
    return s * (1.0 + v * (1.0 - s))


def _rms(v):
    return lax.rsqrt(jnp.mean(v * v, axis=-1, keepdims=True) + EPS)


def _norm_bwd(dn, n, r):
    return r * (dn - n * jnp.mean(dn * n, axis=-1, keepdims=True))


def _colsum(v):
    return jnp.sum(v, axis=0, keepdims=True)


def _norm_matmul(x, g, w, *, tm, tn, name):
    S, D = x.shape
    N = w.shape[1]

    def body(x_ref, g_ref, w_ref, p_ref, h_ref):
        @pl.when(pl.program_id(1) == 0)
        def _():
            xx = x_ref[...]
            h_ref[...] = ((xx * _rms(xx)) * g_ref[...]).astype(BF16)

        p_ref[...] = jnp.dot(h_ref[...], w_ref[...], preferred_element_type=F32).astype(BF16)

    return pl.pallas_call(
        body, grid=(S // tm, N // tn),
        in_specs=[pl.BlockSpec((tm, D), lambda i, j: (i, 0)), _const((1, D)),
                  pl.BlockSpec((D, tn), lambda i, j: (0, j))],
        out_specs=[pl.BlockSpec((tm, tn), lambda i, j: (i, j)), pl.BlockSpec((tm, D), lambda i, j: (i, 0))],
        out_shape=[_sds((S, N), BF16), _sds((S, D), BF16)],
        name=name, compiler_params=_params(("arbitrary", "arbitrary"), 48),
    )(x, g, w)


def _out_norm_res(u, w, x, g, *, tm, name):
    S, K = u.shape
    D = w.shape[1]

    def body(u_ref, w_ref, x_ref, g_ref, x1_ref, y_ref):
        y = jnp.dot(u_ref[...], w_ref[...], preferred_element_type=F32)
        y_ref[...] = y.astype(BF16)
        x1_ref[...] = x_ref[...] + (y * _rms(y)) * g_ref[...]

    return pl.pallas_call(
        body, grid=(S // tm,),
        in_specs=[pl.BlockSpec((tm, K), lambda i: (i, 0)), _const((K, D), single=True),
                  pl.BlockSpec((tm, D), lambda i: (i, 0)), _const((1, D))],
        out_specs=[pl.BlockSpec((tm, D), lambda i: (i, 0)), pl.BlockSpec((tm, D), lambda i: (i, 0))],
        out_shape=[_sds((S, D), F32), _sds((S, D), BF16)],
        name=name, compiler_params=_params(("arbitrary",), 48),
    )(u, w, x, g)


def _out_loss(yy, w, x1, g, tgt, *, tm, name):
    S, K = yy.shape
    D = w.shape[1]

    def body(yy_ref, w_ref, x1_ref, g_ref, t_ref, dout_ref, dx2_ref, dyy_ref, lcol_ref, dg_ref):
        out = jnp.dot(yy_ref[...], w_ref[...], preferred_element_type=F32)
        r = _rms(out)
        n = out * r
        gg = g_ref[...]
        e = x1_ref[...] + n * gg - t_ref[...]
        dx2 = e * (1.0 / D)
        dx2_ref[...] = dx2
        dout = _norm_bwd(dx2 * gg, n, r).astype(BF16)
        dout_ref[...] = dout
        dyy_ref[...] = lax.dot_general(dout, w_ref[...], (((1,), (1,)), ((), ())),
                                       preferred_element_type=F32).astype(BF16)

        @pl.when(pl.program_id(0) == 0)
        def _():
            lcol_ref[...] = jnp.zeros_like(lcol_ref)
            dg_ref[...] = jnp.zeros_like(dg_ref)

        lcol_ref[...] += _colsum(e * e)
        dg_ref[...] += _colsum(dx2 * n)

    return pl.pallas_call(
        body, grid=(S // tm,),
        in_specs=[pl.BlockSpec((tm, K), lambda i: (i, 0)), _const((K, D), single=True),
                  pl.BlockSpec((tm, D), lambda i: (i, 0)), _const((1, D)),
                  pl.BlockSpec((tm, D), lambda i: (i, 0))],
        out_specs=[pl.BlockSpec((tm, D), lambda i: (i, 0)), pl.BlockSpec((tm, D), lambda i: (i, 0)),
                   pl.BlockSpec((tm, K), lambda i: (i, 0)), _const((1, D)), _const((1, D))],
        out_shape=[_sds((S, D), BF16), _sds((S, D), F32), _sds((S, K), BF16), _sds((1, D), F32), _sds((1, D), F32)],
        name=name, compiler_params=_params(("arbitrary",), 52),
    )(yy, w, x1, g, tgt)


def _mm_nt(a, w, *, tm, tk, name):
    S, N = a.shape
    D = w.shape[0]
    n_k = N // tk

    def body(a_ref, w_ref, o_ref, acc_ref):
        k = pl.program_id(1)
        part = lax.dot_general(a_ref[...], w_ref[...], (((1,), (1,)), ((), ())), preferred_element_type=F32)

        @pl.when(k == 0)
        def _():
            acc_ref[...] = part

        @pl.when(k > 0)
        def _():
            acc_ref[...] += part

        @pl.when(k == n_k - 1)
        def _():
            o_ref[...] = acc_ref[...].astype(BF16)

    return pl.pallas_call(
        body, grid=(S // tm, n_k),
        in_specs=[pl.BlockSpec((tm, tk), lambda i, k: (i, k)), pl.BlockSpec((D, tk), lambda i, k: (0, k))],
        out_specs=pl.BlockSpec((tm, D), lambda i, k: (i, 0)),
        out_shape=_sds((S, D), BF16),
        scratch_shapes=[pltpu.VMEM((tm, D), F32)],
        name=name, compiler_params=_params(("arbitrary", "arbitrary"), 48),
    )(a, w)


def _mm_tn(a, b, *, ts, tn, name):
    S, M = a.shape
    N = b.shape[1]
    n_s = S // ts

    def body(a_ref, b_ref, o_ref, acc_ref):
        s = pl.program_id(1)
        part = lax.dot_general(a_ref[...], b_ref[...], (((0,), (0,)), ((), ())), preferred_element_type=F32)

        @pl.when(s == 0)
        def _():
            acc_ref[...] = part

        @pl.when(s > 0)
        def _():
            acc_ref[...] += part

        @pl.when(s == n_s - 1)
        def _():
            o_ref[...] = acc_ref[...].astype(BF16)

    return pl.pallas_call(
        body, grid=(N // tn, n_s),
        in_specs=[pl.BlockSpec((ts, M), lambda j, s: (s, 0)), pl.BlockSpec((ts, tn), lambda j, s: (s, j))],
        out_specs=pl.BlockSpec((M, tn), lambda j, s: (0, j)),
        out_shape=_sds((M, N), BF16),
        scratch_shapes=[pltpu.VMEM((M, tn), F32)],
        name=name, compiler_params=_params(("arbitrary", "arbitrary"), 48),
    )(a, b)


def _pre_bwd_o(dh, x1, dx2, y0, g_pre, g_post, *, tm, name):
    S, D = x1.shape

    def body(dh_ref, x1_ref, dx2_ref, y0_ref, gpre_ref, gpost_ref, dx1_ref, dy0_ref, dgpre_ref, dgpost_ref):
        @pl.when(pl.program_id(0) == 0)
        def _():
            dgpre_ref[...] = jnp.zeros_like(dgpre_ref)
            dgpost_ref[...] = jnp.zeros_like(dgpost_ref)

        dh = dh_ref[...].astype(F32)
        x1 = x1_ref[...]
        r2 = _rms(x1)
        xn = x1 * r2
        dgpre_ref[...] += _colsum(dh * xn)
        dx1 = dx2_ref[...] + _norm_bwd(dh * gpre_ref[...], xn, r2)
        dx1_ref[...] = dx1
        y = y0_ref[...].astype(F32)
        r1 = _rms(y)
        n1 = y * r1
        dgpost_ref[...] += _colsum(dx1 * n1)
        dy0_ref[...] = _norm_bwd(dx1 * gpost_ref[...], n1, r1).astype(BF16)

    row = pl.BlockSpec((tm, D), lambda i: (i, 0))
    return pl.pallas_call(
        body, grid=(S // tm,),
        in_specs=[row, row, row, row, _const((1, D)), _const((1, D))],
        out_specs=[row, row, _const((1, D)), _const((1, D))],
        out_shape=[_sds((S, D), F32), _sds((S, D), BF16), _sds((1, D), F32), _sds((1, D), F32)],
        name=name, compiler_params=_params(("arbitrary",), 48),
    )(dh, x1, dx2, y0, g_pre, g_post)


def _pre_bwd_e(dh, x, dx1, g_pre, *, tm, name):
    S, D = x.shape

    def body(dh_ref, x_ref, dx1_ref, gpre_ref, gx_ref, dgpre_ref):
        @pl.when(pl.program_id(0) == 0)
        def _():
            dgpre_ref[...] = jnp.zeros_like(dgpre_ref)

        dh = dh_ref[...].astype(F32)
        xx = x_ref[...]
        r0 = _rms(xx)
        xn = xx * r0
        dgpre_ref[...] += _colsum(dh * xn)
        gx_ref[...] = dx1_ref[...] + _norm_bwd(dh * gpre_ref[...], xn, r0)

    row = pl.BlockSpec((tm, D), lambda i: (i, 0))
    return pl.pallas_call(
        body, grid=(S // tm,),
        in_specs=[row, row, row, _const((1, D))],
        out_specs=[row, _const((1, D))],
        out_shape=[_sds((S, D), F32), _sds((1, D), F32)],
        name=name, compiler_params=_params(("arbitrary",), 48),
    )(dh, x, dx1, g_pre)


def _taps(ext_ref, w_ref, n_taps, base, cs, tm):
    acc = ext_ref[pl.ds(base, tm), cs] * w_ref[0:1, cs]
    for k in range(1, n_taps):
        acc = acc + ext_ref[pl.ds(base + k, tm), cs] * w_ref[k:k + 1, cs]
    return acc


def _taps_rev(ext_ref, w_ref, n_taps, cs, tm):
    acc = ext_ref[pl.ds(n_taps - 1, tm), cs] * w_ref[0:1, cs]
    for k in range(1, n_taps):
        acc = acc + ext_ref[pl.ds(n_taps - 1 - k, tm), cs] * w_ref[k:k + 1, cs]
    return acc


def _e_mix_fwd(p, wa, wb, bias, ln_g, ln_b, *, tm, name):
    S = p.shape[0]
    W = p.shape[1] // 7
    nb = tm // HALO
    chunks = [slice(c * LANES, (c + 1) * LANES) for c in range(W // LANES)]

    def body(p_ref, hax_ref, hac_ref, hbv_ref, hbg_ref, wa_ref, wb_ref, bias_ref, lg_ref, lb_ref,
             u_ref, cb_ref, ext_ref):
        keep = (pl.program_id(0) > 0).astype(F32)
        col = lambda j, cs: p_ref[:, j * W + cs.start:j * W + cs.stop].astype(F32)

        ext_ref[0:HALO, :] = hax_ref[...].astype(F32) * hac_ref[...].astype(F32) * keep
        ext_ref[HALO:, :] = p_ref[:, 2 * W:3 * W].astype(F32) * p_ref[:, 0:W].astype(F32)
        for cs in chunks:
            conv = _taps(ext_ref, wa_ref, CONV_A, HALO - (CONV_A - 1), cs, tm)
            az = col(3, cs)
            u_ref[:, cs] = (col(1, cs) * conv * (az * _sig(az))).astype(BF16)

        ext_ref[0:HALO, :] = hbv_ref[...].astype(F32) * _sig(hbg_ref[...].astype(F32)) * keep
        ext_ref[HALO:, :] = p_ref[:, 4 * W:5 * W].astype(F32) * _sig(p_ref[:, 5 * W:6 * W].astype(F32))
        s1 = jnp.zeros((tm, LANES), F32)
        for cs in chunks:
            cb = _taps(ext_ref, wb_ref, CONV_B, HALO - (CONV_B - 1), cs, tm) + bias_ref[:, cs]
            cb_ref[:, cs] = cb
            s1 = s1 + cb
        mu = jnp.sum(s1, axis=-1, keepdims=True) * (1.0 / W)
        s2 = jnp.zeros((tm, LANES), F32)
        for cs in chunks:
            xc = cb_ref[:, cs] - mu
            s2 = s2 + xc * xc
        rs = lax.rsqrt(jnp.sum(s2, axis=-1, keepdims=True) * (1.0 / W) + EPS)
        for cs in chunks:
            lb = (cb_ref[:, cs] - mu) * rs * lg_ref[:, cs] + lb_ref[:, cs]
            bz = col(6, cs)
            u_ref[:, W + cs.start:W + cs.stop] = (lb * _sig(lb) * (bz * _sig(bz))).astype(BF16)

    prev = lambda j: pl.BlockSpec((HALO, W), lambda i: (jnp.maximum(i * nb - 1, 0), j))
    return pl.pallas_call(
        body, grid=(S // tm,),
        in_specs=[pl.BlockSpec((tm, 7 * W), lambda i: (i, 0)), prev(0), prev(2), prev(4), prev(5),
                  _const((CONV_A, W)), _const((CONV_B, W)), _const((1, W)), _const((1, W)), _const((1, W))],
        out_specs=[pl.BlockSpec((tm, 2 * W), lambda i: (i, 0)), pl.BlockSpec((tm, W), lambda i: (i, 0))],
        out_shape=[_sds((S, 2 * W), BF16), _sds((S, W), F32)],
        scratch_shapes=[pltpu.VMEM((HALO + tm, W), F32)],
        name=name, compiler_params=_params(("arbitrary",), 48),
    )(p, p, p, p, p, wa, wb, bias, ln_g, ln_b)


def _e_mix_bwd(du, p, cb, wa, wb, ln_g, ln_b, *, tm, name):
    S = p.shape[0]
    W = p.shape[1] // 7
    nb = tm // HALO
    n_t = S // tm
    last_blk = S // HALO - 1
    chunks = [slice(c * LANES, (c + 1) * LANES) for c in range(W // LANES)]

    def body(du_ref, duf_ref, p_ref, fab_ref, faz_ref, fbz_ref, hax_ref, hac_ref, hbv_ref, hbg_ref,
             cb_ref, cbf_ref, wa_ref, wb_ref, lg_ref, lb_ref,
             dp_ref, dwa_ref, dwb_ref, dbias_ref, dlg_ref, dlb_ref, extd_ref, extg_ref):
        i = pl.program_id(0)
        keep_prev = (i > 0).astype(F32)
        keep_next = (i < n_t - 1).astype(F32)
        col = lambda j, cs: p_ref[:, j * W + cs.start:j * W + cs.stop].astype(F32)

        @pl.when(i == 0)
        def _():
            dwa_ref[...] = jnp.zeros_like(dwa_ref)
            dwb_ref[...] = jnp.zeros_like(dwb_ref)
            dbias_ref[...] = jnp.zeros_like(dbias_ref)
            dlg_ref[...] = jnp.zeros_like(dlg_ref)
            dlb_ref[...] = jnp.zeros_like(dlb_ref)

        def dcb_rows(rows, cb_rows_ref, dub, bz_of, dst0, scale, main):
            cbv = cb_rows_ref[...]
            mu = jnp.mean(cbv, axis=-1, keepdims=True)
            xc = cbv - mu
            rs = lax.rsqrt(jnp.mean(xc * xc, axis=-1, keepdims=True) + EPS)
            m1 = jnp.zeros((rows, LANES), F32)
            m2 = jnp.zeros((rows, LANES), F32)
            for cs in chunks:
                nbv = (cb_rows_ref[:, cs] - mu) * rs
                lb = nbv * lg_ref[:, cs] + lb_ref[:, cs]
                sl = _sig(lb)
                bz = bz_of(cs)
                sz = _sig(bz)
                dub_c = dub(cs)
                dlb = dub_c * (bz * sz) * _dsilu(lb, sl)
                if main:
                    dlg_ref[:, cs] += _colsum(dlb * nbv)
                    dlb_ref[:, cs] += _colsum(dlb)
                    dp_ref[:, 6 * W + cs.start:6 * W + cs.stop] = (dub_c * (lb * sl) * _dsilu(bz, sz)).astype(BF16)
                dnb = dlb * lg_ref[:, cs]
                extd_ref[dst0:dst0 + rows, cs] = dnb
                m1 = m1 + dnb
                m2 = m2 + dnb * nbv
            m1 = jnp.sum(m1, axis=-1, keepdims=True) * (1.0 / W)
            m2 = jnp.sum(m2, axis=-1, keepdims=True) * (1.0 / W)
            for cs in chunks:
                nbv = (cb_rows_ref[:, cs] - mu) * rs
                dcb = rs * (extd_ref[dst0:dst0 + rows, cs] - m1 - nbv * m2) * scale
                extd_ref[dst0:dst0 + rows, cs] = dcb
                if main:
                    dbias_ref[:, cs] += _colsum(dcb)

        dcb_rows(tm, cb_ref, lambda cs: du_ref[:, W + cs.start:W + cs.stop].astype(F32),
                 lambda cs: col(6, cs), 0, 1.0, True)
        dcb_rows(HALO, cbf_ref, lambda cs: duf_ref[:, W + cs.start:W + cs.stop].astype(F32),
                 lambda cs: fbz_ref[:, cs].astype(F32), tm, keep_next, False)

        extg_ref[0:HALO, :] = hbv_ref[...].astype(F32) * _sig(hbg_ref[...].astype(F32)) * keep_prev
        extg_ref[HALO:, :] = p_ref[:, 4 * W:5 * W].astype(F32) * _sig(p_ref[:, 5 * W:6 * W].astype(F32))
        base_b = HALO - (CONV_B - 1)
        for cs in chunks:
            dgb = _taps_rev(extd_ref, wb_ref, CONV_B, cs, tm)
            bv = col(4, cs)
            sg = _sig(col(5, cs))
            dp_ref[:, 4 * W + cs.start:4 * W + cs.stop] = (dgb * sg).astype(BF16)
            dp_ref[:, 5 * W + cs.start:5 * W + cs.stop] = (dgb * bv * sg * (1.0 - sg)).astype(BF16)
            dcb = extd_ref[0:tm, cs]
            for k in range(CONV_B):
                dwb_ref[k:k + 1, cs] += _colsum(dcb * extg_ref[pl.ds(base_b + k, tm), cs])

        extg_ref[0:HALO, :] = hax_ref[...].astype(F32) * hac_ref[...].astype(F32) * keep_prev
        extg_ref[HALO:, :] = p_ref[:, 2 * W:3 * W].astype(F32) * p_ref[:, 0:W].astype(F32)
        base_a = HALO - (CONV_A - 1)
        for cs in chunks:
            conv = _taps(extg_ref, wa_ref, CONV_A, base_a, cs, tm)
            az = col(3, cs)
            sz = _sig(az)
            ab = col(1, cs)
            dua = du_ref[:, cs].astype(F32)
            dya = dua * (az * sz)
            dp_ref[:, W + cs.start:W + cs.stop] = (dya * conv).astype(BF16)
            dp_ref[:, 3 * W + cs.start:3 * W + cs.stop] = (dua * (ab * conv) * _dsilu(az, sz)).astype(BF16)
            extd_ref[0:tm, cs] = dya * ab
            azf = faz_ref[:, cs].astype(F32)
            extd_ref[tm:tm + HALO, cs] = (duf_ref[:, cs].astype(F32) * (azf * _sig(azf))
                                          * fab_ref[:, cs].astype(F32) * keep_next)
        for cs in chunks:
            dca = _taps_rev(extd_ref, wa_ref, CONV_A, cs, tm)
            dp_ref[:, cs] = (dca * col(2, cs)).astype(BF16)
            dp_ref[:, 2 * W + cs.start:2 * W + cs.stop] = (dca * col(0, cs)).astype(BF16)
            dconv = extd_ref[0:tm, cs]
            for k in range(CONV_A):
                dwa_ref[k:k + 1, cs] += _colsum(dconv * extg_ref[pl.ds(base_a + k, tm), cs])

    prev = lambda j: pl.BlockSpec((HALO, W), lambda i: (jnp.maximum(i * nb - 1, 0), j))
    nxt = lambda j, w: pl.BlockSpec((HALO, w), lambda i: (jnp.minimum((i + 1) * nb, last_blk), j))
    row = lambda w: pl.BlockSpec((tm, w), lambda i: (i, 0))
    return pl.pallas_call(
        body, grid=(n_t,),
        in_specs=[row(2 * W), nxt(0, 2 * W), row(7 * W), nxt(1, W), nxt(3, W), nxt(6, W),
                  prev(0), prev(2), prev(4), prev(5), row(W), nxt(0, W),
                  _const((CONV_A, W)), _const((CONV_B, W)), _const((1, W)), _const((1, W))],
        out_specs=[row(7 * W), _const((CONV_A, W)), _const((CONV_B, W)), _const((1, W)), _const((1, W)), _const((1, W))],
        out_shape=[_sds((S, 7 * W), BF16), _sds((CONV_A, W), F32), _sds((CONV_B, W), F32),
                   _sds((1, W), F32), _sds((1, W), F32), _sds((1, W), F32)],
        scratch_shapes=[pltpu.VMEM((tm + HALO, W), F32), pltpu.VMEM((HALO + tm, W), F32)],
        name=name, compiler_params=_params(("arbitrary",), 52),
    )(du, du, p, p, p, p, p, p, p, p, cb, cb, wa, wb, ln_g, ln_b)


def _counts(i, tm, rows, off, win):
    t = i * tm + off + lax.broadcasted_iota(jnp.int32, (rows, 1), 0)
    return jnp.minimum(t + 1, win).astype(F32)


def _o_mix_fwd(q, cw, cb, cscale, *, tm, name):
    S = q.shape[0]
    WC = q.shape[1] // 2
    NG = len(POOL_WINDOWS)
    G = WC // NG
    nb = tm // PHALO

    def body(v_ref, z_ref, hv_ref, cw_ref, cb_ref, sc_ref, yy_ref, pooled_ref, gg_ref, ext_ref):
        i = pl.program_id(0)
        keep = (i > 0).astype(F32)
        for g, win in enumerate(POOL_WINDOWS):
            cs = slice(g * G, (g + 1) * G)
            v = v_ref[:, cs].astype(F32)
            ext_ref[0:PHALO, :] = hv_ref[:, cs].astype(F32) * keep
            ext_ref[PHALO:, :] = v
            s = v
            for j in range(1, win):
                s = s + ext_ref[pl.ds(PHALO - j, tm), :]
            pooled = (s / _counts(i, tm, tm, 0, win) - v).astype(BF16)
            pooled_ref[:, cs] = pooled
            gg = jnp.dot(pooled, cw_ref[g], preferred_element_type=F32) + cb_ref[:, cs]
            gg_ref[:, cs] = gg.astype(BF16)
            z = z_ref[:, cs].astype(F32)
            yy_ref[:, cs] = (gg * sc_ref[:, cs] * (z * _sig(z))).astype(BF16)

    row = lambda j: pl.BlockSpec((tm, WC), lambda i: (i, j))
    out = pl.BlockSpec((tm, WC), lambda i: (i, 0))
    return pl.pallas_call(
        body, grid=(S // tm,),
        in_specs=[row(0), row(1), pl.BlockSpec((PHALO, WC), lambda i: (jnp.maximum(i * nb - 1, 0), 0)),
                  _const((NG, G, G)), _const((1, WC)), _const((1, WC))],
        out_specs=[out, out, out],
        out_shape=[_sds((S, WC), BF16)] * 3,
        scratch_shapes=[pltpu.VMEM((PHALO + tm, G), F32)],
        name=name, compiler_params=_params(("arbitrary",), 40),
    )(q, q, q, cw, cb, cscale)


def _o_mix_bwd(dyy, q, gg, pooled, cw, cscale, *, tm, name):
    S = q.shape[0]
    WC = q.shape[1] // 2
    NG = len(POOL_WINDOWS)
    G = WC // NG
    nb = tm // PHALO
    n_t = S // tm
    last_blk = S // PHALO - 1
    nt = (((1,), (1,)), ((), ()))
    tn = (((0,), (0,)), ((), ()))

    def body(dyy_ref, dyyf_ref, z_ref, zf_ref, gg_ref, pooled_ref, cw_ref, sc_ref,
             dq_ref, dcw_ref, dcb_ref, dsc_ref, ext_ref):
        i = pl.program_id(0)
        keep_next = (i < n_t - 1).astype(F32)

        @pl.when(i == 0)
        def _():
            dcw_ref[...] = jnp.zeros_like(dcw_ref)
            dcb_ref[...] = jnp.zeros_like(dcb_ref)
            dsc_ref[...] = jnp.zeros_like(dsc_ref)

        for g, win in enumerate(POOL_WINDOWS):
            cs = slice(g * G, (g + 1) * G)
            sc = sc_ref[:, cs]
            z = z_ref[:, cs].astype(F32)
            sz = _sig(z)
            dyy_c = dyy_ref[:, cs].astype(F32)
            ggv = gg_ref[:, cs].astype(F32)
            dyy0 = dyy_c * (z * sz)
            dq_ref[:, WC + cs.start:WC + cs.stop] = (dyy_c * (ggv * sc) * _dsilu(z, sz)).astype(BF16)
            dgg = dyy0 * sc
            dsc_ref[:, cs] += _colsum(dyy0 * ggv)
            dcb_ref[:, cs] += _colsum(dgg)
            dgg_b = dgg.astype(BF16)
            dcw_ref[g] += lax.dot_general(pooled_ref[:, cs], dgg_b, tn, preferred_element_type=F32)
            dpool = lax.dot_general(dgg_b, cw_ref[g], nt, preferred_element_type=F32)
            zf = zf_ref[:, cs].astype(F32)
            dgg_f = (dyyf_ref[:, cs].astype(F32) * (zf * _sig(zf)) * sc * keep_next).astype(BF16)
            dpool_f = lax.dot_general(dgg_f, cw_ref[g], nt, preferred_element_type=F32)
            ext_ref[0:tm, :] = dpool / _counts(i, tm, tm, 0, win)
            ext_ref[tm:tm + PHALO, :] = dpool_f / _counts(i, tm, PHALO, tm, win)
            dv = ext_ref[0:tm, :] - dpool
            for j in range(1, win):
                dv = dv + ext_ref[pl.ds(j, tm), :]
            dq_ref[:, cs] = dv.astype(BF16)

    row = lambda: pl.BlockSpec((tm, WC), lambda i: (i, 0))
    nxt = lambda j: pl.BlockSpec((PHALO, WC), lambda i: (jnp.minimum((i + 1) * nb, last_blk), j))
    return pl.pallas_call(
        body, grid=(n_t,),
        in_specs=[row(), nxt(0), pl.BlockSpec((tm, WC), lambda i: (i, 1)), nxt(1), row(), row(),
                  _const((NG, G, G)), _const((1, WC))],
        out_specs=[pl.BlockSpec((tm, 2 * WC), lambda i: (i, 0)), _const((NG, G, G)), _const((1, WC)), _const((1, WC))],
        out_shape=[_sds((S, 2 * WC), BF16), _sds((NG, G, G), F32), _sds((1, WC), F32), _sds((1, WC), F32)],
        scratch_shapes=[pltpu.VMEM((tm + PHALO, G), F32)],
        name=name, compiler_params=_params(("arbitrary",), 48),
    )(dyy, dyy, q, q, gg, pooled, cw, cscale)


def _place():
    return lax.axis_index("x"), lax.axis_index("y"), lax.axis_index("c")


def _piece(ref, axis, size, index):
    start = index * size
    if axis == len(ref.shape) - 1:
        start = pl.multiple_of(start, LANES)
    idx = [slice(None)] * len(ref.shape)
    idx[axis] = pl.ds(start, size)
    return ref.at[tuple(idx)]


def _gather_weights(shards, axes, small):
    n = len(shards) + 1
    sizes = [s.shape[a] for s, a in zip(shards, axes)]
    full = [_sds(s.shape[:a] + (N_DEV * s.shape[a],) + s.shape[a + 1:], s.dtype) for s, a in zip(shards, axes)]
    full.append(_sds((N_DEV,) + small.shape, small.dtype))

    def body(*refs):
        ins, outs = refs[:n], refs[n:2 * n]
        send_sems, recv_sems, local_sems = refs[2 * n:]
        x, y, c = _place()
        me, sibling = (x, y, c), (x, y, 1 - c)
        chips = [(1 - x, y), (x, 1 - y), (1 - x, 1 - y)]

        def block(t, owner):
            d = 4 * owner[0] + 2 * owner[1] + owner[2]
            return outs[t].at[d] if t == n - 1 else _piece(outs[t], axes[t], sizes[t], d)

        def copy(t, k, owner, to, src=None):
            blk = block(t, owner)
            return pltpu.make_async_remote_copy(
                src_ref=blk if src is None else src, dst_ref=blk,
                send_sem=send_sems.at[7 * t + k], recv_sem=recv_sems.at[7 * t + k],
                device_id=to, device_id_type=MESH)

        mine = [pltpu.make_async_copy(ins[t], block(t, me), local_sems.at[t]) for t in range(n)]
        first = []
        for t in range(n):
            mine[t].start()
            first.append(copy(t, 0, me, sibling, src=ins[t]))
            first += [copy(t, 1 + j, me, (*chip, c), src=ins[t]) for j, chip in enumerate(chips)]
        for cp in first:
            cp.start()
        passed = []
        for j, chip in enumerate(chips):
            for t in range(n):
                copy(t, 1 + j, (*chip, c), me).wait_recv()
                fwd = copy(t, 4 + j, (*chip, c), sibling)
                fwd.start()
                passed.append(fwd)
        for t in range(n):
            copy(t, 0, sibling, me).wait_recv()
        for j, chip in enumerate(chips):
            for t in range(n):
                copy(t, 4 + j, (*chip, 1 - c), me).wait_recv()
        for cp in first + passed:
            cp.wait_send()
        for cp in mine:
            cp.wait()

    return pl.pallas_call(
        body, out_shape=full, in_specs=[ANY] * n, out_specs=[ANY] * n,
        scratch_shapes=[pltpu.SemaphoreType.DMA((7 * n,)), pltpu.SemaphoreType.DMA((7 * n,)),
                        pltpu.SemaphoreType.DMA((n,))],
        name="gather_weights",
    )(*shards, small)


def _pair_exchange(grads, axes, sizes):
    n = len(grads)
    outs_sds = [_sds((4,) + g.shape[:a] + (s,) + g.shape[a + 1:], g.dtype) for g, a, s in zip(grads, axes, sizes)]

    def body(*refs):
        ins, outs = refs[:n], refs[n:2 * n]
        send_sems, recv_sems = refs[2 * n:]
        x, y, c = _place()
        copies = []
        for t in range(n):
            for qi in range(4):
                cp = pltpu.make_async_remote_copy(
                    src_ref=_piece(ins[t], axes[t], sizes[t], 2 * qi + (1 - c)), dst_ref=outs[t].at[qi],
                    send_sem=send_sems.at[4 * t + qi], recv_sem=recv_sems.at[4 * t + qi],
                    device_id=(x, y, 1 - c), device_id_type=MESH)
                cp.start()
                copies.append(cp)
        for cp in copies:
            cp.wait()

    return pl.pallas_call(
        body, out_shape=outs_sds, in_specs=[ANY] * n, out_specs=[ANY] * n,
        scratch_shapes=[pltpu.SemaphoreType.DMA((4 * n,)), pltpu.SemaphoreType.DMA((4 * n,))],
        name="grad_pair_exchange",
    )(*grads)


def _chip_exchange(sums, small):
    n = len(sums)
    outs_sds = [_sds((3,) + s.shape[1:], s.dtype) for s in sums]
    outs_sds.append(_sds((N_DEV,) + small.shape, small.dtype))

    def body(*refs):
        ins, outs = refs[:n + 1], refs[n + 1:2 * n + 2]
        send_sems, recv_sems, small_send, small_recv, local_sem = refs[2 * n + 2:]
        x, y, c = _place()
        me = 4 * x + 2 * y + c
        chips = [(1 - x, y), (x, 1 - y), (1 - x, 1 - y)]
        copies = []
        for t in range(n):
            for j, (qx, qy) in enumerate(chips):
                cp = pltpu.make_async_remote_copy(
                    src_ref=ins[t].at[2 * qx + qy], dst_ref=outs[t].at[j],
                    send_sem=send_sems.at[3 * t + j], recv_sem=recv_sems.at[3 * t + j],
                    device_id=(qx, qy, c), device_id_type=MESH)
                cp.start()
                copies.append(cp)
        own = pltpu.make_async_copy(ins[n], outs[n].at[me], local_sem)
        own.start()
        for k in range(1, N_DEV):
            peer = (1 - x if k & 4 else x, 1 - y if k & 2 else y, 1 - c if k & 1 else c)
            cp = pltpu.make_async_remote_copy(
                src_ref=ins[n], dst_ref=outs[n].at[me],
                send_sem=small_send.at[k - 1], recv_sem=small_recv.at[k - 1],
                device_id=peer, device_id_type=MESH)
            cp.start()
            copies.append(cp)
        for cp in copies:
            cp.wait()
        own.wait()

    return pl.pallas_call(
        body, out_shape=outs_sds, in_specs=[ANY] * (n + 1), out_specs=[ANY] * (n + 1),
        scratch_shapes=[pltpu.SemaphoreType.DMA((3 * n,)), pltpu.SemaphoreType.DMA((3 * n,)),
                        pltpu.SemaphoreType.DMA((N_DEV - 1,)), pltpu.SemaphoreType.DMA((N_DEV - 1,)),
                        pltpu.SemaphoreType.DMA],
        name="grad_chip_exchange",
    )(*sums, small)


def _pair_sum(c_idx, grad, recv, axis, size, split, *, name):
    nd = len(grad.shape)
    piece = grad.shape[:axis] + (size,) + grad.shape[axis + 1:]
    blk = (piece[0] // split,) + piece[1:]

    def g_map(q, r, c_ref):
        idx = [0] * nd
        idx[axis] = 2 * q + c_ref[0]
        idx[0] = idx[0] * split + r if axis == 0 else r
        return tuple(idx)

    def r_map(q, r, c_ref):
        return (q, r) + (0,) * (nd - 1)

    def body(c_ref, g_ref, r_ref, o_ref):
        o_ref[0] = (g_ref[...].astype(F32) + r_ref[0].astype(F32)).astype(BF16)

    return pl.pallas_call(
        body,
        grid_spec=pltpu.PrefetchScalarGridSpec(
            num_scalar_prefetch=1, grid=(4, split),
            in_specs=[pl.BlockSpec(blk, g_map), pl.BlockSpec((1,) + blk, r_map)],
            out_specs=pl.BlockSpec((1,) + blk, r_map)),
        out_shape=_sds((4,) + piece, BF16),
        name=name, compiler_params=_params(("arbitrary", "arbitrary"), 32),
    )(c_idx, grad, recv)


def _adam_math(w, g, m, v):
    m = ADAM_B1 * m + (1.0 - ADAM_B1) * g
    v = ADAM_B2 * v + (1.0 - ADAM_B2) * (g * g)
    m_hat = m / (1.0 - ADAM_B1 ** ADAM_STEP)
    v_hat = v / (1.0 - ADAM_B2 ** ADAM_STEP)
    delta = -ADAM_LR * (m_hat / (jnp.sqrt(v_hat) + ADAM_EPS) + ADAM_WD * w)
    return delta, m, v


def _adam_big(q_idx, sums, recv, w, m, v, split, *, name):
    shape = w.shape
    nd = len(shape)
    blk = (shape[0] // split,) + shape[1:]
    w_map = lambda r, q_ref: (r,) + (0,) * (nd - 1)
    s_map = lambda r, q_ref: (q_ref[0], r) + (0,) * (nd - 1)
    r_map = lambda r, q_ref: (0, r) + (0,) * (nd - 1)

    def body(q_ref, s_ref, r_ref, w_ref, m_ref, v_ref, g_ref, d_ref, nm_ref, nv_ref):
        g = s_ref[0].astype(F32) + r_ref[0].astype(F32) + r_ref[1].astype(F32) + r_ref[2].astype(F32)
        g_ref[...] = g
        d_ref[...], nm_ref[...], nv_ref[...] = _adam_math(w_ref[...], g, m_ref[...], v_ref[...])

    wspec = pl.BlockSpec(blk, w_map)
    return pl.pallas_call(
        body,
        grid_spec=pltpu.PrefetchScalarGridSpec(
            num_scalar_prefetch=1, grid=(split,),
            in_specs=[pl.BlockSpec((1,) + blk, s_map), pl.BlockSpec((3,) + blk, r_map), wspec, wspec, wspec],
            out_specs=[wspec] * 4),
        out_shape=[_sds(shape, F32)] * 4,
        name=name, compiler_params=_params(("arbitrary",), 32),
    )(q_idx, sums, recv, w, m, v)


def _adam_small(parts, w, m, v, *, name):
    R = w.shape[0]

    def body(p_ref, w_ref, m_ref, v_ref, g_ref, d_ref, nm_ref, nv_ref):
        g = p_ref[0]
        for d in range(1, N_DEV):
            g = g + p_ref[d]
        g_ref[...] = g
        d_ref[...], nm_ref[...], nv_ref[...] = _adam_math(w_ref[...], g, m_ref[...], v_ref[...])

    return pl.pallas_call(
        body, out_shape=[_sds((R, LANES), F32)] * 4, name=name,
        compiler_params=pltpu.CompilerParams(vmem_limit_bytes=32 * MIB),
    )(parts, w, m, v)


def _pack(arrs):
    return jnp.concatenate([a.reshape(-1) for a in arrs]).reshape(-1, LANES)


def _unpack(packed, shapes):
    flat = packed.reshape(-1)
    out, off = [], 0
    for s in shapes:
        n = 1
        for d in s:
            n *= d
        out.append(flat[off:off + n].reshape(s))
        off += n
    return out


BIG = ("e_in", "e_out", "o_in", "o_cw", "o_out")
BIG_AXIS = dict(e_in=1, e_out=0, o_in=1, o_cw=1, o_out=0)
BIG_SPLIT = dict(e_in=8, e_out=4, o_in=4, o_cw=4, o_out=4)
REPLICATED = ("e_norm_pre", "e_norm_post", "e_b_conv_bias", "e_b_ln_g", "e_b_ln_b")
SHARDED = ("e_a_conv", "e_b_conv", "o_norm_pre", "o_norm_post", "o_c_b", "o_c_scale")
SMALL = REPLICATED + SHARDED


def _local_step(x, tgt, wt, sm):
    S, D = x.shape
    tm, tnt, tx = min(TM_MM, S), min(TM_NT, S), min(TM_MIX, S)
    W = wt["e_in"].shape[1] // 7

    p, h0 = _norm_matmul(x, sm["e_norm_pre"], wt["e_in"], tm=tm, tn=W, name="e_in_fwd")
    u, cb = _e_mix_fwd(p, sm["e_a_conv"], sm["e_b_conv"], sm["e_b_conv_bias"], sm["e_b_ln_g"], sm["e_b_ln_b"],
                       tm=tx, name="e_mix_fwd")
    x1, y0 = _out_norm_res(u, wt["e_out"], x, sm["e_norm_post"], tm=tx, name="e_out_fwd")
    q, h1 = _norm_matmul(x1, sm["o_norm_pre"], wt["o_in"], tm=tm, tn=W, name="o_in_fwd")
    yy, pooled, gg = _o_mix_fwd(q, wt["o_cw"], sm["o_c_b"], sm["o_c_scale"], tm=tx, name="o_mix_fwd")
    dout, dx2, dyy, lcol, dg_o_post = _out_loss(yy, wt["o_out"], x1, sm["o_norm_post"], tgt, tm=tx, name="o_out_loss")
    loss = (0.5 / D) * jnp.sum(lcol)

    dq, d_cw, d_cb, d_cscale = _o_mix_bwd(dyy, q, gg, pooled, wt["o_cw"], sm["o_c_scale"], tm=tx, name="o_mix_bwd")
    g_o_out = _mm_tn(yy, dout, ts=tm, tn=W, name="o_out_dw")
    dh1 = _mm_nt(dq, wt["o_in"], tm=tnt, tk=W, name="o_in_bwd")
    dx1, dy0, dg_o_pre, dg_e_post = _pre_bwd_o(dh1, x1, dx2, y0, sm["o_norm_pre"], sm["e_norm_post"],
                                               tm=tx, name="o_pre_bwd")
    g_o_in = _mm_tn(h1, dq, ts=tm, tn=W, name="o_in_dw")
    du = _mm_nt(dy0, wt["e_out"], tm=tnt, tk=W, name="e_out_bwd")
    g_e_out = _mm_tn(u, dy0, ts=tm, tn=W, name="e_out_dw")
    dp, d_wa, d_wb, d_bias, d_lg, d_lb = _e_mix_bwd(du, p, cb, sm["e_a_conv"], sm["e_b_conv"], sm["e_b_ln_g"],
                                                    sm["e_b_ln_b"], tm=tx, name="e_mix_bwd")
    dh0 = _mm_nt(dp, wt["e_in"], tm=tnt, tk=W, name="e_in_bwd")
    grad_x, dg_e_pre = _pre_bwd_e(dh0, x, dx1, sm["e_norm_pre"], tm=tx, name="e_pre_bwd")
    g_e_in = _mm_tn(h0, dp, ts=tm, tn=W, name="e_in_dw")

    big = dict(e_in=g_e_in, e_out=g_e_out, o_in=g_o_in, o_cw=d_cw.astype(BF16), o_out=g_o_out)
    small = dict(e_norm_pre=dg_e_pre, e_norm_post=dg_e_post, e_a_conv=d_wa, e_b_conv=d_wb, e_b_conv_bias=d_bias,
                 e_b_ln_g=d_lg, e_b_ln_b=d_lb, o_norm_pre=dg_o_pre, o_norm_post=dg_o_post, o_c_b=d_cb,
                 o_c_scale=d_cscale)
    return loss, grad_x, big, small


def kernel(x, e_norm_pre, e_norm_post, e_w_in, e_a_conv, e_b_conv, e_b_conv_bias, e_b_ln_g, e_b_ln_b, e_w_out, o_norm_pre, o_norm_post, o_w_in, o_c_w, o_c_b, o_c_scale, o_w_out, loss_target, m_e_norm_pre, m_e_norm_post, m_e_w_in, m_e_a_conv, m_e_b_conv, m_e_b_conv_bias, m_e_b_ln_g, m_e_b_ln_b, m_e_w_out, m_o_norm_pre, m_o_norm_post, m_o_w_in, m_o_c_w, m_o_c_b, m_o_c_scale, m_o_w_out, v_e_norm_pre, v_e_norm_post, v_e_w_in, v_e_a_conv, v_e_b_conv, v_e_b_conv_bias, v_e_b_ln_g, v_e_b_ln_b, v_e_w_out, v_o_norm_pre, v_o_norm_post, v_o_w_in, v_o_c_w, v_o_c_b, v_o_c_scale, v_o_w_out):
    xi, yi, ci = _place()
    me = 4 * xi + 2 * yi + ci
    w_big = dict(e_in=e_w_in[0], e_out=e_w_out[0], o_in=o_w_in[0], o_cw=o_c_w[0], o_out=o_w_out[0])
    m_big = dict(e_in=m_e_w_in[0], e_out=m_e_w_out[0], o_in=m_o_w_in[0], o_cw=m_o_c_w[0], o_out=m_o_w_out[0])
    v_big = dict(e_in=v_e_w_in[0], e_out=v_e_w_out[0], o_in=v_o_w_in[0], o_cw=v_o_c_w[0], o_out=v_o_w_out[0])
    w_small = dict(e_norm_pre=e_norm_pre, e_norm_post=e_norm_post, e_b_conv_bias=e_b_conv_bias, e_b_ln_g=e_b_ln_g,
                   e_b_ln_b=e_b_ln_b, e_a_conv=e_a_conv[0], e_b_conv=e_b_conv[0], o_norm_pre=o_norm_pre,
                   o_norm_post=o_norm_post, o_c_b=o_c_b[0], o_c_scale=o_c_scale)
    m_small = dict(e_norm_pre=m_e_norm_pre, e_norm_post=m_e_norm_post, e_b_conv_bias=m_e_b_conv_bias,
                   e_b_ln_g=m_e_b_ln_g, e_b_ln_b=m_e_b_ln_b, e_a_conv=m_e_a_conv[0], e_b_conv=m_e_b_conv[0],
                   o_norm_pre=m_o_norm_pre, o_norm_post=m_o_norm_post, o_c_b=m_o_c_b[0], o_c_scale=m_o_c_scale)
    v_small = dict(e_norm_pre=v_e_norm_pre, e_norm_post=v_e_norm_post, e_b_conv_bias=v_e_b_conv_bias,
                   e_b_ln_g=v_e_b_ln_g, e_b_ln_b=v_e_b_ln_b, e_a_conv=v_e_a_conv[0], e_b_conv=v_e_b_conv[0],
                   o_norm_pre=v_o_norm_pre, o_norm_post=v_o_norm_post, o_c_b=v_o_c_b[0], o_c_scale=v_o_c_scale)

    axes = [BIG_AXIS[k] for k in BIG]
    gathered = _gather_weights([w_big[k].astype(BF16) for k in BIG], axes, _pack([w_small[k] for k in SHARDED]))
    wt = dict(zip(BIG, gathered[:-1]))
    per_dev = [_unpack(gathered[-1][d], [w_small[k].shape for k in SHARDED]) for d in range(N_DEV)]
    sm = {k: w_small[k] for k in REPLICATED}
    for j, k in enumerate(SHARDED):
        sm[k] = jnp.concatenate([per_dev[d][j] for d in range(N_DEV)], axis=-1)
    n_groups = sm["o_c_b"].shape[0]
    sm["o_c_b"] = sm["o_c_b"].reshape(1, -1)

    loss, grad_x, g_big, g_small = _local_step(x[0], loss_target[0], wt, sm)
    loss = lax.psum(loss, ("x", "y", "c"))

    grads = [g_big[k] for k in BIG]
    sizes = [w_big[k].shape[BIG_AXIS[k]] for k in BIG]
    from_sibling = _pair_exchange(grads, axes, sizes)
    c_idx = jnp.reshape(ci, (1,)).astype(jnp.int32)
    sums = [_pair_sum(c_idx, g, r, a, s, BIG_SPLIT[k], name="pair_sum_" + k)
            for k, g, r, a, s in zip(BIG, grads, from_sibling, axes, sizes)]
    g_small["o_c_b"] = g_small["o_c_b"].reshape(n_groups, -1)
    full_shapes = [g_small[k].shape for k in SMALL]
    exchanged = _chip_exchange(sums, _pack([g_small[k] for k in SMALL]))
    from_chips, small_parts = exchanged[:-1], exchanged[-1]

    q_idx = jnp.reshape(2 * xi + yi, (1,)).astype(jnp.int32)
    big_out = {k: _adam_big(q_idx, s, r, w_big[k], m_big[k], v_big[k], BIG_SPLIT[k], name="adam_" + k)
               for k, s, r in zip(BIG, sums, from_chips)}

    def at_full_size(d):
        out = []
        for k, s in zip(SMALL, full_shapes):
            if k in REPLICATED:
                out.append(d[k])
            else:
                n = d[k].shape[-1]
                out.append(lax.dynamic_update_slice_in_dim(jnp.ones(s, F32), d[k], me * n, axis=-1))
        return _pack(out)

    res_small = _adam_small(small_parts, at_full_size(w_small), at_full_size(m_small), at_full_size(v_small),
                            name="adam_small")
    small_out = {k: [] for k in SMALL}
    for packed in res_small:
        for k, full in zip(SMALL, _unpack(packed, full_shapes)):
            if k in SHARDED:
                n = w_small[k].shape[-1]
                full = lax.dynamic_slice_in_dim(full, me * n, n, axis=-1)
            small_out[k].append(full)

    big_of = dict(e_w_in="e_in", e_w_out="e_out", o_w_in="o_in", o_c_w="o_cw", o_w_out="o_out")
    stacked = ("e_a_conv", "e_b_conv", "o_c_b")

    def leaf(name, which):
        if name in big_of:
            return big_out[big_of[name]][which][None]
        t = small_out[name][which]
        return t[None] if name in stacked else t

    order = ("e_norm_pre", "e_norm_post", "e_w_in", "e_a_conv", "e_b_conv", "e_b_conv_bias", "e_b_ln_g", "e_b_ln_b",
             "e_w_out", "o_norm_pre", "o_norm_post", "o_w_in", "o_c_w", "o_c_b", "o_c_scale", "o_w_out")
    outs = [loss, grad_x[None]]
    for which in range(4):
        outs += [leaf(nm, which) for nm in order]
    return tuple(outs)
```

```python
import jax
import jax.numpy as jnp
from jax import lax
from jax.experimental import pallas as pl
from jax.experimental.pallas import tpu as pltpu

F32 = jnp.float32
BF16 = jnp.bfloat16
EPS = 1e-6
MESH = pl.DeviceIdType.MESH
ANY = pl.BlockSpec(memory_space=pl.ANY)

N_DEV = 8
HALO = 32
PHALO = 16
CONV_A = 3
CONV_B = 31
POOL_WINDOWS = (2, 4, 8, 16)
LANES = 128
MIB = 1024 * 1024

ADAM_LR = 0.001
ADAM_B1 = 0.9
ADAM_B2 = 0.999
ADAM_EPS = 1e-08
ADAM_WD = 0.01
ADAM_STEP = 10

TM_MM = 512
TM_NT = 1024
TM_MIX = 256


def _sds(shape, dtype):
    return jax.ShapeDtypeStruct(tuple(shape), dtype)


def _params(sem, vmem_mib):
    return pltpu.CompilerParams(dimension_semantics=sem, vmem_limit_bytes=vmem_mib * MIB)


def _const(shape, single=False):
    n = len(shape)
    if single:
        return pl.BlockSpec(shape, lambda *_: (0,) * n, pipeline_mode=pl.Buffered(1))
    return pl.BlockSpec(shape, lambda *_: (0,) * n)


def _sig(v):
    return jax.nn.sigmoid(v)


def _dsilu(v, s):
    return s * (1.0 + v * (1.0 - s))


def _rms(v):
    return lax.rsqrt(jnp.mean(v * v, axis=-1, keepdims=True) + EPS)


def _norm_bwd(dn, n, r):
    return r * (dn - n * jnp.mean(dn * n, axis=-1, keepdims=True))


def _colsum(v):
    return jnp.sum(v, axis=0, keepdims=True)


class _Comm:
    def __init__(self, inputs, out_shapes, sems, start, finish, aliases=None):
        self.inputs, self.out_shapes, self.sems = list(inputs), list(out_shapes), list(sems)
        self.start, self.finish = start, finish
        self.aliases = dict(aliases or {})


def _merge(*comms):
    comms = [c for c in comms if c is not None]
    if len(comms) <= 1:
        return comms[0] if comms else None
    spans, i0, o0, s0, aliases = [], 0, 0, 0, {}
    for c in comms:
        spans.append((i0, o0, s0))
        aliases.update({i0 + k: o0 + v for k, v in c.aliases.items()})
        i0, o0, s0 = i0 + len(c.inputs), o0 + len(c.out_shapes), s0 + len(c.sems)

    def run(which):
        def fn(ins, outs, sems):
            for c, (i, o, s) in zip(comms, spans):
                getattr(c, which)(ins[i:i + len(c.inputs)], outs[o:o + len(c.out_shapes)], sems[s:s + len(c.sems)])
        return fn

    return _Comm([a for c in comms for a in c.inputs], [a for c in comms for a in c.out_shapes],
                 [a for c in comms for a in c.sems], run("start"), run("finish"), aliases)


def _call(body, *, grid, in_specs, out_specs, out_shape, operands, name, params, scratch_shapes=(), comm=None):
    n_i, n_o, n_s = len(in_specs), len(out_specs), len(scratch_shapes)
    if comm is None:
        res = pl.pallas_call(body, grid=grid, in_specs=list(in_specs), out_specs=list(out_specs),
                             out_shape=list(out_shape), scratch_shapes=list(scratch_shapes),
                             name=name, compiler_params=params)(*operands)
        return list(res), []
    c_i, c_o = len(comm.inputs), len(comm.out_shapes)

    def carrier(*refs):
        ins, cins = refs[:n_i], refs[n_i:n_i + c_i]
        outs = refs[n_i + c_i:n_i + c_i + n_o]
        couts = refs[n_i + c_i + n_o:n_i + c_i + n_o + c_o]
        scr = refs[n_i + c_i + n_o + c_o:n_i + c_i + n_o + c_o + n_s]
        csems = refs[n_i + c_i + n_o + c_o + n_s:]
        ids = [pl.program_id(d) for d in range(len(grid))]
        first = ids[0] == 0
        last = ids[0] == grid[0] - 1
        for d in range(1, len(grid)):
            first = first & (ids[d] == 0)
            last = last & (ids[d] == grid[d] - 1)

        @pl.when(first)
        def _():
            comm.start(cins, couts, csems)

        body(*ins, *outs, *scr)

        @pl.when(last)
        def _():
            comm.finish(cins, couts, csems)

    res = pl.pallas_call(
        carrier, grid=grid, in_specs=list(in_specs) + [ANY] * c_i, out_specs=list(out_specs) + [ANY] * c_o,
        out_shape=list(out_shape) + comm.out_shapes, scratch_shapes=list(scratch_shapes) + comm.sems,
        input_output_aliases={n_i + k: n_o + v for k, v in comm.aliases.items()},
        name=name, compiler_params=params)(*operands, *comm.inputs)
    return list(res[:n_o]), list(res[n_o:])


def _run_comm(comm, name):
    c_i, c_o = len(comm.inputs), len(comm.out_shapes)

    def body(*refs):
        ins, outs, sems = refs[:c_i], refs[c_i:c_i + c_o], refs[c_i + c_o:]
        comm.start(ins, outs, sems)
        comm.finish(ins, outs, sems)

    res = pl.pallas_call(
        body, in_specs=[ANY] * c_i, out_specs=[ANY] * c_o, out_shape=comm.out_shapes, scratch_shapes=comm.sems,
        input_output_aliases=comm.aliases, name=name)(*comm.inputs)
    return list(res)


def _norm_matmul(x, g, w, *, tm, tn, name, comm=None):
    S, D = x.shape
    N = w.shape[1]

    def body(x_ref, g_ref, w_ref, p_ref, h_ref):
        @pl.when(pl.program_id(1) == 0)
        def _():
            xx = x_ref[...]
            h_ref[...] = ((xx * _rms(xx)) * g_ref[...]).astype(BF16)

        p_ref[...] = jnp.dot(h_ref[...], w_ref[...], preferred_element_type=F32).astype(BF16)

    return _call(
        body, grid=(S // tm, N // tn),
        in_specs=[pl.BlockSpec((tm, D), lambda i, j: (i, 0)), _const((1, D)),
                  pl.BlockSpec((D, tn), lambda i, j: (0, j))],
        out_specs=[pl.BlockSpec((tm, tn), lambda i, j: (i, j)), pl.BlockSpec((tm, D), lambda i, j: (i, 0))],
        out_shape=[_sds((S, N), BF16), _sds((S, D), BF16)], operands=(x, g, w),
        name=name, params=_params(("arbitrary", "arbitrary"), 48), comm=comm)


def _out_norm_res(u, w, x, g, *, tm, name):
    S, K = u.shape
    D = w.shape[1]

    def body(u_ref, w_ref, x_ref, g_ref, x1_ref, y_ref):
        y = jnp.dot(u_ref[...], w_ref[...], preferred_element_type=F32)
        y_ref[...] = y.astype(BF16)
        x1_ref[...] = x_ref[...] + (y * _rms(y)) * g_ref[...]

    return pl.pallas_call(
        body, grid=(S // tm,),
        in_specs=[pl.BlockSpec((tm, K), lambda i: (i, 0)), _const((K, D), single=True),
                  pl.BlockSpec((tm, D), lambda i: (i, 0)), _const((1, D))],
        out_specs=[pl.BlockSpec((tm, D), lambda i: (i, 0)), pl.BlockSpec((tm, D), lambda i: (i, 0))],
        out_shape=[_sds((S, D), F32), _sds((S, D), BF16)],
        name=name, compiler_params=_params(("arbitrary",), 48),
    )(u, w, x, g)


def _out_loss(yy, w, x1, g, tgt, *, tm, name):
    S, K = yy.shape
    D = w.shape[1]

    def body(yy_ref, w_ref, x1_ref, g_ref, t_ref, dout_ref, dx2_ref, dyy_ref, lcol_ref, dg_ref):
        out = jnp.dot(yy_ref[...], w_ref[...], preferred_element_type=F32)
        r = _rms(out)
        n = out * r
        gg = g_ref[...]
        e = x1_ref[...] + n * gg - t_ref[...]
        dx2 = e * (1.0 / D)
        dx2_ref[...] = dx2
        dout = _norm_bwd(dx2 * gg, n, r).astype(BF16)
        dout_ref[...] = dout
        dyy_ref[...] = lax.dot_general(dout, w_ref[...], (((1,), (1,)), ((), ())),
                                       preferred_element_type=F32).astype(BF16)

        @pl.when(pl.program_id(0) == 0)
        def _():
            lcol_ref[...] = jnp.zeros_like(lcol_ref)
            dg_ref[...] = jnp.zeros_like(dg_ref)

        lcol_ref[...] += _colsum(e * e)
        dg_ref[...] += _colsum(dx2 * n)

    return pl.pallas_call(
        body, grid=(S // tm,),
        in_specs=[pl.BlockSpec((tm, K), lambda i: (i, 0)), _const((K, D), single=True),
                  pl.BlockSpec((tm, D), lambda i: (i, 0)), _const((1, D)),
                  pl.BlockSpec((tm, D), lambda i: (i, 0))],
        out_specs=[pl.BlockSpec((tm, D), lambda i: (i, 0)), pl.BlockSpec((tm, D), lambda i: (i, 0)),
                   pl.BlockSpec((tm, K), lambda i: (i, 0)), _const((1, D)), _const((1, D))],
        out_shape=[_sds((S, D), BF16), _sds((S, D), F32), _sds((S, K), BF16), _sds((1, D), F32), _sds((1, D), F32)],
        name=name, compiler_params=_params(("arbitrary",), 52),
    )(yy, w, x1, g, tgt)


def _mm_nt(a, w, *, tm, tk, name, comm=None):
    S, N = a.shape
    D = w.shape[0]
    n_k = N // tk

    def body(a_ref, w_ref, o_ref, acc_ref):
        k = pl.program_id(1)
        part = lax.dot_general(a_ref[...], w_ref[...], (((1,), (1,)), ((), ())), preferred_element_type=F32)

        @pl.when(k == 0)
        def _():
            acc_ref[...] = part

        @pl.when(k > 0)
        def _():
            acc_ref[...] += part

        @pl.when(k == n_k - 1)
        def _():
            o_ref[...] = acc_ref[...].astype(BF16)

    outs, extra = _call(
        body, grid=(S // tm, n_k),
        in_specs=[pl.BlockSpec((tm, tk), lambda i, k: (i, k)), pl.BlockSpec((D, tk), lambda i, k: (0, k))],
        out_specs=[pl.BlockSpec((tm, D), lambda i, k: (i, 0))],
        out_shape=[_sds((S, D), BF16)], operands=(a, w),
        scratch_shapes=[pltpu.VMEM((tm, D), F32)],
        name=name, params=_params(("arbitrary", "arbitrary"), 48), comm=comm)
    return outs[0], extra


def _mm_tn(a, b, *, ts, tn, name, comm=None):
    S, M = a.shape
    N = b.shape[1]
    n_s = S // ts

    def body(a_ref, b_ref, o_ref, acc_ref):
        s = pl.program_id(1)
        part = lax.dot_general(a_ref[...], b_ref[...], (((0,), (0,)), ((), ())), preferred_element_type=F32)

        @pl.when(s == 0)
        def _():
            acc_ref[...] = part

        @pl.when(s > 0)
        def _():
            acc_ref[...] += part

        @pl.when(s == n_s - 1)
        def _():
            o_ref[...] = acc_ref[...].astype(BF16)

    outs, extra = _call(
        body, grid=(N // tn, n_s),
        in_specs=[pl.BlockSpec((ts, M), lambda j, s: (s, 0)), pl.BlockSpec((ts, tn), lambda j, s: (s, j))],
        out_specs=[pl.BlockSpec((M, tn), lambda j, s: (0, j))],
        out_shape=[_sds((M, N), BF16)], operands=(a, b),
        scratch_shapes=[pltpu.VMEM((M, tn), F32)],
        name=name, params=_params(("arbitrary", "arbitrary"), 48), comm=comm)
    return outs[0], extra


def _pre_bwd_o(dh, x1, dx2, y0, g_pre, g_post, *, tm, name, comm=None):
    S, D = x1.shape

    def body(dh_ref, x1_ref, dx2_ref, y0_ref, gpre_ref, gpost_ref, dx1_ref, dy0_ref, dgpre_ref, dgpost_ref):
        @pl.when(pl.program_id(0) == 0)
        def _():
            dgpre_ref[...] = jnp.zeros_like(dgpre_ref)
            dgpost_ref[...] = jnp.zeros_like(dgpost_ref)

        dh = dh_ref[...].astype(F32)
        x1 = x1_ref[...]
        r2 = _rms(x1)
        xn = x1 * r2
        dgpre_ref[...] += _colsum(dh * xn)
        dx1 = dx2_ref[...] + _norm_bwd(dh * gpre_ref[...], xn, r2)
        dx1_ref[...] = dx1
        y = y0_ref[...].astype(F32)
        r1 = _rms(y)
        n1 = y * r1
        dgpost_ref[...] += _colsum(dx1 * n1)
        dy0_ref[...] = _norm_bwd(dx1 * gpost_ref[...], n1, r1).astype(BF16)

    row = pl.BlockSpec((tm, D), lambda i: (i, 0))
    return _call(
        body, grid=(S // tm,),
        in_specs=[row, row, row, row, _const((1, D)), _const((1, D))],
        out_specs=[row, row, _const((1, D)), _const((1, D))],
        out_shape=[_sds((S, D), F32), _sds((S, D), BF16), _sds((1, D), F32), _sds((1, D), F32)],
        operands=(dh, x1, dx2, y0, g_pre, g_post),
        name=name, params=_params(("arbitrary",), 48), comm=comm)


def _pre_bwd_e(dh, x, dx1, g_pre, *, tm, name):
    S, D = x.shape

    def body(dh_ref, x_ref, dx1_ref, gpre_ref, gx_ref, dgpre_ref):
        @pl.when(pl.program_id(0) == 0)
        def _():
            dgpre_ref[...] = jnp.zeros_like(dgpre_ref)

        dh = dh_ref[...].astype(F32)
        xx = x_ref[...]
        r0 = _rms(xx)
        xn = xx * r0
        dgpre_ref[...] += _colsum(dh * xn)
        gx_ref[...] = dx1_ref[...] + _norm_bwd(dh * gpre_ref[...], xn, r0)

    row = pl.BlockSpec((tm, D), lambda i: (i, 0))
    return pl.pallas_call(
        body, grid=(S // tm,),
        in_specs=[row, row, row, _const((1, D))],
        out_specs=[row, _const((1, D))],
        out_shape=[_sds((S, D), F32), _sds((1, D), F32)],
        name=name, compiler_params=_params(("arbitrary",), 48),
    )(dh, x, dx1, g_pre)


def _taps(ext_ref, w_ref, n_taps, base, cs, tm):
    acc = ext_ref[pl.ds(base, tm), cs] * w_ref[0:1, cs]
    for k in range(1, n_taps):
        acc = acc + ext_ref[pl.ds(base + k, tm), cs] * w_ref[k:k + 1, cs]
    return acc


def _taps_rev(ext_ref, w_ref, n_taps, cs, tm):
    acc = ext_ref[pl.ds(n_taps - 1, tm), cs] * w_ref[0:1, cs]
    for k in range(1, n_taps):
        acc = acc + ext_ref[pl.ds(n_taps - 1 - k, tm), cs] * w_ref[k:k + 1, cs]
    return acc


def _e_mix_fwd(p, wa, wb, bias, ln_g, ln_b, *, tm, name, comm=None):
    S = p.shape[0]
    W = p.shape[1] // 7
    nb = tm // HALO
    chunks = [slice(c * LANES, (c + 1) * LANES) for c in range(W // LANES)]

    def body(p_ref, hax_ref, hac_ref, hbv_ref, hbg_ref, wa_ref, wb_ref, bias_ref, lg_ref, lb_ref,
             u_ref, cb_ref, ext_ref):
        keep = (pl.program_id(0) > 0).astype(F32)
        col = lambda j, cs: p_ref[:, j * W + cs.start:j * W + cs.stop].astype(F32)

        ext_ref[0:HALO, :] = hax_ref[...].astype(F32) * hac_ref[...].astype(F32) * keep
        ext_ref[HALO:, :] = p_ref[:, 2 * W:3 * W].astype(F32) * p_ref[:, 0:W].astype(F32)
        for cs in chunks:
            conv = _taps(ext_ref, wa_ref, CONV_A, HALO - (CONV_A - 1), cs, tm)
            az = col(3, cs)
            u_ref[:, cs] = (col(1, cs) * conv * (az * _sig(az))).astype(BF16)

        ext_ref[0:HALO, :] = hbv_ref[...].astype(F32) * _sig(hbg_ref[...].astype(F32)) * keep
        ext_ref[HALO:, :] = p_ref[:, 4 * W:5 * W].astype(F32) * _sig(p_ref[:, 5 * W:6 * W].astype(F32))
        s1 = jnp.zeros((tm, LANES), F32)
        for cs in chunks:
            cb = _taps(ext_ref, wb_ref, CONV_B, HALO - (CONV_B - 1), cs, tm) + bias_ref[:, cs]
            cb_ref[:, cs] = cb
            s1 = s1 + cb
        mu = jnp.sum(s1, axis=-1, keepdims=True) * (1.0 / W)
        s2 = jnp.zeros((tm, LANES), F32)
        for cs in chunks:
            xc = cb_ref[:, cs] - mu
            s2 = s2 + xc * xc
        rs = lax.rsqrt(jnp.sum(s2, axis=-1, keepdims=True) * (1.0 / W) + EPS)
        for cs in chunks:
            lb = (cb_ref[:, cs] - mu) * rs * lg_ref[:, cs] + lb_ref[:, cs]
            bz = col(6, cs)
            u_ref[:, W + cs.start:W + cs.stop] = (lb * _sig(lb) * (bz * _sig(bz))).astype(BF16)

    prev = lambda j: pl.BlockSpec((HALO, W), lambda i: (jnp.maximum(i * nb - 1, 0), j))
    return _call(
        body, grid=(S // tm,),
        in_specs=[pl.BlockSpec((tm, 7 * W), lambda i: (i, 0)), prev(0), prev(2), prev(4), prev(5),
                  _const((CONV_A, W)), _const((CONV_B, W)), _const((1, W)), _const((1, W)), _const((1, W))],
        out_specs=[pl.BlockSpec((tm, 2 * W), lambda i: (i, 0)), pl.BlockSpec((tm, W), lambda i: (i, 0))],
        out_shape=[_sds((S, 2 * W), BF16), _sds((S, W), F32)],
        operands=(p, p, p, p, p, wa, wb, bias, ln_g, ln_b),
        scratch_shapes=[pltpu.VMEM((HALO + tm, W), F32)],
        name=name, params=_params(("arbitrary",), 48), comm=comm)


def _e_mix_bwd(du, p, cb, wa, wb, ln_g, ln_b, *, tm, name, comm=None):
    S = p.shape[0]
    W = p.shape[1] // 7
    nb = tm // HALO
    n_t = S // tm
    last_blk = S // HALO - 1
    chunks = [slice(c * LANES, (c + 1) * LANES) for c in range(W // LANES)]

    def body(du_ref, duf_ref, p_ref, fab_ref, faz_ref, fbz_ref, hax_ref, hac_ref, hbv_ref, hbg_ref,
             cb_ref, cbf_ref, wa_ref, wb_ref, lg_ref, lb_ref,
             dp_ref, dwa_ref, dwb_ref, dbias_ref, dlg_ref, dlb_ref, extd_ref, extg_ref):
        i = pl.program_id(0)
        keep_prev = (i > 0).astype(F32)
        keep_next = (i < n_t - 1).astype(F32)
        col = lambda j, cs: p_ref[:, j * W + cs.start:j * W + cs.stop].astype(F32)

        @pl.when(i == 0)
        def _():
            dwa_ref[...] = jnp.zeros_like(dwa_ref)
            dwb_ref[...] = jnp.zeros_like(dwb_ref)
            dbias_ref[...] = jnp.zeros_like(dbias_ref)
            dlg_ref[...] = jnp.zeros_like(dlg_ref)
            dlb_ref[...] = jnp.zeros_like(dlb_ref)

        def dcb_rows(rows, cb_rows_ref, dub, bz_of, dst0, scale, main):
            cbv = cb_rows_ref[...]
            mu = jnp.mean(cbv, axis=-1, keepdims=True)
            xc = cbv - mu
            rs = lax.rsqrt(jnp.mean(xc * xc, axis=-1, keepdims=True) + EPS)
            m1 = jnp.zeros((rows, LANES), F32)
            m2 = jnp.zeros((rows, LANES), F32)
            for cs in chunks:
                nbv = (cb_rows_ref[:, cs] - mu) * rs
                lb = nbv * lg_ref[:, cs] + lb_ref[:, cs]
                sl = _sig(lb)
                bz = bz_of(cs)
                sz = _sig(bz)
                dub_c = dub(cs)
                dlb = dub_c * (bz * sz) * _dsilu(lb, sl)
                if main:
                    dlg_ref[:, cs] += _colsum(dlb * nbv)
                    dlb_ref[:, cs] += _colsum(dlb)
                    dp_ref[:, 6 * W + cs.start:6 * W + cs.stop] = (dub_c * (lb * sl) * _dsilu(bz, sz)).astype(BF16)
                dnb = dlb * lg_ref[:, cs]
                extd_ref[dst0:dst0 + rows, cs] = dnb
                m1 = m1 + dnb
                m2 = m2 + dnb * nbv
            m1 = jnp.sum(m1, axis=-1, keepdims=True) * (1.0 / W)
            m2 = jnp.sum(m2, axis=-1, keepdims=True) * (1.0 / W)
            for cs in chunks:
                nbv = (cb_rows_ref[:, cs] - mu) * rs
                dcb = rs * (extd_ref[dst0:dst0 + rows, cs] - m1 - nbv * m2) * scale
                extd_ref[dst0:dst0 + rows, cs] = dcb
                if main:
                    dbias_ref[:, cs] += _colsum(dcb)

        dcb_rows(tm, cb_ref, lambda cs: du_ref[:, W + cs.start:W + cs.stop].astype(F32),
                 lambda cs: col(6, cs), 0, 1.0, True)
        dcb_rows(HALO, cbf_ref, lambda cs: duf_ref[:, W + cs.start:W + cs.stop].astype(F32),
                 lambda cs: fbz_ref[:, cs].astype(F32), tm, keep_next, False)

        extg_ref[0:HALO, :] = hbv_ref[...].astype(F32) * _sig(hbg_ref[...].astype(F32)) * keep_prev
        extg_ref[HALO:, :] = p_ref[:, 4 * W:5 * W].astype(F32) * _sig(p_ref[:, 5 * W:6 * W].astype(F32))
        base_b = HALO - (CONV_B - 1)
        for cs in chunks:
            dgb = _taps_rev(extd_ref, wb_ref, CONV_B, cs, tm)
            bv = col(4, cs)
            sg = _sig(col(5, cs))
            dp_ref[:, 4 * W + cs.start:4 * W + cs.stop] = (dgb * sg).astype(BF16)
            dp_ref[:, 5 * W + cs.start:5 * W + cs.stop] = (dgb * bv * sg * (1.0 - sg)).astype(BF16)
            dcb = extd_ref[0:tm, cs]
            for k in range(CONV_B):
                dwb_ref[k:k + 1, cs] += _colsum(dcb * extg_ref[pl.ds(base_b + k, tm), cs])

        extg_ref[0:HALO, :] = hax_ref[...].astype(F32) * hac_ref[...].astype(F32) * keep_prev
        extg_ref[HALO:, :] = p_ref[:, 2 * W:3 * W].astype(F32) * p_ref[:, 0:W].astype(F32)
        base_a = HALO - (CONV_A - 1)
        for cs in chunks:
            conv = _taps(extg_ref, wa_ref, CONV_A, base_a, cs, tm)
            az = col(3, cs)
            sz = _sig(az)
            ab = col(1, cs)
            dua = du_ref[:, cs].astype(F32)
            dya = dua * (az * sz)
            dp_ref[:, W + cs.start:W + cs.stop] = (dya * conv).astype(BF16)
            dp_ref[:, 3 * W + cs.start:3 * W + cs.stop] = (dua * (ab * conv) * _dsilu(az, sz)).astype(BF16)
            extd_ref[0:tm, cs] = dya * ab
            azf = faz_ref[:, cs].astype(F32)
            extd_ref[tm:tm + HALO, cs] = (duf_ref[:, cs].astype(F32) * (azf * _sig(azf))
                                          * fab_ref[:, cs].astype(F32) * keep_next)
        for cs in chunks:
            dca = _taps_rev(extd_ref, wa_ref, CONV_A, cs, tm)
            dp_ref[:, cs] = (dca * col(2, cs)).astype(BF16)
            dp_ref[:, 2 * W + cs.start:2 * W + cs.stop] = (dca * col(0, cs)).astype(BF16)
            dconv = extd_ref[0:tm, cs]
            for k in range(CONV_A):
                dwa_ref[k:k + 1, cs] += _colsum(dconv * extg_ref[pl.ds(base_a + k, tm), cs])

    prev = lambda j: pl.BlockSpec((HALO, W), lambda i: (jnp.maximum(i * nb - 1, 0), j))
    nxt = lambda j, w: pl.BlockSpec((HALO, w), lambda i: (jnp.minimum((i + 1) * nb, last_blk), j))
    row = lambda w: pl.BlockSpec((tm, w), lambda i: (i, 0))
    return _call(
        body, grid=(n_t,),
        in_specs=[row(2 * W), nxt(0, 2 * W), row(7 * W), nxt(1, W), nxt(3, W), nxt(6, W),
                  prev(0), prev(2), prev(4), prev(5), row(W), nxt(0, W),
                  _const((CONV_A, W)), _const((CONV_B, W)), _const((1, W)), _const((1, W))],
        out_specs=[row(7 * W), _const((CONV_A, W)), _const((CONV_B, W)), _const((1, W)), _const((1, W)), _const((1, W))],
        out_shape=[_sds((S, 7 * W), BF16), _sds((CONV_A, W), F32), _sds((CONV_B, W), F32),
                   _sds((1, W), F32), _sds((1, W), F32), _sds((1, W), F32)],
        operands=(du, du, p, p, p, p, p, p, p, p, cb, cb, wa, wb, ln_g, ln_b),
        scratch_shapes=[pltpu.VMEM((tm + HALO, W), F32), pltpu.VMEM((HALO + tm, W), F32)],
        name=name, params=_params(("arbitrary",), 52), comm=comm)


def _counts(i, tm, rows, off, win):
    t = i * tm + off + lax.broadcasted_iota(jnp.int32, (rows, 1), 0)
    return jnp.minimum(t + 1, win).astype(F32)


def _o_mix_fwd(q, cw, cb, cscale, *, tm, name):
    S = q.shape[0]
    WC = q.shape[1] // 2
    NG = len(POOL_WINDOWS)
    G = WC // NG
    nb = tm // PHALO

    def body(v_ref, z_ref, hv_ref, cw_ref, cb_ref, sc_ref, yy_ref, pooled_ref, gg_ref, ext_ref):
        i = pl.program_id(0)
        keep = (i > 0).astype(F32)
        for g, win in enumerate(POOL_WINDOWS):
            cs = slice(g * G, (g + 1) * G)
            v = v_ref[:, cs].astype(F32)
            ext_ref[0:PHALO, :] = hv_ref[:, cs].astype(F32) * keep
            ext_ref[PHALO:, :] = v
            s = v
            for j in range(1, win):
                s = s + ext_ref[pl.ds(PHALO - j, tm), :]
            pooled = (s / _counts(i, tm, tm, 0, win) - v).astype(BF16)
            pooled_ref[:, cs] = pooled
            gg = jnp.dot(pooled, cw_ref[g], preferred_element_type=F32) + cb_ref[:, cs]
            gg_ref[:, cs] = gg.astype(BF16)
            z = z_ref[:, cs].astype(F32)
            yy_ref[:, cs] = (gg * sc_ref[:, cs] * (z * _sig(z))).astype(BF16)

    row = lambda j: pl.BlockSpec((tm, WC), lambda i: (i, j))
    out = pl.BlockSpec((tm, WC), lambda i: (i, 0))
    return pl.pallas_call(
        body, grid=(S // tm,),
        in_specs=[row(0), row(1), pl.BlockSpec((PHALO, WC), lambda i: (jnp.maximum(i * nb - 1, 0), 0)),
                  _const((NG, G, G)), _const((1, WC)), _const((1, WC))],
        out_specs=[out, out, out],
        out_shape=[_sds((S, WC), BF16)] * 3,
        scratch_shapes=[pltpu.VMEM((PHALO + tm, G), F32)],
        name=name, compiler_params=_params(("arbitrary",), 40),
    )(q, q, q, cw, cb, cscale)


def _o_mix_bwd(dyy, q, gg, pooled, cw, cscale, *, tm, name):
    S = q.shape[0]
    WC = q.shape[1] // 2
    NG = len(POOL_WINDOWS)
    G = WC // NG
    nb = tm // PHALO
    n_t = S // tm
    last_blk = S // PHALO - 1
    nt = (((1,), (1,)), ((), ()))
    tn = (((0,), (0,)), ((), ()))

    def body(dyy_ref, dyyf_ref, z_ref, zf_ref, gg_ref, pooled_ref, cw_ref, sc_ref,
             dq_ref, dcw_ref, dcb_ref, dsc_ref, ext_ref):
        i = pl.program_id(0)
        keep_next = (i < n_t - 1).astype(F32)

        @pl.when(i == 0)
        def _():
            dcw_ref[...] = jnp.zeros_like(dcw_ref)
            dcb_ref[...] = jnp.zeros_like(dcb_ref)
            dsc_ref[...] = jnp.zeros_like(dsc_ref)

        for g, win in enumerate(POOL_WINDOWS):
            cs = slice(g * G, (g + 1) * G)
            sc = sc_ref[:, cs]
            z = z_ref[:, cs].astype(F32)
            sz = _sig(z)
            dyy_c = dyy_ref[:, cs].astype(F32)
            ggv = gg_ref[:, cs].astype(F32)
            dyy0 = dyy_c * (z * sz)
            dq_ref[:, WC + cs.start:WC + cs.stop] = (dyy_c * (ggv * sc) * _dsilu(z, sz)).astype(BF16)
            dgg = dyy0 * sc
            dsc_ref[:, cs] += _colsum(dyy0 * ggv)
            dcb_ref[:, cs] += _colsum(dgg)
            dgg_b = dgg.astype(BF16)
            dcw_ref[g] += lax.dot_general(pooled_ref[:, cs], dgg_b, tn, preferred_element_type=F32)
            dpool = lax.dot_general(dgg_b, cw_ref[g], nt, preferred_element_type=F32)
            zf = zf_ref[:, cs].astype(F32)
            dgg_f = (dyyf_ref[:, cs].astype(F32) * (zf * _sig(zf)) * sc * keep_next).astype(BF16)
            dpool_f = lax.dot_general(dgg_f, cw_ref[g], nt, preferred_element_type=F32)
            ext_ref[0:tm, :] = dpool / _counts(i, tm, tm, 0, win)
            ext_ref[tm:tm + PHALO, :] = dpool_f / _counts(i, tm, PHALO, tm, win)
            dv = ext_ref[0:tm, :] - dpool
            for j in range(1, win):
                dv = dv + ext_ref[pl.ds(j, tm), :]
            dq_ref[:, cs] = dv.astype(BF16)

    row = lambda: pl.BlockSpec((tm, WC), lambda i: (i, 0))
    nxt = lambda j: pl.BlockSpec((PHALO, WC), lambda i: (jnp.minimum((i + 1) * nb, last_blk), j))
    return pl.pallas_call(
        body, grid=(n_t,),
        in_specs=[row(), nxt(0), pl.BlockSpec((tm, WC), lambda i: (i, 1)), nxt(1), row(), row(),
                  _const((NG, G, G)), _const((1, WC))],
        out_specs=[pl.BlockSpec((tm, 2 * WC), lambda i: (i, 0)), _const((NG, G, G)), _const((1, WC)), _const((1, WC))],
        out_shape=[_sds((S, 2 * WC), BF16), _sds((NG, G, G), F32), _sds((1, WC), F32), _sds((1, WC), F32)],
        scratch_shapes=[pltpu.VMEM((tm + PHALO, G), F32)],
        name=name, compiler_params=_params(("arbitrary",), 48),
    )(dyy, dyy, q, q, gg, pooled, cw, cscale)


def _place():
    return lax.axis_index("x"), lax.axis_index("y"), lax.axis_index("c")


def _piece(ref, axis, size, index):
    start = index * size
    if axis == len(ref.shape) - 1:
        start = pl.multiple_of(start, LANES)
    idx = [slice(None)] * len(ref.shape)
    idx[axis] = pl.ds(start, size)
    return ref.at[tuple(idx)]


def _gather_weights(shards, axes, small):
    n = len(shards) + 1
    sizes = [s.shape[a] for s, a in zip(shards, axes)]
    full = [_sds(s.shape[:a] + (N_DEV * s.shape[a],) + s.shape[a + 1:], s.dtype) for s, a in zip(shards, axes)]
    full.append(_sds((N_DEV,) + small.shape, small.dtype))

    def body(*refs):
        ins, outs = refs[:n], refs[n:2 * n]
        send_sems, recv_sems, local_sems = refs[2 * n:]
        x, y, c = _place()
        me, sibling = (x, y, c), (x, y, 1 - c)
        chips = [(1 - x, y), (x, 1 - y), (1 - x, 1 - y)]

        def block(t, owner):
            d = 4 * owner[0] + 2 * owner[1] + owner[2]
            return outs[t].at[d] if t == n - 1 else _piece(outs[t], axes[t], sizes[t], d)

        def copy(t, k, owner, to, src=None):
            blk = block(t, owner)
            return pltpu.make_async_remote_copy(
                src_ref=blk if src is None else src, dst_ref=blk,
                send_sem=send_sems.at[7 * t + k], recv_sem=recv_sems.at[7 * t + k],
                device_id=to, device_id_type=MESH)

        mine = [pltpu.make_async_copy(ins[t], block(t, me), local_sems.at[t]) for t in range(n)]
        first = []
        for t in range(n):
            mine[t].start()
            first.append(copy(t, 0, me, sibling, src=ins[t]))
            first += [copy(t, 1 + j, me, (*chip, c), src=ins[t]) for j, chip in enumerate(chips)]
        for cp in first:
            cp.start()
        passed = []
        for j, chip in enumerate(chips):
            for t in range(n):
                copy(t, 1 + j, (*chip, c), me).wait_recv()
                fwd = copy(t, 4 + j, (*chip, c), sibling)
                fwd.start()
                passed.append(fwd)
        for t in range(n):
            copy(t, 0, sibling, me).wait_recv()
        for j, chip in enumerate(chips):
            for t in range(n):
                copy(t, 4 + j, (*chip, 1 - c), me).wait_recv()
        for cp in first + passed:
            cp.wait_send()
        for cp in mine:
            cp.wait()

    return pl.pallas_call(
        body, out_shape=full, in_specs=[ANY] * n, out_specs=[ANY] * n,
        scratch_shapes=[pltpu.SemaphoreType.DMA((7 * n,)), pltpu.SemaphoreType.DMA((7 * n,)),
                        pltpu.SemaphoreType.DMA((n,))],
        name="gather_weights",
    )(*shards, small)


def _gather_stage1(shards, axes):
    n = len(shards)
    sizes = [s.shape[a] for s, a in zip(shards, axes)]
    full = [_sds(s.shape[:a] + (N_DEV * s.shape[a],) + s.shape[a + 1:], s.dtype) for s, a in zip(shards, axes)]

    def copies(ins, outs, sems):
        send_sems, recv_sems, local_sems = sems
        x, y, c = _place()
        peers = [(x, y, 1 - c), (1 - x, y, c), (x, 1 - y, c), (1 - x, 1 - y, c)]
        out = []
        for t in range(n):
            blk = _piece(outs[t], axes[t], sizes[t], 4 * x + 2 * y + c)
            out.append(pltpu.make_async_copy(ins[t], blk, local_sems.at[t]))
            for k, peer in enumerate(peers):
                out.append(pltpu.make_async_remote_copy(
                    src_ref=ins[t], dst_ref=blk, send_sem=send_sems.at[4 * t + k], recv_sem=recv_sems.at[4 * t + k],
                    device_id=peer, device_id_type=MESH))
        return out

    def start(ins, outs, sems):
        for cp in copies(ins, outs, sems):
            cp.start()

    def finish(ins, outs, sems):
        for cp in copies(ins, outs, sems):
            cp.wait()

    sems = [pltpu.SemaphoreType.DMA((4 * n,)), pltpu.SemaphoreType.DMA((4 * n,)), pltpu.SemaphoreType.DMA((n,))]
    return _Comm(shards, full, sems, start, finish)


def _gather_stage2(fulls, axes):
    n = len(fulls)
    sizes = [f.shape[a] // N_DEV for f, a in zip(fulls, axes)]

    def copies(ins, outs, sems):
        send_sems, recv_sems = sems
        x, y, c = _place()
        out = []
        for t in range(n):
            for j, (qx, qy) in enumerate([(1 - x, y), (x, 1 - y), (1 - x, 1 - y)]):
                d = 4 * qx + 2 * qy + c
                out.append(pltpu.make_async_remote_copy(
                    src_ref=_piece(ins[t], axes[t], sizes[t], d), dst_ref=_piece(outs[t], axes[t], sizes[t], d),
                    send_sem=send_sems.at[3 * t + j], recv_sem=recv_sems.at[3 * t + j],
                    device_id=(x, y, 1 - c), device_id_type=MESH))
        return out

    def start(ins, outs, sems):
        for cp in copies(ins, outs, sems):
            cp.start()

    def finish(ins, outs, sems):
        for cp in copies(ins, outs, sems):
            cp.wait()

    sems = [pltpu.SemaphoreType.DMA((3 * n,)), pltpu.SemaphoreType.DMA((3 * n,))]
    return _Comm(fulls, [_sds(f.shape, f.dtype) for f in fulls], sems, start, finish,
                 aliases={t: t for t in range(n)})


def _pair_comm(grads, axes, sizes):
    n = len(grads)
    outs_sds = [_sds((4,) + g.shape[:a] + (s,) + g.shape[a + 1:], g.dtype) for g, a, s in zip(grads, axes, sizes)]

    def copies(ins, outs, sems):
        send_sems, recv_sems = sems
        x, y, c = _place()
        return [pltpu.make_async_remote_copy(
            src_ref=_piece(ins[t], axes[t], sizes[t], 2 * qi + (1 - c)), dst_ref=outs[t].at[qi],
            send_sem=send_sems.at[4 * t + qi], recv_sem=recv_sems.at[4 * t + qi],
            device_id=(x, y, 1 - c), device_id_type=MESH) for t in range(n) for qi in range(4)]

    def start(ins, outs, sems):
        for cp in copies(ins, outs, sems):
            cp.start()

    def finish(ins, outs, sems):
        for cp in copies(ins, outs, sems):
            cp.wait()

    sems = [pltpu.SemaphoreType.DMA((4 * n,)), pltpu.SemaphoreType.DMA((4 * n,))]
    return _Comm(grads, outs_sds, sems, start, finish)


def _chip_comm(sums):
    n = len(sums)
    outs_sds = [_sds((3,) + s.shape[1:], s.dtype) for s in sums]

    def copies(ins, outs, sems):
        send_sems, recv_sems = sems
        x, y, c = _place()
        return [pltpu.make_async_remote_copy(
            src_ref=ins[t].at[2 * qx + qy], dst_ref=outs[t].at[j],
            send_sem=send_sems.at[3 * t + j], recv_sem=recv_sems.at[3 * t + j],
            device_id=(qx, qy, c), device_id_type=MESH)
            for t in range(n) for j, (qx, qy) in enumerate([(1 - x, y), (x, 1 - y), (1 - x, 1 - y)])]

    def start(ins, outs, sems):
        for cp in copies(ins, outs, sems):
            cp.start()

    def finish(ins, outs, sems):
        for cp in copies(ins, outs, sems):
            cp.wait()

    sems = [pltpu.SemaphoreType.DMA((3 * n,)), pltpu.SemaphoreType.DMA((3 * n,))]
    return _Comm(sums, outs_sds, sems, start, finish)


def _small_comm(small):
    def copies(ins, outs, sems):
        send_sems, recv_sems, local_sem = sems
        x, y, c = _place()
        mine = outs[0].at[4 * x + 2 * y + c]
        out = [pltpu.make_async_copy(ins[0], mine, local_sem.at[0])]
        for k in range(1, N_DEV):
            peer = (1 - x if k & 4 else x, 1 - y if k & 2 else y, 1 - c if k & 1 else c)
            out.append(pltpu.make_async_remote_copy(
                src_ref=ins[0], dst_ref=mine, send_sem=send_sems.at[k - 1], recv_sem=recv_sems.at[k - 1],
                device_id=peer, device_id_type=MESH))
        return out

    def start(ins, outs, sems):
        for cp in copies(ins, outs, sems):
            cp.start()

    def finish(ins, outs, sems):
        for cp in copies(ins, outs, sems):
            cp.wait()

    sems = [pltpu.SemaphoreType.DMA((N_DEV - 1,)), pltpu.SemaphoreType.DMA((N_DEV - 1,)), pltpu.SemaphoreType.DMA((1,))]
    return _Comm([small], [_sds((N_DEV,) + small.shape, small.dtype)], sems, start, finish)


def _pair_sum(c_idx, grad, recv, axis, size, split, *, name):
    nd = len(grad.shape)
    piece = grad.shape[:axis] + (size,) + grad.shape[axis + 1:]
    blk = (piece[0] // split,) + piece[1:]

    def g_map(q, r, c_ref):
        idx = [0] * nd
        idx[axis] = 2 * q + c_ref[0]
        idx[0] = idx[0] * split + r if axis == 0 else r
        return tuple(idx)

    def r_map(q, r, c_ref):
        return (q, r) + (0,) * (nd - 1)

    def body(c_ref, g_ref, r_ref, o_ref):
        o_ref[0] = (g_ref[...].astype(F32) + r_ref[0].astype(F32)).astype(BF16)

    return pl.pallas_call(
        body,
        grid_spec=pltpu.PrefetchScalarGridSpec(
            num_scalar_prefetch=1, grid=(4, split),
            in_specs=[pl.BlockSpec(blk, g_map), pl.BlockSpec((1,) + blk, r_map)],
            out_specs=pl.BlockSpec((1,) + blk, r_map)),
        out_shape=_sds((4,) + piece, BF16),
        name=name, compiler_params=_params(("arbitrary", "arbitrary"), 32),
    )(c_idx, grad, recv)


def _adam_math(w, g, m, v):
    m = ADAM_B1 * m + (1.0 - ADAM_B1) * g
    v = ADAM_B2 * v + (1.0 - ADAM_B2) * (g * g)
    m_hat = m / (1.0 - ADAM_B1 ** ADAM_STEP)
    v_hat = v / (1.0 - ADAM_B2 ** ADAM_STEP)
    delta = -ADAM_LR * (m_hat / (jnp.sqrt(v_hat) + ADAM_EPS) + ADAM_WD * w)
    return delta, m, v


def _adam_big(q_idx, sums, recv, w, m, v, split, *, name):
    shape = w.shape
    nd = len(shape)
    blk = (shape[0] // split,) + shape[1:]
    w_map = lambda r, q_ref: (r,) + (0,) * (nd - 1)
    s_map = lambda r, q_ref: (q_ref[0], r) + (0,) * (nd - 1)
    r_map = lambda r, q_ref: (0, r) + (0,) * (nd - 1)

    def body(q_ref, s_ref, r_ref, w_ref, m_ref, v_ref, g_ref, d_ref, nm_ref, nv_ref):
        g = s_ref[0].astype(F32) + r_ref[0].astype(F32) + r_ref[1].astype(F32) + r_ref[2].astype(F32)
        g_ref[...] = g
        d_ref[...], nm_ref[...], nv_ref[...] = _adam_math(w_ref[...], g, m_ref[...], v_ref[...])

    wspec = pl.BlockSpec(blk, w_map)
    return pl.pallas_call(
        body,
        grid_spec=pltpu.PrefetchScalarGridSpec(
            num_scalar_prefetch=1, grid=(split,),
            in_specs=[pl.BlockSpec((1,) + blk, s_map), pl.BlockSpec((3,) + blk, r_map), wspec, wspec, wspec],
            out_specs=[wspec] * 4),
        out_shape=[_sds(shape, F32)] * 4,
        name=name, compiler_params=_params(("arbitrary",), 32),
    )(q_idx, sums, recv, w, m, v)


def _adam_small(parts, w, m, v, *, name):
    R = w.shape[0]

    def body(p_ref, w_ref, m_ref, v_ref, g_ref, d_ref, nm_ref, nv_ref):
        g = p_ref[0]
        for d in range(1, N_DEV):
            g = g + p_ref[d]
        g_ref[...] = g
        d_ref[...], nm_ref[...], nv_ref[...] = _adam_math(w_ref[...], g, m_ref[...], v_ref[...])

    return pl.pallas_call(
        body, out_shape=[_sds((R, LANES), F32)] * 4, name=name,
        compiler_params=pltpu.CompilerParams(vmem_limit_bytes=32 * MIB),
    )(parts, w, m, v)


def _pack(arrs):
    return jnp.concatenate([a.reshape(-1) for a in arrs]).reshape(-1, LANES)


def _unpack(packed, shapes):
    flat = packed.reshape(-1)
    out, off = [], 0
    for s in shapes:
        n = 1
        for d in s:
            n *= d
        out.append(flat[off:off + n].reshape(s))
        off += n
    return out


BIG = ("e_in", "e_out", "o_in", "o_cw", "o_out")
BIG_AXIS = dict(e_in=1, e_out=0, o_in=1, o_cw=1, o_out=0)
BIG_SPLIT = dict(e_in=8, e_out=4, o_in=4, o_cw=4, o_out=4)
REPLICATED = ("e_norm_pre", "e_norm_post", "e_b_conv_bias", "e_b_ln_g", "e_b_ln_b")
SHARDED = ("e_a_conv", "e_b_conv", "o_norm_pre", "o_norm_post", "o_c_b", "o_c_scale")
SMALL = REPLICATED + SHARDED


class _Exchange:
    def __init__(self, shards, c_idx):
        self.shards = shards
        self.c_idx = c_idx
        self.reduced = {}

    def gather1(self):
        keys = list(self.shards)
        return _gather_stage1([self.shards[k] for k in keys], [BIG_AXIS[k] for k in keys])

    def gather2(self, fulls):
        return _gather_stage2(fulls, [BIG_AXIS[k] for k in self.shards])

    def pair(self, grads):
        keys = list(grads)
        return _pair_comm([grads[k] for k in keys], [BIG_AXIS[k] for k in keys],
                          [grads[k].shape[BIG_AXIS[k]] // N_DEV for k in keys])

    def pair_sums(self, grads, received):
        return {k: _pair_sum(self.c_idx, grads[k], r, BIG_AXIS[k], grads[k].shape[BIG_AXIS[k]] // N_DEV,
                             BIG_SPLIT[k], name="pair_sum_" + k) for k, r in zip(grads, received)}

    def chips(self, sums):
        return _chip_comm([sums[k] for k in sums])

    def done(self, sums, received):
        self.reduced.update({k: (sums[k], r) for k, r in zip(sums, received)})


def _local_step(x, tgt, wt, sm, ex=None):
    S, D = x.shape
    tm, tnt, tx = min(TM_MM, S), min(TM_NT, S), min(TM_MIX, S)
    W = wt["e_in"].shape[1] // 7
    on = ex is not None

    (p, h0), part = _norm_matmul(x, sm["e_norm_pre"], wt["e_in"], tm=tm, tn=W, name="e_in_fwd",
                                 comm=ex.gather1() if on else None)
    (u, cb), full = _e_mix_fwd(p, sm["e_a_conv"], sm["e_b_conv"], sm["e_b_conv_bias"], sm["e_b_ln_g"],
                               sm["e_b_ln_b"], tm=tx, name="e_mix_fwd", comm=ex.gather2(part) if on else None)
    if on:
        wt = {**wt, **dict(zip(ex.shards, full))}
    x1, y0 = _out_norm_res(u, wt["e_out"], x, sm["e_norm_post"], tm=tx, name="e_out_fwd")
    (q, h1), _ = _norm_matmul(x1, sm["o_norm_pre"], wt["o_in"], tm=tm, tn=W, name="o_in_fwd")
    yy, pooled, gg = _o_mix_fwd(q, wt["o_cw"], sm["o_c_b"], sm["o_c_scale"], tm=tx, name="o_mix_fwd")
    dout, dx2, dyy, lcol, dg_o_post = _out_loss(yy, wt["o_out"], x1, sm["o_norm_post"], tgt, tm=tx, name="o_out_loss")
    loss = (0.5 / D) * jnp.sum(lcol)

    dq, d_cw, d_cb, d_cscale = _o_mix_bwd(dyy, q, gg, pooled, wt["o_cw"], sm["o_c_scale"], tm=tx, name="o_mix_bwd")
    g_o_out, _ = _mm_tn(yy, dout, ts=tm, tn=W, name="o_out_dw")
    ga = dict(o_out=g_o_out, o_cw=d_cw.astype(BF16))
    dh1, ra = _mm_nt(dq, wt["o_in"], tm=tnt, tk=W, name="o_in_bwd", comm=ex.pair(ga) if on else None)
    sa = ex.pair_sums(ga, ra) if on else None
    (dx1, dy0, dg_o_pre, dg_e_post), ra = _pre_bwd_o(dh1, x1, dx2, y0, sm["o_norm_pre"], sm["e_norm_post"],
                                                     tm=tx, name="o_pre_bwd", comm=ex.chips(sa) if on else None)
    g_o_in, _ = _mm_tn(h1, dq, ts=tm, tn=W, name="o_in_dw")
    gb = dict(o_in=g_o_in)
    du, rb = _mm_nt(dy0, wt["e_out"], tm=tnt, tk=W, name="e_out_bwd", comm=ex.pair(gb) if on else None)
    sb = ex.pair_sums(gb, rb) if on else None
    g_e_out, _ = _mm_tn(u, dy0, ts=tm, tn=W, name="e_out_dw")
    gc = dict(e_out=g_e_out)
    (dp, d_wa, d_wb, d_bias, d_lg, d_lb), rbc = _e_mix_bwd(
        du, p, cb, sm["e_a_conv"], sm["e_b_conv"], sm["e_b_ln_g"], sm["e_b_ln_b"], tm=tx, name="e_mix_bwd",
        comm=_merge(ex.chips(sb), ex.pair(gc)) if on else None)
    sc = ex.pair_sums(gc, rbc[1:]) if on else None
    g_e_in, rc = _mm_tn(h0, dp, ts=tm, tn=W, name="e_in_dw", comm=ex.chips(sc) if on else None)
    gd = dict(e_in=g_e_in)
    sd = ex.pair_sums(gd, _run_comm(ex.pair(gd), "pair_e_in")) if on else None
    dh0, rd = _mm_nt(dp, wt["e_in"], tm=tnt, tk=W, name="e_in_bwd", comm=ex.chips(sd) if on else None)
    grad_x, dg_e_pre = _pre_bwd_e(dh0, x, dx1, sm["e_norm_pre"], tm=tx, name="e_pre_bwd")
    if on:
        ex.done(sa, ra)
        ex.done(sb, rbc[:1])
        ex.done(sc, rc)
        ex.done(sd, rd)

    big = {**ga, **gb, **gc, **gd}
    small = dict(e_norm_pre=dg_e_pre, e_norm_post=dg_e_post, e_a_conv=d_wa, e_b_conv=d_wb, e_b_conv_bias=d_bias,
                 e_b_ln_g=d_lg, e_b_ln_b=d_lb, o_norm_pre=dg_o_pre, o_norm_post=dg_o_post, o_c_b=d_cb,
                 o_c_scale=d_cscale)
    return loss, grad_x, big, small


def kernel(x, e_norm_pre, e_norm_post, e_w_in, e_a_conv, e_b_conv, e_b_conv_bias, e_b_ln_g, e_b_ln_b, e_w_out, o_norm_pre, o_norm_post, o_w_in, o_c_w, o_c_b, o_c_scale, o_w_out, loss_target, m_e_norm_pre, m_e_norm_post, m_e_w_in, m_e_a_conv, m_e_b_conv, m_e_b_conv_bias, m_e_b_ln_g, m_e_b_ln_b, m_e_w_out, m_o_norm_pre, m_o_norm_post, m_o_w_in, m_o_c_w, m_o_c_b, m_o_c_scale, m_o_w_out, v_e_norm_pre, v_e_norm_post, v_e_w_in, v_e_a_conv, v_e_b_conv, v_e_b_conv_bias, v_e_b_ln_g, v_e_b_ln_b, v_e_w_out, v_o_norm_pre, v_o_norm_post, v_o_w_in, v_o_c_w, v_o_c_b, v_o_c_scale, v_o_w_out):
    xi, yi, ci = _place()
    me = 4 * xi + 2 * yi + ci
    w_big = dict(e_in=e_w_in[0], e_out=e_w_out[0], o_in=o_w_in[0], o_cw=o_c_w[0], o_out=o_w_out[0])
    m_big = dict(e_in=m_e_w_in[0], e_out=m_e_w_out[0], o_in=m_o_w_in[0], o_cw=m_o_c_w[0], o_out=m_o_w_out[0])
    v_big = dict(e_in=v_e_w_in[0], e_out=v_e_w_out[0], o_in=v_o_w_in[0], o_cw=v_o_c_w[0], o_out=v_o_w_out[0])
    w_small = dict(e_norm_pre=e_norm_pre, e_norm_post=e_norm_post, e_b_conv_bias=e_b_conv_bias, e_b_ln_g=e_b_ln_g,
                   e_b_ln_b=e_b_ln_b, e_a_conv=e_a_conv[0], e_b_conv=e_b_conv[0], o_norm_pre=o_norm_pre,
                   o_norm_post=o_norm_post, o_c_b=o_c_b[0], o_c_scale=o_c_scale)
    m_small = dict(e_norm_pre=m_e_norm_pre, e_norm_post=m_e_norm_post, e_b_conv_bias=m_e_b_conv_bias,
                   e_b_ln_g=m_e_b_ln_g, e_b_ln_b=m_e_b_ln_b, e_a_conv=m_e_a_conv[0], e_b_conv=m_e_b_conv[0],
                   o_norm_pre=m_o_norm_pre, o_norm_post=m_o_norm_post, o_c_b=m_o_c_b[0], o_c_scale=m_o_c_scale)
    v_small = dict(e_norm_pre=v_e_norm_pre, e_norm_post=v_e_norm_post, e_b_conv_bias=v_e_b_conv_bias,
                   e_b_ln_g=v_e_b_ln_g, e_b_ln_b=v_e_b_ln_b, e_a_conv=v_e_a_conv[0], e_b_conv=v_e_b_conv[0],
                   o_norm_pre=v_o_norm_pre, o_norm_post=v_o_norm_post, o_c_b=v_o_c_b[0], o_c_scale=v_o_c_scale)

    gathered = _gather_weights([w_big["e_in"].astype(BF16)], [BIG_AXIS["e_in"]],
                               _pack([w_small[k] for k in SHARDED]))
    wt = dict(e_in=gathered[0])
    per_dev = [_unpack(gathered[-1][d], [w_small[k].shape for k in SHARDED]) for d in range(N_DEV)]
    sm = {k: w_small[k] for k in REPLICATED}
    for j, k in enumerate(SHARDED):
        sm[k] = jnp.concatenate([per_dev[d][j] for d in range(N_DEV)], axis=-1)
    n_groups = sm["o_c_b"].shape[0]
    sm["o_c_b"] = sm["o_c_b"].reshape(1, -1)

    c_idx = jnp.reshape(ci, (1,)).astype(jnp.int32)
    ex = _Exchange({k: w_big[k].astype(BF16) for k in BIG[1:]}, c_idx)
    loss, grad_x, _, g_small = _local_step(x[0], loss_target[0], wt, sm, ex)
    loss = lax.psum(loss, ("x", "y", "c"))

    g_small["o_c_b"] = g_small["o_c_b"].reshape(n_groups, -1)
    full_shapes = [g_small[k].shape for k in SMALL]
    small_parts = _run_comm(_small_comm(_pack([g_small[k] for k in SMALL])), "small_grad_exchange")[0]

    q_idx = jnp.reshape(2 * xi + yi, (1,)).astype(jnp.int32)
    big_out = {k: _adam_big(q_idx, *ex.reduced[k], w_big[k], m_big[k], v_big[k], BIG_SPLIT[k], name="adam_" + k)
               for k in BIG}

    def at_full_size(d):
        out = []
        for k, s in zip(SMALL, full_shapes):
            if k in REPLICATED:
                out.append(d[k])
            else:
                n = d[k].shape[-1]
                out.append(lax.dynamic_update_slice_in_dim(jnp.ones(s, F32), d[k], me * n, axis=-1))
        return _pack(out)

    res_small = _adam_small(small_parts, at_full_size(w_small), at_full_size(m_small), at_full_size(v_small),
                            name="adam_small")
    small_out = {k: [] for k in SMALL}
    for packed in res_small:
        for k, full in zip(SMALL, _unpack(packed, full_shapes)):
            if k in SHARDED:
                n = w_small[k].shape[-1]
                full = lax.dynamic_slice_in_dim(full, me * n, n, axis=-1)
            small_out[k].append(full)

    big_of = dict(e_w_in="e_in", e_w_out="e_out", o_w_in="o_in", o_c_w="o_cw", o_w_out="o_out")
    stacked = ("e_a_conv", "e_b_conv", "o_c_b")

    def leaf(name, which):
        if name in big_of:
            return big_out[big_of[name]][which][None]
        t = small_out[name][which]
        return t[None] if name in stacked else t

    order = ("e_norm_pre", "e_norm_post", "e_w_in", "e_a_conv", "e_b_conv", "e_b_conv_bias", "e_b_ln_g", "e_b_ln_b",
             "e_w_out", "o_norm_pre", "o_norm_post", "o_w_in", "o_c_w", "o_c_b", "o_c_scale", "o_w_out")
    outs = [loss, grad_x[None]]
    for which in range(4):
        outs += [leaf(nm, which) for nm in order]
    return tuple(outs)
```

```python
import jax
import jax.numpy as jnp
from jax import lax
from jax.experimental import pallas as pl
from jax.experimental.pallas import tpu as pltpu

F32 = jnp.float32
BF16 = jnp.bfloat16
EPS = 1e-6
MESH = pl.DeviceIdType.MESH
ANY = pl.BlockSpec(memory_space=pl.ANY)

N_DEV = 8
HALO = 32
PHALO = 16
CONV_A = 3
CONV_B = 31
POOL_WINDOWS = (2, 4, 8, 16)
LANES = 128
MIB = 1024 * 1024

ADAM_LR = 0.001
ADAM_B1 = 0.9
ADAM_B2 = 0.999
ADAM_EPS = 1e-08
ADAM_WD = 0.01
ADAM_STEP = 10

TM_MM = 512
TM_NT = 1024
TM_MIX = 256


def _sds(shape, dtype):
    return jax.ShapeDtypeStruct(tuple(shape), dtype)


def _params(sem, vmem_mib):
    return pltpu.CompilerParams(dimension_semantics=sem, vmem_limit_bytes=vmem_mib * MIB)


def _const(shape, single=False):
    n = len(shape)
    if single:
        return pl.BlockSpec(shape, lambda *_: (0,) * n, pipeline_mode=pl.Buffered(1))
    return pl.BlockSpec(shape, lambda *_: (0,) * n)


def _sig(v):
    return jax.nn.sigmoid(v)


def _dsilu(v, s):
    return s * (1.0 + v * (1.0 - s))


def _rms(v):
    return lax.rsqrt(jnp.mean(v * v, axis=-1, keepdims=True) + EPS)


def _norm_bwd(dn, n, r):
    return r * (dn - n * jnp.mean(dn * n, axis=-1, keepdims=True))


def _colsum(v):
    return jnp.sum(v, axis=0, keepdims=True)


class _Comm:
    def __init__(self, inputs, out_shapes, sems, start, finish, aliases=None):
        self.inputs, self.out_shapes, self.sems = list(inputs), list(out_shapes), list(sems)
        self.start, self.finish = start, finish
        self.aliases = dict(aliases or {})


def _merge(*comms):
    comms = [c for c in comms if c is not None]
    if len(comms) <= 1:
        return comms[0] if comms else None
    spans, i0, o0, s0, aliases = [], 0, 0, 0, {}
    for c in comms:
        spans.append((i0, o0, s0))
        aliases.update({i0 + k: o0 + v for k, v in c.aliases.items()})
        i0, o0, s0 = i0 + len(c.inputs), o0 + len(c.out_shapes), s0 + len(c.sems)

    def run(which):
        def fn(ins, outs, sems):
            for c, (i, o, s) in zip(comms, spans):
                getattr(c, which)(ins[i:i + len(c.inputs)], outs[o:o + len(c.out_shapes)], sems[s:s + len(c.sems)])
        return fn

    return _Comm([a for c in comms for a in c.inputs], [a for c in comms for a in c.out_shapes],
                 [a for c in comms for a in c.sems], run("start"), run("finish"), aliases)


def _call(body, *, grid, in_specs, out_specs, out_shape, operands, name, params, scratch_shapes=(), comm=None):
    n_i, n_o, n_s = len(in_specs), len(out_specs), len(scratch_shapes)
    if comm is None:
        res = pl.pallas_call(body, grid=grid, in_specs=list(in_specs), out_specs=list(out_specs),
                             out_shape=list(out_shape), scratch_shapes=list(scratch_shapes),
                             name=name, compiler_params=params)(*operands)
        return list(res), []
    c_i, c_o = len(comm.inputs), len(comm.out_shapes)

    def carrier(*refs):
        ins, cins = refs[:n_i], refs[n_i:n_i + c_i]
        outs = refs[n_i + c_i:n_i + c_i + n_o]
        couts = refs[n_i + c_i + n_o:n_i + c_i + n_o + c_o]
        scr = refs[n_i + c_i + n_o + c_o:n_i + c_i + n_o + c_o + n_s]
        csems = refs[n_i + c_i + n_o + c_o + n_s:]
        ids = [pl.program_id(d) for d in range(len(grid))]
        first = ids[0] == 0
        last = ids[0] == grid[0] - 1
        for d in range(1, len(grid)):
            first = first & (ids[d] == 0)
            last = last & (ids[d] == grid[d] - 1)

        @pl.when(first)
        def _():
            comm.start(cins, couts, csems)

        body(*ins, *outs, *scr)

        @pl.when(last)
        def _():
            comm.finish(cins, couts, csems)

    res = pl.pallas_call(
        carrier, grid=grid, in_specs=list(in_specs) + [ANY] * c_i, out_specs=list(out_specs) + [ANY] * c_o,
        out_shape=list(out_shape) + comm.out_shapes, scratch_shapes=list(scratch_shapes) + comm.sems,
        input_output_aliases={n_i + k: n_o + v for k, v in comm.aliases.items()},
        name=name, compiler_params=params)(*operands, *comm.inputs)
    return list(res[:n_o]), list(res[n_o:])


def _run_comm(comm, name):
    c_i, c_o = len(comm.inputs), len(comm.out_shapes)

    def body(*refs):
        ins, outs, sems = refs[:c_i], refs[c_i:c_i + c_o], refs[c_i + c_o:]
        comm.start(ins, outs, sems)
        comm.finish(ins, outs, sems)

    res = pl.pallas_call(
        body, in_specs=[ANY] * c_i, out_specs=[ANY] * c_o, out_shape=comm.out_shapes, scratch_shapes=comm.sems,
        input_output_aliases=comm.aliases, name=name)(*comm.inputs)
    return list(res)


def _norm_matmul(x, g, w, *, tm, tn, name, comm=None):
    S, D = x.shape
    N = w.shape[1]

    def body(x_ref, g_ref, w_ref, p_ref, h_ref):
        @pl.when(pl.program_id(1) == 0)
        def _():
            xx = x_ref[...]
            h_ref[...] = ((xx * _rms(xx)) * g_ref[...]).astype(BF16)

        p_ref[...] = jnp.dot(h_ref[...], w_ref[...], preferred_element_type=F32).astype(BF16)

    return _call(
        body, grid=(S // tm, N // tn),
        in_specs=[pl.BlockSpec((tm, D), lambda i, j: (i, 0)), _const((1, D)),
                  pl.BlockSpec((D, tn), lambda i, j: (0, j))],
        out_specs=[pl.BlockSpec((tm, tn), lambda i, j: (i, j)), pl.BlockSpec((tm, D), lambda i, j: (i, 0))],
        out_shape=[_sds((S, N), BF16), _sds((S, D), BF16)], operands=(x, g, w),
        name=name, params=_params(("arbitrary", "arbitrary"), 48), comm=comm)


def _out_norm_res(u, w, x, g, *, tm, name):
    S, K = u.shape
    D = w.shape[1]

    def body(u_ref, w_ref, x_ref, g_ref, x1_ref, y_ref):
        y = jnp.dot(u_ref[...], w_ref[...], preferred_element_type=F32)
        y_ref[...] = y.astype(BF16)
        x1_ref[...] = x_ref[...] + (y * _rms(y)) * g_ref[...]

    return pl.pallas_call(
        body, grid=(S // tm,),
        in_specs=[pl.BlockSpec((tm, K), lambda i: (i, 0)), _const((K, D), single=True),
                  pl.BlockSpec((tm, D), lambda i: (i, 0)), _const((1, D))],
        out_specs=[pl.BlockSpec((tm, D), lambda i: (i, 0)), pl.BlockSpec((tm, D), lambda i: (i, 0))],
        out_shape=[_sds((S, D), F32), _sds((S, D), BF16)],
        name=name, compiler_params=_params(("arbitrary",), 48),
    )(u, w, x, g)


def _out_loss(yy, w, x1, g, tgt, *, tm, name):
    S, K = yy.shape
    D = w.shape[1]

    def body(yy_ref, w_ref, x1_ref, g_ref, t_ref, dout_ref, dx2_ref, dyy_ref, lcol_ref, dg_ref):
        out = jnp.dot(yy_ref[...], w_ref[...], preferred_element_type=F32)
        r = _rms(out)
        n = out * r
        gg = g_ref[...]
        e = x1_ref[...] + n * gg - t_ref[...]
        dx2 = e * (1.0 / D)
        dx2_ref[...] = dx2
        dout = _norm_bwd(dx2 * gg, n, r).astype(BF16)
        dout_ref[...] = dout
        dyy_ref[...] = lax.dot_general(dout, w_ref[...], (((1,), (1,)), ((), ())),
                                       preferred_element_type=F32).astype(BF16)

        @pl.when(pl.program_id(0) == 0)
        def _():
            lcol_ref[...] = jnp.zeros_like(lcol_ref)
            dg_ref[...] = jnp.zeros_like(dg_ref)

        lcol_ref[...] += _colsum(e * e)
        dg_ref[...] += _colsum(dx2 * n)

    return pl.pallas_call(
        body, grid=(S // tm,),
        in_specs=[pl.BlockSpec((tm, K), lambda i: (i, 0)), _const((K, D), single=True),
                  pl.BlockSpec((tm, D), lambda i: (i, 0)), _const((1, D)),
                  pl.BlockSpec((tm, D), lambda i: (i, 0))],
        out_specs=[pl.BlockSpec((tm, D), lambda i: (i, 0)), pl.BlockSpec((tm, D), lambda i: (i, 0)),
                   pl.BlockSpec((tm, K), lambda i: (i, 0)), _const((1, D)), _const((1, D))],
        out_shape=[_sds((S, D), BF16), _sds((S, D), F32), _sds((S, K), BF16), _sds((1, D), F32), _sds((1, D), F32)],
        name=name, compiler_params=_params(("arbitrary",), 52),
    )(yy, w, x1, g, tgt)


def _mm_nt(a, w, *, tm, tk, name, comm=None):
    S, N = a.shape
    D = w.shape[0]
    n_k = N // tk

    def body(a_ref, w_ref, o_ref, acc_ref):
        k = pl.program_id(1)
        part = lax.dot_general(a_ref[...], w_ref[...], (((1,), (1,)), ((), ())), preferred_element_type=F32)

        @pl.when(k == 0)
        def _():
            acc_ref[...] = part

        @pl.when(k > 0)
        def _():
            acc_ref[...] += part

        @pl.when(k == n_k - 1)
        def _():
            o_ref[...] = acc_ref[...].astype(BF16)

    outs, extra = _call(
        body, grid=(S // tm, n_k),
        in_specs=[pl.BlockSpec((tm, tk), lambda i, k: (i, k)), pl.BlockSpec((D, tk), lambda i, k: (0, k))],
        out_specs=[pl.BlockSpec((tm, D), lambda i, k: (i, 0))],
        out_shape=[_sds((S, D), BF16)], operands=(a, w),
        scratch_shapes=[pltpu.VMEM((tm, D), F32)],
        name=name, params=_params(("arbitrary", "arbitrary"), 48), comm=comm)
    return outs[0], extra


def _mm_tn(a, b, *, ts, tn, name, comm=None):
    S, M = a.shape
    N = b.shape[1]
    n_s = S // ts

    def body(a_ref, b_ref, o_ref, acc_ref):
        s = pl.program_id(1)
        part = lax.dot_general(a_ref[...], b_ref[...], (((0,), (0,)), ((), ())), preferred_element_type=F32)

        @pl.when(s == 0)
        def _():
            acc_ref[...] = part

        @pl.when(s > 0)
        def _():
            acc_ref[...] += part

        @pl.when(s == n_s - 1)
        def _():
            o_ref[...] = acc_ref[...].astype(BF16)

    outs, extra = _call(
        body, grid=(N // tn, n_s),
        in_specs=[pl.BlockSpec((ts, M), lambda j, s: (s, 0)), pl.BlockSpec((ts, tn), lambda j, s: (s, j))],
        out_specs=[pl.BlockSpec((M, tn), lambda j, s: (0, j))],
        out_shape=[_sds((M, N), BF16)], operands=(a, b),
        scratch_shapes=[pltpu.VMEM((M, tn), F32)],
        name=name, params=_params(("arbitrary", "arbitrary"), 48), comm=comm)
    return outs[0], extra


def _pre_bwd_o(dh, x1, dx2, y0, g_pre, g_post, *, tm, name, comm=None):
    S, D = x1.shape

    def body(dh_ref, x1_ref, dx2_ref, y0_ref, gpre_ref, gpost_ref, dx1_ref, dy0_ref, dgpre_ref, dgpost_ref):
        @pl.when(pl.program_id(0) == 0)
        def _():
            dgpre_ref[...] = jnp.zeros_like(dgpre_ref)
            dgpost_ref[...] = jnp.zeros_like(dgpost_ref)

        dh = dh_ref[...].astype(F32)
        x1 = x1_ref[...]
        r2 = _rms(x1)
        xn = x1 * r2
        dgpre_ref[...] += _colsum(dh * xn)
        dx1 = dx2_ref[...] + _norm_bwd(dh * gpre_ref[...], xn, r2)
        dx1_ref[...] = dx1
        y = y0_ref[...].astype(F32)
        r1 = _rms(y)
        n1 = y * r1
        dgpost_ref[...] += _colsum(dx1 * n1)
        dy0_ref[...] = _norm_bwd(dx1 * gpost_ref[...], n1, r1).astype(BF16)

    row = pl.BlockSpec((tm, D), lambda i: (i, 0))
    return _call(
        body, grid=(S // tm,),
        in_specs=[row, row, row, row, _const((1, D)), _const((1, D))],
        out_specs=[row, row, _const((1, D)), _const((1, D))],
        out_shape=[_sds((S, D), F32), _sds((S, D), BF16), _sds((1, D), F32), _sds((1, D), F32)],
        operands=(dh, x1, dx2, y0, g_pre, g_post),
        name=name, params=_params(("arbitrary",), 48), comm=comm)


def _pre_bwd_e(dh, x, dx1, g_pre, *, tm, name):
    S, D = x.shape

    def body(dh_ref, x_ref, dx1_ref, gpre_ref, gx_ref, dgpre_ref):
        @pl.when(pl.program_id(0) == 0)
        def _():
            dgpre_ref[...] = jnp.zeros_like(dgpre_ref)

        dh = dh_ref[...].astype(F32)
        xx = x_ref[...]
        r0 = _rms(xx)
        xn = xx * r0
        dgpre_ref[...] += _colsum(dh * xn)
        gx_ref[...] = dx1_ref[...] + _norm_bwd(dh * gpre_ref[...], xn, r0)

    row = pl.BlockSpec((tm, D), lambda i: (i, 0))
    return pl.pallas_call(
        body, grid=(S // tm,),
        in_specs=[row, row, row, _const((1, D))],
        out_specs=[row, _const((1, D))],
        out_shape=[_sds((S, D), F32), _sds((1, D), F32)],
        name=name, compiler_params=_params(("arbitrary",), 48),
    )(dh, x, dx1, g_pre)


SUBLANES = 8


def _shift_copies(sh_ref, ext_ref, cs):
    for b in range(1, SUBLANES):
        sh_ref[b - 1] = ext_ref[pl.ds(b, sh_ref.shape[1]), cs]


def _rows_at(ext_ref, sh_ref, off, cs, tm):
    b = off % SUBLANES
    if b == 0 or sh_ref is None:
        return ext_ref[pl.ds(off, tm), cs]
    return sh_ref[b - 1, pl.ds(off - b, tm), :]


def _taps(ext_ref, w_ref, n_taps, base, cs, tm, sh_ref=None):
    acc = _rows_at(ext_ref, sh_ref, base, cs, tm) * w_ref[0:1, cs]
    for k in range(1, n_taps):
        acc = acc + _rows_at(ext_ref, sh_ref, base + k, cs, tm) * w_ref[k:k + 1, cs]
    return acc


def _taps_rev(ext_ref, w_ref, n_taps, cs, tm, sh_ref=None):
    acc = _rows_at(ext_ref, sh_ref, n_taps - 1, cs, tm) * w_ref[0:1, cs]
    for k in range(1, n_taps):
        acc = acc + _rows_at(ext_ref, sh_ref, n_taps - 1 - k, cs, tm) * w_ref[k:k + 1, cs]
    return acc


def _e_mix_fwd(p, wa, wb, bias, ln_g, ln_b, *, tm, name, comm=None):
    S = p.shape[0]
    W = p.shape[1] // 7
    nb = tm // HALO
    chunks = [slice(c * LANES, (c + 1) * LANES) for c in range(W // LANES)]

    def body(p_ref, hax_ref, hac_ref, hbv_ref, hbg_ref, wa_ref, wb_ref, bias_ref, lg_ref, lb_ref,
             u_ref, cb_ref, ext_ref, sh_ref):
        keep = (pl.program_id(0) > 0).astype(F32)
        col = lambda j, cs: p_ref[:, j * W + cs.start:j * W + cs.stop].astype(F32)

        ext_ref[0:HALO, :] = hax_ref[...].astype(F32) * hac_ref[...].astype(F32) * keep
        ext_ref[HALO:, :] = p_ref[:, 2 * W:3 * W].astype(F32) * p_ref[:, 0:W].astype(F32)
        for cs in chunks:
            conv = _taps(ext_ref, wa_ref, CONV_A, HALO - (CONV_A - 1), cs, tm)
            az = col(3, cs)
            u_ref[:, cs] = (col(1, cs) * conv * (az * _sig(az))).astype(BF16)

        ext_ref[0:HALO, :] = hbv_ref[...].astype(F32) * _sig(hbg_ref[...].astype(F32)) * keep
        ext_ref[HALO:, :] = p_ref[:, 4 * W:5 * W].astype(F32) * _sig(p_ref[:, 5 * W:6 * W].astype(F32))
        s1 = jnp.zeros((tm, LANES), F32)
        for cs in chunks:
            _shift_copies(sh_ref, ext_ref, cs)
            cb = _taps(ext_ref, wb_ref, CONV_B, HALO - (CONV_B - 1), cs, tm, sh_ref) + bias_ref[:, cs]
            cb_ref[:, cs] = cb
            s1 = s1 + cb
        mu = jnp.sum(s1, axis=-1, keepdims=True) * (1.0 / W)
        s2 = jnp.zeros((tm, LANES), F32)
        for cs in chunks:
            xc = cb_ref[:, cs] - mu
            s2 = s2 + xc * xc
        rs = lax.rsqrt(jnp.sum(s2, axis=-1, keepdims=True) * (1.0 / W) + EPS)
        for cs in chunks:
            lb = (cb_ref[:, cs] - mu) * rs * lg_ref[:, cs] + lb_ref[:, cs]
            bz = col(6, cs)
            u_ref[:, W + cs.start:W + cs.stop] = (lb * _sig(lb) * (bz * _sig(bz))).astype(BF16)

    prev = lambda j: pl.BlockSpec((HALO, W), lambda i: (jnp.maximum(i * nb - 1, 0), j))
    return _call(
        body, grid=(S // tm,),
        in_specs=[pl.BlockSpec((tm, 7 * W), lambda i: (i, 0)), prev(0), prev(2), prev(4), prev(5),
                  _const((CONV_A, W)), _const((CONV_B, W)), _const((1, W)), _const((1, W)), _const((1, W))],
        out_specs=[pl.BlockSpec((tm, 2 * W), lambda i: (i, 0)), pl.BlockSpec((tm, W), lambda i: (i, 0))],
        out_shape=[_sds((S, 2 * W), BF16), _sds((S, W), F32)],
        operands=(p, p, p, p, p, wa, wb, bias, ln_g, ln_b),
        scratch_shapes=[pltpu.VMEM((HALO + tm, W), F32),
                        pltpu.VMEM((SUBLANES - 1, HALO + tm - SUBLANES, LANES), F32)],
        name=name, params=_params(("arbitrary",), 48), comm=comm)


def _e_mix_bwd(du, p, cb, wa, wb, ln_g, ln_b, *, tm, name, comm=None):
    S = p.shape[0]
    W = p.shape[1] // 7
    nb = tm // HALO
    n_t = S // tm
    last_blk = S // HALO - 1
    chunks = [slice(c * LANES, (c + 1) * LANES) for c in range(W // LANES)]

    def body(du_ref, duf_ref, p_ref, fab_ref, faz_ref, fbz_ref, hax_ref, hac_ref, hbv_ref, hbg_ref,
             cb_ref, cbf_ref, wa_ref, wb_ref, lg_ref, lb_ref,
             dp_ref, dwa_ref, dwb_ref, dbias_ref, dlg_ref, dlb_ref, extd_ref, extg_ref, shd_ref, shg_ref):
        i = pl.program_id(0)
        keep_prev = (i > 0).astype(F32)
        keep_next = (i < n_t - 1).astype(F32)
        col = lambda j, cs: p_ref[:, j * W + cs.start:j * W + cs.stop].astype(F32)

        @pl.when(i == 0)
        def _():
            dwa_ref[...] = jnp.zeros_like(dwa_ref)
            dwb_ref[...] = jnp.zeros_like(dwb_ref)
            dbias_ref[...] = jnp.zeros_like(dbias_ref)
            dlg_ref[...] = jnp.zeros_like(dlg_ref)
            dlb_ref[...] = jnp.zeros_like(dlb_ref)

        def dcb_rows(rows, cb_rows_ref, dub, bz_of, dst0, scale, main):
            cbv = cb_rows_ref[...]
            mu = jnp.mean(cbv, axis=-1, keepdims=True)
            xc = cbv - mu
            rs = lax.rsqrt(jnp.mean(xc * xc, axis=-1, keepdims=True) + EPS)
            m1 = jnp.zeros((rows, LANES), F32)
            m2 = jnp.zeros((rows, LANES), F32)
            for cs in chunks:
                nbv = (cb_rows_ref[:, cs] - mu) * rs
                lb = nbv * lg_ref[:, cs] + lb_ref[:, cs]
                sl = _sig(lb)
                bz = bz_of(cs)
                sz = _sig(bz)
                dub_c = dub(cs)
                dlb = dub_c * (bz * sz) * _dsilu(lb, sl)
                if main:
                    dlg_ref[:, cs] += _colsum(dlb * nbv)
                    dlb_ref[:, cs] += _colsum(dlb)
                    dp_ref[:, 6 * W + cs.start:6 * W + cs.stop] = (dub_c * (lb * sl) * _dsilu(bz, sz)).astype(BF16)
                dnb = dlb * lg_ref[:, cs]
                extd_ref[dst0:dst0 + rows, cs] = dnb
                m1 = m1 + dnb
                m2 = m2 + dnb * nbv
            m1 = jnp.sum(m1, axis=-1, keepdims=True) * (1.0 / W)
            m2 = jnp.sum(m2, axis=-1, keepdims=True) * (1.0 / W)
            for cs in chunks:
                nbv = (cb_rows_ref[:, cs] - mu) * rs
                dcb = rs * (extd_ref[dst0:dst0 + rows, cs] - m1 - nbv * m2) * scale
                extd_ref[dst0:dst0 + rows, cs] = dcb
                if main:
                    dbias_ref[:, cs] += _colsum(dcb)

        dcb_rows(tm, cb_ref, lambda cs: du_ref[:, W + cs.start:W + cs.stop].astype(F32),
                 lambda cs: col(6, cs), 0, 1.0, True)
        dcb_rows(HALO, cbf_ref, lambda cs: duf_ref[:, W + cs.start:W + cs.stop].astype(F32),
                 lambda cs: fbz_ref[:, cs].astype(F32), tm, keep_next, False)

        extg_ref[0:HALO, :] = hbv_ref[...].astype(F32) * _sig(hbg_ref[...].astype(F32)) * keep_prev
        extg_ref[HALO:, :] = p_ref[:, 4 * W:5 * W].astype(F32) * _sig(p_ref[:, 5 * W:6 * W].astype(F32))
        base_b = HALO - (CONV_B - 1)
        for cs in chunks:
            _shift_copies(shd_ref, extd_ref, cs)
            _shift_copies(shg_ref, extg_ref, cs)
            dgb = _taps_rev(extd_ref, wb_ref, CONV_B, cs, tm, shd_ref)
            bv = col(4, cs)
            sg = _sig(col(5, cs))
            dp_ref[:, 4 * W + cs.start:4 * W + cs.stop] = (dgb * sg).astype(BF16)
            dp_ref[:, 5 * W + cs.start:5 * W + cs.stop] = (dgb * bv * sg * (1.0 - sg)).astype(BF16)
            dcb = extd_ref[0:tm, cs]
            for k in range(CONV_B):
                dwb_ref[k:k + 1, cs] += _colsum(dcb * _rows_at(extg_ref, shg_ref, base_b + k, cs, tm))

        extg_ref[0:HALO, :] = hax_ref[...].astype(F32) * hac_ref[...].astype(F32) * keep_prev
        extg_ref[HALO:, :] = p_ref[:, 2 * W:3 * W].astype(F32) * p_ref[:, 0:W].astype(F32)
        base_a = HALO - (CONV_A - 1)
        for cs in chunks:
            conv = _taps(extg_ref, wa_ref, CONV_A, base_a, cs, tm)
            az = col(3, cs)
            sz = _sig(az)
            ab = col(1, cs)
            dua = du_ref[:, cs].astype(F32)
            dya = dua * (az * sz)
            dp_ref[:, W + cs.start:W + cs.stop] = (dya * conv).astype(BF16)
            dp_ref[:, 3 * W + cs.start:3 * W + cs.stop] = (dua * (ab * conv) * _dsilu(az, sz)).astype(BF16)
            extd_ref[0:tm, cs] = dya * ab
            azf = faz_ref[:, cs].astype(F32)
            extd_ref[tm:tm + HALO, cs] = (duf_ref[:, cs].astype(F32) * (azf * _sig(azf))
                                          * fab_ref[:, cs].astype(F32) * keep_next)
        for cs in chunks:
            dca = _taps_rev(extd_ref, wa_ref, CONV_A, cs, tm)
            dp_ref[:, cs] = (dca * col(2, cs)).astype(BF16)
            dp_ref[:, 2 * W + cs.start:2 * W + cs.stop] = (dca * col(0, cs)).astype(BF16)
            dconv = extd_ref[0:tm, cs]
            for k in range(CONV_A):
                dwa_ref[k:k + 1, cs] += _colsum(dconv * extg_ref[pl.ds(base_a + k, tm), cs])

    prev = lambda j: pl.BlockSpec((HALO, W), lambda i: (jnp.maximum(i * nb - 1, 0), j))
    nxt = lambda j, w: pl.BlockSpec((HALO, w), lambda i: (jnp.minimum((i + 1) * nb, last_blk), j))
    row = lambda w: pl.BlockSpec((tm, w), lambda i: (i, 0))
    return _call(
        body, grid=(n_t,),
        in_specs=[row(2 * W), nxt(0, 2 * W), row(7 * W), nxt(1, W), nxt(3, W), nxt(6, W),
                  prev(0), prev(2), prev(4), prev(5), row(W), nxt(0, W),
                  _const((CONV_A, W)), _const((CONV_B, W)), _const((1, W)), _const((1, W))],
        out_specs=[row(7 * W), _const((CONV_A, W)), _const((CONV_B, W)), _const((1, W)), _const((1, W)), _const((1, W))],
        out_shape=[_sds((S, 7 * W), BF16), _sds((CONV_A, W), F32), _sds((CONV_B, W), F32),
                   _sds((1, W), F32), _sds((1, W), F32), _sds((1, W), F32)],
        operands=(du, du, p, p, p, p, p, p, p, p, cb, cb, wa, wb, ln_g, ln_b),
        scratch_shapes=[pltpu.VMEM((tm + HALO, W), F32), pltpu.VMEM((HALO + tm, W), F32),
                        pltpu.VMEM((SUBLANES - 1, HALO + tm - SUBLANES, LANES), F32),
                        pltpu.VMEM((SUBLANES - 1, HALO + tm - SUBLANES, LANES), F32)],
        name=name, params=_params(("arbitrary",), 52), comm=comm)


def _counts(i, tm, rows, off, win):
    t = i * tm + off + lax.broadcasted_iota(jnp.int32, (rows, 1), 0)
    return jnp.minimum(t + 1, win).astype(F32)


def _o_mix_fwd(q, cw, cb, cscale, *, tm, name):
    S = q.shape[0]
    WC = q.shape[1] // 2
    NG = len(POOL_WINDOWS)
    G = WC // NG
    nb = tm // PHALO

    def body(v_ref, z_ref, hv_ref, cw_ref, cb_ref, sc_ref, yy_ref, pooled_ref, gg_ref, ext_ref):
        i = pl.program_id(0)
        keep = (i > 0).astype(F32)
        for g, win in enumerate(POOL_WINDOWS):
            cs = slice(g * G, (g + 1) * G)
            v = v_ref[:, cs].astype(F32)
            ext_ref[0:PHALO, :] = hv_ref[:, cs].astype(F32) * keep
            ext_ref[PHALO:, :] = v
            s = v
            for j in range(1, win):
                s = s + ext_ref[pl.ds(PHALO - j, tm), :]
            pooled = (s / _counts(i, tm, tm, 0, win) - v).astype(BF16)
            pooled_ref[:, cs] = pooled
            gg = jnp.dot(pooled, cw_ref[g], preferred_element_type=F32) + cb_ref[:, cs]
            gg_ref[:, cs] = gg.astype(BF16)
            z = z_ref[:, cs].astype(F32)
            yy_ref[:, cs] = (gg * sc_ref[:, cs] * (z * _sig(z))).astype(BF16)

    row = lambda j: pl.BlockSpec((tm, WC), lambda i: (i, j))
    out = pl.BlockSpec((tm, WC), lambda i: (i, 0))
    return pl.pallas_call(
        body, grid=(S // tm,),
        in_specs=[row(0), row(1), pl.BlockSpec((PHALO, WC), lambda i: (jnp.maximum(i * nb - 1, 0), 0)),
                  _const((NG, G, G)), _const((1, WC)), _const((1, WC))],
        out_specs=[out, out, out],
        out_shape=[_sds((S, WC), BF16)] * 3,
        scratch_shapes=[pltpu.VMEM((PHALO + tm, G), F32)],
        name=name, compiler_params=_params(("arbitrary",), 40),
    )(q, q, q, cw, cb, cscale)


def _o_mix_bwd(dyy, q, gg, pooled, cw, cscale, *, tm, name):
    S = q.shape[0]
    WC = q.shape[1] // 2
    NG = len(POOL_WINDOWS)
    G = WC // NG
    nb = tm // PHALO
    n_t = S // tm
    last_blk = S // PHALO - 1
    nt = (((1,), (1,)), ((), ()))
    tn = (((0,), (0,)), ((), ()))

    def body(dyy_ref, dyyf_ref, z_ref, zf_ref, gg_ref, pooled_ref, cw_ref, sc_ref,
             dq_ref, dcw_ref, dcb_ref, dsc_ref, ext_ref):
        i = pl.program_id(0)
        keep_next = (i < n_t - 1).astype(F32)

        @pl.when(i == 0)
        def _():
            dcw_ref[...] = jnp.zeros_like(dcw_ref)
            dcb_ref[...] = jnp.zeros_like(dcb_ref)
            dsc_ref[...] = jnp.zeros_like(dsc_ref)

        for g, win in enumerate(POOL_WINDOWS):
            cs = slice(g * G, (g + 1) * G)
            sc = sc_ref[:, cs]
            z = z_ref[:, cs].astype(F32)
            sz = _sig(z)
            dyy_c = dyy_ref[:, cs].astype(F32)
            ggv = gg_ref[:, cs].astype(F32)
            dyy0 = dyy_c * (z * sz)
            dq_ref[:, WC + cs.start:WC + cs.stop] = (dyy_c * (ggv * sc) * _dsilu(z, sz)).astype(BF16)
            dgg = dyy0 * sc
            dsc_ref[:, cs] += _colsum(dyy0 * ggv)
            dcb_ref[:, cs] += _colsum(dgg)
            dgg_b = dgg.astype(BF16)
            dcw_ref[g] += lax.dot_general(pooled_ref[:, cs], dgg_b, tn, preferred_element_type=F32)
            dpool = lax.dot_general(dgg_b, cw_ref[g], nt, preferred_element_type=F32)
            zf = zf_ref[:, cs].astype(F32)
            dgg_f = (dyyf_ref[:, cs].astype(F32) * (zf * _sig(zf)) * sc * keep_next).astype(BF16)
            dpool_f = lax.dot_general(dgg_f, cw_ref[g], nt, preferred_element_type=F32)
            ext_ref[0:tm, :] = dpool / _counts(i, tm, tm, 0, win)
            ext_ref[tm:tm + PHALO, :] = dpool_f / _counts(i, tm, PHALO, tm, win)
            dv = ext_ref[0:tm, :] - dpool
            for j in range(1, win):
                dv = dv + ext_ref[pl.ds(j, tm), :]
            dq_ref[:, cs] = dv.astype(BF16)

    row = lambda: pl.BlockSpec((tm, WC), lambda i: (i, 0))
    nxt = lambda j: pl.BlockSpec((PHALO, WC), lambda i: (jnp.minimum((i + 1) * nb, last_blk), j))
    return pl.pallas_call(
        body, grid=(n_t,),
        in_specs=[row(), nxt(0), pl.BlockSpec((tm, WC), lambda i: (i, 1)), nxt(1), row(), row(),
                  _const((NG, G, G)), _const((1, WC))],
        out_specs=[pl.BlockSpec((tm, 2 * WC), lambda i: (i, 0)), _const((NG, G, G)), _const((1, WC)), _const((1, WC))],
        out_shape=[_sds((S, 2 * WC), BF16), _sds((NG, G, G), F32), _sds((1, WC), F32), _sds((1, WC), F32)],
        scratch_shapes=[pltpu.VMEM((tm + PHALO, G), F32)],
        name=name, compiler_params=_params(("arbitrary",), 48),
    )(dyy, dyy, q, q, gg, pooled, cw, cscale)


def _place():
    return lax.axis_index("x"), lax.axis_index("y"), lax.axis_index("c")


def _piece(ref, axis, size, index):
    start = index * size
    if axis == len(ref.shape) - 1:
        start = pl.multiple_of(start, LANES)
    idx = [slice(None)] * len(ref.shape)
    idx[axis] = pl.ds(start, size)
    return ref.at[tuple(idx)]


def _gather_weights(shards, axes, small):
    n = len(shards) + 1
    sizes = [s.shape[a] for s, a in zip(shards, axes)]
    full = [_sds(s.shape[:a] + (N_DEV * s.shape[a],) + s.shape[a + 1:], s.dtype) for s, a in zip(shards, axes)]
    full.append(_sds((N_DEV,) + small.shape, small.dtype))

    def body(*refs):
        ins, outs = refs[:n], refs[n:2 * n]
        send_sems, recv_sems, local_sems = refs[2 * n:]
        x, y, c = _place()
        me, sibling = (x, y, c), (x, y, 1 - c)
        chips = [(1 - x, y), (x, 1 - y), (1 - x, 1 - y)]

        def block(t, owner):
            d = 4 * owner[0] + 2 * owner[1] + owner[2]
            return outs[t].at[d] if t == n - 1 else _piece(outs[t], axes[t], sizes[t], d)

        def copy(t, k, owner, to, src=None):
            blk = block(t, owner)
            return pltpu.make_async_remote_copy(
                src_ref=blk if src is None else src, dst_ref=blk,
                send_sem=send_sems.at[7 * t + k], recv_sem=recv_sems.at[7 * t + k],
                device_id=to, device_id_type=MESH)

        mine = [pltpu.make_async_copy(ins[t], block(t, me), local_sems.at[t]) for t in range(n)]
        first = []
        for t in range(n):
            mine[t].start()
            first.append(copy(t, 0, me, sibling, src=ins[t]))
            first += [copy(t, 1 + j, me, (*chip, c), src=ins[t]) for j, chip in enumerate(chips)]
        for cp in first:
            cp.start()
        passed = []
        for j, chip in enumerate(chips):
            for t in range(n):
                copy(t, 1 + j, (*chip, c), me).wait_recv()
                fwd = copy(t, 4 + j, (*chip, c), sibling)
                fwd.start()
                passed.append(fwd)
        for t in range(n):
            copy(t, 0, sibling, me).wait_recv()
        for j, chip in enumerate(chips):
            for t in range(n):
                copy(t, 4 + j, (*chip, 1 - c), me).wait_recv()
        for cp in first + passed:
            cp.wait_send()
        for cp in mine:
            cp.wait()

    return pl.pallas_call(
        body, out_shape=full, in_specs=[ANY] * n, out_specs=[ANY] * n,
        scratch_shapes=[pltpu.SemaphoreType.DMA((7 * n,)), pltpu.SemaphoreType.DMA((7 * n,)),
                        pltpu.SemaphoreType.DMA((n,))],
        name="gather_weights",
    )(*shards, small)


def _gather_stage1(shards, axes):
    n = len(shards)
    sizes = [s.shape[a] for s, a in zip(shards, axes)]
    full = [_sds(s.shape[:a] + (N_DEV * s.shape[a],) + s.shape[a + 1:], s.dtype) for s, a in zip(shards, axes)]

    def copies(ins, outs, sems):
        send_sems, recv_sems, local_sems = sems
        x, y, c = _place()
        peers = [(x, y, 1 - c), (1 - x, y, c), (x, 1 - y, c), (1 - x, 1 - y, c)]
        out = []
        for t in range(n):
            blk = _piece(outs[t], axes[t], sizes[t], 4 * x + 2 * y + c)
            out.append(pltpu.make_async_copy(ins[t], blk, local_sems.at[t]))
            for k, peer in enumerate(peers):
                out.append(pltpu.make_async_remote_copy(
                    src_ref=ins[t], dst_ref=blk, send_sem=send_sems.at[4 * t + k], recv_sem=recv_sems.at[4 * t + k],
                    device_id=peer, device_id_type=MESH))
        return out

    def start(ins, outs, sems):
        for cp in copies(ins, outs, sems):
            cp.start()

    def finish(ins, outs, sems):
        for cp in copies(ins, outs, sems):
            cp.wait()

    sems = [pltpu.SemaphoreType.DMA((4 * n,)), pltpu.SemaphoreType.DMA((4 * n,)), pltpu.SemaphoreType.DMA((n,))]
    return _Comm(shards, full, sems, start, finish)


def _gather_stage2(fulls, axes):
    n = len(fulls)
    sizes = [f.shape[a] // N_DEV for f, a in zip(fulls, axes)]

    def copies(ins, outs, sems):
        send_sems, recv_sems = sems
        x, y, c = _place()
        out = []
        for t in range(n):
            for j, (qx, qy) in enumerate([(1 - x, y), (x, 1 - y), (1 - x, 1 - y)]):
                d = 4 * qx + 2 * qy + c
                out.append(pltpu.make_async_remote_copy(
                    src_ref=_piece(ins[t], axes[t], sizes[t], d), dst_ref=_piece(outs[t], axes[t], sizes[t], d),
                    send_sem=send_sems.at[3 * t + j], recv_sem=recv_sems.at[3 * t + j],
                    device_id=(x, y, 1 - c), device_id_type=MESH))
        return out

    def start(ins, outs, sems):
        for cp in copies(ins, outs, sems):
            cp.start()

    def finish(ins, outs, sems):
        for cp in copies(ins, outs, sems):
            cp.wait()

    sems = [pltpu.SemaphoreType.DMA((3 * n,)), pltpu.SemaphoreType.DMA((3 * n,))]
    return _Comm(fulls, [_sds(f.shape, f.dtype) for f in fulls], sems, start, finish,
                 aliases={t: t for t in range(n)})


def _pair_comm(grads, axes, sizes):
    n = len(grads)
    outs_sds = [_sds((4,) + g.shape[:a] + (s,) + g.shape[a + 1:], g.dtype) for g, a, s in zip(grads, axes, sizes)]

    def copies(ins, outs, sems):
        send_sems, recv_sems = sems
        x, y, c = _place()
        return [pltpu.make_async_remote_copy(
            src_ref=_piece(ins[t], axes[t], sizes[t], 2 * qi + (1 - c)), dst_ref=outs[t].at[qi],
            send_sem=send_sems.at[4 * t + qi], recv_sem=recv_sems.at[4 * t + qi],
            device_id=(x, y, 1 - c), device_id_type=MESH) for t in range(n) for qi in range(4)]

    def start(ins, outs, sems):
        for cp in copies(ins, outs, sems):
            cp.start()

    def finish(ins, outs, sems):
        for cp in copies(ins, outs, sems):
            cp.wait()

    sems = [pltpu.SemaphoreType.DMA((4 * n,)), pltpu.SemaphoreType.DMA((4 * n,))]
    return _Comm(grads, outs_sds, sems, start, finish)


def _chip_comm(sums):
    n = len(sums)
    outs_sds = [_sds((3,) + s.shape[1:], s.dtype) for s in sums]

    def copies(ins, outs, sems):
        send_sems, recv_sems = sems
        x, y, c = _place()
        return [pltpu.make_async_remote_copy(
            src_ref=ins[t].at[2 * qx + qy], dst_ref=outs[t].at[j],
            send_sem=send_sems.at[3 * t + j], recv_sem=recv_sems.at[3 * t + j],
            device_id=(qx, qy, c), device_id_type=MESH)
            for t in range(n) for j, (qx, qy) in enumerate([(1 - x, y), (x, 1 - y), (1 - x, 1 - y)])]

    def start(ins, outs, sems):
        for cp in copies(ins, outs, sems):
            cp.start()

    def finish(ins, outs, sems):
        for cp in copies(ins, outs, sems):
            cp.wait()

    sems = [pltpu.SemaphoreType.DMA((3 * n,)), pltpu.SemaphoreType.DMA((3 * n,))]
    return _Comm(sums, outs_sds, sems, start, finish)


def _small_comm(small):
    def copies(ins, outs, sems):
        send_sems, recv_sems, local_sem = sems
        x, y, c = _place()
        mine = outs[0].at[4 * x + 2 * y + c]
        out = [pltpu.make_async_copy(ins[0], mine, local_sem.at[0])]
        for k in range(1, N_DEV):
            peer = (1 - x if k & 4 else x, 1 - y if k & 2 else y, 1 - c if k & 1 else c)
            out.append(pltpu.make_async_remote_copy(
                src_ref=ins[0], dst_ref=mine, send_sem=send_sems.at[k - 1], recv_sem=recv_sems.at[k - 1],
                device_id=peer, device_id_type=MESH))
        return out

    def start(ins, outs, sems):
        for cp in copies(ins, outs, sems):
            cp.start()

    def finish(ins, outs, sems):
        for cp in copies(ins, outs, sems):
            cp.wait()

    sems = [pltpu.SemaphoreType.DMA((N_DEV - 1,)), pltpu.SemaphoreType.DMA((N_DEV - 1,)), pltpu.SemaphoreType.DMA((1,))]
    return _Comm([small], [_sds((N_DEV,) + small.shape, small.dtype)], sems, start, finish)


def _pair_sum(c_idx, grad, recv, axis, size, split, *, name):
    nd = len(grad.shape)
    piece = grad.shape[:axis] + (size,) + grad.shape[axis + 1:]
    blk = (piece[0] // split,) + piece[1:]

    def g_map(q, r, c_ref):
        idx = [0] * nd
        idx[axis] = 2 * q + c_ref[0]
        idx[0] = idx[0] * split + r if axis == 0 else r
        return tuple(idx)

    def r_map(q, r, c_ref):
        return (q, r) + (0,) * (nd - 1)

    def body(c_ref, g_ref, r_ref, o_ref):
        o_ref[0] = (g_ref[...].astype(F32) + r_ref[0].astype(F32)).astype(BF16)

    return pl.pallas_call(
        body,
        grid_spec=pltpu.PrefetchScalarGridSpec(
            num_scalar_prefetch=1, grid=(4, split),
            in_specs=[pl.BlockSpec(blk, g_map), pl.BlockSpec((1,) + blk, r_map)],
            out_specs=pl.BlockSpec((1,) + blk, r_map)),
        out_shape=_sds((4,) + piece, BF16),
        name=name, compiler_params=_params(("arbitrary", "arbitrary"), 32),
    )(c_idx, grad, recv)


def _adam_math(w, g, m, v):
    m = ADAM_B1 * m + (1.0 - ADAM_B1) * g
    v = ADAM_B2 * v + (1.0 - ADAM_B2) * (g * g)
    m_hat = m / (1.0 - ADAM_B1 ** ADAM_STEP)
    v_hat = v / (1.0 - ADAM_B2 ** ADAM_STEP)
    delta = -ADAM_LR * (m_hat / (jnp.sqrt(v_hat) + ADAM_EPS) + ADAM_WD * w)
    return delta, m, v


def _adam_big(q_idx, sums, recv, w, m, v, split, *, name):
    shape = w.shape
    nd = len(shape)
    blk = (shape[0] // split,) + shape[1:]
    w_map = lambda r, q_ref: (r,) + (0,) * (nd - 1)
    s_map = lambda r, q_ref: (q_ref[0], r) + (0,) * (nd - 1)
    r_map = lambda r, q_ref: (0, r) + (0,) * (nd - 1)

    def body(q_ref, s_ref, r_ref, w_ref, m_ref, v_ref, g_ref, d_ref, nm_ref, nv_ref):
        g = s_ref[0].astype(F32) + r_ref[0].astype(F32) + r_ref[1].astype(F32) + r_ref[2].astype(F32)
        g_ref[...] = g
        d_ref[...], nm_ref[...], nv_ref[...] = _adam_math(w_ref[...], g, m_ref[...], v_ref[...])

    wspec = pl.BlockSpec(blk, w_map)
    return pl.pallas_call(
        body,
        grid_spec=pltpu.PrefetchScalarGridSpec(
            num_scalar_prefetch=1, grid=(split,),
            in_specs=[pl.BlockSpec((1,) + blk, s_map), pl.BlockSpec((3,) + blk, r_map), wspec, wspec, wspec],
            out_specs=[wspec] * 4),
        out_shape=[_sds(shape, F32)] * 4,
        name=name, compiler_params=_params(("arbitrary",), 32),
    )(q_idx, sums, recv, w, m, v)


def _adam_small(parts, w, m, v, *, name):
    R = w.shape[0]

    def body(p_ref, w_ref, m_ref, v_ref, g_ref, d_ref, nm_ref, nv_ref):
        g = p_ref[0]
        for d in range(1, N_DEV):
            g = g + p_ref[d]
        g_ref[...] = g
        d_ref[...], nm_ref[...], nv_ref[...] = _adam_math(w_ref[...], g, m_ref[...], v_ref[...])

    return pl.pallas_call(
        body, out_shape=[_sds((R, LANES), F32)] * 4, name=name,
        compiler_params=pltpu.CompilerParams(vmem_limit_bytes=32 * MIB),
    )(parts, w, m, v)


def _pack(arrs):
    return jnp.concatenate([a.reshape(-1) for a in arrs]).reshape(-1, LANES)


def _unpack(packed, shapes):
    flat = packed.reshape(-1)
    out, off = [], 0
    for s in shapes:
        n = 1
        for d in s:
            n *= d
        out.append(flat[off:off + n].reshape(s))
        off += n
    return out


BIG = ("e_in", "e_out", "o_in", "o_cw", "o_out")
BIG_AXIS = dict(e_in=1, e_out=0, o_in=1, o_cw=1, o_out=0)
BIG_SPLIT = dict(e_in=8, e_out=4, o_in=4, o_cw=4, o_out=4)
REPLICATED = ("e_norm_pre", "e_norm_post", "e_b_conv_bias", "e_b_ln_g", "e_b_ln_b")
SHARDED = ("e_a_conv", "e_b_conv", "o_norm_pre", "o_norm_post", "o_c_b", "o_c_scale")
SMALL = REPLICATED + SHARDED


class _Exchange:
    def __init__(self, shards, c_idx):
        self.shards = shards
        self.c_idx = c_idx
        self.reduced = {}

    def gather1(self):
        keys = list(self.shards)
        return _gather_stage1([self.shards[k] for k in keys], [BIG_AXIS[k] for k in keys])

    def gather2(self, fulls):
        return _gather_stage2(fulls, [BIG_AXIS[k] for k in self.shards])

    def pair(self, grads):
        keys = list(grads)
        return _pair_comm([grads[k] for k in keys], [BIG_AXIS[k] for k in keys],
                          [grads[k].shape[BIG_AXIS[k]] // N_DEV for k in keys])

    def pair_sums(self, grads, received):
        return {k: _pair_sum(self.c_idx, grads[k], r, BIG_AXIS[k], grads[k].shape[BIG_AXIS[k]] // N_DEV,
                             BIG_SPLIT[k], name="pair_sum_" + k) for k, r in zip(grads, received)}

    def chips(self, sums):
        return _chip_comm([sums[k] for k in sums])

    def done(self, sums, received):
        self.reduced.update({k: (sums[k], r) for k, r in zip(sums, received)})


def _local_step(x, tgt, wt, sm, ex=None):
    S, D = x.shape
    tm, tnt, tx = min(TM_MM, S), min(TM_NT, S), min(TM_MIX, S)
    W = wt["e_in"].shape[1] // 7
    on = ex is not None

    (p, h0), part = _norm_matmul(x, sm["e_norm_pre"], wt["e_in"], tm=tm, tn=W, name="e_in_fwd",
                                 comm=ex.gather1() if on else None)
    (u, cb), full = _e_mix_fwd(p, sm["e_a_conv"], sm["e_b_conv"], sm["e_b_conv_bias"], sm["e_b_ln_g"],
                               sm["e_b_ln_b"], tm=tx, name="e_mix_fwd", comm=ex.gather2(part) if on else None)
    if on:
        wt = {**wt, **dict(zip(ex.shards, full))}
    x1, y0 = _out_norm_res(u, wt["e_out"], x, sm["e_norm_post"], tm=tx, name="e_out_fwd")
    (q, h1), _ = _norm_matmul(x1, sm["o_norm_pre"], wt["o_in"], tm=tm, tn=W, name="o_in_fwd")
    yy, pooled, gg = _o_mix_fwd(q, wt["o_cw"], sm["o_c_b"], sm["o_c_scale"], tm=tx, name="o_mix_fwd")
    dout, dx2, dyy, lcol, dg_o_post = _out_loss(yy, wt["o_out"], x1, sm["o_norm_post"], tgt, tm=tx, name="o_out_loss")
    loss = (0.5 / D) * jnp.sum(lcol)

    dq, d_cw, d_cb, d_cscale = _o_mix_bwd(dyy, q, gg, pooled, wt["o_cw"], sm["o_c_scale"], tm=tx, name="o_mix_bwd")
    g_o_out, _ = _mm_tn(yy, dout, ts=tnt, tn=W, name="o_out_dw")
    ga = dict(o_out=g_o_out, o_cw=d_cw.astype(BF16))
    dh1, ra = _mm_nt(dq, wt["o_in"], tm=tnt, tk=W, name="o_in_bwd", comm=ex.pair(ga) if on else None)
    sa = ex.pair_sums(ga, ra) if on else None
    (dx1, dy0, dg_o_pre, dg_e_post), ra = _pre_bwd_o(dh1, x1, dx2, y0, sm["o_norm_pre"], sm["e_norm_post"],
                                                     tm=tx, name="o_pre_bwd", comm=ex.chips(sa) if on else None)
    g_o_in, _ = _mm_tn(h1, dq, ts=tnt, tn=W, name="o_in_dw")
    gb = dict(o_in=g_o_in)
    du, rb = _mm_nt(dy0, wt["e_out"], tm=tnt, tk=W, name="e_out_bwd", comm=ex.pair(gb) if on else None)
    sb = ex.pair_sums(gb, rb) if on else None
    g_e_out, _ = _mm_tn(u, dy0, ts=tnt, tn=W, name="e_out_dw")
    gc = dict(e_out=g_e_out)
    (dp, d_wa, d_wb, d_bias, d_lg, d_lb), rbc = _e_mix_bwd(
        du, p, cb, sm["e_a_conv"], sm["e_b_conv"], sm["e_b_ln_g"], sm["e_b_ln_b"], tm=tx, name="e_mix_bwd",
        comm=_merge(ex.chips(sb), ex.pair(gc)) if on else None)
    sc = ex.pair_sums(gc, rbc[1:]) if on else None
    g_e_in, rc = _mm_tn(h0, dp, ts=tnt, tn=W, name="e_in_dw", comm=ex.chips(sc) if on else None)
    gd = dict(e_in=g_e_in)
    sd = ex.pair_sums(gd, _run_comm(ex.pair(gd), "pair_e_in")) if on else None
    dh0, rd = _mm_nt(dp, wt["e_in"], tm=tnt, tk=W, name="e_in_bwd", comm=ex.chips(sd) if on else None)
    grad_x, dg_e_pre = _pre_bwd_e(dh0, x, dx1, sm["e_norm_pre"], tm=tx, name="e_pre_bwd")
    if on:
        ex.done(sa, ra)
        ex.done(sb, rbc[:1])
        ex.done(sc, rc)
        ex.done(sd, rd)

    big = {**ga, **gb, **gc, **gd}
    small = dict(e_norm_pre=dg_e_pre, e_norm_post=dg_e_post, e_a_conv=d_wa, e_b_conv=d_wb, e_b_conv_bias=d_bias,
                 e_b_ln_g=d_lg, e_b_ln_b=d_lb, o_norm_pre=dg_o_pre, o_norm_post=dg_o_post, o_c_b=d_cb,
                 o_c_scale=d_cscale)
    return loss, grad_x, big, small


def kernel(x, e_norm_pre, e_norm_post, e_w_in, e_a_conv, e_b_conv, e_b_conv_bias, e_b_ln_g, e_b_ln_b, e_w_out, o_norm_pre, o_norm_post, o_w_in, o_c_w, o_c_b, o_c_scale, o_w_out, loss_target, m_e_norm_pre, m_e_norm_post, m_e_w_in, m_e_a_conv, m_e_b_conv, m_e_b_conv_bias, m_e_b_ln_g, m_e_b_ln_b, m_e_w_out, m_o_norm_pre, m_o_norm_post, m_o_w_in, m_o_c_w, m_o_c_b, m_o_c_scale, m_o_w_out, v_e_norm_pre, v_e_norm_post, v_e_w_in, v_e_a_conv, v_e_b_conv, v_e_b_conv_bias, v_e_b_ln_g, v_e_b_ln_b, v_e_w_out, v_o_norm_pre, v_o_norm_post, v_o_w_in, v_o_c_w, v_o_c_b, v_o_c_scale, v_o_w_out):
    xi, yi, ci = _place()
    me = 4 * xi + 2 * yi + ci
    w_big = dict(e_in=e_w_in[0], e_out=e_w_out[0], o_in=o_w_in[0], o_cw=o_c_w[0], o_out=o_w_out[0])
    m_big = dict(e_in=m_e_w_in[0], e_out=m_e_w_out[0], o_in=m_o_w_in[0], o_cw=m_o_c_w[0], o_out=m_o_w_out[0])
    v_big = dict(e_in=v_e_w_in[0], e_out=v_e_w_out[0], o_in=v_o_w_in[0], o_cw=v_o_c_w[0], o_out=v_o_w_out[0])
    w_small = dict(e_norm_pre=e_norm_pre, e_norm_post=e_norm_post, e_b_conv_bias=e_b_conv_bias, e_b_ln_g=e_b_ln_g,
                   e_b_ln_b=e_b_ln_b, e_a_conv=e_a_conv[0], e_b_conv=e_b_conv[0], o_norm_pre=o_norm_pre,
                   o_norm_post=o_norm_post, o_c_b=o_c_b[0], o_c_scale=o_c_scale)
    m_small = dict(e_norm_pre=m_e_norm_pre, e_norm_post=m_e_norm_post, e_b_conv_bias=m_e_b_conv_bias,
                   e_b_ln_g=m_e_b_ln_g, e_b_ln_b=m_e_b_ln_b, e_a_conv=m_e_a_conv[0], e_b_conv=m_e_b_conv[0],
                   o_norm_pre=m_o_norm_pre, o_norm_post=m_o_norm_post, o_c_b=m_o_c_b[0], o_c_scale=m_o_c_scale)
    v_small = dict(e_norm_pre=v_e_norm_pre, e_norm_post=v_e_norm_post, e_b_conv_bias=v_e_b_conv_bias,
                   e_b_ln_g=v_e_b_ln_g, e_b_ln_b=v_e_b_ln_b, e_a_conv=v_e_a_conv[0], e_b_conv=v_e_b_conv[0],
                   o_norm_pre=v_o_norm_pre, o_norm_post=v_o_norm_post, o_c_b=v_o_c_b[0], o_c_scale=v_o_c_scale)

    gathered = _gather_weights([w_big["e_in"].astype(BF16)], [BIG_AXIS["e_in"]],
                               _pack([w_small[k] for k in SHARDED]))
    wt = dict(e_in=gathered[0])
    per_dev = [_unpack(gathered[-1][d], [w_small[k].shape for k in SHARDED]) for d in range(N_DEV)]
    sm = {k: w_small[k] for k in REPLICATED}
    for j, k in enumerate(SHARDED):
        sm[k] = jnp.concatenate([per_dev[d][j] for d in range(N_DEV)], axis=-1)
    n_groups = sm["o_c_b"].shape[0]
    sm["o_c_b"] = sm["o_c_b"].reshape(1, -1)

    c_idx = jnp.reshape(ci, (1,)).astype(jnp.int32)
    ex = _Exchange({k: w_big[k].astype(BF16) for k in BIG[1:]}, c_idx)
    loss, grad_x, _, g_small = _local_step(x[0], loss_target[0], wt, sm, ex)
    loss = lax.psum(loss, ("x", "y", "c"))

    g_small["o_c_b"] = g_small["o_c_b"].reshape(n_groups, -1)
    full_shapes = [g_small[k].shape for k in SMALL]
    small_parts = _run_comm(_small_comm(_pack([g_small[k] for k in SMALL])), "small_grad_exchange")[0]

    q_idx = jnp.reshape(2 * xi + yi, (1,)).astype(jnp.int32)
    big_out = {k: _adam_big(q_idx, *ex.reduced[k], w_big[k], m_big[k], v_big[k], BIG_SPLIT[k], name="adam_" + k)
               for k in BIG}

    def at_full_size(d):
        out = []
        for k, s in zip(SMALL, full_shapes):
            if k in REPLICATED:
                out.append(d[k])
            else:
                n = d[k].shape[-1]
                out.append(lax.dynamic_update_slice_in_dim(jnp.ones(s, F32), d[k], me * n, axis=-1))
        return _pack(out)

    res_small = _adam_small(small_parts, at_full_size(w_small), at_full_size(m_small), at_full_size(v_small),
                            name="adam_small")
    small_out = {k: [] for k in SMALL}
    for packed in res_small:
        for k, full in zip(SMALL, _unpack(packed, full_shapes)):
            if k in SHARDED:
                n = w_small[k].shape[-1]
                full = lax.dynamic_slice_in_dim(full, me * n, n, axis=-1)
            small_out[k].append(full)

    big_of = dict(e_w_in="e_in", e_w_out="e_out", o_w_in="o_in", o_c_w="o_cw", o_w_out="o_out")
    stacked = ("e_a_conv", "e_b_conv", "o_c_b")

    def leaf(name, which):
        if name in big_of:
            return big_out[big_of[name]][which][None]
        t = small_out[name][which]
        return t[None] if name in stacked else t

    order = ("e_norm_pre", "e_norm_post", "e_w_in", "e_a_conv", "e_b_conv", "e_b_conv_bias", "e_b_ln_g", "e_b_ln_b",
             "e_w_out", "o_norm_pre", "o_norm_post", "o_w_in", "o_c_w", "o_c_b", "o_c_scale", "o_w_out")
    outs = [loss, grad_x[None]]
    for which in range(4):
        outs += [leaf(nm, which) for nm in order]
    return tuple(outs)
```

```python
import jax
import jax.numpy as jnp
from jax import lax
from jax.experimental import pallas as pl
from jax.experimental.pallas import tpu as pltpu

F32 = jnp.float32
BF16 = jnp.bfloat16
EPS = 1e-6
MESH = pl.DeviceIdType.MESH
ANY = pl.BlockSpec(memory_space=pl.ANY)

N_DEV = 8
HALO = 32
PHALO = 16
CONV_A = 3
CONV_B = 31
POOL_WINDOWS = (2, 4, 8, 16)
LANES = 128
MIB = 1024 * 1024

ADAM_LR = 0.001
ADAM_B1 = 0.9
ADAM_B2 = 0.999
ADAM_EPS = 1e-08
ADAM_WD = 0.01
ADAM_STEP = 10

TM_NT = 1024
TM_MIX = 256


def _sds(shape, dtype):
    return jax.ShapeDtypeStruct(tuple(shape), dtype)


def _params(sem, vmem_mib):
    return pltpu.CompilerParams(dimension_semantics=sem, vmem_limit_bytes=vmem_mib * MIB)


def _const(shape, single=False):
    n = len(shape)
    if single:
        return pl.BlockSpec(shape, lambda *_: (0,) * n, pipeline_mode=pl.Buffered(1))
    return pl.BlockSpec(shape, lambda *_: (0,) * n)


def _sig(v):
    return jax.nn.sigmoid(v)


def _dsilu(v, s):
    return s * (1.0 + v * (1.0 - s))


def _rms(v):
    return lax.rsqrt(jnp.mean(v * v, axis=-1, keepdims=True) + EPS)


def _norm_bwd(dn, n, r):
    return r * (dn - n * jnp.mean(dn * n, axis=-1, keepdims=True))


def _colsum(v):
    return jnp.sum(v, axis=0, keepdims=True)


class _Comm:
    def __init__(self, inputs, out_shapes, sems, start, finish, aliases=None, middle=None):
        self.inputs, self.out_shapes, self.sems = list(inputs), list(out_shapes), list(sems)
        self.start, self.finish, self.middle = start, finish, middle
        self.aliases = dict(aliases or {})


def _merge(*comms):
    comms = [c for c in comms if c is not None]
    if len(comms) <= 1:
        return comms[0] if comms else None
    spans, i0, o0, s0, aliases = [], 0, 0, 0, {}
    for c in comms:
        spans.append((i0, o0, s0))
        aliases.update({i0 + k: o0 + v for k, v in c.aliases.items()})
        i0, o0, s0 = i0 + len(c.inputs), o0 + len(c.out_shapes), s0 + len(c.sems)

    def run(which):
        def fn(ins, outs, sems):
            for c, (i, o, s) in zip(comms, spans):
                hook = getattr(c, which)
                if hook is not None:
                    hook(ins[i:i + len(c.inputs)], outs[o:o + len(c.out_shapes)], sems[s:s + len(c.sems)])
        return fn

    return _Comm([a for c in comms for a in c.inputs], [a for c in comms for a in c.out_shapes],
                 [a for c in comms for a in c.sems], run("start"), run("finish"), aliases,
                 run("middle") if any(c.middle is not None for c in comms) else None)


def _call(body, *, grid, in_specs, out_specs, out_shape, operands, name, params, scratch_shapes=(), comm=None,
          prefetch=None):
    n_p = 0 if prefetch is None else 1
    n_i, n_o, n_s = len(in_specs), len(out_specs), len(scratch_shapes)
    if comm is None:
        comm = _Comm([], [], [], None, None)
    c_i, c_o = len(comm.inputs), len(comm.out_shapes)

    def carrier(*refs):
        pre, refs = refs[:n_p], refs[n_p:]
        ins, cins = refs[:n_i], refs[n_i:n_i + c_i]
        outs = refs[n_i + c_i:n_i + c_i + n_o]
        couts = refs[n_i + c_i + n_o:n_i + c_i + n_o + c_o]
        scr = refs[n_i + c_i + n_o + c_o:n_i + c_i + n_o + c_o + n_s]
        csems = refs[n_i + c_i + n_o + c_o + n_s:]
        ids = [pl.program_id(d) for d in range(len(grid))]
        first = ids[0] == 0
        half = ids[0] == grid[0] // 2
        last = ids[0] == grid[0] - 1
        for d in range(1, len(grid)):
            first = first & (ids[d] == 0)
            half = half & (ids[d] == 0)
            last = last & (ids[d] == grid[d] - 1)

        if comm.start is not None:
            @pl.when(first)
            def _():
                comm.start(cins, couts, csems)

        if comm.middle is not None:
            assert grid[0] >= 2

            @pl.when(half)
            def _():
                comm.middle(cins, couts, csems)

        body(*pre, *ins, *outs, *scr)

        if comm.finish is not None:
            @pl.when(last)
            def _():
                comm.finish(cins, couts, csems)

    specs = dict(grid=grid, in_specs=list(in_specs) + [ANY] * c_i, out_specs=list(out_specs) + [ANY] * c_o,
                 scratch_shapes=list(scratch_shapes) + comm.sems)
    if n_p:
        specs = dict(grid_spec=pltpu.PrefetchScalarGridSpec(num_scalar_prefetch=1, **specs))
    res = pl.pallas_call(
        carrier, out_shape=list(out_shape) + comm.out_shapes,
        input_output_aliases={n_p + n_i + k: n_o + v for k, v in comm.aliases.items()},
        name=name, compiler_params=params, **specs)(*(() if prefetch is None else (prefetch,)), *operands, *comm.inputs)
    return list(res[:n_o]), list(res[n_o:])


def _run_comm(comm, name):
    c_i, c_o = len(comm.inputs), len(comm.out_shapes)

    def body(*refs):
        ins, outs, sems = refs[:c_i], refs[c_i:c_i + c_o], refs[c_i + c_o:]
        comm.start(ins, outs, sems)
        comm.finish(ins, outs, sems)

    res = pl.pallas_call(
        body, in_specs=[ANY] * c_i, out_specs=[ANY] * c_o, out_shape=comm.out_shapes, scratch_shapes=comm.sems,
        input_output_aliases=comm.aliases, name=name)(*comm.inputs)
    return list(res)


def _rms_norm(x, g, *, tm, name, comm=None):
    S, D = x.shape

    def body(x_ref, g_ref, h_ref):
        xx = x_ref[...]
        h_ref[...] = ((xx * _rms(xx)) * g_ref[...]).astype(BF16)

    row = pl.BlockSpec((tm, D), lambda i: (i, 0))
    outs, extra = _call(body, grid=(S // tm,), in_specs=[row, _const((1, D))], out_specs=[row],
                        out_shape=[_sds((S, D), BF16)], operands=(x, g), name=name,
                        params=_params(("arbitrary",), 32), comm=comm)
    return outs[0], extra


def _gather_matmul(order, h, shard, *, tm, name, comm=None):
    S, K = h.shape
    nb = shard.shape[1]
    n_i = S // tm

    def body(order_ref, h_ref, shard_ref, p_ref, full_ref, wbuf, send_sems, recv_sems, dma_sems):
        j, i = pl.program_id(0), pl.program_id(1)
        x, y, c = _place()
        me, sibling = (x, y, c), (x, y, 1 - c)
        chips = [(1 - x, y), (x, 1 - y), (1 - x, 1 - y)]

        def block(owner):
            return _piece(full_ref, 1, nb, 4 * owner[0] + 2 * owner[1] + owner[2])

        def copy(k, owner, to, src=None):
            blk = block(owner)
            return pltpu.make_async_remote_copy(
                src_ref=blk if src is None else src, dst_ref=blk, send_sem=send_sems.at[k], recv_sem=recv_sems.at[k],
                device_id=to, device_id_type=MESH)

        def passed_on(jj):
            copy(1 + jj, (*chips[jj], c), me).wait_recv()
            copy(4 + jj, (*chips[jj], c), sibling).start()

        def load(qx, qy):
            cp = pltpu.make_async_copy(_piece(full_ref, 1, 2 * nb, 2 * qx + qy), wbuf, dma_sems.at[1])
            cp.start()
            cp.wait()

        @pl.when((j == 0) & (i == 0))
        def _():
            own = pltpu.make_async_copy(shard_ref, block(me), dma_sems.at[0])
            own.start()
            copy(0, me, sibling, src=shard_ref).start()
            for jj, chip in enumerate(chips):
                copy(1 + jj, me, (*chip, c), src=shard_ref).start()
            own.wait()
            copy(0, sibling, me).wait_recv()
            load(x, y)

        @pl.when((j == 1) & (i == 0))
        def _():
            passed_on(0)
            passed_on(1)
            copy(4, (*chips[0], 1 - c), me).wait_recv()
            load(*chips[0])

        @pl.when((j == 2) & (i == 0))
        def _():
            copy(5, (*chips[1], 1 - c), me).wait_recv()
            load(*chips[1])

        @pl.when((j == 3) & (i == 0))
        def _():
            passed_on(2)
            copy(6, (*chips[2], 1 - c), me).wait_recv()
            load(*chips[2])

        p_ref[...] = jnp.dot(h_ref[...], wbuf[...], preferred_element_type=F32).astype(BF16)

        @pl.when((j == 3) & (i == n_i - 1))
        def _():
            for k in range(7):
                copy(k, me, sibling).wait_send()

    outs, extra = _call(
        body, grid=(4, n_i), prefetch=order,
        in_specs=[pl.BlockSpec((tm, K), lambda j, i, o: (i, 0)), ANY],
        out_specs=[pl.BlockSpec((tm, 2 * nb), lambda j, i, o: (i, o[j])), ANY],
        out_shape=[_sds((S, N_DEV * nb), BF16), _sds((K, N_DEV * nb), BF16)], operands=(h, shard),
        scratch_shapes=[pltpu.VMEM((K, 2 * nb), BF16), pltpu.SemaphoreType.DMA((7,)), pltpu.SemaphoreType.DMA((7,)),
                        pltpu.SemaphoreType.DMA((2,))],
        name=name, params=_params(("arbitrary", "arbitrary"), 48), comm=comm)
    return outs[0], outs[1], extra


def _out_norm_res(u, w, x, g, *, tm, name, comm=None):
    S, K = u.shape
    D = w.shape[1]

    def body(u_ref, w_ref, x_ref, g_ref, x1_ref, y_ref):
        y = jnp.dot(u_ref[...], w_ref[...], preferred_element_type=F32)
        y_ref[...] = y.astype(BF16)
        x1_ref[...] = x_ref[...] + (y * _rms(y)) * g_ref[...]

    return _call(
        body, grid=(S // tm,),
        in_specs=[pl.BlockSpec((tm, K), lambda i: (i, 0)), _const((K, D), single=True),
                  pl.BlockSpec((tm, D), lambda i: (i, 0)), _const((1, D))],
        out_specs=[pl.BlockSpec((tm, D), lambda i: (i, 0)), pl.BlockSpec((tm, D), lambda i: (i, 0))],
        out_shape=[_sds((S, D), F32), _sds((S, D), BF16)], operands=(u, w, x, g),
        name=name, params=_params(("arbitrary",), 48), comm=comm)


def _out_loss(yy, w, x1, g, tgt, *, tm, name):
    S, K = yy.shape
    D = w.shape[1]

    def body(yy_ref, w_ref, x1_ref, g_ref, t_ref, dout_ref, dx2_ref, dyy_ref, lcol_ref, dg_ref):
        out = jnp.dot(yy_ref[...], w_ref[...], preferred_element_type=F32)
        r = _rms(out)
        n = out * r
        gg = g_ref[...]
        e = x1_ref[...] + n * gg - t_ref[...]
        dx2 = e * (1.0 / D)
        dx2_ref[...] = dx2
        dout = _norm_bwd(dx2 * gg, n, r).astype(BF16)
        dout_ref[...] = dout
        dyy_ref[...] = lax.dot_general(dout, w_ref[...], (((1,), (1,)), ((), ())),
                                       preferred_element_type=F32).astype(BF16)

        @pl.when(pl.program_id(0) == 0)
        def _():
            lcol_ref[...] = jnp.zeros_like(lcol_ref)
            dg_ref[...] = jnp.zeros_like(dg_ref)

        lcol_ref[...] += _colsum(e * e)
        dg_ref[...] += _colsum(dx2 * n)

    return pl.pallas_call(
        body, grid=(S // tm,),
        in_specs=[pl.BlockSpec((tm, K), lambda i: (i, 0)), _const((K, D), single=True),
                  pl.BlockSpec((tm, D), lambda i: (i, 0)), _const((1, D)),
                  pl.BlockSpec((tm, D), lambda i: (i, 0))],
        out_specs=[pl.BlockSpec((tm, D), lambda i: (i, 0)), pl.BlockSpec((tm, D), lambda i: (i, 0)),
                   pl.BlockSpec((tm, K), lambda i: (i, 0)), _const((1, D)), _const((1, D))],
        out_shape=[_sds((S, D), BF16), _sds((S, D), F32), _sds((S, K), BF16), _sds((1, D), F32), _sds((1, D), F32)],
        name=name, compiler_params=_params(("arbitrary",), 52),
    )(yy, w, x1, g, tgt)


def _mm_nt(a, w, *, tm, tk, name, comm=None):
    S, N = a.shape
    D = w.shape[0]
    n_k = N // tk

    def body(a_ref, w_ref, o_ref, acc_ref):
        k = pl.program_id(1)
        part = lax.dot_general(a_ref[...], w_ref[...], (((1,), (1,)), ((), ())), preferred_element_type=F32)

        @pl.when(k == 0)
        def _():
            acc_ref[...] = part

        @pl.when(k > 0)
        def _():
            acc_ref[...] += part

        @pl.when(k == n_k - 1)
        def _():
            o_ref[...] = acc_ref[...].astype(BF16)

    outs, extra = _call(
        body, grid=(S // tm, n_k),
        in_specs=[pl.BlockSpec((tm, tk), lambda i, k: (i, k)), pl.BlockSpec((D, tk), lambda i, k: (0, k))],
        out_specs=[pl.BlockSpec((tm, D), lambda i, k: (i, 0))],
        out_shape=[_sds((S, D), BF16)], operands=(a, w),
        scratch_shapes=[pltpu.VMEM((tm, D), F32)],
        name=name, params=_params(("arbitrary", "arbitrary"), 48), comm=comm)
    return outs[0], extra


def _mm_tn(a, b, *, ts, tn, name, comm=None):
    S, M = a.shape
    N = b.shape[1]
    n_s = S // ts

    def body(a_ref, b_ref, o_ref, acc_ref):
        s = pl.program_id(1)
        part = lax.dot_general(a_ref[...], b_ref[...], (((0,), (0,)), ((), ())), preferred_element_type=F32)

        @pl.when(s == 0)
        def _():
            acc_ref[...] = part

        @pl.when(s > 0)
        def _():
            acc_ref[...] += part

        @pl.when(s == n_s - 1)
        def _():
            o_ref[...] = acc_ref[...].astype(BF16)

    outs, extra = _call(
        body, grid=(N // tn, n_s),
        in_specs=[pl.BlockSpec((ts, M), lambda j, s: (s, 0)), pl.BlockSpec((ts, tn), lambda j, s: (s, j))],
        out_specs=[pl.BlockSpec((M, tn), lambda j, s: (0, j))],
        out_shape=[_sds((M, N), BF16)], operands=(a, b),
        scratch_shapes=[pltpu.VMEM((M, tn), F32)],
        name=name, params=_params(("arbitrary", "arbitrary"), 48), comm=comm)
    return outs[0], extra


def _pre_bwd_o(dh, x1, dx2, y0, g_pre, g_post, *, tm, name, comm=None):
    S, D = x1.shape

    def body(dh_ref, x1_ref, dx2_ref, y0_ref, gpre_ref, gpost_ref, dx1_ref, dy0_ref, dgpre_ref, dgpost_ref):
        @pl.when(pl.program_id(0) == 0)
        def _():
            dgpre_ref[...] = jnp.zeros_like(dgpre_ref)
            dgpost_ref[...] = jnp.zeros_like(dgpost_ref)

        dh = dh_ref[...].astype(F32)
        x1 = x1_ref[...]
        r2 = _rms(x1)
        xn = x1 * r2
        dgpre_ref[...] += _colsum(dh * xn)
        dx1 = dx2_ref[...] + _norm_bwd(dh * gpre_ref[...], xn, r2)
        dx1_ref[...] = dx1
        y = y0_ref[...].astype(F32)
        r1 = _rms(y)
        n1 = y * r1
        dgpost_ref[...] += _colsum(dx1 * n1)
        dy0_ref[...] = _norm_bwd(dx1 * gpost_ref[...], n1, r1).astype(BF16)

    row = pl.BlockSpec((tm, D), lambda i: (i, 0))
    return _call(
        body, grid=(S // tm,),
        in_specs=[row, row, row, row, _const((1, D)), _const((1, D))],
        out_specs=[row, row, _const((1, D)), _const((1, D))],
        out_shape=[_sds((S, D), F32), _sds((S, D), BF16), _sds((1, D), F32), _sds((1, D), F32)],
        operands=(dh, x1, dx2, y0, g_pre, g_post),
        name=name, params=_params(("arbitrary",), 48), comm=comm)


def _pre_bwd_e(dh, x, dx1, g_pre, *, tm, name):
    S, D = x.shape

    def body(dh_ref, x_ref, dx1_ref, gpre_ref, gx_ref, dgpre_ref):
        @pl.when(pl.program_id(0) == 0)
        def _():
            dgpre_ref[...] = jnp.zeros_like(dgpre_ref)

        dh = dh_ref[...].astype(F32)
        xx = x_ref[...]
        r0 = _rms(xx)
        xn = xx * r0
        dgpre_ref[...] += _colsum(dh * xn)
        gx_ref[...] = dx1_ref[...] + _norm_bwd(dh * gpre_ref[...], xn, r0)

    row = pl.BlockSpec((tm, D), lambda i: (i, 0))
    return pl.pallas_call(
        body, grid=(S // tm,),
        in_specs=[row, row, row, _const((1, D))],
        out_specs=[row, _const((1, D))],
        out_shape=[_sds((S, D), F32), _sds((1, D), F32)],
        name=name, compiler_params=_params(("arbitrary",), 48),
    )(dh, x, dx1, g_pre)


SUBLANES = 8


def _shift_copies(sh_ref, ext_ref, cs):
    for b in range(1, SUBLANES):
        sh_ref[b - 1] = ext_ref[pl.ds(b, sh_ref.shape[1]), cs]


def _rows_at(ext_ref, sh_ref, off, cs, tm):
    b = off % SUBLANES
    if b == 0 or sh_ref is None:
        return ext_ref[pl.ds(off, tm), cs]
    return sh_ref[b - 1, pl.ds(off - b, tm), :]


def _taps(ext_ref, w_ref, n_taps, base, cs, tm, sh_ref=None):
    acc = _rows_at(ext_ref, sh_ref, base, cs, tm) * w_ref[0:1, cs]
    for k in range(1, n_taps):
        acc = acc + _rows_at(ext_ref, sh_ref, base + k, cs, tm) * w_ref[k:k + 1, cs]
    return acc


def _taps_rev(ext_ref, w_ref, n_taps, cs, tm, sh_ref=None):
    acc = _rows_at(ext_ref, sh_ref, n_taps - 1, cs, tm) * w_ref[0:1, cs]
    for k in range(1, n_taps):
        acc = acc + _rows_at(ext_ref, sh_ref, n_taps - 1 - k, cs, tm) * w_ref[k:k + 1, cs]
    return acc


def _e_mix_fwd(p, wa, wb, bias, ln_g, ln_b, *, tm, name, comm=None):
    S = p.shape[0]
    W = p.shape[1] // 7
    nb = tm // HALO
    chunks = [slice(c * LANES, (c + 1) * LANES) for c in range(W // LANES)]

    def body(p_ref, hax_ref, hac_ref, hbv_ref, hbg_ref, wa_ref, wb_ref, bias_ref, lg_ref, lb_ref,
             u_ref, cb_ref, ext_ref, sh_ref):
        keep = (pl.program_id(0) > 0).astype(F32)
        col = lambda j, cs: p_ref[:, j * W + cs.start:j * W + cs.stop].astype(F32)

        ext_ref[0:HALO, :] = hax_ref[...].astype(F32) * hac_ref[...].astype(F32) * keep
        ext_ref[HALO:, :] = p_ref[:, 2 * W:3 * W].astype(F32) * p_ref[:, 0:W].astype(F32)
        for cs in chunks:
            conv = _taps(ext_ref, wa_ref, CONV_A, HALO - (CONV_A - 1), cs, tm)
            az = col(3, cs)
            u_ref[:, cs] = (col(1, cs) * conv * (az * _sig(az))).astype(BF16)

        ext_ref[0:HALO, :] = hbv_ref[...].astype(F32) * _sig(hbg_ref[...].astype(F32)) * keep
        ext_ref[HALO:, :] = p_ref[:, 4 * W:5 * W].astype(F32) * _sig(p_ref[:, 5 * W:6 * W].astype(F32))
        s1 = jnp.zeros((tm, LANES), F32)
        for cs in chunks:
            _shift_copies(sh_ref, ext_ref, cs)
            cb = _taps(ext_ref, wb_ref, CONV_B, HALO - (CONV_B - 1), cs, tm, sh_ref) + bias_ref[:, cs]
            cb_ref[:, cs] = cb
            s1 = s1 + cb
        mu = jnp.sum(s1, axis=-1, keepdims=True) * (1.0 / W)
        s2 = jnp.zeros((tm, LANES), F32)
        for cs in chunks:
            xc = cb_ref[:, cs] - mu
            s2 = s2 + xc * xc
        rs = lax.rsqrt(jnp.sum(s2, axis=-1, keepdims=True) * (1.0 / W) + EPS)
        for cs in chunks:
            lb = (cb_ref[:, cs] - mu) * rs * lg_ref[:, cs] + lb_ref[:, cs]
            bz = col(6, cs)
            u_ref[:, W + cs.start:W + cs.stop] = (lb * _sig(lb) * (bz * _sig(bz))).astype(BF16)

    prev = lambda j: pl.BlockSpec((HALO, W), lambda i: (jnp.maximum(i * nb - 1, 0), j))
    return _call(
        body, grid=(S // tm,),
        in_specs=[pl.BlockSpec((tm, 7 * W), lambda i: (i, 0)), prev(0), prev(2), prev(4), prev(5),
                  _const((CONV_A, W)), _const((CONV_B, W)), _const((1, W)), _const((1, W)), _const((1, W))],
        out_specs=[pl.BlockSpec((tm, 2 * W), lambda i: (i, 0)), pl.BlockSpec((tm, W), lambda i: (i, 0))],
        out_shape=[_sds((S, 2 * W), BF16), _sds((S, W), F32)],
        operands=(p, p, p, p, p, wa, wb, bias, ln_g, ln_b),
        scratch_shapes=[pltpu.VMEM((HALO + tm, W), F32),
                        pltpu.VMEM((SUBLANES - 1, HALO + tm - SUBLANES, LANES), F32)],
        name=name, params=_params(("arbitrary",), 48), comm=comm)


def _e_mix_bwd(du, p, cb, wa, wb, ln_g, ln_b, *, tm, name, comm=None):
    S = p.shape[0]
    W = p.shape[1] // 7
    nb = tm // HALO
    n_t = S // tm
    last_blk = S // HALO - 1
    chunks = [slice(c * LANES, (c + 1) * LANES) for c in range(W // LANES)]

    def body(du_ref, duf_ref, p_ref, fab_ref, faz_ref, fbz_ref, hax_ref, hac_ref, hbv_ref, hbg_ref,
             cb_ref, cbf_ref, wa_ref, wb_ref, lg_ref, lb_ref,
             dp_ref, dwa_ref, dwb_ref, dbias_ref, dlg_ref, dlb_ref, extd_ref, extg_ref, shd_ref, shg_ref):
        i = pl.program_id(0)
        keep_prev = (i > 0).astype(F32)
        keep_next = (i < n_t - 1).astype(F32)
        col = lambda j, cs: p_ref[:, j * W + cs.start:j * W + cs.stop].astype(F32)

        @pl.when(i == 0)
        def _():
            dwa_ref[...] = jnp.zeros_like(dwa_ref)
            dwb_ref[...] = jnp.zeros_like(dwb_ref)
            dbias_ref[...] = jnp.zeros_like(dbias_ref)
            dlg_ref[...] = jnp.zeros_like(dlg_ref)
            dlb_ref[...] = jnp.zeros_like(dlb_ref)

        def dcb_rows(rows, cb_rows_ref, dub, bz_of, dst0, scale, main):
            cbv = cb_rows_ref[...]
            mu = jnp.mean(cbv, axis=-1, keepdims=True)
            xc = cbv - mu
            rs = lax.rsqrt(jnp.mean(xc * xc, axis=-1, keepdims=True) + EPS)
            m1 = jnp.zeros((rows, LANES), F32)
            m2 = jnp.zeros((rows, LANES), F32)
            for cs in chunks:
                nbv = (cb_rows_ref[:, cs] - mu) * rs
                lb = nbv * lg_ref[:, cs] + lb_ref[:, cs]
                sl = _sig(lb)
                bz = bz_of(cs)
                sz = _sig(bz)
                dub_c = dub(cs)
                dlb = dub_c * (bz * sz) * _dsilu(lb, sl)
                if main:
                    dlg_ref[:, cs] += _colsum(dlb * nbv)
                    dlb_ref[:, cs] += _colsum(dlb)
                    dp_ref[:, 6 * W + cs.start:6 * W + cs.stop] = (dub_c * (lb * sl) * _dsilu(bz, sz)).astype(BF16)
                dnb = dlb * lg_ref[:, cs]
                extd_ref[dst0:dst0 + rows, cs] = dnb
                m1 = m1 + dnb
                m2 = m2 + dnb * nbv
            m1 = jnp.sum(m1, axis=-1, keepdims=True) * (1.0 / W)
            m2 = jnp.sum(m2, axis=-1, keepdims=True) * (1.0 / W)
            for cs in chunks:
                nbv = (cb_rows_ref[:, cs] - mu) * rs
                dcb = rs * (extd_ref[dst0:dst0 + rows, cs] - m1 - nbv * m2) * scale
                extd_ref[dst0:dst0 + rows, cs] = dcb
                if main:
                    dbias_ref[:, cs] += _colsum(dcb)

        dcb_rows(tm, cb_ref, lambda cs: du_ref[:, W + cs.start:W + cs.stop].astype(F32),
                 lambda cs: col(6, cs), 0, 1.0, True)
        dcb_rows(HALO, cbf_ref, lambda cs: duf_ref[:, W + cs.start:W + cs.stop].astype(F32),
                 lambda cs: fbz_ref[:, cs].astype(F32), tm, keep_next, False)

        extg_ref[0:HALO, :] = hbv_ref[...].astype(F32) * _sig(hbg_ref[...].astype(F32)) * keep_prev
        extg_ref[HALO:, :] = p_ref[:, 4 * W:5 * W].astype(F32) * _sig(p_ref[:, 5 * W:6 * W].astype(F32))
        base_b = HALO - (CONV_B - 1)
        for cs in chunks:
            _shift_copies(shd_ref, extd_ref, cs)
            _shift_copies(shg_ref, extg_ref, cs)
            dgb = _taps_rev(extd_ref, wb_ref, CONV_B, cs, tm, shd_ref)
            bv = col(4, cs)
            sg = _sig(col(5, cs))
            dp_ref[:, 4 * W + cs.start:4 * W + cs.stop] = (dgb * sg).astype(BF16)
            dp_ref[:, 5 * W + cs.start:5 * W + cs.stop] = (dgb * bv * sg * (1.0 - sg)).astype(BF16)
            dcb = extd_ref[0:tm, cs]
            for k in range(CONV_B):
                dwb_ref[k:k + 1, cs] += _colsum(dcb * _rows_at(extg_ref, shg_ref, base_b + k, cs, tm))

        extg_ref[0:HALO, :] = hax_ref[...].astype(F32) * hac_ref[...].astype(F32) * keep_prev
        extg_ref[HALO:, :] = p_ref[:, 2 * W:3 * W].astype(F32) * p_ref[:, 0:W].astype(F32)
        base_a = HALO - (CONV_A - 1)
        for cs in chunks:
            conv = _taps(extg_ref, wa_ref, CONV_A, base_a, cs, tm)
            az = col(3, cs)
            sz = _sig(az)
            ab = col(1, cs)
            dua = du_ref[:, cs].astype(F32)
            dya = dua * (az * sz)
            dp_ref[:, W + cs.start:W + cs.stop] = (dya * conv).astype(BF16)
            dp_ref[:, 3 * W + cs.start:3 * W + cs.stop] = (dua * (ab * conv) * _dsilu(az, sz)).astype(BF16)
            extd_ref[0:tm, cs] = dya * ab
            azf = faz_ref[:, cs].astype(F32)
            extd_ref[tm:tm + HALO, cs] = (duf_ref[:, cs].astype(F32) * (azf * _sig(azf))
                                          * fab_ref[:, cs].astype(F32) * keep_next)
        for cs in chunks:
            dca = _taps_rev(extd_ref, wa_ref, CONV_A, cs, tm)
            dp_ref[:, cs] = (dca * col(2, cs)).astype(BF16)
            dp_ref[:, 2 * W + cs.start:2 * W + cs.stop] = (dca * col(0, cs)).astype(BF16)
            dconv = extd_ref[0:tm, cs]
            for k in range(CONV_A):
                dwa_ref[k:k + 1, cs] += _colsum(dconv * extg_ref[pl.ds(base_a + k, tm), cs])

    prev = lambda j: pl.BlockSpec((HALO, W), lambda i: (jnp.maximum(i * nb - 1, 0), j))
    nxt = lambda j, w: pl.BlockSpec((HALO, w), lambda i: (jnp.minimum((i + 1) * nb, last_blk), j))
    row = lambda w: pl.BlockSpec((tm, w), lambda i: (i, 0))
    return _call(
        body, grid=(n_t,),
        in_specs=[row(2 * W), nxt(0, 2 * W), row(7 * W), nxt(1, W), nxt(3, W), nxt(6, W),
                  prev(0), prev(2), prev(4), prev(5), row(W), nxt(0, W),
                  _const((CONV_A, W)), _const((CONV_B, W)), _const((1, W)), _const((1, W))],
        out_specs=[row(7 * W), _const((CONV_A, W)), _const((CONV_B, W)), _const((1, W)), _const((1, W)), _const((1, W))],
        out_shape=[_sds((S, 7 * W), BF16), _sds((CONV_A, W), F32), _sds((CONV_B, W), F32),
                   _sds((1, W), F32), _sds((1, W), F32), _sds((1, W), F32)],
        operands=(du, du, p, p, p, p, p, p, p, p, cb, cb, wa, wb, ln_g, ln_b),
        scratch_shapes=[pltpu.VMEM((tm + HALO, W), F32), pltpu.VMEM((HALO + tm, W), F32),
                        pltpu.VMEM((SUBLANES - 1, HALO + tm - SUBLANES, LANES), F32),
                        pltpu.VMEM((SUBLANES - 1, HALO + tm - SUBLANES, LANES), F32)],
        name=name, params=_params(("arbitrary",), 52), comm=comm)


def _counts(i, tm, rows, off, win):
    t = i * tm + off + lax.broadcasted_iota(jnp.int32, (rows, 1), 0)
    return jnp.minimum(t + 1, win).astype(F32)


def _o_mix_fwd(q, cw, cb, cscale, *, tm, name):
    S = q.shape[0]
    WC = q.shape[1] // 2
    NG = len(POOL_WINDOWS)
    G = WC // NG
    nb = tm // PHALO

    def body(v_ref, z_ref, hv_ref, cw_ref, cb_ref, sc_ref, yy_ref, pooled_ref, gg_ref, ext_ref):
        i = pl.program_id(0)
        keep = (i > 0).astype(F32)
        for g, win in enumerate(POOL_WINDOWS):
            cs = slice(g * G, (g + 1) * G)
            v = v_ref[:, cs].astype(F32)
            ext_ref[0:PHALO, :] = hv_ref[:, cs].astype(F32) * keep
            ext_ref[PHALO:, :] = v
            s = v
            for j in range(1, win):
                s = s + ext_ref[pl.ds(PHALO - j, tm), :]
            pooled = (s / _counts(i, tm, tm, 0, win) - v).astype(BF16)
            pooled_ref[:, cs] = pooled
            gg = jnp.dot(pooled, cw_ref[g], preferred_element_type=F32) + cb_ref[:, cs]
            gg_ref[:, cs] = gg.astype(BF16)
            z = z_ref[:, cs].astype(F32)
            yy_ref[:, cs] = (gg * sc_ref[:, cs] * (z * _sig(z))).astype(BF16)

    row = lambda j: pl.BlockSpec((tm, WC), lambda i: (i, j))
    out = pl.BlockSpec((tm, WC), lambda i: (i, 0))
    return pl.pallas_call(
        body, grid=(S // tm,),
        in_specs=[row(0), row(1), pl.BlockSpec((PHALO, WC), lambda i: (jnp.maximum(i * nb - 1, 0), 0)),
                  _const((NG, G, G)), _const((1, WC)), _const((1, WC))],
        out_specs=[out, out, out],
        out_shape=[_sds((S, WC), BF16)] * 3,
        scratch_shapes=[pltpu.VMEM((PHALO + tm, G), F32)],
        name=name, compiler_params=_params(("arbitrary",), 40),
    )(q, q, q, cw, cb, cscale)


def _o_mix_bwd(dyy, q, gg, pooled, cw, cscale, *, tm, name):
    S = q.shape[0]
    WC = q.shape[1] // 2
    NG = len(POOL_WINDOWS)
    G = WC // NG
    nb = tm // PHALO
    n_t = S // tm
    last_blk = S // PHALO - 1
    nt = (((1,), (1,)), ((), ()))
    tn = (((0,), (0,)), ((), ()))

    def body(dyy_ref, dyyf_ref, z_ref, zf_ref, gg_ref, pooled_ref, cw_ref, sc_ref,
             dq_ref, dcw_ref, dcb_ref, dsc_ref, ext_ref):
        i = pl.program_id(0)
        keep_next = (i < n_t - 1).astype(F32)

        @pl.when(i == 0)
        def _():
            dcw_ref[...] = jnp.zeros_like(dcw_ref)
            dcb_ref[...] = jnp.zeros_like(dcb_ref)
            dsc_ref[...] = jnp.zeros_like(dsc_ref)

        for g, win in enumerate(POOL_WINDOWS):
            cs = slice(g * G, (g + 1) * G)
            sc = sc_ref[:, cs]
            z = z_ref[:, cs].astype(F32)
            sz = _sig(z)
            dyy_c = dyy_ref[:, cs].astype(F32)
            ggv = gg_ref[:, cs].astype(F32)
            dyy0 = dyy_c * (z * sz)
            dq_ref[:, WC + cs.start:WC + cs.stop] = (dyy_c * (ggv * sc) * _dsilu(z, sz)).astype(BF16)
            dgg = dyy0 * sc
            dsc_ref[:, cs] += _colsum(dyy0 * ggv)
            dcb_ref[:, cs] += _colsum(dgg)
            dgg_b = dgg.astype(BF16)
            dcw_ref[g] += lax.dot_general(pooled_ref[:, cs], dgg_b, tn, preferred_element_type=F32)
            dpool = lax.dot_general(dgg_b, cw_ref[g], nt, preferred_element_type=F32)
            zf = zf_ref[:, cs].astype(F32)
            dgg_f = (dyyf_ref[:, cs].astype(F32) * (zf * _sig(zf)) * sc * keep_next).astype(BF16)
            dpool_f = lax.dot_general(dgg_f, cw_ref[g], nt, preferred_element_type=F32)
            ext_ref[0:tm, :] = dpool / _counts(i, tm, tm, 0, win)
            ext_ref[tm:tm + PHALO, :] = dpool_f / _counts(i, tm, PHALO, tm, win)
            dv = ext_ref[0:tm, :] - dpool
            for j in range(1, win):
                dv = dv + ext_ref[pl.ds(j, tm), :]
            dq_ref[:, cs] = dv.astype(BF16)

    row = lambda: pl.BlockSpec((tm, WC), lambda i: (i, 0))
    nxt = lambda j: pl.BlockSpec((PHALO, WC), lambda i: (jnp.minimum((i + 1) * nb, last_blk), j))
    return pl.pallas_call(
        body, grid=(n_t,),
        in_specs=[row(), nxt(0), pl.BlockSpec((tm, WC), lambda i: (i, 1)), nxt(1), row(), row(),
                  _const((NG, G, G)), _const((1, WC))],
        out_specs=[pl.BlockSpec((tm, 2 * WC), lambda i: (i, 0)), _const((NG, G, G)), _const((1, WC)), _const((1, WC))],
        out_shape=[_sds((S, 2 * WC), BF16), _sds((NG, G, G), F32), _sds((1, WC), F32), _sds((1, WC), F32)],
        scratch_shapes=[pltpu.VMEM((tm + PHALO, G), F32)],
        name=name, compiler_params=_params(("arbitrary",), 48),
    )(dyy, dyy, q, q, gg, pooled, cw, cscale)


def _place():
    return lax.axis_index("x"), lax.axis_index("y"), lax.axis_index("c")


def _piece(ref, axis, size, index):
    start = index * size
    if axis == len(ref.shape) - 1:
        start = pl.multiple_of(start, LANES)
    idx = [slice(None)] * len(ref.shape)
    idx[axis] = pl.ds(start, size)
    return ref.at[tuple(idx)]


def _gather_full(shards, axes):
    n = len(shards)
    sizes = [s.shape[a] for s, a in zip(shards, axes)]
    full = [_sds(s.shape[:a] + (N_DEV * s.shape[a],) + s.shape[a + 1:], s.dtype) for s, a in zip(shards, axes)]

    def plan(ins, outs, sems):
        send_sems, recv_sems, local_sems = sems
        x, y, c = _place()
        me, sibling = (x, y, c), (x, y, 1 - c)
        chips = [(1 - x, y), (x, 1 - y), (1 - x, 1 - y)]

        def copy(t, k, owner, to, src=None):
            blk = _piece(outs[t], axes[t], sizes[t], 4 * owner[0] + 2 * owner[1] + owner[2])
            return pltpu.make_async_remote_copy(
                src_ref=blk if src is None else src, dst_ref=blk,
                send_sem=send_sems.at[7 * t + k], recv_sem=recv_sems.at[7 * t + k], device_id=to, device_id_type=MESH)

        mine = [pltpu.make_async_copy(ins[t], _piece(outs[t], axes[t], sizes[t], 4 * x + 2 * y + c), local_sems.at[t])
                for t in range(n)]
        return me, sibling, chips, copy, mine

    def start(ins, outs, sems):
        me, sibling, chips, copy, mine = plan(ins, outs, sems)
        for t in range(n):
            mine[t].start()
            copy(t, 0, me, sibling, src=ins[t]).start()
            for j, chip in enumerate(chips):
                copy(t, 1 + j, me, (*chip, me[2]), src=ins[t]).start()

    def middle(ins, outs, sems):
        me, sibling, chips, copy, mine = plan(ins, outs, sems)
        for j, chip in enumerate(chips):
            for t in range(n):
                copy(t, 1 + j, (*chip, me[2]), me).wait_recv()
                copy(t, 4 + j, (*chip, me[2]), sibling).start()

    def finish(ins, outs, sems):
        me, sibling, chips, copy, mine = plan(ins, outs, sems)
        for t in range(n):
            copy(t, 0, sibling, me).wait_recv()
            for j, chip in enumerate(chips):
                copy(t, 4 + j, (*chip, sibling[2]), me).wait_recv()
            for k in range(7):
                copy(t, k, me, sibling).wait_send()
            mine[t].wait()

    sems = [pltpu.SemaphoreType.DMA((7 * n,)), pltpu.SemaphoreType.DMA((7 * n,)), pltpu.SemaphoreType.DMA((n,))]
    return _Comm(shards, full, sems, start, finish, middle=middle)


def _gather_stage1(shards, axes):
    n = len(shards)
    sizes = [s.shape[a] for s, a in zip(shards, axes)]
    full = [_sds(s.shape[:a] + (N_DEV * s.shape[a],) + s.shape[a + 1:], s.dtype) for s, a in zip(shards, axes)]

    def copies(ins, outs, sems):
        send_sems, recv_sems, local_sems = sems
        x, y, c = _place()
        peers = [(x, y, 1 - c), (1 - x, y, c), (x, 1 - y, c), (1 - x, 1 - y, c)]
        out = []
        for t in range(n):
            blk = _piece(outs[t], axes[t], sizes[t], 4 * x + 2 * y + c)
            out.append(pltpu.make_async_copy(ins[t], blk, local_sems.at[t]))
            for k, peer in enumerate(peers):
                out.append(pltpu.make_async_remote_copy(
                    src_ref=ins[t], dst_ref=blk, send_sem=send_sems.at[4 * t + k], recv_sem=recv_sems.at[4 * t + k],
                    device_id=peer, device_id_type=MESH))
        return out

    def start(ins, outs, sems):
        for cp in copies(ins, outs, sems):
            cp.start()

    def finish(ins, outs, sems):
        for cp in copies(ins, outs, sems):
            cp.wait()

    sems = [pltpu.SemaphoreType.DMA((4 * n,)), pltpu.SemaphoreType.DMA((4 * n,)), pltpu.SemaphoreType.DMA((n,))]
    return _Comm(shards, full, sems, start, finish)


def _gather_stage2(fulls, axes):
    n = len(fulls)
    sizes = [f.shape[a] // N_DEV for f, a in zip(fulls, axes)]

    def copies(ins, outs, sems):
        send_sems, recv_sems = sems
        x, y, c = _place()
        out = []
        for t in range(n):
            for j, (qx, qy) in enumerate([(1 - x, y), (x, 1 - y), (1 - x, 1 - y)]):
                d = 4 * qx + 2 * qy + c
                out.append(pltpu.make_async_remote_copy(
                    src_ref=_piece(ins[t], axes[t], sizes[t], d), dst_ref=_piece(outs[t], axes[t], sizes[t], d),
                    send_sem=send_sems.at[3 * t + j], recv_sem=recv_sems.at[3 * t + j],
                    device_id=(x, y, 1 - c), device_id_type=MESH))
        return out

    def start(ins, outs, sems):
        for cp in copies(ins, outs, sems):
            cp.start()

    def finish(ins, outs, sems):
        for cp in copies(ins, outs, sems):
            cp.wait()

    sems = [pltpu.SemaphoreType.DMA((3 * n,)), pltpu.SemaphoreType.DMA((3 * n,))]
    return _Comm(fulls, [_sds(f.shape, f.dtype) for f in fulls], sems, start, finish,
                 aliases={t: t for t in range(n)})


def _pair_comm(grads, axes, sizes):
    n = len(grads)
    outs_sds = [_sds((4,) + g.shape[:a] + (s,) + g.shape[a + 1:], g.dtype) for g, a, s in zip(grads, axes, sizes)]

    def copies(ins, outs, sems):
        send_sems, recv_sems = sems
        x, y, c = _place()
        return [pltpu.make_async_remote_copy(
            src_ref=_piece(ins[t], axes[t], sizes[t], 2 * qi + (1 - c)), dst_ref=outs[t].at[qi],
            send_sem=send_sems.at[4 * t + qi], recv_sem=recv_sems.at[4 * t + qi],
            device_id=(x, y, 1 - c), device_id_type=MESH) for t in range(n) for qi in range(4)]

    def start(ins, outs, sems):
        for cp in copies(ins, outs, sems):
            cp.start()

    def finish(ins, outs, sems):
        for cp in copies(ins, outs, sems):
            cp.wait()

    sems = [pltpu.SemaphoreType.DMA((4 * n,)), pltpu.SemaphoreType.DMA((4 * n,))]
    return _Comm(grads, outs_sds, sems, start, finish)


def _chip_comm(sums):
    n = len(sums)
    outs_sds = [_sds((3,) + s.shape[1:], s.dtype) for s in sums]

    def copies(ins, outs, sems):
        send_sems, recv_sems = sems
        x, y, c = _place()
        return [pltpu.make_async_remote_copy(
            src_ref=ins[t].at[2 * qx + qy], dst_ref=outs[t].at[j],
            send_sem=send_sems.at[3 * t + j], recv_sem=recv_sems.at[3 * t + j],
            device_id=(qx, qy, c), device_id_type=MESH)
            for t in range(n) for j, (qx, qy) in enumerate([(1 - x, y), (x, 1 - y), (1 - x, 1 - y)])]

    def start(ins, outs, sems):
        for cp in copies(ins, outs, sems):
            cp.start()

    def finish(ins, outs, sems):
        for cp in copies(ins, outs, sems):
            cp.wait()

    sems = [pltpu.SemaphoreType.DMA((3 * n,)), pltpu.SemaphoreType.DMA((3 * n,))]
    return _Comm(sums, outs_sds, sems, start, finish)


def _small_comm(small):
    def copies(ins, outs, sems):
        send_sems, recv_sems, local_sem = sems
        x, y, c = _place()
        mine = outs[0].at[4 * x + 2 * y + c]
        out = [pltpu.make_async_copy(ins[0], mine, local_sem.at[0])]
        for k in range(1, N_DEV):
            peer = (1 - x if k & 4 else x, 1 - y if k & 2 else y, 1 - c if k & 1 else c)
            out.append(pltpu.make_async_remote_copy(
                src_ref=ins[0], dst_ref=mine, send_sem=send_sems.at[k - 1], recv_sem=recv_sems.at[k - 1],
                device_id=peer, device_id_type=MESH))
        return out

    def start(ins, outs, sems):
        for cp in copies(ins, outs, sems):
            cp.start()

    def finish(ins, outs, sems):
        for cp in copies(ins, outs, sems):
            cp.wait()

    sems = [pltpu.SemaphoreType.DMA((N_DEV - 1,)), pltpu.SemaphoreType.DMA((N_DEV - 1,)), pltpu.SemaphoreType.DMA((1,))]
    return _Comm([small], [_sds((N_DEV,) + small.shape, small.dtype)], sems, start, finish)


def _pair_sum(c_idx, grad, recv, axis, size, split, *, name):
    nd = len(grad.shape)
    piece = grad.shape[:axis] + (size,) + grad.shape[axis + 1:]
    blk = (piece[0] // split,) + piece[1:]

    def g_map(q, r, c_ref):
        idx = [0] * nd
        idx[axis] = 2 * q + c_ref[0]
        idx[0] = idx[0] * split + r if axis == 0 else r
        return tuple(idx)

    def r_map(q, r, c_ref):
        return (q, r) + (0,) * (nd - 1)

    def body(c_ref, g_ref, r_ref, o_ref):
        o_ref[0] = (g_ref[...].astype(F32) + r_ref[0].astype(F32)).astype(BF16)

    return pl.pallas_call(
        body,
        grid_spec=pltpu.PrefetchScalarGridSpec(
            num_scalar_prefetch=1, grid=(4, split),
            in_specs=[pl.BlockSpec(blk, g_map), pl.BlockSpec((1,) + blk, r_map)],
            out_specs=pl.BlockSpec((1,) + blk, r_map)),
        out_shape=_sds((4,) + piece, BF16),
        name=name, compiler_params=_params(("arbitrary", "arbitrary"), 32),
    )(c_idx, grad, recv)


def _adam_math(w, g, m, v):
    m = ADAM_B1 * m + (1.0 - ADAM_B1) * g
    v = ADAM_B2 * v + (1.0 - ADAM_B2) * (g * g)
    m_hat = m / (1.0 - ADAM_B1 ** ADAM_STEP)
    v_hat = v / (1.0 - ADAM_B2 ** ADAM_STEP)
    delta = -ADAM_LR * (m_hat / (jnp.sqrt(v_hat) + ADAM_EPS) + ADAM_WD * w)
    return delta, m, v


def _adam_big(q_idx, sums, recv, w, m, v, split, *, name):
    shape = w.shape
    nd = len(shape)
    blk = (shape[0] // split,) + shape[1:]
    w_map = lambda r, q_ref: (r,) + (0,) * (nd - 1)
    s_map = lambda r, q_ref: (q_ref[0], r) + (0,) * (nd - 1)
    r_map = lambda r, q_ref: (0, r) + (0,) * (nd - 1)

    def body(q_ref, s_ref, r_ref, w_ref, m_ref, v_ref, g_ref, d_ref, nm_ref, nv_ref):
        g = s_ref[0].astype(F32) + r_ref[0].astype(F32) + r_ref[1].astype(F32) + r_ref[2].astype(F32)
        g_ref[...] = g
        d_ref[...], nm_ref[...], nv_ref[...] = _adam_math(w_ref[...], g, m_ref[...], v_ref[...])

    wspec = pl.BlockSpec(blk, w_map)
    return pl.pallas_call(
        body,
        grid_spec=pltpu.PrefetchScalarGridSpec(
            num_scalar_prefetch=1, grid=(split,),
            in_specs=[pl.BlockSpec((1,) + blk, s_map), pl.BlockSpec((3,) + blk, r_map), wspec, wspec, wspec],
            out_specs=[wspec] * 4),
        out_shape=[_sds(shape, F32)] * 4,
        name=name, compiler_params=_params(("arbitrary",), 32),
    )(q_idx, sums, recv, w, m, v)


def _adam_small(parts, w, m, v, *, name):
    R = w.shape[0]

    def body(p_ref, w_ref, m_ref, v_ref, g_ref, d_ref, nm_ref, nv_ref):
        g = p_ref[0]
        for d in range(1, N_DEV):
            g = g + p_ref[d]
        g_ref[...] = g
        d_ref[...], nm_ref[...], nv_ref[...] = _adam_math(w_ref[...], g, m_ref[...], v_ref[...])

    return pl.pallas_call(
        body, out_shape=[_sds((R, LANES), F32)] * 4, name=name,
        compiler_params=pltpu.CompilerParams(vmem_limit_bytes=32 * MIB),
    )(parts, w, m, v)


def _pack(arrs):
    return jnp.concatenate([a.reshape(-1) for a in arrs]).reshape(-1, LANES)


def _unpack(packed, shapes):
    flat = packed.reshape(-1)
    out, off = [], 0
    for s in shapes:
        n = 1
        for d in s:
            n *= d
        out.append(flat[off:off + n].reshape(s))
        off += n
    return out


BIG = ("e_in", "e_out", "o_in", "o_cw", "o_out")
BIG_AXIS = dict(e_in=1, e_out=0, o_in=1, o_cw=1, o_out=0)
BIG_SPLIT = dict(e_in=8, e_out=4, o_in=4, o_cw=4, o_out=4)
REPLICATED = ("e_norm_pre", "e_norm_post", "e_b_conv_bias", "e_b_ln_g", "e_b_ln_b")
SHARDED = ("e_a_conv", "e_b_conv", "o_norm_pre", "o_norm_post", "o_c_b", "o_c_scale")
SMALL = REPLICATED + SHARDED


class _Exchange:
    def __init__(self, shards, small, order, c_idx):
        self.shards = shards
        self.small = small
        self.order = order
        self.c_idx = c_idx
        self.reduced = {}

    def gather(self, keys):
        return _gather_full([self.shards[k] for k in keys], [BIG_AXIS[k] for k in keys])

    def gather1(self, keys):
        return _gather_stage1([self.shards[k] for k in keys], [BIG_AXIS[k] for k in keys])

    def gather2(self, keys, fulls):
        return _gather_stage2(fulls, [BIG_AXIS[k] for k in keys])

    def pair(self, grads):
        keys = list(grads)
        return _pair_comm([grads[k] for k in keys], [BIG_AXIS[k] for k in keys],
                          [grads[k].shape[BIG_AXIS[k]] // N_DEV for k in keys])

    def pair_sums(self, grads, received):
        return {k: _pair_sum(self.c_idx, grads[k], r, BIG_AXIS[k], grads[k].shape[BIG_AXIS[k]] // N_DEV,
                             BIG_SPLIT[k], name="pair_sum_" + k) for k, r in zip(grads, received)}

    def chips(self, sums):
        return _chip_comm([sums[k] for k in sums])

    def done(self, sums, received):
        self.reduced.update({k: (sums[k], r) for k, r in zip(sums, received)})


def _local_step(x, tgt, w_small, ex):
    S, D = x.shape
    tnt, tx = min(TM_NT, S), min(TM_MIX, S)

    h0, got = _rms_norm(x, w_small["e_norm_pre"], tm=tx, name="e_norm", comm=_small_comm(ex.small))
    per_dev = [_unpack(got[0][d], [w_small[k].shape for k in SHARDED]) for d in range(N_DEV)]
    sm = {k: w_small[k] for k in REPLICATED}
    for j, k in enumerate(SHARDED):
        sm[k] = jnp.concatenate([per_dev[d][j] for d in range(N_DEV)], axis=-1)
    n_groups = sm["o_c_b"].shape[0]
    sm["o_c_b"] = sm["o_c_b"].reshape(1, -1)

    wt = {}
    p, wt["e_in"], _ = _gather_matmul(ex.order, h0, ex.shards["e_in"], tm=tnt, name="e_in_fwd")
    W = p.shape[1] // 7
    (u, cb), got = _e_mix_fwd(p, sm["e_a_conv"], sm["e_b_conv"], sm["e_b_conv_bias"], sm["e_b_ln_g"],
                              sm["e_b_ln_b"], tm=tx, name="e_mix_fwd", comm=ex.gather(["e_out"]))
    wt["e_out"] = got[0]
    late = ["o_out", "o_cw"]
    (x1, y0), part = _out_norm_res(u, wt["e_out"], x, sm["e_norm_post"], tm=tx, name="e_out_fwd",
                                   comm=ex.gather1(late))
    h1, _ = _rms_norm(x1, sm["o_norm_pre"], tm=tx, name="o_norm")
    q, wt["o_in"], got = _gather_matmul(ex.order, h1, ex.shards["o_in"], tm=tnt, name="o_in_fwd",
                                        comm=ex.gather2(late, part))
    wt.update(zip(late, got))
    yy, pooled, gg = _o_mix_fwd(q, wt["o_cw"], sm["o_c_b"], sm["o_c_scale"], tm=tx, name="o_mix_fwd")
    dout, dx2, dyy, lcol, dg_o_post = _out_loss(yy, wt["o_out"], x1, sm["o_norm_post"], tgt, tm=tx, name="o_out_loss")
    loss = (0.5 / D) * jnp.sum(lcol)

    dq, d_cw, d_cb, d_cscale = _o_mix_bwd(dyy, q, gg, pooled, wt["o_cw"], sm["o_c_scale"], tm=tx, name="o_mix_bwd")
    g_o_out, _ = _mm_tn(yy, dout, ts=tnt, tn=W, name="o_out_dw")
    ga = dict(o_out=g_o_out, o_cw=d_cw.astype(BF16))
    dh1, ra = _mm_nt(dq, wt["o_in"], tm=tnt, tk=W, name="o_in_bwd", comm=ex.pair(ga))
    sa = ex.pair_sums(ga, ra)
    (dx1, dy0, dg_o_pre, dg_e_post), ra = _pre_bwd_o(dh1, x1, dx2, y0, sm["o_norm_pre"], sm["e_norm_post"],
                                                     tm=tx, name="o_pre_bwd", comm=ex.chips(sa))
    ex.done(sa, ra)
    g_o_in, _ = _mm_tn(h1, dq, ts=tnt, tn=W, name="o_in_dw")
    gb = dict(o_in=g_o_in)
    du, rb = _mm_nt(dy0, wt["e_out"], tm=tnt, tk=W, name="e_out_bwd", comm=ex.pair(gb))
    sb = ex.pair_sums(gb, rb)
    g_e_out, _ = _mm_tn(u, dy0, ts=tnt, tn=W, name="e_out_dw")
    gc = dict(e_out=g_e_out)
    (dp, d_wa, d_wb, d_bias, d_lg, d_lb), rbc = _e_mix_bwd(
        du, p, cb, sm["e_a_conv"], sm["e_b_conv"], sm["e_b_ln_g"], sm["e_b_ln_b"], tm=tx, name="e_mix_bwd",
        comm=_merge(ex.chips(sb), ex.pair(gc)))
    ex.done(sb, rbc[:1])
    sc = ex.pair_sums(gc, rbc[1:])
    g_e_in, rc = _mm_tn(h0, dp, ts=tnt, tn=W, name="e_in_dw", comm=ex.chips(sc))
    ex.done(sc, rc)
    gd = dict(e_in=g_e_in)
    sd = ex.pair_sums(gd, _run_comm(ex.pair(gd), "pair_e_in"))
    dh0, rd = _mm_nt(dp, wt["e_in"], tm=tnt, tk=W, name="e_in_bwd", comm=ex.chips(sd))
    ex.done(sd, rd)
    grad_x, dg_e_pre = _pre_bwd_e(dh0, x, dx1, sm["e_norm_pre"], tm=tx, name="e_pre_bwd")

    small = dict(e_norm_pre=dg_e_pre, e_norm_post=dg_e_post, e_a_conv=d_wa, e_b_conv=d_wb, e_b_conv_bias=d_bias,
                 e_b_ln_g=d_lg, e_b_ln_b=d_lb, o_norm_pre=dg_o_pre, o_norm_post=dg_o_post,
                 o_c_b=d_cb.reshape(n_groups, -1), o_c_scale=d_cscale)
    return loss, grad_x, small


def kernel(x, e_norm_pre, e_norm_post, e_w_in, e_a_conv, e_b_conv, e_b_conv_bias, e_b_ln_g, e_b_ln_b, e_w_out, o_norm_pre, o_norm_post, o_w_in, o_c_w, o_c_b, o_c_scale, o_w_out, loss_target, m_e_norm_pre, m_e_norm_post, m_e_w_in, m_e_a_conv, m_e_b_conv, m_e_b_conv_bias, m_e_b_ln_g, m_e_b_ln_b, m_e_w_out, m_o_norm_pre, m_o_norm_post, m_o_w_in, m_o_c_w, m_o_c_b, m_o_c_scale, m_o_w_out, v_e_norm_pre, v_e_norm_post, v_e_w_in, v_e_a_conv, v_e_b_conv, v_e_b_conv_bias, v_e_b_ln_g, v_e_b_ln_b, v_e_w_out, v_o_norm_pre, v_o_norm_post, v_o_w_in, v_o_c_w, v_o_c_b, v_o_c_scale, v_o_w_out):
    xi, yi, ci = _place()
    me = 4 * xi + 2 * yi + ci
    w_big = dict(e_in=e_w_in[0], e_out=e_w_out[0], o_in=o_w_in[0], o_cw=o_c_w[0], o_out=o_w_out[0])
    m_big = dict(e_in=m_e_w_in[0], e_out=m_e_w_out[0], o_in=m_o_w_in[0], o_cw=m_o_c_w[0], o_out=m_o_w_out[0])
    v_big = dict(e_in=v_e_w_in[0], e_out=v_e_w_out[0], o_in=v_o_w_in[0], o_cw=v_o_c_w[0], o_out=v_o_w_out[0])
    w_small = dict(e_norm_pre=e_norm_pre, e_norm_post=e_norm_post, e_b_conv_bias=e_b_conv_bias, e_b_ln_g=e_b_ln_g,
                   e_b_ln_b=e_b_ln_b, e_a_conv=e_a_conv[0], e_b_conv=e_b_conv[0], o_norm_pre=o_norm_pre,
                   o_norm_post=o_norm_post, o_c_b=o_c_b[0], o_c_scale=o_c_scale)
    m_small = dict(e_norm_pre=m_e_norm_pre, e_norm_post=m_e_norm_post, e_b_conv_bias=m_e_b_conv_bias,
                   e_b_ln_g=m_e_b_ln_g, e_b_ln_b=m_e_b_ln_b, e_a_conv=m_e_a_conv[0], e_b_conv=m_e_b_conv[0],
                   o_norm_pre=m_o_norm_pre, o_norm_post=m_o_norm_post, o_c_b=m_o_c_b[0], o_c_scale=m_o_c_scale)
    v_small = dict(e_norm_pre=v_e_norm_pre, e_norm_post=v_e_norm_post, e_b_conv_bias=v_e_b_conv_bias,
                   e_b_ln_g=v_e_b_ln_g, e_b_ln_b=v_e_b_ln_b, e_a_conv=v_e_a_conv[0], e_b_conv=v_e_b_conv[0],
                   o_norm_pre=v_o_norm_pre, o_norm_post=v_o_norm_post, o_c_b=v_o_c_b[0], o_c_scale=v_o_c_scale)

    c_idx = jnp.reshape(ci, (1,)).astype(jnp.int32)
    order = jnp.stack([2 * xi + yi, 2 * (1 - xi) + yi, 2 * xi + (1 - yi), 2 * (1 - xi) + (1 - yi)]).astype(jnp.int32)
    ex = _Exchange({k: w_big[k].astype(BF16) for k in BIG}, _pack([w_small[k] for k in SHARDED]), order, c_idx)
    loss, grad_x, g_small = _local_step(x[0], loss_target[0], w_small, ex)
    loss = lax.psum(loss, ("x", "y", "c"))

    full_shapes = [g_small[k].shape for k in SMALL]
    small_parts = _run_comm(_small_comm(_pack([g_small[k] for k in SMALL])), "small_grad_exchange")[0]

    q_idx = jnp.reshape(2 * xi + yi, (1,)).astype(jnp.int32)
    big_out = {k: _adam_big(q_idx, *ex.reduced[k], w_big[k], m_big[k], v_big[k], BIG_SPLIT[k], name="adam_" + k)
               for k in BIG}

    def at_full_size(d):
        out = []
        for k, s in zip(SMALL, full_shapes):
            if k in REPLICATED:
                out.append(d[k])
            else:
                n = d[k].shape[-1]
                out.append(lax.dynamic_update_slice_in_dim(jnp.ones(s, F32), d[k], me * n, axis=-1))
        return _pack(out)

    res_small = _adam_small(small_parts, at_full_size(w_small), at_full_size(m_small), at_full_size(v_small),
                            name="adam_small")
    small_out = {k: [] for k in SMALL}
    for packed in res_small:
        for k, full in zip(SMALL, _unpack(packed, full_shapes)):
            if k in SHARDED:
                n = w_small[k].shape[-1]
                full = lax.dynamic_slice_in_dim(full, me * n, n, axis=-1)
            small_out[k].append(full)

    big_of = dict(e_w_in="e_in", e_w_out="e_out", o_w_in="o_in", o_c_w="o_cw", o_w_out="o_out")
    stacked = ("e_a_conv", "e_b_conv", "o_c_b")

    def leaf(name, which):
        if name in big_of:
            return big_out[big_of[name]][which][None]
        t = small_out[name][which]
        return t[None] if name in stacked else t

    order = ("e_norm_pre", "e_norm_post", "e_w_in", "e_a_conv", "e_b_conv", "e_b_conv_bias", "e_b_ln_g", "e_b_ln_b",
             "e_w_out", "o_norm_pre", "o_norm_post", "o_w_in", "o_c_w", "o_c_b", "o_c_scale", "o_w_out")
    outs = [loss, grad_x[None]]
    for which in range(4):
        outs += [leaf(nm, which) for nm in order]
    return tuple(outs)
```

```python
import jax
import jax.numpy as jnp
from jax import lax
from jax.experimental import pallas as pl
from jax.experimental.pallas import tpu as pltpu

F32 = jnp.float32
BF16 = jnp.bfloat16
EPS = 1e-6
MESH = pl.DeviceIdType.MESH
ANY = pl.BlockSpec(memory_space=pl.ANY)

N_DEV = 8
HALO = 32
PHALO = 16
CONV_A = 3
CONV_B = 31
POOL_WINDOWS = (2, 4, 8, 16)
LANES = 128
MIB = 1024 * 1024

ADAM_LR = 0.001
ADAM_B1 = 0.9
ADAM_B2 = 0.999
ADAM_EPS = 1e-08
ADAM_WD = 0.01
ADAM_STEP = 10

TM_NT = 1024
TM_MIX = 256


def _sds(shape, dtype):
    return jax.ShapeDtypeStruct(tuple(shape), dtype)


def _params(sem, vmem_mib):
    return pltpu.CompilerParams(dimension_semantics=sem, vmem_limit_bytes=vmem_mib * MIB)


def _const(shape, single=False):
    n = len(shape)
    if single:
        return pl.BlockSpec(shape, lambda *_: (0,) * n, pipeline_mode=pl.Buffered(1))
    return pl.BlockSpec(shape, lambda *_: (0,) * n)


def _sig(v):
    return jax.nn.sigmoid(v)


def _dsilu(v, s):
    return s * (1.0 + v * (1.0 - s))


def _rms(v):
    return lax.rsqrt(jnp.mean(v * v, axis=-1, keepdims=True) + EPS)


def _norm_bwd(dn, n, r):
    return r * (dn - n * jnp.mean(dn * n, axis=-1, keepdims=True))


def _colsum(v):
    return jnp.sum(v, axis=0, keepdims=True)


class _Comm:
    def __init__(self, inputs, out_shapes, sems, start, finish, aliases=None, middle=None):
        self.inputs, self.out_shapes, self.sems = list(inputs), list(out_shapes), list(sems)
        self.start, self.finish, self.middle = start, finish, middle
        self.aliases = dict(aliases or {})


def _merge(*comms):
    comms = [c for c in comms if c is not None]
    if len(comms) <= 1:
        return comms[0] if comms else None
    spans, i0, o0, s0, aliases = [], 0, 0, 0, {}
    for c in comms:
        spans.append((i0, o0, s0))
        aliases.update({i0 + k: o0 + v for k, v in c.aliases.items()})
        i0, o0, s0 = i0 + len(c.inputs), o0 + len(c.out_shapes), s0 + len(c.sems)

    def run(which):
        def fn(ins, outs, sems):
            for c, (i, o, s) in zip(comms, spans):
                hook = getattr(c, which)
                if hook is not None:
                    hook(ins[i:i + len(c.inputs)], outs[o:o + len(c.out_shapes)], sems[s:s + len(c.sems)])
        return fn

    return _Comm([a for c in comms for a in c.inputs], [a for c in comms for a in c.out_shapes],
                 [a for c in comms for a in c.sems], run("start"), run("finish"), aliases,
                 run("middle") if any(c.middle is not None for c in comms) else None)


def _call(body, *, grid, in_specs, out_specs, out_shape, operands, name, params, scratch_shapes=(), comm=None,
          prefetch=None):
    n_p = 0 if prefetch is None else 1
    n_i, n_o, n_s = len(in_specs), len(out_specs), len(scratch_shapes)
    if comm is None:
        comm = _Comm([], [], [], None, None)
    c_i, c_o = len(comm.inputs), len(comm.out_shapes)

    def carrier(*refs):
        pre, refs = refs[:n_p], refs[n_p:]
        ins, cins = refs[:n_i], refs[n_i:n_i + c_i]
        outs = refs[n_i + c_i:n_i + c_i + n_o]
        couts = refs[n_i + c_i + n_o:n_i + c_i + n_o + c_o]
        scr = refs[n_i + c_i + n_o + c_o:n_i + c_i + n_o + c_o + n_s]
        csems = refs[n_i + c_i + n_o + c_o + n_s:]
        ids = [pl.program_id(d) for d in range(len(grid))]
        first = ids[0] == 0
        half = ids[0] == grid[0] // 2
        last = ids[0] == grid[0] - 1
        for d in range(1, len(grid)):
            first = first & (ids[d] == 0)
            half = half & (ids[d] == 0)
            last = last & (ids[d] == grid[d] - 1)

        if comm.start is not None:
            @pl.when(first)
            def _():
                comm.start(cins, couts, csems)

        if comm.middle is not None:
            assert grid[0] >= 2

            @pl.when(half)
            def _():
                comm.middle(cins, couts, csems)

        body(*pre, *ins, *outs, *scr)

        if comm.finish is not None:
            @pl.when(last)
            def _():
                comm.finish(cins, couts, csems)

    specs = dict(grid=grid, in_specs=list(in_specs) + [ANY] * c_i, out_specs=list(out_specs) + [ANY] * c_o,
                 scratch_shapes=list(scratch_shapes) + comm.sems)
    if n_p:
        specs = dict(grid_spec=pltpu.PrefetchScalarGridSpec(num_scalar_prefetch=1, **specs))
    res = pl.pallas_call(
        carrier, out_shape=list(out_shape) + comm.out_shapes,
        input_output_aliases={n_p + n_i + k: n_o + v for k, v in comm.aliases.items()},
        name=name, compiler_params=params, **specs)(*(() if prefetch is None else (prefetch,)), *operands, *comm.inputs)
    return list(res[:n_o]), list(res[n_o:])


def _run_comm(comm, name):
    c_i, c_o = len(comm.inputs), len(comm.out_shapes)

    def body(*refs):
        ins, outs, sems = refs[:c_i], refs[c_i:c_i + c_o], refs[c_i + c_o:]
        comm.start(ins, outs, sems)
        comm.finish(ins, outs, sems)

    res = pl.pallas_call(
        body, in_specs=[ANY] * c_i, out_specs=[ANY] * c_o, out_shape=comm.out_shapes, scratch_shapes=comm.sems,
        input_output_aliases=comm.aliases, name=name)(*comm.inputs)
    return list(res)


def _rms_norm(x, g, *, tm, name, comm=None):
    S, D = x.shape

    def body(x_ref, g_ref, h_ref):
        xx = x_ref[...]
        h_ref[...] = ((xx * _rms(xx)) * g_ref[...]).astype(BF16)

    row = pl.BlockSpec((tm, D), lambda i: (i, 0))
    outs, extra = _call(body, grid=(S // tm,), in_specs=[row, _const((1, D))], out_specs=[row],
                        out_shape=[_sds((S, D), BF16)], operands=(x, g), name=name,
                        params=_params(("arbitrary",), 32), comm=comm)
    return outs[0], extra


def _gather_matmul(order, h, shard, *, tm, name, comm=None):
    S, K = h.shape
    nb = shard.shape[1]
    n_i = S // tm

    def body(order_ref, h_ref, shard_ref, p_ref, full_ref, wbuf, send_sems, recv_sems, dma_sems):
        j, i = pl.program_id(0), pl.program_id(1)
        x, y, c = _place()
        cps = _gather_copies(shard_ref, full_ref, 1, nb, send_sems, recv_sems, 0)

        def load(qx, qy):
            cp = pltpu.make_async_copy(_piece(full_ref, 1, 2 * nb, 2 * qx + qy), wbuf, dma_sems.at[1])
            cp.start()
            cp.wait()

        @pl.when((j == 0) & (i == 0))
        def _():
            own = pltpu.make_async_copy(shard_ref, _piece(full_ref, 1, nb, 4 * x + 2 * y + c), dma_sems.at[0])
            own.start()
            for k in (0, 1, 2):
                cps[k].start()
            own.wait()
            cps[0].wait_recv()
            load(x, y)

        @pl.when((j == 1) & (i == 0))
        def _():
            cps[1].wait_recv()
            cps[3].start()
            cps[5].start()
            cps[2].wait_recv()
            cps[4].start()
            cps[6].start()
            cps[5].wait_recv()
            load(1 - x, y)

        @pl.when((j == 2) & (i == 0))
        def _():
            cps[6].wait_recv()
            load(x, 1 - y)

        @pl.when((j == 3) & (i == 0))
        def _():
            cps[3].wait_recv()
            cps[4].wait_recv()
            cps[7].start()
            cps[7].wait_recv()
            load(1 - x, 1 - y)

        p_ref[...] = jnp.dot(h_ref[...], wbuf[...], preferred_element_type=F32).astype(BF16)

        @pl.when((j == 3) & (i == n_i - 1))
        def _():
            for cp in cps:
                cp.wait_send()

    outs, extra = _call(
        body, grid=(4, n_i), prefetch=order,
        in_specs=[pl.BlockSpec((tm, K), lambda j, i, o: (i, 0)), ANY],
        out_specs=[pl.BlockSpec((tm, 2 * nb), lambda j, i, o: (i, o[j])), ANY],
        out_shape=[_sds((S, N_DEV * nb), BF16), _sds((K, N_DEV * nb), BF16)], operands=(h, shard),
        scratch_shapes=[pltpu.VMEM((K, 2 * nb), BF16), pltpu.SemaphoreType.DMA((N_GATHER,)),
                        pltpu.SemaphoreType.DMA((N_GATHER,)), pltpu.SemaphoreType.DMA((2,))],
        name=name, params=_params(("arbitrary", "arbitrary"), 48), comm=comm)
    return outs[0], outs[1], extra


def _out_norm_res(u, w, x, g, *, tm, name, comm=None):
    S, K = u.shape
    D = w.shape[1]

    def body(u_ref, w_ref, x_ref, g_ref, x1_ref, y_ref):
        y = jnp.dot(u_ref[...], w_ref[...], preferred_element_type=F32)
        y_ref[...] = y.astype(BF16)
        x1_ref[...] = x_ref[...] + (y * _rms(y)) * g_ref[...]

    return _call(
        body, grid=(S // tm,),
        in_specs=[pl.BlockSpec((tm, K), lambda i: (i, 0)), _const((K, D), single=True),
                  pl.BlockSpec((tm, D), lambda i: (i, 0)), _const((1, D))],
        out_specs=[pl.BlockSpec((tm, D), lambda i: (i, 0)), pl.BlockSpec((tm, D), lambda i: (i, 0))],
        out_shape=[_sds((S, D), F32), _sds((S, D), BF16)], operands=(u, w, x, g),
        name=name, params=_params(("arbitrary",), 48), comm=comm)


def _out_loss(yy, w, x1, g, tgt, *, tm, name):
    S, K = yy.shape
    D = w.shape[1]

    def body(yy_ref, w_ref, x1_ref, g_ref, t_ref, dout_ref, dx2_ref, dyy_ref, lcol_ref, dg_ref):
        out = jnp.dot(yy_ref[...], w_ref[...], preferred_element_type=F32)
        r = _rms(out)
        n = out * r
        gg = g_ref[...]
        e = x1_ref[...] + n * gg - t_ref[...]
        dx2 = e * (1.0 / D)
        dx2_ref[...] = dx2
        dout = _norm_bwd(dx2 * gg, n, r).astype(BF16)
        dout_ref[...] = dout
        dyy_ref[...] = lax.dot_general(dout, w_ref[...], (((1,), (1,)), ((), ())),
                                       preferred_element_type=F32).astype(BF16)

        @pl.when(pl.program_id(0) == 0)
        def _():
            lcol_ref[...] = jnp.zeros_like(lcol_ref)
            dg_ref[...] = jnp.zeros_like(dg_ref)

        lcol_ref[...] += _colsum(e * e)
        dg_ref[...] += _colsum(dx2 * n)

    return pl.pallas_call(
        body, grid=(S // tm,),
        in_specs=[pl.BlockSpec((tm, K), lambda i: (i, 0)), _const((K, D), single=True),
                  pl.BlockSpec((tm, D), lambda i: (i, 0)), _const((1, D)),
                  pl.BlockSpec((tm, D), lambda i: (i, 0))],
        out_specs=[pl.BlockSpec((tm, D), lambda i: (i, 0)), pl.BlockSpec((tm, D), lambda i: (i, 0)),
                   pl.BlockSpec((tm, K), lambda i: (i, 0)), _const((1, D)), _const((1, D))],
        out_shape=[_sds((S, D), BF16), _sds((S, D), F32), _sds((S, K), BF16), _sds((1, D), F32), _sds((1, D), F32)],
        name=name, compiler_params=_params(("arbitrary",), 52),
    )(yy, w, x1, g, tgt)


def _mm_nt(a, w, *, tm, tk, name, comm=None):
    S, N = a.shape
    D = w.shape[0]
    n_k = N // tk

    def body(a_ref, w_ref, o_ref, acc_ref):
        k = pl.program_id(1)
        part = lax.dot_general(a_ref[...], w_ref[...], (((1,), (1,)), ((), ())), preferred_element_type=F32)

        @pl.when(k == 0)
        def _():
            acc_ref[...] = part

        @pl.when(k > 0)
        def _():
            acc_ref[...] += part

        @pl.when(k == n_k - 1)
        def _():
            o_ref[...] = acc_ref[...].astype(BF16)

    outs, extra = _call(
        body, grid=(S // tm, n_k),
        in_specs=[pl.BlockSpec((tm, tk), lambda i, k: (i, k)), pl.BlockSpec((D, tk), lambda i, k: (0, k))],
        out_specs=[pl.BlockSpec((tm, D), lambda i, k: (i, 0))],
        out_shape=[_sds((S, D), BF16)], operands=(a, w),
        scratch_shapes=[pltpu.VMEM((tm, D), F32)],
        name=name, params=_params(("arbitrary", "arbitrary"), 48), comm=comm)
    return outs[0], extra


def _mm_tn(a, b, *, ts, tn, name, comm=None):
    S, M = a.shape
    N = b.shape[1]
    n_s = S // ts

    def body(a_ref, b_ref, o_ref, acc_ref):
        s = pl.program_id(1)
        part = lax.dot_general(a_ref[...], b_ref[...], (((0,), (0,)), ((), ())), preferred_element_type=F32)

        @pl.when(s == 0)
        def _():
            acc_ref[...] = part

        @pl.when(s > 0)
        def _():
            acc_ref[...] += part

        @pl.when(s == n_s - 1)
        def _():
            o_ref[...] = acc_ref[...].astype(BF16)

    outs, extra = _call(
        body, grid=(N // tn, n_s),
        in_specs=[pl.BlockSpec((ts, M), lambda j, s: (s, 0)), pl.BlockSpec((ts, tn), lambda j, s: (s, j))],
        out_specs=[pl.BlockSpec((M, tn), lambda j, s: (0, j))],
        out_shape=[_sds((M, N), BF16)], operands=(a, b),
        scratch_shapes=[pltpu.VMEM((M, tn), F32)],
        name=name, params=_params(("arbitrary", "arbitrary"), 48), comm=comm)
    return outs[0], extra


def _pre_bwd_o(dh, x1, dx2, y0, g_pre, g_post, *, tm, name, comm=None):
    S, D = x1.shape

    def body(dh_ref, x1_ref, dx2_ref, y0_ref, gpre_ref, gpost_ref, dx1_ref, dy0_ref, dgpre_ref, dgpost_ref):
        @pl.when(pl.program_id(0) == 0)
        def _():
            dgpre_ref[...] = jnp.zeros_like(dgpre_ref)
            dgpost_ref[...] = jnp.zeros_like(dgpost_ref)

        dh = dh_ref[...].astype(F32)
        x1 = x1_ref[...]
        r2 = _rms(x1)
        xn = x1 * r2
        dgpre_ref[...] += _colsum(dh * xn)
        dx1 = dx2_ref[...] + _norm_bwd(dh * gpre_ref[...], xn, r2)
        dx1_ref[...] = dx1
        y = y0_ref[...].astype(F32)
        r1 = _rms(y)
        n1 = y * r1
        dgpost_ref[...] += _colsum(dx1 * n1)
        dy0_ref[...] = _norm_bwd(dx1 * gpost_ref[...], n1, r1).astype(BF16)

    row = pl.BlockSpec((tm, D), lambda i: (i, 0))
    return _call(
        body, grid=(S // tm,),
        in_specs=[row, row, row, row, _const((1, D)), _const((1, D))],
        out_specs=[row, row, _const((1, D)), _const((1, D))],
        out_shape=[_sds((S, D), F32), _sds((S, D), BF16), _sds((1, D), F32), _sds((1, D), F32)],
        operands=(dh, x1, dx2, y0, g_pre, g_post),
        name=name, params=_params(("arbitrary",), 48), comm=comm)


def _pre_bwd_e(dh, x, dx1, g_pre, *, tm, name):
    S, D = x.shape

    def body(dh_ref, x_ref, dx1_ref, gpre_ref, gx_ref, dgpre_ref):
        @pl.when(pl.program_id(0) == 0)
        def _():
            dgpre_ref[...] = jnp.zeros_like(dgpre_ref)

        dh = dh_ref[...].astype(F32)
        xx = x_ref[...]
        r0 = _rms(xx)
        xn = xx * r0
        dgpre_ref[...] += _colsum(dh * xn)
        gx_ref[...] = dx1_ref[...] + _norm_bwd(dh * gpre_ref[...], xn, r0)

    row = pl.BlockSpec((tm, D), lambda i: (i, 0))
    return pl.pallas_call(
        body, grid=(S // tm,),
        in_specs=[row, row, row, _const((1, D))],
        out_specs=[row, _const((1, D))],
        out_shape=[_sds((S, D), F32), _sds((1, D), F32)],
        name=name, compiler_params=_params(("arbitrary",), 48),
    )(dh, x, dx1, g_pre)


SUBLANES = 8


def _shift_copies(sh_ref, ext_ref, cs):
    for b in range(1, SUBLANES):
        sh_ref[b - 1] = ext_ref[pl.ds(b, sh_ref.shape[1]), cs]


def _rows_at(ext_ref, sh_ref, off, cs, tm):
    b = off % SUBLANES
    if b == 0 or sh_ref is None:
        return ext_ref[pl.ds(off, tm), cs]
    return sh_ref[b - 1, pl.ds(off - b, tm), :]


def _taps(ext_ref, w_ref, n_taps, base, cs, tm, sh_ref=None):
    acc = _rows_at(ext_ref, sh_ref, base, cs, tm) * w_ref[0:1, cs]
    for k in range(1, n_taps):
        acc = acc + _rows_at(ext_ref, sh_ref, base + k, cs, tm) * w_ref[k:k + 1, cs]
    return acc


def _taps_rev(ext_ref, w_ref, n_taps, cs, tm, sh_ref=None):
    acc = _rows_at(ext_ref, sh_ref, n_taps - 1, cs, tm) * w_ref[0:1, cs]
    for k in range(1, n_taps):
        acc = acc + _rows_at(ext_ref, sh_ref, n_taps - 1 - k, cs, tm) * w_ref[k:k + 1, cs]
    return acc


def _e_mix_fwd(p, wa, wb, bias, ln_g, ln_b, *, tm, name, comm=None):
    S = p.shape[0]
    W = p.shape[1] // 7
    nb = tm // HALO
    chunks = [slice(c * LANES, (c + 1) * LANES) for c in range(W // LANES)]

    def body(p_ref, hax_ref, hac_ref, hbv_ref, hbg_ref, wa_ref, wb_ref, bias_ref, lg_ref, lb_ref,
             u_ref, cb_ref, ext_ref, sh_ref):
        keep = (pl.program_id(0) > 0).astype(F32)
        col = lambda j, cs: p_ref[:, j * W + cs.start:j * W + cs.stop].astype(F32)

        ext_ref[0:HALO, :] = hax_ref[...].astype(F32) * hac_ref[...].astype(F32) * keep
        ext_ref[HALO:, :] = p_ref[:, 2 * W:3 * W].astype(F32) * p_ref[:, 0:W].astype(F32)
        for cs in chunks:
            conv = _taps(ext_ref, wa_ref, CONV_A, HALO - (CONV_A - 1), cs, tm)
            az = col(3, cs)
            u_ref[:, cs] = (col(1, cs) * conv * (az * _sig(az))).astype(BF16)

        ext_ref[0:HALO, :] = hbv_ref[...].astype(F32) * _sig(hbg_ref[...].astype(F32)) * keep
        ext_ref[HALO:, :] = p_ref[:, 4 * W:5 * W].astype(F32) * _sig(p_ref[:, 5 * W:6 * W].astype(F32))
        s1 = jnp.zeros((tm, LANES), F32)
        for cs in chunks:
            _shift_copies(sh_ref, ext_ref, cs)
            cb = _taps(ext_ref, wb_ref, CONV_B, HALO - (CONV_B - 1), cs, tm, sh_ref) + bias_ref[:, cs]
            cb_ref[:, cs] = cb
            s1 = s1 + cb
        mu = jnp.sum(s1, axis=-1, keepdims=True) * (1.0 / W)
        s2 = jnp.zeros((tm, LANES), F32)
        for cs in chunks:
            xc = cb_ref[:, cs] - mu
            s2 = s2 + xc * xc
        rs = lax.rsqrt(jnp.sum(s2, axis=-1, keepdims=True) * (1.0 / W) + EPS)
        for cs in chunks:
            lb = (cb_ref[:, cs] - mu) * rs * lg_ref[:, cs] + lb_ref[:, cs]
            bz = col(6, cs)
            u_ref[:, W + cs.start:W + cs.stop] = (lb * _sig(lb) * (bz * _sig(bz))).astype(BF16)

    prev = lambda j: pl.BlockSpec((HALO, W), lambda i: (jnp.maximum(i * nb - 1, 0), j))
    return _call(
        body, grid=(S // tm,),
        in_specs=[pl.BlockSpec((tm, 7 * W), lambda i: (i, 0)), prev(0), prev(2), prev(4), prev(5),
                  _const((CONV_A, W)), _const((CONV_B, W)), _const((1, W)), _const((1, W)), _const((1, W))],
        out_specs=[pl.BlockSpec((tm, 2 * W), lambda i: (i, 0)), pl.BlockSpec((tm, W), lambda i: (i, 0))],
        out_shape=[_sds((S, 2 * W), BF16), _sds((S, W), F32)],
        operands=(p, p, p, p, p, wa, wb, bias, ln_g, ln_b),
        scratch_shapes=[pltpu.VMEM((HALO + tm, W), F32),
                        pltpu.VMEM((SUBLANES - 1, HALO + tm - SUBLANES, LANES), F32)],
        name=name, params=_params(("arbitrary",), 48), comm=comm)


def _e_mix_bwd(du, p, cb, wa, wb, ln_g, ln_b, *, tm, name, comm=None):
    S = p.shape[0]
    W = p.shape[1] // 7
    nb = tm // HALO
    n_t = S // tm
    last_blk = S // HALO - 1
    chunks = [slice(c * LANES, (c + 1) * LANES) for c in range(W // LANES)]

    def body(du_ref, duf_ref, p_ref, fab_ref, faz_ref, fbz_ref, hax_ref, hac_ref, hbv_ref, hbg_ref,
             cb_ref, cbf_ref, wa_ref, wb_ref, lg_ref, lb_ref,
             dp_ref, dwa_ref, dwb_ref, dbias_ref, dlg_ref, dlb_ref, extd_ref, extg_ref, shd_ref, shg_ref):
        i = pl.program_id(0)
        keep_prev = (i > 0).astype(F32)
        keep_next = (i < n_t - 1).astype(F32)
        col = lambda j, cs: p_ref[:, j * W + cs.start:j * W + cs.stop].astype(F32)

        @pl.when(i == 0)
        def _():
            dwa_ref[...] = jnp.zeros_like(dwa_ref)
            dwb_ref[...] = jnp.zeros_like(dwb_ref)
            dbias_ref[...] = jnp.zeros_like(dbias_ref)
            dlg_ref[...] = jnp.zeros_like(dlg_ref)
            dlb_ref[...] = jnp.zeros_like(dlb_ref)

        def dcb_rows(rows, cb_rows_ref, dub, bz_of, dst0, scale, main):
            cbv = cb_rows_ref[...]
            mu = jnp.mean(cbv, axis=-1, keepdims=True)
            xc = cbv - mu
            rs = lax.rsqrt(jnp.mean(xc * xc, axis=-1, keepdims=True) + EPS)
            m1 = jnp.zeros((rows, LANES), F32)
            m2 = jnp.zeros((rows, LANES), F32)
            for cs in chunks:
                nbv = (cb_rows_ref[:, cs] - mu) * rs
                lb = nbv * lg_ref[:, cs] + lb_ref[:, cs]
                sl = _sig(lb)
                bz = bz_of(cs)
                sz = _sig(bz)
                dub_c = dub(cs)
                dlb = dub_c * (bz * sz) * _dsilu(lb, sl)
                if main:
                    dlg_ref[:, cs] += _colsum(dlb * nbv)
                    dlb_ref[:, cs] += _colsum(dlb)
                    dp_ref[:, 6 * W + cs.start:6 * W + cs.stop] = (dub_c * (lb * sl) * _dsilu(bz, sz)).astype(BF16)
                dnb = dlb * lg_ref[:, cs]
                extd_ref[dst0:dst0 + rows, cs] = dnb
                m1 = m1 + dnb
                m2 = m2 + dnb * nbv
            m1 = jnp.sum(m1, axis=-1, keepdims=True) * (1.0 / W)
            m2 = jnp.sum(m2, axis=-1, keepdims=True) * (1.0 / W)
            for cs in chunks:
                nbv = (cb_rows_ref[:, cs] - mu) * rs
                dcb = rs * (extd_ref[dst0:dst0 + rows, cs] - m1 - nbv * m2) * scale
                extd_ref[dst0:dst0 + rows, cs] = dcb
                if main:
                    dbias_ref[:, cs] += _colsum(dcb)

        dcb_rows(tm, cb_ref, lambda cs: du_ref[:, W + cs.start:W + cs.stop].astype(F32),
                 lambda cs: col(6, cs), 0, 1.0, True)
        dcb_rows(HALO, cbf_ref, lambda cs: duf_ref[:, W + cs.start:W + cs.stop].astype(F32),
                 lambda cs: fbz_ref[:, cs].astype(F32), tm, keep_next, False)

        extg_ref[0:HALO, :] = hbv_ref[...].astype(F32) * _sig(hbg_ref[...].astype(F32)) * keep_prev
        extg_ref[HALO:, :] = p_ref[:, 4 * W:5 * W].astype(F32) * _sig(p_ref[:, 5 * W:6 * W].astype(F32))
        base_b = HALO - (CONV_B - 1)
        for cs in chunks:
            _shift_copies(shd_ref, extd_ref, cs)
            _shift_copies(shg_ref, extg_ref, cs)
            dgb = _taps_rev(extd_ref, wb_ref, CONV_B, cs, tm, shd_ref)
            bv = col(4, cs)
            sg = _sig(col(5, cs))
            dp_ref[:, 4 * W + cs.start:4 * W + cs.stop] = (dgb * sg).astype(BF16)
            dp_ref[:, 5 * W + cs.start:5 * W + cs.stop] = (dgb * bv * sg * (1.0 - sg)).astype(BF16)
            dcb = extd_ref[0:tm, cs]
            for k in range(CONV_B):
                dwb_ref[k:k + 1, cs] += _colsum(dcb * _rows_at(extg_ref, shg_ref, base_b + k, cs, tm))

        extg_ref[0:HALO, :] = hax_ref[...].astype(F32) * hac_ref[...].astype(F32) * keep_prev
        extg_ref[HALO:, :] = p_ref[:, 2 * W:3 * W].astype(F32) * p_ref[:, 0:W].astype(F32)
        base_a = HALO - (CONV_A - 1)
        for cs in chunks:
            conv = _taps(extg_ref, wa_ref, CONV_A, base_a, cs, tm)
            az = col(3, cs)
            sz = _sig(az)
            ab = col(1, cs)
            dua = du_ref[:, cs].astype(F32)
            dya = dua * (az * sz)
            dp_ref[:, W + cs.start:W + cs.stop] = (dya * conv).astype(BF16)
            dp_ref[:, 3 * W + cs.start:3 * W + cs.stop] = (dua * (ab * conv) * _dsilu(az, sz)).astype(BF16)
            extd_ref[0:tm, cs] = dya * ab
            azf = faz_ref[:, cs].astype(F32)
            extd_ref[tm:tm + HALO, cs] = (duf_ref[:, cs].astype(F32) * (azf * _sig(azf))
                                          * fab_ref[:, cs].astype(F32) * keep_next)
        for cs in chunks:
            dca = _taps_rev(extd_ref, wa_ref, CONV_A, cs, tm)
            dp_ref[:, cs] = (dca * col(2, cs)).astype(BF16)
            dp_ref[:, 2 * W + cs.start:2 * W + cs.stop] = (dca * col(0, cs)).astype(BF16)
            dconv = extd_ref[0:tm, cs]
            for k in range(CONV_A):
                dwa_ref[k:k + 1, cs] += _colsum(dconv * extg_ref[pl.ds(base_a + k, tm), cs])

    prev = lambda j: pl.BlockSpec((HALO, W), lambda i: (jnp.maximum(i * nb - 1, 0), j))
    nxt = lambda j, w: pl.BlockSpec((HALO, w), lambda i: (jnp.minimum((i + 1) * nb, last_blk), j))
    row = lambda w: pl.BlockSpec((tm, w), lambda i: (i, 0))
    return _call(
        body, grid=(n_t,),
        in_specs=[row(2 * W), nxt(0, 2 * W), row(7 * W), nxt(1, W), nxt(3, W), nxt(6, W),
                  prev(0), prev(2), prev(4), prev(5), row(W), nxt(0, W),
                  _const((CONV_A, W)), _const((CONV_B, W)), _const((1, W)), _const((1, W))],
        out_specs=[row(7 * W), _const((CONV_A, W)), _const((CONV_B, W)), _const((1, W)), _const((1, W)), _const((1, W))],
        out_shape=[_sds((S, 7 * W), BF16), _sds((CONV_A, W), F32), _sds((CONV_B, W), F32),
                   _sds((1, W), F32), _sds((1, W), F32), _sds((1, W), F32)],
        operands=(du, du, p, p, p, p, p, p, p, p, cb, cb, wa, wb, ln_g, ln_b),
        scratch_shapes=[pltpu.VMEM((tm + HALO, W), F32), pltpu.VMEM((HALO + tm, W), F32),
                        pltpu.VMEM((SUBLANES - 1, HALO + tm - SUBLANES, LANES), F32),
                        pltpu.VMEM((SUBLANES - 1, HALO + tm - SUBLANES, LANES), F32)],
        name=name, params=_params(("arbitrary",), 52), comm=comm)


def _counts(i, tm, rows, off, win):
    t = i * tm + off + lax.broadcasted_iota(jnp.int32, (rows, 1), 0)
    return jnp.minimum(t + 1, win).astype(F32)


def _o_mix_fwd(q, cw, cb, cscale, *, tm, name):
    S = q.shape[0]
    WC = q.shape[1] // 2
    NG = len(POOL_WINDOWS)
    G = WC // NG
    nb = tm // PHALO

    def body(v_ref, z_ref, hv_ref, cw_ref, cb_ref, sc_ref, yy_ref, pooled_ref, gg_ref, ext_ref):
        i = pl.program_id(0)
        keep = (i > 0).astype(F32)
        for g, win in enumerate(POOL_WINDOWS):
            cs = slice(g * G, (g + 1) * G)
            v = v_ref[:, cs].astype(F32)
            ext_ref[0:PHALO, :] = hv_ref[:, cs].astype(F32) * keep
            ext_ref[PHALO:, :] = v
            s = v
            for j in range(1, win):
                s = s + ext_ref[pl.ds(PHALO - j, tm), :]
            pooled = (s / _counts(i, tm, tm, 0, win) - v).astype(BF16)
            pooled_ref[:, cs] = pooled
            gg = jnp.dot(pooled, cw_ref[g], preferred_element_type=F32) + cb_ref[:, cs]
            gg_ref[:, cs] = gg.astype(BF16)
            z = z_ref[:, cs].astype(F32)
            yy_ref[:, cs] = (gg * sc_ref[:, cs] * (z * _sig(z))).astype(BF16)

    row = lambda j: pl.BlockSpec((tm, WC), lambda i: (i, j))
    out = pl.BlockSpec((tm, WC), lambda i: (i, 0))
    return pl.pallas_call(
        body, grid=(S // tm,),
        in_specs=[row(0), row(1), pl.BlockSpec((PHALO, WC), lambda i: (jnp.maximum(i * nb - 1, 0), 0)),
                  _const((NG, G, G)), _const((1, WC)), _const((1, WC))],
        out_specs=[out, out, out],
        out_shape=[_sds((S, WC), BF16)] * 3,
        scratch_shapes=[pltpu.VMEM((PHALO + tm, G), F32)],
        name=name, compiler_params=_params(("arbitrary",), 40),
    )(q, q, q, cw, cb, cscale)


def _o_mix_bwd(dyy, q, gg, pooled, cw, cscale, *, tm, name):
    S = q.shape[0]
    WC = q.shape[1] // 2
    NG = len(POOL_WINDOWS)
    G = WC // NG
    nb = tm // PHALO
    n_t = S // tm
    last_blk = S // PHALO - 1
    nt = (((1,), (1,)), ((), ()))
    tn = (((0,), (0,)), ((), ()))

    def body(dyy_ref, dyyf_ref, z_ref, zf_ref, gg_ref, pooled_ref, cw_ref, sc_ref,
             dq_ref, dcw_ref, dcb_ref, dsc_ref, ext_ref):
        i = pl.program_id(0)
        keep_next = (i < n_t - 1).astype(F32)

        @pl.when(i == 0)
        def _():
            dcw_ref[...] = jnp.zeros_like(dcw_ref)
            dcb_ref[...] = jnp.zeros_like(dcb_ref)
            dsc_ref[...] = jnp.zeros_like(dsc_ref)

        for g, win in enumerate(POOL_WINDOWS):
            cs = slice(g * G, (g + 1) * G)
            sc = sc_ref[:, cs]
            z = z_ref[:, cs].astype(F32)
            sz = _sig(z)
            dyy_c = dyy_ref[:, cs].astype(F32)
            ggv = gg_ref[:, cs].astype(F32)
            dyy0 = dyy_c * (z * sz)
            dq_ref[:, WC + cs.start:WC + cs.stop] = (dyy_c * (ggv * sc) * _dsilu(z, sz)).astype(BF16)
            dgg = dyy0 * sc
            dsc_ref[:, cs] += _colsum(dyy0 * ggv)
            dcb_ref[:, cs] += _colsum(dgg)
            dgg_b = dgg.astype(BF16)
            dcw_ref[g] += lax.dot_general(pooled_ref[:, cs], dgg_b, tn, preferred_element_type=F32)
            dpool = lax.dot_general(dgg_b, cw_ref[g], nt, preferred_element_type=F32)
            zf = zf_ref[:, cs].astype(F32)
            dgg_f = (dyyf_ref[:, cs].astype(F32) * (zf * _sig(zf)) * sc * keep_next).astype(BF16)
            dpool_f = lax.dot_general(dgg_f, cw_ref[g], nt, preferred_element_type=F32)
            ext_ref[0:tm, :] = dpool / _counts(i, tm, tm, 0, win)
            ext_ref[tm:tm + PHALO, :] = dpool_f / _counts(i, tm, PHALO, tm, win)
            dv = ext_ref[0:tm, :] - dpool
            for j in range(1, win):
                dv = dv + ext_ref[pl.ds(j, tm), :]
            dq_ref[:, cs] = dv.astype(BF16)

    row = lambda: pl.BlockSpec((tm, WC), lambda i: (i, 0))
    nxt = lambda j: pl.BlockSpec((PHALO, WC), lambda i: (jnp.minimum((i + 1) * nb, last_blk), j))
    return pl.pallas_call(
        body, grid=(n_t,),
        in_specs=[row(), nxt(0), pl.BlockSpec((tm, WC), lambda i: (i, 1)), nxt(1), row(), row(),
                  _const((NG, G, G)), _const((1, WC))],
        out_specs=[pl.BlockSpec((tm, 2 * WC), lambda i: (i, 0)), _const((NG, G, G)), _const((1, WC)), _const((1, WC))],
        out_shape=[_sds((S, 2 * WC), BF16), _sds((NG, G, G), F32), _sds((1, WC), F32), _sds((1, WC), F32)],
        scratch_shapes=[pltpu.VMEM((tm + PHALO, G), F32)],
        name=name, compiler_params=_params(("arbitrary",), 48),
    )(dyy, dyy, q, q, gg, pooled, cw, cscale)


def _place():
    return lax.axis_index("x"), lax.axis_index("y"), lax.axis_index("c")


def _piece(ref, axis, size, index):
    start = index * size
    if axis == len(ref.shape) - 1:
        start = pl.multiple_of(start, LANES)
    idx = [slice(None)] * len(ref.shape)
    idx[axis] = pl.ds(start, size)
    return ref.at[tuple(idx)]


def _gather_copies(src, out, axis, size, send_sems, recv_sems, base, held=None):
    x, y, c = _place()
    sib, xn, yn = (x, y, 1 - c), (1 - x, y, c), (x, 1 - y, c)

    def blk(px, py, of=out):
        return _piece(of, axis, size, 4 * px + 2 * py + c)

    def half(ref, h):
        n = ref.shape[0] // 2
        return ref.at[pl.ds(h * n, n)]

    def rc(k, s, d, to):
        return pltpu.make_async_remote_copy(src_ref=s, dst_ref=d, send_sem=send_sems.at[base + k],
                                            recv_sem=recv_sems.at[base + k], device_id=to, device_id_type=MESH)

    own, xb, yb, db = blk(x, y), blk(1 - x, y), blk(x, 1 - y), blk(1 - x, 1 - y)
    got = out if held is None else held
    xs, ys, ds = blk(1 - x, y, got), blk(x, 1 - y, got), blk(1 - x, 1 - y, got)
    return [rc(0, src, own, sib), rc(1, src, own, xn), rc(2, src, own, yn),
            rc(3, half(xs, 0), half(xb, 0), yn), rc(4, half(ys, 1), half(yb, 1), xn),
            rc(5, xs, xb, sib), rc(6, ys, yb, sib), rc(7, ds, db, sib)]


N_GATHER = 8


def _gather_comm(shards, axes, phases):
    n = len(shards)
    if phases == "second":
        sizes = [s.shape[a] // N_DEV for s, a in zip(shards, axes)]
        full = [_sds(s.shape, s.dtype) for s in shards]
    else:
        sizes = [s.shape[a] for s, a in zip(shards, axes)]
        full = [_sds(s.shape[:a] + (N_DEV * s.shape[a],) + s.shape[a + 1:], s.dtype) for s, a in zip(shards, axes)]

    def plan(ins, outs, sems):
        x, y, c = _place()
        me = 4 * x + 2 * y + c
        if phases == "second":
            cps = [_gather_copies(_piece(ins[t], axes[t], sizes[t], me), outs[t], axes[t], sizes[t], sems[0], sems[1],
                                  N_GATHER * t, ins[t]) for t in range(n)]
        else:
            cps = [_gather_copies(ins[t], outs[t], axes[t], sizes[t], sems[0], sems[1], N_GATHER * t)
                   for t in range(n)]
        mine = [pltpu.make_async_copy(ins[t], _piece(outs[t], axes[t], sizes[t], me), sems[2].at[t])
                for t in range(n)] if phases != "second" else []
        return cps, mine

    def send_own(ins, outs, sems):
        cps, mine = plan(ins, outs, sems)
        for t in range(n):
            mine[t].start()
            for k in (0, 1, 2):
                cps[t][k].start()

    def pass_on(ins, outs, sems):
        cps, _ = plan(ins, outs, sems)
        for t in range(n):
            if phases == "all":
                cps[t][1].wait_recv()
            cps[t][3].start()
            cps[t][5].start()
        for t in range(n):
            if phases == "all":
                cps[t][2].wait_recv()
            cps[t][4].start()
            cps[t][6].start()

    def own_landed(ins, outs, sems):
        cps, mine = plan(ins, outs, sems)
        for t in range(n):
            for k in (0, 1, 2):
                cps[t][k].wait()
            mine[t].wait()

    def all_landed(ins, outs, sems):
        cps, mine = plan(ins, outs, sems)
        for t in range(n):
            cps[t][3].wait_recv()
            cps[t][4].wait_recv()
            cps[t][7].start()
        for t in range(n):
            for k in ((0, 5, 6, 7) if phases == "all" else (5, 6, 7)):
                cps[t][k].wait_recv()
            for k in (range(N_GATHER) if phases == "all" else range(3, N_GATHER)):
                cps[t][k].wait_send()
            if phases == "all":
                mine[t].wait()

    sems = [pltpu.SemaphoreType.DMA((N_GATHER * n,)), pltpu.SemaphoreType.DMA((N_GATHER * n,))]
    if phases != "second":
        sems.append(pltpu.SemaphoreType.DMA((n,)))
    if phases == "all":
        return _Comm(shards, full, sems, send_own, all_landed, middle=pass_on)
    if phases == "first":
        return _Comm(shards, full, sems, send_own, own_landed)
    return _Comm(shards, full, sems, pass_on, all_landed, aliases={t: t for t in range(n)})


def _pair_comm(grads, axes, sizes):
    n = len(grads)
    outs_sds = [_sds((4,) + g.shape[:a] + (s,) + g.shape[a + 1:], g.dtype) for g, a, s in zip(grads, axes, sizes)]

    def copies(ins, outs, sems):
        send_sems, recv_sems = sems
        x, y, c = _place()
        return [pltpu.make_async_remote_copy(
            src_ref=_piece(ins[t], axes[t], sizes[t], 2 * qi + (1 - c)), dst_ref=outs[t].at[qi],
            send_sem=send_sems.at[4 * t + qi], recv_sem=recv_sems.at[4 * t + qi],
            device_id=(x, y, 1 - c), device_id_type=MESH) for t in range(n) for qi in range(4)]

    def start(ins, outs, sems):
        for cp in copies(ins, outs, sems):
            cp.start()

    def finish(ins, outs, sems):
        for cp in copies(ins, outs, sems):
            cp.wait()

    sems = [pltpu.SemaphoreType.DMA((4 * n,)), pltpu.SemaphoreType.DMA((4 * n,))]
    return _Comm(grads, outs_sds, sems, start, finish)


def _chip_comm(sums):
    n = len(sums)
    outs_sds = [_sds((3,) + s.shape[1:], s.dtype) for s in sums]

    def copies(ins, outs, sems):
        send_sems, recv_sems = sems
        x, y, c = _place()
        return [pltpu.make_async_remote_copy(
            src_ref=ins[t].at[2 * qx + qy], dst_ref=outs[t].at[j],
            send_sem=send_sems.at[3 * t + j], recv_sem=recv_sems.at[3 * t + j],
            device_id=(qx, qy, c), device_id_type=MESH)
            for t in range(n) for j, (qx, qy) in enumerate([(1 - x, y), (x, 1 - y), (1 - x, 1 - y)])]

    def start(ins, outs, sems):
        for cp in copies(ins, outs, sems):
            cp.start()

    def finish(ins, outs, sems):
        for cp in copies(ins, outs, sems):
            cp.wait()

    sems = [pltpu.SemaphoreType.DMA((3 * n,)), pltpu.SemaphoreType.DMA((3 * n,))]
    return _Comm(sums, outs_sds, sems, start, finish)


def _small_comm(small):
    def copies(ins, outs, sems):
        send_sems, recv_sems, local_sem = sems
        x, y, c = _place()
        mine = outs[0].at[4 * x + 2 * y + c]
        out = [pltpu.make_async_copy(ins[0], mine, local_sem.at[0])]
        for k in range(1, N_DEV):
            peer = (1 - x if k & 4 else x, 1 - y if k & 2 else y, 1 - c if k & 1 else c)
            out.append(pltpu.make_async_remote_copy(
                src_ref=ins[0], dst_ref=mine, send_sem=send_sems.at[k - 1], recv_sem=recv_sems.at[k - 1],
                device_id=peer, device_id_type=MESH))
        return out

    def start(ins, outs, sems):
        for cp in copies(ins, outs, sems):
            cp.start()

    def finish(ins, outs, sems):
        for cp in copies(ins, outs, sems):
            cp.wait()

    sems = [pltpu.SemaphoreType.DMA((N_DEV - 1,)), pltpu.SemaphoreType.DMA((N_DEV - 1,)), pltpu.SemaphoreType.DMA((1,))]
    return _Comm([small], [_sds((N_DEV,) + small.shape, small.dtype)], sems, start, finish)


def _pair_sum(c_idx, grad, recv, axis, size, split, *, name):
    nd = len(grad.shape)
    piece = grad.shape[:axis] + (size,) + grad.shape[axis + 1:]
    blk = (piece[0] // split,) + piece[1:]

    def g_map(q, r, c_ref):
        idx = [0] * nd
        idx[axis] = 2 * q + c_ref[0]
        idx[0] = idx[0] * split + r if axis == 0 else r
        return tuple(idx)

    def r_map(q, r, c_ref):
        return (q, r) + (0,) * (nd - 1)

    def body(c_ref, g_ref, r_ref, o_ref):
        o_ref[0] = (g_ref[...].astype(F32) + r_ref[0].astype(F32)).astype(BF16)

    return pl.pallas_call(
        body,
        grid_spec=pltpu.PrefetchScalarGridSpec(
            num_scalar_prefetch=1, grid=(4, split),
            in_specs=[pl.BlockSpec(blk, g_map), pl.BlockSpec((1,) + blk, r_map)],
            out_specs=pl.BlockSpec((1,) + blk, r_map)),
        out_shape=_sds((4,) + piece, BF16),
        name=name, compiler_params=_params(("arbitrary", "arbitrary"), 32),
    )(c_idx, grad, recv)


def _adam_math(w, g, m, v):
    m = ADAM_B1 * m + (1.0 - ADAM_B1) * g
    v = ADAM_B2 * v + (1.0 - ADAM_B2) * (g * g)
    m_hat = m / (1.0 - ADAM_B1 ** ADAM_STEP)
    v_hat = v / (1.0 - ADAM_B2 ** ADAM_STEP)
    delta = -ADAM_LR * (m_hat / (jnp.sqrt(v_hat) + ADAM_EPS) + ADAM_WD * w)
    return delta, m, v


def _adam_big(q_idx, sums, recv, w, m, v, split, *, name):
    shape = w.shape
    nd = len(shape)
    blk = (shape[0] // split,) + shape[1:]
    w_map = lambda r, q_ref: (r,) + (0,) * (nd - 1)
    s_map = lambda r, q_ref: (q_ref[0], r) + (0,) * (nd - 1)
    r_map = lambda r, q_ref: (0, r) + (0,) * (nd - 1)

    def body(q_ref, s_ref, r_ref, w_ref, m_ref, v_ref, g_ref, d_ref, nm_ref, nv_ref):
        g = s_ref[0].astype(F32) + r_ref[0].astype(F32) + r_ref[1].astype(F32) + r_ref[2].astype(F32)
        g_ref[...] = g
        d_ref[...], nm_ref[...], nv_ref[...] = _adam_math(w_ref[...], g, m_ref[...], v_ref[...])

    wspec = pl.BlockSpec(blk, w_map)
    return pl.pallas_call(
        body,
        grid_spec=pltpu.PrefetchScalarGridSpec(
            num_scalar_prefetch=1, grid=(split,),
            in_specs=[pl.BlockSpec((1,) + blk, s_map), pl.BlockSpec((3,) + blk, r_map), wspec, wspec, wspec],
            out_specs=[wspec] * 4),
        out_shape=[_sds(shape, F32)] * 4,
        name=name, compiler_params=_params(("arbitrary",), 32),
    )(q_idx, sums, recv, w, m, v)


def _adam_small(parts, w, m, v, *, name):
    R = w.shape[0]

    def body(p_ref, w_ref, m_ref, v_ref, g_ref, d_ref, nm_ref, nv_ref):
        g = p_ref[0]
        for d in range(1, N_DEV):
            g = g + p_ref[d]
        g_ref[...] = g
        d_ref[...], nm_ref[...], nv_ref[...] = _adam_math(w_ref[...], g, m_ref[...], v_ref[...])

    return pl.pallas_call(
        body, out_shape=[_sds((R, LANES), F32)] * 4, name=name,
        compiler_params=pltpu.CompilerParams(vmem_limit_bytes=32 * MIB),
    )(parts, w, m, v)


def _pack(arrs):
    return jnp.concatenate([a.reshape(-1) for a in arrs]).reshape(-1, LANES)


def _unpack(packed, shapes):
    flat = packed.reshape(-1)
    out, off = [], 0
    for s in shapes:
        n = 1
        for d in s:
            n *= d
        out.append(flat[off:off + n].reshape(s))
        off += n
    return out


BIG = ("e_in", "e_out", "o_in", "o_cw", "o_out")
BIG_AXIS = dict(e_in=1, e_out=0, o_in=1, o_cw=1, o_out=0)
BIG_SPLIT = dict(e_in=8, e_out=4, o_in=4, o_cw=4, o_out=4)
REPLICATED = ("e_norm_pre", "e_norm_post", "e_b_conv_bias", "e_b_ln_g", "e_b_ln_b")
SHARDED = ("e_a_conv", "e_b_conv", "o_norm_pre", "o_norm_post", "o_c_b", "o_c_scale")
SMALL = REPLICATED + SHARDED


class _Exchange:
    def __init__(self, shards, small, order, c_idx):
        self.shards = shards
        self.small = small
        self.order = order
        self.c_idx = c_idx
        self.reduced = {}

    def gather(self, keys):
        return _gather_comm([self.shards[k] for k in keys], [BIG_AXIS[k] for k in keys], "all")

    def gather1(self, keys):
        return _gather_comm([self.shards[k] for k in keys], [BIG_AXIS[k] for k in keys], "first")

    def gather2(self, keys, firsts):
        return _gather_comm(firsts, [BIG_AXIS[k] for k in keys], "second")

    def pair(self, grads):
        keys = list(grads)
        return _pair_comm([grads[k] for k in keys], [BIG_AXIS[k] for k in keys],
                          [grads[k].shape[BIG_AXIS[k]] // N_DEV for k in keys])

    def pair_sums(self, grads, received):
        return {k: _pair_sum(self.c_idx, grads[k], r, BIG_AXIS[k], grads[k].shape[BIG_AXIS[k]] // N_DEV,
                             BIG_SPLIT[k], name="pair_sum_" + k) for k, r in zip(grads, received)}

    def chips(self, sums):
        return _chip_comm([sums[k] for k in sums])

    def done(self, sums, received):
        self.reduced.update({k: (sums[k], r) for k, r in zip(sums, received)})


def _local_step(x, tgt, w_small, ex):
    S, D = x.shape
    tnt, tx = min(TM_NT, S), min(TM_MIX, S)

    h0, got = _rms_norm(x, w_small["e_norm_pre"], tm=tx, name="e_norm", comm=_small_comm(ex.small))
    per_dev = [_unpack(got[0][d], [w_small[k].shape for k in SHARDED]) for d in range(N_DEV)]
    sm = {k: w_small[k] for k in REPLICATED}
    for j, k in enumerate(SHARDED):
        sm[k] = jnp.concatenate([per_dev[d][j] for d in range(N_DEV)], axis=-1)
    n_groups = sm["o_c_b"].shape[0]
    sm["o_c_b"] = sm["o_c_b"].reshape(1, -1)

    wt = {}
    p, wt["e_in"], _ = _gather_matmul(ex.order, h0, ex.shards["e_in"], tm=tnt, name="e_in_fwd")
    W = p.shape[1] // 7
    (u, cb), got = _e_mix_fwd(p, sm["e_a_conv"], sm["e_b_conv"], sm["e_b_conv_bias"], sm["e_b_ln_g"],
                              sm["e_b_ln_b"], tm=tx, name="e_mix_fwd", comm=ex.gather(["e_out"]))
    wt["e_out"] = got[0]
    late = ["o_out", "o_cw"]
    (x1, y0), part = _out_norm_res(u, wt["e_out"], x, sm["e_norm_post"], tm=tx, name="e_out_fwd",
                                   comm=ex.gather1(late))
    h1, _ = _rms_norm(x1, sm["o_norm_pre"], tm=tx, name="o_norm")
    q, wt["o_in"], got = _gather_matmul(ex.order, h1, ex.shards["o_in"], tm=tnt, name="o_in_fwd",
                                        comm=ex.gather2(late, part))
    wt.update(zip(late, got))
    yy, pooled, gg = _o_mix_fwd(q, wt["o_cw"], sm["o_c_b"], sm["o_c_scale"], tm=tx, name="o_mix_fwd")
    dout, dx2, dyy, lcol, dg_o_post = _out_loss(yy, wt["o_out"], x1, sm["o_norm_post"], tgt, tm=tx, name="o_out_loss")
    loss = (0.5 / D) * jnp.sum(lcol)

    dq, d_cw, d_cb, d_cscale = _o_mix_bwd(dyy, q, gg, pooled, wt["o_cw"], sm["o_c_scale"], tm=tx, name="o_mix_bwd")
    g_o_out, _ = _mm_tn(yy, dout, ts=tnt, tn=W, name="o_out_dw")
    ga = dict(o_out=g_o_out, o_cw=d_cw.astype(BF16))
    dh1, ra = _mm_nt(dq, wt["o_in"], tm=tnt, tk=W, name="o_in_bwd", comm=ex.pair(ga))
    sa = ex.pair_sums(ga, ra)
    (dx1, dy0, dg_o_pre, dg_e_post), ra = _pre_bwd_o(dh1, x1, dx2, y0, sm["o_norm_pre"], sm["e_norm_post"],
                                                     tm=tx, name="o_pre_bwd", comm=ex.chips(sa))
    ex.done(sa, ra)
    g_o_in, _ = _mm_tn(h1, dq, ts=tnt, tn=W, name="o_in_dw")
    gb = dict(o_in=g_o_in)
    du, rb = _mm_nt(dy0, wt["e_out"], tm=tnt, tk=W, name="e_out_bwd", comm=ex.pair(gb))
    sb = ex.pair_sums(gb, rb)
    g_e_out, _ = _mm_tn(u, dy0, ts=tnt, tn=W, name="e_out_dw")
    gc = dict(e_out=g_e_out)
    (dp, d_wa, d_wb, d_bias, d_lg, d_lb), rbc = _e_mix_bwd(
        du, p, cb, sm["e_a_conv"], sm["e_b_conv"], sm["e_b_ln_g"], sm["e_b_ln_b"], tm=tx, name="e_mix_bwd",
        comm=_merge(ex.chips(sb), ex.pair(gc)))
    ex.done(sb, rbc[:1])
    sc = ex.pair_sums(gc, rbc[1:])
    g_e_in, rc = _mm_tn(h0, dp, ts=tnt, tn=W, name="e_in_dw", comm=ex.chips(sc))
    ex.done(sc, rc)
    gd = dict(e_in=g_e_in)
    sd = ex.pair_sums(gd, _run_comm(ex.pair(gd), "pair_e_in"))
    dh0, rd = _mm_nt(dp, wt["e_in"], tm=tnt, tk=W, name="e_in_bwd", comm=ex.chips(sd))
    ex.done(sd, rd)
    grad_x, dg_e_pre = _pre_bwd_e(dh0, x, dx1, sm["e_norm_pre"], tm=tx, name="e_pre_bwd")

    small = dict(e_norm_pre=dg_e_pre, e_norm_post=dg_e_post, e_a_conv=d_wa, e_b_conv=d_wb, e_b_conv_bias=d_bias,
                 e_b_ln_g=d_lg, e_b_ln_b=d_lb, o_norm_pre=dg_o_pre, o_norm_post=dg_o_post,
                 o_c_b=d_cb.reshape(n_groups, -1), o_c_scale=d_cscale)
    return loss, grad_x, small


def kernel(x, e_norm_pre, e_norm_post, e_w_in, e_a_conv, e_b_conv, e_b_conv_bias, e_b_ln_g, e_b_ln_b, e_w_out, o_norm_pre, o_norm_post, o_w_in, o_c_w, o_c_b, o_c_scale, o_w_out, loss_target, m_e_norm_pre, m_e_norm_post, m_e_w_in, m_e_a_conv, m_e_b_conv, m_e_b_conv_bias, m_e_b_ln_g, m_e_b_ln_b, m_e_w_out, m_o_norm_pre, m_o_norm_post, m_o_w_in, m_o_c_w, m_o_c_b, m_o_c_scale, m_o_w_out, v_e_norm_pre, v_e_norm_post, v_e_w_in, v_e_a_conv, v_e_b_conv, v_e_b_conv_bias, v_e_b_ln_g, v_e_b_ln_b, v_e_w_out, v_o_norm_pre, v_o_norm_post, v_o_w_in, v_o_c_w, v_o_c_b, v_o_c_scale, v_o_w_out):
    xi, yi, ci = _place()
    me = 4 * xi + 2 * yi + ci
    w_big = dict(e_in=e_w_in[0], e_out=e_w_out[0], o_in=o_w_in[0], o_cw=o_c_w[0], o_out=o_w_out[0])
    m_big = dict(e_in=m_e_w_in[0], e_out=m_e_w_out[0], o_in=m_o_w_in[0], o_cw=m_o_c_w[0], o_out=m_o_w_out[0])
    v_big = dict(e_in=v_e_w_in[0], e_out=v_e_w_out[0], o_in=v_o_w_in[0], o_cw=v_o_c_w[0], o_out=v_o_w_out[0])
    w_small = dict(e_norm_pre=e_norm_pre, e_norm_post=e_norm_post, e_b_conv_bias=e_b_conv_bias, e_b_ln_g=e_b_ln_g,
                   e_b_ln_b=e_b_ln_b, e_a_conv=e_a_conv[0], e_b_conv=e_b_conv[0], o_norm_pre=o_norm_pre,
                   o_norm_post=o_norm_post, o_c_b=o_c_b[0], o_c_scale=o_c_scale)
    m_small = dict(e_norm_pre=m_e_norm_pre, e_norm_post=m_e_norm_post, e_b_conv_bias=m_e_b_conv_bias,
                   e_b_ln_g=m_e_b_ln_g, e_b_ln_b=m_e_b_ln_b, e_a_conv=m_e_a_conv[0], e_b_conv=m_e_b_conv[0],
                   o_norm_pre=m_o_norm_pre, o_norm_post=m_o_norm_post, o_c_b=m_o_c_b[0], o_c_scale=m_o_c_scale)
    v_small = dict(e_norm_pre=v_e_norm_pre, e_norm_post=v_e_norm_post, e_b_conv_bias=v_e_b_conv_bias,
                   e_b_ln_g=v_e_b_ln_g, e_b_ln_b=v_e_b_ln_b, e_a_conv=v_e_a_conv[0], e_b_conv=v_e_b_conv[0],
                   o_norm_pre=v_o_norm_pre, o_norm_post=v_o_norm_post, o_c_b=v_o_c_b[0], o_c_scale=v_o_c_scale)

    c_idx = jnp.reshape(ci, (1,)).astype(jnp.int32)
    order = jnp.stack([2 * xi + yi, 2 * (1 - xi) + yi, 2 * xi + (1 - yi), 2 * (1 - xi) + (1 - yi)]).astype(jnp.int32)
    ex = _Exchange({k: w_big[k].astype(BF16) for k in BIG}, _pack([w_small[k] for k in SHARDED]), order, c_idx)
    loss, grad_x, g_small = _local_step(x[0], loss_target[0], w_small, ex)
    loss = lax.psum(loss, ("x", "y", "c"))

    full_shapes = [g_small[k].shape for k in SMALL]
    small_parts = _run_comm(_small_comm(_pack([g_small[k] for k in SMALL])), "small_grad_exchange")[0]

    q_idx = jnp.reshape(2 * xi + yi, (1,)).astype(jnp.int32)
    big_out = {k: _adam_big(q_idx, *ex.reduced[k], w_big[k], m_big[k], v_big[k], BIG_SPLIT[k], name="adam_" + k)
               for k in BIG}

    def at_full_size(d):
        out = []
        for k, s in zip(SMALL, full_shapes):
            if k in REPLICATED:
                out.append(d[k])
            else:
                n = d[k].shape[-1]
                out.append(lax.dynamic_update_slice_in_dim(jnp.ones(s, F32), d[k], me * n, axis=-1))
        return _pack(out)

    res_small = _adam_small(small_parts, at_full_size(w_small), at_full_size(m_small), at_full_size(v_small),
                            name="adam_small")
    small_out = {k: [] for k in SMALL}
    for packed in res_small:
        for k, full in zip(SMALL, _unpack(packed, full_shapes)):
            if k in SHARDED:
                n = w_small[k].shape[-1]
                full = lax.dynamic_slice_in_dim(full, me * n, n, axis=-1)
            small_out[k].append(full)

    big_of = dict(e_w_in="e_in", e_w_out="e_out", o_w_in="o_in", o_c_w="o_cw", o_w_out="o_out")
    stacked = ("e_a_conv", "e_b_conv", "o_c_b")

    def leaf(name, which):
        if name in big_of:
            return big_out[big_of[name]][which][None]
        t = small_out[name][which]
        return t[None] if name in stacked else t

    order = ("e_norm_pre", "e_norm_post", "e_w_in", "e_a_conv", "e_b_conv", "e_b_conv_bias", "e_b_ln_g", "e_b_ln_b",
             "e_w_out", "o_norm_pre", "o_norm_post", "o_w_in", "o_c_w", "o_c_b", "o_c_scale", "o_w_out")
    outs = [loss, grad_x[None]]
    for which in range(4):
        outs += [leaf(nm, which) for nm in order]
    return tuple(outs)
```

```python
import jax
import jax.numpy as jnp
from jax import lax
from jax.experimental import pallas as pl
from jax.experimental.pallas import tpu as pltpu

F32 = jnp.float32
BF16 = jnp.bfloat16
EPS = 1e-6
MESH = pl.DeviceIdType.MESH
ANY = pl.BlockSpec(memory_space=pl.ANY)

N_DEV = 8
HALO = 32
PHALO = 16
CONV_A = 3
CONV_B = 31
POOL_WINDOWS = (2, 4, 8, 16)
LANES = 128
MIB = 1024 * 1024

ADAM_LR = 0.001
ADAM_B1 = 0.9
ADAM_B2 = 0.999
ADAM_EPS = 1e-08
ADAM_WD = 0.01
ADAM_STEP = 10

TM_NT = 1024
TM_MIX = 256


def _sds(shape, dtype):
    return jax.ShapeDtypeStruct(tuple(shape), dtype)


def _params(sem, vmem_mib):
    return pltpu.CompilerParams(dimension_semantics=sem, vmem_limit_bytes=vmem_mib * MIB)


def _const(shape, single=False):
    n = len(shape)
    if single:
        return pl.BlockSpec(shape, lambda *_: (0,) * n, pipeline_mode=pl.Buffered(1))
    return pl.BlockSpec(shape, lambda *_: (0,) * n)


def _sig(v):
    return jax.nn.sigmoid(v)


def _dsilu(v, s):
    return s * (1.0 + v * (1.0 - s))


def _rms(v):
    return lax.rsqrt(jnp.mean(v * v, axis=-1, keepdims=True) + EPS)


def _norm_bwd(dn, n, r):
    return r * (dn - n * jnp.mean(dn * n, axis=-1, keepdims=True))


def _colsum(v):
    return jnp.sum(v, axis=0, keepdims=True)


class _Comm:
    def __init__(self, inputs, out_shapes, sems, start, finish, aliases=None, middle=None):
        self.inputs, self.out_shapes, self.sems = list(inputs), list(out_shapes), list(sems)
        self.start, self.finish, self.middle = start, finish, middle
        self.aliases = dict(aliases or {})


def _merge(*comms):
    comms = [c for c in comms if c is not None]
    if len(comms) <= 1:
        return comms[0] if comms else None
    spans, i0, o0, s0, aliases = [], 0, 0, 0, {}
    for c in comms:
        spans.append((i0, o0, s0))
        aliases.update({i0 + k: o0 + v for k, v in c.aliases.items()})
        i0, o0, s0 = i0 + len(c.inputs), o0 + len(c.out_shapes), s0 + len(c.sems)

    def run(which):
        def fn(ins, outs, sems):
            for c, (i, o, s) in zip(comms, spans):
                hook = getattr(c, which)
                if hook is not None:
                    hook(ins[i:i + len(c.inputs)], outs[o:o + len(c.out_shapes)], sems[s:s + len(c.sems)])
        return fn

    return _Comm([a for c in comms for a in c.inputs], [a for c in comms for a in c.out_shapes],
                 [a for c in comms for a in c.sems], run("start"), run("finish"), aliases,
                 run("middle") if any(c.middle is not None for c in comms) else None)


def _call(body, *, grid, in_specs, out_specs, out_shape, operands, name, params, scratch_shapes=(), comm=None,
          prefetch=None):
    n_p = 0 if prefetch is None else 1
    n_i, n_o, n_s = len(in_specs), len(out_specs), len(scratch_shapes)
    if comm is None:
        comm = _Comm([], [], [], None, None)
    c_i, c_o = len(comm.inputs), len(comm.out_shapes)

    def carrier(*refs):
        pre, refs = refs[:n_p], refs[n_p:]
        ins, cins = refs[:n_i], refs[n_i:n_i + c_i]
        outs = refs[n_i + c_i:n_i + c_i + n_o]
        couts = refs[n_i + c_i + n_o:n_i + c_i + n_o + c_o]
        scr = refs[n_i + c_i + n_o + c_o:n_i + c_i + n_o + c_o + n_s]
        csems = refs[n_i + c_i + n_o + c_o + n_s:]
        ids = [pl.program_id(d) for d in range(len(grid))]
        first = ids[0] == 0
        half = ids[0] == grid[0] // 2
        last = ids[0] == grid[0] - 1
        for d in range(1, len(grid)):
            first = first & (ids[d] == 0)
            half = half & (ids[d] == 0)
            last = last & (ids[d] == grid[d] - 1)

        if comm.start is not None:
            @pl.when(first)
            def _():
                comm.start(cins, couts, csems)

        if comm.middle is not None:
            assert grid[0] >= 2

            @pl.when(half)
            def _():
                comm.middle(cins, couts, csems)

        body(*pre, *ins, *outs, *scr)

        if comm.finish is not None:
            @pl.when(last)
            def _():
                comm.finish(cins, couts, csems)

    specs = dict(grid=grid, in_specs=list(in_specs) + [ANY] * c_i, out_specs=list(out_specs) + [ANY] * c_o,
                 scratch_shapes=list(scratch_shapes) + comm.sems)
    if n_p:
        specs = dict(grid_spec=pltpu.PrefetchScalarGridSpec(num_scalar_prefetch=1, **specs))
    res = pl.pallas_call(
        carrier, out_shape=list(out_shape) + comm.out_shapes,
        input_output_aliases={n_p + n_i + k: n_o + v for k, v in comm.aliases.items()},
        name=name, compiler_params=params, **specs)(*(() if prefetch is None else (prefetch,)), *operands, *comm.inputs)
    return list(res[:n_o]), list(res[n_o:])


def _run_comm(comm, name):
    c_i, c_o = len(comm.inputs), len(comm.out_shapes)

    def body(*refs):
        ins, outs, sems = refs[:c_i], refs[c_i:c_i + c_o], refs[c_i + c_o:]
        comm.start(ins, outs, sems)
        comm.finish(ins, outs, sems)

    res = pl.pallas_call(
        body, in_specs=[ANY] * c_i, out_specs=[ANY] * c_o, out_shape=comm.out_shapes, scratch_shapes=comm.sems,
        input_output_aliases=comm.aliases, name=name)(*comm.inputs)
    return list(res)


def _rms_norm(x, g, *, tm, name, comm=None):
    S, D = x.shape

    def body(x_ref, g_ref, h_ref):
        xx = x_ref[...]
        h_ref[...] = ((xx * _rms(xx)) * g_ref[...]).astype(BF16)

    row = pl.BlockSpec((tm, D), lambda i: (i, 0))
    outs, extra = _call(body, grid=(S // tm,), in_specs=[row, _const((1, D))], out_specs=[row],
                        out_shape=[_sds((S, D), BF16)], operands=(x, g), name=name,
                        params=_params(("arbitrary",), 32), comm=comm)
    return outs[0], extra


def _gather_matmul(order, h, shard, *, tm, name, comm=None):
    S, K = h.shape
    nb = shard.shape[1]
    n_i = S // tm

    def body(order_ref, h_ref, shard_ref, p_ref, full_ref, wbuf, send_sems, recv_sems, dma_sems):
        j, i = pl.program_id(0), pl.program_id(1)
        x, y, c = _place()
        cps = _gather_copies(shard_ref, full_ref, 1, nb, send_sems, recv_sems, 0)

        def load(qx, qy):
            cp = pltpu.make_async_copy(_piece(full_ref, 1, 2 * nb, 2 * qx + qy), wbuf, dma_sems.at[1])
            cp.start()
            cp.wait()

        @pl.when((j == 0) & (i == 0))
        def _():
            own = pltpu.make_async_copy(shard_ref, _piece(full_ref, 1, nb, 4 * x + 2 * y + c), dma_sems.at[0])
            own.start()
            for k in (0, 1, 2):
                cps[k].start()
            own.wait()
            cps[0].wait_recv()
            load(x, y)

        @pl.when((j == 1) & (i == 0))
        def _():
            cps[1].wait_recv()
            cps[3].start()
            cps[5].start()
            cps[2].wait_recv()
            cps[4].start()
            cps[6].start()
            cps[5].wait_recv()
            load(1 - x, y)

        @pl.when((j == 2) & (i == 0))
        def _():
            cps[6].wait_recv()
            load(x, 1 - y)

        @pl.when((j == 3) & (i == 0))
        def _():
            cps[3].wait_recv()
            cps[4].wait_recv()
            cps[7].start()
            cps[7].wait_recv()
            load(1 - x, 1 - y)

        p_ref[...] = jnp.dot(h_ref[...], wbuf[...], preferred_element_type=F32).astype(BF16)

        @pl.when((j == 3) & (i == n_i - 1))
        def _():
            for cp in cps:
                cp.wait_send()

    outs, extra = _call(
        body, grid=(4, n_i), prefetch=order,
        in_specs=[pl.BlockSpec((tm, K), lambda j, i, o: (i, 0)), ANY],
        out_specs=[pl.BlockSpec((tm, 2 * nb), lambda j, i, o: (i, o[j])), ANY],
        out_shape=[_sds((S, N_DEV * nb), BF16), _sds((K, N_DEV * nb), BF16)], operands=(h, shard),
        scratch_shapes=[pltpu.VMEM((K, 2 * nb), BF16), pltpu.SemaphoreType.DMA((N_GATHER,)),
                        pltpu.SemaphoreType.DMA((N_GATHER,)), pltpu.SemaphoreType.DMA((2,))],
        name=name, params=_params(("arbitrary", "arbitrary"), 48), comm=comm)
    return outs[0], outs[1], extra


def _out_norm_res(u, w, x, g, *, tm, name, comm=None):
    S, K = u.shape
    D = w.shape[1]

    def body(u_ref, w_ref, x_ref, g_ref, x1_ref, y_ref):
        y = jnp.dot(u_ref[...], w_ref[...], preferred_element_type=F32)
        y_ref[...] = y.astype(BF16)
        x1_ref[...] = x_ref[...] + (y * _rms(y)) * g_ref[...]

    return _call(
        body, grid=(S // tm,),
        in_specs=[pl.BlockSpec((tm, K), lambda i: (i, 0)), _const((K, D), single=True),
                  pl.BlockSpec((tm, D), lambda i: (i, 0)), _const((1, D))],
        out_specs=[pl.BlockSpec((tm, D), lambda i: (i, 0)), pl.BlockSpec((tm, D), lambda i: (i, 0))],
        out_shape=[_sds((S, D), F32), _sds((S, D), BF16)], operands=(u, w, x, g),
        name=name, params=_params(("arbitrary",), 48), comm=comm)


def _out_loss(yy, w, x1, g, tgt, *, tm, name):
    S, K = yy.shape
    D = w.shape[1]

    def body(yy_ref, w_ref, x1_ref, g_ref, t_ref, dout_ref, dx2_ref, dyy_ref, lcol_ref, dg_ref):
        out = jnp.dot(yy_ref[...], w_ref[...], preferred_element_type=F32)
        r = _rms(out)
        n = out * r
        gg = g_ref[...]
        e = x1_ref[...] + n * gg - t_ref[...]
        dx2 = e * (1.0 / D)
        dx2_ref[...] = dx2
        dout = _norm_bwd(dx2 * gg, n, r).astype(BF16)
        dout_ref[...] = dout
        dyy_ref[...] = lax.dot_general(dout, w_ref[...], (((1,), (1,)), ((), ())),
                                       preferred_element_type=F32).astype(BF16)

        @pl.when(pl.program_id(0) == 0)
        def _():
            lcol_ref[...] = jnp.zeros_like(lcol_ref)
            dg_ref[...] = jnp.zeros_like(dg_ref)

        lcol_ref[...] += _colsum(e * e)
        dg_ref[...] += _colsum(dx2 * n)

    return pl.pallas_call(
        body, grid=(S // tm,),
        in_specs=[pl.BlockSpec((tm, K), lambda i: (i, 0)), _const((K, D), single=True),
                  pl.BlockSpec((tm, D), lambda i: (i, 0)), _const((1, D)),
                  pl.BlockSpec((tm, D), lambda i: (i, 0))],
        out_specs=[pl.BlockSpec((tm, D), lambda i: (i, 0)), pl.BlockSpec((tm, D), lambda i: (i, 0)),
                   pl.BlockSpec((tm, K), lambda i: (i, 0)), _const((1, D)), _const((1, D))],
        out_shape=[_sds((S, D), BF16), _sds((S, D), F32), _sds((S, K), BF16), _sds((1, D), F32), _sds((1, D), F32)],
        name=name, compiler_params=_params(("arbitrary",), 52),
    )(yy, w, x1, g, tgt)


def _mm_nt(a, w, *, tm, tk, name, comm=None):
    S, N = a.shape
    D = w.shape[0]
    n_k = N // tk

    def body(a_ref, w_ref, o_ref, acc_ref):
        k = pl.program_id(1)

        @pl.when(k == 0)
        def _():
            acc_ref[...] = jnp.zeros_like(acc_ref)

        acc_ref[...] = lax.dot_general(a_ref[...], w_ref[...], (((1,), (1,)), ((), ())),
                                       preferred_element_type=F32) + acc_ref[...]

        @pl.when(k == n_k - 1)
        def _():
            o_ref[...] = acc_ref[...].astype(BF16)

    outs, extra = _call(
        body, grid=(S // tm, n_k),
        in_specs=[pl.BlockSpec((tm, tk), lambda i, k: (i, k)), pl.BlockSpec((D, tk), lambda i, k: (0, k))],
        out_specs=[pl.BlockSpec((tm, D), lambda i, k: (i, 0))],
        out_shape=[_sds((S, D), BF16)], operands=(a, w),
        scratch_shapes=[pltpu.VMEM((tm, D), F32)],
        name=name, params=_params(("arbitrary", "arbitrary"), 48), comm=comm)
    return outs[0], extra


def _mm_tn(a, b, *, ts, tn, name, comm=None):
    S, M = a.shape
    N = b.shape[1]
    n_s = S // ts

    def body(a_ref, b_ref, o_ref, acc_ref):
        s = pl.program_id(1)

        @pl.when(s == 0)
        def _():
            acc_ref[...] = jnp.zeros_like(acc_ref)

        acc_ref[...] = lax.dot_general(a_ref[...], b_ref[...], (((0,), (0,)), ((), ())),
                                       preferred_element_type=F32) + acc_ref[...]

        @pl.when(s == n_s - 1)
        def _():
            o_ref[...] = acc_ref[...].astype(BF16)

    outs, extra = _call(
        body, grid=(N // tn, n_s),
        in_specs=[pl.BlockSpec((ts, M), lambda j, s: (s, 0)), pl.BlockSpec((ts, tn), lambda j, s: (s, j))],
        out_specs=[pl.BlockSpec((M, tn), lambda j, s: (0, j))],
        out_shape=[_sds((M, N), BF16)], operands=(a, b),
        scratch_shapes=[pltpu.VMEM((M, tn), F32)],
        name=name, params=_params(("arbitrary", "arbitrary"), 48), comm=comm)
    return outs[0], extra


def _pre_bwd_o(dh, x1, dx2, y0, g_pre, g_post, *, tm, name, comm=None):
    S, D = x1.shape

    def body(dh_ref, x1_ref, dx2_ref, y0_ref, gpre_ref, gpost_ref, dx1_ref, dy0_ref, dgpre_ref, dgpost_ref):
        @pl.when(pl.program_id(0) == 0)
        def _():
            dgpre_ref[...] = jnp.zeros_like(dgpre_ref)
            dgpost_ref[...] = jnp.zeros_like(dgpost_ref)

        dh = dh_ref[...].astype(F32)
        x1 = x1_ref[...]
        r2 = _rms(x1)
        xn = x1 * r2
        dgpre_ref[...] += _colsum(dh * xn)
        dx1 = dx2_ref[...] + _norm_bwd(dh * gpre_ref[...], xn, r2)
        dx1_ref[...] = dx1
        y = y0_ref[...].astype(F32)
        r1 = _rms(y)
        n1 = y * r1
        dgpost_ref[...] += _colsum(dx1 * n1)
        dy0_ref[...] = _norm_bwd(dx1 * gpost_ref[...], n1, r1).astype(BF16)

    row = pl.BlockSpec((tm, D), lambda i: (i, 0))
    return _call(
        body, grid=(S // tm,),
        in_specs=[row, row, row, row, _const((1, D)), _const((1, D))],
        out_specs=[row, row, _const((1, D)), _const((1, D))],
        out_shape=[_sds((S, D), F32), _sds((S, D), BF16), _sds((1, D), F32), _sds((1, D), F32)],
        operands=(dh, x1, dx2, y0, g_pre, g_post),
        name=name, params=_params(("arbitrary",), 48), comm=comm)


def _pre_bwd_e(dh, x, dx1, g_pre, *, tm, name):
    S, D = x.shape

    def body(dh_ref, x_ref, dx1_ref, gpre_ref, gx_ref, dgpre_ref):
        @pl.when(pl.program_id(0) == 0)
        def _():
            dgpre_ref[...] = jnp.zeros_like(dgpre_ref)

        dh = dh_ref[...].astype(F32)
        xx = x_ref[...]
        r0 = _rms(xx)
        xn = xx * r0
        dgpre_ref[...] += _colsum(dh * xn)
        gx_ref[...] = dx1_ref[...] + _norm_bwd(dh * gpre_ref[...], xn, r0)

    row = pl.BlockSpec((tm, D), lambda i: (i, 0))
    return pl.pallas_call(
        body, grid=(S // tm,),
        in_specs=[row, row, row, _const((1, D))],
        out_specs=[row, _const((1, D))],
        out_shape=[_sds((S, D), F32), _sds((1, D), F32)],
        name=name, compiler_params=_params(("arbitrary",), 48),
    )(dh, x, dx1, g_pre)


SUBLANES = 8


def _shift_copies(sh_ref, ext_ref, cs):
    for b in range(1, SUBLANES):
        sh_ref[b - 1] = ext_ref[pl.ds(b, sh_ref.shape[1]), cs]


def _rows_at(ext_ref, sh_ref, off, cs, tm):
    b = off % SUBLANES
    if b == 0 or sh_ref is None:
        return ext_ref[pl.ds(off, tm), cs]
    return sh_ref[b - 1, pl.ds(off - b, tm), :]


def _taps(ext_ref, w_ref, n_taps, base, cs, tm, sh_ref=None):
    acc = _rows_at(ext_ref, sh_ref, base, cs, tm) * w_ref[0:1, cs]
    for k in range(1, n_taps):
        acc = acc + _rows_at(ext_ref, sh_ref, base + k, cs, tm) * w_ref[k:k + 1, cs]
    return acc


def _taps_rev(ext_ref, w_ref, n_taps, cs, tm, sh_ref=None):
    acc = _rows_at(ext_ref, sh_ref, n_taps - 1, cs, tm) * w_ref[0:1, cs]
    for k in range(1, n_taps):
        acc = acc + _rows_at(ext_ref, sh_ref, n_taps - 1 - k, cs, tm) * w_ref[k:k + 1, cs]
    return acc


def _e_mix_fwd(p, wa, wb, bias, ln_g, ln_b, *, tm, name, comm=None):
    S = p.shape[0]
    W = p.shape[1] // 7
    nb = tm // HALO
    chunks = [slice(c * LANES, (c + 1) * LANES) for c in range(W // LANES)]

    def body(p_ref, hax_ref, hac_ref, hbv_ref, hbg_ref, wa_ref, wb_ref, bias_ref, lg_ref, lb_ref,
             u_ref, cb_ref, ext_ref, sh_ref):
        keep = (pl.program_id(0) > 0).astype(F32)
        col = lambda j, cs: p_ref[:, j * W + cs.start:j * W + cs.stop].astype(F32)

        ext_ref[0:HALO, :] = hax_ref[...].astype(F32) * hac_ref[...].astype(F32) * keep
        ext_ref[HALO:, :] = p_ref[:, 2 * W:3 * W].astype(F32) * p_ref[:, 0:W].astype(F32)
        for cs in chunks:
            conv = _taps(ext_ref, wa_ref, CONV_A, HALO - (CONV_A - 1), cs, tm)
            az = col(3, cs)
            u_ref[:, cs] = (col(1, cs) * conv * (az * _sig(az))).astype(BF16)

        ext_ref[0:HALO, :] = hbv_ref[...].astype(F32) * _sig(hbg_ref[...].astype(F32)) * keep
        ext_ref[HALO:, :] = p_ref[:, 4 * W:5 * W].astype(F32) * _sig(p_ref[:, 5 * W:6 * W].astype(F32))
        s1 = jnp.zeros((tm, LANES), F32)
        for cs in chunks:
            _shift_copies(sh_ref, ext_ref, cs)
            cb = _taps(ext_ref, wb_ref, CONV_B, HALO - (CONV_B - 1), cs, tm, sh_ref) + bias_ref[:, cs]
            cb_ref[:, cs] = cb
            s1 = s1 + cb
        mu = jnp.sum(s1, axis=-1, keepdims=True) * (1.0 / W)
        s2 = jnp.zeros((tm, LANES), F32)
        for cs in chunks:
            xc = cb_ref[:, cs] - mu
            s2 = s2 + xc * xc
        rs = lax.rsqrt(jnp.sum(s2, axis=-1, keepdims=True) * (1.0 / W) + EPS)
        for cs in chunks:
            lb = (cb_ref[:, cs] - mu) * rs * lg_ref[:, cs] + lb_ref[:, cs]
            bz = col(6, cs)
            u_ref[:, W + cs.start:W + cs.stop] = (lb * _sig(lb) * (bz * _sig(bz))).astype(BF16)

    prev = lambda j: pl.BlockSpec((HALO, W), lambda i: (jnp.maximum(i * nb - 1, 0), j))
    return _call(
        body, grid=(S // tm,),
        in_specs=[pl.BlockSpec((tm, 7 * W), lambda i: (i, 0)), prev(0), prev(2), prev(4), prev(5),
                  _const((CONV_A, W)), _const((CONV_B, W)), _const((1, W)), _const((1, W)), _const((1, W))],
        out_specs=[pl.BlockSpec((tm, 2 * W), lambda i: (i, 0)), pl.BlockSpec((tm, W), lambda i: (i, 0))],
        out_shape=[_sds((S, 2 * W), BF16), _sds((S, W), F32)],
        operands=(p, p, p, p, p, wa, wb, bias, ln_g, ln_b),
        scratch_shapes=[pltpu.VMEM((HALO + tm, W), F32),
                        pltpu.VMEM((SUBLANES - 1, HALO + tm - SUBLANES, LANES), F32)],
        name=name, params=_params(("arbitrary",), 48), comm=comm)


def _e_mix_bwd(du, p, cb, wa, wb, ln_g, ln_b, *, tm, name, comm=None):
    S = p.shape[0]
    W = p.shape[1] // 7
    nb = tm // HALO
    n_t = S // tm
    last_blk = S // HALO - 1
    chunks = [slice(c * LANES, (c + 1) * LANES) for c in range(W // LANES)]

    def body(du_ref, duf_ref, p_ref, fab_ref, faz_ref, fbz_ref, hax_ref, hac_ref, hbv_ref, hbg_ref,
             cb_ref, cbf_ref, wa_ref, wb_ref, lg_ref, lb_ref,
             dp_ref, dwa_ref, dwb_ref, dbias_ref, dlg_ref, dlb_ref, extd_ref, extg_ref, shd_ref, shg_ref):
        i = pl.program_id(0)
        keep_prev = (i > 0).astype(F32)
        keep_next = (i < n_t - 1).astype(F32)
        col = lambda j, cs: p_ref[:, j * W + cs.start:j * W + cs.stop].astype(F32)

        @pl.when(i == 0)
        def _():
            dwa_ref[...] = jnp.zeros_like(dwa_ref)
            dwb_ref[...] = jnp.zeros_like(dwb_ref)
            dbias_ref[...] = jnp.zeros_like(dbias_ref)
            dlg_ref[...] = jnp.zeros_like(dlg_ref)
            dlb_ref[...] = jnp.zeros_like(dlb_ref)

        def dcb_rows(rows, cb_rows_ref, dub, bz_of, dst0, scale, main):
            cbv = cb_rows_ref[...]
            mu = jnp.mean(cbv, axis=-1, keepdims=True)
            xc = cbv - mu
            rs = lax.rsqrt(jnp.mean(xc * xc, axis=-1, keepdims=True) + EPS)
            m1 = jnp.zeros((rows, LANES), F32)
            m2 = jnp.zeros((rows, LANES), F32)
            for cs in chunks:
                nbv = (cb_rows_ref[:, cs] - mu) * rs
                lb = nbv * lg_ref[:, cs] + lb_ref[:, cs]
                sl = _sig(lb)
                bz = bz_of(cs)
                sz = _sig(bz)
                dub_c = dub(cs)
                dlb = dub_c * (bz * sz) * _dsilu(lb, sl)
                if main:
                    dlg_ref[:, cs] += _colsum(dlb * nbv)
                    dlb_ref[:, cs] += _colsum(dlb)
                    dp_ref[:, 6 * W + cs.start:6 * W + cs.stop] = (dub_c * (lb * sl) * _dsilu(bz, sz)).astype(BF16)
                dnb = dlb * lg_ref[:, cs]
                extd_ref[dst0:dst0 + rows, cs] = dnb
                m1 = m1 + dnb
                m2 = m2 + dnb * nbv
            m1 = jnp.sum(m1, axis=-1, keepdims=True) * (1.0 / W)
            m2 = jnp.sum(m2, axis=-1, keepdims=True) * (1.0 / W)
            for cs in chunks:
                nbv = (cb_rows_ref[:, cs] - mu) * rs
                dcb = rs * (extd_ref[dst0:dst0 + rows, cs] - m1 - nbv * m2) * scale
                extd_ref[dst0:dst0 + rows, cs] = dcb
                if main:
                    dbias_ref[:, cs] += _colsum(dcb)

        dcb_rows(tm, cb_ref, lambda cs: du_ref[:, W + cs.start:W + cs.stop].astype(F32),
                 lambda cs: col(6, cs), 0, 1.0, True)
        dcb_rows(HALO, cbf_ref, lambda cs: duf_ref[:, W + cs.start:W + cs.stop].astype(F32),
                 lambda cs: fbz_ref[:, cs].astype(F32), tm, keep_next, False)

        extg_ref[0:HALO, :] = hbv_ref[...].astype(F32) * _sig(hbg_ref[...].astype(F32)) * keep_prev
        extg_ref[HALO:, :] = p_ref[:, 4 * W:5 * W].astype(F32) * _sig(p_ref[:, 5 * W:6 * W].astype(F32))
        base_b = HALO - (CONV_B - 1)
        for cs in chunks:
            _shift_copies(shd_ref, extd_ref, cs)
            _shift_copies(shg_ref, extg_ref, cs)
            dgb = _taps_rev(extd_ref, wb_ref, CONV_B, cs, tm, shd_ref)
            bv = col(4, cs)
            sg = _sig(col(5, cs))
            dp_ref[:, 4 * W + cs.start:4 * W + cs.stop] = (dgb * sg).astype(BF16)
            dp_ref[:, 5 * W + cs.start:5 * W + cs.stop] = (dgb * bv * sg * (1.0 - sg)).astype(BF16)
            dcb = extd_ref[0:tm, cs]
            for k in range(CONV_B):
                dwb_ref[k:k + 1, cs] += _colsum(dcb * _rows_at(extg_ref, shg_ref, base_b + k, cs, tm))

        extg_ref[0:HALO, :] = hax_ref[...].astype(F32) * hac_ref[...].astype(F32) * keep_prev
        extg_ref[HALO:, :] = p_ref[:, 2 * W:3 * W].astype(F32) * p_ref[:, 0:W].astype(F32)
        base_a = HALO - (CONV_A - 1)
        for cs in chunks:
            conv = _taps(extg_ref, wa_ref, CONV_A, base_a, cs, tm)
            az = col(3, cs)
            sz = _sig(az)
            ab = col(1, cs)
            dua = du_ref[:, cs].astype(F32)
            dya = dua * (az * sz)
            dp_ref[:, W + cs.start:W + cs.stop] = (dya * conv).astype(BF16)
            dp_ref[:, 3 * W + cs.start:3 * W + cs.stop] = (dua * (ab * conv) * _dsilu(az, sz)).astype(BF16)
            extd_ref[0:tm, cs] = dya * ab
            azf = faz_ref[:, cs].astype(F32)
            extd_ref[tm:tm + HALO, cs] = (duf_ref[:, cs].astype(F32) * (azf * _sig(azf))
                                          * fab_ref[:, cs].astype(F32) * keep_next)
        for cs in chunks:
            dca = _taps_rev(extd_ref, wa_ref, CONV_A, cs, tm)
            dp_ref[:, cs] = (dca * col(2, cs)).astype(BF16)
            dp_ref[:, 2 * W + cs.start:2 * W + cs.stop] = (dca * col(0, cs)).astype(BF16)
            dconv = extd_ref[0:tm, cs]
            for k in range(CONV_A):
                dwa_ref[k:k + 1, cs] += _colsum(dconv * extg_ref[pl.ds(base_a + k, tm), cs])

    prev = lambda j: pl.BlockSpec((HALO, W), lambda i: (jnp.maximum(i * nb - 1, 0), j))
    nxt = lambda j, w: pl.BlockSpec((HALO, w), lambda i: (jnp.minimum((i + 1) * nb, last_blk), j))
    row = lambda w: pl.BlockSpec((tm, w), lambda i: (i, 0))
    return _call(
        body, grid=(n_t,),
        in_specs=[row(2 * W), nxt(0, 2 * W), row(7 * W), nxt(1, W), nxt(3, W), nxt(6, W),
                  prev(0), prev(2), prev(4), prev(5), row(W), nxt(0, W),
                  _const((CONV_A, W)), _const((CONV_B, W)), _const((1, W)), _const((1, W))],
        out_specs=[row(7 * W), _const((CONV_A, W)), _const((CONV_B, W)), _const((1, W)), _const((1, W)), _const((1, W))],
        out_shape=[_sds((S, 7 * W), BF16), _sds((CONV_A, W), F32), _sds((CONV_B, W), F32),
                   _sds((1, W), F32), _sds((1, W), F32), _sds((1, W), F32)],
        operands=(du, du, p, p, p, p, p, p, p, p, cb, cb, wa, wb, ln_g, ln_b),
        scratch_shapes=[pltpu.VMEM((tm + HALO, W), F32), pltpu.VMEM((HALO + tm, W), F32),
                        pltpu.VMEM((SUBLANES - 1, HALO + tm - SUBLANES, LANES), F32),
                        pltpu.VMEM((SUBLANES - 1, HALO + tm - SUBLANES, LANES), F32)],
        name=name, params=_params(("arbitrary",), 52), comm=comm)


def _counts(i, tm, rows, off, win):
    t = i * tm + off + lax.broadcasted_iota(jnp.int32, (rows, 1), 0)
    return jnp.minimum(t + 1, win).astype(F32)


def _o_mix_fwd(q, cw, cb, cscale, *, tm, name):
    S = q.shape[0]
    WC = q.shape[1] // 2
    NG = len(POOL_WINDOWS)
    G = WC // NG
    nb = tm // PHALO

    def body(v_ref, z_ref, hv_ref, cw_ref, cb_ref, sc_ref, yy_ref, pooled_ref, gg_ref, ext_ref):
        i = pl.program_id(0)
        keep = (i > 0).astype(F32)
        for g, win in enumerate(POOL_WINDOWS):
            cs = slice(g * G, (g + 1) * G)
            v = v_ref[:, cs].astype(F32)
            ext_ref[0:PHALO, :] = hv_ref[:, cs].astype(F32) * keep
            ext_ref[PHALO:, :] = v
            s = v
            for j in range(1, win):
                s = s + ext_ref[pl.ds(PHALO - j, tm), :]
            pooled = (s / _counts(i, tm, tm, 0, win) - v).astype(BF16)
            pooled_ref[:, cs] = pooled
            gg = jnp.dot(pooled, cw_ref[g], preferred_element_type=F32) + cb_ref[:, cs]
            gg_ref[:, cs] = gg.astype(BF16)
            z = z_ref[:, cs].astype(F32)
            yy_ref[:, cs] = (gg * sc_ref[:, cs] * (z * _sig(z))).astype(BF16)

    row = lambda j: pl.BlockSpec((tm, WC), lambda i: (i, j))
    out = pl.BlockSpec((tm, WC), lambda i: (i, 0))
    return pl.pallas_call(
        body, grid=(S // tm,),
        in_specs=[row(0), row(1), pl.BlockSpec((PHALO, WC), lambda i: (jnp.maximum(i * nb - 1, 0), 0)),
                  _const((NG, G, G)), _const((1, WC)), _const((1, WC))],
        out_specs=[out, out, out],
        out_shape=[_sds((S, WC), BF16)] * 3,
        scratch_shapes=[pltpu.VMEM((PHALO + tm, G), F32)],
        name=name, compiler_params=_params(("arbitrary",), 40),
    )(q, q, q, cw, cb, cscale)


def _o_mix_bwd(dyy, q, gg, pooled, cw, cscale, *, tm, name):
    S = q.shape[0]
    WC = q.shape[1] // 2
    NG = len(POOL_WINDOWS)
    G = WC // NG
    nb = tm // PHALO
    n_t = S // tm
    last_blk = S // PHALO - 1
    nt = (((1,), (1,)), ((), ()))
    tn = (((0,), (0,)), ((), ()))

    def body(dyy_ref, dyyf_ref, z_ref, zf_ref, gg_ref, pooled_ref, cw_ref, sc_ref,
             dq_ref, dcw_ref, dcb_ref, dsc_ref, ext_ref):
        i = pl.program_id(0)
        keep_next = (i < n_t - 1).astype(F32)

        @pl.when(i == 0)
        def _():
            dcw_ref[...] = jnp.zeros_like(dcw_ref)
            dcb_ref[...] = jnp.zeros_like(dcb_ref)
            dsc_ref[...] = jnp.zeros_like(dsc_ref)

        for g, win in enumerate(POOL_WINDOWS):
            cs = slice(g * G, (g + 1) * G)
            sc = sc_ref[:, cs]
            z = z_ref[:, cs].astype(F32)
            sz = _sig(z)
            dyy_c = dyy_ref[:, cs].astype(F32)
            ggv = gg_ref[:, cs].astype(F32)
            dyy0 = dyy_c * (z * sz)
            dq_ref[:, WC + cs.start:WC + cs.stop] = (dyy_c * (ggv * sc) * _dsilu(z, sz)).astype(BF16)
            dgg = dyy0 * sc
            dsc_ref[:, cs] += _colsum(dyy0 * ggv)
            dcb_ref[:, cs] += _colsum(dgg)
            dgg_b = dgg.astype(BF16)
            dcw_ref[g] += lax.dot_general(pooled_ref[:, cs], dgg_b, tn, preferred_element_type=F32)
            dpool = lax.dot_general(dgg_b, cw_ref[g], nt, preferred_element_type=F32)
            zf = zf_ref[:, cs].astype(F32)
            dgg_f = (dyyf_ref[:, cs].astype(F32) * (zf * _sig(zf)) * sc * keep_next).astype(BF16)
            dpool_f = lax.dot_general(dgg_f, cw_ref[g], nt, preferred_element_type=F32)
            ext_ref[0:tm, :] = dpool / _counts(i, tm, tm, 0, win)
            ext_ref[tm:tm + PHALO, :] = dpool_f / _counts(i, tm, PHALO, tm, win)
            dv = ext_ref[0:tm, :] - dpool
            for j in range(1, win):
                dv = dv + ext_ref[pl.ds(j, tm), :]
            dq_ref[:, cs] = dv.astype(BF16)

    row = lambda: pl.BlockSpec((tm, WC), lambda i: (i, 0))
    nxt = lambda j: pl.BlockSpec((PHALO, WC), lambda i: (jnp.minimum((i + 1) * nb, last_blk), j))
    return pl.pallas_call(
        body, grid=(n_t,),
        in_specs=[row(), nxt(0), pl.BlockSpec((tm, WC), lambda i: (i, 1)), nxt(1), row(), row(),
                  _const((NG, G, G)), _const((1, WC))],
        out_specs=[pl.BlockSpec((tm, 2 * WC), lambda i: (i, 0)), _const((NG, G, G)), _const((1, WC)), _const((1, WC))],
        out_shape=[_sds((S, 2 * WC), BF16), _sds((NG, G, G), F32), _sds((1, WC), F32), _sds((1, WC), F32)],
        scratch_shapes=[pltpu.VMEM((tm + PHALO, G), F32)],
        name=name, compiler_params=_params(("arbitrary",), 48),
    )(dyy, dyy, q, q, gg, pooled, cw, cscale)


def _place():
    return lax.axis_index("x"), lax.axis_index("y"), lax.axis_index("c")


def _piece(ref, axis, size, index):
    start = index * size
    if axis == len(ref.shape) - 1:
        start = pl.multiple_of(start, LANES)
    idx = [slice(None)] * len(ref.shape)
    idx[axis] = pl.ds(start, size)
    return ref.at[tuple(idx)]


def _gather_copies(src, out, axis, size, send_sems, recv_sems, base, held=None):
    x, y, c = _place()
    sib, xn, yn = (x, y, 1 - c), (1 - x, y, c), (x, 1 - y, c)

    def blk(px, py, of=out):
        return _piece(of, axis, size, 4 * px + 2 * py + c)

    def half(ref, h):
        n = ref.shape[0] // 2
        return ref.at[pl.ds(h * n, n)]

    def rc(k, s, d, to):
        return pltpu.make_async_remote_copy(src_ref=s, dst_ref=d, send_sem=send_sems.at[base + k],
                                            recv_sem=recv_sems.at[base + k], device_id=to, device_id_type=MESH)

    own, xb, yb, db = blk(x, y), blk(1 - x, y), blk(x, 1 - y), blk(1 - x, 1 - y)
    got = out if held is None else held
    xs, ys, ds = blk(1 - x, y, got), blk(x, 1 - y, got), blk(1 - x, 1 - y, got)
    return [rc(0, src, own, sib), rc(1, src, own, xn), rc(2, src, own, yn),
            rc(3, half(xs, 0), half(xb, 0), yn), rc(4, half(ys, 1), half(yb, 1), xn),
            rc(5, xs, xb, sib), rc(6, ys, yb, sib), rc(7, ds, db, sib)]


N_GATHER = 8


def _gather_comm(shards, axes, phases):
    n = len(shards)
    if phases == "second":
        sizes = [s.shape[a] // N_DEV for s, a in zip(shards, axes)]
        full = [_sds(s.shape, s.dtype) for s in shards]
    else:
        sizes = [s.shape[a] for s, a in zip(shards, axes)]
        full = [_sds(s.shape[:a] + (N_DEV * s.shape[a],) + s.shape[a + 1:], s.dtype) for s, a in zip(shards, axes)]

    def plan(ins, outs, sems):
        x, y, c = _place()
        me = 4 * x + 2 * y + c
        if phases == "second":
            cps = [_gather_copies(_piece(ins[t], axes[t], sizes[t], me), outs[t], axes[t], sizes[t], sems[0], sems[1],
                                  N_GATHER * t, ins[t]) for t in range(n)]
        else:
            cps = [_gather_copies(ins[t], outs[t], axes[t], sizes[t], sems[0], sems[1], N_GATHER * t)
                   for t in range(n)]
        mine = [pltpu.make_async_copy(ins[t], _piece(outs[t], axes[t], sizes[t], me), sems[2].at[t])
                for t in range(n)] if phases != "second" else []
        return cps, mine

    def send_own(ins, outs, sems):
        cps, mine = plan(ins, outs, sems)
        for t in range(n):
            mine[t].start()
            for k in (0, 1, 2):
                cps[t][k].start()

    def pass_on(ins, outs, sems):
        cps, _ = plan(ins, outs, sems)
        for t in range(n):
            if phases == "all":
                cps[t][1].wait_recv()
            cps[t][3].start()
            cps[t][5].start()
        for t in range(n):
            if phases == "all":
                cps[t][2].wait_recv()
            cps[t][4].start()
            cps[t][6].start()

    def own_landed(ins, outs, sems):
        cps, mine = plan(ins, outs, sems)
        for t in range(n):
            for k in (0, 1, 2):
                cps[t][k].wait()
            mine[t].wait()

    def all_landed(ins, outs, sems):
        cps, mine = plan(ins, outs, sems)
        for t in range(n):
            cps[t][3].wait_recv()
            cps[t][4].wait_recv()
            cps[t][7].start()
        for t in range(n):
            for k in ((0, 5, 6, 7) if phases == "all" else (5, 6, 7)):
                cps[t][k].wait_recv()
            for k in (range(N_GATHER) if phases == "all" else range(3, N_GATHER)):
                cps[t][k].wait_send()
            if phases == "all":
                mine[t].wait()

    sems = [pltpu.SemaphoreType.DMA((N_GATHER * n,)), pltpu.SemaphoreType.DMA((N_GATHER * n,))]
    if phases != "second":
        sems.append(pltpu.SemaphoreType.DMA((n,)))
    if phases == "all":
        return _Comm(shards, full, sems, send_own, all_landed, middle=pass_on)
    if phases == "first":
        return _Comm(shards, full, sems, send_own, own_landed)
    return _Comm(shards, full, sems, pass_on, all_landed, aliases={t: t for t in range(n)})


def _pair_comm(grads, axes, sizes):
    n = len(grads)
    outs_sds = [_sds((4,) + g.shape[:a] + (s,) + g.shape[a + 1:], g.dtype) for g, a, s in zip(grads, axes, sizes)]

    def copies(ins, outs, sems):
        send_sems, recv_sems = sems
        x, y, c = _place()
        return [pltpu.make_async_remote_copy(
            src_ref=_piece(ins[t], axes[t], sizes[t], 2 * qi + (1 - c)), dst_ref=outs[t].at[qi],
            send_sem=send_sems.at[4 * t + qi], recv_sem=recv_sems.at[4 * t + qi],
            device_id=(x, y, 1 - c), device_id_type=MESH) for t in range(n) for qi in range(4)]

    def start(ins, outs, sems):
        for cp in copies(ins, outs, sems):
            cp.start()

    def finish(ins, outs, sems):
        for cp in copies(ins, outs, sems):
            cp.wait()

    sems = [pltpu.SemaphoreType.DMA((4 * n,)), pltpu.SemaphoreType.DMA((4 * n,))]
    return _Comm(grads, outs_sds, sems, start, finish)


def _chip_comm(sums):
    n = len(sums)
    outs_sds = [_sds((3,) + s.shape[1:], s.dtype) for s in sums]

    def copies(ins, outs, sems):
        send_sems, recv_sems = sems
        x, y, c = _place()
        return [pltpu.make_async_remote_copy(
            src_ref=ins[t].at[2 * qx + qy], dst_ref=outs[t].at[j],
            send_sem=send_sems.at[3 * t + j], recv_sem=recv_sems.at[3 * t + j],
            device_id=(qx, qy, c), device_id_type=MESH)
            for t in range(n) for j, (qx, qy) in enumerate([(1 - x, y), (x, 1 - y), (1 - x, 1 - y)])]

    def start(ins, outs, sems):
        for cp in copies(ins, outs, sems):
            cp.start()

    def finish(ins, outs, sems):
        for cp in copies(ins, outs, sems):
            cp.wait()

    sems = [pltpu.SemaphoreType.DMA((3 * n,)), pltpu.SemaphoreType.DMA((3 * n,))]
    return _Comm(sums, outs_sds, sems, start, finish)


def _small_comm(small):
    def copies(ins, outs, sems):
        send_sems, recv_sems, local_sem = sems
        x, y, c = _place()
        mine = outs[0].at[4 * x + 2 * y + c]
        out = [pltpu.make_async_copy(ins[0], mine, local_sem.at[0])]
        for k in range(1, N_DEV):
            peer = (1 - x if k & 4 else x, 1 - y if k & 2 else y, 1 - c if k & 1 else c)
            out.append(pltpu.make_async_remote_copy(
                src_ref=ins[0], dst_ref=mine, send_sem=send_sems.at[k - 1], recv_sem=recv_sems.at[k - 1],
                device_id=peer, device_id_type=MESH))
        return out

    def start(ins, outs, sems):
        for cp in copies(ins, outs, sems):
            cp.start()

    def finish(ins, outs, sems):
        for cp in copies(ins, outs, sems):
            cp.wait()

    sems = [pltpu.SemaphoreType.DMA((N_DEV - 1,)), pltpu.SemaphoreType.DMA((N_DEV - 1,)), pltpu.SemaphoreType.DMA((1,))]
    return _Comm([small], [_sds((N_DEV,) + small.shape, small.dtype)], sems, start, finish)


def _pair_sum(c_idx, grad, recv, axis, size, split, *, name):
    nd = len(grad.shape)
    piece = grad.shape[:axis] + (size,) + grad.shape[axis + 1:]
    blk = (piece[0] // split,) + piece[1:]

    def g_map(q, r, c_ref):
        idx = [0] * nd
        idx[axis] = 2 * q + c_ref[0]
        idx[0] = idx[0] * split + r if axis == 0 else r
        return tuple(idx)

    def r_map(q, r, c_ref):
        return (q, r) + (0,) * (nd - 1)

    def body(c_ref, g_ref, r_ref, o_ref):
        o_ref[0] = (g_ref[...].astype(F32) + r_ref[0].astype(F32)).astype(BF16)

    return pl.pallas_call(
        body,
        grid_spec=pltpu.PrefetchScalarGridSpec(
            num_scalar_prefetch=1, grid=(4, split),
            in_specs=[pl.BlockSpec(blk, g_map), pl.BlockSpec((1,) + blk, r_map)],
            out_specs=pl.BlockSpec((1,) + blk, r_map)),
        out_shape=_sds((4,) + piece, BF16),
        name=name, compiler_params=_params(("arbitrary", "arbitrary"), 32),
    )(c_idx, grad, recv)


def _adam_math(w, g, m, v):
    m = ADAM_B1 * m + (1.0 - ADAM_B1) * g
    v = ADAM_B2 * v + (1.0 - ADAM_B2) * (g * g)
    m_hat = m / (1.0 - ADAM_B1 ** ADAM_STEP)
    v_hat = v / (1.0 - ADAM_B2 ** ADAM_STEP)
    delta = -ADAM_LR * (m_hat / (jnp.sqrt(v_hat) + ADAM_EPS) + ADAM_WD * w)
    return delta, m, v


def _adam_big(q_idx, sums, recv, w, m, v, split, *, name, comm=None):
    shape = w.shape
    nd = len(shape)
    blk = (shape[0] // split,) + shape[1:]
    w_map = lambda r, q_ref: (r,) + (0,) * (nd - 1)
    s_map = lambda r, q_ref: (q_ref[0], r) + (0,) * (nd - 1)
    r_map = lambda r, q_ref: (0, r) + (0,) * (nd - 1)

    def body(q_ref, s_ref, r_ref, w_ref, m_ref, v_ref, g_ref, d_ref, nm_ref, nv_ref):
        g = s_ref[0].astype(F32) + r_ref[0].astype(F32) + r_ref[1].astype(F32) + r_ref[2].astype(F32)
        g_ref[...] = g
        d_ref[...], nm_ref[...], nv_ref[...] = _adam_math(w_ref[...], g, m_ref[...], v_ref[...])

    wspec = pl.BlockSpec(blk, w_map)
    return _call(
        body, grid=(split,), prefetch=q_idx,
        in_specs=[pl.BlockSpec((1,) + blk, s_map), pl.BlockSpec((3,) + blk, r_map), wspec, wspec, wspec],
        out_specs=[wspec] * 4, out_shape=[_sds(shape, F32)] * 4, operands=(sums, recv, w, m, v),
        name=name, params=_params(("arbitrary",), 32), comm=comm)


def _adam_small(parts, w, m, v, *, name):
    R = w.shape[0]

    def body(p_ref, w_ref, m_ref, v_ref, g_ref, d_ref, nm_ref, nv_ref):
        g = p_ref[0]
        for d in range(1, N_DEV):
            g = g + p_ref[d]
        g_ref[...] = g
        d_ref[...], nm_ref[...], nv_ref[...] = _adam_math(w_ref[...], g, m_ref[...], v_ref[...])

    return pl.pallas_call(
        body, out_shape=[_sds((R, LANES), F32)] * 4, name=name,
        compiler_params=pltpu.CompilerParams(vmem_limit_bytes=32 * MIB),
    )(parts, w, m, v)


def _pack(arrs):
    return jnp.concatenate([a.reshape(-1) for a in arrs]).reshape(-1, LANES)


def _unpack(packed, shapes):
    flat = packed.reshape(-1)
    out, off = [], 0
    for s in shapes:
        n = 1
        for d in s:
            n *= d
        out.append(flat[off:off + n].reshape(s))
        off += n
    return out


BIG = ("e_in", "e_out", "o_in", "o_cw", "o_out")
BIG_AXIS = dict(e_in=1, e_out=0, o_in=1, o_cw=1, o_out=0)
BIG_SPLIT = dict(e_in=8, e_out=4, o_in=4, o_cw=4, o_out=4)
REPLICATED = ("e_norm_pre", "e_norm_post", "e_b_conv_bias", "e_b_ln_g", "e_b_ln_b")
SHARDED = ("e_a_conv", "e_b_conv", "o_norm_pre", "o_norm_post", "o_c_b", "o_c_scale")
SMALL = REPLICATED + SHARDED


class _Exchange:
    def __init__(self, shards, small, order, c_idx):
        self.shards = shards
        self.small = small
        self.order = order
        self.c_idx = c_idx
        self.reduced = {}

    def gather(self, keys):
        return _gather_comm([self.shards[k] for k in keys], [BIG_AXIS[k] for k in keys], "all")

    def gather1(self, keys):
        return _gather_comm([self.shards[k] for k in keys], [BIG_AXIS[k] for k in keys], "first")

    def gather2(self, keys, firsts):
        return _gather_comm(firsts, [BIG_AXIS[k] for k in keys], "second")

    def pair(self, grads):
        keys = list(grads)
        return _pair_comm([grads[k] for k in keys], [BIG_AXIS[k] for k in keys],
                          [grads[k].shape[BIG_AXIS[k]] // N_DEV for k in keys])

    def pair_sums(self, grads, received):
        return {k: _pair_sum(self.c_idx, grads[k], r, BIG_AXIS[k], grads[k].shape[BIG_AXIS[k]] // N_DEV,
                             BIG_SPLIT[k], name="pair_sum_" + k) for k, r in zip(grads, received)}

    def chips(self, sums):
        return _chip_comm([sums[k] for k in sums])

    def done(self, sums, received):
        self.reduced.update({k: (sums[k], r) for k, r in zip(sums, received)})


def _local_step(x, tgt, w_small, ex):
    S, D = x.shape
    tnt, tx = min(TM_NT, S), min(TM_MIX, S)

    h0, got = _rms_norm(x, w_small["e_norm_pre"], tm=tx, name="e_norm", comm=_small_comm(ex.small))
    per_dev = [_unpack(got[0][d], [w_small[k].shape for k in SHARDED]) for d in range(N_DEV)]
    sm = {k: w_small[k] for k in REPLICATED}
    for j, k in enumerate(SHARDED):
        sm[k] = jnp.concatenate([per_dev[d][j] for d in range(N_DEV)], axis=-1)
    n_groups = sm["o_c_b"].shape[0]
    sm["o_c_b"] = sm["o_c_b"].reshape(1, -1)

    wt = {}
    p, wt["e_in"], _ = _gather_matmul(ex.order, h0, ex.shards["e_in"], tm=tnt, name="e_in_fwd")
    W = p.shape[1] // 7
    (u, cb), got = _e_mix_fwd(p, sm["e_a_conv"], sm["e_b_conv"], sm["e_b_conv_bias"], sm["e_b_ln_g"],
                              sm["e_b_ln_b"], tm=tx, name="e_mix_fwd", comm=ex.gather(["e_out"]))
    wt["e_out"] = got[0]
    late = ["o_out", "o_cw"]
    (x1, y0), part = _out_norm_res(u, wt["e_out"], x, sm["e_norm_post"], tm=tx, name="e_out_fwd",
                                   comm=ex.gather1(late))
    h1, _ = _rms_norm(x1, sm["o_norm_pre"], tm=tx, name="o_norm")
    q, wt["o_in"], got = _gather_matmul(ex.order, h1, ex.shards["o_in"], tm=tnt, name="o_in_fwd",
                                        comm=ex.gather2(late, part))
    wt.update(zip(late, got))
    yy, pooled, gg = _o_mix_fwd(q, wt["o_cw"], sm["o_c_b"], sm["o_c_scale"], tm=tx, name="o_mix_fwd")
    dout, dx2, dyy, lcol, dg_o_post = _out_loss(yy, wt["o_out"], x1, sm["o_norm_post"], tgt, tm=tx, name="o_out_loss")
    loss = (0.5 / D) * jnp.sum(lcol)

    dq, d_cw, d_cb, d_cscale = _o_mix_bwd(dyy, q, gg, pooled, wt["o_cw"], sm["o_c_scale"], tm=tx, name="o_mix_bwd")
    g_o_out, _ = _mm_tn(yy, dout, ts=tnt, tn=W, name="o_out_dw")
    ga = dict(o_out=g_o_out, o_cw=d_cw.astype(BF16))
    dh1, ra = _mm_nt(dq, wt["o_in"], tm=tnt, tk=W, name="o_in_bwd", comm=ex.pair(ga))
    sa = ex.pair_sums(ga, ra)
    (dx1, dy0, dg_o_pre, dg_e_post), ra = _pre_bwd_o(dh1, x1, dx2, y0, sm["o_norm_pre"], sm["e_norm_post"],
                                                     tm=tx, name="o_pre_bwd", comm=ex.chips(sa))
    ex.done(sa, ra)
    g_o_in, _ = _mm_tn(h1, dq, ts=tnt, tn=W, name="o_in_dw")
    gb = dict(o_in=g_o_in)
    du, rb = _mm_nt(dy0, wt["e_out"], tm=tnt, tk=W, name="e_out_bwd", comm=ex.pair(gb))
    sb = ex.pair_sums(gb, rb)
    g_e_out, _ = _mm_tn(u, dy0, ts=tnt, tn=W, name="e_out_dw")
    gc = dict(e_out=g_e_out)
    (dp, d_wa, d_wb, d_bias, d_lg, d_lb), rbc = _e_mix_bwd(
        du, p, cb, sm["e_a_conv"], sm["e_b_conv"], sm["e_b_ln_g"], sm["e_b_ln_b"], tm=tx, name="e_mix_bwd",
        comm=_merge(ex.chips(sb), ex.pair(gc)))
    ex.done(sb, rbc[:1])
    sc = ex.pair_sums(gc, rbc[1:])
    g_e_in, rc = _mm_tn(h0, dp, ts=tnt, tn=W, name="e_in_dw", comm=ex.chips(sc))
    ex.done(sc, rc)
    gd = dict(e_in=g_e_in)
    sd = ex.pair_sums(gd, _run_comm(ex.pair(gd), "pair_e_in"))
    dh0, rd = _mm_nt(dp, wt["e_in"], tm=tnt, tk=W, name="e_in_bwd", comm=ex.chips(sd))
    ex.done(sd, rd)
    grad_x, dg_e_pre = _pre_bwd_e(dh0, x, dx1, sm["e_norm_pre"], tm=tx, name="e_pre_bwd")

    small = dict(e_norm_pre=dg_e_pre, e_norm_post=dg_e_post, e_a_conv=d_wa, e_b_conv=d_wb, e_b_conv_bias=d_bias,
                 e_b_ln_g=d_lg, e_b_ln_b=d_lb, o_norm_pre=dg_o_pre, o_norm_post=dg_o_post,
                 o_c_b=d_cb.reshape(n_groups, -1), o_c_scale=d_cscale)
    return loss, grad_x, small


def kernel(x, e_norm_pre, e_norm_post, e_w_in, e_a_conv, e_b_conv, e_b_conv_bias, e_b_ln_g, e_b_ln_b, e_w_out, o_norm_pre, o_norm_post, o_w_in, o_c_w, o_c_b, o_c_scale, o_w_out, loss_target, m_e_norm_pre, m_e_norm_post, m_e_w_in, m_e_a_conv, m_e_b_conv, m_e_b_conv_bias, m_e_b_ln_g, m_e_b_ln_b, m_e_w_out, m_o_norm_pre, m_o_norm_post, m_o_w_in, m_o_c_w, m_o_c_b, m_o_c_scale, m_o_w_out, v_e_norm_pre, v_e_norm_post, v_e_w_in, v_e_a_conv, v_e_b_conv, v_e_b_conv_bias, v_e_b_ln_g, v_e_b_ln_b, v_e_w_out, v_o_norm_pre, v_o_norm_post, v_o_w_in, v_o_c_w, v_o_c_b, v_o_c_scale, v_o_w_out):
    xi, yi, ci = _place()
    me = 4 * xi + 2 * yi + ci
    w_big = dict(e_in=e_w_in[0], e_out=e_w_out[0], o_in=o_w_in[0], o_cw=o_c_w[0], o_out=o_w_out[0])
    m_big = dict(e_in=m_e_w_in[0], e_out=m_e_w_out[0], o_in=m_o_w_in[0], o_cw=m_o_c_w[0], o_out=m_o_w_out[0])
    v_big = dict(e_in=v_e_w_in[0], e_out=v_e_w_out[0], o_in=v_o_w_in[0], o_cw=v_o_c_w[0], o_out=v_o_w_out[0])
    w_small = dict(e_norm_pre=e_norm_pre, e_norm_post=e_norm_post, e_b_conv_bias=e_b_conv_bias, e_b_ln_g=e_b_ln_g,
                   e_b_ln_b=e_b_ln_b, e_a_conv=e_a_conv[0], e_b_conv=e_b_conv[0], o_norm_pre=o_norm_pre,
                   o_norm_post=o_norm_post, o_c_b=o_c_b[0], o_c_scale=o_c_scale)
    m_small = dict(e_norm_pre=m_e_norm_pre, e_norm_post=m_e_norm_post, e_b_conv_bias=m_e_b_conv_bias,
                   e_b_ln_g=m_e_b_ln_g, e_b_ln_b=m_e_b_ln_b, e_a_conv=m_e_a_conv[0], e_b_conv=m_e_b_conv[0],
                   o_norm_pre=m_o_norm_pre, o_norm_post=m_o_norm_post, o_c_b=m_o_c_b[0], o_c_scale=m_o_c_scale)
    v_small = dict(e_norm_pre=v_e_norm_pre, e_norm_post=v_e_norm_post, e_b_conv_bias=v_e_b_conv_bias,
                   e_b_ln_g=v_e_b_ln_g, e_b_ln_b=v_e_b_ln_b, e_a_conv=v_e_a_conv[0], e_b_conv=v_e_b_conv[0],
                   o_norm_pre=v_o_norm_pre, o_norm_post=v_o_norm_post, o_c_b=v_o_c_b[0], o_c_scale=v_o_c_scale)

    c_idx = jnp.reshape(ci, (1,)).astype(jnp.int32)
    order = jnp.stack([2 * xi + yi, 2 * (1 - xi) + yi, 2 * xi + (1 - yi), 2 * (1 - xi) + (1 - yi)]).astype(jnp.int32)
    ex = _Exchange({k: w_big[k].astype(BF16) for k in BIG}, _pack([w_small[k] for k in SHARDED]), order, c_idx)
    loss, grad_x, g_small = _local_step(x[0], loss_target[0], w_small, ex)
    loss = lax.psum(loss, ("x", "y", "c"))

    full_shapes = [g_small[k].shape for k in SMALL]
    q_idx = jnp.reshape(2 * xi + yi, (1,)).astype(jnp.int32)
    big_out = {}
    for k in BIG[::-1]:
        comm = _small_comm(_pack([g_small[j] for j in SMALL])) if k == BIG[0] else None
        big_out[k], got = _adam_big(q_idx, *ex.reduced[k], w_big[k], m_big[k], v_big[k], BIG_SPLIT[k],
                                    name="adam_" + k, comm=comm)
    small_parts = got[0]

    def at_full_size(d):
        out = []
        for k, s in zip(SMALL, full_shapes):
            if k in REPLICATED:
                out.append(d[k])
            else:
                n = d[k].shape[-1]
                out.append(lax.dynamic_update_slice_in_dim(jnp.ones(s, F32), d[k], me * n, axis=-1))
        return _pack(out)

    res_small = _adam_small(small_parts, at_full_size(w_small), at_full_size(m_small), at_full_size(v_small),
                            name="adam_small")
    small_out = {k: [] for k in SMALL}
    for packed in res_small:
        for k, full in zip(SMALL, _unpack(packed, full_shapes)):
            if k in SHARDED:
                n = w_small[k].shape[-1]
                full = lax.dynamic_slice_in_dim(full, me * n, n, axis=-1)
            small_out[k].append(full)

    big_of = dict(e_w_in="e_in", e_w_out="e_out", o_w_in="o_in", o_c_w="o_cw", o_w_out="o_out")
    stacked = ("e_a_conv", "e_b_conv", "o_c_b")

    def leaf(name, which):
        if name in big_of:
            return big_out[big_of[name]][which][None]
        t = small_out[name][which]
        return t[None] if name in stacked else t

    order = ("e_norm_pre", "e_norm_post", "e_w_in", "e_a_conv", "e_b_conv", "e_b_conv_bias", "e_b_ln_g", "e_b_ln_b",
             "e_w_out", "o_norm_pre", "o_norm_post", "o_w_in", "o_c_w", "o_c_b", "o_c_scale", "o_w_out")
    outs = [loss, grad_x[None]]
    for which in range(4):
        outs += [leaf(nm, which) for nm in order]
    return tuple(outs)
```

```python
import jax
import jax.numpy as jnp
from jax import lax
from jax.experimental import pallas as pl
from jax.experimental.pallas import tpu as pltpu

F32 = jnp.float32
BF16 = jnp.bfloat16
EPS = 1e-6
MESH = pl.DeviceIdType.MESH
ANY = pl.BlockSpec(memory_space=pl.ANY)

N_DEV = 8
HALO = 32
PHALO = 16
CONV_A = 3
CONV_B = 31
POOL_WINDOWS = (2, 4, 8, 16)
LANES = 128
MIB = 1024 * 1024

ADAM_LR = 0.001
ADAM_B1 = 0.9
ADAM_B2 = 0.999
ADAM_EPS = 1e-08
ADAM_WD = 0.01
ADAM_STEP = 10

TM_NT = 1024
TM_MIX = 256


def _sds(shape, dtype):
    return jax.ShapeDtypeStruct(tuple(shape), dtype)


def _params(sem, vmem_mib):
    return pltpu.CompilerParams(dimension_semantics=sem, vmem_limit_bytes=vmem_mib * MIB)


def _const(shape, single=False):
    n = len(shape)
    if single:
        return pl.BlockSpec(shape, lambda *_: (0,) * n, pipeline_mode=pl.Buffered(1))
    return pl.BlockSpec(shape, lambda *_: (0,) * n)


def _sig(v):
    return jax.nn.sigmoid(v)


def _dsilu(v, s):
    return s * (1.0 + v * (1.0 - s))


def _rms(v):
    return lax.rsqrt(jnp.mean(v * v, axis=-1, keepdims=True) + EPS)


def _norm_bwd(dn, n, r):
    return r * (dn - n * jnp.mean(dn * n, axis=-1, keepdims=True))


def _colsum(v):
    return jnp.sum(v, axis=0, keepdims=True)


class _Comm:
    def __init__(self, inputs, out_shapes, sems, start, finish, aliases=None, middle=None):
        self.inputs, self.out_shapes, self.sems = list(inputs), list(out_shapes), list(sems)
        self.start, self.finish, self.middle = start, finish, middle
        self.aliases = dict(aliases or {})


def _merge(*comms):
    comms = [c for c in comms if c is not None]
    if len(comms) <= 1:
        return comms[0] if comms else None
    spans, i0, o0, s0, aliases = [], 0, 0, 0, {}
    for c in comms:
        spans.append((i0, o0, s0))
        aliases.update({i0 + k: o0 + v for k, v in c.aliases.items()})
        i0, o0, s0 = i0 + len(c.inputs), o0 + len(c.out_shapes), s0 + len(c.sems)

    def run(which):
        def fn(ins, outs, sems):
            for c, (i, o, s) in zip(comms, spans):
                hook = getattr(c, which)
                if hook is not None:
                    hook(ins[i:i + len(c.inputs)], outs[o:o + len(c.out_shapes)], sems[s:s + len(c.sems)])
        return fn

    return _Comm([a for c in comms for a in c.inputs], [a for c in comms for a in c.out_shapes],
                 [a for c in comms for a in c.sems], run("start"), run("finish"), aliases,
                 run("middle") if any(c.middle is not None for c in comms) else None)


def _in_hbm(a):
    return pltpu.with_memory_space_constraint(a, pltpu.HBM)


def _call(body, *, grid, in_specs, out_specs, out_shape, operands, name, params, scratch_shapes=(), comm=None,
          prefetch=None, own_copies_first=False):
    n_p = 0 if prefetch is None else 1
    n_i, n_o, n_s = len(in_specs), len(out_specs), len(scratch_shapes)
    if comm is None:
        comm = _Comm([], [], [], None, None)
    c_i, c_o = len(comm.inputs), len(comm.out_shapes)

    def carrier(*refs):
        pre, refs = refs[:n_p], refs[n_p:]
        ins, cins = refs[:n_i], refs[n_i:n_i + c_i]
        outs = refs[n_i + c_i:n_i + c_i + n_o]
        couts = refs[n_i + c_i + n_o:n_i + c_i + n_o + c_o]
        scr = refs[n_i + c_i + n_o + c_o:n_i + c_i + n_o + c_o + n_s]
        csems = refs[n_i + c_i + n_o + c_o + n_s:]
        ids = [pl.program_id(d) for d in range(len(grid))]
        first = ids[0] == 0
        half = ids[0] == grid[0] // 2
        last = ids[0] == grid[0] - 1
        for d in range(1, len(grid)):
            first = first & (ids[d] == 0)
            half = half & (ids[d] == 0)
            last = last & (ids[d] == grid[d] - 1)

        def start():
            if comm.start is not None:
                @pl.when(first)
                def _():
                    comm.start(cins, couts, csems)

        if not own_copies_first:
            start()
        if comm.middle is not None:
            assert grid[0] >= 2

            @pl.when(half)
            def _():
                comm.middle(cins, couts, csems)

        body(*pre, *ins, *outs, *scr)
        if own_copies_first:
            start()

        if comm.finish is not None:
            @pl.when(last)
            def _():
                comm.finish(cins, couts, csems)

    specs = dict(grid=grid, in_specs=list(in_specs) + [ANY] * c_i, out_specs=list(out_specs) + [ANY] * c_o,
                 scratch_shapes=list(scratch_shapes) + comm.sems)
    if n_p:
        specs = dict(grid_spec=pltpu.PrefetchScalarGridSpec(num_scalar_prefetch=1, **specs))
    res = pl.pallas_call(
        carrier, out_shape=list(out_shape) + comm.out_shapes,
        input_output_aliases={n_p + n_i + k: n_o + v for k, v in comm.aliases.items()},
        name=name, compiler_params=params, **specs)(
            *(() if prefetch is None else (prefetch,)), *map(_in_hbm, operands), *map(_in_hbm, comm.inputs))
    return list(res[:n_o]), list(res[n_o:])


def _run_comm(comm, name):
    c_i, c_o = len(comm.inputs), len(comm.out_shapes)

    def body(*refs):
        ins, outs, sems = refs[:c_i], refs[c_i:c_i + c_o], refs[c_i + c_o:]
        comm.start(ins, outs, sems)
        comm.finish(ins, outs, sems)

    res = pl.pallas_call(
        body, in_specs=[ANY] * c_i, out_specs=[ANY] * c_o, out_shape=comm.out_shapes, scratch_shapes=comm.sems,
        input_output_aliases=comm.aliases, name=name)(*map(_in_hbm, comm.inputs))
    return list(res)


def _rms_norm(x, g, *, tm, name, comm=None):
    S, D = x.shape

    def body(x_ref, g_ref, h_ref):
        xx = x_ref[...]
        h_ref[...] = ((xx * _rms(xx)) * g_ref[...]).astype(BF16)

    row = pl.BlockSpec((tm, D), lambda i: (i, 0))
    outs, extra = _call(body, grid=(S // tm,), in_specs=[row, _const((1, D))], out_specs=[row],
                        out_shape=[_sds((S, D), BF16)], operands=(x, g), name=name,
                        params=_params(("arbitrary",), 32), comm=comm)
    return outs[0], extra


def _gather_matmul(order, h, shard, *, tm, name, comm=None):
    S, K = h.shape
    nb = shard.shape[1]
    n_i = S // tm

    def body(order_ref, h_ref, shard_ref, p_ref, full_ref, wbuf, send_sems, recv_sems, dma_sems):
        j, i = pl.program_id(0), pl.program_id(1)
        x, y, c = _place()
        cps = _gather_copies(shard_ref, full_ref, 1, nb, send_sems, recv_sems, 0)

        def load(qx, qy):
            cp = pltpu.make_async_copy(_piece(full_ref, 1, 2 * nb, 2 * qx + qy), wbuf, dma_sems.at[1])
            cp.start()
            cp.wait()

        @pl.when((j == 0) & (i == 0))
        def _():
            own = pltpu.make_async_copy(shard_ref, _piece(full_ref, 1, nb, 4 * x + 2 * y + c), dma_sems.at[0])
            own.start()
            for k in (0, 1, 2):
                cps[k].start()
            own.wait()
            cps[0].wait_recv()
            load(x, y)

        @pl.when((j == 1) & (i == 0))
        def _():
            cps[1].wait_recv()
            cps[3].start()
            cps[5].start()
            cps[2].wait_recv()
            cps[4].start()
            cps[6].start()
            cps[5].wait_recv()
            load(1 - x, y)

        @pl.when((j == 2) & (i == 0))
        def _():
            cps[6].wait_recv()
            load(x, 1 - y)

        @pl.when((j == 3) & (i == 0))
        def _():
            cps[3].wait_recv()
            cps[4].wait_recv()
            cps[7].start()
            cps[7].wait_recv()
            load(1 - x, 1 - y)

        p_ref[...] = jnp.dot(h_ref[...], wbuf[...], preferred_element_type=F32).astype(BF16)

        @pl.when((j == 3) & (i == n_i - 1))
        def _():
            for cp in cps:
                cp.wait_send()

    outs, extra = _call(
        body, grid=(4, n_i), prefetch=order,
        in_specs=[pl.BlockSpec((tm, K), lambda j, i, o: (i, 0)), ANY],
        out_specs=[pl.BlockSpec((tm, 2 * nb), lambda j, i, o: (i, o[j])), ANY],
        out_shape=[_sds((S, N_DEV * nb), BF16), _sds((K, N_DEV * nb), BF16)], operands=(h, shard),
        scratch_shapes=[pltpu.VMEM((K, 2 * nb), BF16), pltpu.SemaphoreType.DMA((N_GATHER,)),
                        pltpu.SemaphoreType.DMA((N_GATHER,)), pltpu.SemaphoreType.DMA((2,))],
        name=name, params=_params(("arbitrary", "arbitrary"), 48), comm=comm, own_copies_first=True)
    return outs[0], outs[1], extra


def _out_norm_res(u, w, x, g, *, tm, name, comm=None):
    S, K = u.shape
    D = w.shape[1]

    def body(u_ref, w_ref, x_ref, g_ref, x1_ref, y_ref):
        y = jnp.dot(u_ref[...], w_ref[...], preferred_element_type=F32)
        y_ref[...] = y.astype(BF16)
        x1_ref[...] = x_ref[...] + (y * _rms(y)) * g_ref[...]

    return _call(
        body, grid=(S // tm,),
        in_specs=[pl.BlockSpec((tm, K), lambda i: (i, 0)), _const((K, D), single=True),
                  pl.BlockSpec((tm, D), lambda i: (i, 0)), _const((1, D))],
        out_specs=[pl.BlockSpec((tm, D), lambda i: (i, 0)), pl.BlockSpec((tm, D), lambda i: (i, 0))],
        out_shape=[_sds((S, D), F32), _sds((S, D), BF16)], operands=(u, w, x, g),
        name=name, params=_params(("arbitrary",), 48), comm=comm)


def _out_loss(yy, w, x1, g, tgt, *, tm, name):
    S, K = yy.shape
    D = w.shape[1]

    def body(yy_ref, w_ref, x1_ref, g_ref, t_ref, dout_ref, dx2_ref, dyy_ref, lcol_ref, dg_ref):
        out = jnp.dot(yy_ref[...], w_ref[...], preferred_element_type=F32)
        r = _rms(out)
        n = out * r
        gg = g_ref[...]
        e = x1_ref[...] + n * gg - t_ref[...]
        dx2 = e * (1.0 / D)
        dx2_ref[...] = dx2
        dout = _norm_bwd(dx2 * gg, n, r).astype(BF16)
        dout_ref[...] = dout
        dyy_ref[...] = lax.dot_general(dout, w_ref[...], (((1,), (1,)), ((), ())),
                                       preferred_element_type=F32).astype(BF16)

        @pl.when(pl.program_id(0) == 0)
        def _():
            lcol_ref[...] = jnp.zeros_like(lcol_ref)
            dg_ref[...] = jnp.zeros_like(dg_ref)

        lcol_ref[...] += _colsum(e * e)
        dg_ref[...] += _colsum(dx2 * n)

    return _call(
        body, grid=(S // tm,),
        in_specs=[pl.BlockSpec((tm, K), lambda i: (i, 0)), _const((K, D), single=True),
                  pl.BlockSpec((tm, D), lambda i: (i, 0)), _const((1, D)),
                  pl.BlockSpec((tm, D), lambda i: (i, 0))],
        out_specs=[pl.BlockSpec((tm, D), lambda i: (i, 0)), pl.BlockSpec((tm, D), lambda i: (i, 0)),
                   pl.BlockSpec((tm, K), lambda i: (i, 0)), _const((1, D)), _const((1, D))],
        out_shape=[_sds((S, D), BF16), _sds((S, D), F32), _sds((S, K), BF16), _sds((1, D), F32), _sds((1, D), F32)],
        operands=(yy, w, x1, g, tgt), name=name, params=_params(("arbitrary",), 52))[0]


def _mm_nt(a, w, *, tm, tk, name, comm=None):
    S, N = a.shape
    D = w.shape[0]
    n_k = N // tk

    def body(a_ref, w_ref, o_ref, acc_ref):
        k = pl.program_id(1)

        @pl.when(k == 0)
        def _():
            acc_ref[...] = jnp.zeros_like(acc_ref)

        acc_ref[...] = lax.dot_general(a_ref[...], w_ref[...], (((1,), (1,)), ((), ())),
                                       preferred_element_type=F32) + acc_ref[...]

        @pl.when(k == n_k - 1)
        def _():
            o_ref[...] = acc_ref[...].astype(BF16)

    outs, extra = _call(
        body, grid=(S // tm, n_k),
        in_specs=[pl.BlockSpec((tm, tk), lambda i, k: (i, k)), pl.BlockSpec((D, tk), lambda i, k: (0, k))],
        out_specs=[pl.BlockSpec((tm, D), lambda i, k: (i, 0))],
        out_shape=[_sds((S, D), BF16)], operands=(a, w),
        scratch_shapes=[pltpu.VMEM((tm, D), F32)],
        name=name, params=_params(("arbitrary", "arbitrary"), 48), comm=comm)
    return outs[0], extra


def _mm_tn(a, b, *, ts, tn, name, comm=None):
    S, M = a.shape
    N = b.shape[1]
    n_s = S // ts

    def body(a_ref, b_ref, o_ref, acc_ref):
        s = pl.program_id(1)

        @pl.when(s == 0)
        def _():
            acc_ref[...] = jnp.zeros_like(acc_ref)

        acc_ref[...] = lax.dot_general(a_ref[...], b_ref[...], (((0,), (0,)), ((), ())),
                                       preferred_element_type=F32) + acc_ref[...]

        @pl.when(s == n_s - 1)
        def _():
            o_ref[...] = acc_ref[...].astype(BF16)

    outs, extra = _call(
        body, grid=(N // tn, n_s),
        in_specs=[pl.BlockSpec((ts, M), lambda j, s: (s, 0)), pl.BlockSpec((ts, tn), lambda j, s: (s, j))],
        out_specs=[pl.BlockSpec((M, tn), lambda j, s: (0, j))],
        out_shape=[_sds((M, N), BF16)], operands=(a, b),
        scratch_shapes=[pltpu.VMEM((M, tn), F32)],
        name=name, params=_params(("arbitrary", "arbitrary"), 48), comm=comm)
    return outs[0], extra


def _pre_bwd_o(dh, x1, dx2, y0, g_pre, g_post, *, tm, name, comm=None):
    S, D = x1.shape

    def body(dh_ref, x1_ref, dx2_ref, y0_ref, gpre_ref, gpost_ref, dx1_ref, dy0_ref, dgpre_ref, dgpost_ref):
        @pl.when(pl.program_id(0) == 0)
        def _():
            dgpre_ref[...] = jnp.zeros_like(dgpre_ref)
            dgpost_ref[...] = jnp.zeros_like(dgpost_ref)

        dh = dh_ref[...].astype(F32)
        x1 = x1_ref[...]
        r2 = _rms(x1)
        xn = x1 * r2
        dgpre_ref[...] += _colsum(dh * xn)
        dx1 = dx2_ref[...] + _norm_bwd(dh * gpre_ref[...], xn, r2)
        dx1_ref[...] = dx1
        y = y0_ref[...].astype(F32)
        r1 = _rms(y)
        n1 = y * r1
        dgpost_ref[...] += _colsum(dx1 * n1)
        dy0_ref[...] = _norm_bwd(dx1 * gpost_ref[...], n1, r1).astype(BF16)

    row = pl.BlockSpec((tm, D), lambda i: (i, 0))
    return _call(
        body, grid=(S // tm,),
        in_specs=[row, row, row, row, _const((1, D)), _const((1, D))],
        out_specs=[row, row, _const((1, D)), _const((1, D))],
        out_shape=[_sds((S, D), F32), _sds((S, D), BF16), _sds((1, D), F32), _sds((1, D), F32)],
        operands=(dh, x1, dx2, y0, g_pre, g_post),
        name=name, params=_params(("arbitrary",), 48), comm=comm)


def _pre_bwd_e(dh, x, dx1, g_pre, *, tm, name):
    S, D = x.shape

    def body(dh_ref, x_ref, dx1_ref, gpre_ref, gx_ref, dgpre_ref):
        @pl.when(pl.program_id(0) == 0)
        def _():
            dgpre_ref[...] = jnp.zeros_like(dgpre_ref)

        dh = dh_ref[...].astype(F32)
        xx = x_ref[...]
        r0 = _rms(xx)
        xn = xx * r0
        dgpre_ref[...] += _colsum(dh * xn)
        gx_ref[...] = dx1_ref[...] + _norm_bwd(dh * gpre_ref[...], xn, r0)

    row = pl.BlockSpec((tm, D), lambda i: (i, 0))
    return _call(
        body, grid=(S // tm,),
        in_specs=[row, row, row, _const((1, D))],
        out_specs=[row, _const((1, D))],
        out_shape=[_sds((S, D), F32), _sds((1, D), F32)],
        operands=(dh, x, dx1, g_pre), name=name, params=_params(("arbitrary",), 48))[0]


SUBLANES = 8


def _shift_copies(sh_ref, ext_ref, cs):
    for b in range(1, SUBLANES):
        sh_ref[b - 1] = ext_ref[pl.ds(b, sh_ref.shape[1]), cs]


def _rows_at(ext_ref, sh_ref, off, cs, tm):
    b = off % SUBLANES
    if b == 0 or sh_ref is None:
        return ext_ref[pl.ds(off, tm), cs]
    return sh_ref[b - 1, pl.ds(off - b, tm), :]


def _taps(ext_ref, w_ref, n_taps, base, cs, tm, sh_ref=None):
    acc = _rows_at(ext_ref, sh_ref, base, cs, tm) * w_ref[0:1, cs]
    for k in range(1, n_taps):
        acc = acc + _rows_at(ext_ref, sh_ref, base + k, cs, tm) * w_ref[k:k + 1, cs]
    return acc


def _taps_rev(ext_ref, w_ref, n_taps, cs, tm, sh_ref=None):
    acc = _rows_at(ext_ref, sh_ref, n_taps - 1, cs, tm) * w_ref[0:1, cs]
    for k in range(1, n_taps):
        acc = acc + _rows_at(ext_ref, sh_ref, n_taps - 1 - k, cs, tm) * w_ref[k:k + 1, cs]
    return acc


def _e_mix_fwd(p, wa, wb, bias, ln_g, ln_b, *, tm, name, comm=None):
    S = p.shape[0]
    W = p.shape[1] // 7
    nb = tm // HALO
    chunks = [slice(c * LANES, (c + 1) * LANES) for c in range(W // LANES)]

    def body(p_ref, hax_ref, hac_ref, hbv_ref, hbg_ref, wa_ref, wb_ref, bias_ref, lg_ref, lb_ref,
             u_ref, cb_ref, ext_ref, sh_ref):
        keep = (pl.program_id(0) > 0).astype(F32)
        col = lambda j, cs: p_ref[:, j * W + cs.start:j * W + cs.stop].astype(F32)

        ext_ref[0:HALO, :] = hax_ref[...].astype(F32) * hac_ref[...].astype(F32) * keep
        ext_ref[HALO:, :] = p_ref[:, 2 * W:3 * W].astype(F32) * p_ref[:, 0:W].astype(F32)
        for cs in chunks:
            conv = _taps(ext_ref, wa_ref, CONV_A, HALO - (CONV_A - 1), cs, tm)
            az = col(3, cs)
            u_ref[:, cs] = (col(1, cs) * conv * (az * _sig(az))).astype(BF16)

        ext_ref[0:HALO, :] = hbv_ref[...].astype(F32) * _sig(hbg_ref[...].astype(F32)) * keep
        ext_ref[HALO:, :] = p_ref[:, 4 * W:5 * W].astype(F32) * _sig(p_ref[:, 5 * W:6 * W].astype(F32))
        s1 = jnp.zeros((tm, LANES), F32)
        for cs in chunks:
            _shift_copies(sh_ref, ext_ref, cs)
            cb = _taps(ext_ref, wb_ref, CONV_B, HALO - (CONV_B - 1), cs, tm, sh_ref) + bias_ref[:, cs]
            cb_ref[:, cs] = cb
            s1 = s1 + cb
        mu = jnp.sum(s1, axis=-1, keepdims=True) * (1.0 / W)
        s2 = jnp.zeros((tm, LANES), F32)
        for cs in chunks:
            xc = cb_ref[:, cs] - mu
            s2 = s2 + xc * xc
        rs = lax.rsqrt(jnp.sum(s2, axis=-1, keepdims=True) * (1.0 / W) + EPS)
        for cs in chunks:
            lb = (cb_ref[:, cs] - mu) * rs * lg_ref[:, cs] + lb_ref[:, cs]
            bz = col(6, cs)
            u_ref[:, W + cs.start:W + cs.stop] = (lb * _sig(lb) * (bz * _sig(bz))).astype(BF16)

    prev = lambda j: pl.BlockSpec((HALO, W), lambda i: (jnp.maximum(i * nb - 1, 0), j))
    return _call(
        body, grid=(S // tm,),
        in_specs=[pl.BlockSpec((tm, 7 * W), lambda i: (i, 0)), prev(0), prev(2), prev(4), prev(5),
                  _const((CONV_A, W)), _const((CONV_B, W)), _const((1, W)), _const((1, W)), _const((1, W))],
        out_specs=[pl.BlockSpec((tm, 2 * W), lambda i: (i, 0)), pl.BlockSpec((tm, W), lambda i: (i, 0))],
        out_shape=[_sds((S, 2 * W), BF16), _sds((S, W), F32)],
        operands=(p, p, p, p, p, wa, wb, bias, ln_g, ln_b),
        scratch_shapes=[pltpu.VMEM((HALO + tm, W), F32),
                        pltpu.VMEM((SUBLANES - 1, HALO + tm - SUBLANES, LANES), F32)],
        name=name, params=_params(("arbitrary",), 48), comm=comm)


def _e_mix_bwd(du, p, cb, wa, wb, ln_g, ln_b, *, tm, name, comm=None):
    S = p.shape[0]
    W = p.shape[1] // 7
    nb = tm // HALO
    n_t = S // tm
    last_blk = S // HALO - 1
    chunks = [slice(c * LANES, (c + 1) * LANES) for c in range(W // LANES)]

    def body(du_ref, duf_ref, p_ref, fab_ref, faz_ref, fbz_ref, hax_ref, hac_ref, hbv_ref, hbg_ref,
             cb_ref, cbf_ref, wa_ref, wb_ref, lg_ref, lb_ref,
             dp_ref, dwa_ref, dwb_ref, dbias_ref, dlg_ref, dlb_ref, extd_ref, extg_ref, shd_ref, shg_ref):
        i = pl.program_id(0)
        keep_prev = (i > 0).astype(F32)
        keep_next = (i < n_t - 1).astype(F32)
        col = lambda j, cs: p_ref[:, j * W + cs.start:j * W + cs.stop].astype(F32)

        @pl.when(i == 0)
        def _():
            dwa_ref[...] = jnp.zeros_like(dwa_ref)
            dwb_ref[...] = jnp.zeros_like(dwb_ref)
            dbias_ref[...] = jnp.zeros_like(dbias_ref)
            dlg_ref[...] = jnp.zeros_like(dlg_ref)
            dlb_ref[...] = jnp.zeros_like(dlb_ref)

        def dcb_rows(rows, cb_rows_ref, dub, bz_of, dst0, scale, main):
            cbv = cb_rows_ref[...]
            mu = jnp.mean(cbv, axis=-1, keepdims=True)
            xc = cbv - mu
            rs = lax.rsqrt(jnp.mean(xc * xc, axis=-1, keepdims=True) + EPS)
            m1 = jnp.zeros((rows, LANES), F32)
            m2 = jnp.zeros((rows, LANES), F32)
            for cs in chunks:
                nbv = (cb_rows_ref[:, cs] - mu) * rs
                lb = nbv * lg_ref[:, cs] + lb_ref[:, cs]
                sl = _sig(lb)
                bz = bz_of(cs)
                sz = _sig(bz)
                dub_c = dub(cs)
                dlb = dub_c * (bz * sz) * _dsilu(lb, sl)
                if main:
                    dlg_ref[:, cs] += _colsum(dlb * nbv)
                    dlb_ref[:, cs] += _colsum(dlb)
                    dp_ref[:, 6 * W + cs.start:6 * W + cs.stop] = (dub_c * (lb * sl) * _dsilu(bz, sz)).astype(BF16)
                dnb = dlb * lg_ref[:, cs]
                extd_ref[dst0:dst0 + rows, cs] = dnb
                m1 = m1 + dnb
                m2 = m2 + dnb * nbv
            m1 = jnp.sum(m1, axis=-1, keepdims=True) * (1.0 / W)
            m2 = jnp.sum(m2, axis=-1, keepdims=True) * (1.0 / W)
            for cs in chunks:
                nbv = (cb_rows_ref[:, cs] - mu) * rs
                dcb = rs * (extd_ref[dst0:dst0 + rows, cs] - m1 - nbv * m2) * scale
                extd_ref[dst0:dst0 + rows, cs] = dcb
                if main:
                    dbias_ref[:, cs] += _colsum(dcb)

        dcb_rows(tm, cb_ref, lambda cs: du_ref[:, W + cs.start:W + cs.stop].astype(F32),
                 lambda cs: col(6, cs), 0, 1.0, True)
        dcb_rows(HALO, cbf_ref, lambda cs: duf_ref[:, W + cs.start:W + cs.stop].astype(F32),
                 lambda cs: fbz_ref[:, cs].astype(F32), tm, keep_next, False)

        extg_ref[0:HALO, :] = hbv_ref[...].astype(F32) * _sig(hbg_ref[...].astype(F32)) * keep_prev
        extg_ref[HALO:, :] = p_ref[:, 4 * W:5 * W].astype(F32) * _sig(p_ref[:, 5 * W:6 * W].astype(F32))
        base_b = HALO - (CONV_B - 1)
        for cs in chunks:
            _shift_copies(shd_ref, extd_ref, cs)
            _shift_copies(shg_ref, extg_ref, cs)
            dgb = _taps_rev(extd_ref, wb_ref, CONV_B, cs, tm, shd_ref)
            bv = col(4, cs)
            sg = _sig(col(5, cs))
            dp_ref[:, 4 * W + cs.start:4 * W + cs.stop] = (dgb * sg).astype(BF16)
            dp_ref[:, 5 * W + cs.start:5 * W + cs.stop] = (dgb * bv * sg * (1.0 - sg)).astype(BF16)
            dcb = extd_ref[0:tm, cs]
            for k in range(CONV_B):
                dwb_ref[k:k + 1, cs] += _colsum(dcb * _rows_at(extg_ref, shg_ref, base_b + k, cs, tm))

        extg_ref[0:HALO, :] = hax_ref[...].astype(F32) * hac_ref[...].astype(F32) * keep_prev
        extg_ref[HALO:, :] = p_ref[:, 2 * W:3 * W].astype(F32) * p_ref[:, 0:W].astype(F32)
        base_a = HALO - (CONV_A - 1)
        for cs in chunks:
            conv = _taps(extg_ref, wa_ref, CONV_A, base_a, cs, tm)
            az = col(3, cs)
            sz = _sig(az)
            ab = col(1, cs)
            dua = du_ref[:, cs].astype(F32)
            dya = dua * (az * sz)
            dp_ref[:, W + cs.start:W + cs.stop] = (dya * conv).astype(BF16)
            dp_ref[:, 3 * W + cs.start:3 * W + cs.stop] = (dua * (ab * conv) * _dsilu(az, sz)).astype(BF16)
            extd_ref[0:tm, cs] = dya * ab
            azf = faz_ref[:, cs].astype(F32)
            extd_ref[tm:tm + HALO, cs] = (duf_ref[:, cs].astype(F32) * (azf * _sig(azf))
                                          * fab_ref[:, cs].astype(F32) * keep_next)
        for cs in chunks:
            dca = _taps_rev(extd_ref, wa_ref, CONV_A, cs, tm)
            dp_ref[:, cs] = (dca * col(2, cs)).astype(BF16)
            dp_ref[:, 2 * W + cs.start:2 * W + cs.stop] = (dca * col(0, cs)).astype(BF16)
            dconv = extd_ref[0:tm, cs]
            for k in range(CONV_A):
                dwa_ref[k:k + 1, cs] += _colsum(dconv * extg_ref[pl.ds(base_a + k, tm), cs])

    prev = lambda j: pl.BlockSpec((HALO, W), lambda i: (jnp.maximum(i * nb - 1, 0), j))
    nxt = lambda j, w: pl.BlockSpec((HALO, w), lambda i: (jnp.minimum((i + 1) * nb, last_blk), j))
    row = lambda w: pl.BlockSpec((tm, w), lambda i: (i, 0))
    return _call(
        body, grid=(n_t,),
        in_specs=[row(2 * W), nxt(0, 2 * W), row(7 * W), nxt(1, W), nxt(3, W), nxt(6, W),
                  prev(0), prev(2), prev(4), prev(5), row(W), nxt(0, W),
                  _const((CONV_A, W)), _const((CONV_B, W)), _const((1, W)), _const((1, W))],
        out_specs=[row(7 * W), _const((CONV_A, W)), _const((CONV_B, W)), _const((1, W)), _const((1, W)), _const((1, W))],
        out_shape=[_sds((S, 7 * W), BF16), _sds((CONV_A, W), F32), _sds((CONV_B, W), F32),
                   _sds((1, W), F32), _sds((1, W), F32), _sds((1, W), F32)],
        operands=(du, du, p, p, p, p, p, p, p, p, cb, cb, wa, wb, ln_g, ln_b),
        scratch_shapes=[pltpu.VMEM((tm + HALO, W), F32), pltpu.VMEM((HALO + tm, W), F32),
                        pltpu.VMEM((SUBLANES - 1, HALO + tm - SUBLANES, LANES), F32),
                        pltpu.VMEM((SUBLANES - 1, HALO + tm - SUBLANES, LANES), F32)],
        name=name, params=_params(("arbitrary",), 52), comm=comm)


def _counts(i, tm, rows, off, win):
    t = i * tm + off + lax.broadcasted_iota(jnp.int32, (rows, 1), 0)
    return jnp.minimum(t + 1, win).astype(F32)


def _o_mix_fwd(q, cw, cb, cscale, *, tm, name):
    S = q.shape[0]
    WC = q.shape[1] // 2
    NG = len(POOL_WINDOWS)
    G = WC // NG
    nb = tm // PHALO

    def body(v_ref, z_ref, hv_ref, cw_ref, cb_ref, sc_ref, yy_ref, pooled_ref, gg_ref, ext_ref):
        i = pl.program_id(0)
        keep = (i > 0).astype(F32)
        for g, win in enumerate(POOL_WINDOWS):
            cs = slice(g * G, (g + 1) * G)
            v = v_ref[:, cs].astype(F32)
            ext_ref[0:PHALO, :] = hv_ref[:, cs].astype(F32) * keep
            ext_ref[PHALO:, :] = v
            s = v
            for j in range(1, win):
                s = s + ext_ref[pl.ds(PHALO - j, tm), :]
            pooled = (s / _counts(i, tm, tm, 0, win) - v).astype(BF16)
            pooled_ref[:, cs] = pooled
            gg = jnp.dot(pooled, cw_ref[g], preferred_element_type=F32) + cb_ref[:, cs]
            gg_ref[:, cs] = gg.astype(BF16)
            z = z_ref[:, cs].astype(F32)
            yy_ref[:, cs] = (gg * sc_ref[:, cs] * (z * _sig(z))).astype(BF16)

    row = lambda j: pl.BlockSpec((tm, WC), lambda i: (i, j))
    out = pl.BlockSpec((tm, WC), lambda i: (i, 0))
    return _call(
        body, grid=(S // tm,),
        in_specs=[row(0), row(1), pl.BlockSpec((PHALO, WC), lambda i: (jnp.maximum(i * nb - 1, 0), 0)),
                  _const((NG, G, G)), _const((1, WC)), _const((1, WC))],
        out_specs=[out, out, out],
        out_shape=[_sds((S, WC), BF16)] * 3, operands=(q, q, q, cw, cb, cscale),
        scratch_shapes=[pltpu.VMEM((PHALO + tm, G), F32)],
        name=name, params=_params(("arbitrary",), 40))[0]


def _o_mix_bwd(dyy, q, gg, pooled, cw, cscale, *, tm, name):
    S = q.shape[0]
    WC = q.shape[1] // 2
    NG = len(POOL_WINDOWS)
    G = WC // NG
    nb = tm // PHALO
    n_t = S // tm
    last_blk = S // PHALO - 1
    nt = (((1,), (1,)), ((), ()))
    tn = (((0,), (0,)), ((), ()))

    def body(dyy_ref, dyyf_ref, z_ref, zf_ref, gg_ref, pooled_ref, cw_ref, sc_ref,
             dq_ref, dcw_ref, dcb_ref, dsc_ref, ext_ref):
        i = pl.program_id(0)
        keep_next = (i < n_t - 1).astype(F32)

        @pl.when(i == 0)
        def _():
            dcw_ref[...] = jnp.zeros_like(dcw_ref)
            dcb_ref[...] = jnp.zeros_like(dcb_ref)
            dsc_ref[...] = jnp.zeros_like(dsc_ref)

        for g, win in enumerate(POOL_WINDOWS):
            cs = slice(g * G, (g + 1) * G)
            sc = sc_ref[:, cs]
            z = z_ref[:, cs].astype(F32)
            sz = _sig(z)
            dyy_c = dyy_ref[:, cs].astype(F32)
            ggv = gg_ref[:, cs].astype(F32)
            dyy0 = dyy_c * (z * sz)
            dq_ref[:, WC + cs.start:WC + cs.stop] = (dyy_c * (ggv * sc) * _dsilu(z, sz)).astype(BF16)
            dgg = dyy0 * sc
            dsc_ref[:, cs] += _colsum(dyy0 * ggv)
            dcb_ref[:, cs] += _colsum(dgg)
            dgg_b = dgg.astype(BF16)
            dcw_ref[g] += lax.dot_general(pooled_ref[:, cs], dgg_b, tn, preferred_element_type=F32)
            dpool = lax.dot_general(dgg_b, cw_ref[g], nt, preferred_element_type=F32)
            zf = zf_ref[:, cs].astype(F32)
            dgg_f = (dyyf_ref[:, cs].astype(F32) * (zf * _sig(zf)) * sc * keep_next).astype(BF16)
            dpool_f = lax.dot_general(dgg_f, cw_ref[g], nt, preferred_element_type=F32)
            ext_ref[0:tm, :] = dpool / _counts(i, tm, tm, 0, win)
            ext_ref[tm:tm + PHALO, :] = dpool_f / _counts(i, tm, PHALO, tm, win)
            dv = ext_ref[0:tm, :] - dpool
            for j in range(1, win):
                dv = dv + ext_ref[pl.ds(j, tm), :]
            dq_ref[:, cs] = dv.astype(BF16)

    row = lambda: pl.BlockSpec((tm, WC), lambda i: (i, 0))
    nxt = lambda j: pl.BlockSpec((PHALO, WC), lambda i: (jnp.minimum((i + 1) * nb, last_blk), j))
    return _call(
        body, grid=(n_t,),
        in_specs=[row(), nxt(0), pl.BlockSpec((tm, WC), lambda i: (i, 1)), nxt(1), row(), row(),
                  _const((NG, G, G)), _const((1, WC))],
        out_specs=[pl.BlockSpec((tm, 2 * WC), lambda i: (i, 0)), _const((NG, G, G)), _const((1, WC)), _const((1, WC))],
        out_shape=[_sds((S, 2 * WC), BF16), _sds((NG, G, G), F32), _sds((1, WC), F32), _sds((1, WC), F32)],
        operands=(dyy, dyy, q, q, gg, pooled, cw, cscale),
        scratch_shapes=[pltpu.VMEM((tm + PHALO, G), F32)],
        name=name, params=_params(("arbitrary",), 48))[0]


def _place():
    return lax.axis_index("x"), lax.axis_index("y"), lax.axis_index("c")


def _piece(ref, axis, size, index):
    start = index * size
    if axis == len(ref.shape) - 1:
        start = pl.multiple_of(start, LANES)
    idx = [slice(None)] * len(ref.shape)
    idx[axis] = pl.ds(start, size)
    return ref.at[tuple(idx)]


def _gather_copies(src, out, axis, size, send_sems, recv_sems, base, held=None):
    x, y, c = _place()
    sib, xn, yn = (x, y, 1 - c), (1 - x, y, c), (x, 1 - y, c)

    def blk(px, py, of=out):
        return _piece(of, axis, size, 4 * px + 2 * py + c)

    def half(ref, h):
        n = ref.shape[0] // 2
        return ref.at[pl.ds(h * n, n)]

    def rc(k, s, d, to):
        return pltpu.make_async_remote_copy(src_ref=s, dst_ref=d, send_sem=send_sems.at[base + k],
                                            recv_sem=recv_sems.at[base + k], device_id=to, device_id_type=MESH)

    own, xb, yb, db = blk(x, y), blk(1 - x, y), blk(x, 1 - y), blk(1 - x, 1 - y)
    got = out if held is None else held
    xs, ys, ds = blk(1 - x, y, got), blk(x, 1 - y, got), blk(1 - x, 1 - y, got)
    return [rc(0, src, own, sib), rc(1, src, own, xn), rc(2, src, own, yn),
            rc(3, half(xs, 0), half(xb, 0), yn), rc(4, half(ys, 1), half(yb, 1), xn),
            rc(5, xs, xb, sib), rc(6, ys, yb, sib), rc(7, ds, db, sib)]


N_GATHER = 8


def _gather_comm(shards, axes, phases):
    n = len(shards)
    if phases == "second":
        sizes = [s.shape[a] // N_DEV for s, a in zip(shards, axes)]
        full = [_sds(s.shape, s.dtype) for s in shards]
    else:
        sizes = [s.shape[a] for s, a in zip(shards, axes)]
        full = [_sds(s.shape[:a] + (N_DEV * s.shape[a],) + s.shape[a + 1:], s.dtype) for s, a in zip(shards, axes)]

    def plan(ins, outs, sems):
        x, y, c = _place()
        me = 4 * x + 2 * y + c
        if phases == "second":
            cps = [_gather_copies(_piece(ins[t], axes[t], sizes[t], me), outs[t], axes[t], sizes[t], sems[0], sems[1],
                                  N_GATHER * t, ins[t]) for t in range(n)]
        else:
            cps = [_gather_copies(ins[t], outs[t], axes[t], sizes[t], sems[0], sems[1], N_GATHER * t)
                   for t in range(n)]
        mine = [pltpu.make_async_copy(ins[t], _piece(outs[t], axes[t], sizes[t], me), sems[2].at[t])
                for t in range(n)] if phases != "second" else []
        return cps, mine

    def send_own(ins, outs, sems):
        cps, mine = plan(ins, outs, sems)
        for t in range(n):
            mine[t].start()
            for k in (0, 1, 2):
                cps[t][k].start()

    def pass_on(ins, outs, sems):
        cps, _ = plan(ins, outs, sems)
        for t in range(n):
            if phases == "all":
                cps[t][1].wait_recv()
            cps[t][3].start()
            cps[t][5].start()
        for t in range(n):
            if phases == "all":
                cps[t][2].wait_recv()
            cps[t][4].start()
            cps[t][6].start()

    def own_landed(ins, outs, sems):
        cps, mine = plan(ins, outs, sems)
        for t in range(n):
            for k in (0, 1, 2):
                cps[t][k].wait()
            mine[t].wait()

    def all_landed(ins, outs, sems):
        cps, mine = plan(ins, outs, sems)
        for t in range(n):
            cps[t][3].wait_recv()
            cps[t][4].wait_recv()
            cps[t][7].start()
        for t in range(n):
            for k in ((0, 5, 6, 7) if phases == "all" else (5, 6, 7)):
                cps[t][k].wait_recv()
            for k in (range(N_GATHER) if phases == "all" else range(3, N_GATHER)):
                cps[t][k].wait_send()
            if phases == "all":
                mine[t].wait()

    sems = [pltpu.SemaphoreType.DMA((N_GATHER * n,)), pltpu.SemaphoreType.DMA((N_GATHER * n,))]
    if phases != "second":
        sems.append(pltpu.SemaphoreType.DMA((n,)))
    if phases == "all":
        return _Comm(shards, full, sems, send_own, all_landed, middle=pass_on)
    if phases == "first":
        return _Comm(shards, full, sems, send_own, own_landed)
    return _Comm(shards, full, sems, pass_on, all_landed, aliases={t: t for t in range(n)})


def _pair_comm(grads, axes, sizes):
    n = len(grads)
    outs_sds = [_sds((4,) + g.shape[:a] + (s,) + g.shape[a + 1:], g.dtype) for g, a, s in zip(grads, axes, sizes)]

    def copies(ins, outs, sems):
        send_sems, recv_sems = sems
        x, y, c = _place()
        return [pltpu.make_async_remote_copy(
            src_ref=_piece(ins[t], axes[t], sizes[t], 2 * qi + (1 - c)), dst_ref=outs[t].at[qi],
            send_sem=send_sems.at[4 * t + qi], recv_sem=recv_sems.at[4 * t + qi],
            device_id=(x, y, 1 - c), device_id_type=MESH) for t in range(n) for qi in range(4)]

    def start(ins, outs, sems):
        for cp in copies(ins, outs, sems):
            cp.start()

    def finish(ins, outs, sems):
        for cp in copies(ins, outs, sems):
            cp.wait()

    sems = [pltpu.SemaphoreType.DMA((4 * n,)), pltpu.SemaphoreType.DMA((4 * n,))]
    return _Comm(grads, outs_sds, sems, start, finish)


def _chip_comm(sums):
    n = len(sums)
    outs_sds = [_sds((3,) + s.shape[1:], s.dtype) for s in sums]

    def copies(ins, outs, sems):
        send_sems, recv_sems = sems
        x, y, c = _place()
        return [pltpu.make_async_remote_copy(
            src_ref=ins[t].at[2 * qx + qy], dst_ref=outs[t].at[j],
            send_sem=send_sems.at[3 * t + j], recv_sem=recv_sems.at[3 * t + j],
            device_id=(qx, qy, c), device_id_type=MESH)
            for t in range(n) for j, (qx, qy) in enumerate([(1 - x, y), (x, 1 - y), (1 - x, 1 - y)])]

    def start(ins, outs, sems):
        for cp in copies(ins, outs, sems):
            cp.start()

    def finish(ins, outs, sems):
        for cp in copies(ins, outs, sems):
            cp.wait()

    sems = [pltpu.SemaphoreType.DMA((3 * n,)), pltpu.SemaphoreType.DMA((3 * n,))]
    return _Comm(sums, outs_sds, sems, start, finish)


def _small_comm(small):
    def copies(ins, outs, sems):
        send_sems, recv_sems, local_sem = sems
        x, y, c = _place()
        mine = outs[0].at[4 * x + 2 * y + c]
        out = [pltpu.make_async_copy(ins[0], mine, local_sem.at[0])]
        for k in range(1, N_DEV):
            peer = (1 - x if k & 4 else x, 1 - y if k & 2 else y, 1 - c if k & 1 else c)
            out.append(pltpu.make_async_remote_copy(
                src_ref=ins[0], dst_ref=mine, send_sem=send_sems.at[k - 1], recv_sem=recv_sems.at[k - 1],
                device_id=peer, device_id_type=MESH))
        return out

    def start(ins, outs, sems):
        for cp in copies(ins, outs, sems):
            cp.start()

    def finish(ins, outs, sems):
        for cp in copies(ins, outs, sems):
            cp.wait()

    sems = [pltpu.SemaphoreType.DMA((N_DEV - 1,)), pltpu.SemaphoreType.DMA((N_DEV - 1,)), pltpu.SemaphoreType.DMA((1,))]
    return _Comm([small], [_sds((N_DEV,) + small.shape, small.dtype)], sems, start, finish)


def _small_scatter_comm(send):
    def copies(ins, outs, sems):
        send_sems, recv_sems, local_sem = sems
        x, y, c = _place()
        me = 4 * x + 2 * y + c
        out = [pltpu.make_async_copy(ins[0].at[me], outs[0].at[me], local_sem.at[0])]
        for k in range(1, N_DEV):
            px, py, pc = (1 - x if k & 4 else x, 1 - y if k & 2 else y, 1 - c if k & 1 else c)
            out.append(pltpu.make_async_remote_copy(
                src_ref=ins[0].at[4 * px + 2 * py + pc], dst_ref=outs[0].at[me], send_sem=send_sems.at[k - 1],
                recv_sem=recv_sems.at[k - 1], device_id=(px, py, pc), device_id_type=MESH))
        return out

    def start(ins, outs, sems):
        for cp in copies(ins, outs, sems):
            cp.start()

    def finish(ins, outs, sems):
        for cp in copies(ins, outs, sems):
            cp.wait()

    sems = [pltpu.SemaphoreType.DMA((N_DEV - 1,)), pltpu.SemaphoreType.DMA((N_DEV - 1,)), pltpu.SemaphoreType.DMA((1,))]
    return _Comm([send], [_sds(send.shape, send.dtype)], sems, start, finish)


def _pair_sum(c_idx, grad, recv, axis, size, split, *, name):
    nd = len(grad.shape)
    piece = grad.shape[:axis] + (size,) + grad.shape[axis + 1:]
    blk = (piece[0] // split,) + piece[1:]

    def g_map(q, r, c_ref):
        idx = [0] * nd
        idx[axis] = 2 * q + c_ref[0]
        idx[0] = idx[0] * split + r if axis == 0 else r
        return tuple(idx)

    def r_map(q, r, c_ref):
        return (q, r) + (0,) * (nd - 1)

    def body(c_ref, g_ref, r_ref, o_ref):
        o_ref[0] = (g_ref[...].astype(F32) + r_ref[0].astype(F32)).astype(BF16)

    return _call(
        body, grid=(4, split), prefetch=c_idx,
        in_specs=[pl.BlockSpec(blk, g_map), pl.BlockSpec((1,) + blk, r_map)],
        out_specs=[pl.BlockSpec((1,) + blk, r_map)], out_shape=[_sds((4,) + piece, BF16)],
        operands=(grad, recv), name=name, params=_params(("arbitrary", "arbitrary"), 32))[0][0]


def _adam_math(w, g, m, v):
    m = ADAM_B1 * m + (1.0 - ADAM_B1) * g
    v = ADAM_B2 * v + (1.0 - ADAM_B2) * (g * g)
    m_hat = m / (1.0 - ADAM_B1 ** ADAM_STEP)
    v_hat = v / (1.0 - ADAM_B2 ** ADAM_STEP)
    delta = -ADAM_LR * (m_hat / (jnp.sqrt(v_hat) + ADAM_EPS) + ADAM_WD * w)
    return delta, m, v


def _adam_big(q_idx, sums, recv, w, m, v, split, *, name, comm=None):
    shape = w.shape
    nd = len(shape)
    blk = (shape[0] // split,) + shape[1:]
    w_map = lambda r, q_ref: (r,) + (0,) * (nd - 1)
    s_map = lambda r, q_ref: (q_ref[0], r) + (0,) * (nd - 1)
    r_map = lambda r, q_ref: (0, r) + (0,) * (nd - 1)

    def body(q_ref, s_ref, r_ref, w_ref, m_ref, v_ref, g_ref, d_ref, nm_ref, nv_ref):
        g = s_ref[0].astype(F32) + r_ref[0].astype(F32) + r_ref[1].astype(F32) + r_ref[2].astype(F32)
        g_ref[...] = g
        d_ref[...], nm_ref[...], nv_ref[...] = _adam_math(w_ref[...], g, m_ref[...], v_ref[...])

    wspec = pl.BlockSpec(blk, w_map)
    return _call(
        body, grid=(split,), prefetch=q_idx,
        in_specs=[pl.BlockSpec((1,) + blk, s_map), pl.BlockSpec((3,) + blk, r_map), wspec, wspec, wspec],
        out_specs=[wspec] * 4, out_shape=[_sds(shape, F32)] * 4, operands=(sums, recv, w, m, v),
        name=name, params=_params(("arbitrary",), 32), comm=comm)


def _adam_small(parts, w, m, v, *, name):
    R = w.shape[0]

    def body(p_ref, w_ref, m_ref, v_ref, g_ref, d_ref, nm_ref, nv_ref):
        g = p_ref[0]
        for d in range(1, N_DEV):
            g = g + p_ref[d]
        g_ref[...] = g
        d_ref[...], nm_ref[...], nv_ref[...] = _adam_math(w_ref[...], g, m_ref[...], v_ref[...])

    whole = _const((R, LANES))
    return _call(
        body, grid=(1,), in_specs=[_const((N_DEV, R, LANES)), whole, whole, whole], out_specs=[whole] * 4,
        out_shape=[_sds((R, LANES), F32)] * 4, operands=(parts, w, m, v), name=name,
        params=_params(("arbitrary",), 32))[0]


def _pack(arrs):
    return jnp.concatenate([a.reshape(-1) for a in arrs]).reshape(-1, LANES)


def _unpack(packed, shapes):
    flat = packed.reshape(-1)
    out, off = [], 0
    for s in shapes:
        n = 1
        for d in s:
            n *= d
        out.append(flat[off:off + n].reshape(s))
        off += n
    return out


BIG = ("e_in", "e_out", "o_in", "o_cw", "o_out")
BIG_AXIS = dict(e_in=1, e_out=0, o_in=1, o_cw=1, o_out=0)
BIG_SPLIT = dict(e_in=8, e_out=4, o_in=4, o_cw=4, o_out=4)
REPLICATED = ("e_norm_pre", "e_norm_post", "e_b_conv_bias", "e_b_ln_g", "e_b_ln_b")
SHARDED = ("e_a_conv", "e_b_conv", "o_norm_pre", "o_norm_post", "o_c_b", "o_c_scale")
SMALL = REPLICATED + SHARDED


class _Exchange:
    def __init__(self, shards, small, order, c_idx):
        self.shards = shards
        self.small = small
        self.order = order
        self.c_idx = c_idx
        self.reduced = {}

    def gather(self, keys):
        return _gather_comm([self.shards[k] for k in keys], [BIG_AXIS[k] for k in keys], "all")

    def gather1(self, keys):
        return _gather_comm([self.shards[k] for k in keys], [BIG_AXIS[k] for k in keys], "first")

    def gather2(self, keys, firsts):
        return _gather_comm(firsts, [BIG_AXIS[k] for k in keys], "second")

    def pair(self, grads):
        keys = list(grads)
        return _pair_comm([grads[k] for k in keys], [BIG_AXIS[k] for k in keys],
                          [grads[k].shape[BIG_AXIS[k]] // N_DEV for k in keys])

    def pair_sums(self, grads, received):
        return {k: _pair_sum(self.c_idx, grads[k], r, BIG_AXIS[k], grads[k].shape[BIG_AXIS[k]] // N_DEV,
                             BIG_SPLIT[k], name="pair_sum_" + k) for k, r in zip(grads, received)}

    def chips(self, sums):
        return _chip_comm([sums[k] for k in sums])

    def done(self, sums, received):
        self.reduced.update({k: (sums[k], r) for k, r in zip(sums, received)})


def _local_step(x, tgt, w_small, ex):
    S, D = x.shape
    tnt, tx = min(TM_NT, S), min(TM_MIX, S)

    h0, got = _rms_norm(x, w_small["e_norm_pre"], tm=tx, name="e_norm", comm=_small_comm(ex.small))
    per_dev = [_unpack(got[0][d], [w_small[k].shape for k in SHARDED]) for d in range(N_DEV)]
    sm = {k: w_small[k] for k in REPLICATED}
    for j, k in enumerate(SHARDED):
        sm[k] = jnp.concatenate([per_dev[d][j] for d in range(N_DEV)], axis=-1)
    n_groups = sm["o_c_b"].shape[0]
    sm["o_c_b"] = sm["o_c_b"].reshape(1, -1)

    wt = {}
    p, wt["e_in"], _ = _gather_matmul(ex.order, h0, ex.shards["e_in"], tm=tnt, name="e_in_fwd")
    W = p.shape[1] // 7
    (u, cb), got = _e_mix_fwd(p, sm["e_a_conv"], sm["e_b_conv"], sm["e_b_conv_bias"], sm["e_b_ln_g"],
                              sm["e_b_ln_b"], tm=tx, name="e_mix_fwd", comm=ex.gather(["e_out"]))
    wt["e_out"] = got[0]
    late = ["o_out", "o_cw"]
    (x1, y0), part = _out_norm_res(u, wt["e_out"], x, sm["e_norm_post"], tm=tx, name="e_out_fwd",
                                   comm=ex.gather1(late))
    h1, _ = _rms_norm(x1, sm["o_norm_pre"], tm=tx, name="o_norm")
    q, wt["o_in"], got = _gather_matmul(ex.order, h1, ex.shards["o_in"], tm=tnt, name="o_in_fwd",
                                        comm=ex.gather2(late, part))
    wt.update(zip(late, got))
    yy, pooled, gg = _o_mix_fwd(q, wt["o_cw"], sm["o_c_b"], sm["o_c_scale"], tm=tx, name="o_mix_fwd")
    dout, dx2, dyy, lcol, dg_o_post = _out_loss(yy, wt["o_out"], x1, sm["o_norm_post"], tgt, tm=tx, name="o_out_loss")
    loss = (0.5 / D) * jnp.sum(lcol)

    dq, d_cw, d_cb, d_cscale = _o_mix_bwd(dyy, q, gg, pooled, wt["o_cw"], sm["o_c_scale"], tm=tx, name="o_mix_bwd")
    g_o_out, _ = _mm_tn(yy, dout, ts=tnt, tn=W, name="o_out_dw")
    ga = dict(o_out=g_o_out, o_cw=d_cw.astype(BF16))
    dh1, ra = _mm_nt(dq, wt["o_in"], tm=tnt, tk=W, name="o_in_bwd", comm=ex.pair(ga))
    sa = ex.pair_sums(ga, ra)
    (dx1, dy0, dg_o_pre, dg_e_post), ra = _pre_bwd_o(dh1, x1, dx2, y0, sm["o_norm_pre"], sm["e_norm_post"],
                                                     tm=tx, name="o_pre_bwd", comm=ex.chips(sa))
    ex.done(sa, ra)
    g_o_in, _ = _mm_tn(h1, dq, ts=tnt, tn=W, name="o_in_dw")
    gb = dict(o_in=g_o_in)
    du, rb = _mm_nt(dy0, wt["e_out"], tm=tnt, tk=W, name="e_out_bwd", comm=ex.pair(gb))
    sb = ex.pair_sums(gb, rb)
    g_e_out, _ = _mm_tn(u, dy0, ts=tnt, tn=W, name="e_out_dw")
    gc = dict(e_out=g_e_out)
    (dp, d_wa, d_wb, d_bias, d_lg, d_lb), rbc = _e_mix_bwd(
        du, p, cb, sm["e_a_conv"], sm["e_b_conv"], sm["e_b_ln_g"], sm["e_b_ln_b"], tm=tx, name="e_mix_bwd",
        comm=_merge(ex.chips(sb), ex.pair(gc)))
    ex.done(sb, rbc[:1])
    sc = ex.pair_sums(gc, rbc[1:])
    g_e_in, rc = _mm_tn(h0, dp, ts=tnt, tn=W, name="e_in_dw", comm=ex.chips(sc))
    ex.done(sc, rc)
    gd = dict(e_in=g_e_in)
    sd = ex.pair_sums(gd, _run_comm(ex.pair(gd), "pair_e_in"))
    dh0, rd = _mm_nt(dp, wt["e_in"], tm=tnt, tk=W, name="e_in_bwd", comm=ex.chips(sd))
    ex.done(sd, rd)
    grad_x, dg_e_pre = _pre_bwd_e(dh0, x, dx1, sm["e_norm_pre"], tm=tx, name="e_pre_bwd")

    small = dict(e_norm_pre=dg_e_pre, e_norm_post=dg_e_post, e_a_conv=d_wa, e_b_conv=d_wb, e_b_conv_bias=d_bias,
                 e_b_ln_g=d_lg, e_b_ln_b=d_lb, o_norm_pre=dg_o_pre, o_norm_post=dg_o_post,
                 o_c_b=d_cb.reshape(n_groups, -1), o_c_scale=d_cscale)
    return loss, grad_x, small


def kernel(x, e_norm_pre, e_norm_post, e_w_in, e_a_conv, e_b_conv, e_b_conv_bias, e_b_ln_g, e_b_ln_b, e_w_out, o_norm_pre, o_norm_post, o_w_in, o_c_w, o_c_b, o_c_scale, o_w_out, loss_target, m_e_norm_pre, m_e_norm_post, m_e_w_in, m_e_a_conv, m_e_b_conv, m_e_b_conv_bias, m_e_b_ln_g, m_e_b_ln_b, m_e_w_out, m_o_norm_pre, m_o_norm_post, m_o_w_in, m_o_c_w, m_o_c_b, m_o_c_scale, m_o_w_out, v_e_norm_pre, v_e_norm_post, v_e_w_in, v_e_a_conv, v_e_b_conv, v_e_b_conv_bias, v_e_b_ln_g, v_e_b_ln_b, v_e_w_out, v_o_norm_pre, v_o_norm_post, v_o_w_in, v_o_c_w, v_o_c_b, v_o_c_scale, v_o_w_out):
    xi, yi, ci = _place()
    w_big = dict(e_in=e_w_in[0], e_out=e_w_out[0], o_in=o_w_in[0], o_cw=o_c_w[0], o_out=o_w_out[0])
    m_big = dict(e_in=m_e_w_in[0], e_out=m_e_w_out[0], o_in=m_o_w_in[0], o_cw=m_o_c_w[0], o_out=m_o_w_out[0])
    v_big = dict(e_in=v_e_w_in[0], e_out=v_e_w_out[0], o_in=v_o_w_in[0], o_cw=v_o_c_w[0], o_out=v_o_w_out[0])
    w_small = dict(e_norm_pre=e_norm_pre, e_norm_post=e_norm_post, e_b_conv_bias=e_b_conv_bias, e_b_ln_g=e_b_ln_g,
                   e_b_ln_b=e_b_ln_b, e_a_conv=e_a_conv[0], e_b_conv=e_b_conv[0], o_norm_pre=o_norm_pre,
                   o_norm_post=o_norm_post, o_c_b=o_c_b[0], o_c_scale=o_c_scale)
    m_small = dict(e_norm_pre=m_e_norm_pre, e_norm_post=m_e_norm_post, e_b_conv_bias=m_e_b_conv_bias,
                   e_b_ln_g=m_e_b_ln_g, e_b_ln_b=m_e_b_ln_b, e_a_conv=m_e_a_conv[0], e_b_conv=m_e_b_conv[0],
                   o_norm_pre=m_o_norm_pre, o_norm_post=m_o_norm_post, o_c_b=m_o_c_b[0], o_c_scale=m_o_c_scale)
    v_small = dict(e_norm_pre=v_e_norm_pre, e_norm_post=v_e_norm_post, e_b_conv_bias=v_e_b_conv_bias,
                   e_b_ln_g=v_e_b_ln_g, e_b_ln_b=v_e_b_ln_b, e_a_conv=v_e_a_conv[0], e_b_conv=v_e_b_conv[0],
                   o_norm_pre=v_o_norm_pre, o_norm_post=v_o_norm_post, o_c_b=v_o_c_b[0], o_c_scale=v_o_c_scale)

    c_idx = jnp.reshape(ci, (1,)).astype(jnp.int32)
    order = jnp.stack([2 * xi + yi, 2 * (1 - xi) + yi, 2 * xi + (1 - yi), 2 * (1 - xi) + (1 - yi)]).astype(jnp.int32)
    ex = _Exchange({k: w_big[k].astype(BF16) for k in BIG}, _pack([w_small[k] for k in SHARDED]), order, c_idx)
    loss, grad_x, g_small = _local_step(x[0], loss_target[0], w_small, ex)

    q_idx = jnp.reshape(2 * xi + yi, (1,)).astype(jnp.int32)
    big_out = {k: _adam_big(q_idx, *ex.reduced[k], w_big[k], m_big[k], v_big[k], BIG_SPLIT[k], name="adam_" + k)[0]
               for k in BIG}

    rep = _pack([g_small[k] for k in REPLICATED])
    loss_row = jnp.pad(jnp.reshape(loss, (1, 1)), ((0, 0), (0, LANES - 1)))
    blocks = []
    for k in SHARDED:
        r, n = w_small[k].shape
        blocks.append(g_small[k].reshape(r, N_DEV, n).transpose(1, 0, 2).reshape(N_DEV, r * n))
    blocks = jnp.concatenate(blocks, axis=1).reshape(N_DEV, -1, LANES)
    head = jnp.concatenate([rep, loss_row], axis=0)
    send = jnp.concatenate([jnp.broadcast_to(head[None], (N_DEV,) + head.shape), blocks], axis=1)
    parts = _run_comm(_small_scatter_comm(send), "small_grad_exchange")[0]

    def own_rows(d):
        return jnp.concatenate([_pack([d[k] for k in REPLICATED]), jnp.ones((1, LANES), F32),
                                _pack([d[k] for k in SHARDED])], axis=0)

    res_small = _adam_small(parts, own_rows(w_small), own_rows(m_small), own_rows(v_small), name="adam_small")
    n_rep = rep.shape[0]
    loss = res_small[0][n_rep, 0]
    small_out = {k: [] for k in SMALL}
    for packed in res_small:
        for k, t in zip(REPLICATED, _unpack(packed[:n_rep], [w_small[k].shape for k in REPLICATED])):
            small_out[k].append(t)
        for k, t in zip(SHARDED, _unpack(packed[n_rep + 1:], [w_small[k].shape for k in SHARDED])):
            small_out[k].append(t)

    big_of = dict(e_w_in="e_in", e_w_out="e_out", o_w_in="o_in", o_c_w="o_cw", o_w_out="o_out")
    stacked = ("e_a_conv", "e_b_conv", "o_c_b")

    def leaf(name, which):
        if name in big_of:
            return big_out[big_of[name]][which][None]
        t = small_out[name][which]
        return t[None] if name in stacked else t

    order = ("e_norm_pre", "e_norm_post", "e_w_in", "e_a_conv", "e_b_conv", "e_b_conv_bias", "e_b_ln_g", "e_b_ln_b",
             "e_w_out", "o_norm_pre", "o_norm_post", "o_w_in", "o_c_w", "o_c_b", "o_c_scale", "o_w_out")
    outs = [loss, grad_x[None]]
    for which in range(4):
        outs += [leaf(nm, which) for nm in order]
    return tuple(outs)
```

```python
import jax
import jax.numpy as jnp
from jax import lax
from jax.experimental import pallas as pl
from jax.experimental.pallas import tpu as pltpu

F32 = jnp.float32
BF16 = jnp.bfloat16
EPS = 1e-6
MESH = pl.DeviceIdType.MESH
ANY = pl.BlockSpec(memory_space=pl.ANY)

N_DEV = 8
HALO = 32
PHALO = 16
CONV_A = 3
CONV_B = 31
POOL_WINDOWS = (2, 4, 8, 16)
LANES = 128
MIB = 1024 * 1024

ADAM_LR = 0.001
ADAM_B1 = 0.9
ADAM_B2 = 0.999
ADAM_EPS = 1e-08
ADAM_WD = 0.01
ADAM_STEP = 10

TM_NT = 1024
TM_MIX = 256


def _sds(shape, dtype):
    return jax.ShapeDtypeStruct(tuple(shape), dtype)


def _params(sem, vmem_mib):
    return pltpu.CompilerParams(dimension_semantics=sem, vmem_limit_bytes=vmem_mib * MIB)


def _const(shape, single=False):
    n = len(shape)
    if single:
        return pl.BlockSpec(shape, lambda *_: (0,) * n, pipeline_mode=pl.Buffered(1))
    return pl.BlockSpec(shape, lambda *_: (0,) * n)


def _sig(v):
    return jax.nn.sigmoid(v)


def _dsilu(v, s):
    return s * (1.0 + v * (1.0 - s))


def _rms(v):
    return lax.rsqrt(jnp.mean(v * v, axis=-1, keepdims=True) + EPS)


def _norm_bwd(dn, n, r):
    return r * (dn - n * jnp.mean(dn * n, axis=-1, keepdims=True))


def _colsum(v):
    return jnp.sum(v, axis=0, keepdims=True)


class _Comm:
    def __init__(self, inputs, out_shapes, sems, start, finish, aliases=None, middle=None):
        self.inputs, self.out_shapes, self.sems = list(inputs), list(out_shapes), list(sems)
        self.start, self.finish, self.middle = start, finish, middle
        self.aliases = dict(aliases or {})


def _merge(*comms):
    comms = [c for c in comms if c is not None]
    if len(comms) <= 1:
        return comms[0] if comms else None
    spans, i0, o0, s0, aliases = [], 0, 0, 0, {}
    for c in comms:
        spans.append((i0, o0, s0))
        aliases.update({i0 + k: o0 + v for k, v in c.aliases.items()})
        i0, o0, s0 = i0 + len(c.inputs), o0 + len(c.out_shapes), s0 + len(c.sems)

    def run(which):
        def fn(ins, outs, sems):
            for c, (i, o, s) in zip(comms, spans):
                hook = getattr(c, which)
                if hook is not None:
                    hook(ins[i:i + len(c.inputs)], outs[o:o + len(c.out_shapes)], sems[s:s + len(c.sems)])
        return fn

    return _Comm([a for c in comms for a in c.inputs], [a for c in comms for a in c.out_shapes],
                 [a for c in comms for a in c.sems], run("start"), run("finish"), aliases,
                 run("middle") if any(c.middle is not None for c in comms) else None)


def _call(body, *, grid, in_specs, out_specs, out_shape, operands, name, params, scratch_shapes=(), comm=None,
          prefetch=None, own_copies_first=False):
    n_p = 0 if prefetch is None else 1
    n_i, n_o, n_s = len(in_specs), len(out_specs), len(scratch_shapes)
    if comm is None:
        comm = _Comm([], [], [], None, None)
    c_i, c_o = len(comm.inputs), len(comm.out_shapes)

    def carrier(*refs):
        pre, refs = refs[:n_p], refs[n_p:]
        ins, cins = refs[:n_i], refs[n_i:n_i + c_i]
        outs = refs[n_i + c_i:n_i + c_i + n_o]
        couts = refs[n_i + c_i + n_o:n_i + c_i + n_o + c_o]
        scr = refs[n_i + c_i + n_o + c_o:n_i + c_i + n_o + c_o + n_s]
        csems = refs[n_i + c_i + n_o + c_o + n_s:]
        ids = [pl.program_id(d) for d in range(len(grid))]
        first = ids[0] == 0
        half = ids[0] == grid[0] // 2
        last = ids[0] == grid[0] - 1
        for d in range(1, len(grid)):
            first = first & (ids[d] == 0)
            half = half & (ids[d] == 0)
            last = last & (ids[d] == grid[d] - 1)

        def start():
            if comm.start is not None:
                @pl.when(first)
                def _():
                    comm.start(cins, couts, csems)

        if not own_copies_first:
            start()
        if comm.middle is not None:
            assert grid[0] >= 2

            @pl.when(half)
            def _():
                comm.middle(cins, couts, csems)

        body(*pre, *ins, *outs, *scr)
        if own_copies_first:
            start()

        if comm.finish is not None:
            @pl.when(last)
            def _():
                comm.finish(cins, couts, csems)

    specs = dict(grid=grid, in_specs=list(in_specs) + [ANY] * c_i, out_specs=list(out_specs) + [ANY] * c_o,
                 scratch_shapes=list(scratch_shapes) + comm.sems)
    if n_p:
        specs = dict(grid_spec=pltpu.PrefetchScalarGridSpec(num_scalar_prefetch=1, **specs))
    res = pl.pallas_call(
        carrier, out_shape=list(out_shape) + comm.out_shapes,
        input_output_aliases={n_p + n_i + k: n_o + v for k, v in comm.aliases.items()},
        name=name, compiler_params=params, **specs)(*(() if prefetch is None else (prefetch,)), *operands, *comm.inputs)
    return list(res[:n_o]), list(res[n_o:])


def _run_comm(comm, name):
    c_i, c_o = len(comm.inputs), len(comm.out_shapes)

    def body(*refs):
        ins, outs, sems = refs[:c_i], refs[c_i:c_i + c_o], refs[c_i + c_o:]
        comm.start(ins, outs, sems)
        comm.finish(ins, outs, sems)

    res = pl.pallas_call(
        body, in_specs=[ANY] * c_i, out_specs=[ANY] * c_o, out_shape=comm.out_shapes, scratch_shapes=comm.sems,
        input_output_aliases=comm.aliases, name=name)(*comm.inputs)
    return list(res)


def _rms_norm(x, g, *, tm, name, comm=None):
    S, D = x.shape

    def body(x_ref, g_ref, h_ref):
        xx = x_ref[...]
        h_ref[...] = ((xx * _rms(xx)) * g_ref[...]).astype(BF16)

    row = pl.BlockSpec((tm, D), lambda i: (i, 0))
    outs, extra = _call(body, grid=(S // tm,), in_specs=[row, _const((1, D))], out_specs=[row],
                        out_shape=[_sds((S, D), BF16)], operands=(x, g), name=name,
                        params=_params(("arbitrary",), 32), comm=comm)
    return outs[0], extra


def _gather_matmul(order, h, shard, *, tm, name, comm=None):
    S, K = h.shape
    nb = shard.shape[1]
    n_i = S // tm

    def body(order_ref, h_ref, shard_ref, p_ref, full_ref, wbuf, send_sems, recv_sems, dma_sems):
        j, i = pl.program_id(0), pl.program_id(1)
        x, y, c = _place()
        cps = _gather_copies(shard_ref, full_ref, 1, nb, send_sems, recv_sems, 0)

        def load(qx, qy):
            cp = pltpu.make_async_copy(_piece(full_ref, 1, 2 * nb, 2 * qx + qy), wbuf, dma_sems.at[1])
            cp.start()
            cp.wait()

        @pl.when((j == 0) & (i == 0))
        def _():
            own = pltpu.make_async_copy(shard_ref, _piece(full_ref, 1, nb, 4 * x + 2 * y + c), dma_sems.at[0])
            own.start()
            for k in (0, 1, 2):
                cps[k].start()
            own.wait()
            cps[0].wait_recv()
            load(x, y)

        @pl.when((j == 1) & (i == 0))
        def _():
            cps[1].wait_recv()
            cps[3].start()
            cps[5].start()
            cps[2].wait_recv()
            cps[4].start()
            cps[6].start()
            cps[5].wait_recv()
            load(1 - x, y)

        @pl.when((j == 2) & (i == 0))
        def _():
            cps[6].wait_recv()
            load(x, 1 - y)

        @pl.when((j == 3) & (i == 0))
        def _():
            cps[3].wait_recv()
            cps[4].wait_recv()
            cps[7].start()
            cps[7].wait_recv()
            load(1 - x, 1 - y)

        p_ref[...] = jnp.dot(h_ref[...], wbuf[...], preferred_element_type=F32).astype(BF16)

        @pl.when((j == 3) & (i == n_i - 1))
        def _():
            for cp in cps:
                cp.wait_send()

    outs, extra = _call(
        body, grid=(4, n_i), prefetch=order,
        in_specs=[pl.BlockSpec((tm, K), lambda j, i, o: (i, 0)), ANY],
        out_specs=[pl.BlockSpec((tm, 2 * nb), lambda j, i, o: (i, o[j])), ANY],
        out_shape=[_sds((S, N_DEV * nb), BF16), _sds((K, N_DEV * nb), BF16)], operands=(h, shard),
        scratch_shapes=[pltpu.VMEM((K, 2 * nb), BF16), pltpu.SemaphoreType.DMA((N_GATHER,)),
                        pltpu.SemaphoreType.DMA((N_GATHER,)), pltpu.SemaphoreType.DMA((2,))],
        name=name, params=_params(("arbitrary", "arbitrary"), 48), comm=comm, own_copies_first=True)
    return outs[0], outs[1], extra


def _out_norm_res(u, w, x, g, *, tm, name, comm=None):
    S, K = u.shape
    D = w.shape[1]

    def body(u_ref, w_ref, x_ref, g_ref, x1_ref, y_ref):
        y = jnp.dot(u_ref[...], w_ref[...], preferred_element_type=F32)
        y_ref[...] = y.astype(BF16)
        x1_ref[...] = x_ref[...] + (y * _rms(y)) * g_ref[...]

    return _call(
        body, grid=(S // tm,),
        in_specs=[pl.BlockSpec((tm, K), lambda i: (i, 0)), _const((K, D), single=True),
                  pl.BlockSpec((tm, D), lambda i: (i, 0)), _const((1, D))],
        out_specs=[pl.BlockSpec((tm, D), lambda i: (i, 0)), pl.BlockSpec((tm, D), lambda i: (i, 0))],
        out_shape=[_sds((S, D), F32), _sds((S, D), BF16)], operands=(u, w, x, g),
        name=name, params=_params(("arbitrary",), 48), comm=comm)


def _out_loss(yy, w, x1, g, tgt, *, tm, name):
    S, K = yy.shape
    D = w.shape[1]

    def body(yy_ref, w_ref, x1_ref, g_ref, t_ref, dout_ref, dx2_ref, dyy_ref, lcol_ref, dg_ref):
        out = jnp.dot(yy_ref[...], w_ref[...], preferred_element_type=F32)
        r = _rms(out)
        n = out * r
        gg = g_ref[...]
        e = x1_ref[...] + n * gg - t_ref[...]
        dx2 = e * (1.0 / D)
        dx2_ref[...] = dx2
        dout = _norm_bwd(dx2 * gg, n, r).astype(BF16)
        dout_ref[...] = dout
        dyy_ref[...] = lax.dot_general(dout, w_ref[...], (((1,), (1,)), ((), ())),
                                       preferred_element_type=F32).astype(BF16)

        @pl.when(pl.program_id(0) == 0)
        def _():
            lcol_ref[...] = jnp.zeros_like(lcol_ref)
            dg_ref[...] = jnp.zeros_like(dg_ref)

        lcol_ref[...] += _colsum(e * e)
        dg_ref[...] += _colsum(dx2 * n)

    return _call(
        body, grid=(S // tm,),
        in_specs=[pl.BlockSpec((tm, K), lambda i: (i, 0)), _const((K, D), single=True),
                  pl.BlockSpec((tm, D), lambda i: (i, 0)), _const((1, D)),
                  pl.BlockSpec((tm, D), lambda i: (i, 0))],
        out_specs=[pl.BlockSpec((tm, D), lambda i: (i, 0)), pl.BlockSpec((tm, D), lambda i: (i, 0)),
                   pl.BlockSpec((tm, K), lambda i: (i, 0)), _const((1, D)), _const((1, D))],
        out_shape=[_sds((S, D), BF16), _sds((S, D), F32), _sds((S, K), BF16), _sds((1, D), F32), _sds((1, D), F32)],
        operands=(yy, w, x1, g, tgt), name=name, params=_params(("arbitrary",), 52))[0]


def _mm_nt(a, w, *, tm, tk, name, comm=None):
    S, N = a.shape
    D = w.shape[0]
    n_k = N // tk

    def body(a_ref, w_ref, o_ref, acc_ref):
        k = pl.program_id(1)

        @pl.when(k == 0)
        def _():
            acc_ref[...] = jnp.zeros_like(acc_ref)

        acc_ref[...] = lax.dot_general(a_ref[...], w_ref[...], (((1,), (1,)), ((), ())),
                                       preferred_element_type=F32) + acc_ref[...]

        @pl.when(k == n_k - 1)
        def _():
            o_ref[...] = acc_ref[...].astype(BF16)

    outs, extra = _call(
        body, grid=(S // tm, n_k),
        in_specs=[pl.BlockSpec((tm, tk), lambda i, k: (i, k)), pl.BlockSpec((D, tk), lambda i, k: (0, k))],
        out_specs=[pl.BlockSpec((tm, D), lambda i, k: (i, 0))],
        out_shape=[_sds((S, D), BF16)], operands=(a, w),
        scratch_shapes=[pltpu.VMEM((tm, D), F32)],
        name=name, params=_params(("arbitrary", "arbitrary"), 48), comm=comm)
    return outs[0], extra


def _mm_tn(a, b, *, ts, tn, name, comm=None):
    S, M = a.shape
    N = b.shape[1]
    n_s = S // ts

    def body(a_ref, b_ref, o_ref, acc_ref):
        s = pl.program_id(1)

        @pl.when(s == 0)
        def _():
            acc_ref[...] = jnp.zeros_like(acc_ref)

        acc_ref[...] = lax.dot_general(a_ref[...], b_ref[...], (((0,), (0,)), ((), ())),
                                       preferred_element_type=F32) + acc_ref[...]

        @pl.when(s == n_s - 1)
        def _():
            o_ref[...] = acc_ref[...].astype(BF16)

    outs, extra = _call(
        body, grid=(N // tn, n_s),
        in_specs=[pl.BlockSpec((ts, M), lambda j, s: (s, 0)), pl.BlockSpec((ts, tn), lambda j, s: (s, j))],
        out_specs=[pl.BlockSpec((M, tn), lambda j, s: (0, j))],
        out_shape=[_sds((M, N), BF16)], operands=(a, b),
        scratch_shapes=[pltpu.VMEM((M, tn), F32)],
        name=name, params=_params(("arbitrary", "arbitrary"), 48), comm=comm)
    return outs[0], extra


def _pre_bwd_o(dh, x1, dx2, y0, g_pre, g_post, *, tm, name, comm=None):
    S, D = x1.shape

    def body(dh_ref, x1_ref, dx2_ref, y0_ref, gpre_ref, gpost_ref, dx1_ref, dy0_ref, dgpre_ref, dgpost_ref):
        @pl.when(pl.program_id(0) == 0)
        def _():
            dgpre_ref[...] = jnp.zeros_like(dgpre_ref)
            dgpost_ref[...] = jnp.zeros_like(dgpost_ref)

        dh = dh_ref[...].astype(F32)
        x1 = x1_ref[...]
        r2 = _rms(x1)
        xn = x1 * r2
        dgpre_ref[...] += _colsum(dh * xn)
        dx1 = dx2_ref[...] + _norm_bwd(dh * gpre_ref[...], xn, r2)
        dx1_ref[...] = dx1
        y = y0_ref[...].astype(F32)
        r1 = _rms(y)
        n1 = y * r1
        dgpost_ref[...] += _colsum(dx1 * n1)
        dy0_ref[...] = _norm_bwd(dx1 * gpost_ref[...], n1, r1).astype(BF16)

    row = pl.BlockSpec((tm, D), lambda i: (i, 0))
    return _call(
        body, grid=(S // tm,),
        in_specs=[row, row, row, row, _const((1, D)), _const((1, D))],
        out_specs=[row, row, _const((1, D)), _const((1, D))],
        out_shape=[_sds((S, D), F32), _sds((S, D), BF16), _sds((1, D), F32), _sds((1, D), F32)],
        operands=(dh, x1, dx2, y0, g_pre, g_post),
        name=name, params=_params(("arbitrary",), 48), comm=comm)


def _pre_bwd_e(dh, x, dx1, g_pre, *, tm, name):
    S, D = x.shape

    def body(dh_ref, x_ref, dx1_ref, gpre_ref, gx_ref, dgpre_ref):
        @pl.when(pl.program_id(0) == 0)
        def _():
            dgpre_ref[...] = jnp.zeros_like(dgpre_ref)

        dh = dh_ref[...].astype(F32)
        xx = x_ref[...]
        r0 = _rms(xx)
        xn = xx * r0
        dgpre_ref[...] += _colsum(dh * xn)
        gx_ref[...] = dx1_ref[...] + _norm_bwd(dh * gpre_ref[...], xn, r0)

    row = pl.BlockSpec((tm, D), lambda i: (i, 0))
    return _call(
        body, grid=(S // tm,),
        in_specs=[row, row, row, _const((1, D))],
        out_specs=[row, _const((1, D))],
        out_shape=[_sds((S, D), F32), _sds((1, D), F32)],
        operands=(dh, x, dx1, g_pre), name=name, params=_params(("arbitrary",), 48))[0]


SUBLANES = 8


def _shift_copies(sh_ref, ext_ref, cs):
    for b in range(1, SUBLANES):
        sh_ref[b - 1] = ext_ref[pl.ds(b, sh_ref.shape[1]), cs]


def _rows_at(ext_ref, sh_ref, off, cs, tm):
    b = off % SUBLANES
    if b == 0 or sh_ref is None:
        return ext_ref[pl.ds(off, tm), cs]
    return sh_ref[b - 1, pl.ds(off - b, tm), :]


def _taps(ext_ref, w_ref, n_taps, base, cs, tm, sh_ref=None):
    acc = _rows_at(ext_ref, sh_ref, base, cs, tm) * w_ref[0:1, cs]
    for k in range(1, n_taps):
        acc = acc + _rows_at(ext_ref, sh_ref, base + k, cs, tm) * w_ref[k:k + 1, cs]
    return acc


def _taps_rev(ext_ref, w_ref, n_taps, cs, tm, sh_ref=None):
    acc = _rows_at(ext_ref, sh_ref, n_taps - 1, cs, tm) * w_ref[0:1, cs]
    for k in range(1, n_taps):
        acc = acc + _rows_at(ext_ref, sh_ref, n_taps - 1 - k, cs, tm) * w_ref[k:k + 1, cs]
    return acc


def _e_mix_fwd(p, wa, wb, bias, ln_g, ln_b, *, tm, name, comm=None):
    S = p.shape[0]
    W = p.shape[1] // 7
    nb = tm // HALO
    chunks = [slice(c * LANES, (c + 1) * LANES) for c in range(W // LANES)]

    def body(p_ref, hax_ref, hac_ref, hbv_ref, hbg_ref, wa_ref, wb_ref, bias_ref, lg_ref, lb_ref,
             u_ref, cb_ref, ext_ref, sh_ref):
        keep = (pl.program_id(0) > 0).astype(F32)
        col = lambda j, cs: p_ref[:, j * W + cs.start:j * W + cs.stop].astype(F32)

        ext_ref[0:HALO, :] = hax_ref[...].astype(F32) * hac_ref[...].astype(F32) * keep
        ext_ref[HALO:, :] = p_ref[:, 2 * W:3 * W].astype(F32) * p_ref[:, 0:W].astype(F32)
        for cs in chunks:
            conv = _taps(ext_ref, wa_ref, CONV_A, HALO - (CONV_A - 1), cs, tm)
            az = col(3, cs)
            u_ref[:, cs] = (col(1, cs) * conv * (az * _sig(az))).astype(BF16)

        ext_ref[0:HALO, :] = hbv_ref[...].astype(F32) * _sig(hbg_ref[...].astype(F32)) * keep
        ext_ref[HALO:, :] = p_ref[:, 4 * W:5 * W].astype(F32) * _sig(p_ref[:, 5 * W:6 * W].astype(F32))
        s1 = jnp.zeros((tm, LANES), F32)
        for cs in chunks:
            _shift_copies(sh_ref, ext_ref, cs)
            cb = _taps(ext_ref, wb_ref, CONV_B, HALO - (CONV_B - 1), cs, tm, sh_ref) + bias_ref[:, cs]
            cb_ref[:, cs] = cb
            s1 = s1 + cb
        mu = jnp.sum(s1, axis=-1, keepdims=True) * (1.0 / W)
        s2 = jnp.zeros((tm, LANES), F32)
        for cs in chunks:
            xc = cb_ref[:, cs] - mu
            s2 = s2 + xc * xc
        rs = lax.rsqrt(jnp.sum(s2, axis=-1, keepdims=True) * (1.0 / W) + EPS)
        for cs in chunks:
            lb = (cb_ref[:, cs] - mu) * rs * lg_ref[:, cs] + lb_ref[:, cs]
            bz = col(6, cs)
            u_ref[:, W + cs.start:W + cs.stop] = (lb * _sig(lb) * (bz * _sig(bz))).astype(BF16)

    prev = lambda j: pl.BlockSpec((HALO, W), lambda i: (jnp.maximum(i * nb - 1, 0), j))
    return _call(
        body, grid=(S // tm,),
        in_specs=[pl.BlockSpec((tm, 7 * W), lambda i: (i, 0)), prev(0), prev(2), prev(4), prev(5),
                  _const((CONV_A, W)), _const((CONV_B, W)), _const((1, W)), _const((1, W)), _const((1, W))],
        out_specs=[pl.BlockSpec((tm, 2 * W), lambda i: (i, 0)), pl.BlockSpec((tm, W), lambda i: (i, 0))],
        out_shape=[_sds((S, 2 * W), BF16), _sds((S, W), F32)],
        operands=(p, p, p, p, p, wa, wb, bias, ln_g, ln_b),
        scratch_shapes=[pltpu.VMEM((HALO + tm, W), F32),
                        pltpu.VMEM((SUBLANES - 1, HALO + tm - SUBLANES, LANES), F32)],
        name=name, params=_params(("arbitrary",), 48), comm=comm)


def _e_mix_bwd(du, p, cb, wa, wb, ln_g, ln_b, *, tm, name, comm=None):
    S = p.shape[0]
    W = p.shape[1] // 7
    nb = tm // HALO
    n_t = S // tm
    last_blk = S // HALO - 1
    chunks = [slice(c * LANES, (c + 1) * LANES) for c in range(W // LANES)]

    def body(du_ref, duf_ref, p_ref, fab_ref, faz_ref, fbz_ref, hax_ref, hac_ref, hbv_ref, hbg_ref,
             cb_ref, cbf_ref, wa_ref, wb_ref, lg_ref, lb_ref,
             dp_ref, dwa_ref, dwb_ref, dbias_ref, dlg_ref, dlb_ref, extd_ref, extg_ref, shd_ref, shg_ref):
        i = pl.program_id(0)
        keep_prev = (i > 0).astype(F32)
        keep_next = (i < n_t - 1).astype(F32)
        col = lambda j, cs: p_ref[:, j * W + cs.start:j * W + cs.stop].astype(F32)

        @pl.when(i == 0)
        def _():
            dwa_ref[...] = jnp.zeros_like(dwa_ref)
            dwb_ref[...] = jnp.zeros_like(dwb_ref)
            dbias_ref[...] = jnp.zeros_like(dbias_ref)
            dlg_ref[...] = jnp.zeros_like(dlg_ref)
            dlb_ref[...] = jnp.zeros_like(dlb_ref)

        def dcb_rows(rows, cb_rows_ref, dub, bz_of, dst0, scale, main):
            cbv = cb_rows_ref[...]
            mu = jnp.mean(cbv, axis=-1, keepdims=True)
            xc = cbv - mu
            rs = lax.rsqrt(jnp.mean(xc * xc, axis=-1, keepdims=True) + EPS)
            m1 = jnp.zeros((rows, LANES), F32)
            m2 = jnp.zeros((rows, LANES), F32)
            for cs in chunks:
                nbv = (cb_rows_ref[:, cs] - mu) * rs
                lb = nbv * lg_ref[:, cs] + lb_ref[:, cs]
                sl = _sig(lb)
                bz = bz_of(cs)
                sz = _sig(bz)
                dub_c = dub(cs)
                dlb = dub_c * (bz * sz) * _dsilu(lb, sl)
                if main:
                    dlg_ref[:, cs] += _colsum(dlb * nbv)
                    dlb_ref[:, cs] += _colsum(dlb)
                    dp_ref[:, 6 * W + cs.start:6 * W + cs.stop] = (dub_c * (lb * sl) * _dsilu(bz, sz)).astype(BF16)
                dnb = dlb * lg_ref[:, cs]
                extd_ref[dst0:dst0 + rows, cs] = dnb
                m1 = m1 + dnb
                m2 = m2 + dnb * nbv
            m1 = jnp.sum(m1, axis=-1, keepdims=True) * (1.0 / W)
            m2 = jnp.sum(m2, axis=-1, keepdims=True) * (1.0 / W)
            for cs in chunks:
                nbv = (cb_rows_ref[:, cs] - mu) * rs
                dcb = rs * (extd_ref[dst0:dst0 + rows, cs] - m1 - nbv * m2) * scale
                extd_ref[dst0:dst0 + rows, cs] = dcb
                if main:
                    dbias_ref[:, cs] += _colsum(dcb)

        dcb_rows(tm, cb_ref, lambda cs: du_ref[:, W + cs.start:W + cs.stop].astype(F32),
                 lambda cs: col(6, cs), 0, 1.0, True)
        dcb_rows(HALO, cbf_ref, lambda cs: duf_ref[:, W + cs.start:W + cs.stop].astype(F32),
                 lambda cs: fbz_ref[:, cs].astype(F32), tm, keep_next, False)

        extg_ref[0:HALO, :] = hbv_ref[...].astype(F32) * _sig(hbg_ref[...].astype(F32)) * keep_prev
        extg_ref[HALO:, :] = p_ref[:, 4 * W:5 * W].astype(F32) * _sig(p_ref[:, 5 * W:6 * W].astype(F32))
        base_b = HALO - (CONV_B - 1)
        for cs in chunks:
            _shift_copies(shd_ref, extd_ref, cs)
            _shift_copies(shg_ref, extg_ref, cs)
            dgb = _taps_rev(extd_ref, wb_ref, CONV_B, cs, tm, shd_ref)
            bv = col(4, cs)
            sg = _sig(col(5, cs))
            dp_ref[:, 4 * W + cs.start:4 * W + cs.stop] = (dgb * sg).astype(BF16)
            dp_ref[:, 5 * W + cs.start:5 * W + cs.stop] = (dgb * bv * sg * (1.0 - sg)).astype(BF16)
            dcb = extd_ref[0:tm, cs]
            for k in range(CONV_B):
                dwb_ref[k:k + 1, cs] += _colsum(dcb * _rows_at(extg_ref, shg_ref, base_b + k, cs, tm))

        extg_ref[0:HALO, :] = hax_ref[...].astype(F32) * hac_ref[...].astype(F32) * keep_prev
        extg_ref[HALO:, :] = p_ref[:, 2 * W:3 * W].astype(F32) * p_ref[:, 0:W].astype(F32)
        base_a = HALO - (CONV_A - 1)
        for cs in chunks:
            conv = _taps(extg_ref, wa_ref, CONV_A, base_a, cs, tm)
            az = col(3, cs)
            sz = _sig(az)
            ab = col(1, cs)
            dua = du_ref[:, cs].astype(F32)
            dya = dua * (az * sz)
            dp_ref[:, W + cs.start:W + cs.stop] = (dya * conv).astype(BF16)
            dp_ref[:, 3 * W + cs.start:3 * W + cs.stop] = (dua * (ab * conv) * _dsilu(az, sz)).astype(BF16)
            extd_ref[0:tm, cs] = dya * ab
            azf = faz_ref[:, cs].astype(F32)
            extd_ref[tm:tm + HALO, cs] = (duf_ref[:, cs].astype(F32) * (azf * _sig(azf))
                                          * fab_ref[:, cs].astype(F32) * keep_next)
        for cs in chunks:
            dca = _taps_rev(extd_ref, wa_ref, CONV_A, cs, tm)
            dp_ref[:, cs] = (dca * col(2, cs)).astype(BF16)
            dp_ref[:, 2 * W + cs.start:2 * W + cs.stop] = (dca * col(0, cs)).astype(BF16)
            dconv = extd_ref[0:tm, cs]
            for k in range(CONV_A):
                dwa_ref[k:k + 1, cs] += _colsum(dconv * extg_ref[pl.ds(base_a + k, tm), cs])

    prev = lambda j: pl.BlockSpec((HALO, W), lambda i: (jnp.maximum(i * nb - 1, 0), j))
    nxt = lambda j, w: pl.BlockSpec((HALO, w), lambda i: (jnp.minimum((i + 1) * nb, last_blk), j))
    row = lambda w: pl.BlockSpec((tm, w), lambda i: (i, 0))
    return _call(
        body, grid=(n_t,),
        in_specs=[row(2 * W), nxt(0, 2 * W), row(7 * W), nxt(1, W), nxt(3, W), nxt(6, W),
                  prev(0), prev(2), prev(4), prev(5), row(W), nxt(0, W),
                  _const((CONV_A, W)), _const((CONV_B, W)), _const((1, W)), _const((1, W))],
        out_specs=[row(7 * W), _const((CONV_A, W)), _const((CONV_B, W)), _const((1, W)), _const((1, W)), _const((1, W))],
        out_shape=[_sds((S, 7 * W), BF16), _sds((CONV_A, W), F32), _sds((CONV_B, W), F32),
                   _sds((1, W), F32), _sds((1, W), F32), _sds((1, W), F32)],
        operands=(du, du, p, p, p, p, p, p, p, p, cb, cb, wa, wb, ln_g, ln_b),
        scratch_shapes=[pltpu.VMEM((tm + HALO, W), F32), pltpu.VMEM((HALO + tm, W), F32),
                        pltpu.VMEM((SUBLANES - 1, HALO + tm - SUBLANES, LANES), F32),
                        pltpu.VMEM((SUBLANES - 1, HALO + tm - SUBLANES, LANES), F32)],
        name=name, params=_params(("arbitrary",), 52), comm=comm)


def _counts(i, tm, rows, off, win):
    t = i * tm + off + lax.broadcasted_iota(jnp.int32, (rows, 1), 0)
    return jnp.minimum(t + 1, win).astype(F32)


def _o_mix_fwd(q, cw, cb, cscale, *, tm, name):
    S = q.shape[0]
    WC = q.shape[1] // 2
    NG = len(POOL_WINDOWS)
    G = WC // NG
    nb = tm // PHALO

    def body(v_ref, z_ref, hv_ref, cw_ref, cb_ref, sc_ref, yy_ref, pooled_ref, gg_ref, ext_ref):
        i = pl.program_id(0)
        keep = (i > 0).astype(F32)
        for g, win in enumerate(POOL_WINDOWS):
            cs = slice(g * G, (g + 1) * G)
            v = v_ref[:, cs].astype(F32)
            ext_ref[0:PHALO, :] = hv_ref[:, cs].astype(F32) * keep
            ext_ref[PHALO:, :] = v
            s = v
            for j in range(1, win):
                s = s + ext_ref[pl.ds(PHALO - j, tm), :]
            pooled = (s / _counts(i, tm, tm, 0, win) - v).astype(BF16)
            pooled_ref[:, cs] = pooled
            gg = jnp.dot(pooled, cw_ref[g], preferred_element_type=F32) + cb_ref[:, cs]
            gg_ref[:, cs] = gg.astype(BF16)
            z = z_ref[:, cs].astype(F32)
            yy_ref[:, cs] = (gg * sc_ref[:, cs] * (z * _sig(z))).astype(BF16)

    row = lambda j: pl.BlockSpec((tm, WC), lambda i: (i, j))
    out = pl.BlockSpec((tm, WC), lambda i: (i, 0))
    return _call(
        body, grid=(S // tm,),
        in_specs=[row(0), row(1), pl.BlockSpec((PHALO, WC), lambda i: (jnp.maximum(i * nb - 1, 0), 0)),
                  _const((NG, G, G)), _const((1, WC)), _const((1, WC))],
        out_specs=[out, out, out],
        out_shape=[_sds((S, WC), BF16)] * 3, operands=(q, q, q, cw, cb, cscale),
        scratch_shapes=[pltpu.VMEM((PHALO + tm, G), F32)],
        name=name, params=_params(("arbitrary",), 40))[0]


def _o_mix_bwd(dyy, q, gg, pooled, cw, cscale, *, tm, name):
    S = q.shape[0]
    WC = q.shape[1] // 2
    NG = len(POOL_WINDOWS)
    G = WC // NG
    nb = tm // PHALO
    n_t = S // tm
    last_blk = S // PHALO - 1
    nt = (((1,), (1,)), ((), ()))
    tn = (((0,), (0,)), ((), ()))

    def body(dyy_ref, dyyf_ref, z_ref, zf_ref, gg_ref, pooled_ref, cw_ref, sc_ref,
             dq_ref, dcw_ref, dcb_ref, dsc_ref, ext_ref):
        i = pl.program_id(0)
        keep_next = (i < n_t - 1).astype(F32)

        @pl.when(i == 0)
        def _():
            dcw_ref[...] = jnp.zeros_like(dcw_ref)
            dcb_ref[...] = jnp.zeros_like(dcb_ref)
            dsc_ref[...] = jnp.zeros_like(dsc_ref)

        for g, win in enumerate(POOL_WINDOWS):
            cs = slice(g * G, (g + 1) * G)
            sc = sc_ref[:, cs]
            z = z_ref[:, cs].astype(F32)
            sz = _sig(z)
            dyy_c = dyy_ref[:, cs].astype(F32)
            ggv = gg_ref[:, cs].astype(F32)
            dyy0 = dyy_c * (z * sz)
            dq_ref[:, WC + cs.start:WC + cs.stop] = (dyy_c * (ggv * sc) * _dsilu(z, sz)).astype(BF16)
            dgg = dyy0 * sc
            dsc_ref[:, cs] += _colsum(dyy0 * ggv)
            dcb_ref[:, cs] += _colsum(dgg)
            dgg_b = dgg.astype(BF16)
            dcw_ref[g] += lax.dot_general(pooled_ref[:, cs], dgg_b, tn, preferred_element_type=F32)
            dpool = lax.dot_general(dgg_b, cw_ref[g], nt, preferred_element_type=F32)
            zf = zf_ref[:, cs].astype(F32)
            dgg_f = (dyyf_ref[:, cs].astype(F32) * (zf * _sig(zf)) * sc * keep_next).astype(BF16)
            dpool_f = lax.dot_general(dgg_f, cw_ref[g], nt, preferred_element_type=F32)
            ext_ref[0:tm, :] = dpool / _counts(i, tm, tm, 0, win)
            ext_ref[tm:tm + PHALO, :] = dpool_f / _counts(i, tm, PHALO, tm, win)
            dv = ext_ref[0:tm, :] - dpool
            for j in range(1, win):
                dv = dv + ext_ref[pl.ds(j, tm), :]
            dq_ref[:, cs] = dv.astype(BF16)

    row = lambda: pl.BlockSpec((tm, WC), lambda i: (i, 0))
    nxt = lambda j: pl.BlockSpec((PHALO, WC), lambda i: (jnp.minimum((i + 1) * nb, last_blk), j))
    return _call(
        body, grid=(n_t,),
        in_specs=[row(), nxt(0), pl.BlockSpec((tm, WC), lambda i: (i, 1)), nxt(1), row(), row(),
                  _const((NG, G, G)), _const((1, WC))],
        out_specs=[pl.BlockSpec((tm, 2 * WC), lambda i: (i, 0)), _const((NG, G, G)), _const((1, WC)), _const((1, WC))],
        out_shape=[_sds((S, 2 * WC), BF16), _sds((NG, G, G), F32), _sds((1, WC), F32), _sds((1, WC), F32)],
        operands=(dyy, dyy, q, q, gg, pooled, cw, cscale),
        scratch_shapes=[pltpu.VMEM((tm + PHALO, G), F32)],
        name=name, params=_params(("arbitrary",), 48))[0]


def _place():
    return lax.axis_index("x"), lax.axis_index("y"), lax.axis_index("c")


def _piece(ref, axis, size, index):
    start = index * size
    if axis == len(ref.shape) - 1:
        start = pl.multiple_of(start, LANES)
    idx = [slice(None)] * len(ref.shape)
    idx[axis] = pl.ds(start, size)
    return ref.at[tuple(idx)]


def _gather_copies(src, out, axis, size, send_sems, recv_sems, base, held=None):
    x, y, c = _place()
    sib, xn, yn = (x, y, 1 - c), (1 - x, y, c), (x, 1 - y, c)

    def blk(px, py, of=out):
        return _piece(of, axis, size, 4 * px + 2 * py + c)

    def half(ref, h):
        n = ref.shape[0] // 2
        return ref.at[pl.ds(h * n, n)]

    def rc(k, s, d, to):
        return pltpu.make_async_remote_copy(src_ref=s, dst_ref=d, send_sem=send_sems.at[base + k],
                                            recv_sem=recv_sems.at[base + k], device_id=to, device_id_type=MESH)

    own, xb, yb, db = blk(x, y), blk(1 - x, y), blk(x, 1 - y), blk(1 - x, 1 - y)
    got = out if held is None else held
    xs, ys, ds = blk(1 - x, y, got), blk(x, 1 - y, got), blk(1 - x, 1 - y, got)
    return [rc(0, src, own, sib), rc(1, src, own, xn), rc(2, src, own, yn),
            rc(3, half(xs, 0), half(xb, 0), yn), rc(4, half(ys, 1), half(yb, 1), xn),
            rc(5, xs, xb, sib), rc(6, ys, yb, sib), rc(7, ds, db, sib)]


N_GATHER = 8


def _gather_comm(shards, axes, phases):
    n = len(shards)
    if phases == "second":
        sizes = [s.shape[a] // N_DEV for s, a in zip(shards, axes)]
        full = [_sds(s.shape, s.dtype) for s in shards]
    else:
        sizes = [s.shape[a] for s, a in zip(shards, axes)]
        full = [_sds(s.shape[:a] + (N_DEV * s.shape[a],) + s.shape[a + 1:], s.dtype) for s, a in zip(shards, axes)]

    def plan(ins, outs, sems):
        x, y, c = _place()
        me = 4 * x + 2 * y + c
        if phases == "second":
            cps = [_gather_copies(_piece(ins[t], axes[t], sizes[t], me), outs[t], axes[t], sizes[t], sems[0], sems[1],
                                  N_GATHER * t, ins[t]) for t in range(n)]
        else:
            cps = [_gather_copies(ins[t], outs[t], axes[t], sizes[t], sems[0], sems[1], N_GATHER * t)
                   for t in range(n)]
        mine = [pltpu.make_async_copy(ins[t], _piece(outs[t], axes[t], sizes[t], me), sems[2].at[t])
                for t in range(n)] if phases != "second" else []
        return cps, mine

    def send_own(ins, outs, sems):
        cps, mine = plan(ins, outs, sems)
        for t in range(n):
            mine[t].start()
            for k in (0, 1, 2):
                cps[t][k].start()

    def pass_on(ins, outs, sems):
        cps, _ = plan(ins, outs, sems)
        for t in range(n):
            if phases == "all":
                cps[t][1].wait_recv()
            cps[t][3].start()
            cps[t][5].start()
        for t in range(n):
            if phases == "all":
                cps[t][2].wait_recv()
            cps[t][4].start()
            cps[t][6].start()

    def own_landed(ins, outs, sems):
        cps, mine = plan(ins, outs, sems)
        for t in range(n):
            for k in (0, 1, 2):
                cps[t][k].wait()
            mine[t].wait()

    def all_landed(ins, outs, sems):
        cps, mine = plan(ins, outs, sems)
        for t in range(n):
            cps[t][3].wait_recv()
            cps[t][4].wait_recv()
            cps[t][7].start()
        for t in range(n):
            for k in ((0, 5, 6, 7) if phases == "all" else (5, 6, 7)):
                cps[t][k].wait_recv()
            for k in (range(N_GATHER) if phases == "all" else range(3, N_GATHER)):
                cps[t][k].wait_send()
            if phases == "all":
                mine[t].wait()

    sems = [pltpu.SemaphoreType.DMA((N_GATHER * n,)), pltpu.SemaphoreType.DMA((N_GATHER * n,))]
    if phases != "second":
        sems.append(pltpu.SemaphoreType.DMA((n,)))
    if phases == "all":
        return _Comm(shards, full, sems, send_own, all_landed, middle=pass_on)
    if phases == "first":
        return _Comm(shards, full, sems, send_own, own_landed)
    return _Comm(shards, full, sems, pass_on, all_landed, aliases={t: t for t in range(n)})


def _pair_comm(grads, axes, sizes):
    n = len(grads)
    outs_sds = [_sds((4,) + g.shape[:a] + (s,) + g.shape[a + 1:], g.dtype) for g, a, s in zip(grads, axes, sizes)]

    def copies(ins, outs, sems):
        send_sems, recv_sems = sems
        x, y, c = _place()
        return [pltpu.make_async_remote_copy(
            src_ref=_piece(ins[t], axes[t], sizes[t], 2 * qi + (1 - c)), dst_ref=outs[t].at[qi],
            send_sem=send_sems.at[4 * t + qi], recv_sem=recv_sems.at[4 * t + qi],
            device_id=(x, y, 1 - c), device_id_type=MESH) for t in range(n) for qi in range(4)]

    def start(ins, outs, sems):
        for cp in copies(ins, outs, sems):
            cp.start()

    def finish(ins, outs, sems):
        for cp in copies(ins, outs, sems):
            cp.wait()

    sems = [pltpu.SemaphoreType.DMA((4 * n,)), pltpu.SemaphoreType.DMA((4 * n,))]
    return _Comm(grads, outs_sds, sems, start, finish)


def _chip_comm(sums):
    n = len(sums)
    outs_sds = [_sds((3,) + s.shape[1:], s.dtype) for s in sums]

    def copies(ins, outs, sems):
        send_sems, recv_sems = sems
        x, y, c = _place()
        return [pltpu.make_async_remote_copy(
            src_ref=ins[t].at[2 * qx + qy], dst_ref=outs[t].at[j],
            send_sem=send_sems.at[3 * t + j], recv_sem=recv_sems.at[3 * t + j],
            device_id=(qx, qy, c), device_id_type=MESH)
            for t in range(n) for j, (qx, qy) in enumerate([(1 - x, y), (x, 1 - y), (1 - x, 1 - y)])]

    def start(ins, outs, sems):
        for cp in copies(ins, outs, sems):
            cp.start()

    def finish(ins, outs, sems):
        for cp in copies(ins, outs, sems):
            cp.wait()

    sems = [pltpu.SemaphoreType.DMA((3 * n,)), pltpu.SemaphoreType.DMA((3 * n,))]
    return _Comm(sums, outs_sds, sems, start, finish)


def _small_comm(small):
    def copies(ins, outs, sems):
        send_sems, recv_sems, local_sem = sems
        x, y, c = _place()
        mine = outs[0].at[4 * x + 2 * y + c]
        out = [pltpu.make_async_copy(ins[0], mine, local_sem.at[0])]
        for k in range(1, N_DEV):
            peer = (1 - x if k & 4 else x, 1 - y if k & 2 else y, 1 - c if k & 1 else c)
            out.append(pltpu.make_async_remote_copy(
                src_ref=ins[0], dst_ref=mine, send_sem=send_sems.at[k - 1], recv_sem=recv_sems.at[k - 1],
                device_id=peer, device_id_type=MESH))
        return out

    def start(ins, outs, sems):
        for cp in copies(ins, outs, sems):
            cp.start()

    def finish(ins, outs, sems):
        for cp in copies(ins, outs, sems):
            cp.wait()

    sems = [pltpu.SemaphoreType.DMA((N_DEV - 1,)), pltpu.SemaphoreType.DMA((N_DEV - 1,)), pltpu.SemaphoreType.DMA((1,))]
    return _Comm([small], [_sds((N_DEV,) + small.shape, small.dtype)], sems, start, finish)


def _small_scatter_comm(send):
    def copies(ins, outs, sems):
        send_sems, recv_sems, local_sem = sems
        x, y, c = _place()
        me = 4 * x + 2 * y + c
        out = [pltpu.make_async_copy(ins[0].at[me], outs[0].at[me], local_sem.at[0])]
        for k in range(1, N_DEV):
            px, py, pc = (1 - x if k & 4 else x, 1 - y if k & 2 else y, 1 - c if k & 1 else c)
            out.append(pltpu.make_async_remote_copy(
                src_ref=ins[0].at[4 * px + 2 * py + pc], dst_ref=outs[0].at[me], send_sem=send_sems.at[k - 1],
                recv_sem=recv_sems.at[k - 1], device_id=(px, py, pc), device_id_type=MESH))
        return out

    def start(ins, outs, sems):
        for cp in copies(ins, outs, sems):
            cp.start()

    def finish(ins, outs, sems):
        for cp in copies(ins, outs, sems):
            cp.wait()

    sems = [pltpu.SemaphoreType.DMA((N_DEV - 1,)), pltpu.SemaphoreType.DMA((N_DEV - 1,)), pltpu.SemaphoreType.DMA((1,))]
    return _Comm([send], [_sds(send.shape, send.dtype)], sems, start, finish)


def _pair_sum(c_idx, grad, recv, axis, size, split, *, name):
    nd = len(grad.shape)
    piece = grad.shape[:axis] + (size,) + grad.shape[axis + 1:]
    blk = (piece[0] // split,) + piece[1:]

    def g_map(q, r, c_ref):
        idx = [0] * nd
        idx[axis] = 2 * q + c_ref[0]
        idx[0] = idx[0] * split + r if axis == 0 else r
        return tuple(idx)

    def r_map(q, r, c_ref):
        return (q, r) + (0,) * (nd - 1)

    def body(c_ref, g_ref, r_ref, o_ref):
        o_ref[0] = (g_ref[...].astype(F32) + r_ref[0].astype(F32)).astype(BF16)

    return _call(
        body, grid=(4, split), prefetch=c_idx,
        in_specs=[pl.BlockSpec(blk, g_map), pl.BlockSpec((1,) + blk, r_map)],
        out_specs=[pl.BlockSpec((1,) + blk, r_map)], out_shape=[_sds((4,) + piece, BF16)],
        operands=(grad, recv), name=name, params=_params(("arbitrary", "arbitrary"), 32))[0][0]


def _adam_math(w, g, m, v):
    m = ADAM_B1 * m + (1.0 - ADAM_B1) * g
    v = ADAM_B2 * v + (1.0 - ADAM_B2) * (g * g)
    m_hat = m / (1.0 - ADAM_B1 ** ADAM_STEP)
    v_hat = v / (1.0 - ADAM_B2 ** ADAM_STEP)
    delta = -ADAM_LR * (m_hat / (jnp.sqrt(v_hat) + ADAM_EPS) + ADAM_WD * w)
    return delta, m, v


def _adam_big(q_idx, sums, recv, w, m, v, split, *, name, comm=None):
    shape = w.shape
    nd = len(shape)
    blk = (shape[0] // split,) + shape[1:]
    w_map = lambda r, q_ref: (r,) + (0,) * (nd - 1)
    s_map = lambda r, q_ref: (q_ref[0], r) + (0,) * (nd - 1)
    r_map = lambda r, q_ref: (0, r) + (0,) * (nd - 1)

    def body(q_ref, s_ref, r_ref, w_ref, m_ref, v_ref, g_ref, d_ref, nm_ref, nv_ref):
        g = s_ref[0].astype(F32) + r_ref[0].astype(F32) + r_ref[1].astype(F32) + r_ref[2].astype(F32)
        g_ref[...] = g
        d_ref[...], nm_ref[...], nv_ref[...] = _adam_math(w_ref[...], g, m_ref[...], v_ref[...])

    wspec = pl.BlockSpec(blk, w_map)
    return _call(
        body, grid=(split,), prefetch=q_idx,
        in_specs=[pl.BlockSpec((1,) + blk, s_map), pl.BlockSpec((3,) + blk, r_map), wspec, wspec, wspec],
        out_specs=[wspec] * 4, out_shape=[_sds(shape, F32)] * 4, operands=(sums, recv, w, m, v),
        name=name, params=_params(("arbitrary",), 32), comm=comm)


def _adam_small(parts, w, m, v, *, name):
    R = w.shape[0]

    def body(p_ref, w_ref, m_ref, v_ref, g_ref, d_ref, nm_ref, nv_ref):
        g = p_ref[0]
        for d in range(1, N_DEV):
            g = g + p_ref[d]
        g_ref[...] = g
        d_ref[...], nm_ref[...], nv_ref[...] = _adam_math(w_ref[...], g, m_ref[...], v_ref[...])

    whole = _const((R, LANES))
    return _call(
        body, grid=(1,), in_specs=[_const((N_DEV, R, LANES)), whole, whole, whole], out_specs=[whole] * 4,
        out_shape=[_sds((R, LANES), F32)] * 4, operands=(parts, w, m, v), name=name,
        params=_params(("arbitrary",), 32))[0]


def _pack(arrs):
    return jnp.concatenate([a.reshape(-1) for a in arrs]).reshape(-1, LANES)


def _unpack(packed, shapes):
    flat = packed.reshape(-1)
    out, off = [], 0
    for s in shapes:
        n = 1
        for d in s:
            n *= d
        out.append(flat[off:off + n].reshape(s))
        off += n
    return out


BIG = ("e_in", "e_out", "o_in", "o_cw", "o_out")
BIG_AXIS = dict(e_in=1, e_out=0, o_in=1, o_cw=1, o_out=0)
BIG_SPLIT = dict(e_in=8, e_out=4, o_in=4, o_cw=4, o_out=4)
REPLICATED = ("e_norm_pre", "e_norm_post", "e_b_conv_bias", "e_b_ln_g", "e_b_ln_b")
SHARDED = ("e_a_conv", "e_b_conv", "o_norm_pre", "o_norm_post", "o_c_b", "o_c_scale")
SMALL = REPLICATED + SHARDED


class _Exchange:
    def __init__(self, shards, small, order, c_idx):
        self.shards = shards
        self.small = small
        self.order = order
        self.c_idx = c_idx
        self.reduced = {}

    def gather(self, keys):
        return _gather_comm([self.shards[k] for k in keys], [BIG_AXIS[k] for k in keys], "all")

    def gather1(self, keys):
        return _gather_comm([self.shards[k] for k in keys], [BIG_AXIS[k] for k in keys], "first")

    def gather2(self, keys, firsts):
        return _gather_comm(firsts, [BIG_AXIS[k] for k in keys], "second")

    def pair(self, grads):
        keys = list(grads)
        return _pair_comm([grads[k] for k in keys], [BIG_AXIS[k] for k in keys],
                          [grads[k].shape[BIG_AXIS[k]] // N_DEV for k in keys])

    def pair_sums(self, grads, received):
        return {k: _pair_sum(self.c_idx, grads[k], r, BIG_AXIS[k], grads[k].shape[BIG_AXIS[k]] // N_DEV,
                             BIG_SPLIT[k], name="pair_sum_" + k) for k, r in zip(grads, received)}

    def chips(self, sums):
        return _chip_comm([sums[k] for k in sums])

    def done(self, sums, received):
        self.reduced.update({k: (sums[k], r) for k, r in zip(sums, received)})


def _local_step(x, tgt, w_small, ex):
    S, D = x.shape
    tnt, tx = min(TM_NT, S), min(TM_MIX, S)

    h0, got = _rms_norm(x, w_small["e_norm_pre"], tm=tx, name="e_norm", comm=_small_comm(ex.small))
    per_dev = [_unpack(got[0][d], [w_small[k].shape for k in SHARDED]) for d in range(N_DEV)]
    sm = {k: w_small[k] for k in REPLICATED}
    for j, k in enumerate(SHARDED):
        sm[k] = jnp.concatenate([per_dev[d][j] for d in range(N_DEV)], axis=-1)
    n_groups = sm["o_c_b"].shape[0]
    sm["o_c_b"] = sm["o_c_b"].reshape(1, -1)

    wt = {}
    p, wt["e_in"], _ = _gather_matmul(ex.order, h0, ex.shards["e_in"], tm=tnt, name="e_in_fwd")
    W = p.shape[1] // 7
    (u, cb), got = _e_mix_fwd(p, sm["e_a_conv"], sm["e_b_conv"], sm["e_b_conv_bias"], sm["e_b_ln_g"],
                              sm["e_b_ln_b"], tm=tx, name="e_mix_fwd", comm=ex.gather(["e_out"]))
    wt["e_out"] = got[0]
    late = ["o_out", "o_cw"]
    (x1, y0), part = _out_norm_res(u, wt["e_out"], x, sm["e_norm_post"], tm=tx, name="e_out_fwd",
                                   comm=ex.gather1(late))
    h1, _ = _rms_norm(x1, sm["o_norm_pre"], tm=tx, name="o_norm")
    q, wt["o_in"], got = _gather_matmul(ex.order, h1, ex.shards["o_in"], tm=tnt, name="o_in_fwd",
                                        comm=ex.gather2(late, part))
    wt.update(zip(late, got))
    yy, pooled, gg = _o_mix_fwd(q, wt["o_cw"], sm["o_c_b"], sm["o_c_scale"], tm=tx, name="o_mix_fwd")
    dout, dx2, dyy, lcol, dg_o_post = _out_loss(yy, wt["o_out"], x1, sm["o_norm_post"], tgt, tm=tx, name="o_out_loss")
    loss = (0.5 / D) * jnp.sum(lcol)

    dq, d_cw, d_cb, d_cscale = _o_mix_bwd(dyy, q, gg, pooled, wt["o_cw"], sm["o_c_scale"], tm=tx, name="o_mix_bwd")
    g_o_out, _ = _mm_tn(yy, dout, ts=tnt, tn=W, name="o_out_dw")
    ga = dict(o_out=g_o_out, o_cw=d_cw.astype(BF16))
    dh1, ra = _mm_nt(dq, wt["o_in"], tm=tnt, tk=W, name="o_in_bwd", comm=ex.pair(ga))
    sa = ex.pair_sums(ga, ra)
    (dx1, dy0, dg_o_pre, dg_e_post), ra = _pre_bwd_o(dh1, x1, dx2, y0, sm["o_norm_pre"], sm["e_norm_post"],
                                                     tm=tx, name="o_pre_bwd", comm=ex.chips(sa))
    ex.done(sa, ra)
    g_o_in, _ = _mm_tn(h1, dq, ts=tnt, tn=W, name="o_in_dw")
    gb = dict(o_in=g_o_in)
    du, rb = _mm_nt(dy0, wt["e_out"], tm=tnt, tk=W, name="e_out_bwd", comm=ex.pair(gb))
    sb = ex.pair_sums(gb, rb)
    g_e_out, _ = _mm_tn(u, dy0, ts=tnt, tn=W, name="e_out_dw")
    gc = dict(e_out=g_e_out)
    (dp, d_wa, d_wb, d_bias, d_lg, d_lb), rbc = _e_mix_bwd(
        du, p, cb, sm["e_a_conv"], sm["e_b_conv"], sm["e_b_ln_g"], sm["e_b_ln_b"], tm=tx, name="e_mix_bwd",
        comm=_merge(ex.chips(sb), ex.pair(gc)))
    ex.done(sb, rbc[:1])
    sc = ex.pair_sums(gc, rbc[1:])
    g_e_in, rc = _mm_tn(h0, dp, ts=tnt, tn=W, name="e_in_dw", comm=ex.chips(sc))
    ex.done(sc, rc)
    gd = dict(e_in=g_e_in)
    sd = ex.pair_sums(gd, _run_comm(ex.pair(gd), "pair_e_in"))
    dh0, rd = _mm_nt(dp, wt["e_in"], tm=tnt, tk=W, name="e_in_bwd", comm=ex.chips(sd))
    ex.done(sd, rd)
    grad_x, dg_e_pre = _pre_bwd_e(dh0, x, dx1, sm["e_norm_pre"], tm=tx, name="e_pre_bwd")

    small = dict(e_norm_pre=dg_e_pre, e_norm_post=dg_e_post, e_a_conv=d_wa, e_b_conv=d_wb, e_b_conv_bias=d_bias,
                 e_b_ln_g=d_lg, e_b_ln_b=d_lb, o_norm_pre=dg_o_pre, o_norm_post=dg_o_post,
                 o_c_b=d_cb.reshape(n_groups, -1), o_c_scale=d_cscale)
    return loss, grad_x, small


def kernel(x, e_norm_pre, e_norm_post, e_w_in, e_a_conv, e_b_conv, e_b_conv_bias, e_b_ln_g, e_b_ln_b, e_w_out, o_norm_pre, o_norm_post, o_w_in, o_c_w, o_c_b, o_c_scale, o_w_out, loss_target, m_e_norm_pre, m_e_norm_post, m_e_w_in, m_e_a_conv, m_e_b_conv, m_e_b_conv_bias, m_e_b_ln_g, m_e_b_ln_b, m_e_w_out, m_o_norm_pre, m_o_norm_post, m_o_w_in, m_o_c_w, m_o_c_b, m_o_c_scale, m_o_w_out, v_e_norm_pre, v_e_norm_post, v_e_w_in, v_e_a_conv, v_e_b_conv, v_e_b_conv_bias, v_e_b_ln_g, v_e_b_ln_b, v_e_w_out, v_o_norm_pre, v_o_norm_post, v_o_w_in, v_o_c_w, v_o_c_b, v_o_c_scale, v_o_w_out):
    xi, yi, ci = _place()
    w_big = dict(e_in=e_w_in[0], e_out=e_w_out[0], o_in=o_w_in[0], o_cw=o_c_w[0], o_out=o_w_out[0])
    m_big = dict(e_in=m_e_w_in[0], e_out=m_e_w_out[0], o_in=m_o_w_in[0], o_cw=m_o_c_w[0], o_out=m_o_w_out[0])
    v_big = dict(e_in=v_e_w_in[0], e_out=v_e_w_out[0], o_in=v_o_w_in[0], o_cw=v_o_c_w[0], o_out=v_o_w_out[0])
    w_small = dict(e_norm_pre=e_norm_pre, e_norm_post=e_norm_post, e_b_conv_bias=e_b_conv_bias, e_b_ln_g=e_b_ln_g,
                   e_b_ln_b=e_b_ln_b, e_a_conv=e_a_conv[0], e_b_conv=e_b_conv[0], o_norm_pre=o_norm_pre,
                   o_norm_post=o_norm_post, o_c_b=o_c_b[0], o_c_scale=o_c_scale)
    m_small = dict(e_norm_pre=m_e_norm_pre, e_norm_post=m_e_norm_post, e_b_conv_bias=m_e_b_conv_bias,
                   e_b_ln_g=m_e_b_ln_g, e_b_ln_b=m_e_b_ln_b, e_a_conv=m_e_a_conv[0], e_b_conv=m_e_b_conv[0],
                   o_norm_pre=m_o_norm_pre, o_norm_post=m_o_norm_post, o_c_b=m_o_c_b[0], o_c_scale=m_o_c_scale)
    v_small = dict(e_norm_pre=v_e_norm_pre, e_norm_post=v_e_norm_post, e_b_conv_bias=v_e_b_conv_bias,
                   e_b_ln_g=v_e_b_ln_g, e_b_ln_b=v_e_b_ln_b, e_a_conv=v_e_a_conv[0], e_b_conv=v_e_b_conv[0],
                   o_norm_pre=v_o_norm_pre, o_norm_post=v_o_norm_post, o_c_b=v_o_c_b[0], o_c_scale=v_o_c_scale)

    c_idx = jnp.reshape(ci, (1,)).astype(jnp.int32)
    order = jnp.stack([2 * xi + yi, 2 * (1 - xi) + yi, 2 * xi + (1 - yi), 2 * (1 - xi) + (1 - yi)]).astype(jnp.int32)
    ex = _Exchange({k: w_big[k].astype(BF16) for k in BIG}, _pack([w_small[k] for k in SHARDED]), order, c_idx)
    loss, grad_x, g_small = _local_step(x[0], loss_target[0], w_small, ex)

    q_idx = jnp.reshape(2 * xi + yi, (1,)).astype(jnp.int32)
    big_out = {k: _adam_big(q_idx, *ex.reduced[k], w_big[k], m_big[k], v_big[k], BIG_SPLIT[k], name="adam_" + k)[0]
               for k in BIG}

    rep = _pack([g_small[k] for k in REPLICATED])
    loss_row = jnp.pad(jnp.reshape(loss, (1, 1)), ((0, 0), (0, LANES - 1)))
    blocks = []
    for k in SHARDED:
        r, n = w_small[k].shape
        blocks.append(g_small[k].reshape(r, N_DEV, n).transpose(1, 0, 2).reshape(N_DEV, r * n))
    blocks = jnp.concatenate(blocks, axis=1).reshape(N_DEV, -1, LANES)
    head = jnp.concatenate([rep, loss_row], axis=0)
    send = jnp.concatenate([jnp.broadcast_to(head[None], (N_DEV,) + head.shape), blocks], axis=1)
    parts = _run_comm(_small_scatter_comm(send), "small_grad_exchange")[0]

    def own_rows(d):
        return jnp.concatenate([_pack([d[k] for k in REPLICATED]), jnp.ones((1, LANES), F32),
                                _pack([d[k] for k in SHARDED])], axis=0)

    res_small = _adam_small(parts, own_rows(w_small), own_rows(m_small), own_rows(v_small), name="adam_small")
    n_rep = rep.shape[0]
    loss = res_small[0][n_rep, 0]
    small_out = {k: [] for k in SMALL}
    for packed in res_small:
        for k, t in zip(REPLICATED, _unpack(packed[:n_rep], [w_small[k].shape for k in REPLICATED])):
            small_out[k].append(t)
        for k, t in zip(SHARDED, _unpack(packed[n_rep + 1:], [w_small[k].shape for k in SHARDED])):
            small_out[k].append(t)

    big_of = dict(e_w_in="e_in", e_w_out="e_out", o_w_in="o_in", o_c_w="o_cw", o_w_out="o_out")
    stacked = ("e_a_conv", "e_b_conv", "o_c_b")

    def leaf(name, which):
        if name in big_of:
            return big_out[big_of[name]][which][None]
        t = small_out[name][which]
        return t[None] if name in stacked else t

    order = ("e_norm_pre", "e_norm_post", "e_w_in", "e_a_conv", "e_b_conv", "e_b_conv_bias", "e_b_ln_g", "e_b_ln_b",
             "e_w_out", "o_norm_pre", "o_norm_post", "o_w_in", "o_c_w", "o_c_b", "o_c_scale", "o_w_out")
    outs = [loss, grad_x[None]]
    for which in range(4):
        outs += [leaf(nm, which) for nm in order]
    return tuple(outs)
```

```python
import jax
import jax.numpy as jnp
from jax import lax
from jax.experimental import pallas as pl
from jax.experimental.pallas import tpu as pltpu

F32 = jnp.float32
BF16 = jnp.bfloat16
EPS = 1e-6
MESH = pl.DeviceIdType.MESH
ANY = pl.BlockSpec(memory_space=pl.ANY)

N_DEV = 8
HALO = 32
PHALO = 16
CONV_A = 3
CONV_B = 31
POOL_WINDOWS = (2, 4, 8, 16)
LANES = 128
MIB = 1024 * 1024

ADAM_LR = 0.001
ADAM_B1 = 0.9
ADAM_B2 = 0.999
ADAM_EPS = 1e-08
ADAM_WD = 0.01
ADAM_STEP = 10

TM_NT = 1024
TM_MIX = 256
TM_WIDE = 512


def _sds(shape, dtype):
    return jax.ShapeDtypeStruct(tuple(shape), dtype)


def _params(sem, vmem_mib):
    return pltpu.CompilerParams(dimension_semantics=sem, vmem_limit_bytes=vmem_mib * MIB)


def _const(shape, single=False):
    n = len(shape)
    if single:
        return pl.BlockSpec(shape, lambda *_: (0,) * n, pipeline_mode=pl.Buffered(1))
    return pl.BlockSpec(shape, lambda *_: (0,) * n)


def _sig(v):
    return jax.nn.sigmoid(v)


def _dsilu(v, s):
    return s * (1.0 + v * (1.0 - s))


def _rms(v):
    return lax.rsqrt(jnp.mean(v * v, axis=-1, keepdims=True) + EPS)


def _norm_bwd(dn, n, r):
    return r * (dn - n * jnp.mean(dn * n, axis=-1, keepdims=True))


def _colsum(v):
    return jnp.sum(v, axis=0, keepdims=True)


class _Comm:
    def __init__(self, inputs, out_shapes, sems, start, finish, aliases=None, middle=None):
        self.inputs, self.out_shapes, self.sems = list(inputs), list(out_shapes), list(sems)
        self.start, self.finish, self.middle = start, finish, middle
        self.aliases = dict(aliases or {})


def _merge(*comms):
    comms = [c for c in comms if c is not None]
    if len(comms) <= 1:
        return comms[0] if comms else None
    spans, i0, o0, s0, aliases = [], 0, 0, 0, {}
    for c in comms:
        spans.append((i0, o0, s0))
        aliases.update({i0 + k: o0 + v for k, v in c.aliases.items()})
        i0, o0, s0 = i0 + len(c.inputs), o0 + len(c.out_shapes), s0 + len(c.sems)

    def run(which):
        def fn(ins, outs, sems):
            for c, (i, o, s) in zip(comms, spans):
                hook = getattr(c, which)
                if hook is not None:
                    hook(ins[i:i + len(c.inputs)], outs[o:o + len(c.out_shapes)], sems[s:s + len(c.sems)])
        return fn

    return _Comm([a for c in comms for a in c.inputs], [a for c in comms for a in c.out_shapes],
                 [a for c in comms for a in c.sems], run("start"), run("finish"), aliases,
                 run("middle") if any(c.middle is not None for c in comms) else None)


def _call(body, *, grid, in_specs, out_specs, out_shape, operands, name, params, scratch_shapes=(), comm=None,
          prefetch=None, own_copies_first=False):
    n_p = 0 if prefetch is None else 1
    n_i, n_o, n_s = len(in_specs), len(out_specs), len(scratch_shapes)
    if comm is None:
        comm = _Comm([], [], [], None, None)
    c_i, c_o = len(comm.inputs), len(comm.out_shapes)

    def carrier(*refs):
        pre, refs = refs[:n_p], refs[n_p:]
        ins, cins = refs[:n_i], refs[n_i:n_i + c_i]
        outs = refs[n_i + c_i:n_i + c_i + n_o]
        couts = refs[n_i + c_i + n_o:n_i + c_i + n_o + c_o]
        scr = refs[n_i + c_i + n_o + c_o:n_i + c_i + n_o + c_o + n_s]
        csems = refs[n_i + c_i + n_o + c_o + n_s:]
        ids = [pl.program_id(d) for d in range(len(grid))]
        first = ids[0] == 0
        half = ids[0] == grid[0] // 2
        last = ids[0] == grid[0] - 1
        for d in range(1, len(grid)):
            first = first & (ids[d] == 0)
            half = half & (ids[d] == 0)
            last = last & (ids[d] == grid[d] - 1)

        def start():
            if comm.start is not None:
                @pl.when(first)
                def _():
                    comm.start(cins, couts, csems)

        if not own_copies_first:
            start()
        if comm.middle is not None:
            assert grid[0] >= 2

            @pl.when(half)
            def _():
                comm.middle(cins, couts, csems)

        body(*pre, *ins, *outs, *scr)
        if own_copies_first:
            start()

        if comm.finish is not None:
            @pl.when(last)
            def _():
                comm.finish(cins, couts, csems)

    specs = dict(grid=grid, in_specs=list(in_specs) + [ANY] * c_i, out_specs=list(out_specs) + [ANY] * c_o,
                 scratch_shapes=list(scratch_shapes) + comm.sems)
    if n_p:
        specs = dict(grid_spec=pltpu.PrefetchScalarGridSpec(num_scalar_prefetch=1, **specs))
    res = pl.pallas_call(
        carrier, out_shape=list(out_shape) + comm.out_shapes,
        input_output_aliases={n_p + n_i + k: n_o + v for k, v in comm.aliases.items()},
        name=name, compiler_params=params, **specs)(*(() if prefetch is None else (prefetch,)), *operands, *comm.inputs)
    return list(res[:n_o]), list(res[n_o:])


def _run_comm(comm, name):
    c_i, c_o = len(comm.inputs), len(comm.out_shapes)

    def body(*refs):
        ins, outs, sems = refs[:c_i], refs[c_i:c_i + c_o], refs[c_i + c_o:]
        comm.start(ins, outs, sems)
        comm.finish(ins, outs, sems)

    res = pl.pallas_call(
        body, in_specs=[ANY] * c_i, out_specs=[ANY] * c_o, out_shape=comm.out_shapes, scratch_shapes=comm.sems,
        input_output_aliases=comm.aliases, name=name)(*comm.inputs)
    return list(res)


def _rms_norm(x, g, *, tm, name, comm=None):
    S, D = x.shape

    def body(x_ref, g_ref, h_ref):
        xx = x_ref[...]
        h_ref[...] = ((xx * _rms(xx)) * g_ref[...]).astype(BF16)

    row = pl.BlockSpec((tm, D), lambda i: (i, 0))
    outs, extra = _call(body, grid=(S // tm,), in_specs=[row, _const((1, D))], out_specs=[row],
                        out_shape=[_sds((S, D), BF16)], operands=(x, g), name=name,
                        params=_params(("arbitrary",), 32), comm=comm)
    return outs[0], extra


def _gather_matmul(order, h, shard, *, tm, name, comm=None):
    S, K = h.shape
    nb = shard.shape[1]
    n_i = S // tm

    def body(order_ref, h_ref, shard_ref, p_ref, full_ref, wbuf, send_sems, recv_sems, dma_sems):
        j, i = pl.program_id(0), pl.program_id(1)
        x, y, c = _place()
        cps = _gather_copies(shard_ref, full_ref, 1, nb, send_sems, recv_sems, 0)

        def load(qx, qy):
            cp = pltpu.make_async_copy(_piece(full_ref, 1, 2 * nb, 2 * qx + qy), wbuf, dma_sems.at[1])
            cp.start()
            cp.wait()

        @pl.when((j == 0) & (i == 0))
        def _():
            own = pltpu.make_async_copy(shard_ref, _piece(full_ref, 1, nb, 4 * x + 2 * y + c), dma_sems.at[0])
            own.start()
            for k in (0, 1, 2):
                cps[k].start()
            own.wait()
            cps[0].wait_recv()
            load(x, y)

        @pl.when((j == 1) & (i == 0))
        def _():
            cps[1].wait_recv()
            cps[3].start()
            cps[5].start()
            cps[2].wait_recv()
            cps[4].start()
            cps[6].start()
            cps[5].wait_recv()
            load(1 - x, y)

        @pl.when((j == 2) & (i == 0))
        def _():
            cps[6].wait_recv()
            load(x, 1 - y)

        @pl.when((j == 3) & (i == 0))
        def _():
            cps[3].wait_recv()
            cps[4].wait_recv()
            cps[7].start()
            cps[7].wait_recv()
            load(1 - x, 1 - y)

        p_ref[...] = jnp.dot(h_ref[...], wbuf[...], preferred_element_type=F32).astype(BF16)

        @pl.when((j == 3) & (i == n_i - 1))
        def _():
            for cp in cps:
                cp.wait_send()

    outs, extra = _call(
        body, grid=(4, n_i), prefetch=order,
        in_specs=[pl.BlockSpec((tm, K), lambda j, i, o: (i, 0)), ANY],
        out_specs=[pl.BlockSpec((tm, 2 * nb), lambda j, i, o: (i, o[j])), ANY],
        out_shape=[_sds((S, N_DEV * nb), BF16), _sds((K, N_DEV * nb), BF16)], operands=(h, shard),
        scratch_shapes=[pltpu.VMEM((K, 2 * nb), BF16), pltpu.SemaphoreType.DMA((N_GATHER,)),
                        pltpu.SemaphoreType.DMA((N_GATHER,)), pltpu.SemaphoreType.DMA((2,))],
        name=name, params=_params(("arbitrary", "arbitrary"), 48), comm=comm, own_copies_first=True)
    return outs[0], outs[1], extra


def _out_norm_res(u, w, x, g, *, tm, name, comm=None):
    S, K = u.shape
    D = w.shape[1]

    def body(u_ref, w_ref, x_ref, g_ref, x1_ref, y_ref):
        y = jnp.dot(u_ref[...], w_ref[...], preferred_element_type=F32)
        y_ref[...] = y.astype(BF16)
        x1_ref[...] = x_ref[...] + (y * _rms(y)) * g_ref[...]

    return _call(
        body, grid=(S // tm,),
        in_specs=[pl.BlockSpec((tm, K), lambda i: (i, 0)), _const((K, D), single=True),
                  pl.BlockSpec((tm, D), lambda i: (i, 0)), _const((1, D))],
        out_specs=[pl.BlockSpec((tm, D), lambda i: (i, 0)), pl.BlockSpec((tm, D), lambda i: (i, 0))],
        out_shape=[_sds((S, D), F32), _sds((S, D), BF16)], operands=(u, w, x, g),
        name=name, params=_params(("arbitrary",), 56), comm=comm)


def _out_loss(yy, w, x1, g, tgt, *, tm, name):
    S, K = yy.shape
    D = w.shape[1]

    def body(yy_ref, w_ref, x1_ref, g_ref, t_ref, dout_ref, dx2_ref, dyy_ref, lcol_ref, dg_ref):
        out = jnp.dot(yy_ref[...], w_ref[...], preferred_element_type=F32)
        r = _rms(out)
        n = out * r
        gg = g_ref[...]
        e = x1_ref[...] + n * gg - t_ref[...]
        dx2 = e * (1.0 / D)
        dx2_ref[...] = dx2
        dout = _norm_bwd(dx2 * gg, n, r).astype(BF16)
        dout_ref[...] = dout
        dyy_ref[...] = lax.dot_general(dout, w_ref[...], (((1,), (1,)), ((), ())),
                                       preferred_element_type=F32).astype(BF16)

        @pl.when(pl.program_id(0) == 0)
        def _():
            lcol_ref[...] = jnp.zeros_like(lcol_ref)
            dg_ref[...] = jnp.zeros_like(dg_ref)

        lcol_ref[...] += _colsum(e * e)
        dg_ref[...] += _colsum(dx2 * n)

    return _call(
        body, grid=(S // tm,),
        in_specs=[pl.BlockSpec((tm, K), lambda i: (i, 0)), _const((K, D), single=True),
                  pl.BlockSpec((tm, D), lambda i: (i, 0)), _const((1, D)),
                  pl.BlockSpec((tm, D), lambda i: (i, 0))],
        out_specs=[pl.BlockSpec((tm, D), lambda i: (i, 0)), pl.BlockSpec((tm, D), lambda i: (i, 0)),
                   pl.BlockSpec((tm, K), lambda i: (i, 0)), _const((1, D)), _const((1, D))],
        out_shape=[_sds((S, D), BF16), _sds((S, D), F32), _sds((S, K), BF16), _sds((1, D), F32), _sds((1, D), F32)],
        operands=(yy, w, x1, g, tgt), name=name, params=_params(("arbitrary",), 52))[0]


def _mm_nt(a, w, *, tm, tk, name, comm=None):
    S, N = a.shape
    D = w.shape[0]
    n_k = N // tk

    def body(a_ref, w_ref, o_ref, acc_ref):
        k = pl.program_id(1)

        @pl.when(k == 0)
        def _():
            acc_ref[...] = jnp.zeros_like(acc_ref)

        acc_ref[...] = lax.dot_general(a_ref[...], w_ref[...], (((1,), (1,)), ((), ())),
                                       preferred_element_type=F32) + acc_ref[...]

        @pl.when(k == n_k - 1)
        def _():
            o_ref[...] = acc_ref[...].astype(BF16)

    outs, extra = _call(
        body, grid=(S // tm, n_k),
        in_specs=[pl.BlockSpec((tm, tk), lambda i, k: (i, k)), pl.BlockSpec((D, tk), lambda i, k: (0, k))],
        out_specs=[pl.BlockSpec((tm, D), lambda i, k: (i, 0))],
        out_shape=[_sds((S, D), BF16)], operands=(a, w),
        scratch_shapes=[pltpu.VMEM((tm, D), F32)],
        name=name, params=_params(("arbitrary", "arbitrary"), 48), comm=comm)
    return outs[0], extra


def _mm_tn(a, b, *, ts, tn, name, comm=None):
    S, M = a.shape
    N = b.shape[1]
    n_s = S // ts

    def body(a_ref, b_ref, o_ref, acc_ref):
        s = pl.program_id(1)

        @pl.when(s == 0)
        def _():
            acc_ref[...] = jnp.zeros_like(acc_ref)

        acc_ref[...] = lax.dot_general(a_ref[...], b_ref[...], (((0,), (0,)), ((), ())),
                                       preferred_element_type=F32) + acc_ref[...]

        @pl.when(s == n_s - 1)
        def _():
            o_ref[...] = acc_ref[...].astype(BF16)

    outs, extra = _call(
        body, grid=(N // tn, n_s),
        in_specs=[pl.BlockSpec((ts, M), lambda j, s: (s, 0)), pl.BlockSpec((ts, tn), lambda j, s: (s, j))],
        out_specs=[pl.BlockSpec((M, tn), lambda j, s: (0, j))],
        out_shape=[_sds((M, N), BF16)], operands=(a, b),
        scratch_shapes=[pltpu.VMEM((M, tn), F32)],
        name=name, params=_params(("arbitrary", "arbitrary"), 48), comm=comm)
    return outs[0], extra


def _pre_bwd_o(dh, x1, dx2, y0, g_pre, g_post, *, tm, name, comm=None):
    S, D = x1.shape

    def body(dh_ref, x1_ref, dx2_ref, y0_ref, gpre_ref, gpost_ref, dx1_ref, dy0_ref, dgpre_ref, dgpost_ref):
        @pl.when(pl.program_id(0) == 0)
        def _():
            dgpre_ref[...] = jnp.zeros_like(dgpre_ref)
            dgpost_ref[...] = jnp.zeros_like(dgpost_ref)

        dh = dh_ref[...].astype(F32)
        x1 = x1_ref[...]
        r2 = _rms(x1)
        xn = x1 * r2
        dgpre_ref[...] += _colsum(dh * xn)
        dx1 = dx2_ref[...] + _norm_bwd(dh * gpre_ref[...], xn, r2)
        dx1_ref[...] = dx1
        y = y0_ref[...].astype(F32)
        r1 = _rms(y)
        n1 = y * r1
        dgpost_ref[...] += _colsum(dx1 * n1)
        dy0_ref[...] = _norm_bwd(dx1 * gpost_ref[...], n1, r1).astype(BF16)

    row = pl.BlockSpec((tm, D), lambda i: (i, 0))
    return _call(
        body, grid=(S // tm,),
        in_specs=[row, row, row, row, _const((1, D)), _const((1, D))],
        out_specs=[row, row, _const((1, D)), _const((1, D))],
        out_shape=[_sds((S, D), F32), _sds((S, D), BF16), _sds((1, D), F32), _sds((1, D), F32)],
        operands=(dh, x1, dx2, y0, g_pre, g_post),
        name=name, params=_params(("arbitrary",), 48), comm=comm)


def _pre_bwd_e(dh, x, dx1, g_pre, *, tm, name):
    S, D = x.shape

    def body(dh_ref, x_ref, dx1_ref, gpre_ref, gx_ref, dgpre_ref):
        @pl.when(pl.program_id(0) == 0)
        def _():
            dgpre_ref[...] = jnp.zeros_like(dgpre_ref)

        dh = dh_ref[...].astype(F32)
        xx = x_ref[...]
        r0 = _rms(xx)
        xn = xx * r0
        dgpre_ref[...] += _colsum(dh * xn)
        gx_ref[...] = dx1_ref[...] + _norm_bwd(dh * gpre_ref[...], xn, r0)

    row = pl.BlockSpec((tm, D), lambda i: (i, 0))
    return _call(
        body, grid=(S // tm,),
        in_specs=[row, row, row, _const((1, D))],
        out_specs=[row, _const((1, D))],
        out_shape=[_sds((S, D), F32), _sds((1, D), F32)],
        operands=(dh, x, dx1, g_pre), name=name, params=_params(("arbitrary",), 56))[0]


SUBLANES = 8


def _shift_copies(sh_ref, ext_ref, cs):
    for b in range(1, SUBLANES):
        sh_ref[b - 1] = ext_ref[pl.ds(b, sh_ref.shape[1]), cs]


def _rows_at(ext_ref, sh_ref, off, cs, tm):
    b = off % SUBLANES
    if b == 0 or sh_ref is None:
        return ext_ref[pl.ds(off, tm), cs]
    return sh_ref[b - 1, pl.ds(off - b, tm), :]


def _taps(ext_ref, w_ref, n_taps, base, cs, tm, sh_ref=None):
    acc = _rows_at(ext_ref, sh_ref, base, cs, tm) * w_ref[0:1, cs]
    for k in range(1, n_taps):
        acc = acc + _rows_at(ext_ref, sh_ref, base + k, cs, tm) * w_ref[k:k + 1, cs]
    return acc


def _taps_rev(ext_ref, w_ref, n_taps, cs, tm, sh_ref=None):
    acc = _rows_at(ext_ref, sh_ref, n_taps - 1, cs, tm) * w_ref[0:1, cs]
    for k in range(1, n_taps):
        acc = acc + _rows_at(ext_ref, sh_ref, n_taps - 1 - k, cs, tm) * w_ref[k:k + 1, cs]
    return acc


def _e_mix_fwd(p, wa, wb, bias, ln_g, ln_b, *, tm, name, comm=None):
    S = p.shape[0]
    W = p.shape[1] // 7
    nb = tm // HALO
    chunks = [slice(c * LANES, (c + 1) * LANES) for c in range(W // LANES)]

    def body(p_ref, hax_ref, hac_ref, hbv_ref, hbg_ref, wa_ref, wb_ref, bias_ref, lg_ref, lb_ref,
             u_ref, cb_ref, ext_ref, sh_ref):
        keep = (pl.program_id(0) > 0).astype(F32)
        col = lambda j, cs: p_ref[:, j * W + cs.start:j * W + cs.stop].astype(F32)

        ext_ref[0:HALO, :] = hax_ref[...].astype(F32) * hac_ref[...].astype(F32) * keep
        ext_ref[HALO:, :] = p_ref[:, 2 * W:3 * W].astype(F32) * p_ref[:, 0:W].astype(F32)
        for cs in chunks:
            conv = _taps(ext_ref, wa_ref, CONV_A, HALO - (CONV_A - 1), cs, tm)
            az = col(3, cs)
            u_ref[:, cs] = (col(1, cs) * conv * (az * _sig(az))).astype(BF16)

        ext_ref[0:HALO, :] = hbv_ref[...].astype(F32) * _sig(hbg_ref[...].astype(F32)) * keep
        ext_ref[HALO:, :] = p_ref[:, 4 * W:5 * W].astype(F32) * _sig(p_ref[:, 5 * W:6 * W].astype(F32))
        s1 = jnp.zeros((tm, LANES), F32)
        for cs in chunks:
            _shift_copies(sh_ref, ext_ref, cs)
            cb = _taps(ext_ref, wb_ref, CONV_B, HALO - (CONV_B - 1), cs, tm, sh_ref) + bias_ref[:, cs]
            cb_ref[:, cs] = cb
            s1 = s1 + cb
        mu = jnp.sum(s1, axis=-1, keepdims=True) * (1.0 / W)
        s2 = jnp.zeros((tm, LANES), F32)
        for cs in chunks:
            xc = cb_ref[:, cs] - mu
            s2 = s2 + xc * xc
        rs = lax.rsqrt(jnp.sum(s2, axis=-1, keepdims=True) * (1.0 / W) + EPS)
        for cs in chunks:
            lb = (cb_ref[:, cs] - mu) * rs * lg_ref[:, cs] + lb_ref[:, cs]
            bz = col(6, cs)
            u_ref[:, W + cs.start:W + cs.stop] = (lb * _sig(lb) * (bz * _sig(bz))).astype(BF16)

    prev = lambda j: pl.BlockSpec((HALO, W), lambda i: (jnp.maximum(i * nb - 1, 0), j))
    return _call(
        body, grid=(S // tm,),
        in_specs=[pl.BlockSpec((tm, 7 * W), lambda i: (i, 0)), prev(0), prev(2), prev(4), prev(5),
                  _const((CONV_A, W)), _const((CONV_B, W)), _const((1, W)), _const((1, W)), _const((1, W))],
        out_specs=[pl.BlockSpec((tm, 2 * W), lambda i: (i, 0)), pl.BlockSpec((tm, W), lambda i: (i, 0))],
        out_shape=[_sds((S, 2 * W), BF16), _sds((S, W), F32)],
        operands=(p, p, p, p, p, wa, wb, bias, ln_g, ln_b),
        scratch_shapes=[pltpu.VMEM((HALO + tm, W), F32),
                        pltpu.VMEM((SUBLANES - 1, HALO + tm - SUBLANES, LANES), F32)],
        name=name, params=_params(("arbitrary",), 48), comm=comm)


def _e_mix_bwd(du, p, cb, wa, wb, ln_g, ln_b, *, tm, name, comm=None):
    S = p.shape[0]
    W = p.shape[1] // 7
    nb = tm // HALO
    n_t = S // tm
    last_blk = S // HALO - 1
    chunks = [slice(c * LANES, (c + 1) * LANES) for c in range(W // LANES)]

    def body(du_ref, duf_ref, p_ref, fab_ref, faz_ref, fbz_ref, hax_ref, hac_ref, hbv_ref, hbg_ref,
             cb_ref, cbf_ref, wa_ref, wb_ref, lg_ref, lb_ref,
             dp_ref, dwa_ref, dwb_ref, dbias_ref, dlg_ref, dlb_ref, extd_ref, extg_ref, shd_ref, shg_ref):
        i = pl.program_id(0)
        keep_prev = (i > 0).astype(F32)
        keep_next = (i < n_t - 1).astype(F32)
        col = lambda j, cs: p_ref[:, j * W + cs.start:j * W + cs.stop].astype(F32)

        @pl.when(i == 0)
        def _():
            dwa_ref[...] = jnp.zeros_like(dwa_ref)
            dwb_ref[...] = jnp.zeros_like(dwb_ref)
            dbias_ref[...] = jnp.zeros_like(dbias_ref)
            dlg_ref[...] = jnp.zeros_like(dlg_ref)
            dlb_ref[...] = jnp.zeros_like(dlb_ref)

        def dcb_rows(rows, cb_rows_ref, dub, bz_of, dst0, scale, main):
            cbv = cb_rows_ref[...]
            mu = jnp.mean(cbv, axis=-1, keepdims=True)
            xc = cbv - mu
            rs = lax.rsqrt(jnp.mean(xc * xc, axis=-1, keepdims=True) + EPS)
            m1 = jnp.zeros((rows, LANES), F32)
            m2 = jnp.zeros((rows, LANES), F32)
            for cs in chunks:
                nbv = (cb_rows_ref[:, cs] - mu) * rs
                lb = nbv * lg_ref[:, cs] + lb_ref[:, cs]
                sl = _sig(lb)
                bz = bz_of(cs)
                sz = _sig(bz)
                dub_c = dub(cs)
                dlb = dub_c * (bz * sz) * _dsilu(lb, sl)
                if main:
                    dlg_ref[:, cs] += _colsum(dlb * nbv)
                    dlb_ref[:, cs] += _colsum(dlb)
                    dp_ref[:, 6 * W + cs.start:6 * W + cs.stop] = (dub_c * (lb * sl) * _dsilu(bz, sz)).astype(BF16)
                dnb = dlb * lg_ref[:, cs]
                extd_ref[dst0:dst0 + rows, cs] = dnb
                m1 = m1 + dnb
                m2 = m2 + dnb * nbv
            m1 = jnp.sum(m1, axis=-1, keepdims=True) * (1.0 / W)
            m2 = jnp.sum(m2, axis=-1, keepdims=True) * (1.0 / W)
            for cs in chunks:
                nbv = (cb_rows_ref[:, cs] - mu) * rs
                dcb = rs * (extd_ref[dst0:dst0 + rows, cs] - m1 - nbv * m2) * scale
                extd_ref[dst0:dst0 + rows, cs] = dcb
                if main:
                    dbias_ref[:, cs] += _colsum(dcb)

        dcb_rows(tm, cb_ref, lambda cs: du_ref[:, W + cs.start:W + cs.stop].astype(F32),
                 lambda cs: col(6, cs), 0, 1.0, True)
        dcb_rows(HALO, cbf_ref, lambda cs: duf_ref[:, W + cs.start:W + cs.stop].astype(F32),
                 lambda cs: fbz_ref[:, cs].astype(F32), tm, keep_next, False)

        extg_ref[0:HALO, :] = hbv_ref[...].astype(F32) * _sig(hbg_ref[...].astype(F32)) * keep_prev
        extg_ref[HALO:, :] = p_ref[:, 4 * W:5 * W].astype(F32) * _sig(p_ref[:, 5 * W:6 * W].astype(F32))
        base_b = HALO - (CONV_B - 1)
        for cs in chunks:
            _shift_copies(shd_ref, extd_ref, cs)
            _shift_copies(shg_ref, extg_ref, cs)
            dgb = _taps_rev(extd_ref, wb_ref, CONV_B, cs, tm, shd_ref)
            bv = col(4, cs)
            sg = _sig(col(5, cs))
            dp_ref[:, 4 * W + cs.start:4 * W + cs.stop] = (dgb * sg).astype(BF16)
            dp_ref[:, 5 * W + cs.start:5 * W + cs.stop] = (dgb * bv * sg * (1.0 - sg)).astype(BF16)
            dcb = extd_ref[0:tm, cs]
            for k in range(CONV_B):
                dwb_ref[k:k + 1, cs] += _colsum(dcb * _rows_at(extg_ref, shg_ref, base_b + k, cs, tm))

        extg_ref[0:HALO, :] = hax_ref[...].astype(F32) * hac_ref[...].astype(F32) * keep_prev
        extg_ref[HALO:, :] = p_ref[:, 2 * W:3 * W].astype(F32) * p_ref[:, 0:W].astype(F32)
        base_a = HALO - (CONV_A - 1)
        for cs in chunks:
            conv = _taps(extg_ref, wa_ref, CONV_A, base_a, cs, tm)
            az = col(3, cs)
            sz = _sig(az)
            ab = col(1, cs)
            dua = du_ref[:, cs].astype(F32)
            dya = dua * (az * sz)
            dp_ref[:, W + cs.start:W + cs.stop] = (dya * conv).astype(BF16)
            dp_ref[:, 3 * W + cs.start:3 * W + cs.stop] = (dua * (ab * conv) * _dsilu(az, sz)).astype(BF16)
            extd_ref[0:tm, cs] = dya * ab
            azf = faz_ref[:, cs].astype(F32)
            extd_ref[tm:tm + HALO, cs] = (duf_ref[:, cs].astype(F32) * (azf * _sig(azf))
                                          * fab_ref[:, cs].astype(F32) * keep_next)
        for cs in chunks:
            dca = _taps_rev(extd_ref, wa_ref, CONV_A, cs, tm)
            dp_ref[:, cs] = (dca * col(2, cs)).astype(BF16)
            dp_ref[:, 2 * W + cs.start:2 * W + cs.stop] = (dca * col(0, cs)).astype(BF16)
            dconv = extd_ref[0:tm, cs]
            for k in range(CONV_A):
                dwa_ref[k:k + 1, cs] += _colsum(dconv * extg_ref[pl.ds(base_a + k, tm), cs])

    prev = lambda j: pl.BlockSpec((HALO, W), lambda i: (jnp.maximum(i * nb - 1, 0), j))
    nxt = lambda j, w: pl.BlockSpec((HALO, w), lambda i: (jnp.minimum((i + 1) * nb, last_blk), j))
    row = lambda w: pl.BlockSpec((tm, w), lambda i: (i, 0))
    return _call(
        body, grid=(n_t,),
        in_specs=[row(2 * W), nxt(0, 2 * W), row(7 * W), nxt(1, W), nxt(3, W), nxt(6, W),
                  prev(0), prev(2), prev(4), prev(5), row(W), nxt(0, W),
                  _const((CONV_A, W)), _const((CONV_B, W)), _const((1, W)), _const((1, W))],
        out_specs=[row(7 * W), _const((CONV_A, W)), _const((CONV_B, W)), _const((1, W)), _const((1, W)), _const((1, W))],
        out_shape=[_sds((S, 7 * W), BF16), _sds((CONV_A, W), F32), _sds((CONV_B, W), F32),
                   _sds((1, W), F32), _sds((1, W), F32), _sds((1, W), F32)],
        operands=(du, du, p, p, p, p, p, p, p, p, cb, cb, wa, wb, ln_g, ln_b),
        scratch_shapes=[pltpu.VMEM((tm + HALO, W), F32), pltpu.VMEM((HALO + tm, W), F32),
                        pltpu.VMEM((SUBLANES - 1, HALO + tm - SUBLANES, LANES), F32),
                        pltpu.VMEM((SUBLANES - 1, HALO + tm - SUBLANES, LANES), F32)],
        name=name, params=_params(("arbitrary",), 52), comm=comm)


def _counts(i, tm, rows, off, win):
    t = i * tm + off + lax.broadcasted_iota(jnp.int32, (rows, 1), 0)
    return jnp.minimum(t + 1, win).astype(F32)


def _o_mix_fwd(q, cw, cb, cscale, *, tm, name):
    S = q.shape[0]
    WC = q.shape[1] // 2
    NG = len(POOL_WINDOWS)
    G = WC // NG
    nb = tm // PHALO

    def body(v_ref, z_ref, hv_ref, cw_ref, cb_ref, sc_ref, yy_ref, pooled_ref, gg_ref, ext_ref):
        i = pl.program_id(0)
        keep = (i > 0).astype(F32)
        for g, win in enumerate(POOL_WINDOWS):
            cs = slice(g * G, (g + 1) * G)
            v = v_ref[:, cs].astype(F32)
            ext_ref[0:PHALO, :] = hv_ref[:, cs].astype(F32) * keep
            ext_ref[PHALO:, :] = v
            s = v
            for j in range(1, win):
                s = s + ext_ref[pl.ds(PHALO - j, tm), :]
            pooled = (s / _counts(i, tm, tm, 0, win) - v).astype(BF16)
            pooled_ref[:, cs] = pooled
            gg = jnp.dot(pooled, cw_ref[g], preferred_element_type=F32) + cb_ref[:, cs]
            gg_ref[:, cs] = gg.astype(BF16)
            z = z_ref[:, cs].astype(F32)
            yy_ref[:, cs] = (gg * sc_ref[:, cs] * (z * _sig(z))).astype(BF16)

    row = lambda j: pl.BlockSpec((tm, WC), lambda i: (i, j))
    out = pl.BlockSpec((tm, WC), lambda i: (i, 0))
    return _call(
        body, grid=(S // tm,),
        in_specs=[row(0), row(1), pl.BlockSpec((PHALO, WC), lambda i: (jnp.maximum(i * nb - 1, 0), 0)),
                  _const((NG, G, G)), _const((1, WC)), _const((1, WC))],
        out_specs=[out, out, out],
        out_shape=[_sds((S, WC), BF16)] * 3, operands=(q, q, q, cw, cb, cscale),
        scratch_shapes=[pltpu.VMEM((PHALO + tm, G), F32)],
        name=name, params=_params(("arbitrary",), 40))[0]


def _o_mix_bwd(dyy, q, gg, pooled, cw, cscale, *, tm, name):
    S = q.shape[0]
    WC = q.shape[1] // 2
    NG = len(POOL_WINDOWS)
    G = WC // NG
    nb = tm // PHALO
    n_t = S // tm
    last_blk = S // PHALO - 1
    nt = (((1,), (1,)), ((), ()))
    tn = (((0,), (0,)), ((), ()))

    def body(dyy_ref, dyyf_ref, z_ref, zf_ref, gg_ref, pooled_ref, cw_ref, sc_ref,
             dq_ref, dcw_ref, dcb_ref, dsc_ref, ext_ref):
        i = pl.program_id(0)
        keep_next = (i < n_t - 1).astype(F32)

        @pl.when(i == 0)
        def _():
            dcw_ref[...] = jnp.zeros_like(dcw_ref)
            dcb_ref[...] = jnp.zeros_like(dcb_ref)
            dsc_ref[...] = jnp.zeros_like(dsc_ref)

        for g, win in enumerate(POOL_WINDOWS):
            cs = slice(g * G, (g + 1) * G)
            sc = sc_ref[:, cs]
            z = z_ref[:, cs].astype(F32)
            sz = _sig(z)
            dyy_c = dyy_ref[:, cs].astype(F32)
            ggv = gg_ref[:, cs].astype(F32)
            dyy0 = dyy_c * (z * sz)
            dq_ref[:, WC + cs.start:WC + cs.stop] = (dyy_c * (ggv * sc) * _dsilu(z, sz)).astype(BF16)
            dgg = dyy0 * sc
            dsc_ref[:, cs] += _colsum(dyy0 * ggv)
            dcb_ref[:, cs] += _colsum(dgg)
            dgg_b = dgg.astype(BF16)
            dcw_ref[g] += lax.dot_general(pooled_ref[:, cs], dgg_b, tn, preferred_element_type=F32)
            dpool = lax.dot_general(dgg_b, cw_ref[g], nt, preferred_element_type=F32)
            zf = zf_ref[:, cs].astype(F32)
            dgg_f = (dyyf_ref[:, cs].astype(F32) * (zf * _sig(zf)) * sc * keep_next).astype(BF16)
            dpool_f = lax.dot_general(dgg_f, cw_ref[g], nt, preferred_element_type=F32)
            ext_ref[0:tm, :] = dpool / _counts(i, tm, tm, 0, win)
            ext_ref[tm:tm + PHALO, :] = dpool_f / _counts(i, tm, PHALO, tm, win)
            dv = ext_ref[0:tm, :] - dpool
            for j in range(1, win):
                dv = dv + ext_ref[pl.ds(j, tm), :]
            dq_ref[:, cs] = dv.astype(BF16)

    row = lambda: pl.BlockSpec((tm, WC), lambda i: (i, 0))
    nxt = lambda j: pl.BlockSpec((PHALO, WC), lambda i: (jnp.minimum((i + 1) * nb, last_blk), j))
    return _call(
        body, grid=(n_t,),
        in_specs=[row(), nxt(0), pl.BlockSpec((tm, WC), lambda i: (i, 1)), nxt(1), row(), row(),
                  _const((NG, G, G)), _const((1, WC))],
        out_specs=[pl.BlockSpec((tm, 2 * WC), lambda i: (i, 0)), _const((NG, G, G)), _const((1, WC)), _const((1, WC))],
        out_shape=[_sds((S, 2 * WC), BF16), _sds((NG, G, G), F32), _sds((1, WC), F32), _sds((1, WC), F32)],
        operands=(dyy, dyy, q, q, gg, pooled, cw, cscale),
        scratch_shapes=[pltpu.VMEM((tm + PHALO, G), F32)],
        name=name, params=_params(("arbitrary",), 48))[0]


def _place():
    return lax.axis_index("x"), lax.axis_index("y"), lax.axis_index("c")


def _piece(ref, axis, size, index):
    start = index * size
    if axis == len(ref.shape) - 1:
        start = pl.multiple_of(start, LANES)
    idx = [slice(None)] * len(ref.shape)
    idx[axis] = pl.ds(start, size)
    return ref.at[tuple(idx)]


def _gather_copies(src, out, axis, size, send_sems, recv_sems, base, held=None):
    x, y, c = _place()
    sib, xn, yn = (x, y, 1 - c), (1 - x, y, c), (x, 1 - y, c)

    def blk(px, py, of=out):
        return _piece(of, axis, size, 4 * px + 2 * py + c)

    def half(ref, h):
        n = ref.shape[0] // 2
        return ref.at[pl.ds(h * n, n)]

    def rc(k, s, d, to):
        return pltpu.make_async_remote_copy(src_ref=s, dst_ref=d, send_sem=send_sems.at[base + k],
                                            recv_sem=recv_sems.at[base + k], device_id=to, device_id_type=MESH)

    own, xb, yb, db = blk(x, y), blk(1 - x, y), blk(x, 1 - y), blk(1 - x, 1 - y)
    got = out if held is None else held
    xs, ys, ds = blk(1 - x, y, got), blk(x, 1 - y, got), blk(1 - x, 1 - y, got)
    return [rc(0, src, own, sib), rc(1, src, own, xn), rc(2, src, own, yn),
            rc(3, half(xs, 0), half(xb, 0), yn), rc(4, half(ys, 1), half(yb, 1), xn),
            rc(5, xs, xb, sib), rc(6, ys, yb, sib), rc(7, ds, db, sib)]


N_GATHER = 8


def _gather_comm(shards, axes, phases):
    n = len(shards)
    if phases == "second":
        sizes = [s.shape[a] // N_DEV for s, a in zip(shards, axes)]
        full = [_sds(s.shape, s.dtype) for s in shards]
    else:
        sizes = [s.shape[a] for s, a in zip(shards, axes)]
        full = [_sds(s.shape[:a] + (N_DEV * s.shape[a],) + s.shape[a + 1:], s.dtype) for s, a in zip(shards, axes)]

    def plan(ins, outs, sems):
        x, y, c = _place()
        me = 4 * x + 2 * y + c
        if phases == "second":
            cps = [_gather_copies(_piece(ins[t], axes[t], sizes[t], me), outs[t], axes[t], sizes[t], sems[0], sems[1],
                                  N_GATHER * t, ins[t]) for t in range(n)]
        else:
            cps = [_gather_copies(ins[t], outs[t], axes[t], sizes[t], sems[0], sems[1], N_GATHER * t)
                   for t in range(n)]
        mine = [pltpu.make_async_copy(ins[t], _piece(outs[t], axes[t], sizes[t], me), sems[2].at[t])
                for t in range(n)] if phases != "second" else []
        return cps, mine

    def send_own(ins, outs, sems):
        cps, mine = plan(ins, outs, sems)
        for t in range(n):
            mine[t].start()
            for k in (0, 1, 2):
                cps[t][k].start()

    def pass_on(ins, outs, sems):
        cps, _ = plan(ins, outs, sems)
        for t in range(n):
            if phases == "all":
                cps[t][1].wait_recv()
            cps[t][3].start()
            cps[t][5].start()
        for t in range(n):
            if phases == "all":
                cps[t][2].wait_recv()
            cps[t][4].start()
            cps[t][6].start()

    def own_landed(ins, outs, sems):
        cps, mine = plan(ins, outs, sems)
        for t in range(n):
            for k in (0, 1, 2):
                cps[t][k].wait()
            mine[t].wait()

    def all_landed(ins, outs, sems):
        cps, mine = plan(ins, outs, sems)
        for t in range(n):
            cps[t][3].wait_recv()
            cps[t][4].wait_recv()
            cps[t][7].start()
        for t in range(n):
            for k in ((0, 5, 6, 7) if phases == "all" else (5, 6, 7)):
                cps[t][k].wait_recv()
            for k in (range(N_GATHER) if phases == "all" else range(3, N_GATHER)):
                cps[t][k].wait_send()
            if phases == "all":
                mine[t].wait()

    sems = [pltpu.SemaphoreType.DMA((N_GATHER * n,)), pltpu.SemaphoreType.DMA((N_GATHER * n,))]
    if phases != "second":
        sems.append(pltpu.SemaphoreType.DMA((n,)))
    if phases == "all":
        return _Comm(shards, full, sems, send_own, all_landed, middle=pass_on)
    if phases == "first":
        return _Comm(shards, full, sems, send_own, own_landed)
    return _Comm(shards, full, sems, pass_on, all_landed, aliases={t: t for t in range(n)})


def _pair_comm(grads, axes, sizes):
    n = len(grads)
    outs_sds = [_sds((4,) + g.shape[:a] + (s,) + g.shape[a + 1:], g.dtype) for g, a, s in zip(grads, axes, sizes)]

    def copies(ins, outs, sems):
        send_sems, recv_sems = sems
        x, y, c = _place()
        return [pltpu.make_async_remote_copy(
            src_ref=_piece(ins[t], axes[t], sizes[t], 2 * qi + (1 - c)), dst_ref=outs[t].at[qi],
            send_sem=send_sems.at[4 * t + qi], recv_sem=recv_sems.at[4 * t + qi],
            device_id=(x, y, 1 - c), device_id_type=MESH) for t in range(n) for qi in range(4)]

    def start(ins, outs, sems):
        for cp in copies(ins, outs, sems):
            cp.start()

    def finish(ins, outs, sems):
        for cp in copies(ins, outs, sems):
            cp.wait()

    sems = [pltpu.SemaphoreType.DMA((4 * n,)), pltpu.SemaphoreType.DMA((4 * n,))]
    return _Comm(grads, outs_sds, sems, start, finish)


def _chip_comm(sums):
    n = len(sums)
    outs_sds = [_sds((3,) + s.shape[1:], s.dtype) for s in sums]

    def copies(ins, outs, sems):
        send_sems, recv_sems = sems
        x, y, c = _place()
        return [pltpu.make_async_remote_copy(
            src_ref=ins[t].at[2 * qx + qy], dst_ref=outs[t].at[j],
            send_sem=send_sems.at[3 * t + j], recv_sem=recv_sems.at[3 * t + j],
            device_id=(qx, qy, c), device_id_type=MESH)
            for t in range(n) for j, (qx, qy) in enumerate([(1 - x, y), (x, 1 - y), (1 - x, 1 - y)])]

    def start(ins, outs, sems):
        for cp in copies(ins, outs, sems):
            cp.start()

    def finish(ins, outs, sems):
        for cp in copies(ins, outs, sems):
            cp.wait()

    sems = [pltpu.SemaphoreType.DMA((3 * n,)), pltpu.SemaphoreType.DMA((3 * n,))]
    return _Comm(sums, outs_sds, sems, start, finish)


def _small_comm(small):
    def copies(ins, outs, sems):
        send_sems, recv_sems, local_sem = sems
        x, y, c = _place()
        mine = outs[0].at[4 * x + 2 * y + c]
        out = [pltpu.make_async_copy(ins[0], mine, local_sem.at[0])]
        for k in range(1, N_DEV):
            peer = (1 - x if k & 4 else x, 1 - y if k & 2 else y, 1 - c if k & 1 else c)
            out.append(pltpu.make_async_remote_copy(
                src_ref=ins[0], dst_ref=mine, send_sem=send_sems.at[k - 1], recv_sem=recv_sems.at[k - 1],
                device_id=peer, device_id_type=MESH))
        return out

    def start(ins, outs, sems):
        for cp in copies(ins, outs, sems):
            cp.start()

    def finish(ins, outs, sems):
        for cp in copies(ins, outs, sems):
            cp.wait()

    sems = [pltpu.SemaphoreType.DMA((N_DEV - 1,)), pltpu.SemaphoreType.DMA((N_DEV - 1,)), pltpu.SemaphoreType.DMA((1,))]
    return _Comm([small], [_sds((N_DEV,) + small.shape, small.dtype)], sems, start, finish)


def _small_scatter_comm(send):
    def copies(ins, outs, sems):
        send_sems, recv_sems, local_sem = sems
        x, y, c = _place()
        me = 4 * x + 2 * y + c
        out = [pltpu.make_async_copy(ins[0].at[me], outs[0].at[me], local_sem.at[0])]
        for k in range(1, N_DEV):
            px, py, pc = (1 - x if k & 4 else x, 1 - y if k & 2 else y, 1 - c if k & 1 else c)
            out.append(pltpu.make_async_remote_copy(
                src_ref=ins[0].at[4 * px + 2 * py + pc], dst_ref=outs[0].at[me], send_sem=send_sems.at[k - 1],
                recv_sem=recv_sems.at[k - 1], device_id=(px, py, pc), device_id_type=MESH))
        return out

    def start(ins, outs, sems):
        for cp in copies(ins, outs, sems):
            cp.start()

    def finish(ins, outs, sems):
        for cp in copies(ins, outs, sems):
            cp.wait()

    sems = [pltpu.SemaphoreType.DMA((N_DEV - 1,)), pltpu.SemaphoreType.DMA((N_DEV - 1,)), pltpu.SemaphoreType.DMA((1,))]
    return _Comm([send], [_sds(send.shape, send.dtype)], sems, start, finish)


def _pair_sum(c_idx, grad, recv, axis, size, split, *, name):
    nd = len(grad.shape)
    piece = grad.shape[:axis] + (size,) + grad.shape[axis + 1:]
    blk = (piece[0] // split,) + piece[1:]

    def g_map(q, r, c_ref):
        idx = [0] * nd
        idx[axis] = 2 * q + c_ref[0]
        idx[0] = idx[0] * split + r if axis == 0 else r
        return tuple(idx)

    def r_map(q, r, c_ref):
        return (q, r) + (0,) * (nd - 1)

    def body(c_ref, g_ref, r_ref, o_ref):
        o_ref[0] = (g_ref[...].astype(F32) + r_ref[0].astype(F32)).astype(BF16)

    return _call(
        body, grid=(4, split), prefetch=c_idx,
        in_specs=[pl.BlockSpec(blk, g_map), pl.BlockSpec((1,) + blk, r_map)],
        out_specs=[pl.BlockSpec((1,) + blk, r_map)], out_shape=[_sds((4,) + piece, BF16)],
        operands=(grad, recv), name=name, params=_params(("arbitrary", "arbitrary"), 32))[0][0]


def _adam_math(w, g, m, v):
    m = ADAM_B1 * m + (1.0 - ADAM_B1) * g
    v = ADAM_B2 * v + (1.0 - ADAM_B2) * (g * g)
    m_hat = m / (1.0 - ADAM_B1 ** ADAM_STEP)
    v_hat = v / (1.0 - ADAM_B2 ** ADAM_STEP)
    delta = -ADAM_LR * (m_hat / (jnp.sqrt(v_hat) + ADAM_EPS) + ADAM_WD * w)
    return delta, m, v


def _adam_big(q_idx, sums, recv, w, m, v, split, *, name, comm=None):
    shape = w.shape
    nd = len(shape)
    blk = (shape[0] // split,) + shape[1:]
    w_map = lambda r, q_ref: (r,) + (0,) * (nd - 1)
    s_map = lambda r, q_ref: (q_ref[0], r) + (0,) * (nd - 1)
    r_map = lambda r, q_ref: (0, r) + (0,) * (nd - 1)

    def body(q_ref, s_ref, r_ref, w_ref, m_ref, v_ref, g_ref, d_ref, nm_ref, nv_ref):
        g = s_ref[0].astype(F32) + r_ref[0].astype(F32) + r_ref[1].astype(F32) + r_ref[2].astype(F32)
        g_ref[...] = g
        d_ref[...], nm_ref[...], nv_ref[...] = _adam_math(w_ref[...], g, m_ref[...], v_ref[...])

    wspec = pl.BlockSpec(blk, w_map)
    return _call(
        body, grid=(split,), prefetch=q_idx,
        in_specs=[pl.BlockSpec((1,) + blk, s_map), pl.BlockSpec((3,) + blk, r_map), wspec, wspec, wspec],
        out_specs=[wspec] * 4, out_shape=[_sds(shape, F32)] * 4, operands=(sums, recv, w, m, v),
        name=name, params=_params(("arbitrary",), 32), comm=comm)


def _adam_small(parts, w, m, v, *, name):
    R = w.shape[0]

    def body(p_ref, w_ref, m_ref, v_ref, g_ref, d_ref, nm_ref, nv_ref):
        g = p_ref[0]
        for d in range(1, N_DEV):
            g = g + p_ref[d]
        g_ref[...] = g
        d_ref[...], nm_ref[...], nv_ref[...] = _adam_math(w_ref[...], g, m_ref[...], v_ref[...])

    whole = _const((R, LANES))
    return _call(
        body, grid=(1,), in_specs=[_const((N_DEV, R, LANES)), whole, whole, whole], out_specs=[whole] * 4,
        out_shape=[_sds((R, LANES), F32)] * 4, operands=(parts, w, m, v), name=name,
        params=_params(("arbitrary",), 32))[0]


def _pack(arrs):
    return jnp.concatenate([a.reshape(-1) for a in arrs]).reshape(-1, LANES)


def _unpack(packed, shapes):
    flat = packed.reshape(-1)
    out, off = [], 0
    for s in shapes:
        n = 1
        for d in s:
            n *= d
        out.append(flat[off:off + n].reshape(s))
        off += n
    return out


BIG = ("e_in", "e_out", "o_in", "o_cw", "o_out")
BIG_AXIS = dict(e_in=1, e_out=0, o_in=1, o_cw=1, o_out=0)
BIG_SPLIT = dict(e_in=8, e_out=4, o_in=4, o_cw=4, o_out=4)
REPLICATED = ("e_norm_pre", "e_norm_post", "e_b_conv_bias", "e_b_ln_g", "e_b_ln_b")
SHARDED = ("e_a_conv", "e_b_conv", "o_norm_pre", "o_norm_post", "o_c_b", "o_c_scale")
SMALL = REPLICATED + SHARDED


class _Exchange:
    def __init__(self, shards, small, order, c_idx):
        self.shards = shards
        self.small = small
        self.order = order
        self.c_idx = c_idx
        self.reduced = {}

    def gather(self, keys):
        return _gather_comm([self.shards[k] for k in keys], [BIG_AXIS[k] for k in keys], "all")

    def gather1(self, keys):
        return _gather_comm([self.shards[k] for k in keys], [BIG_AXIS[k] for k in keys], "first")

    def gather2(self, keys, firsts):
        return _gather_comm(firsts, [BIG_AXIS[k] for k in keys], "second")

    def pair(self, grads):
        keys = list(grads)
        return _pair_comm([grads[k] for k in keys], [BIG_AXIS[k] for k in keys],
                          [grads[k].shape[BIG_AXIS[k]] // N_DEV for k in keys])

    def pair_sums(self, grads, received):
        return {k: _pair_sum(self.c_idx, grads[k], r, BIG_AXIS[k], grads[k].shape[BIG_AXIS[k]] // N_DEV,
                             BIG_SPLIT[k], name="pair_sum_" + k) for k, r in zip(grads, received)}

    def chips(self, sums):
        return _chip_comm([sums[k] for k in sums])

    def done(self, sums, received):
        self.reduced.update({k: (sums[k], r) for k, r in zip(sums, received)})


def _local_step(x, tgt, w_small, ex):
    S, D = x.shape
    tnt, tx, tw = min(TM_NT, S), min(TM_MIX, S), min(TM_WIDE, S)

    h0, got = _rms_norm(x, w_small["e_norm_pre"], tm=tw, name="e_norm", comm=_small_comm(ex.small))
    per_dev = [_unpack(got[0][d], [w_small[k].shape for k in SHARDED]) for d in range(N_DEV)]
    sm = {k: w_small[k] for k in REPLICATED}
    for j, k in enumerate(SHARDED):
        sm[k] = jnp.concatenate([per_dev[d][j] for d in range(N_DEV)], axis=-1)
    n_groups = sm["o_c_b"].shape[0]
    sm["o_c_b"] = sm["o_c_b"].reshape(1, -1)

    wt = {}
    p, wt["e_in"], _ = _gather_matmul(ex.order, h0, ex.shards["e_in"], tm=tnt, name="e_in_fwd")
    W = p.shape[1] // 7
    (u, cb), got = _e_mix_fwd(p, sm["e_a_conv"], sm["e_b_conv"], sm["e_b_conv_bias"], sm["e_b_ln_g"],
                              sm["e_b_ln_b"], tm=tx, name="e_mix_fwd", comm=ex.gather(["e_out"]))
    wt["e_out"] = got[0]
    late = ["o_out", "o_cw"]
    (x1, y0), part = _out_norm_res(u, wt["e_out"], x, sm["e_norm_post"], tm=tw, name="e_out_fwd",
                                   comm=ex.gather1(late))
    h1, _ = _rms_norm(x1, sm["o_norm_pre"], tm=tw, name="o_norm")
    q, wt["o_in"], got = _gather_matmul(ex.order, h1, ex.shards["o_in"], tm=tnt, name="o_in_fwd",
                                        comm=ex.gather2(late, part))
    wt.update(zip(late, got))
    yy, pooled, gg = _o_mix_fwd(q, wt["o_cw"], sm["o_c_b"], sm["o_c_scale"], tm=tw, name="o_mix_fwd")
    dout, dx2, dyy, lcol, dg_o_post = _out_loss(yy, wt["o_out"], x1, sm["o_norm_post"], tgt, tm=tx, name="o_out_loss")
    loss = (0.5 / D) * jnp.sum(lcol)

    dq, d_cw, d_cb, d_cscale = _o_mix_bwd(dyy, q, gg, pooled, wt["o_cw"], sm["o_c_scale"], tm=tw, name="o_mix_bwd")
    g_o_out, _ = _mm_tn(yy, dout, ts=tnt, tn=W, name="o_out_dw")
    ga = dict(o_out=g_o_out, o_cw=d_cw.astype(BF16))
    dh1, ra = _mm_nt(dq, wt["o_in"], tm=tnt, tk=W, name="o_in_bwd", comm=ex.pair(ga))
    sa = ex.pair_sums(ga, ra)
    (dx1, dy0, dg_o_pre, dg_e_post), ra = _pre_bwd_o(dh1, x1, dx2, y0, sm["o_norm_pre"], sm["e_norm_post"],
                                                     tm=tx, name="o_pre_bwd", comm=ex.chips(sa))
    ex.done(sa, ra)
    g_o_in, _ = _mm_tn(h1, dq, ts=tnt, tn=W, name="o_in_dw")
    gb = dict(o_in=g_o_in)
    du, rb = _mm_nt(dy0, wt["e_out"], tm=tnt, tk=W, name="e_out_bwd", comm=ex.pair(gb))
    sb = ex.pair_sums(gb, rb)
    g_e_out, _ = _mm_tn(u, dy0, ts=tnt, tn=W, name="e_out_dw")
    gc = dict(e_out=g_e_out)
    (dp, d_wa, d_wb, d_bias, d_lg, d_lb), rbc = _e_mix_bwd(
        du, p, cb, sm["e_a_conv"], sm["e_b_conv"], sm["e_b_ln_g"], sm["e_b_ln_b"], tm=tx, name="e_mix_bwd",
        comm=_merge(ex.chips(sb), ex.pair(gc)))
    ex.done(sb, rbc[:1])
    sc = ex.pair_sums(gc, rbc[1:])
    g_e_in, rc = _mm_tn(h0, dp, ts=tnt, tn=W, name="e_in_dw", comm=ex.chips(sc))
    ex.done(sc, rc)
    gd = dict(e_in=g_e_in)
    sd = ex.pair_sums(gd, _run_comm(ex.pair(gd), "pair_e_in"))
    dh0, rd = _mm_nt(dp, wt["e_in"], tm=tnt, tk=W, name="e_in_bwd", comm=ex.chips(sd))
    ex.done(sd, rd)
    grad_x, dg_e_pre = _pre_bwd_e(dh0, x, dx1, sm["e_norm_pre"], tm=tw, name="e_pre_bwd")

    small = dict(e_norm_pre=dg_e_pre, e_norm_post=dg_e_post, e_a_conv=d_wa, e_b_conv=d_wb, e_b_conv_bias=d_bias,
                 e_b_ln_g=d_lg, e_b_ln_b=d_lb, o_norm_pre=dg_o_pre, o_norm_post=dg_o_post,
                 o_c_b=d_cb.reshape(n_groups, -1), o_c_scale=d_cscale)
    return loss, grad_x, small


def kernel(x, e_norm_pre, e_norm_post, e_w_in, e_a_conv, e_b_conv, e_b_conv_bias, e_b_ln_g, e_b_ln_b, e_w_out, o_norm_pre, o_norm_post, o_w_in, o_c_w, o_c_b, o_c_scale, o_w_out, loss_target, m_e_norm_pre, m_e_norm_post, m_e_w_in, m_e_a_conv, m_e_b_conv, m_e_b_conv_bias, m_e_b_ln_g, m_e_b_ln_b, m_e_w_out, m_o_norm_pre, m_o_norm_post, m_o_w_in, m_o_c_w, m_o_c_b, m_o_c_scale, m_o_w_out, v_e_norm_pre, v_e_norm_post, v_e_w_in, v_e_a_conv, v_e_b_conv, v_e_b_conv_bias, v_e_b_ln_g, v_e_b_ln_b, v_e_w_out, v_o_norm_pre, v_o_norm_post, v_o_w_in, v_o_c_w, v_o_c_b, v_o_c_scale, v_o_w_out):
    xi, yi, ci = _place()
    w_big = dict(e_in=e_w_in[0], e_out=e_w_out[0], o_in=o_w_in[0], o_cw=o_c_w[0], o_out=o_w_out[0])
    m_big = dict(e_in=m_e_w_in[0], e_out=m_e_w_out[0], o_in=m_o_w_in[0], o_cw=m_o_c_w[0], o_out=m_o_w_out[0])
    v_big = dict(e_in=v_e_w_in[0], e_out=v_e_w_out[0], o_in=v_o_w_in[0], o_cw=v_o_c_w[0], o_out=v_o_w_out[0])
    w_small = dict(e_norm_pre=e_norm_pre, e_norm_post=e_norm_post, e_b_conv_bias=e_b_conv_bias, e_b_ln_g=e_b_ln_g,
                   e_b_ln_b=e_b_ln_b, e_a_conv=e_a_conv[0], e_b_conv=e_b_conv[0], o_norm_pre=o_norm_pre,
                   o_norm_post=o_norm_post, o_c_b=o_c_b[0], o_c_scale=o_c_scale)
    m_small = dict(e_norm_pre=m_e_norm_pre, e_norm_post=m_e_norm_post, e_b_conv_bias=m_e_b_conv_bias,
                   e_b_ln_g=m_e_b_ln_g, e_b_ln_b=m_e_b_ln_b, e_a_conv=m_e_a_conv[0], e_b_conv=m_e_b_conv[0],
                   o_norm_pre=m_o_norm_pre, o_norm_post=m_o_norm_post, o_c_b=m_o_c_b[0], o_c_scale=m_o_c_scale)
    v_small = dict(e_norm_pre=v_e_norm_pre, e_norm_post=v_e_norm_post, e_b_conv_bias=v_e_b_conv_bias,
                   e_b_ln_g=v_e_b_ln_g, e_b_ln_b=v_e_b_ln_b, e_a_conv=v_e_a_conv[0], e_b_conv=v_e_b_conv[0],
                   o_norm_pre=v_o_norm_pre, o_norm_post=v_o_norm_post, o_c_b=v_o_c_b[0], o_c_scale=v_o_c_scale)

    c_idx = jnp.reshape(ci, (1,)).astype(jnp.int32)
    order = jnp.stack([2 * xi + yi, 2 * (1 - xi) + yi, 2 * xi + (1 - yi), 2 * (1 - xi) + (1 - yi)]).astype(jnp.int32)
    ex = _Exchange({k: w_big[k].astype(BF16) for k in BIG}, _pack([w_small[k] for k in SHARDED]), order, c_idx)
    loss, grad_x, g_small = _local_step(x[0], loss_target[0], w_small, ex)

    q_idx = jnp.reshape(2 * xi + yi, (1,)).astype(jnp.int32)
    big_out = {k: _adam_big(q_idx, *ex.reduced[k], w_big[k], m_big[k], v_big[k], BIG_SPLIT[k], name="adam_" + k)[0]
               for k in BIG}

    rep = _pack([g_small[k] for k in REPLICATED])
    loss_row = jnp.pad(jnp.reshape(loss, (1, 1)), ((0, 0), (0, LANES - 1)))
    blocks = []
    for k in SHARDED:
        r, n = w_small[k].shape
        blocks.append(g_small[k].reshape(r, N_DEV, n).transpose(1, 0, 2).reshape(N_DEV, r * n))
    blocks = jnp.concatenate(blocks, axis=1).reshape(N_DEV, -1, LANES)
    head = jnp.concatenate([rep, loss_row], axis=0)
    send = jnp.concatenate([jnp.broadcast_to(head[None], (N_DEV,) + head.shape), blocks], axis=1)
    parts = _run_comm(_small_scatter_comm(send), "small_grad_exchange")[0]

    def own_rows(d):
        return jnp.concatenate([_pack([d[k] for k in REPLICATED]), jnp.ones((1, LANES), F32),
                                _pack([d[k] for k in SHARDED])], axis=0)

    res_small = _adam_small(parts, own_rows(w_small), own_rows(m_small), own_rows(v_small), name="adam_small")
    n_rep = rep.shape[0]
    loss = res_small[0][n_rep, 0]
    small_out = {k: [] for k in SMALL}
    for packed in res_small:
        for k, t in zip(REPLICATED, _unpack(packed[:n_rep], [w_small[k].shape for k in REPLICATED])):
            small_out[k].append(t)
        for k, t in zip(SHARDED, _unpack(packed[n_rep + 1:], [w_small[k].shape for k in SHARDED])):
            small_out[k].append(t)

    big_of = dict(e_w_in="e_in", e_w_out="e_out", o_w_in="o_in", o_c_w="o_cw", o_w_out="o_out")
    stacked = ("e_a_conv", "e_b_conv", "o_c_b")

    def leaf(name, which):
        if name in big_of:
            return big_out[big_of[name]][which][None]
        t = small_out[name][which]
        return t[None] if name in stacked else t

    order = ("e_norm_pre", "e_norm_post", "e_w_in", "e_a_conv", "e_b_conv", "e_b_conv_bias", "e_b_ln_g", "e_b_ln_b",
             "e_w_out", "o_norm_pre", "o_norm_post", "o_w_in", "o_c_w", "o_c_b", "o_c_scale", "o_w_out")
    outs = [loss, grad_x[None]]
    for which in range(4):
        outs += [leaf(nm, which) for nm in order]
    return tuple(outs)
```

```python
import jax
import jax.numpy as jnp
from jax import lax
from jax.experimental import pallas as pl
from jax.experimental.pallas import tpu as pltpu

F32 = jnp.float32
BF16 = jnp.bfloat16
EPS = 1e-6
MESH = pl.DeviceIdType.MESH
ANY = pl.BlockSpec(memory_space=pl.ANY)

N_DEV = 8
HALO = 32
PHALO = 16
CONV_A = 3
CONV_B = 31
POOL_WINDOWS = (2, 4, 8, 16)
LANES = 128
MIB = 1024 * 1024

ADAM_LR = 0.001
ADAM_B1 = 0.9
ADAM_B2 = 0.999
ADAM_EPS = 1e-08
ADAM_WD = 0.01
ADAM_STEP = 10

TM_NT = 1024
TM_MIX = 256
TM_WIDE = 512


def _sds(shape, dtype):
    return jax.ShapeDtypeStruct(tuple(shape), dtype)


def _params(sem, vmem_mib):
    return pltpu.CompilerParams(dimension_semantics=sem, vmem_limit_bytes=vmem_mib * MIB)


def _const(shape, single=False):
    n = len(shape)
    if single:
        return pl.BlockSpec(shape, lambda *_: (0,) * n, pipeline_mode=pl.Buffered(1))
    return pl.BlockSpec(shape, lambda *_: (0,) * n)


def _sig(v):
    return jax.nn.sigmoid(v)


def _dsilu(v, s):
    return s * (1.0 + v * (1.0 - s))


def _rms(v):
    return lax.rsqrt(jnp.mean(v * v, axis=-1, keepdims=True) + EPS)


def _norm_bwd(dn, n, r):
    return r * (dn - n * jnp.mean(dn * n, axis=-1, keepdims=True))


def _colsum(v):
    return jnp.sum(v, axis=0, keepdims=True)


class _Comm:
    def __init__(self, inputs, out_shapes, sems, start, finish, aliases=None, middle=None):
        self.inputs, self.out_shapes, self.sems = list(inputs), list(out_shapes), list(sems)
        self.start, self.finish, self.middle = start, finish, middle
        self.aliases = dict(aliases or {})


def _merge(*comms):
    comms = [c for c in comms if c is not None]
    if len(comms) <= 1:
        return comms[0] if comms else None
    spans, i0, o0, s0, aliases = [], 0, 0, 0, {}
    for c in comms:
        spans.append((i0, o0, s0))
        aliases.update({i0 + k: o0 + v for k, v in c.aliases.items()})
        i0, o0, s0 = i0 + len(c.inputs), o0 + len(c.out_shapes), s0 + len(c.sems)

    def run(which):
        def fn(ins, outs, sems):
            for c, (i, o, s) in zip(comms, spans):
                hook = getattr(c, which)
                if hook is not None:
                    hook(ins[i:i + len(c.inputs)], outs[o:o + len(c.out_shapes)], sems[s:s + len(c.sems)])
        return fn

    return _Comm([a for c in comms for a in c.inputs], [a for c in comms for a in c.out_shapes],
                 [a for c in comms for a in c.sems], run("start"), run("finish"), aliases,
                 run("middle") if any(c.middle is not None for c in comms) else None)


def _call(body, *, grid, in_specs, out_specs, out_shape, operands, name, params, scratch_shapes=(), comm=None,
          prefetch=None, own_copies_first=False):
    n_p = 0 if prefetch is None else 1
    n_i, n_o, n_s = len(in_specs), len(out_specs), len(scratch_shapes)
    if comm is None:
        comm = _Comm([], [], [], None, None)
    c_i, c_o = len(comm.inputs), len(comm.out_shapes)

    def carrier(*refs):
        pre, refs = refs[:n_p], refs[n_p:]
        ins, cins = refs[:n_i], refs[n_i:n_i + c_i]
        outs = refs[n_i + c_i:n_i + c_i + n_o]
        couts = refs[n_i + c_i + n_o:n_i + c_i + n_o + c_o]
        scr = refs[n_i + c_i + n_o + c_o:n_i + c_i + n_o + c_o + n_s]
        csems = refs[n_i + c_i + n_o + c_o + n_s:]
        ids = [pl.program_id(d) for d in range(len(grid))]
        first = ids[0] == 0
        half = ids[0] == grid[0] // 2
        last = ids[0] == grid[0] - 1
        for d in range(1, len(grid)):
            first = first & (ids[d] == 0)
            half = half & (ids[d] == 0)
            last = last & (ids[d] == grid[d] - 1)

        def start():
            if comm.start is not None:
                @pl.when(first)
                def _():
                    comm.start(cins, couts, csems)

        if not own_copies_first:
            start()
        if comm.middle is not None:
            assert grid[0] >= 2

            @pl.when(half)
            def _():
                comm.middle(cins, couts, csems)

        body(*pre, *ins, *outs, *scr)
        if own_copies_first:
            start()

        if comm.finish is not None:
            @pl.when(last)
            def _():
                comm.finish(cins, couts, csems)

    specs = dict(grid=grid, in_specs=list(in_specs) + [ANY] * c_i, out_specs=list(out_specs) + [ANY] * c_o,
                 scratch_shapes=list(scratch_shapes) + comm.sems)
    if n_p:
        specs = dict(grid_spec=pltpu.PrefetchScalarGridSpec(num_scalar_prefetch=1, **specs))
    res = pl.pallas_call(
        carrier, out_shape=list(out_shape) + comm.out_shapes,
        input_output_aliases={n_p + n_i + k: n_o + v for k, v in comm.aliases.items()},
        name=name, compiler_params=params, **specs)(*(() if prefetch is None else (prefetch,)), *operands, *comm.inputs)
    return list(res[:n_o]), list(res[n_o:])


def _run_comm(comm, name):
    c_i, c_o = len(comm.inputs), len(comm.out_shapes)

    def body(*refs):
        ins, outs, sems = refs[:c_i], refs[c_i:c_i + c_o], refs[c_i + c_o:]
        comm.start(ins, outs, sems)
        comm.finish(ins, outs, sems)

    res = pl.pallas_call(
        body, in_specs=[ANY] * c_i, out_specs=[ANY] * c_o, out_shape=comm.out_shapes, scratch_shapes=comm.sems,
        input_output_aliases=comm.aliases, name=name)(*comm.inputs)
    return list(res)


def _gather_matmul(order, x, g, shard, *, tm, name, comm=None):
    S, K = x.shape
    nb = shard.shape[1]
    n_i = S // tm

    def body(order_ref, x_ref, g_ref, shard_ref, p_ref, h_ref, full_ref, hbuf, wbuf, send_sems, recv_sems, dma_sems):
        j, i = pl.program_id(0), pl.program_id(1)
        px, py, pc = _place()
        cps = _gather_copies(shard_ref, full_ref, 1, nb, send_sems, recv_sems, 0)
        own = pltpu.make_async_copy(shard_ref, _piece(full_ref, 1, nb, 4 * px + 2 * py + pc), dma_sems.at[0])

        def load(qx, qy):
            cp = pltpu.make_async_copy(_piece(full_ref, 1, 2 * nb, 2 * qx + qy), wbuf, dma_sems.at[1])
            cp.start()
            cp.wait()

        @pl.when((j == 0) & (i == 0))
        def _():
            own.start()
            for k in (0, 1, 2):
                cps[k].start()

        @pl.when(j == 0)
        def _():
            xx = x_ref[...]
            hh = ((xx * _rms(xx)) * g_ref[...]).astype(BF16)
            hbuf[i] = hh
            h_ref[...] = hh

        @pl.when((j == 0) & (i == 0))
        def _():
            own.wait()
            cps[0].wait_recv()
            load(px, py)

        @pl.when((j == 1) & (i == 0))
        def _():
            cps[1].wait_recv()
            cps[3].start()
            cps[5].start()
            cps[2].wait_recv()
            cps[4].start()
            cps[6].start()
            cps[5].wait_recv()
            load(1 - px, py)

        @pl.when((j == 2) & (i == 0))
        def _():
            cps[6].wait_recv()
            load(px, 1 - py)

        @pl.when((j == 3) & (i == 0))
        def _():
            cps[3].wait_recv()
            cps[4].wait_recv()
            cps[7].start()
            cps[7].wait_recv()
            load(1 - px, 1 - py)

        p_ref[...] = jnp.dot(hbuf[i], wbuf[...], preferred_element_type=F32).astype(BF16)

        @pl.when((j == 3) & (i == n_i - 1))
        def _():
            for cp in cps:
                cp.wait_send()

    first_pass = lambda j, i, o: (jnp.where(j == 0, i, n_i - 1), 0)
    outs, extra = _call(
        body, grid=(4, n_i), prefetch=order,
        in_specs=[pl.BlockSpec((tm, K), first_pass), pl.BlockSpec((1, K), lambda j, i, o: (0, 0)), ANY],
        out_specs=[pl.BlockSpec((tm, 2 * nb), lambda j, i, o: (i, o[j])), pl.BlockSpec((tm, K), first_pass), ANY],
        out_shape=[_sds((S, N_DEV * nb), BF16), _sds((S, K), BF16), _sds((K, N_DEV * nb), BF16)],
        operands=(x, g, shard),
        scratch_shapes=[pltpu.VMEM((n_i, tm, K), BF16), pltpu.VMEM((K, 2 * nb), BF16),
                        pltpu.SemaphoreType.DMA((N_GATHER,)), pltpu.SemaphoreType.DMA((N_GATHER,)),
                        pltpu.SemaphoreType.DMA((2,))],
        name=name, params=_params(("arbitrary", "arbitrary"), 58), comm=comm, own_copies_first=True)
    return outs[0], outs[1], outs[2], extra


def _out_norm_res(u, w, x, g, *, tm, name, comm=None):
    S, K = u.shape
    D = w.shape[1]

    def body(u_ref, w_ref, x_ref, g_ref, x1_ref, y_ref):
        y = jnp.dot(u_ref[...], w_ref[...], preferred_element_type=F32)
        y_ref[...] = y.astype(BF16)
        x1_ref[...] = x_ref[...] + (y * _rms(y)) * g_ref[...]

    return _call(
        body, grid=(S // tm,),
        in_specs=[pl.BlockSpec((tm, K), lambda i: (i, 0)), _const((K, D), single=True),
                  pl.BlockSpec((tm, D), lambda i: (i, 0)), _const((1, D))],
        out_specs=[pl.BlockSpec((tm, D), lambda i: (i, 0)), pl.BlockSpec((tm, D), lambda i: (i, 0))],
        out_shape=[_sds((S, D), F32), _sds((S, D), BF16)], operands=(u, w, x, g),
        name=name, params=_params(("arbitrary",), 56), comm=comm)


def _out_loss(yy, w, x1, g, tgt, *, tm, name):
    S, K = yy.shape
    D = w.shape[1]

    def body(yy_ref, w_ref, x1_ref, g_ref, t_ref, dout_ref, dx2_ref, dyy_ref, lcol_ref, dg_ref):
        out = jnp.dot(yy_ref[...], w_ref[...], preferred_element_type=F32)
        r = _rms(out)
        n = out * r
        gg = g_ref[...]
        e = x1_ref[...] + n * gg - t_ref[...]
        dx2 = e * (1.0 / D)
        dx2_ref[...] = dx2
        dout = _norm_bwd(dx2 * gg, n, r).astype(BF16)
        dout_ref[...] = dout
        dyy_ref[...] = lax.dot_general(dout, w_ref[...], (((1,), (1,)), ((), ())),
                                       preferred_element_type=F32).astype(BF16)

        @pl.when(pl.program_id(0) == 0)
        def _():
            lcol_ref[...] = jnp.zeros_like(lcol_ref)
            dg_ref[...] = jnp.zeros_like(dg_ref)

        lcol_ref[...] += _colsum(e * e)
        dg_ref[...] += _colsum(dx2 * n)

    return _call(
        body, grid=(S // tm,),
        in_specs=[pl.BlockSpec((tm, K), lambda i: (i, 0)), _const((K, D), single=True),
                  pl.BlockSpec((tm, D), lambda i: (i, 0)), _const((1, D)),
                  pl.BlockSpec((tm, D), lambda i: (i, 0))],
        out_specs=[pl.BlockSpec((tm, D), lambda i: (i, 0)), pl.BlockSpec((tm, D), lambda i: (i, 0)),
                   pl.BlockSpec((tm, K), lambda i: (i, 0)), _const((1, D)), _const((1, D))],
        out_shape=[_sds((S, D), BF16), _sds((S, D), F32), _sds((S, K), BF16), _sds((1, D), F32), _sds((1, D), F32)],
        operands=(yy, w, x1, g, tgt), name=name, params=_params(("arbitrary",), 52))[0]


def _mm_nt(a, w, *, tm, tk, name, comm=None):
    S, N = a.shape
    D = w.shape[0]
    n_k = N // tk

    def body(a_ref, w_ref, o_ref, acc_ref):
        k = pl.program_id(1)

        @pl.when(k == 0)
        def _():
            acc_ref[...] = jnp.zeros_like(acc_ref)

        acc_ref[...] = lax.dot_general(a_ref[...], w_ref[...], (((1,), (1,)), ((), ())),
                                       preferred_element_type=F32) + acc_ref[...]

        @pl.when(k == n_k - 1)
        def _():
            o_ref[...] = acc_ref[...].astype(BF16)

    outs, extra = _call(
        body, grid=(S // tm, n_k),
        in_specs=[pl.BlockSpec((tm, tk), lambda i, k: (i, k)), pl.BlockSpec((D, tk), lambda i, k: (0, k))],
        out_specs=[pl.BlockSpec((tm, D), lambda i, k: (i, 0))],
        out_shape=[_sds((S, D), BF16)], operands=(a, w),
        scratch_shapes=[pltpu.VMEM((tm, D), F32)],
        name=name, params=_params(("arbitrary", "arbitrary"), 48), comm=comm)
    return outs[0], extra


def _mm_tn(a, b, *, ts, tn, name, comm=None):
    S, M = a.shape
    N = b.shape[1]
    n_s = S // ts

    def body(a_ref, b_ref, o_ref, acc_ref):
        s = pl.program_id(1)

        @pl.when(s == 0)
        def _():
            acc_ref[...] = jnp.zeros_like(acc_ref)

        acc_ref[...] = lax.dot_general(a_ref[...], b_ref[...], (((0,), (0,)), ((), ())),
                                       preferred_element_type=F32) + acc_ref[...]

        @pl.when(s == n_s - 1)
        def _():
            o_ref[...] = acc_ref[...].astype(BF16)

    outs, extra = _call(
        body, grid=(N // tn, n_s),
        in_specs=[pl.BlockSpec((ts, M), lambda j, s: (s, 0)), pl.BlockSpec((ts, tn), lambda j, s: (s, j))],
        out_specs=[pl.BlockSpec((M, tn), lambda j, s: (0, j))],
        out_shape=[_sds((M, N), BF16)], operands=(a, b),
        scratch_shapes=[pltpu.VMEM((M, tn), F32)],
        name=name, params=_params(("arbitrary", "arbitrary"), 48), comm=comm)
    return outs[0], extra


def _pre_bwd_o(dh, x1, dx2, y0, g_pre, g_post, *, tm, name, comm=None):
    S, D = x1.shape

    def body(dh_ref, x1_ref, dx2_ref, y0_ref, gpre_ref, gpost_ref, dx1_ref, dy0_ref, dgpre_ref, dgpost_ref):
        @pl.when(pl.program_id(0) == 0)
        def _():
            dgpre_ref[...] = jnp.zeros_like(dgpre_ref)
            dgpost_ref[...] = jnp.zeros_like(dgpost_ref)

        dh = dh_ref[...].astype(F32)
        x1 = x1_ref[...]
        r2 = _rms(x1)
        xn = x1 * r2
        dgpre_ref[...] += _colsum(dh * xn)
        dx1 = dx2_ref[...] + _norm_bwd(dh * gpre_ref[...], xn, r2)
        dx1_ref[...] = dx1
        y = y0_ref[...].astype(F32)
        r1 = _rms(y)
        n1 = y * r1
        dgpost_ref[...] += _colsum(dx1 * n1)
        dy0_ref[...] = _norm_bwd(dx1 * gpost_ref[...], n1, r1).astype(BF16)

    row = pl.BlockSpec((tm, D), lambda i: (i, 0))
    return _call(
        body, grid=(S // tm,),
        in_specs=[row, row, row, row, _const((1, D)), _const((1, D))],
        out_specs=[row, row, _const((1, D)), _const((1, D))],
        out_shape=[_sds((S, D), F32), _sds((S, D), BF16), _sds((1, D), F32), _sds((1, D), F32)],
        operands=(dh, x1, dx2, y0, g_pre, g_post),
        name=name, params=_params(("arbitrary",), 48), comm=comm)


def _pre_bwd_e(dh, x, dx1, g_pre, *, tm, name):
    S, D = x.shape

    def body(dh_ref, x_ref, dx1_ref, gpre_ref, gx_ref, dgpre_ref):
        @pl.when(pl.program_id(0) == 0)
        def _():
            dgpre_ref[...] = jnp.zeros_like(dgpre_ref)

        dh = dh_ref[...].astype(F32)
        xx = x_ref[...]
        r0 = _rms(xx)
        xn = xx * r0
        dgpre_ref[...] += _colsum(dh * xn)
        gx_ref[...] = dx1_ref[...] + _norm_bwd(dh * gpre_ref[...], xn, r0)

    row = pl.BlockSpec((tm, D), lambda i: (i, 0))
    return _call(
        body, grid=(S // tm,),
        in_specs=[row, row, row, _const((1, D))],
        out_specs=[row, _const((1, D))],
        out_shape=[_sds((S, D), F32), _sds((1, D), F32)],
        operands=(dh, x, dx1, g_pre), name=name, params=_params(("arbitrary",), 56))[0]


SUBLANES = 8


def _shift_copies(sh_ref, ext_ref, cs):
    for b in range(1, SUBLANES):
        sh_ref[b - 1] = ext_ref[pl.ds(b, sh_ref.shape[1]), cs]


def _rows_at(ext_ref, sh_ref, off, cs, tm):
    b = off % SUBLANES
    if b == 0 or sh_ref is None:
        return ext_ref[pl.ds(off, tm), cs]
    return sh_ref[b - 1, pl.ds(off - b, tm), :]


def _taps(ext_ref, w_ref, n_taps, base, cs, tm, sh_ref=None):
    acc = _rows_at(ext_ref, sh_ref, base, cs, tm) * w_ref[0:1, cs]
    for k in range(1, n_taps):
        acc = acc + _rows_at(ext_ref, sh_ref, base + k, cs, tm) * w_ref[k:k + 1, cs]
    return acc


def _taps_rev(ext_ref, w_ref, n_taps, cs, tm, sh_ref=None):
    acc = _rows_at(ext_ref, sh_ref, n_taps - 1, cs, tm) * w_ref[0:1, cs]
    for k in range(1, n_taps):
        acc = acc + _rows_at(ext_ref, sh_ref, n_taps - 1 - k, cs, tm) * w_ref[k:k + 1, cs]
    return acc


def _e_mix_fwd(p, wa, wb, bias, ln_g, ln_b, *, tm, name, comm=None):
    S = p.shape[0]
    W = p.shape[1] // 7
    nb = tm // HALO
    chunks = [slice(c * LANES, (c + 1) * LANES) for c in range(W // LANES)]

    def body(p_ref, hax_ref, hac_ref, hbv_ref, hbg_ref, wa_ref, wb_ref, bias_ref, lg_ref, lb_ref,
             u_ref, cb_ref, ext_ref, sh_ref):
        keep = (pl.program_id(0) > 0).astype(F32)
        col = lambda j, cs: p_ref[:, j * W + cs.start:j * W + cs.stop].astype(F32)

        ext_ref[0:HALO, :] = hax_ref[...].astype(F32) * hac_ref[...].astype(F32) * keep
        ext_ref[HALO:, :] = p_ref[:, 2 * W:3 * W].astype(F32) * p_ref[:, 0:W].astype(F32)
        for cs in chunks:
            conv = _taps(ext_ref, wa_ref, CONV_A, HALO - (CONV_A - 1), cs, tm)
            az = col(3, cs)
            u_ref[:, cs] = (col(1, cs) * conv * (az * _sig(az))).astype(BF16)

        ext_ref[0:HALO, :] = hbv_ref[...].astype(F32) * _sig(hbg_ref[...].astype(F32)) * keep
        ext_ref[HALO:, :] = p_ref[:, 4 * W:5 * W].astype(F32) * _sig(p_ref[:, 5 * W:6 * W].astype(F32))
        s1 = jnp.zeros((tm, LANES), F32)
        for cs in chunks:
            _shift_copies(sh_ref, ext_ref, cs)
            cb = _taps(ext_ref, wb_ref, CONV_B, HALO - (CONV_B - 1), cs, tm, sh_ref) + bias_ref[:, cs]
            cb_ref[:, cs] = cb
            s1 = s1 + cb
        mu = jnp.sum(s1, axis=-1, keepdims=True) * (1.0 / W)
        s2 = jnp.zeros((tm, LANES), F32)
        for cs in chunks:
            xc = cb_ref[:, cs] - mu
            s2 = s2 + xc * xc
        rs = lax.rsqrt(jnp.sum(s2, axis=-1, keepdims=True) * (1.0 / W) + EPS)
        for cs in chunks:
            lb = (cb_ref[:, cs] - mu) * rs * lg_ref[:, cs] + lb_ref[:, cs]
            bz = col(6, cs)
            u_ref[:, W + cs.start:W + cs.stop] = (lb * _sig(lb) * (bz * _sig(bz))).astype(BF16)

    prev = lambda j: pl.BlockSpec((HALO, W), lambda i: (jnp.maximum(i * nb - 1, 0), j))
    return _call(
        body, grid=(S // tm,),
        in_specs=[pl.BlockSpec((tm, 7 * W), lambda i: (i, 0)), prev(0), prev(2), prev(4), prev(5),
                  _const((CONV_A, W)), _const((CONV_B, W)), _const((1, W)), _const((1, W)), _const((1, W))],
        out_specs=[pl.BlockSpec((tm, 2 * W), lambda i: (i, 0)), pl.BlockSpec((tm, W), lambda i: (i, 0))],
        out_shape=[_sds((S, 2 * W), BF16), _sds((S, W), F32)],
        operands=(p, p, p, p, p, wa, wb, bias, ln_g, ln_b),
        scratch_shapes=[pltpu.VMEM((HALO + tm, W), F32),
                        pltpu.VMEM((SUBLANES - 1, HALO + tm - SUBLANES, LANES), F32)],
        name=name, params=_params(("arbitrary",), 48), comm=comm)


def _e_mix_bwd(du, p, cb, wa, wb, ln_g, ln_b, *, tm, name, comm=None):
    S = p.shape[0]
    W = p.shape[1] // 7
    nb = tm // HALO
    n_t = S // tm
    last_blk = S // HALO - 1
    chunks = [slice(c * LANES, (c + 1) * LANES) for c in range(W // LANES)]

    def body(du_ref, duf_ref, p_ref, fab_ref, faz_ref, fbz_ref, hax_ref, hac_ref, hbv_ref, hbg_ref,
             cb_ref, cbf_ref, wa_ref, wb_ref, lg_ref, lb_ref,
             dp_ref, dwa_ref, dwb_ref, dbias_ref, dlg_ref, dlb_ref, extd_ref, extg_ref, shd_ref, shg_ref):
        i = pl.program_id(0)
        keep_prev = (i > 0).astype(F32)
        keep_next = (i < n_t - 1).astype(F32)
        col = lambda j, cs: p_ref[:, j * W + cs.start:j * W + cs.stop].astype(F32)

        @pl.when(i == 0)
        def _():
            dwa_ref[...] = jnp.zeros_like(dwa_ref)
            dwb_ref[...] = jnp.zeros_like(dwb_ref)
            dbias_ref[...] = jnp.zeros_like(dbias_ref)
            dlg_ref[...] = jnp.zeros_like(dlg_ref)
            dlb_ref[...] = jnp.zeros_like(dlb_ref)

        def dcb_rows(rows, cb_rows_ref, dub, bz_of, dst0, scale, main):
            cbv = cb_rows_ref[...]
            mu = jnp.mean(cbv, axis=-1, keepdims=True)
            xc = cbv - mu
            rs = lax.rsqrt(jnp.mean(xc * xc, axis=-1, keepdims=True) + EPS)
            m1 = jnp.zeros((rows, LANES), F32)
            m2 = jnp.zeros((rows, LANES), F32)
            for cs in chunks:
                nbv = (cb_rows_ref[:, cs] - mu) * rs
                lb = nbv * lg_ref[:, cs] + lb_ref[:, cs]
                sl = _sig(lb)
                bz = bz_of(cs)
                sz = _sig(bz)
                dub_c = dub(cs)
                dlb = dub_c * (bz * sz) * _dsilu(lb, sl)
                if main:
                    dlg_ref[:, cs] += _colsum(dlb * nbv)
                    dlb_ref[:, cs] += _colsum(dlb)
                    dp_ref[:, 6 * W + cs.start:6 * W + cs.stop] = (dub_c * (lb * sl) * _dsilu(bz, sz)).astype(BF16)
                dnb = dlb * lg_ref[:, cs]
                extd_ref[dst0:dst0 + rows, cs] = dnb
                m1 = m1 + dnb
                m2 = m2 + dnb * nbv
            m1 = jnp.sum(m1, axis=-1, keepdims=True) * (1.0 / W)
            m2 = jnp.sum(m2, axis=-1, keepdims=True) * (1.0 / W)
            for cs in chunks:
                nbv = (cb_rows_ref[:, cs] - mu) * rs
                dcb = rs * (extd_ref[dst0:dst0 + rows, cs] - m1 - nbv * m2) * scale
                extd_ref[dst0:dst0 + rows, cs] = dcb
                if main:
                    dbias_ref[:, cs] += _colsum(dcb)

        dcb_rows(tm, cb_ref, lambda cs: du_ref[:, W + cs.start:W + cs.stop].astype(F32),
                 lambda cs: col(6, cs), 0, 1.0, True)
        dcb_rows(HALO, cbf_ref, lambda cs: duf_ref[:, W + cs.start:W + cs.stop].astype(F32),
                 lambda cs: fbz_ref[:, cs].astype(F32), tm, keep_next, False)

        extg_ref[0:HALO, :] = hbv_ref[...].astype(F32) * _sig(hbg_ref[...].astype(F32)) * keep_prev
        extg_ref[HALO:, :] = p_ref[:, 4 * W:5 * W].astype(F32) * _sig(p_ref[:, 5 * W:6 * W].astype(F32))
        base_b = HALO - (CONV_B - 1)
        for cs in chunks:
            _shift_copies(shd_ref, extd_ref, cs)
            _shift_copies(shg_ref, extg_ref, cs)
            dgb = _taps_rev(extd_ref, wb_ref, CONV_B, cs, tm, shd_ref)
            bv = col(4, cs)
            sg = _sig(col(5, cs))
            dp_ref[:, 4 * W + cs.start:4 * W + cs.stop] = (dgb * sg).astype(BF16)
            dp_ref[:, 5 * W + cs.start:5 * W + cs.stop] = (dgb * bv * sg * (1.0 - sg)).astype(BF16)
            dcb = extd_ref[0:tm, cs]
            for k in range(CONV_B):
                dwb_ref[k:k + 1, cs] += _colsum(dcb * _rows_at(extg_ref, shg_ref, base_b + k, cs, tm))

        extg_ref[0:HALO, :] = hax_ref[...].astype(F32) * hac_ref[...].astype(F32) * keep_prev
        extg_ref[HALO:, :] = p_ref[:, 2 * W:3 * W].astype(F32) * p_ref[:, 0:W].astype(F32)
        base_a = HALO - (CONV_A - 1)
        for cs in chunks:
            conv = _taps(extg_ref, wa_ref, CONV_A, base_a, cs, tm)
            az = col(3, cs)
            sz = _sig(az)
            ab = col(1, cs)
            dua = du_ref[:, cs].astype(F32)
            dya = dua * (az * sz)
            dp_ref[:, W + cs.start:W + cs.stop] = (dya * conv).astype(BF16)
            dp_ref[:, 3 * W + cs.start:3 * W + cs.stop] = (dua * (ab * conv) * _dsilu(az, sz)).astype(BF16)
            extd_ref[0:tm, cs] = dya * ab
            azf = faz_ref[:, cs].astype(F32)
            extd_ref[tm:tm + HALO, cs] = (duf_ref[:, cs].astype(F32) * (azf * _sig(azf))
                                          * fab_ref[:, cs].astype(F32) * keep_next)
        for cs in chunks:
            dca = _taps_rev(extd_ref, wa_ref, CONV_A, cs, tm)
            dp_ref[:, cs] = (dca * col(2, cs)).astype(BF16)
            dp_ref[:, 2 * W + cs.start:2 * W + cs.stop] = (dca * col(0, cs)).astype(BF16)
            dconv = extd_ref[0:tm, cs]
            for k in range(CONV_A):
                dwa_ref[k:k + 1, cs] += _colsum(dconv * extg_ref[pl.ds(base_a + k, tm), cs])

    prev = lambda j: pl.BlockSpec((HALO, W), lambda i: (jnp.maximum(i * nb - 1, 0), j))
    nxt = lambda j, w: pl.BlockSpec((HALO, w), lambda i: (jnp.minimum((i + 1) * nb, last_blk), j))
    row = lambda w: pl.BlockSpec((tm, w), lambda i: (i, 0))
    return _call(
        body, grid=(n_t,),
        in_specs=[row(2 * W), nxt(0, 2 * W), row(7 * W), nxt(1, W), nxt(3, W), nxt(6, W),
                  prev(0), prev(2), prev(4), prev(5), row(W), nxt(0, W),
                  _const((CONV_A, W)), _const((CONV_B, W)), _const((1, W)), _const((1, W))],
        out_specs=[row(7 * W), _const((CONV_A, W)), _const((CONV_B, W)), _const((1, W)), _const((1, W)), _const((1, W))],
        out_shape=[_sds((S, 7 * W), BF16), _sds((CONV_A, W), F32), _sds((CONV_B, W), F32),
                   _sds((1, W), F32), _sds((1, W), F32), _sds((1, W), F32)],
        operands=(du, du, p, p, p, p, p, p, p, p, cb, cb, wa, wb, ln_g, ln_b),
        scratch_shapes=[pltpu.VMEM((tm + HALO, W), F32), pltpu.VMEM((HALO + tm, W), F32),
                        pltpu.VMEM((SUBLANES - 1, HALO + tm - SUBLANES, LANES), F32),
                        pltpu.VMEM((SUBLANES - 1, HALO + tm - SUBLANES, LANES), F32)],
        name=name, params=_params(("arbitrary",), 52), comm=comm)


def _counts(i, tm, rows, off, win):
    t = i * tm + off + lax.broadcasted_iota(jnp.int32, (rows, 1), 0)
    return jnp.minimum(t + 1, win).astype(F32)


def _o_mix_fwd(q, cw, cb, cscale, *, tm, name):
    S = q.shape[0]
    WC = q.shape[1] // 2
    NG = len(POOL_WINDOWS)
    G = WC // NG
    nb = tm // PHALO

    def body(v_ref, z_ref, hv_ref, cw_ref, cb_ref, sc_ref, yy_ref, pooled_ref, gg_ref, ext_ref):
        i = pl.program_id(0)
        keep = (i > 0).astype(F32)
        for g, win in enumerate(POOL_WINDOWS):
            cs = slice(g * G, (g + 1) * G)
            v = v_ref[:, cs].astype(F32)
            ext_ref[0:PHALO, :] = hv_ref[:, cs].astype(F32) * keep
            ext_ref[PHALO:, :] = v
            s = v
            for j in range(1, win):
                s = s + ext_ref[pl.ds(PHALO - j, tm), :]
            pooled = (s / _counts(i, tm, tm, 0, win) - v).astype(BF16)
            pooled_ref[:, cs] = pooled
            gg = jnp.dot(pooled, cw_ref[g], preferred_element_type=F32) + cb_ref[:, cs]
            gg_ref[:, cs] = gg.astype(BF16)
            z = z_ref[:, cs].astype(F32)
            yy_ref[:, cs] = (gg * sc_ref[:, cs] * (z * _sig(z))).astype(BF16)

    row = lambda j: pl.BlockSpec((tm, WC), lambda i: (i, j))
    out = pl.BlockSpec((tm, WC), lambda i: (i, 0))
    return _call(
        body, grid=(S // tm,),
        in_specs=[row(0), row(1), pl.BlockSpec((PHALO, WC), lambda i: (jnp.maximum(i * nb - 1, 0), 0)),
                  _const((NG, G, G)), _const((1, WC)), _const((1, WC))],
        out_specs=[out, out, out],
        out_shape=[_sds((S, WC), BF16)] * 3, operands=(q, q, q, cw, cb, cscale),
        scratch_shapes=[pltpu.VMEM((PHALO + tm, G), F32)],
        name=name, params=_params(("arbitrary",), 40))[0]


def _o_mix_bwd(dyy, q, gg, pooled, cw, cscale, *, tm, name):
    S = q.shape[0]
    WC = q.shape[1] // 2
    NG = len(POOL_WINDOWS)
    G = WC // NG
    nb = tm // PHALO
    n_t = S // tm
    last_blk = S // PHALO - 1
    nt = (((1,), (1,)), ((), ()))
    tn = (((0,), (0,)), ((), ()))

    def body(dyy_ref, dyyf_ref, z_ref, zf_ref, gg_ref, pooled_ref, cw_ref, sc_ref,
             dq_ref, dcw_ref, dcb_ref, dsc_ref, ext_ref):
        i = pl.program_id(0)
        keep_next = (i < n_t - 1).astype(F32)

        @pl.when(i == 0)
        def _():
            dcw_ref[...] = jnp.zeros_like(dcw_ref)
            dcb_ref[...] = jnp.zeros_like(dcb_ref)
            dsc_ref[...] = jnp.zeros_like(dsc_ref)

        for g, win in enumerate(POOL_WINDOWS):
            cs = slice(g * G, (g + 1) * G)
            sc = sc_ref[:, cs]
            z = z_ref[:, cs].astype(F32)
            sz = _sig(z)
            dyy_c = dyy_ref[:, cs].astype(F32)
            ggv = gg_ref[:, cs].astype(F32)
            dyy0 = dyy_c * (z * sz)
            dq_ref[:, WC + cs.start:WC + cs.stop] = (dyy_c * (ggv * sc) * _dsilu(z, sz)).astype(BF16)
            dgg = dyy0 * sc
            dsc_ref[:, cs] += _colsum(dyy0 * ggv)
            dcb_ref[:, cs] += _colsum(dgg)
            dgg_b = dgg.astype(BF16)
            dcw_ref[g] += lax.dot_general(pooled_ref[:, cs], dgg_b, tn, preferred_element_type=F32)
            dpool = lax.dot_general(dgg_b, cw_ref[g], nt, preferred_element_type=F32)
            zf = zf_ref[:, cs].astype(F32)
            dgg_f = (dyyf_ref[:, cs].astype(F32) * (zf * _sig(zf)) * sc * keep_next).astype(BF16)
            dpool_f = lax.dot_general(dgg_f, cw_ref[g], nt, preferred_element_type=F32)
            ext_ref[0:tm, :] = dpool / _counts(i, tm, tm, 0, win)
            ext_ref[tm:tm + PHALO, :] = dpool_f / _counts(i, tm, PHALO, tm, win)
            dv = ext_ref[0:tm, :] - dpool
            for j in range(1, win):
                dv = dv + ext_ref[pl.ds(j, tm), :]
            dq_ref[:, cs] = dv.astype(BF16)

    row = lambda: pl.BlockSpec((tm, WC), lambda i: (i, 0))
    nxt = lambda j: pl.BlockSpec((PHALO, WC), lambda i: (jnp.minimum((i + 1) * nb, last_blk), j))
    return _call(
        body, grid=(n_t,),
        in_specs=[row(), nxt(0), pl.BlockSpec((tm, WC), lambda i: (i, 1)), nxt(1), row(), row(),
                  _const((NG, G, G)), _const((1, WC))],
        out_specs=[pl.BlockSpec((tm, 2 * WC), lambda i: (i, 0)), _const((NG, G, G)), _const((1, WC)), _const((1, WC))],
        out_shape=[_sds((S, 2 * WC), BF16), _sds((NG, G, G), F32), _sds((1, WC), F32), _sds((1, WC), F32)],
        operands=(dyy, dyy, q, q, gg, pooled, cw, cscale),
        scratch_shapes=[pltpu.VMEM((tm + PHALO, G), F32)],
        name=name, params=_params(("arbitrary",), 48))[0]


def _place():
    return lax.axis_index("x"), lax.axis_index("y"), lax.axis_index("c")


def _piece(ref, axis, size, index):
    start = index * size
    if axis == len(ref.shape) - 1:
        start = pl.multiple_of(start, LANES)
    idx = [slice(None)] * len(ref.shape)
    idx[axis] = pl.ds(start, size)
    return ref.at[tuple(idx)]


def _gather_copies(src, out, axis, size, send_sems, recv_sems, base, held=None):
    x, y, c = _place()
    sib, xn, yn = (x, y, 1 - c), (1 - x, y, c), (x, 1 - y, c)

    def blk(px, py, of=out):
        return _piece(of, axis, size, 4 * px + 2 * py + c)

    def half(ref, h):
        n = ref.shape[0] // 2
        return ref.at[pl.ds(h * n, n)]

    def rc(k, s, d, to):
        return pltpu.make_async_remote_copy(src_ref=s, dst_ref=d, send_sem=send_sems.at[base + k],
                                            recv_sem=recv_sems.at[base + k], device_id=to, device_id_type=MESH)

    own, xb, yb, db = blk(x, y), blk(1 - x, y), blk(x, 1 - y), blk(1 - x, 1 - y)
    got = out if held is None else held
    xs, ys, ds = blk(1 - x, y, got), blk(x, 1 - y, got), blk(1 - x, 1 - y, got)
    return [rc(0, src, own, sib), rc(1, src, own, xn), rc(2, src, own, yn),
            rc(3, half(xs, 0), half(xb, 0), yn), rc(4, half(ys, 1), half(yb, 1), xn),
            rc(5, xs, xb, sib), rc(6, ys, yb, sib), rc(7, ds, db, sib)]


N_GATHER = 8


def _gather_comm(shards, axes, phases):
    n = len(shards)
    if phases == "second":
        sizes = [s.shape[a] // N_DEV for s, a in zip(shards, axes)]
        full = [_sds(s.shape, s.dtype) for s in shards]
    else:
        sizes = [s.shape[a] for s, a in zip(shards, axes)]
        full = [_sds(s.shape[:a] + (N_DEV * s.shape[a],) + s.shape[a + 1:], s.dtype) for s, a in zip(shards, axes)]

    def plan(ins, outs, sems):
        x, y, c = _place()
        me = 4 * x + 2 * y + c
        if phases == "second":
            cps = [_gather_copies(_piece(ins[t], axes[t], sizes[t], me), outs[t], axes[t], sizes[t], sems[0], sems[1],
                                  N_GATHER * t, ins[t]) for t in range(n)]
        else:
            cps = [_gather_copies(ins[t], outs[t], axes[t], sizes[t], sems[0], sems[1], N_GATHER * t)
                   for t in range(n)]
        mine = [pltpu.make_async_copy(ins[t], _piece(outs[t], axes[t], sizes[t], me), sems[2].at[t])
                for t in range(n)] if phases != "second" else []
        return cps, mine

    def send_own(ins, outs, sems):
        cps, mine = plan(ins, outs, sems)
        for t in range(n):
            mine[t].start()
            for k in (0, 1, 2):
                cps[t][k].start()

    def pass_on(ins, outs, sems):
        cps, _ = plan(ins, outs, sems)
        for t in range(n):
            if phases == "all":
                cps[t][1].wait_recv()
            cps[t][3].start()
            cps[t][5].start()
        for t in range(n):
            if phases == "all":
                cps[t][2].wait_recv()
            cps[t][4].start()
            cps[t][6].start()

    def own_landed(ins, outs, sems):
        cps, mine = plan(ins, outs, sems)
        for t in range(n):
            for k in (0, 1, 2):
                cps[t][k].wait()
            mine[t].wait()

    def all_landed(ins, outs, sems):
        cps, mine = plan(ins, outs, sems)
        for t in range(n):
            cps[t][3].wait_recv()
            cps[t][4].wait_recv()
            cps[t][7].start()
        for t in range(n):
            for k in ((0, 5, 6, 7) if phases == "all" else (5, 6, 7)):
                cps[t][k].wait_recv()
            for k in (range(N_GATHER) if phases == "all" else range(3, N_GATHER)):
                cps[t][k].wait_send()
            if phases == "all":
                mine[t].wait()

    sems = [pltpu.SemaphoreType.DMA((N_GATHER * n,)), pltpu.SemaphoreType.DMA((N_GATHER * n,))]
    if phases != "second":
        sems.append(pltpu.SemaphoreType.DMA((n,)))
    if phases == "all":
        return _Comm(shards, full, sems, send_own, all_landed, middle=pass_on)
    if phases == "first":
        return _Comm(shards, full, sems, send_own, own_landed)
    return _Comm(shards, full, sems, pass_on, all_landed, aliases={t: t for t in range(n)})


def _pair_comm(grads, axes, sizes):
    n = len(grads)
    outs_sds = [_sds((4,) + g.shape[:a] + (s,) + g.shape[a + 1:], g.dtype) for g, a, s in zip(grads, axes, sizes)]

    def copies(ins, outs, sems):
        send_sems, recv_sems = sems
        x, y, c = _place()
        return [pltpu.make_async_remote_copy(
            src_ref=_piece(ins[t], axes[t], sizes[t], 2 * qi + (1 - c)), dst_ref=outs[t].at[qi],
            send_sem=send_sems.at[4 * t + qi], recv_sem=recv_sems.at[4 * t + qi],
            device_id=(x, y, 1 - c), device_id_type=MESH) for t in range(n) for qi in range(4)]

    def start(ins, outs, sems):
        for cp in copies(ins, outs, sems):
            cp.start()

    def finish(ins, outs, sems):
        for cp in copies(ins, outs, sems):
            cp.wait()

    sems = [pltpu.SemaphoreType.DMA((4 * n,)), pltpu.SemaphoreType.DMA((4 * n,))]
    return _Comm(grads, outs_sds, sems, start, finish)


def _chip_comm(sums):
    n = len(sums)
    outs_sds = [_sds((3,) + s.shape[1:], s.dtype) for s in sums]

    def copies(ins, outs, sems):
        send_sems, recv_sems = sems
        x, y, c = _place()
        return [pltpu.make_async_remote_copy(
            src_ref=ins[t].at[2 * qx + qy], dst_ref=outs[t].at[j],
            send_sem=send_sems.at[3 * t + j], recv_sem=recv_sems.at[3 * t + j],
            device_id=(qx, qy, c), device_id_type=MESH)
            for t in range(n) for j, (qx, qy) in enumerate([(1 - x, y), (x, 1 - y), (1 - x, 1 - y)])]

    def start(ins, outs, sems):
        for cp in copies(ins, outs, sems):
            cp.start()

    def finish(ins, outs, sems):
        for cp in copies(ins, outs, sems):
            cp.wait()

    sems = [pltpu.SemaphoreType.DMA((3 * n,)), pltpu.SemaphoreType.DMA((3 * n,))]
    return _Comm(sums, outs_sds, sems, start, finish)


def _small_comm(small):
    def copies(ins, outs, sems):
        send_sems, recv_sems, local_sem = sems
        x, y, c = _place()
        mine = outs[0].at[4 * x + 2 * y + c]
        out = [pltpu.make_async_copy(ins[0], mine, local_sem.at[0])]
        for k in range(1, N_DEV):
            peer = (1 - x if k & 4 else x, 1 - y if k & 2 else y, 1 - c if k & 1 else c)
            out.append(pltpu.make_async_remote_copy(
                src_ref=ins[0], dst_ref=mine, send_sem=send_sems.at[k - 1], recv_sem=recv_sems.at[k - 1],
                device_id=peer, device_id_type=MESH))
        return out

    def start(ins, outs, sems):
        for cp in copies(ins, outs, sems):
            cp.start()

    def finish(ins, outs, sems):
        for cp in copies(ins, outs, sems):
            cp.wait()

    sems = [pltpu.SemaphoreType.DMA((N_DEV - 1,)), pltpu.SemaphoreType.DMA((N_DEV - 1,)), pltpu.SemaphoreType.DMA((1,))]
    return _Comm([small], [_sds((N_DEV,) + small.shape, small.dtype)], sems, start, finish)


def _small_scatter_comm(send):
    def copies(ins, outs, sems):
        send_sems, recv_sems, local_sem = sems
        x, y, c = _place()
        me = 4 * x + 2 * y + c
        out = [pltpu.make_async_copy(ins[0].at[me], outs[0].at[me], local_sem.at[0])]
        for k in range(1, N_DEV):
            px, py, pc = (1 - x if k & 4 else x, 1 - y if k & 2 else y, 1 - c if k & 1 else c)
            out.append(pltpu.make_async_remote_copy(
                src_ref=ins[0].at[4 * px + 2 * py + pc], dst_ref=outs[0].at[me], send_sem=send_sems.at[k - 1],
                recv_sem=recv_sems.at[k - 1], device_id=(px, py, pc), device_id_type=MESH))
        return out

    def start(ins, outs, sems):
        for cp in copies(ins, outs, sems):
            cp.start()

    def finish(ins, outs, sems):
        for cp in copies(ins, outs, sems):
            cp.wait()

    sems = [pltpu.SemaphoreType.DMA((N_DEV - 1,)), pltpu.SemaphoreType.DMA((N_DEV - 1,)), pltpu.SemaphoreType.DMA((1,))]
    return _Comm([send], [_sds(send.shape, send.dtype)], sems, start, finish)


def _pair_sum(c_idx, grad, recv, axis, size, split, *, name):
    nd = len(grad.shape)
    piece = grad.shape[:axis] + (size,) + grad.shape[axis + 1:]
    blk = (piece[0] // split,) + piece[1:]

    def g_map(q, r, c_ref):
        idx = [0] * nd
        idx[axis] = 2 * q + c_ref[0]
        idx[0] = idx[0] * split + r if axis == 0 else r
        return tuple(idx)

    def r_map(q, r, c_ref):
        return (q, r) + (0,) * (nd - 1)

    def body(c_ref, g_ref, r_ref, o_ref):
        o_ref[0] = (g_ref[...].astype(F32) + r_ref[0].astype(F32)).astype(BF16)

    return _call(
        body, grid=(4, split), prefetch=c_idx,
        in_specs=[pl.BlockSpec(blk, g_map), pl.BlockSpec((1,) + blk, r_map)],
        out_specs=[pl.BlockSpec((1,) + blk, r_map)], out_shape=[_sds((4,) + piece, BF16)],
        operands=(grad, recv), name=name, params=_params(("arbitrary", "arbitrary"), 32))[0][0]


def _adam_math(w, g, m, v):
    m = ADAM_B1 * m + (1.0 - ADAM_B1) * g
    v = ADAM_B2 * v + (1.0 - ADAM_B2) * (g * g)
    m_hat = m / (1.0 - ADAM_B1 ** ADAM_STEP)
    v_hat = v / (1.0 - ADAM_B2 ** ADAM_STEP)
    delta = -ADAM_LR * (m_hat / (jnp.sqrt(v_hat) + ADAM_EPS) + ADAM_WD * w)
    return delta, m, v


def _adam_big(q_idx, sums, recv, w, m, v, split, *, name, comm=None):
    shape = w.shape
    nd = len(shape)
    blk = (shape[0] // split,) + shape[1:]
    w_map = lambda r, q_ref: (r,) + (0,) * (nd - 1)
    s_map = lambda r, q_ref: (q_ref[0], r) + (0,) * (nd - 1)
    r_map = lambda r, q_ref: (0, r) + (0,) * (nd - 1)

    def body(q_ref, s_ref, r_ref, w_ref, m_ref, v_ref, g_ref, d_ref, nm_ref, nv_ref):
        g = s_ref[0].astype(F32) + r_ref[0].astype(F32) + r_ref[1].astype(F32) + r_ref[2].astype(F32)
        g_ref[...] = g
        d_ref[...], nm_ref[...], nv_ref[...] = _adam_math(w_ref[...], g, m_ref[...], v_ref[...])

    wspec = pl.BlockSpec(blk, w_map)
    return _call(
        body, grid=(split,), prefetch=q_idx,
        in_specs=[pl.BlockSpec((1,) + blk, s_map), pl.BlockSpec((3,) + blk, r_map), wspec, wspec, wspec],
        out_specs=[wspec] * 4, out_shape=[_sds(shape, F32)] * 4, operands=(sums, recv, w, m, v),
        name=name, params=_params(("arbitrary",), 32), comm=comm)


def _adam_small(parts, w, m, v, *, name):
    R = w.shape[0]

    def body(p_ref, w_ref, m_ref, v_ref, g_ref, d_ref, nm_ref, nv_ref):
        g = p_ref[0]
        for d in range(1, N_DEV):
            g = g + p_ref[d]
        g_ref[...] = g
        d_ref[...], nm_ref[...], nv_ref[...] = _adam_math(w_ref[...], g, m_ref[...], v_ref[...])

    whole = _const((R, LANES))
    return _call(
        body, grid=(1,), in_specs=[_const((N_DEV, R, LANES)), whole, whole, whole], out_specs=[whole] * 4,
        out_shape=[_sds((R, LANES), F32)] * 4, operands=(parts, w, m, v), name=name,
        params=_params(("arbitrary",), 32))[0]


def _pack(arrs):
    return jnp.concatenate([a.reshape(-1) for a in arrs]).reshape(-1, LANES)


def _unpack(packed, shapes):
    flat = packed.reshape(-1)
    out, off = [], 0
    for s in shapes:
        n = 1
        for d in s:
            n *= d
        out.append(flat[off:off + n].reshape(s))
        off += n
    return out


BIG = ("e_in", "e_out", "o_in", "o_cw", "o_out")
BIG_AXIS = dict(e_in=1, e_out=0, o_in=1, o_cw=1, o_out=0)
BIG_SPLIT = dict(e_in=8, e_out=4, o_in=4, o_cw=4, o_out=4)
REPLICATED = ("e_norm_pre", "e_norm_post", "e_b_conv_bias", "e_b_ln_g", "e_b_ln_b")
SHARDED = ("e_a_conv", "e_b_conv", "o_norm_pre", "o_norm_post", "o_c_b", "o_c_scale")
SMALL = REPLICATED + SHARDED


class _Exchange:
    def __init__(self, shards, small, order, c_idx):
        self.shards = shards
        self.small = small
        self.order = order
        self.c_idx = c_idx
        self.reduced = {}

    def gather(self, keys):
        return _gather_comm([self.shards[k] for k in keys], [BIG_AXIS[k] for k in keys], "all")

    def gather1(self, keys):
        return _gather_comm([self.shards[k] for k in keys], [BIG_AXIS[k] for k in keys], "first")

    def gather2(self, keys, firsts):
        return _gather_comm(firsts, [BIG_AXIS[k] for k in keys], "second")

    def pair(self, grads):
        keys = list(grads)
        return _pair_comm([grads[k] for k in keys], [BIG_AXIS[k] for k in keys],
                          [grads[k].shape[BIG_AXIS[k]] // N_DEV for k in keys])

    def pair_sums(self, grads, received):
        return {k: _pair_sum(self.c_idx, grads[k], r, BIG_AXIS[k], grads[k].shape[BIG_AXIS[k]] // N_DEV,
                             BIG_SPLIT[k], name="pair_sum_" + k) for k, r in zip(grads, received)}

    def chips(self, sums):
        return _chip_comm([sums[k] for k in sums])

    def done(self, sums, received):
        self.reduced.update({k: (sums[k], r) for k, r in zip(sums, received)})


def _local_step(x, tgt, w_small, ex):
    S, D = x.shape
    tnt, tx, tw = min(TM_NT, S), min(TM_MIX, S), min(TM_WIDE, S)

    wt = {}
    p, h0, wt["e_in"], got = _gather_matmul(ex.order, x, w_small["e_norm_pre"], ex.shards["e_in"], tm=tw,
                                            name="e_in_fwd", comm=_small_comm(ex.small))
    per_dev = [_unpack(got[0][d], [w_small[k].shape for k in SHARDED]) for d in range(N_DEV)]
    sm = {k: w_small[k] for k in REPLICATED}
    for j, k in enumerate(SHARDED):
        sm[k] = jnp.concatenate([per_dev[d][j] for d in range(N_DEV)], axis=-1)
    n_groups = sm["o_c_b"].shape[0]
    sm["o_c_b"] = sm["o_c_b"].reshape(1, -1)

    W = p.shape[1] // 7
    (u, cb), got = _e_mix_fwd(p, sm["e_a_conv"], sm["e_b_conv"], sm["e_b_conv_bias"], sm["e_b_ln_g"],
                              sm["e_b_ln_b"], tm=tx, name="e_mix_fwd", comm=ex.gather(["e_out"]))
    wt["e_out"] = got[0]
    late = ["o_out", "o_cw"]
    (x1, y0), part = _out_norm_res(u, wt["e_out"], x, sm["e_norm_post"], tm=tw, name="e_out_fwd",
                                   comm=ex.gather1(late))
    q, h1, wt["o_in"], got = _gather_matmul(ex.order, x1, sm["o_norm_pre"], ex.shards["o_in"], tm=tw,
                                            name="o_in_fwd", comm=ex.gather2(late, part))
    wt.update(zip(late, got))
    yy, pooled, gg = _o_mix_fwd(q, wt["o_cw"], sm["o_c_b"], sm["o_c_scale"], tm=tw, name="o_mix_fwd")
    dout, dx2, dyy, lcol, dg_o_post = _out_loss(yy, wt["o_out"], x1, sm["o_norm_post"], tgt, tm=tx, name="o_out_loss")
    loss = (0.5 / D) * jnp.sum(lcol)

    dq, d_cw, d_cb, d_cscale = _o_mix_bwd(dyy, q, gg, pooled, wt["o_cw"], sm["o_c_scale"], tm=tw, name="o_mix_bwd")
    g_o_out, _ = _mm_tn(yy, dout, ts=tnt, tn=W, name="o_out_dw")
    ga = dict(o_out=g_o_out, o_cw=d_cw.astype(BF16))
    dh1, ra = _mm_nt(dq, wt["o_in"], tm=tnt, tk=W, name="o_in_bwd", comm=ex.pair(ga))
    sa = ex.pair_sums(ga, ra)
    (dx1, dy0, dg_o_pre, dg_e_post), ra = _pre_bwd_o(dh1, x1, dx2, y0, sm["o_norm_pre"], sm["e_norm_post"],
                                                     tm=tx, name="o_pre_bwd", comm=ex.chips(sa))
    ex.done(sa, ra)
    g_o_in, _ = _mm_tn(h1, dq, ts=tnt, tn=W, name="o_in_dw")
    gb = dict(o_in=g_o_in)
    du, rb = _mm_nt(dy0, wt["e_out"], tm=tnt, tk=W, name="e_out_bwd", comm=ex.pair(gb))
    sb = ex.pair_sums(gb, rb)
    g_e_out, _ = _mm_tn(u, dy0, ts=tnt, tn=W, name="e_out_dw")
    gc = dict(e_out=g_e_out)
    (dp, d_wa, d_wb, d_bias, d_lg, d_lb), rbc = _e_mix_bwd(
        du, p, cb, sm["e_a_conv"], sm["e_b_conv"], sm["e_b_ln_g"], sm["e_b_ln_b"], tm=tx, name="e_mix_bwd",
        comm=_merge(ex.chips(sb), ex.pair(gc)))
    ex.done(sb, rbc[:1])
    sc = ex.pair_sums(gc, rbc[1:])
    g_e_in, rc = _mm_tn(h0, dp, ts=tnt, tn=W, name="e_in_dw", comm=ex.chips(sc))
    ex.done(sc, rc)
    gd = dict(e_in=g_e_in)
    sd = ex.pair_sums(gd, _run_comm(ex.pair(gd), "pair_e_in"))
    dh0, rd = _mm_nt(dp, wt["e_in"], tm=tnt, tk=W, name="e_in_bwd", comm=ex.chips(sd))
    ex.done(sd, rd)
    grad_x, dg_e_pre = _pre_bwd_e(dh0, x, dx1, sm["e_norm_pre"], tm=tw, name="e_pre_bwd")

    small = dict(e_norm_pre=dg_e_pre, e_norm_post=dg_e_post, e_a_conv=d_wa, e_b_conv=d_wb, e_b_conv_bias=d_bias,
                 e_b_ln_g=d_lg, e_b_ln_b=d_lb, o_norm_pre=dg_o_pre, o_norm_post=dg_o_post,
                 o_c_b=d_cb.reshape(n_groups, -1), o_c_scale=d_cscale)
    return loss, grad_x, small


def kernel(x, e_norm_pre, e_norm_post, e_w_in, e_a_conv, e_b_conv, e_b_conv_bias, e_b_ln_g, e_b_ln_b, e_w_out, o_norm_pre, o_norm_post, o_w_in, o_c_w, o_c_b, o_c_scale, o_w_out, loss_target, m_e_norm_pre, m_e_norm_post, m_e_w_in, m_e_a_conv, m_e_b_conv, m_e_b_conv_bias, m_e_b_ln_g, m_e_b_ln_b, m_e_w_out, m_o_norm_pre, m_o_norm_post, m_o_w_in, m_o_c_w, m_o_c_b, m_o_c_scale, m_o_w_out, v_e_norm_pre, v_e_norm_post, v_e_w_in, v_e_a_conv, v_e_b_conv, v_e_b_conv_bias, v_e_b_ln_g, v_e_b_ln_b, v_e_w_out, v_o_norm_pre, v_o_norm_post, v_o_w_in, v_o_c_w, v_o_c_b, v_o_c_scale, v_o_w_out):
    xi, yi, ci = _place()
    w_big = dict(e_in=e_w_in[0], e_out=e_w_out[0], o_in=o_w_in[0], o_cw=o_c_w[0], o_out=o_w_out[0])
    m_big = dict(e_in=m_e_w_in[0], e_out=m_e_w_out[0], o_in=m_o_w_in[0], o_cw=m_o_c_w[0], o_out=m_o_w_out[0])
    v_big = dict(e_in=v_e_w_in[0], e_out=v_e_w_out[0], o_in=v_o_w_in[0], o_cw=v_o_c_w[0], o_out=v_o_w_out[0])
    w_small = dict(e_norm_pre=e_norm_pre, e_norm_post=e_norm_post, e_b_conv_bias=e_b_conv_bias, e_b_ln_g=e_b_ln_g,
                   e_b_ln_b=e_b_ln_b, e_a_conv=e_a_conv[0], e_b_conv=e_b_conv[0], o_norm_pre=o_norm_pre,
                   o_norm_post=o_norm_post, o_c_b=o_c_b[0], o_c_scale=o_c_scale)
    m_small = dict(e_norm_pre=m_e_norm_pre, e_norm_post=m_e_norm_post, e_b_conv_bias=m_e_b_conv_bias,
                   e_b_ln_g=m_e_b_ln_g, e_b_ln_b=m_e_b_ln_b, e_a_conv=m_e_a_conv[0], e_b_conv=m_e_b_conv[0],
                   o_norm_pre=m_o_norm_pre, o_norm_post=m_o_norm_post, o_c_b=m_o_c_b[0], o_c_scale=m_o_c_scale)
    v_small = dict(e_norm_pre=v_e_norm_pre, e_norm_post=v_e_norm_post, e_b_conv_bias=v_e_b_conv_bias,
                   e_b_ln_g=v_e_b_ln_g, e_b_ln_b=v_e_b_ln_b, e_a_conv=v_e_a_conv[0], e_b_conv=v_e_b_conv[0],
                   o_norm_pre=v_o_norm_pre, o_norm_post=v_o_norm_post, o_c_b=v_o_c_b[0], o_c_scale=v_o_c_scale)

    c_idx = jnp.reshape(ci, (1,)).astype(jnp.int32)
    order = jnp.stack([2 * xi + yi, 2 * (1 - xi) + yi, 2 * xi + (1 - yi), 2 * (1 - xi) + (1 - yi)]).astype(jnp.int32)
    ex = _Exchange({k: w_big[k].astype(BF16) for k in BIG}, _pack([w_small[k] for k in SHARDED]), order, c_idx)
    loss, grad_x, g_small = _local_step(x[0], loss_target[0], w_small, ex)

    q_idx = jnp.reshape(2 * xi + yi, (1,)).astype(jnp.int32)
    big_out = {k: _adam_big(q_idx, *ex.reduced[k], w_big[k], m_big[k], v_big[k], BIG_SPLIT[k], name="adam_" + k)[0]
               for k in BIG}

    rep = _pack([g_small[k] for k in REPLICATED])
    loss_row = jnp.pad(jnp.reshape(loss, (1, 1)), ((0, 0), (0, LANES - 1)))
    blocks = []
    for k in SHARDED:
        r, n = w_small[k].shape
        blocks.append(g_small[k].reshape(r, N_DEV, n).transpose(1, 0, 2).reshape(N_DEV, r * n))
    blocks = jnp.concatenate(blocks, axis=1).reshape(N_DEV, -1, LANES)
    head = jnp.concatenate([rep, loss_row], axis=0)
    send = jnp.concatenate([jnp.broadcast_to(head[None], (N_DEV,) + head.shape), blocks], axis=1)
    parts = _run_comm(_small_scatter_comm(send), "small_grad_exchange")[0]

    def own_rows(d):
        return jnp.concatenate([_pack([d[k] for k in REPLICATED]), jnp.ones((1, LANES), F32),
                                _pack([d[k] for k in SHARDED])], axis=0)

    res_small = _adam_small(parts, own_rows(w_small), own_rows(m_small), own_rows(v_small), name="adam_small")
    n_rep = rep.shape[0]
    loss = res_small[0][n_rep, 0]
    small_out = {k: [] for k in SMALL}
    for packed in res_small:
        for k, t in zip(REPLICATED, _unpack(packed[:n_rep], [w_small[k].shape for k in REPLICATED])):
            small_out[k].append(t)
        for k, t in zip(SHARDED, _unpack(packed[n_rep + 1:], [w_small[k].shape for k in SHARDED])):
            small_out[k].append(t)

    big_of = dict(e_w_in="e_in", e_w_out="e_out", o_w_in="o_in", o_c_w="o_cw", o_w_out="o_out")
    stacked = ("e_a_conv", "e_b_conv", "o_c_b")

    def leaf(name, which):
        if name in big_of:
            return big_out[big_of[name]][which][None]
        t = small_out[name][which]
        return t[None] if name in stacked else t

    order = ("e_norm_pre", "e_norm_post", "e_w_in", "e_a_conv", "e_b_conv", "e_b_conv_bias", "e_b_ln_g", "e_b_ln_b",
             "e_w_out", "o_norm_pre", "o_norm_post", "o_w_in", "o_c_w", "o_c_b", "o_c_scale", "o_w_out")
    outs = [loss, grad_x[None]]
    for which in range(4):
        outs += [leaf(nm, which) for nm in order]
    return tuple(outs)
```

```python
import jax
import jax.numpy as jnp
from jax import lax
from jax.experimental import pallas as pl
from jax.experimental.pallas import tpu as pltpu

F32 = jnp.float32
BF16 = jnp.bfloat16
EPS = 1e-6
MESH = pl.DeviceIdType.MESH
ANY = pl.BlockSpec(memory_space=pl.ANY)

N_DEV = 8
HALO = 32
PHALO = 16
CONV_A = 3
CONV_B = 31
POOL_WINDOWS = (2, 4, 8, 16)
LANES = 128
MIB = 1024 * 1024

ADAM_LR = 0.001
ADAM_B1 = 0.9
ADAM_B2 = 0.999
ADAM_EPS = 1e-08
ADAM_WD = 0.01
ADAM_STEP = 10

TM_NT = 1024
TM_MIX = 256
TM_WIDE = 512


def _sds(shape, dtype):
    return jax.ShapeDtypeStruct(tuple(shape), dtype)


def _params(sem, vmem_mib):
    return pltpu.CompilerParams(dimension_semantics=sem, vmem_limit_bytes=vmem_mib * MIB)


def _const(shape, single=False):
    n = len(shape)
    if single:
        return pl.BlockSpec(shape, lambda *_: (0,) * n, pipeline_mode=pl.Buffered(1))
    return pl.BlockSpec(shape, lambda *_: (0,) * n)


def _sig(v):
    return jax.nn.sigmoid(v)


def _dsilu(v, s):
    return s * (1.0 + v * (1.0 - s))


def _rms(v):
    return lax.rsqrt(jnp.mean(v * v, axis=-1, keepdims=True) + EPS)


def _norm_bwd(dn, n, r):
    return r * (dn - n * jnp.mean(dn * n, axis=-1, keepdims=True))


def _colsum(v):
    return jnp.sum(v, axis=0, keepdims=True)


class _Comm:
    def __init__(self, inputs, out_shapes, sems, start, finish, aliases=None, middle=None):
        self.inputs, self.out_shapes, self.sems = list(inputs), list(out_shapes), list(sems)
        self.start, self.finish, self.middle = start, finish, middle
        self.aliases = dict(aliases or {})


def _merge(*comms):
    comms = [c for c in comms if c is not None]
    if len(comms) <= 1:
        return comms[0] if comms else None
    spans, i0, o0, s0, aliases = [], 0, 0, 0, {}
    for c in comms:
        spans.append((i0, o0, s0))
        aliases.update({i0 + k: o0 + v for k, v in c.aliases.items()})
        i0, o0, s0 = i0 + len(c.inputs), o0 + len(c.out_shapes), s0 + len(c.sems)

    def run(which):
        def fn(ins, outs, sems):
            for c, (i, o, s) in zip(comms, spans):
                hook = getattr(c, which)
                if hook is not None:
                    hook(ins[i:i + len(c.inputs)], outs[o:o + len(c.out_shapes)], sems[s:s + len(c.sems)])
        return fn

    return _Comm([a for c in comms for a in c.inputs], [a for c in comms for a in c.out_shapes],
                 [a for c in comms for a in c.sems], run("start"), run("finish"), aliases,
                 run("middle") if any(c.middle is not None for c in comms) else None)


def _call(body, *, grid, in_specs, out_specs, out_shape, operands, name, params, scratch_shapes=(), comm=None,
          prefetch=None, own_copies_first=False):
    n_p = 0 if prefetch is None else 1
    n_i, n_o, n_s = len(in_specs), len(out_specs), len(scratch_shapes)
    if comm is None:
        comm = _Comm([], [], [], None, None)
    c_i, c_o = len(comm.inputs), len(comm.out_shapes)

    def carrier(*refs):
        pre, refs = refs[:n_p], refs[n_p:]
        ins, cins = refs[:n_i], refs[n_i:n_i + c_i]
        outs = refs[n_i + c_i:n_i + c_i + n_o]
        couts = refs[n_i + c_i + n_o:n_i + c_i + n_o + c_o]
        scr = refs[n_i + c_i + n_o + c_o:n_i + c_i + n_o + c_o + n_s]
        csems = refs[n_i + c_i + n_o + c_o + n_s:]
        ids = [pl.program_id(d) for d in range(len(grid))]
        first = ids[0] == 0
        half = ids[0] == grid[0] // 2
        last = ids[0] == grid[0] - 1
        for d in range(1, len(grid)):
            first = first & (ids[d] == 0)
            half = half & (ids[d] == 0)
            last = last & (ids[d] == grid[d] - 1)

        def start():
            if comm.start is not None:
                @pl.when(first)
                def _():
                    comm.start(cins, couts, csems)

        if not own_copies_first:
            start()
        if comm.middle is not None:
            assert grid[0] >= 2

            @pl.when(half)
            def _():
                comm.middle(cins, couts, csems)

        body(*pre, *ins, *outs, *scr)
        if own_copies_first:
            start()

        if comm.finish is not None:
            @pl.when(last)
            def _():
                comm.finish(cins, couts, csems)

    specs = dict(grid=grid, in_specs=list(in_specs) + [ANY] * c_i, out_specs=list(out_specs) + [ANY] * c_o,
                 scratch_shapes=list(scratch_shapes) + comm.sems)
    if n_p:
        specs = dict(grid_spec=pltpu.PrefetchScalarGridSpec(num_scalar_prefetch=1, **specs))
    res = pl.pallas_call(
        carrier, out_shape=list(out_shape) + comm.out_shapes,
        input_output_aliases={n_p + n_i + k: n_o + v for k, v in comm.aliases.items()},
        name=name, compiler_params=params, **specs)(*(() if prefetch is None else (prefetch,)), *operands, *comm.inputs)
    return list(res[:n_o]), list(res[n_o:])


def _run_comm(comm, name):
    c_i, c_o = len(comm.inputs), len(comm.out_shapes)

    def body(*refs):
        ins, outs, sems = refs[:c_i], refs[c_i:c_i + c_o], refs[c_i + c_o:]
        comm.start(ins, outs, sems)
        comm.finish(ins, outs, sems)

    res = pl.pallas_call(
        body, in_specs=[ANY] * c_i, out_specs=[ANY] * c_o, out_shape=comm.out_shapes, scratch_shapes=comm.sems,
        input_output_aliases=comm.aliases, name=name)(*comm.inputs)
    return list(res)


def _gather_matmul(order, x, g, shard, *, tm, name, comm=None):
    S, K = x.shape
    nb = shard.shape[1]
    n_i = S // tm

    def body(order_ref, x_ref, g_ref, shard_ref, p_ref, h_ref, full_ref, hbuf, wbuf, stage, send_sems, recv_sems,
             dma_sems):
        j, i = pl.program_id(0), pl.program_id(1)
        px, py, pc = _place()
        cps = _gather_copies(stage, full_ref, 1, nb, send_sems, recv_sems, 0)
        own = pltpu.make_async_copy(stage, _piece(full_ref, 1, nb, 4 * px + 2 * py + pc), dma_sems.at[0])
        keep_h = pltpu.make_async_copy(hbuf, h_ref, dma_sems.at[2])

        def load(src, dst):
            cp = pltpu.make_async_copy(src, dst, dma_sems.at[1])
            cp.start()
            cp.wait()

        def load_pair(qx, qy):
            load(_piece(full_ref, 1, 2 * nb, 2 * qx + qy), wbuf)

        @pl.when((j == 0) & (i == 0))
        def _():
            load(shard_ref, stage)
            own.start()
            for k in (0, 1, 2):
                cps[k].start()

        @pl.when(j == 0)
        def _():
            xx = x_ref[...]
            hbuf[i] = ((xx * _rms(xx)) * g_ref[...]).astype(BF16)

        @pl.when((j == 0) & (i == 0))
        def _():
            own.wait()
            cps[0].wait_recv()
            load_pair(px, py)

        @pl.when((j == 1) & (i == 0))
        def _():
            keep_h.start()
            cps[1].wait_recv()
            cps[3].start()
            cps[5].start()
            cps[2].wait_recv()
            cps[4].start()
            cps[6].start()
            cps[5].wait_recv()
            load_pair(1 - px, py)

        @pl.when((j == 2) & (i == 0))
        def _():
            cps[6].wait_recv()
            load_pair(px, 1 - py)

        @pl.when((j == 3) & (i == 0))
        def _():
            cps[3].wait_recv()
            cps[4].wait_recv()
            cps[7].start()
            cps[7].wait_recv()
            load_pair(1 - px, 1 - py)

        p_ref[...] = jnp.dot(hbuf[i], wbuf[...], preferred_element_type=F32).astype(BF16)

        @pl.when((j == 3) & (i == n_i - 1))
        def _():
            for cp in cps:
                cp.wait_send()
            keep_h.wait()

    first_pass = lambda j, i, o: (jnp.where(j == 0, i, n_i - 1), 0)
    outs, extra = _call(
        body, grid=(4, n_i), prefetch=order,
        in_specs=[pl.BlockSpec((tm, K), first_pass), pl.BlockSpec((1, K), lambda j, i, o: (0, 0)), ANY],
        out_specs=[pl.BlockSpec((tm, 2 * nb), lambda j, i, o: (i, o[j])), ANY, ANY],
        out_shape=[_sds((S, N_DEV * nb), BF16), _sds((n_i, tm, K), BF16), _sds((K, N_DEV * nb), BF16)],
        operands=(x, g, shard),
        scratch_shapes=[pltpu.VMEM((n_i, tm, K), BF16), pltpu.VMEM((K, 2 * nb), BF16), pltpu.VMEM((K, nb), BF16),
                        pltpu.SemaphoreType.DMA((N_GATHER,)), pltpu.SemaphoreType.DMA((N_GATHER,)),
                        pltpu.SemaphoreType.DMA((3,))],
        name=name, params=_params(("arbitrary", "arbitrary"), 58), comm=comm, own_copies_first=True)
    return outs[0], outs[1].reshape(S, K), outs[2], extra


def _out_norm_res(u, w, x, g, *, tm, name, comm=None):
    S, K = u.shape
    D = w.shape[1]

    def body(u_ref, w_ref, x_ref, g_ref, x1_ref, y_ref):
        y = jnp.dot(u_ref[...], w_ref[...], preferred_element_type=F32)
        y_ref[...] = y.astype(BF16)
        x1_ref[...] = x_ref[...] + (y * _rms(y)) * g_ref[...]

    return _call(
        body, grid=(S // tm,),
        in_specs=[pl.BlockSpec((tm, K), lambda i: (i, 0)), _const((K, D), single=True),
                  pl.BlockSpec((tm, D), lambda i: (i, 0)), _const((1, D))],
        out_specs=[pl.BlockSpec((tm, D), lambda i: (i, 0)), pl.BlockSpec((tm, D), lambda i: (i, 0))],
        out_shape=[_sds((S, D), F32), _sds((S, D), BF16)], operands=(u, w, x, g),
        name=name, params=_params(("arbitrary",), 56), comm=comm)


def _out_loss(yy, w, x1, g, tgt, *, tm, name):
    S, K = yy.shape
    D = w.shape[1]

    def body(yy_ref, w_ref, x1_ref, g_ref, t_ref, dout_ref, dx2_ref, dyy_ref, lcol_ref, dg_ref):
        out = jnp.dot(yy_ref[...], w_ref[...], preferred_element_type=F32)
        r = _rms(out)
        n = out * r
        gg = g_ref[...]
        e = x1_ref[...] + n * gg - t_ref[...]
        dx2 = e * (1.0 / D)
        dx2_ref[...] = dx2
        dout = _norm_bwd(dx2 * gg, n, r).astype(BF16)
        dout_ref[...] = dout
        dyy_ref[...] = lax.dot_general(dout, w_ref[...], (((1,), (1,)), ((), ())),
                                       preferred_element_type=F32).astype(BF16)

        @pl.when(pl.program_id(0) == 0)
        def _():
            lcol_ref[...] = jnp.zeros_like(lcol_ref)
            dg_ref[...] = jnp.zeros_like(dg_ref)

        lcol_ref[...] += _colsum(e * e)
        dg_ref[...] += _colsum(dx2 * n)

    return _call(
        body, grid=(S // tm,),
        in_specs=[pl.BlockSpec((tm, K), lambda i: (i, 0)), _const((K, D), single=True),
                  pl.BlockSpec((tm, D), lambda i: (i, 0)), _const((1, D)),
                  pl.BlockSpec((tm, D), lambda i: (i, 0))],
        out_specs=[pl.BlockSpec((tm, D), lambda i: (i, 0)), pl.BlockSpec((tm, D), lambda i: (i, 0)),
                   pl.BlockSpec((tm, K), lambda i: (i, 0)), _const((1, D)), _const((1, D))],
        out_shape=[_sds((S, D), BF16), _sds((S, D), F32), _sds((S, K), BF16), _sds((1, D), F32), _sds((1, D), F32)],
        operands=(yy, w, x1, g, tgt), name=name, params=_params(("arbitrary",), 52))[0]


def _mm_nt(a, w, *, tm, tk, name, comm=None):
    S, N = a.shape
    D = w.shape[0]
    n_k = N // tk

    def body(a_ref, w_ref, o_ref, acc_ref):
        k = pl.program_id(1)

        @pl.when(k == 0)
        def _():
            acc_ref[...] = jnp.zeros_like(acc_ref)

        acc_ref[...] = lax.dot_general(a_ref[...], w_ref[...], (((1,), (1,)), ((), ())),
                                       preferred_element_type=F32) + acc_ref[...]

        @pl.when(k == n_k - 1)
        def _():
            o_ref[...] = acc_ref[...].astype(BF16)

    outs, extra = _call(
        body, grid=(S // tm, n_k),
        in_specs=[pl.BlockSpec((tm, tk), lambda i, k: (i, k)), pl.BlockSpec((D, tk), lambda i, k: (0, k))],
        out_specs=[pl.BlockSpec((tm, D), lambda i, k: (i, 0))],
        out_shape=[_sds((S, D), BF16)], operands=(a, w),
        scratch_shapes=[pltpu.VMEM((tm, D), F32)],
        name=name, params=_params(("arbitrary", "arbitrary"), 48), comm=comm)
    return outs[0], extra


def _mm_tn(a, b, *, ts, tn, name, comm=None):
    S, M = a.shape
    N = b.shape[1]
    n_s = S // ts

    def body(a_ref, b_ref, o_ref, acc_ref):
        s = pl.program_id(1)

        @pl.when(s == 0)
        def _():
            acc_ref[...] = jnp.zeros_like(acc_ref)

        acc_ref[...] = lax.dot_general(a_ref[...], b_ref[...], (((0,), (0,)), ((), ())),
                                       preferred_element_type=F32) + acc_ref[...]

        @pl.when(s == n_s - 1)
        def _():
            o_ref[...] = acc_ref[...].astype(BF16)

    outs, extra = _call(
        body, grid=(N // tn, n_s),
        in_specs=[pl.BlockSpec((ts, M), lambda j, s: (s, 0)), pl.BlockSpec((ts, tn), lambda j, s: (s, j))],
        out_specs=[pl.BlockSpec((M, tn), lambda j, s: (0, j))],
        out_shape=[_sds((M, N), BF16)], operands=(a, b),
        scratch_shapes=[pltpu.VMEM((M, tn), F32)],
        name=name, params=_params(("arbitrary", "arbitrary"), 48), comm=comm)
    return outs[0], extra


def _pre_bwd_o(dh, x1, dx2, y0, g_pre, g_post, *, tm, name, comm=None):
    S, D = x1.shape

    def body(dh_ref, x1_ref, dx2_ref, y0_ref, gpre_ref, gpost_ref, dx1_ref, dy0_ref, dgpre_ref, dgpost_ref):
        @pl.when(pl.program_id(0) == 0)
        def _():
            dgpre_ref[...] = jnp.zeros_like(dgpre_ref)
            dgpost_ref[...] = jnp.zeros_like(dgpost_ref)

        dh = dh_ref[...].astype(F32)
        x1 = x1_ref[...]
        r2 = _rms(x1)
        xn = x1 * r2
        dgpre_ref[...] += _colsum(dh * xn)
        dx1 = dx2_ref[...] + _norm_bwd(dh * gpre_ref[...], xn, r2)
        dx1_ref[...] = dx1
        y = y0_ref[...].astype(F32)
        r1 = _rms(y)
        n1 = y * r1
        dgpost_ref[...] += _colsum(dx1 * n1)
        dy0_ref[...] = _norm_bwd(dx1 * gpost_ref[...], n1, r1).astype(BF16)

    row = pl.BlockSpec((tm, D), lambda i: (i, 0))
    return _call(
        body, grid=(S // tm,),
        in_specs=[row, row, row, row, _const((1, D)), _const((1, D))],
        out_specs=[row, row, _const((1, D)), _const((1, D))],
        out_shape=[_sds((S, D), F32), _sds((S, D), BF16), _sds((1, D), F32), _sds((1, D), F32)],
        operands=(dh, x1, dx2, y0, g_pre, g_post),
        name=name, params=_params(("arbitrary",), 48), comm=comm)


def _pre_bwd_e(dh, x, dx1, g_pre, *, tm, name):
    S, D = x.shape

    def body(dh_ref, x_ref, dx1_ref, gpre_ref, gx_ref, dgpre_ref):
        @pl.when(pl.program_id(0) == 0)
        def _():
            dgpre_ref[...] = jnp.zeros_like(dgpre_ref)

        dh = dh_ref[...].astype(F32)
        xx = x_ref[...]
        r0 = _rms(xx)
        xn = xx * r0
        dgpre_ref[...] += _colsum(dh * xn)
        gx_ref[...] = dx1_ref[...] + _norm_bwd(dh * gpre_ref[...], xn, r0)

    row = pl.BlockSpec((tm, D), lambda i: (i, 0))
    return _call(
        body, grid=(S // tm,),
        in_specs=[row, row, row, _const((1, D))],
        out_specs=[row, _const((1, D))],
        out_shape=[_sds((S, D), F32), _sds((1, D), F32)],
        operands=(dh, x, dx1, g_pre), name=name, params=_params(("arbitrary",), 56))[0]


SUBLANES = 8


def _shift_copies(sh_ref, ext_ref, cs):
    for b in range(1, SUBLANES):
        sh_ref[b - 1] = ext_ref[pl.ds(b, sh_ref.shape[1]), cs]


def _rows_at(ext_ref, sh_ref, off, cs, tm):
    b = off % SUBLANES
    if b == 0 or sh_ref is None:
        return ext_ref[pl.ds(off, tm), cs]
    return sh_ref[b - 1, pl.ds(off - b, tm), :]


def _taps(ext_ref, w_ref, n_taps, base, cs, tm, sh_ref=None):
    acc = _rows_at(ext_ref, sh_ref, base, cs, tm) * w_ref[0:1, cs]
    for k in range(1, n_taps):
        acc = acc + _rows_at(ext_ref, sh_ref, base + k, cs, tm) * w_ref[k:k + 1, cs]
    return acc


def _taps_rev(ext_ref, w_ref, n_taps, cs, tm, sh_ref=None):
    acc = _rows_at(ext_ref, sh_ref, n_taps - 1, cs, tm) * w_ref[0:1, cs]
    for k in range(1, n_taps):
        acc = acc + _rows_at(ext_ref, sh_ref, n_taps - 1 - k, cs, tm) * w_ref[k:k + 1, cs]
    return acc


def _e_mix_fwd(p, wa, wb, bias, ln_g, ln_b, *, tm, name, comm=None):
    S = p.shape[0]
    W = p.shape[1] // 7
    nb = tm // HALO
    chunks = [slice(c * LANES, (c + 1) * LANES) for c in range(W // LANES)]

    def body(p_ref, hax_ref, hac_ref, hbv_ref, hbg_ref, wa_ref, wb_ref, bias_ref, lg_ref, lb_ref,
             u_ref, cb_ref, ext_ref, sh_ref):
        keep = (pl.program_id(0) > 0).astype(F32)
        col = lambda j, cs: p_ref[:, j * W + cs.start:j * W + cs.stop].astype(F32)

        ext_ref[0:HALO, :] = hax_ref[...].astype(F32) * hac_ref[...].astype(F32) * keep
        ext_ref[HALO:, :] = p_ref[:, 2 * W:3 * W].astype(F32) * p_ref[:, 0:W].astype(F32)
        for cs in chunks:
            conv = _taps(ext_ref, wa_ref, CONV_A, HALO - (CONV_A - 1), cs, tm)
            az = col(3, cs)
            u_ref[:, cs] = (col(1, cs) * conv * (az * _sig(az))).astype(BF16)

        ext_ref[0:HALO, :] = hbv_ref[...].astype(F32) * _sig(hbg_ref[...].astype(F32)) * keep
        ext_ref[HALO:, :] = p_ref[:, 4 * W:5 * W].astype(F32) * _sig(p_ref[:, 5 * W:6 * W].astype(F32))
        s1 = jnp.zeros((tm, LANES), F32)
        for cs in chunks:
            _shift_copies(sh_ref, ext_ref, cs)
            cb = _taps(ext_ref, wb_ref, CONV_B, HALO - (CONV_B - 1), cs, tm, sh_ref) + bias_ref[:, cs]
            cb_ref[:, cs] = cb
            s1 = s1 + cb
        mu = jnp.sum(s1, axis=-1, keepdims=True) * (1.0 / W)
        s2 = jnp.zeros((tm, LANES), F32)
        for cs in chunks:
            xc = cb_ref[:, cs] - mu
            s2 = s2 + xc * xc
        rs = lax.rsqrt(jnp.sum(s2, axis=-1, keepdims=True) * (1.0 / W) + EPS)
        for cs in chunks:
            lb = (cb_ref[:, cs] - mu) * rs * lg_ref[:, cs] + lb_ref[:, cs]
            bz = col(6, cs)
            u_ref[:, W + cs.start:W + cs.stop] = (lb * _sig(lb) * (bz * _sig(bz))).astype(BF16)

    prev = lambda j: pl.BlockSpec((HALO, W), lambda i: (jnp.maximum(i * nb - 1, 0), j))
    return _call(
        body, grid=(S // tm,),
        in_specs=[pl.BlockSpec((tm, 7 * W), lambda i: (i, 0)), prev(0), prev(2), prev(4), prev(5),
                  _const((CONV_A, W)), _const((CONV_B, W)), _const((1, W)), _const((1, W)), _const((1, W))],
        out_specs=[pl.BlockSpec((tm, 2 * W), lambda i: (i, 0)), pl.BlockSpec((tm, W), lambda i: (i, 0))],
        out_shape=[_sds((S, 2 * W), BF16), _sds((S, W), F32)],
        operands=(p, p, p, p, p, wa, wb, bias, ln_g, ln_b),
        scratch_shapes=[pltpu.VMEM((HALO + tm, W), F32),
                        pltpu.VMEM((SUBLANES - 1, HALO + tm - SUBLANES, LANES), F32)],
        name=name, params=_params(("arbitrary",), 48), comm=comm)


def _e_mix_bwd(du, p, cb, wa, wb, ln_g, ln_b, *, tm, name, comm=None):
    S = p.shape[0]
    W = p.shape[1] // 7
    nb = tm // HALO
    n_t = S // tm
    last_blk = S // HALO - 1
    chunks = [slice(c * LANES, (c + 1) * LANES) for c in range(W // LANES)]

    def body(du_ref, duf_ref, p_ref, fab_ref, faz_ref, fbz_ref, hax_ref, hac_ref, hbv_ref, hbg_ref,
             cb_ref, cbf_ref, wa_ref, wb_ref, lg_ref, lb_ref,
             dp_ref, dwa_ref, dwb_ref, dbias_ref, dlg_ref, dlb_ref, extd_ref, extg_ref, shd_ref, shg_ref):
        i = pl.program_id(0)
        keep_prev = (i > 0).astype(F32)
        keep_next = (i < n_t - 1).astype(F32)
        col = lambda j, cs: p_ref[:, j * W + cs.start:j * W + cs.stop].astype(F32)

        @pl.when(i == 0)
        def _():
            dwa_ref[...] = jnp.zeros_like(dwa_ref)
            dwb_ref[...] = jnp.zeros_like(dwb_ref)
            dbias_ref[...] = jnp.zeros_like(dbias_ref)
            dlg_ref[...] = jnp.zeros_like(dlg_ref)
            dlb_ref[...] = jnp.zeros_like(dlb_ref)

        def dcb_rows(rows, cb_rows_ref, dub, bz_of, dst0, scale, main):
            cbv = cb_rows_ref[...]
            mu = jnp.mean(cbv, axis=-1, keepdims=True)
            xc = cbv - mu
            rs = lax.rsqrt(jnp.mean(xc * xc, axis=-1, keepdims=True) + EPS)
            m1 = jnp.zeros((rows, LANES), F32)
            m2 = jnp.zeros((rows, LANES), F32)
            for cs in chunks:
                nbv = (cb_rows_ref[:, cs] - mu) * rs
                lb = nbv * lg_ref[:, cs] + lb_ref[:, cs]
                sl = _sig(lb)
                bz = bz_of(cs)
                sz = _sig(bz)
                dub_c = dub(cs)
                dlb = dub_c * (bz * sz) * _dsilu(lb, sl)
                if main:
                    dlg_ref[:, cs] += _colsum(dlb * nbv)
                    dlb_ref[:, cs] += _colsum(dlb)
                    dp_ref[:, 6 * W + cs.start:6 * W + cs.stop] = (dub_c * (lb * sl) * _dsilu(bz, sz)).astype(BF16)
                dnb = dlb * lg_ref[:, cs]
                extd_ref[dst0:dst0 + rows, cs] = dnb
                m1 = m1 + dnb
                m2 = m2 + dnb * nbv
            m1 = jnp.sum(m1, axis=-1, keepdims=True) * (1.0 / W)
            m2 = jnp.sum(m2, axis=-1, keepdims=True) * (1.0 / W)
            for cs in chunks:
                nbv = (cb_rows_ref[:, cs] - mu) * rs
                dcb = rs * (extd_ref[dst0:dst0 + rows, cs] - m1 - nbv * m2) * scale
                extd_ref[dst0:dst0 + rows, cs] = dcb
                if main:
                    dbias_ref[:, cs] += _colsum(dcb)

        dcb_rows(tm, cb_ref, lambda cs: du_ref[:, W + cs.start:W + cs.stop].astype(F32),
                 lambda cs: col(6, cs), 0, 1.0, True)
        dcb_rows(HALO, cbf_ref, lambda cs: duf_ref[:, W + cs.start:W + cs.stop].astype(F32),
                 lambda cs: fbz_ref[:, cs].astype(F32), tm, keep_next, False)

        extg_ref[0:HALO, :] = hbv_ref[...].astype(F32) * _sig(hbg_ref[...].astype(F32)) * keep_prev
        extg_ref[HALO:, :] = p_ref[:, 4 * W:5 * W].astype(F32) * _sig(p_ref[:, 5 * W:6 * W].astype(F32))
        base_b = HALO - (CONV_B - 1)
        for cs in chunks:
            _shift_copies(shd_ref, extd_ref, cs)
            _shift_copies(shg_ref, extg_ref, cs)
            dgb = _taps_rev(extd_ref, wb_ref, CONV_B, cs, tm, shd_ref)
            bv = col(4, cs)
            sg = _sig(col(5, cs))
            dp_ref[:, 4 * W + cs.start:4 * W + cs.stop] = (dgb * sg).astype(BF16)
            dp_ref[:, 5 * W + cs.start:5 * W + cs.stop] = (dgb * bv * sg * (1.0 - sg)).astype(BF16)
            dcb = extd_ref[0:tm, cs]
            for k in range(CONV_B):
                dwb_ref[k:k + 1, cs] += _colsum(dcb * _rows_at(extg_ref, shg_ref, base_b + k, cs, tm))

        extg_ref[0:HALO, :] = hax_ref[...].astype(F32) * hac_ref[...].astype(F32) * keep_prev
        extg_ref[HALO:, :] = p_ref[:, 2 * W:3 * W].astype(F32) * p_ref[:, 0:W].astype(F32)
        base_a = HALO - (CONV_A - 1)
        for cs in chunks:
            conv = _taps(extg_ref, wa_ref, CONV_A, base_a, cs, tm)
            az = col(3, cs)
            sz = _sig(az)
            ab = col(1, cs)
            dua = du_ref[:, cs].astype(F32)
            dya = dua * (az * sz)
            dp_ref[:, W + cs.start:W + cs.stop] = (dya * conv).astype(BF16)
            dp_ref[:, 3 * W + cs.start:3 * W + cs.stop] = (dua * (ab * conv) * _dsilu(az, sz)).astype(BF16)
            extd_ref[0:tm, cs] = dya * ab
            azf = faz_ref[:, cs].astype(F32)
            extd_ref[tm:tm + HALO, cs] = (duf_ref[:, cs].astype(F32) * (azf * _sig(azf))
                                          * fab_ref[:, cs].astype(F32) * keep_next)
        for cs in chunks:
            dca = _taps_rev(extd_ref, wa_ref, CONV_A, cs, tm)
            dp_ref[:, cs] = (dca * col(2, cs)).astype(BF16)
            dp_ref[:, 2 * W + cs.start:2 * W + cs.stop] = (dca * col(0, cs)).astype(BF16)
            dconv = extd_ref[0:tm, cs]
            for k in range(CONV_A):
                dwa_ref[k:k + 1, cs] += _colsum(dconv * extg_ref[pl.ds(base_a + k, tm), cs])

    prev = lambda j: pl.BlockSpec((HALO, W), lambda i: (jnp.maximum(i * nb - 1, 0), j))
    nxt = lambda j, w: pl.BlockSpec((HALO, w), lambda i: (jnp.minimum((i + 1) * nb, last_blk), j))
    row = lambda w: pl.BlockSpec((tm, w), lambda i: (i, 0))
    return _call(
        body, grid=(n_t,),
        in_specs=[row(2 * W), nxt(0, 2 * W), row(7 * W), nxt(1, W), nxt(3, W), nxt(6, W),
                  prev(0), prev(2), prev(4), prev(5), row(W), nxt(0, W),
                  _const((CONV_A, W)), _const((CONV_B, W)), _const((1, W)), _const((1, W))],
        out_specs=[row(7 * W), _const((CONV_A, W)), _const((CONV_B, W)), _const((1, W)), _const((1, W)), _const((1, W))],
        out_shape=[_sds((S, 7 * W), BF16), _sds((CONV_A, W), F32), _sds((CONV_B, W), F32),
                   _sds((1, W), F32), _sds((1, W), F32), _sds((1, W), F32)],
        operands=(du, du, p, p, p, p, p, p, p, p, cb, cb, wa, wb, ln_g, ln_b),
        scratch_shapes=[pltpu.VMEM((tm + HALO, W), F32), pltpu.VMEM((HALO + tm, W), F32),
                        pltpu.VMEM((SUBLANES - 1, HALO + tm - SUBLANES, LANES), F32),
                        pltpu.VMEM((SUBLANES - 1, HALO + tm - SUBLANES, LANES), F32)],
        name=name, params=_params(("arbitrary",), 52), comm=comm)


def _counts(i, tm, rows, off, win):
    t = i * tm + off + lax.broadcasted_iota(jnp.int32, (rows, 1), 0)
    return jnp.minimum(t + 1, win).astype(F32)


def _o_mix_fwd(q, cw, cb, cscale, *, tm, name):
    S = q.shape[0]
    WC = q.shape[1] // 2
    NG = len(POOL_WINDOWS)
    G = WC // NG
    nb = tm // PHALO

    def body(v_ref, z_ref, hv_ref, cw_ref, cb_ref, sc_ref, yy_ref, pooled_ref, gg_ref, ext_ref):
        i = pl.program_id(0)
        keep = (i > 0).astype(F32)
        for g, win in enumerate(POOL_WINDOWS):
            cs = slice(g * G, (g + 1) * G)
            v = v_ref[:, cs].astype(F32)
            ext_ref[0:PHALO, :] = hv_ref[:, cs].astype(F32) * keep
            ext_ref[PHALO:, :] = v
            s = v
            for j in range(1, win):
                s = s + ext_ref[pl.ds(PHALO - j, tm), :]
            pooled = (s / _counts(i, tm, tm, 0, win) - v).astype(BF16)
            pooled_ref[:, cs] = pooled
            gg = jnp.dot(pooled, cw_ref[g], preferred_element_type=F32) + cb_ref[:, cs]
            gg_ref[:, cs] = gg.astype(BF16)
            z = z_ref[:, cs].astype(F32)
            yy_ref[:, cs] = (gg * sc_ref[:, cs] * (z * _sig(z))).astype(BF16)

    row = lambda j: pl.BlockSpec((tm, WC), lambda i: (i, j))
    out = pl.BlockSpec((tm, WC), lambda i: (i, 0))
    return _call(
        body, grid=(S // tm,),
        in_specs=[row(0), row(1), pl.BlockSpec((PHALO, WC), lambda i: (jnp.maximum(i * nb - 1, 0), 0)),
                  _const((NG, G, G)), _const((1, WC)), _const((1, WC))],
        out_specs=[out, out, out],
        out_shape=[_sds((S, WC), BF16)] * 3, operands=(q, q, q, cw, cb, cscale),
        scratch_shapes=[pltpu.VMEM((PHALO + tm, G), F32)],
        name=name, params=_params(("arbitrary",), 40))[0]


def _o_mix_bwd(dyy, q, gg, pooled, cw, cscale, *, tm, name):
    S = q.shape[0]
    WC = q.shape[1] // 2
    NG = len(POOL_WINDOWS)
    G = WC // NG
    nb = tm // PHALO
    n_t = S // tm
    last_blk = S // PHALO - 1
    nt = (((1,), (1,)), ((), ()))
    tn = (((0,), (0,)), ((), ()))

    def body(dyy_ref, dyyf_ref, z_ref, zf_ref, gg_ref, pooled_ref, cw_ref, sc_ref,
             dq_ref, dcw_ref, dcb_ref, dsc_ref, ext_ref):
        i = pl.program_id(0)
        keep_next = (i < n_t - 1).astype(F32)

        @pl.when(i == 0)
        def _():
            dcw_ref[...] = jnp.zeros_like(dcw_ref)
            dcb_ref[...] = jnp.zeros_like(dcb_ref)
            dsc_ref[...] = jnp.zeros_like(dsc_ref)

        for g, win in enumerate(POOL_WINDOWS):
            cs = slice(g * G, (g + 1) * G)
            sc = sc_ref[:, cs]
            z = z_ref[:, cs].astype(F32)
            sz = _sig(z)
            dyy_c = dyy_ref[:, cs].astype(F32)
            ggv = gg_ref[:, cs].astype(F32)
            dyy0 = dyy_c * (z * sz)
            dq_ref[:, WC + cs.start:WC + cs.stop] = (dyy_c * (ggv * sc) * _dsilu(z, sz)).astype(BF16)
            dgg = dyy0 * sc
            dsc_ref[:, cs] += _colsum(dyy0 * ggv)
            dcb_ref[:, cs] += _colsum(dgg)
            dgg_b = dgg.astype(BF16)
            dcw_ref[g] += lax.dot_general(pooled_ref[:, cs], dgg_b, tn, preferred_element_type=F32)
            dpool = lax.dot_general(dgg_b, cw_ref[g], nt, preferred_element_type=F32)
            zf = zf_ref[:, cs].astype(F32)
            dgg_f = (dyyf_ref[:, cs].astype(F32) * (zf * _sig(zf)) * sc * keep_next).astype(BF16)
            dpool_f = lax.dot_general(dgg_f, cw_ref[g], nt, preferred_element_type=F32)
            ext_ref[0:tm, :] = dpool / _counts(i, tm, tm, 0, win)
            ext_ref[tm:tm + PHALO, :] = dpool_f / _counts(i, tm, PHALO, tm, win)
            dv = ext_ref[0:tm, :] - dpool
            for j in range(1, win):
                dv = dv + ext_ref[pl.ds(j, tm), :]
            dq_ref[:, cs] = dv.astype(BF16)

    row = lambda: pl.BlockSpec((tm, WC), lambda i: (i, 0))
    nxt = lambda j: pl.BlockSpec((PHALO, WC), lambda i: (jnp.minimum((i + 1) * nb, last_blk), j))
    return _call(
        body, grid=(n_t,),
        in_specs=[row(), nxt(0), pl.BlockSpec((tm, WC), lambda i: (i, 1)), nxt(1), row(), row(),
                  _const((NG, G, G)), _const((1, WC))],
        out_specs=[pl.BlockSpec((tm, 2 * WC), lambda i: (i, 0)), _const((NG, G, G)), _const((1, WC)), _const((1, WC))],
        out_shape=[_sds((S, 2 * WC), BF16), _sds((NG, G, G), F32), _sds((1, WC), F32), _sds((1, WC), F32)],
        operands=(dyy, dyy, q, q, gg, pooled, cw, cscale),
        scratch_shapes=[pltpu.VMEM((tm + PHALO, G), F32)],
        name=name, params=_params(("arbitrary",), 48))[0]


def _place():
    return lax.axis_index("x"), lax.axis_index("y"), lax.axis_index("c")


def _piece(ref, axis, size, index):
    start = index * size
    if axis == len(ref.shape) - 1:
        start = pl.multiple_of(start, LANES)
    idx = [slice(None)] * len(ref.shape)
    idx[axis] = pl.ds(start, size)
    return ref.at[tuple(idx)]


def _gather_copies(src, out, axis, size, send_sems, recv_sems, base, held=None):
    x, y, c = _place()
    sib, xn, yn = (x, y, 1 - c), (1 - x, y, c), (x, 1 - y, c)

    def blk(px, py, of=out):
        return _piece(of, axis, size, 4 * px + 2 * py + c)

    def half(ref, h):
        n = ref.shape[0] // 2
        return ref.at[pl.ds(h * n, n)]

    def rc(k, s, d, to):
        return pltpu.make_async_remote_copy(src_ref=s, dst_ref=d, send_sem=send_sems.at[base + k],
                                            recv_sem=recv_sems.at[base + k], device_id=to, device_id_type=MESH)

    own, xb, yb, db = blk(x, y), blk(1 - x, y), blk(x, 1 - y), blk(1 - x, 1 - y)
    got = out if held is None else held
    xs, ys, ds = blk(1 - x, y, got), blk(x, 1 - y, got), blk(1 - x, 1 - y, got)
    return [rc(0, src, own, sib), rc(1, src, own, xn), rc(2, src, own, yn),
            rc(3, half(xs, 0), half(xb, 0), yn), rc(4, half(ys, 1), half(yb, 1), xn),
            rc(5, xs, xb, sib), rc(6, ys, yb, sib), rc(7, ds, db, sib)]


N_GATHER = 8


def _gather_comm(shards, axes, phases):
    n = len(shards)
    if phases == "second":
        sizes = [s.shape[a] // N_DEV for s, a in zip(shards, axes)]
        full = [_sds(s.shape, s.dtype) for s in shards]
    else:
        sizes = [s.shape[a] for s, a in zip(shards, axes)]
        full = [_sds(s.shape[:a] + (N_DEV * s.shape[a],) + s.shape[a + 1:], s.dtype) for s, a in zip(shards, axes)]

    def plan(ins, outs, sems):
        x, y, c = _place()
        me = 4 * x + 2 * y + c
        if phases == "second":
            cps = [_gather_copies(_piece(ins[t], axes[t], sizes[t], me), outs[t], axes[t], sizes[t], sems[0], sems[1],
                                  N_GATHER * t, ins[t]) for t in range(n)]
        else:
            cps = [_gather_copies(ins[t], outs[t], axes[t], sizes[t], sems[0], sems[1], N_GATHER * t)
                   for t in range(n)]
        mine = [pltpu.make_async_copy(ins[t], _piece(outs[t], axes[t], sizes[t], me), sems[2].at[t])
                for t in range(n)] if phases != "second" else []
        return cps, mine

    def send_own(ins, outs, sems):
        cps, mine = plan(ins, outs, sems)
        for t in range(n):
            mine[t].start()
            for k in (0, 1, 2):
                cps[t][k].start()

    def pass_on(ins, outs, sems):
        cps, _ = plan(ins, outs, sems)
        for t in range(n):
            if phases == "all":
                cps[t][1].wait_recv()
            cps[t][3].start()
            cps[t][5].start()
        for t in range(n):
            if phases == "all":
                cps[t][2].wait_recv()
            cps[t][4].start()
            cps[t][6].start()

    def own_landed(ins, outs, sems):
        cps, mine = plan(ins, outs, sems)
        for t in range(n):
            for k in (0, 1, 2):
                cps[t][k].wait()
            mine[t].wait()

    def all_landed(ins, outs, sems):
        cps, mine = plan(ins, outs, sems)
        for t in range(n):
            cps[t][3].wait_recv()
            cps[t][4].wait_recv()
            cps[t][7].start()
        for t in range(n):
            for k in ((0, 5, 6, 7) if phases == "all" else (5, 6, 7)):
                cps[t][k].wait_recv()
            for k in (range(N_GATHER) if phases == "all" else range(3, N_GATHER)):
                cps[t][k].wait_send()
            if phases == "all":
                mine[t].wait()

    sems = [pltpu.SemaphoreType.DMA((N_GATHER * n,)), pltpu.SemaphoreType.DMA((N_GATHER * n,))]
    if phases != "second":
        sems.append(pltpu.SemaphoreType.DMA((n,)))
    if phases == "all":
        return _Comm(shards, full, sems, send_own, all_landed, middle=pass_on)
    if phases == "first":
        return _Comm(shards, full, sems, send_own, own_landed)
    return _Comm(shards, full, sems, pass_on, all_landed, aliases={t: t for t in range(n)})


def _pair_comm(grads, axes, sizes):
    n = len(grads)
    outs_sds = [_sds((4,) + g.shape[:a] + (s,) + g.shape[a + 1:], g.dtype) for g, a, s in zip(grads, axes, sizes)]

    def copies(ins, outs, sems):
        send_sems, recv_sems = sems
        x, y, c = _place()
        return [pltpu.make_async_remote_copy(
            src_ref=_piece(ins[t], axes[t], sizes[t], 2 * qi + (1 - c)), dst_ref=outs[t].at[qi],
            send_sem=send_sems.at[4 * t + qi], recv_sem=recv_sems.at[4 * t + qi],
            device_id=(x, y, 1 - c), device_id_type=MESH) for t in range(n) for qi in range(4)]

    def start(ins, outs, sems):
        for cp in copies(ins, outs, sems):
            cp.start()

    def finish(ins, outs, sems):
        for cp in copies(ins, outs, sems):
            cp.wait()

    sems = [pltpu.SemaphoreType.DMA((4 * n,)), pltpu.SemaphoreType.DMA((4 * n,))]
    return _Comm(grads, outs_sds, sems, start, finish)


def _chip_comm(sums):
    n = len(sums)
    outs_sds = [_sds((3,) + s.shape[1:], s.dtype) for s in sums]

    def copies(ins, outs, sems):
        send_sems, recv_sems = sems
        x, y, c = _place()
        return [pltpu.make_async_remote_copy(
            src_ref=ins[t].at[2 * qx + qy], dst_ref=outs[t].at[j],
            send_sem=send_sems.at[3 * t + j], recv_sem=recv_sems.at[3 * t + j],
            device_id=(qx, qy, c), device_id_type=MESH)
            for t in range(n) for j, (qx, qy) in enumerate([(1 - x, y), (x, 1 - y), (1 - x, 1 - y)])]

    def start(ins, outs, sems):
        for cp in copies(ins, outs, sems):
            cp.start()

    def finish(ins, outs, sems):
        for cp in copies(ins, outs, sems):
            cp.wait()

    sems = [pltpu.SemaphoreType.DMA((3 * n,)), pltpu.SemaphoreType.DMA((3 * n,))]
    return _Comm(sums, outs_sds, sems, start, finish)


def _small_comm(small):
    def copies(ins, outs, sems):
        send_sems, recv_sems, local_sem = sems
        x, y, c = _place()
        mine = outs[0].at[4 * x + 2 * y + c]
        out = [pltpu.make_async_copy(ins[0], mine, local_sem.at[0])]
        for k in range(1, N_DEV):
            peer = (1 - x if k & 4 else x, 1 - y if k & 2 else y, 1 - c if k & 1 else c)
            out.append(pltpu.make_async_remote_copy(
                src_ref=ins[0], dst_ref=mine, send_sem=send_sems.at[k - 1], recv_sem=recv_sems.at[k - 1],
                device_id=peer, device_id_type=MESH))
        return out

    def start(ins, outs, sems):
        for cp in copies(ins, outs, sems):
            cp.start()

    def finish(ins, outs, sems):
        for cp in copies(ins, outs, sems):
            cp.wait()

    sems = [pltpu.SemaphoreType.DMA((N_DEV - 1,)), pltpu.SemaphoreType.DMA((N_DEV - 1,)), pltpu.SemaphoreType.DMA((1,))]
    return _Comm([small], [_sds((N_DEV,) + small.shape, small.dtype)], sems, start, finish)


def _small_scatter_comm(send):
    def copies(ins, outs, sems):
        send_sems, recv_sems, local_sem = sems
        x, y, c = _place()
        me = 4 * x + 2 * y + c
        out = [pltpu.make_async_copy(ins[0].at[me], outs[0].at[me], local_sem.at[0])]
        for k in range(1, N_DEV):
            px, py, pc = (1 - x if k & 4 else x, 1 - y if k & 2 else y, 1 - c if k & 1 else c)
            out.append(pltpu.make_async_remote_copy(
                src_ref=ins[0].at[4 * px + 2 * py + pc], dst_ref=outs[0].at[me], send_sem=send_sems.at[k - 1],
                recv_sem=recv_sems.at[k - 1], device_id=(px, py, pc), device_id_type=MESH))
        return out

    def start(ins, outs, sems):
        for cp in copies(ins, outs, sems):
            cp.start()

    def finish(ins, outs, sems):
        for cp in copies(ins, outs, sems):
            cp.wait()

    sems = [pltpu.SemaphoreType.DMA((N_DEV - 1,)), pltpu.SemaphoreType.DMA((N_DEV - 1,)), pltpu.SemaphoreType.DMA((1,))]
    return _Comm([send], [_sds(send.shape, send.dtype)], sems, start, finish)


def _pair_sum(c_idx, grad, recv, axis, size, split, *, name):
    nd = len(grad.shape)
    piece = grad.shape[:axis] + (size,) + grad.shape[axis + 1:]
    blk = (piece[0] // split,) + piece[1:]

    def g_map(q, r, c_ref):
        idx = [0] * nd
        idx[axis] = 2 * q + c_ref[0]
        idx[0] = idx[0] * split + r if axis == 0 else r
        return tuple(idx)

    def r_map(q, r, c_ref):
        return (q, r) + (0,) * (nd - 1)

    def body(c_ref, g_ref, r_ref, o_ref):
        o_ref[0] = (g_ref[...].astype(F32) + r_ref[0].astype(F32)).astype(BF16)

    return _call(
        body, grid=(4, split), prefetch=c_idx,
        in_specs=[pl.BlockSpec(blk, g_map), pl.BlockSpec((1,) + blk, r_map)],
        out_specs=[pl.BlockSpec((1,) + blk, r_map)], out_shape=[_sds((4,) + piece, BF16)],
        operands=(grad, recv), name=name, params=_params(("arbitrary", "arbitrary"), 32))[0][0]


def _adam_math(w, g, m, v):
    m = ADAM_B1 * m + (1.0 - ADAM_B1) * g
    v = ADAM_B2 * v + (1.0 - ADAM_B2) * (g * g)
    m_hat = m / (1.0 - ADAM_B1 ** ADAM_STEP)
    v_hat = v / (1.0 - ADAM_B2 ** ADAM_STEP)
    delta = -ADAM_LR * (m_hat / (jnp.sqrt(v_hat) + ADAM_EPS) + ADAM_WD * w)
    return delta, m, v


def _adam_big(q_idx, sums, recv, w, m, v, split, *, name, comm=None):
    shape = w.shape
    nd = len(shape)
    blk = (shape[0] // split,) + shape[1:]
    w_map = lambda r, q_ref: (r,) + (0,) * (nd - 1)
    s_map = lambda r, q_ref: (q_ref[0], r) + (0,) * (nd - 1)
    r_map = lambda r, q_ref: (0, r) + (0,) * (nd - 1)

    def body(q_ref, s_ref, r_ref, w_ref, m_ref, v_ref, g_ref, d_ref, nm_ref, nv_ref):
        g = s_ref[0].astype(F32) + r_ref[0].astype(F32) + r_ref[1].astype(F32) + r_ref[2].astype(F32)
        g_ref[...] = g
        d_ref[...], nm_ref[...], nv_ref[...] = _adam_math(w_ref[...], g, m_ref[...], v_ref[...])

    wspec = pl.BlockSpec(blk, w_map)
    return _call(
        body, grid=(split,), prefetch=q_idx,
        in_specs=[pl.BlockSpec((1,) + blk, s_map), pl.BlockSpec((3,) + blk, r_map), wspec, wspec, wspec],
        out_specs=[wspec] * 4, out_shape=[_sds(shape, F32)] * 4, operands=(sums, recv, w, m, v),
        name=name, params=_params(("arbitrary",), 32), comm=comm)


def _adam_small(parts, w, m, v, *, name):
    R = w.shape[0]

    def body(p_ref, w_ref, m_ref, v_ref, g_ref, d_ref, nm_ref, nv_ref):
        g = p_ref[0]
        for d in range(1, N_DEV):
            g = g + p_ref[d]
        g_ref[...] = g
        d_ref[...], nm_ref[...], nv_ref[...] = _adam_math(w_ref[...], g, m_ref[...], v_ref[...])

    whole = _const((R, LANES))
    return _call(
        body, grid=(1,), in_specs=[_const((N_DEV, R, LANES)), whole, whole, whole], out_specs=[whole] * 4,
        out_shape=[_sds((R, LANES), F32)] * 4, operands=(parts, w, m, v), name=name,
        params=_params(("arbitrary",), 32))[0]


def _pack(arrs):
    return jnp.concatenate([a.reshape(-1) for a in arrs]).reshape(-1, LANES)


def _unpack(packed, shapes):
    flat = packed.reshape(-1)
    out, off = [], 0
    for s in shapes:
        n = 1
        for d in s:
            n *= d
        out.append(flat[off:off + n].reshape(s))
        off += n
    return out


BIG = ("e_in", "e_out", "o_in", "o_cw", "o_out")
BIG_AXIS = dict(e_in=1, e_out=0, o_in=1, o_cw=1, o_out=0)
BIG_SPLIT = dict(e_in=8, e_out=4, o_in=4, o_cw=4, o_out=4)
REPLICATED = ("e_norm_pre", "e_norm_post", "e_b_conv_bias", "e_b_ln_g", "e_b_ln_b")
SHARDED = ("e_a_conv", "e_b_conv", "o_norm_pre", "o_norm_post", "o_c_b", "o_c_scale")
SMALL = REPLICATED + SHARDED


class _Exchange:
    def __init__(self, shards, small, order, c_idx):
        self.shards = shards
        self.small = small
        self.order = order
        self.c_idx = c_idx
        self.reduced = {}

    def gather(self, keys):
        return _gather_comm([self.shards[k] for k in keys], [BIG_AXIS[k] for k in keys], "all")

    def gather1(self, keys):
        return _gather_comm([self.shards[k] for k in keys], [BIG_AXIS[k] for k in keys], "first")

    def gather2(self, keys, firsts):
        return _gather_comm(firsts, [BIG_AXIS[k] for k in keys], "second")

    def pair(self, grads):
        keys = list(grads)
        return _pair_comm([grads[k] for k in keys], [BIG_AXIS[k] for k in keys],
                          [grads[k].shape[BIG_AXIS[k]] // N_DEV for k in keys])

    def pair_sums(self, grads, received):
        return {k: _pair_sum(self.c_idx, grads[k], r, BIG_AXIS[k], grads[k].shape[BIG_AXIS[k]] // N_DEV,
                             BIG_SPLIT[k], name="pair_sum_" + k) for k, r in zip(grads, received)}

    def chips(self, sums):
        return _chip_comm([sums[k] for k in sums])

    def done(self, sums, received):
        self.reduced.update({k: (sums[k], r) for k, r in zip(sums, received)})


def _local_step(x, tgt, w_small, ex):
    S, D = x.shape
    tnt, tx, tw = min(TM_NT, S), min(TM_MIX, S), min(TM_WIDE, S)

    wt = {}
    p, h0, wt["e_in"], got = _gather_matmul(ex.order, x, w_small["e_norm_pre"], ex.shards["e_in"], tm=tw,
                                            name="e_in_fwd", comm=_small_comm(ex.small))
    per_dev = [_unpack(got[0][d], [w_small[k].shape for k in SHARDED]) for d in range(N_DEV)]
    sm = {k: w_small[k] for k in REPLICATED}
    for j, k in enumerate(SHARDED):
        sm[k] = jnp.concatenate([per_dev[d][j] for d in range(N_DEV)], axis=-1)
    n_groups = sm["o_c_b"].shape[0]
    sm["o_c_b"] = sm["o_c_b"].reshape(1, -1)

    W = p.shape[1] // 7
    (u, cb), got = _e_mix_fwd(p, sm["e_a_conv"], sm["e_b_conv"], sm["e_b_conv_bias"], sm["e_b_ln_g"],
                              sm["e_b_ln_b"], tm=tx, name="e_mix_fwd", comm=ex.gather(["e_out"]))
    wt["e_out"] = got[0]
    late = ["o_out", "o_cw"]
    (x1, y0), part = _out_norm_res(u, wt["e_out"], x, sm["e_norm_post"], tm=tw, name="e_out_fwd",
                                   comm=ex.gather1(late))
    q, h1, wt["o_in"], got = _gather_matmul(ex.order, x1, sm["o_norm_pre"], ex.shards["o_in"], tm=tw,
                                            name="o_in_fwd", comm=ex.gather2(late, part))
    wt.update(zip(late, got))
    yy, pooled, gg = _o_mix_fwd(q, wt["o_cw"], sm["o_c_b"], sm["o_c_scale"], tm=tw, name="o_mix_fwd")
    dout, dx2, dyy, lcol, dg_o_post = _out_loss(yy, wt["o_out"], x1, sm["o_norm_post"], tgt, tm=tx, name="o_out_loss")
    loss = (0.5 / D) * jnp.sum(lcol)

    dq, d_cw, d_cb, d_cscale = _o_mix_bwd(dyy, q, gg, pooled, wt["o_cw"], sm["o_c_scale"], tm=tw, name="o_mix_bwd")
    g_o_out, _ = _mm_tn(yy, dout, ts=tnt, tn=W, name="o_out_dw")
    ga = dict(o_out=g_o_out, o_cw=d_cw.astype(BF16))
    dh1, ra = _mm_nt(dq, wt["o_in"], tm=tnt, tk=W, name="o_in_bwd", comm=ex.pair(ga))
    sa = ex.pair_sums(ga, ra)
    (dx1, dy0, dg_o_pre, dg_e_post), ra = _pre_bwd_o(dh1, x1, dx2, y0, sm["o_norm_pre"], sm["e_norm_post"],
                                                     tm=tx, name="o_pre_bwd", comm=ex.chips(sa))
    ex.done(sa, ra)
    g_o_in, _ = _mm_tn(h1, dq, ts=tnt, tn=W, name="o_in_dw")
    gb = dict(o_in=g_o_in)
    du, rb = _mm_nt(dy0, wt["e_out"], tm=tnt, tk=W, name="e_out_bwd", comm=ex.pair(gb))
    sb = ex.pair_sums(gb, rb)
    g_e_out, _ = _mm_tn(u, dy0, ts=tnt, tn=W, name="e_out_dw")
    gc = dict(e_out=g_e_out)
    (dp, d_wa, d_wb, d_bias, d_lg, d_lb), rbc = _e_mix_bwd(
        du, p, cb, sm["e_a_conv"], sm["e_b_conv"], sm["e_b_ln_g"], sm["e_b_ln_b"], tm=tx, name="e_mix_bwd",
        comm=_merge(ex.chips(sb), ex.pair(gc)))
    ex.done(sb, rbc[:1])
    sc = ex.pair_sums(gc, rbc[1:])
    g_e_in, rc = _mm_tn(h0, dp, ts=tnt, tn=W, name="e_in_dw", comm=ex.chips(sc))
    ex.done(sc, rc)
    gd = dict(e_in=g_e_in)
    sd = ex.pair_sums(gd, _run_comm(ex.pair(gd), "pair_e_in"))
    dh0, rd = _mm_nt(dp, wt["e_in"], tm=tnt, tk=W, name="e_in_bwd", comm=ex.chips(sd))
    ex.done(sd, rd)
    grad_x, dg_e_pre = _pre_bwd_e(dh0, x, dx1, sm["e_norm_pre"], tm=tw, name="e_pre_bwd")

    small = dict(e_norm_pre=dg_e_pre, e_norm_post=dg_e_post, e_a_conv=d_wa, e_b_conv=d_wb, e_b_conv_bias=d_bias,
                 e_b_ln_g=d_lg, e_b_ln_b=d_lb, o_norm_pre=dg_o_pre, o_norm_post=dg_o_post,
                 o_c_b=d_cb.reshape(n_groups, -1), o_c_scale=d_cscale)
    return loss, grad_x, small


def kernel(x, e_norm_pre, e_norm_post, e_w_in, e_a_conv, e_b_conv, e_b_conv_bias, e_b_ln_g, e_b_ln_b, e_w_out, o_norm_pre, o_norm_post, o_w_in, o_c_w, o_c_b, o_c_scale, o_w_out, loss_target, m_e_norm_pre, m_e_norm_post, m_e_w_in, m_e_a_conv, m_e_b_conv, m_e_b_conv_bias, m_e_b_ln_g, m_e_b_ln_b, m_e_w_out, m_o_norm_pre, m_o_norm_post, m_o_w_in, m_o_c_w, m_o_c_b, m_o_c_scale, m_o_w_out, v_e_norm_pre, v_e_norm_post, v_e_w_in, v_e_a_conv, v_e_b_conv, v_e_b_conv_bias, v_e_b_ln_g, v_e_b_ln_b, v_e_w_out, v_o_norm_pre, v_o_norm_post, v_o_w_in, v_o_c_w, v_o_c_b, v_o_c_scale, v_o_w_out):
    xi, yi, ci = _place()
    w_big = dict(e_in=e_w_in[0], e_out=e_w_out[0], o_in=o_w_in[0], o_cw=o_c_w[0], o_out=o_w_out[0])
    m_big = dict(e_in=m_e_w_in[0], e_out=m_e_w_out[0], o_in=m_o_w_in[0], o_cw=m_o_c_w[0], o_out=m_o_w_out[0])
    v_big = dict(e_in=v_e_w_in[0], e_out=v_e_w_out[0], o_in=v_o_w_in[0], o_cw=v_o_c_w[0], o_out=v_o_w_out[0])
    w_small = dict(e_norm_pre=e_norm_pre, e_norm_post=e_norm_post, e_b_conv_bias=e_b_conv_bias, e_b_ln_g=e_b_ln_g,
                   e_b_ln_b=e_b_ln_b, e_a_conv=e_a_conv[0], e_b_conv=e_b_conv[0], o_norm_pre=o_norm_pre,
                   o_norm_post=o_norm_post, o_c_b=o_c_b[0], o_c_scale=o_c_scale)
    m_small = dict(e_norm_pre=m_e_norm_pre, e_norm_post=m_e_norm_post, e_b_conv_bias=m_e_b_conv_bias,
                   e_b_ln_g=m_e_b_ln_g, e_b_ln_b=m_e_b_ln_b, e_a_conv=m_e_a_conv[0], e_b_conv=m_e_b_conv[0],
                   o_norm_pre=m_o_norm_pre, o_norm_post=m_o_norm_post, o_c_b=m_o_c_b[0], o_c_scale=m_o_c_scale)
    v_small = dict(e_norm_pre=v_e_norm_pre, e_norm_post=v_e_norm_post, e_b_conv_bias=v_e_b_conv_bias,
                   e_b_ln_g=v_e_b_ln_g, e_b_ln_b=v_e_b_ln_b, e_a_conv=v_e_a_conv[0], e_b_conv=v_e_b_conv[0],
                   o_norm_pre=v_o_norm_pre, o_norm_post=v_o_norm_post, o_c_b=v_o_c_b[0], o_c_scale=v_o_c_scale)

    c_idx = jnp.reshape(ci, (1,)).astype(jnp.int32)
    order = jnp.stack([2 * xi + yi, 2 * (1 - xi) + yi, 2 * xi + (1 - yi), 2 * (1 - xi) + (1 - yi)]).astype(jnp.int32)
    ex = _Exchange({k: w_big[k].astype(BF16) for k in BIG}, _pack([w_small[k] for k in SHARDED]), order, c_idx)
    loss, grad_x, g_small = _local_step(x[0], loss_target[0], w_small, ex)

    q_idx = jnp.reshape(2 * xi + yi, (1,)).astype(jnp.int32)
    big_out = {k: _adam_big(q_idx, *ex.reduced[k], w_big[k], m_big[k], v_big[k], BIG_SPLIT[k], name="adam_" + k)[0]
               for k in BIG}

    rep = _pack([g_small[k] for k in REPLICATED])
    loss_row = jnp.pad(jnp.reshape(loss, (1, 1)), ((0, 0), (0, LANES - 1)))
    blocks = []
    for k in SHARDED:
        r, n = w_small[k].shape
        blocks.append(g_small[k].reshape(r, N_DEV, n).transpose(1, 0, 2).reshape(N_DEV, r * n))
    blocks = jnp.concatenate(blocks, axis=1).reshape(N_DEV, -1, LANES)
    head = jnp.concatenate([rep, loss_row], axis=0)
    send = jnp.concatenate([jnp.broadcast_to(head[None], (N_DEV,) + head.shape), blocks], axis=1)
    parts = _run_comm(_small_scatter_comm(send), "small_grad_exchange")[0]

    def own_rows(d):
        return jnp.concatenate([_pack([d[k] for k in REPLICATED]), jnp.ones((1, LANES), F32),
                                _pack([d[k] for k in SHARDED])], axis=0)

    res_small = _adam_small(parts, own_rows(w_small), own_rows(m_small), own_rows(v_small), name="adam_small")
    n_rep = rep.shape[0]
    loss = res_small[0][n_rep, 0]
    small_out = {k: [] for k in SMALL}
    for packed in res_small:
        for k, t in zip(REPLICATED, _unpack(packed[:n_rep], [w_small[k].shape for k in REPLICATED])):
            small_out[k].append(t)
        for k, t in zip(SHARDED, _unpack(packed[n_rep + 1:], [w_small[k].shape for k in SHARDED])):
            small_out[k].append(t)

    big_of = dict(e_w_in="e_in", e_w_out="e_out", o_w_in="o_in", o_c_w="o_cw", o_w_out="o_out")
    stacked = ("e_a_conv", "e_b_conv", "o_c_b")

    def leaf(name, which):
        if name in big_of:
            return big_out[big_of[name]][which][None]
        t = small_out[name][which]
        return t[None] if name in stacked else t

    order = ("e_norm_pre", "e_norm_post", "e_w_in", "e_a_conv", "e_b_conv", "e_b_conv_bias", "e_b_ln_g", "e_b_ln_b",
             "e_w_out", "o_norm_pre", "o_norm_post", "o_w_in", "o_c_w", "o_c_b", "o_c_scale", "o_w_out")
    outs = [loss, grad_x[None]]
    for which in range(4):
        outs += [leaf(nm, which) for nm in order]
    return tuple(outs)
```

```python
import jax
import jax.numpy as jnp
from jax import lax
from jax.experimental import pallas as pl
from jax.experimental.pallas import tpu as pltpu

F32 = jnp.float32
BF16 = jnp.bfloat16
EPS = 1e-6
MESH = pl.DeviceIdType.MESH
ANY = pl.BlockSpec(memory_space=pl.ANY)

N_DEV = 8
HALO = 32
PHALO = 16
CONV_A = 3
CONV_B = 31
POOL_WINDOWS = (2, 4, 8, 16)
LANES = 128
MIB = 1024 * 1024

ADAM_LR = 0.001
ADAM_B1 = 0.9
ADAM_B2 = 0.999
ADAM_EPS = 1e-08
ADAM_WD = 0.01
ADAM_STEP = 10

TM_NT = 1024
TM_MIX = 256
TM_WIDE = 512


def _sds(shape, dtype):
    return jax.ShapeDtypeStruct(tuple(shape), dtype)


def _params(sem, vmem_mib):
    return pltpu.CompilerParams(dimension_semantics=sem, vmem_limit_bytes=vmem_mib * MIB)


def _const(shape, single=False):
    n = len(shape)
    if single:
        return pl.BlockSpec(shape, lambda *_: (0,) * n, pipeline_mode=pl.Buffered(1))
    return pl.BlockSpec(shape, lambda *_: (0,) * n)


def _sig(v):
    return jax.nn.sigmoid(v)


def _dsilu(v, s):
    return s * (1.0 + v * (1.0 - s))


def _rms(v):
    return lax.rsqrt(jnp.mean(v * v, axis=-1, keepdims=True) + EPS)


def _norm_bwd(dn, n, r):
    return r * (dn - n * jnp.mean(dn * n, axis=-1, keepdims=True))


def _colsum(v):
    return jnp.sum(v, axis=0, keepdims=True)


class _Comm:
    def __init__(self, inputs, out_shapes, sems, start, finish, aliases=None, middle=None):
        self.inputs, self.out_shapes, self.sems = list(inputs), list(out_shapes), list(sems)
        self.start, self.finish, self.middle = start, finish, middle
        self.aliases = dict(aliases or {})


def _merge(*comms):
    comms = [c for c in comms if c is not None]
    if len(comms) <= 1:
        return comms[0] if comms else None
    spans, i0, o0, s0, aliases = [], 0, 0, 0, {}
    for c in comms:
        spans.append((i0, o0, s0))
        aliases.update({i0 + k: o0 + v for k, v in c.aliases.items()})
        i0, o0, s0 = i0 + len(c.inputs), o0 + len(c.out_shapes), s0 + len(c.sems)

    def run(which):
        def fn(ins, outs, sems):
            for c, (i, o, s) in zip(comms, spans):
                hook = getattr(c, which)
                if hook is not None:
                    hook(ins[i:i + len(c.inputs)], outs[o:o + len(c.out_shapes)], sems[s:s + len(c.sems)])
        return fn

    return _Comm([a for c in comms for a in c.inputs], [a for c in comms for a in c.out_shapes],
                 [a for c in comms for a in c.sems], run("start"), run("finish"), aliases,
                 run("middle") if any(c.middle is not None for c in comms) else None)


def _call(body, *, grid, in_specs, out_specs, out_shape, operands, name, params, scratch_shapes=(), comm=None,
          prefetch=None, own_copies_first=False):
    n_p = 0 if prefetch is None else 1
    n_i, n_o, n_s = len(in_specs), len(out_specs), len(scratch_shapes)
    if comm is None:
        comm = _Comm([], [], [], None, None)
    c_i, c_o = len(comm.inputs), len(comm.out_shapes)

    def carrier(*refs):
        pre, refs = refs[:n_p], refs[n_p:]
        ins, cins = refs[:n_i], refs[n_i:n_i + c_i]
        outs = refs[n_i + c_i:n_i + c_i + n_o]
        couts = refs[n_i + c_i + n_o:n_i + c_i + n_o + c_o]
        scr = refs[n_i + c_i + n_o + c_o:n_i + c_i + n_o + c_o + n_s]
        csems = refs[n_i + c_i + n_o + c_o + n_s:]
        ids = [pl.program_id(d) for d in range(len(grid))]
        first = ids[0] == 0
        half = ids[0] == grid[0] // 2
        last = ids[0] == grid[0] - 1
        for d in range(1, len(grid)):
            first = first & (ids[d] == 0)
            half = half & (ids[d] == 0)
            last = last & (ids[d] == grid[d] - 1)

        def start():
            if comm.start is not None:
                @pl.when(first)
                def _():
                    comm.start(cins, couts, csems)

        if not own_copies_first:
            start()
        if comm.middle is not None:
            assert grid[0] >= 2

            @pl.when(half)
            def _():
                comm.middle(cins, couts, csems)

        body(*pre, *ins, *outs, *scr)
        if own_copies_first:
            start()

        if comm.finish is not None:
            @pl.when(last)
            def _():
                comm.finish(cins, couts, csems)

    specs = dict(grid=grid, in_specs=list(in_specs) + [ANY] * c_i, out_specs=list(out_specs) + [ANY] * c_o,
                 scratch_shapes=list(scratch_shapes) + comm.sems)
    if n_p:
        specs = dict(grid_spec=pltpu.PrefetchScalarGridSpec(num_scalar_prefetch=1, **specs))
    res = pl.pallas_call(
        carrier, out_shape=list(out_shape) + comm.out_shapes,
        input_output_aliases={n_p + n_i + k: n_o + v for k, v in comm.aliases.items()},
        name=name, compiler_params=params, **specs)(*(() if prefetch is None else (prefetch,)), *operands, *comm.inputs)
    return list(res[:n_o]), list(res[n_o:])


def _run_comm(comm, name):
    c_i, c_o = len(comm.inputs), len(comm.out_shapes)

    def body(*refs):
        ins, outs, sems = refs[:c_i], refs[c_i:c_i + c_o], refs[c_i + c_o:]
        comm.start(ins, outs, sems)
        comm.finish(ins, outs, sems)

    res = pl.pallas_call(
        body, in_specs=[ANY] * c_i, out_specs=[ANY] * c_o, out_shape=comm.out_shapes, scratch_shapes=comm.sems,
        input_output_aliases=comm.aliases, name=name)(*comm.inputs)
    return list(res)


def _gather_matmul(order, x, g, shard, *, tm, name, comm=None):
    S, K = x.shape
    nb = shard.shape[1]
    n_i = S // tm

    def body(order_ref, x_ref, g_ref, shard_ref, p_ref, h_ref, full_ref, hbuf, wbuf, stage, send_sems, recv_sems,
             dma_sems):
        j, i = pl.program_id(0), pl.program_id(1)
        px, py, pc = _place()
        cps = _gather_copies(stage, full_ref, 1, nb, send_sems, recv_sems, 0)
        own = pltpu.make_async_copy(stage, _piece(full_ref, 1, nb, 4 * px + 2 * py + pc), dma_sems.at[0])
        keep_h = pltpu.make_async_copy(hbuf, h_ref, dma_sems.at[2])

        def load(src, dst):
            cp = pltpu.make_async_copy(src, dst, dma_sems.at[1])
            cp.start()
            cp.wait()

        def load_pair(qx, qy):
            load(_piece(full_ref, 1, 2 * nb, 2 * qx + qy), wbuf)

        @pl.when((j == 0) & (i == 0))
        def _():
            load(shard_ref, stage)
            own.start()
            for k in (0, 1, 2):
                cps[k].start()

        @pl.when(j == 0)
        def _():
            xx = x_ref[...]
            hbuf[i] = ((xx * _rms(xx)) * g_ref[...]).astype(BF16)

        @pl.when((j == 0) & (i == 0))
        def _():
            own.wait()
            cps[0].wait_recv()
            load_pair(px, py)

        @pl.when((j == 1) & (i == 0))
        def _():
            keep_h.start()
            cps[1].wait_recv()
            cps[3].start()
            cps[5].start()
            cps[2].wait_recv()
            cps[4].start()
            cps[6].start()
            cps[5].wait_recv()
            load_pair(1 - px, py)

        @pl.when((j == 2) & (i == 0))
        def _():
            cps[6].wait_recv()
            load_pair(px, 1 - py)

        @pl.when((j == 3) & (i == 0))
        def _():
            cps[3].wait_recv()
            cps[4].wait_recv()
            cps[7].start()
            cps[7].wait_recv()
            load_pair(1 - px, 1 - py)

        p_ref[...] = jnp.dot(hbuf[i], wbuf[...], preferred_element_type=F32).astype(BF16)

        @pl.when((j == 3) & (i == n_i - 1))
        def _():
            for cp in cps:
                cp.wait_send()
            keep_h.wait()

    first_pass = lambda j, i, o: (jnp.where(j == 0, i, n_i - 1), 0)
    outs, extra = _call(
        body, grid=(4, n_i), prefetch=order,
        in_specs=[pl.BlockSpec((tm, K), first_pass), pl.BlockSpec((1, K), lambda j, i, o: (0, 0)), ANY],
        out_specs=[pl.BlockSpec((tm, 2 * nb), lambda j, i, o: (i, o[j])), ANY, ANY],
        out_shape=[_sds((S, N_DEV * nb), BF16), _sds((n_i, tm, K), BF16), _sds((K, N_DEV * nb), BF16)],
        operands=(x, g, shard),
        scratch_shapes=[pltpu.VMEM((n_i, tm, K), BF16), pltpu.VMEM((K, 2 * nb), BF16), pltpu.VMEM((K, nb), BF16),
                        pltpu.SemaphoreType.DMA((N_GATHER,)), pltpu.SemaphoreType.DMA((N_GATHER,)),
                        pltpu.SemaphoreType.DMA((3,))],
        name=name, params=_params(("arbitrary", "arbitrary"), 58), comm=comm, own_copies_first=True)
    return outs[0], outs[1].reshape(S, K), outs[2], extra


def _out_norm_res(u, w, x, g, *, tm, name, comm=None):
    S, K = u.shape
    D = w.shape[1]

    def body(u_ref, w_ref, x_ref, g_ref, x1_ref, y_ref):
        y = jnp.dot(u_ref[...], w_ref[...], preferred_element_type=F32)
        y_ref[...] = y.astype(BF16)
        x1_ref[...] = x_ref[...] + (y * _rms(y)) * g_ref[...]

    return _call(
        body, grid=(S // tm,),
        in_specs=[pl.BlockSpec((tm, K), lambda i: (i, 0)), _const((K, D), single=True),
                  pl.BlockSpec((tm, D), lambda i: (i, 0)), _const((1, D))],
        out_specs=[pl.BlockSpec((tm, D), lambda i: (i, 0)), pl.BlockSpec((tm, D), lambda i: (i, 0))],
        out_shape=[_sds((S, D), F32), _sds((S, D), BF16)], operands=(u, w, x, g),
        name=name, params=_params(("arbitrary",), 56), comm=comm)


def _out_loss(yy, w, x1, g, tgt, *, tm, name):
    S, K = yy.shape
    D = w.shape[1]

    def body(yy_ref, w_ref, x1_ref, g_ref, t_ref, dout_ref, dx2_ref, dyy_ref, lcol_ref, dg_ref):
        out = jnp.dot(yy_ref[...], w_ref[...], preferred_element_type=F32)
        r = _rms(out)
        n = out * r
        gg = g_ref[...]
        e = x1_ref[...] + n * gg - t_ref[...]
        dx2 = e * (1.0 / D)
        dx2_ref[...] = dx2
        dout = _norm_bwd(dx2 * gg, n, r).astype(BF16)
        dout_ref[...] = dout
        dyy_ref[...] = lax.dot_general(dout, w_ref[...], (((1,), (1,)), ((), ())),
                                       preferred_element_type=F32).astype(BF16)

        @pl.when(pl.program_id(0) == 0)
        def _():
            lcol_ref[...] = jnp.zeros_like(lcol_ref)
            dg_ref[...] = jnp.zeros_like(dg_ref)

        lcol_ref[...] += _colsum(e * e)
        dg_ref[...] += _colsum(dx2 * n)

    return _call(
        body, grid=(S // tm,),
        in_specs=[pl.BlockSpec((tm, K), lambda i: (i, 0)), _const((K, D), single=True),
                  pl.BlockSpec((tm, D), lambda i: (i, 0)), _const((1, D)),
                  pl.BlockSpec((tm, D), lambda i: (i, 0))],
        out_specs=[pl.BlockSpec((tm, D), lambda i: (i, 0)), pl.BlockSpec((tm, D), lambda i: (i, 0)),
                   pl.BlockSpec((tm, K), lambda i: (i, 0)), _const((1, D)), _const((1, D))],
        out_shape=[_sds((S, D), BF16), _sds((S, D), F32), _sds((S, K), BF16), _sds((1, D), F32), _sds((1, D), F32)],
        operands=(yy, w, x1, g, tgt), name=name, params=_params(("arbitrary",), 52))[0]


def _mm_nt(a, w, *, tm, tk, name, comm=None):
    S, N = a.shape
    D = w.shape[0]
    n_k = N // tk

    def body(a_ref, w_ref, o_ref, acc_ref):
        k = pl.program_id(1)

        @pl.when(k == 0)
        def _():
            acc_ref[...] = jnp.zeros_like(acc_ref)

        acc_ref[...] = lax.dot_general(a_ref[...], w_ref[...], (((1,), (1,)), ((), ())),
                                       preferred_element_type=F32) + acc_ref[...]

        @pl.when(k == n_k - 1)
        def _():
            o_ref[...] = acc_ref[...].astype(BF16)

    outs, extra = _call(
        body, grid=(S // tm, n_k),
        in_specs=[pl.BlockSpec((tm, tk), lambda i, k: (i, k)), pl.BlockSpec((D, tk), lambda i, k: (0, k))],
        out_specs=[pl.BlockSpec((tm, D), lambda i, k: (i, 0))],
        out_shape=[_sds((S, D), BF16)], operands=(a, w),
        scratch_shapes=[pltpu.VMEM((tm, D), F32)],
        name=name, params=_params(("arbitrary", "arbitrary"), 48), comm=comm)
    return outs[0], extra


def _mm_tn(a, b, *, ts, tn, name, comm=None):
    S, M = a.shape
    N = b.shape[1]
    n_s = S // ts

    def body(a_ref, b_ref, o_ref, acc_ref):
        s = pl.program_id(1)

        @pl.when(s == 0)
        def _():
            acc_ref[...] = jnp.zeros_like(acc_ref)

        acc_ref[...] = lax.dot_general(a_ref[...], b_ref[...], (((0,), (0,)), ((), ())),
                                       preferred_element_type=F32) + acc_ref[...]

        @pl.when(s == n_s - 1)
        def _():
            o_ref[...] = acc_ref[...].astype(BF16)

    outs, extra = _call(
        body, grid=(N // tn, n_s),
        in_specs=[pl.BlockSpec((ts, M), lambda j, s: (s, 0)), pl.BlockSpec((ts, tn), lambda j, s: (s, j))],
        out_specs=[pl.BlockSpec((M, tn), lambda j, s: (0, j))],
        out_shape=[_sds((M, N), BF16)], operands=(a, b),
        scratch_shapes=[pltpu.VMEM((M, tn), F32)],
        name=name, params=_params(("arbitrary", "arbitrary"), 48), comm=comm)
    return outs[0], extra


def _pre_bwd_o(dh, x1, dx2, y0, g_pre, g_post, *, tm, name, comm=None):
    S, D = x1.shape

    def body(dh_ref, x1_ref, dx2_ref, y0_ref, gpre_ref, gpost_ref, dx1_ref, dy0_ref, dgpre_ref, dgpost_ref):
        @pl.when(pl.program_id(0) == 0)
        def _():
            dgpre_ref[...] = jnp.zeros_like(dgpre_ref)
            dgpost_ref[...] = jnp.zeros_like(dgpost_ref)

        dh = dh_ref[...].astype(F32)
        x1 = x1_ref[...]
        r2 = _rms(x1)
        xn = x1 * r2
        dgpre_ref[...] += _colsum(dh * xn)
        dx1 = dx2_ref[...] + _norm_bwd(dh * gpre_ref[...], xn, r2)
        dx1_ref[...] = dx1
        y = y0_ref[...].astype(F32)
        r1 = _rms(y)
        n1 = y * r1
        dgpost_ref[...] += _colsum(dx1 * n1)
        dy0_ref[...] = _norm_bwd(dx1 * gpost_ref[...], n1, r1).astype(BF16)

    row = pl.BlockSpec((tm, D), lambda i: (i, 0))
    return _call(
        body, grid=(S // tm,),
        in_specs=[row, row, row, row, _const((1, D)), _const((1, D))],
        out_specs=[row, row, _const((1, D)), _const((1, D))],
        out_shape=[_sds((S, D), F32), _sds((S, D), BF16), _sds((1, D), F32), _sds((1, D), F32)],
        operands=(dh, x1, dx2, y0, g_pre, g_post),
        name=name, params=_params(("arbitrary",), 48), comm=comm)


def _pre_bwd_e(dh, x, dx1, g_pre, *, tm, name):
    S, D = x.shape

    def body(dh_ref, x_ref, dx1_ref, gpre_ref, gx_ref, dgpre_ref):
        @pl.when(pl.program_id(0) == 0)
        def _():
            dgpre_ref[...] = jnp.zeros_like(dgpre_ref)

        dh = dh_ref[...].astype(F32)
        xx = x_ref[...]
        r0 = _rms(xx)
        xn = xx * r0
        dgpre_ref[...] += _colsum(dh * xn)
        gx_ref[...] = dx1_ref[...] + _norm_bwd(dh * gpre_ref[...], xn, r0)

    row = pl.BlockSpec((tm, D), lambda i: (i, 0))
    return _call(
        body, grid=(S // tm,),
        in_specs=[row, row, row, _const((1, D))],
        out_specs=[row, _const((1, D))],
        out_shape=[_sds((S, D), F32), _sds((1, D), F32)],
        operands=(dh, x, dx1, g_pre), name=name, params=_params(("arbitrary",), 56))[0]


SUBLANES = 8


def _shift_copies(sh_ref, ext_ref, cs):
    for b in range(1, SUBLANES):
        sh_ref[b - 1] = ext_ref[pl.ds(b, sh_ref.shape[1]), cs]


def _rows_at(ext_ref, sh_ref, off, cs, tm):
    b = off % SUBLANES
    if b == 0 or sh_ref is None:
        return ext_ref[pl.ds(off, tm), cs]
    return sh_ref[b - 1, pl.ds(off - b, tm), :]


def _taps(ext_ref, w_ref, n_taps, base, cs, tm, sh_ref=None):
    acc = _rows_at(ext_ref, sh_ref, base, cs, tm) * w_ref[0:1, cs]
    for k in range(1, n_taps):
        acc = acc + _rows_at(ext_ref, sh_ref, base + k, cs, tm) * w_ref[k:k + 1, cs]
    return acc


def _taps_rev(ext_ref, w_ref, n_taps, cs, tm, sh_ref=None):
    acc = _rows_at(ext_ref, sh_ref, n_taps - 1, cs, tm) * w_ref[0:1, cs]
    for k in range(1, n_taps):
        acc = acc + _rows_at(ext_ref, sh_ref, n_taps - 1 - k, cs, tm) * w_ref[k:k + 1, cs]
    return acc


def _e_mix_fwd(p, wa, wb, bias, ln_g, ln_b, *, tm, name, comm=None):
    S = p.shape[0]
    W = p.shape[1] // 7
    nb = tm // HALO
    chunks = [slice(c * LANES, (c + 1) * LANES) for c in range(W // LANES)]

    def body(p_ref, hax_ref, hac_ref, hbv_ref, hbg_ref, wa_ref, wb_ref, bias_ref, lg_ref, lb_ref,
             u_ref, cb_ref, ext_ref, sh_ref):
        keep = (pl.program_id(0) > 0).astype(F32)
        col = lambda j, cs: p_ref[:, j * W + cs.start:j * W + cs.stop].astype(F32)

        ext_ref[0:HALO, :] = hax_ref[...].astype(F32) * hac_ref[...].astype(F32) * keep
        ext_ref[HALO:, :] = p_ref[:, 2 * W:3 * W].astype(F32) * p_ref[:, 0:W].astype(F32)
        for cs in chunks:
            conv = _taps(ext_ref, wa_ref, CONV_A, HALO - (CONV_A - 1), cs, tm)
            az = col(3, cs)
            u_ref[:, cs] = (col(1, cs) * conv * (az * _sig(az))).astype(BF16)

        ext_ref[0:HALO, :] = hbv_ref[...].astype(F32) * _sig(hbg_ref[...].astype(F32)) * keep
        ext_ref[HALO:, :] = p_ref[:, 4 * W:5 * W].astype(F32) * _sig(p_ref[:, 5 * W:6 * W].astype(F32))
        s1 = jnp.zeros((tm, LANES), F32)
        for cs in chunks:
            _shift_copies(sh_ref, ext_ref, cs)
            cb = _taps(ext_ref, wb_ref, CONV_B, HALO - (CONV_B - 1), cs, tm, sh_ref) + bias_ref[:, cs]
            cb_ref[:, cs] = cb
            s1 = s1 + cb
        mu = jnp.sum(s1, axis=-1, keepdims=True) * (1.0 / W)
        s2 = jnp.zeros((tm, LANES), F32)
        for cs in chunks:
            xc = cb_ref[:, cs] - mu
            s2 = s2 + xc * xc
        rs = lax.rsqrt(jnp.sum(s2, axis=-1, keepdims=True) * (1.0 / W) + EPS)
        for cs in chunks:
            lb = (cb_ref[:, cs] - mu) * rs * lg_ref[:, cs] + lb_ref[:, cs]
            bz = col(6, cs)
            u_ref[:, W + cs.start:W + cs.stop] = (lb * _sig(lb) * (bz * _sig(bz))).astype(BF16)

    prev = lambda j: pl.BlockSpec((HALO, W), lambda i: (jnp.maximum(i * nb - 1, 0), j))
    return _call(
        body, grid=(S // tm,),
        in_specs=[pl.BlockSpec((tm, 7 * W), lambda i: (i, 0)), prev(0), prev(2), prev(4), prev(5),
                  _const((CONV_A, W)), _const((CONV_B, W)), _const((1, W)), _const((1, W)), _const((1, W))],
        out_specs=[pl.BlockSpec((tm, 2 * W), lambda i: (i, 0)), pl.BlockSpec((tm, W), lambda i: (i, 0))],
        out_shape=[_sds((S, 2 * W), BF16), _sds((S, W), F32)],
        operands=(p, p, p, p, p, wa, wb, bias, ln_g, ln_b),
        scratch_shapes=[pltpu.VMEM((HALO + tm, W), F32),
                        pltpu.VMEM((SUBLANES - 1, HALO + tm - SUBLANES, LANES), F32)],
        name=name, params=_params(("arbitrary",), 48), comm=comm)


def _e_mix_bwd(du, p, cb, wa, wb, ln_g, ln_b, *, tm, name, comm=None):
    S = p.shape[0]
    W = p.shape[1] // 7
    nb = tm // HALO
    n_t = S // tm
    last_blk = S // HALO - 1
    chunks = [slice(c * LANES, (c + 1) * LANES) for c in range(W // LANES)]

    def body(du_ref, duf_ref, p_ref, fab_ref, faz_ref, fbz_ref, hax_ref, hac_ref, hbv_ref, hbg_ref,
             cb_ref, cbf_ref, wa_ref, wb_ref, lg_ref, lb_ref,
             dp_ref, dwa_ref, dwb_ref, dbias_ref, dlg_ref, dlb_ref, extd_ref, extg_ref, shd_ref, shg_ref):
        i = pl.program_id(0)
        keep_prev = (i > 0).astype(F32)
        keep_next = (i < n_t - 1).astype(F32)
        col = lambda j, cs: p_ref[:, j * W + cs.start:j * W + cs.stop].astype(F32)

        @pl.when(i == 0)
        def _():
            dwa_ref[...] = jnp.zeros_like(dwa_ref)
            dwb_ref[...] = jnp.zeros_like(dwb_ref)
            dbias_ref[...] = jnp.zeros_like(dbias_ref)
            dlg_ref[...] = jnp.zeros_like(dlg_ref)
            dlb_ref[...] = jnp.zeros_like(dlb_ref)

        def dcb_rows(rows, cb_rows_ref, dub, bz_of, dst0, scale, main):
            cbv = cb_rows_ref[...]
            mu = jnp.mean(cbv, axis=-1, keepdims=True)
            xc = cbv - mu
            rs = lax.rsqrt(jnp.mean(xc * xc, axis=-1, keepdims=True) + EPS)
            m1 = jnp.zeros((rows, LANES), F32)
            m2 = jnp.zeros((rows, LANES), F32)
            for cs in chunks:
                nbv = (cb_rows_ref[:, cs] - mu) * rs
                lb = nbv * lg_ref[:, cs] + lb_ref[:, cs]
                sl = _sig(lb)
                bz = bz_of(cs)
                sz = _sig(bz)
                dub_c = dub(cs)
                dlb = dub_c * (bz * sz) * _dsilu(lb, sl)
                if main:
                    dlg_ref[:, cs] += _colsum(dlb * nbv)
                    dlb_ref[:, cs] += _colsum(dlb)
                    dp_ref[:, 6 * W + cs.start:6 * W + cs.stop] = (dub_c * (lb * sl) * _dsilu(bz, sz)).astype(BF16)
                dnb = dlb * lg_ref[:, cs]
                extd_ref[dst0:dst0 + rows, cs] = dnb
                m1 = m1 + dnb
                m2 = m2 + dnb * nbv
            m1 = jnp.sum(m1, axis=-1, keepdims=True) * (1.0 / W)
            m2 = jnp.sum(m2, axis=-1, keepdims=True) * (1.0 / W)
            for cs in chunks:
                nbv = (cb_rows_ref[:, cs] - mu) * rs
                dcb = rs * (extd_ref[dst0:dst0 + rows, cs] - m1 - nbv * m2) * scale
                extd_ref[dst0:dst0 + rows, cs] = dcb
                if main:
                    dbias_ref[:, cs] += _colsum(dcb)

        dcb_rows(tm, cb_ref, lambda cs: du_ref[:, W + cs.start:W + cs.stop].astype(F32),
                 lambda cs: col(6, cs), 0, 1.0, True)
        dcb_rows(HALO, cbf_ref, lambda cs: duf_ref[:, W + cs.start:W + cs.stop].astype(F32),
                 lambda cs: fbz_ref[:, cs].astype(F32), tm, keep_next, False)

        extg_ref[0:HALO, :] = hbv_ref[...].astype(F32) * _sig(hbg_ref[...].astype(F32)) * keep_prev
        extg_ref[HALO:, :] = p_ref[:, 4 * W:5 * W].astype(F32) * _sig(p_ref[:, 5 * W:6 * W].astype(F32))
        base_b = HALO - (CONV_B - 1)
        for cs in chunks:
            _shift_copies(shd_ref, extd_ref, cs)
            _shift_copies(shg_ref, extg_ref, cs)
            dgb = _taps_rev(extd_ref, wb_ref, CONV_B, cs, tm, shd_ref)
            bv = col(4, cs)
            sg = _sig(col(5, cs))
            dp_ref[:, 4 * W + cs.start:4 * W + cs.stop] = (dgb * sg).astype(BF16)
            dp_ref[:, 5 * W + cs.start:5 * W + cs.stop] = (dgb * bv * sg * (1.0 - sg)).astype(BF16)
            dcb = extd_ref[0:tm, cs]
            for k in range(CONV_B):
                dwb_ref[k:k + 1, cs] += _colsum(dcb * _rows_at(extg_ref, shg_ref, base_b + k, cs, tm))

        extg_ref[0:HALO, :] = hax_ref[...].astype(F32) * hac_ref[...].astype(F32) * keep_prev
        extg_ref[HALO:, :] = p_ref[:, 2 * W:3 * W].astype(F32) * p_ref[:, 0:W].astype(F32)
        base_a = HALO - (CONV_A - 1)
        for cs in chunks:
            conv = _taps(extg_ref, wa_ref, CONV_A, base_a, cs, tm)
            az = col(3, cs)
            sz = _sig(az)
            ab = col(1, cs)
            dua = du_ref[:, cs].astype(F32)
            dya = dua * (az * sz)
            dp_ref[:, W + cs.start:W + cs.stop] = (dya * conv).astype(BF16)
            dp_ref[:, 3 * W + cs.start:3 * W + cs.stop] = (dua * (ab * conv) * _dsilu(az, sz)).astype(BF16)
            extd_ref[0:tm, cs] = dya * ab
            azf = faz_ref[:, cs].astype(F32)
            extd_ref[tm:tm + HALO, cs] = (duf_ref[:, cs].astype(F32) * (azf * _sig(azf))
                                          * fab_ref[:, cs].astype(F32) * keep_next)
        for cs in chunks:
            dca = _taps_rev(extd_ref, wa_ref, CONV_A, cs, tm)
            dp_ref[:, cs] = (dca * col(2, cs)).astype(BF16)
            dp_ref[:, 2 * W + cs.start:2 * W + cs.stop] = (dca * col(0, cs)).astype(BF16)
            dconv = extd_ref[0:tm, cs]
            for k in range(CONV_A):
                dwa_ref[k:k + 1, cs] += _colsum(dconv * extg_ref[pl.ds(base_a + k, tm), cs])

    prev = lambda j: pl.BlockSpec((HALO, W), lambda i: (jnp.maximum(i * nb - 1, 0), j))
    nxt = lambda j, w: pl.BlockSpec((HALO, w), lambda i: (jnp.minimum((i + 1) * nb, last_blk), j))
    row = lambda w: pl.BlockSpec((tm, w), lambda i: (i, 0))
    return _call(
        body, grid=(n_t,),
        in_specs=[row(2 * W), nxt(0, 2 * W), row(7 * W), nxt(1, W), nxt(3, W), nxt(6, W),
                  prev(0), prev(2), prev(4), prev(5), row(W), nxt(0, W),
                  _const((CONV_A, W)), _const((CONV_B, W)), _const((1, W)), _const((1, W))],
        out_specs=[row(7 * W), _const((CONV_A, W)), _const((CONV_B, W)), _const((1, W)), _const((1, W)), _const((1, W))],
        out_shape=[_sds((S, 7 * W), BF16), _sds((CONV_A, W), F32), _sds((CONV_B, W), F32),
                   _sds((1, W), F32), _sds((1, W), F32), _sds((1, W), F32)],
        operands=(du, du, p, p, p, p, p, p, p, p, cb, cb, wa, wb, ln_g, ln_b),
        scratch_shapes=[pltpu.VMEM((tm + HALO, W), F32), pltpu.VMEM((HALO + tm, W), F32),
                        pltpu.VMEM((SUBLANES - 1, HALO + tm - SUBLANES, LANES), F32),
                        pltpu.VMEM((SUBLANES - 1, HALO + tm - SUBLANES, LANES), F32)],
        name=name, params=_params(("arbitrary",), 52), comm=comm)


def _counts(i, tm, rows, off, win):
    t = i * tm + off + lax.broadcasted_iota(jnp.int32, (rows, 1), 0)
    return jnp.minimum(t + 1, win).astype(F32)


def _o_mix_fwd(q, cw, cb, cscale, *, tm, name):
    S = q.shape[0]
    WC = q.shape[1] // 2
    NG = len(POOL_WINDOWS)
    G = WC // NG
    nb = tm // PHALO

    def body(v_ref, z_ref, hv_ref, cw_ref, cb_ref, sc_ref, yy_ref, pooled_ref, gg_ref, ext_ref):
        i = pl.program_id(0)
        keep = (i > 0).astype(F32)
        for g, win in enumerate(POOL_WINDOWS):
            cs = slice(g * G, (g + 1) * G)
            v = v_ref[:, cs].astype(F32)
            ext_ref[0:PHALO, :] = hv_ref[:, cs].astype(F32) * keep
            ext_ref[PHALO:, :] = v
            s = v
            for j in range(1, win):
                s = s + ext_ref[pl.ds(PHALO - j, tm), :]
            pooled = (s / _counts(i, tm, tm, 0, win) - v).astype(BF16)
            pooled_ref[:, cs] = pooled
            gg = jnp.dot(pooled, cw_ref[g], preferred_element_type=F32) + cb_ref[:, cs]
            gg_ref[:, cs] = gg.astype(BF16)
            z = z_ref[:, cs].astype(F32)
            yy_ref[:, cs] = (gg * sc_ref[:, cs] * (z * _sig(z))).astype(BF16)

    row = lambda j: pl.BlockSpec((tm, WC), lambda i: (i, j))
    out = pl.BlockSpec((tm, WC), lambda i: (i, 0))
    return _call(
        body, grid=(S // tm,),
        in_specs=[row(0), row(1), pl.BlockSpec((PHALO, WC), lambda i: (jnp.maximum(i * nb - 1, 0), 0)),
                  _const((NG, G, G)), _const((1, WC)), _const((1, WC))],
        out_specs=[out, out, out],
        out_shape=[_sds((S, WC), BF16)] * 3, operands=(q, q, q, cw, cb, cscale),
        scratch_shapes=[pltpu.VMEM((PHALO + tm, G), F32)],
        name=name, params=_params(("arbitrary",), 40))[0]


def _o_mix_bwd(dyy, q, gg, pooled, cw, cscale, *, tm, name):
    S = q.shape[0]
    WC = q.shape[1] // 2
    NG = len(POOL_WINDOWS)
    G = WC // NG
    nb = tm // PHALO
    n_t = S // tm
    last_blk = S // PHALO - 1
    nt = (((1,), (1,)), ((), ()))
    tn = (((0,), (0,)), ((), ()))

    def body(dyy_ref, dyyf_ref, z_ref, zf_ref, gg_ref, pooled_ref, cw_ref, sc_ref,
             dq_ref, dcw_ref, dcb_ref, dsc_ref, ext_ref):
        i = pl.program_id(0)
        keep_next = (i < n_t - 1).astype(F32)

        @pl.when(i == 0)
        def _():
            dcw_ref[...] = jnp.zeros_like(dcw_ref)
            dcb_ref[...] = jnp.zeros_like(dcb_ref)
            dsc_ref[...] = jnp.zeros_like(dsc_ref)

        for g, win in enumerate(POOL_WINDOWS):
            cs = slice(g * G, (g + 1) * G)
            sc = sc_ref[:, cs]
            z = z_ref[:, cs].astype(F32)
            sz = _sig(z)
            dyy_c = dyy_ref[:, cs].astype(F32)
            ggv = gg_ref[:, cs].astype(F32)
            dyy0 = dyy_c * (z * sz)
            dq_ref[:, WC + cs.start:WC + cs.stop] = (dyy_c * (ggv * sc) * _dsilu(z, sz)).astype(BF16)
            dgg = dyy0 * sc
            dsc_ref[:, cs] += _colsum(dyy0 * ggv)
            dcb_ref[:, cs] += _colsum(dgg)
            dgg_b = dgg.astype(BF16)
            dcw_ref[g] += lax.dot_general(pooled_ref[:, cs], dgg_b, tn, preferred_element_type=F32)
            dpool = lax.dot_general(dgg_b, cw_ref[g], nt, preferred_element_type=F32)
            zf = zf_ref[:, cs].astype(F32)
            dgg_f = (dyyf_ref[:, cs].astype(F32) * (zf * _sig(zf)) * sc * keep_next).astype(BF16)
            dpool_f = lax.dot_general(dgg_f, cw_ref[g], nt, preferred_element_type=F32)
            ext_ref[0:tm, :] = dpool / _counts(i, tm, tm, 0, win)
            ext_ref[tm:tm + PHALO, :] = dpool_f / _counts(i, tm, PHALO, tm, win)
            dv = ext_ref[0:tm, :] - dpool
            for j in range(1, win):
                dv = dv + ext_ref[pl.ds(j, tm), :]
            dq_ref[:, cs] = dv.astype(BF16)

    row = lambda: pl.BlockSpec((tm, WC), lambda i: (i, 0))
    nxt = lambda j: pl.BlockSpec((PHALO, WC), lambda i: (jnp.minimum((i + 1) * nb, last_blk), j))
    return _call(
        body, grid=(n_t,),
        in_specs=[row(), nxt(0), pl.BlockSpec((tm, WC), lambda i: (i, 1)), nxt(1), row(), row(),
                  _const((NG, G, G)), _const((1, WC))],
        out_specs=[pl.BlockSpec((tm, 2 * WC), lambda i: (i, 0)), _const((NG, G, G)), _const((1, WC)), _const((1, WC))],
        out_shape=[_sds((S, 2 * WC), BF16), _sds((NG, G, G), F32), _sds((1, WC), F32), _sds((1, WC), F32)],
        operands=(dyy, dyy, q, q, gg, pooled, cw, cscale),
        scratch_shapes=[pltpu.VMEM((tm + PHALO, G), F32)],
        name=name, params=_params(("arbitrary",), 48))[0]


def _place():
    return lax.axis_index("x"), lax.axis_index("y"), lax.axis_index("c")


def _piece(ref, axis, size, index):
    start = index * size
    if axis == len(ref.shape) - 1:
        start = pl.multiple_of(start, LANES)
    idx = [slice(None)] * len(ref.shape)
    idx[axis] = pl.ds(start, size)
    return ref.at[tuple(idx)]


def _gather_copies(src, out, axis, size, send_sems, recv_sems, base, held=None):
    x, y, c = _place()
    sib, xn, yn = (x, y, 1 - c), (1 - x, y, c), (x, 1 - y, c)

    def blk(px, py, of=out):
        return _piece(of, axis, size, 4 * px + 2 * py + c)

    def half(ref, h):
        n = ref.shape[0] // 2
        return ref.at[pl.ds(h * n, n)]

    def rc(k, s, d, to):
        return pltpu.make_async_remote_copy(src_ref=s, dst_ref=d, send_sem=send_sems.at[base + k],
                                            recv_sem=recv_sems.at[base + k], device_id=to, device_id_type=MESH)

    own, xb, yb, db = blk(x, y), blk(1 - x, y), blk(x, 1 - y), blk(1 - x, 1 - y)
    got = out if held is None else held
    xs, ys, ds = blk(1 - x, y, got), blk(x, 1 - y, got), blk(1 - x, 1 - y, got)
    return [rc(0, src, own, sib), rc(1, src, own, xn), rc(2, src, own, yn),
            rc(3, half(xs, 0), half(xb, 0), yn), rc(4, half(ys, 1), half(yb, 1), xn),
            rc(5, xs, xb, sib), rc(6, ys, yb, sib), rc(7, ds, db, sib)]


N_GATHER = 8


def _gather_comm(shards, axes, phases):
    n = len(shards)
    if phases == "second":
        sizes = [s.shape[a] // N_DEV for s, a in zip(shards, axes)]
        full = [_sds(s.shape, s.dtype) for s in shards]
    else:
        sizes = [s.shape[a] for s, a in zip(shards, axes)]
        full = [_sds(s.shape[:a] + (N_DEV * s.shape[a],) + s.shape[a + 1:], s.dtype) for s, a in zip(shards, axes)]

    def plan(ins, outs, sems):
        x, y, c = _place()
        me = 4 * x + 2 * y + c
        if phases == "second":
            cps = [_gather_copies(_piece(ins[t], axes[t], sizes[t], me), outs[t], axes[t], sizes[t], sems[0], sems[1],
                                  N_GATHER * t, ins[t]) for t in range(n)]
        else:
            cps = [_gather_copies(sems[3 + t], outs[t], axes[t], sizes[t], sems[0], sems[1], N_GATHER * t)
                   for t in range(n)]
        mine = [pltpu.make_async_copy(sems[3 + t], _piece(outs[t], axes[t], sizes[t], me), sems[2].at[t])
                for t in range(n)] if phases != "second" else []
        return cps, mine

    def send_own(ins, outs, sems):
        cps, mine = plan(ins, outs, sems)
        for t in range(n):
            stage = pltpu.make_async_copy(ins[t], sems[3 + t], sems[2].at[t])
            stage.start()
            stage.wait()
            mine[t].start()
            for k in (0, 1, 2):
                cps[t][k].start()

    def pass_on(ins, outs, sems):
        cps, _ = plan(ins, outs, sems)
        for t in range(n):
            if phases == "all":
                cps[t][1].wait_recv()
            cps[t][3].start()
            cps[t][5].start()
        for t in range(n):
            if phases == "all":
                cps[t][2].wait_recv()
            cps[t][4].start()
            cps[t][6].start()

    def own_landed(ins, outs, sems):
        cps, mine = plan(ins, outs, sems)
        for t in range(n):
            for k in (0, 1, 2):
                cps[t][k].wait()
            mine[t].wait()

    def all_landed(ins, outs, sems):
        cps, mine = plan(ins, outs, sems)
        for t in range(n):
            cps[t][3].wait_recv()
            cps[t][4].wait_recv()
            cps[t][7].start()
        for t in range(n):
            for k in ((0, 5, 6, 7) if phases == "all" else (5, 6, 7)):
                cps[t][k].wait_recv()
            for k in (range(N_GATHER) if phases == "all" else range(3, N_GATHER)):
                cps[t][k].wait_send()
            if phases == "all":
                mine[t].wait()

    sems = [pltpu.SemaphoreType.DMA((N_GATHER * n,)), pltpu.SemaphoreType.DMA((N_GATHER * n,))]
    if phases != "second":
        sems.append(pltpu.SemaphoreType.DMA((n,)))
        sems += [pltpu.VMEM(s.shape, s.dtype) for s in shards]
    if phases == "all":
        return _Comm(shards, full, sems, send_own, all_landed, middle=pass_on)
    if phases == "first":
        return _Comm(shards, full, sems, send_own, own_landed)
    return _Comm(shards, full, sems, pass_on, all_landed, aliases={t: t for t in range(n)})


def _pair_comm(grads, axes, sizes):
    n = len(grads)
    outs_sds = [_sds((4,) + g.shape[:a] + (s,) + g.shape[a + 1:], g.dtype) for g, a, s in zip(grads, axes, sizes)]

    def copies(ins, outs, sems):
        send_sems, recv_sems = sems
        x, y, c = _place()
        return [pltpu.make_async_remote_copy(
            src_ref=_piece(ins[t], axes[t], sizes[t], 2 * qi + (1 - c)), dst_ref=outs[t].at[qi],
            send_sem=send_sems.at[4 * t + qi], recv_sem=recv_sems.at[4 * t + qi],
            device_id=(x, y, 1 - c), device_id_type=MESH) for t in range(n) for qi in range(4)]

    def start(ins, outs, sems):
        for cp in copies(ins, outs, sems):
            cp.start()

    def finish(ins, outs, sems):
        for cp in copies(ins, outs, sems):
            cp.wait()

    sems = [pltpu.SemaphoreType.DMA((4 * n,)), pltpu.SemaphoreType.DMA((4 * n,))]
    return _Comm(grads, outs_sds, sems, start, finish)


def _chip_comm(sums):
    n = len(sums)
    outs_sds = [_sds((3,) + s.shape[1:], s.dtype) for s in sums]

    def copies(ins, outs, sems):
        send_sems, recv_sems = sems
        x, y, c = _place()
        return [pltpu.make_async_remote_copy(
            src_ref=ins[t].at[2 * qx + qy], dst_ref=outs[t].at[j],
            send_sem=send_sems.at[3 * t + j], recv_sem=recv_sems.at[3 * t + j],
            device_id=(qx, qy, c), device_id_type=MESH)
            for t in range(n) for j, (qx, qy) in enumerate([(1 - x, y), (x, 1 - y), (1 - x, 1 - y)])]

    def start(ins, outs, sems):
        for cp in copies(ins, outs, sems):
            cp.start()

    def finish(ins, outs, sems):
        for cp in copies(ins, outs, sems):
            cp.wait()

    sems = [pltpu.SemaphoreType.DMA((3 * n,)), pltpu.SemaphoreType.DMA((3 * n,))]
    return _Comm(sums, outs_sds, sems, start, finish)


def _small_comm(small):
    def copies(ins, outs, sems):
        send_sems, recv_sems, local_sem = sems
        x, y, c = _place()
        mine = outs[0].at[4 * x + 2 * y + c]
        out = [pltpu.make_async_copy(ins[0], mine, local_sem.at[0])]
        for k in range(1, N_DEV):
            peer = (1 - x if k & 4 else x, 1 - y if k & 2 else y, 1 - c if k & 1 else c)
            out.append(pltpu.make_async_remote_copy(
                src_ref=ins[0], dst_ref=mine, send_sem=send_sems.at[k - 1], recv_sem=recv_sems.at[k - 1],
                device_id=peer, device_id_type=MESH))
        return out

    def start(ins, outs, sems):
        for cp in copies(ins, outs, sems):
            cp.start()

    def finish(ins, outs, sems):
        for cp in copies(ins, outs, sems):
            cp.wait()

    sems = [pltpu.SemaphoreType.DMA((N_DEV - 1,)), pltpu.SemaphoreType.DMA((N_DEV - 1,)), pltpu.SemaphoreType.DMA((1,))]
    return _Comm([small], [_sds((N_DEV,) + small.shape, small.dtype)], sems, start, finish)


def _small_scatter_comm(send):
    def copies(ins, outs, sems):
        send_sems, recv_sems, local_sem = sems
        x, y, c = _place()
        me = 4 * x + 2 * y + c
        out = [pltpu.make_async_copy(ins[0].at[me], outs[0].at[me], local_sem.at[0])]
        for k in range(1, N_DEV):
            px, py, pc = (1 - x if k & 4 else x, 1 - y if k & 2 else y, 1 - c if k & 1 else c)
            out.append(pltpu.make_async_remote_copy(
                src_ref=ins[0].at[4 * px + 2 * py + pc], dst_ref=outs[0].at[me], send_sem=send_sems.at[k - 1],
                recv_sem=recv_sems.at[k - 1], device_id=(px, py, pc), device_id_type=MESH))
        return out

    def start(ins, outs, sems):
        for cp in copies(ins, outs, sems):
            cp.start()

    def finish(ins, outs, sems):
        for cp in copies(ins, outs, sems):
            cp.wait()

    sems = [pltpu.SemaphoreType.DMA((N_DEV - 1,)), pltpu.SemaphoreType.DMA((N_DEV - 1,)), pltpu.SemaphoreType.DMA((1,))]
    return _Comm([send], [_sds(send.shape, send.dtype)], sems, start, finish)


def _pair_sum(c_idx, grad, recv, axis, size, split, *, name):
    nd = len(grad.shape)
    piece = grad.shape[:axis] + (size,) + grad.shape[axis + 1:]
    blk = (piece[0] // split,) + piece[1:]

    def g_map(q, r, c_ref):
        idx = [0] * nd
        idx[axis] = 2 * q + c_ref[0]
        idx[0] = idx[0] * split + r if axis == 0 else r
        return tuple(idx)

    def r_map(q, r, c_ref):
        return (q, r) + (0,) * (nd - 1)

    def body(c_ref, g_ref, r_ref, o_ref):
        o_ref[0] = (g_ref[...].astype(F32) + r_ref[0].astype(F32)).astype(BF16)

    return _call(
        body, grid=(4, split), prefetch=c_idx,
        in_specs=[pl.BlockSpec(blk, g_map), pl.BlockSpec((1,) + blk, r_map)],
        out_specs=[pl.BlockSpec((1,) + blk, r_map)], out_shape=[_sds((4,) + piece, BF16)],
        operands=(grad, recv), name=name, params=_params(("arbitrary", "arbitrary"), 32))[0][0]


def _adam_math(w, g, m, v):
    m = ADAM_B1 * m + (1.0 - ADAM_B1) * g
    v = ADAM_B2 * v + (1.0 - ADAM_B2) * (g * g)
    m_hat = m / (1.0 - ADAM_B1 ** ADAM_STEP)
    v_hat = v / (1.0 - ADAM_B2 ** ADAM_STEP)
    delta = -ADAM_LR * (m_hat / (jnp.sqrt(v_hat) + ADAM_EPS) + ADAM_WD * w)
    return delta, m, v


def _adam_big(q_idx, sums, recv, w, m, v, split, *, name, comm=None):
    shape = w.shape
    nd = len(shape)
    blk = (shape[0] // split,) + shape[1:]
    w_map = lambda r, q_ref: (r,) + (0,) * (nd - 1)
    s_map = lambda r, q_ref: (q_ref[0], r) + (0,) * (nd - 1)
    r_map = lambda r, q_ref: (0, r) + (0,) * (nd - 1)

    def body(q_ref, s_ref, r_ref, w_ref, m_ref, v_ref, g_ref, d_ref, nm_ref, nv_ref):
        g = s_ref[0].astype(F32) + r_ref[0].astype(F32) + r_ref[1].astype(F32) + r_ref[2].astype(F32)
        g_ref[...] = g
        d_ref[...], nm_ref[...], nv_ref[...] = _adam_math(w_ref[...], g, m_ref[...], v_ref[...])

    wspec = pl.BlockSpec(blk, w_map)
    return _call(
        body, grid=(split,), prefetch=q_idx,
        in_specs=[pl.BlockSpec((1,) + blk, s_map), pl.BlockSpec((3,) + blk, r_map), wspec, wspec, wspec],
        out_specs=[wspec] * 4, out_shape=[_sds(shape, F32)] * 4, operands=(sums, recv, w, m, v),
        name=name, params=_params(("arbitrary",), 32), comm=comm)


def _adam_small(parts, w, m, v, *, name):
    R = w.shape[0]

    def body(p_ref, w_ref, m_ref, v_ref, g_ref, d_ref, nm_ref, nv_ref):
        g = p_ref[0]
        for d in range(1, N_DEV):
            g = g + p_ref[d]
        g_ref[...] = g
        d_ref[...], nm_ref[...], nv_ref[...] = _adam_math(w_ref[...], g, m_ref[...], v_ref[...])

    whole = _const((R, LANES))
    return _call(
        body, grid=(1,), in_specs=[_const((N_DEV, R, LANES)), whole, whole, whole], out_specs=[whole] * 4,
        out_shape=[_sds((R, LANES), F32)] * 4, operands=(parts, w, m, v), name=name,
        params=_params(("arbitrary",), 32))[0]


def _pack(arrs):
    return jnp.concatenate([a.reshape(-1) for a in arrs]).reshape(-1, LANES)


def _unpack(packed, shapes):
    flat = packed.reshape(-1)
    out, off = [], 0
    for s in shapes:
        n = 1
        for d in s:
            n *= d
        out.append(flat[off:off + n].reshape(s))
        off += n
    return out


BIG = ("e_in", "e_out", "o_in", "o_cw", "o_out")
BIG_AXIS = dict(e_in=1, e_out=0, o_in=1, o_cw=1, o_out=0)
BIG_SPLIT = dict(e_in=8, e_out=4, o_in=4, o_cw=4, o_out=4)
REPLICATED = ("e_norm_pre", "e_norm_post", "e_b_conv_bias", "e_b_ln_g", "e_b_ln_b")
SHARDED = ("e_a_conv", "e_b_conv", "o_norm_pre", "o_norm_post", "o_c_b", "o_c_scale")
SMALL = REPLICATED + SHARDED


class _Exchange:
    def __init__(self, shards, small, order, c_idx):
        self.shards = shards
        self.small = small
        self.order = order
        self.c_idx = c_idx
        self.reduced = {}

    def gather(self, keys):
        return _gather_comm([self.shards[k] for k in keys], [BIG_AXIS[k] for k in keys], "all")

    def gather1(self, keys):
        return _gather_comm([self.shards[k] for k in keys], [BIG_AXIS[k] for k in keys], "first")

    def gather2(self, keys, firsts):
        return _gather_comm(firsts, [BIG_AXIS[k] for k in keys], "second")

    def pair(self, grads):
        keys = list(grads)
        return _pair_comm([grads[k] for k in keys], [BIG_AXIS[k] for k in keys],
                          [grads[k].shape[BIG_AXIS[k]] // N_DEV for k in keys])

    def pair_sums(self, grads, received):
        return {k: _pair_sum(self.c_idx, grads[k], r, BIG_AXIS[k], grads[k].shape[BIG_AXIS[k]] // N_DEV,
                             BIG_SPLIT[k], name="pair_sum_" + k) for k, r in zip(grads, received)}

    def chips(self, sums):
        return _chip_comm([sums[k] for k in sums])

    def done(self, sums, received):
        self.reduced.update({k: (sums[k], r) for k, r in zip(sums, received)})


def _local_step(x, tgt, w_small, ex):
    S, D = x.shape
    tnt, tx, tw = min(TM_NT, S), min(TM_MIX, S), min(TM_WIDE, S)

    wt = {}
    p, h0, wt["e_in"], got = _gather_matmul(ex.order, x, w_small["e_norm_pre"], ex.shards["e_in"], tm=tw,
                                            name="e_in_fwd", comm=_small_comm(ex.small))
    per_dev = [_unpack(got[0][d], [w_small[k].shape for k in SHARDED]) for d in range(N_DEV)]
    sm = {k: w_small[k] for k in REPLICATED}
    for j, k in enumerate(SHARDED):
        sm[k] = jnp.concatenate([per_dev[d][j] for d in range(N_DEV)], axis=-1)
    n_groups = sm["o_c_b"].shape[0]
    sm["o_c_b"] = sm["o_c_b"].reshape(1, -1)

    W = p.shape[1] // 7
    (u, cb), got = _e_mix_fwd(p, sm["e_a_conv"], sm["e_b_conv"], sm["e_b_conv_bias"], sm["e_b_ln_g"],
                              sm["e_b_ln_b"], tm=tx, name="e_mix_fwd", comm=ex.gather(["e_out"]))
    wt["e_out"] = got[0]
    late = ["o_out", "o_cw"]
    (x1, y0), part = _out_norm_res(u, wt["e_out"], x, sm["e_norm_post"], tm=tw, name="e_out_fwd",
                                   comm=ex.gather1(late))
    q, h1, wt["o_in"], got = _gather_matmul(ex.order, x1, sm["o_norm_pre"], ex.shards["o_in"], tm=tw,
                                            name="o_in_fwd", comm=ex.gather2(late, part))
    wt.update(zip(late, got))
    yy, pooled, gg = _o_mix_fwd(q, wt["o_cw"], sm["o_c_b"], sm["o_c_scale"], tm=tw, name="o_mix_fwd")
    dout, dx2, dyy, lcol, dg_o_post = _out_loss(yy, wt["o_out"], x1, sm["o_norm_post"], tgt, tm=tx, name="o_out_loss")
    loss = (0.5 / D) * jnp.sum(lcol)

    dq, d_cw, d_cb, d_cscale = _o_mix_bwd(dyy, q, gg, pooled, wt["o_cw"], sm["o_c_scale"], tm=tw, name="o_mix_bwd")
    g_o_out, _ = _mm_tn(yy, dout, ts=tnt, tn=W, name="o_out_dw")
    ga = dict(o_out=g_o_out, o_cw=d_cw.astype(BF16))
    dh1, ra = _mm_nt(dq, wt["o_in"], tm=tnt, tk=W, name="o_in_bwd", comm=ex.pair(ga))
    sa = ex.pair_sums(ga, ra)
    (dx1, dy0, dg_o_pre, dg_e_post), ra = _pre_bwd_o(dh1, x1, dx2, y0, sm["o_norm_pre"], sm["e_norm_post"],
                                                     tm=tx, name="o_pre_bwd", comm=ex.chips(sa))
    ex.done(sa, ra)
    g_o_in, _ = _mm_tn(h1, dq, ts=tnt, tn=W, name="o_in_dw")
    gb = dict(o_in=g_o_in)
    du, rb = _mm_nt(dy0, wt["e_out"], tm=tnt, tk=W, name="e_out_bwd", comm=ex.pair(gb))
    sb = ex.pair_sums(gb, rb)
    g_e_out, _ = _mm_tn(u, dy0, ts=tnt, tn=W, name="e_out_dw")
    gc = dict(e_out=g_e_out)
    (dp, d_wa, d_wb, d_bias, d_lg, d_lb), rbc = _e_mix_bwd(
        du, p, cb, sm["e_a_conv"], sm["e_b_conv"], sm["e_b_ln_g"], sm["e_b_ln_b"], tm=tx, name="e_mix_bwd",
        comm=_merge(ex.chips(sb), ex.pair(gc)))
    ex.done(sb, rbc[:1])
    sc = ex.pair_sums(gc, rbc[1:])
    g_e_in, rc = _mm_tn(h0, dp, ts=tnt, tn=W, name="e_in_dw", comm=ex.chips(sc))
    ex.done(sc, rc)
    gd = dict(e_in=g_e_in)
    sd = ex.pair_sums(gd, _run_comm(ex.pair(gd), "pair_e_in"))
    dh0, rd = _mm_nt(dp, wt["e_in"], tm=tnt, tk=W, name="e_in_bwd", comm=ex.chips(sd))
    ex.done(sd, rd)
    grad_x, dg_e_pre = _pre_bwd_e(dh0, x, dx1, sm["e_norm_pre"], tm=tw, name="e_pre_bwd")

    small = dict(e_norm_pre=dg_e_pre, e_norm_post=dg_e_post, e_a_conv=d_wa, e_b_conv=d_wb, e_b_conv_bias=d_bias,
                 e_b_ln_g=d_lg, e_b_ln_b=d_lb, o_norm_pre=dg_o_pre, o_norm_post=dg_o_post,
                 o_c_b=d_cb.reshape(n_groups, -1), o_c_scale=d_cscale)
    return loss, grad_x, small


def kernel(x, e_norm_pre, e_norm_post, e_w_in, e_a_conv, e_b_conv, e_b_conv_bias, e_b_ln_g, e_b_ln_b, e_w_out, o_norm_pre, o_norm_post, o_w_in, o_c_w, o_c_b, o_c_scale, o_w_out, loss_target, m_e_norm_pre, m_e_norm_post, m_e_w_in, m_e_a_conv, m_e_b_conv, m_e_b_conv_bias, m_e_b_ln_g, m_e_b_ln_b, m_e_w_out, m_o_norm_pre, m_o_norm_post, m_o_w_in, m_o_c_w, m_o_c_b, m_o_c_scale, m_o_w_out, v_e_norm_pre, v_e_norm_post, v_e_w_in, v_e_a_conv, v_e_b_conv, v_e_b_conv_bias, v_e_b_ln_g, v_e_b_ln_b, v_e_w_out, v_o_norm_pre, v_o_norm_post, v_o_w_in, v_o_c_w, v_o_c_b, v_o_c_scale, v_o_w_out):
    xi, yi, ci = _place()
    w_big = dict(e_in=e_w_in[0], e_out=e_w_out[0], o_in=o_w_in[0], o_cw=o_c_w[0], o_out=o_w_out[0])
    m_big = dict(e_in=m_e_w_in[0], e_out=m_e_w_out[0], o_in=m_o_w_in[0], o_cw=m_o_c_w[0], o_out=m_o_w_out[0])
    v_big = dict(e_in=v_e_w_in[0], e_out=v_e_w_out[0], o_in=v_o_w_in[0], o_cw=v_o_c_w[0], o_out=v_o_w_out[0])
    w_small = dict(e_norm_pre=e_norm_pre, e_norm_post=e_norm_post, e_b_conv_bias=e_b_conv_bias, e_b_ln_g=e_b_ln_g,
                   e_b_ln_b=e_b_ln_b, e_a_conv=e_a_conv[0], e_b_conv=e_b_conv[0], o_norm_pre=o_norm_pre,
                   o_norm_post=o_norm_post, o_c_b=o_c_b[0], o_c_scale=o_c_scale)
    m_small = dict(e_norm_pre=m_e_norm_pre, e_norm_post=m_e_norm_post, e_b_conv_bias=m_e_b_conv_bias,
                   e_b_ln_g=m_e_b_ln_g, e_b_ln_b=m_e_b_ln_b, e_a_conv=m_e_a_conv[0], e_b_conv=m_e_b_conv[0],
                   o_norm_pre=m_o_norm_pre, o_norm_post=m_o_norm_post, o_c_b=m_o_c_b[0], o_c_scale=m_o_c_scale)
    v_small = dict(e_norm_pre=v_e_norm_pre, e_norm_post=v_e_norm_post, e_b_conv_bias=v_e_b_conv_bias,
                   e_b_ln_g=v_e_b_ln_g, e_b_ln_b=v_e_b_ln_b, e_a_conv=v_e_a_conv[0], e_b_conv=v_e_b_conv[0],
                   o_norm_pre=v_o_norm_pre, o_norm_post=v_o_norm_post, o_c_b=v_o_c_b[0], o_c_scale=v_o_c_scale)

    c_idx = jnp.reshape(ci, (1,)).astype(jnp.int32)
    order = jnp.stack([2 * xi + yi, 2 * (1 - xi) + yi, 2 * xi + (1 - yi), 2 * (1 - xi) + (1 - yi)]).astype(jnp.int32)
    ex = _Exchange({k: w_big[k].astype(BF16) for k in BIG}, _pack([w_small[k] for k in SHARDED]), order, c_idx)
    loss, grad_x, g_small = _local_step(x[0], loss_target[0], w_small, ex)

    q_idx = jnp.reshape(2 * xi + yi, (1,)).astype(jnp.int32)
    big_out = {k: _adam_big(q_idx, *ex.reduced[k], w_big[k], m_big[k], v_big[k], BIG_SPLIT[k], name="adam_" + k)[0]
               for k in BIG}

    rep = _pack([g_small[k] for k in REPLICATED])
    loss_row = jnp.pad(jnp.reshape(loss, (1, 1)), ((0, 0), (0, LANES - 1)))
    blocks = []
    for k in SHARDED:
        r, n = w_small[k].shape
        blocks.append(g_small[k].reshape(r, N_DEV, n).transpose(1, 0, 2).reshape(N_DEV, r * n))
    blocks = jnp.concatenate(blocks, axis=1).reshape(N_DEV, -1, LANES)
    head = jnp.concatenate([rep, loss_row], axis=0)
    send = jnp.concatenate([jnp.broadcast_to(head[None], (N_DEV,) + head.shape), blocks], axis=1)
    parts = _run_comm(_small_scatter_comm(send), "small_grad_exchange")[0]

    def own_rows(d):
        return jnp.concatenate([_pack([d[k] for k in REPLICATED]), jnp.ones((1, LANES), F32),
                                _pack([d[k] for k in SHARDED])], axis=0)

    res_small = _adam_small(parts, own_rows(w_small), own_rows(m_small), own_rows(v_small), name="adam_small")
    n_rep = rep.shape[0]
    loss = res_small[0][n_rep, 0]
    small_out = {k: [] for k in SMALL}
    for packed in res_small:
        for k, t in zip(REPLICATED, _unpack(packed[:n_rep], [w_small[k].shape for k in REPLICATED])):
            small_out[k].append(t)
        for k, t in zip(SHARDED, _unpack(packed[n_rep + 1:], [w_small[k].shape for k in SHARDED])):
            small_out[k].append(t)

    big_of = dict(e_w_in="e_in", e_w_out="e_out", o_w_in="o_in", o_c_w="o_cw", o_w_out="o_out")
    stacked = ("e_a_conv", "e_b_conv", "o_c_b")

    def leaf(name, which):
        if name in big_of:
            return big_out[big_of[name]][which][None]
        t = small_out[name][which]
        return t[None] if name in stacked else t

    order = ("e_norm_pre", "e_norm_post", "e_w_in", "e_a_conv", "e_b_conv", "e_b_conv_bias", "e_b_ln_g", "e_b_ln_b",
             "e_w_out", "o_norm_pre", "o_norm_post", "o_w_in", "o_c_w", "o_c_b", "o_c_scale", "o_w_out")
    outs = [loss, grad_x[None]]
    for which in range(4):
        outs += [leaf(nm, which) for nm in order]
    return tuple(outs)
```

```python
import jax
import jax.numpy as jnp
from jax import lax
from jax.experimental import pallas as pl
from jax.experimental.pallas import tpu as pltpu

F32 = jnp.float32
BF16 = jnp.bfloat16
EPS = 1e-6
MESH = pl.DeviceIdType.MESH
ANY = pl.BlockSpec(memory_space=pl.ANY)

N_DEV = 8
HALO = 32
PHALO = 16
CONV_A = 3
CONV_B = 31
POOL_WINDOWS = (2, 4, 8, 16)
LANES = 128
MIB = 1024 * 1024

ADAM_LR = 0.001
ADAM_B1 = 0.9
ADAM_B2 = 0.999
ADAM_EPS = 1e-08
ADAM_WD = 0.01
ADAM_STEP = 10

TM_NT = 1024
TM_MIX = 256
TM_WIDE = 512


def _sds(shape, dtype):
    return jax.ShapeDtypeStruct(tuple(shape), dtype)


def _params(sem, vmem_mib):
    return pltpu.CompilerParams(dimension_semantics=sem, vmem_limit_bytes=vmem_mib * MIB)


def _const(shape, single=False):
    n = len(shape)
    if single:
        return pl.BlockSpec(shape, lambda *_: (0,) * n, pipeline_mode=pl.Buffered(1))
    return pl.BlockSpec(shape, lambda *_: (0,) * n)


def _sig(v):
    return jax.nn.sigmoid(v)


def _dsilu(v, s):
    return s * (1.0 + v * (1.0 - s))


def _rms(v):
    return lax.rsqrt(jnp.mean(v * v, axis=-1, keepdims=True) + EPS)


def _norm_bwd(dn, n, r):
    return r * (dn - n * jnp.mean(dn * n, axis=-1, keepdims=True))


def _colsum(v):
    return jnp.sum(v, axis=0, keepdims=True)


class _Comm:
    def __init__(self, inputs, out_shapes, sems, start, finish, aliases=None, middle=None):
        self.inputs, self.out_shapes, self.sems = list(inputs), list(out_shapes), list(sems)
        self.start, self.finish, self.middle = start, finish, middle
        self.aliases = dict(aliases or {})


def _merge(*comms):
    comms = [c for c in comms if c is not None]
    if len(comms) <= 1:
        return comms[0] if comms else None
    spans, i0, o0, s0, aliases = [], 0, 0, 0, {}
    for c in comms:
        spans.append((i0, o0, s0))
        aliases.update({i0 + k: o0 + v for k, v in c.aliases.items()})
        i0, o0, s0 = i0 + len(c.inputs), o0 + len(c.out_shapes), s0 + len(c.sems)

    def run(which):
        def fn(ins, outs, sems):
            for c, (i, o, s) in zip(comms, spans):
                hook = getattr(c, which)
                if hook is not None:
                    hook(ins[i:i + len(c.inputs)], outs[o:o + len(c.out_shapes)], sems[s:s + len(c.sems)])
        return fn

    return _Comm([a for c in comms for a in c.inputs], [a for c in comms for a in c.out_shapes],
                 [a for c in comms for a in c.sems], run("start"), run("finish"), aliases,
                 run("middle") if any(c.middle is not None for c in comms) else None)


def _call(body, *, grid, in_specs, out_specs, out_shape, operands, name, params, scratch_shapes=(), comm=None,
          prefetch=None, own_copies_first=False):
    n_p = 0 if prefetch is None else 1
    n_i, n_o, n_s = len(in_specs), len(out_specs), len(scratch_shapes)
    if comm is None:
        comm = _Comm([], [], [], None, None)
    c_i, c_o = len(comm.inputs), len(comm.out_shapes)

    def carrier(*refs):
        pre, refs = refs[:n_p], refs[n_p:]
        ins, cins = refs[:n_i], refs[n_i:n_i + c_i]
        outs = refs[n_i + c_i:n_i + c_i + n_o]
        couts = refs[n_i + c_i + n_o:n_i + c_i + n_o + c_o]
        scr = refs[n_i + c_i + n_o + c_o:n_i + c_i + n_o + c_o + n_s]
        csems = refs[n_i + c_i + n_o + c_o + n_s:]
        ids = [pl.program_id(d) for d in range(len(grid))]
        first = ids[0] == 0
        half = ids[0] == grid[0] // 2
        last = ids[0] == grid[0] - 1
        for d in range(1, len(grid)):
            first = first & (ids[d] == 0)
            half = half & (ids[d] == 0)
            last = last & (ids[d] == grid[d] - 1)

        def start():
            if comm.start is not None:
                @pl.when(first)
                def _():
                    comm.start(cins, couts, csems)

        if not own_copies_first:
            start()
        if comm.middle is not None:
            assert grid[0] >= 2

            @pl.when(half)
            def _():
                comm.middle(cins, couts, csems)

        body(*pre, *ins, *outs, *scr)
        if own_copies_first:
            start()

        if comm.finish is not None:
            @pl.when(last)
            def _():
                comm.finish(cins, couts, csems)

    specs = dict(grid=grid, in_specs=list(in_specs) + [ANY] * c_i, out_specs=list(out_specs) + [ANY] * c_o,
                 scratch_shapes=list(scratch_shapes) + comm.sems)
    if n_p:
        specs = dict(grid_spec=pltpu.PrefetchScalarGridSpec(num_scalar_prefetch=1, **specs))
    res = pl.pallas_call(
        carrier, out_shape=list(out_shape) + comm.out_shapes,
        input_output_aliases={n_p + n_i + k: n_o + v for k, v in comm.aliases.items()},
        name=name, compiler_params=params, **specs)(*(() if prefetch is None else (prefetch,)), *operands, *comm.inputs)
    return list(res[:n_o]), list(res[n_o:])


def _run_comm(comm, name):
    c_i, c_o = len(comm.inputs), len(comm.out_shapes)

    def body(*refs):
        ins, outs, sems = refs[:c_i], refs[c_i:c_i + c_o], refs[c_i + c_o:]
        comm.start(ins, outs, sems)
        comm.finish(ins, outs, sems)

    res = pl.pallas_call(
        body, in_specs=[ANY] * c_i, out_specs=[ANY] * c_o, out_shape=comm.out_shapes, scratch_shapes=comm.sems,
        input_output_aliases=comm.aliases, name=name)(*comm.inputs)
    return list(res)


def _gather_matmul(order, x, g, shard, *, tm, name, comm=None):
    S, K = x.shape
    nb = shard.shape[1]
    n_i = S // tm

    def body(order_ref, x_ref, g_ref, shard_ref, p_ref, h_ref, full_ref, hbuf, wbuf, stage, send_sems, recv_sems,
             dma_sems):
        j, i = pl.program_id(0), pl.program_id(1)
        px, py, pc = _place()
        cps = _gather_copies(stage, full_ref, 1, nb, send_sems, recv_sems, 0)
        own = pltpu.make_async_copy(stage, _piece(full_ref, 1, nb, 4 * px + 2 * py + pc), dma_sems.at[0])
        keep_h = pltpu.make_async_copy(hbuf, h_ref, dma_sems.at[2])

        def load(src, dst):
            cp = pltpu.make_async_copy(src, dst, dma_sems.at[1])
            cp.start()
            cp.wait()

        def load_pair(qx, qy):
            load(_piece(full_ref, 1, 2 * nb, 2 * qx + qy), wbuf)

        @pl.when((j == 0) & (i == 0))
        def _():
            load(shard_ref, stage)
            own.start()
            for k in (0, 1, 2):
                cps[k].start()

        @pl.when(j == 0)
        def _():
            xx = x_ref[...]
            hbuf[i] = ((xx * _rms(xx)) * g_ref[...]).astype(BF16)

        @pl.when((j == 0) & (i == 0))
        def _():
            own.wait()
            cps[0].wait_recv()
            load_pair(px, py)

        @pl.when((j == 1) & (i == 0))
        def _():
            keep_h.start()
            cps[1].wait_recv()
            cps[3].start()
            cps[5].start()
            cps[2].wait_recv()
            cps[4].start()
            cps[6].start()
            cps[5].wait_recv()
            load_pair(1 - px, py)

        @pl.when((j == 2) & (i == 0))
        def _():
            cps[6].wait_recv()
            load_pair(px, 1 - py)

        @pl.when((j == 3) & (i == 0))
        def _():
            cps[3].wait_recv()
            cps[4].wait_recv()
            cps[7].start()
            cps[7].wait_recv()
            load_pair(1 - px, 1 - py)

        p_ref[...] = jnp.dot(hbuf[i], wbuf[...], preferred_element_type=F32).astype(BF16)

        @pl.when((j == 3) & (i == n_i - 1))
        def _():
            for cp in cps:
                cp.wait_send()
            keep_h.wait()

    first_pass = lambda j, i, o: (jnp.where(j == 0, i, n_i - 1), 0)
    outs, extra = _call(
        body, grid=(4, n_i), prefetch=order,
        in_specs=[pl.BlockSpec((tm, K), first_pass), pl.BlockSpec((1, K), lambda j, i, o: (0, 0)), ANY],
        out_specs=[pl.BlockSpec((tm, 2 * nb), lambda j, i, o: (i, o[j])), ANY, ANY],
        out_shape=[_sds((S, N_DEV * nb), BF16), _sds((n_i, tm, K), BF16), _sds((K, N_DEV * nb), BF16)],
        operands=(x, g, shard),
        scratch_shapes=[pltpu.VMEM((n_i, tm, K), BF16), pltpu.VMEM((K, 2 * nb), BF16), pltpu.VMEM((K, nb), BF16),
                        pltpu.SemaphoreType.DMA((N_GATHER,)), pltpu.SemaphoreType.DMA((N_GATHER,)),
                        pltpu.SemaphoreType.DMA((3,))],
        name=name, params=_params(("arbitrary", "arbitrary"), 58), comm=comm, own_copies_first=True)
    return outs[0], outs[1].reshape(S, K), outs[2], extra


def _out_norm_res(u, w, x, g, *, tm, name, comm=None):
    S, K = u.shape
    D = w.shape[1]

    def body(u_ref, w_ref, x_ref, g_ref, x1_ref, y_ref):
        y = jnp.dot(u_ref[...], w_ref[...], preferred_element_type=F32)
        y_ref[...] = y.astype(BF16)
        x1_ref[...] = x_ref[...] + (y * _rms(y)) * g_ref[...]

    return _call(
        body, grid=(S // tm,),
        in_specs=[pl.BlockSpec((tm, K), lambda i: (i, 0)), _const((K, D), single=True),
                  pl.BlockSpec((tm, D), lambda i: (i, 0)), _const((1, D))],
        out_specs=[pl.BlockSpec((tm, D), lambda i: (i, 0)), pl.BlockSpec((tm, D), lambda i: (i, 0))],
        out_shape=[_sds((S, D), F32), _sds((S, D), BF16)], operands=(u, w, x, g),
        name=name, params=_params(("arbitrary",), 56), comm=comm)


def _out_loss(yy, w, x1, g, tgt, *, tm, name):
    S, K = yy.shape
    D = w.shape[1]

    def body(yy_ref, w_ref, x1_ref, g_ref, t_ref, dout_ref, dx2_ref, dyy_ref, lcol_ref, dg_ref):
        out = jnp.dot(yy_ref[...], w_ref[...], preferred_element_type=F32)
        r = _rms(out)
        n = out * r
        gg = g_ref[...]
        e = x1_ref[...] + n * gg - t_ref[...]
        dx2 = e * (1.0 / D)
        dx2_ref[...] = dx2
        dout = _norm_bwd(dx2 * gg, n, r).astype(BF16)
        dout_ref[...] = dout
        dyy_ref[...] = lax.dot_general(dout, w_ref[...], (((1,), (1,)), ((), ())),
                                       preferred_element_type=F32).astype(BF16)

        @pl.when(pl.program_id(0) == 0)
        def _():
            lcol_ref[...] = jnp.zeros_like(lcol_ref)
            dg_ref[...] = jnp.zeros_like(dg_ref)

        lcol_ref[...] += _colsum(e * e)
        dg_ref[...] += _colsum(dx2 * n)

    return _call(
        body, grid=(S // tm,),
        in_specs=[pl.BlockSpec((tm, K), lambda i: (i, 0)), _const((K, D), single=True),
                  pl.BlockSpec((tm, D), lambda i: (i, 0)), _const((1, D)),
                  pl.BlockSpec((tm, D), lambda i: (i, 0))],
        out_specs=[pl.BlockSpec((tm, D), lambda i: (i, 0)), pl.BlockSpec((tm, D), lambda i: (i, 0)),
                   pl.BlockSpec((tm, K), lambda i: (i, 0)), _const((1, D)), _const((1, D))],
        out_shape=[_sds((S, D), BF16), _sds((S, D), F32), _sds((S, K), BF16), _sds((1, D), F32), _sds((1, D), F32)],
        operands=(yy, w, x1, g, tgt), name=name, params=_params(("arbitrary",), 52))[0]


def _mm_nt(a, w, *, tm, tk, name, comm=None):
    S, N = a.shape
    D = w.shape[0]
    n_k = N // tk

    def body(a_ref, w_ref, o_ref, acc_ref):
        k = pl.program_id(1)

        @pl.when(k == 0)
        def _():
            acc_ref[...] = jnp.zeros_like(acc_ref)

        acc_ref[...] = lax.dot_general(a_ref[...], w_ref[...], (((1,), (1,)), ((), ())),
                                       preferred_element_type=F32) + acc_ref[...]

        @pl.when(k == n_k - 1)
        def _():
            o_ref[...] = acc_ref[...].astype(BF16)

    outs, extra = _call(
        body, grid=(S // tm, n_k),
        in_specs=[pl.BlockSpec((tm, tk), lambda i, k: (i, k)), pl.BlockSpec((D, tk), lambda i, k: (0, k))],
        out_specs=[pl.BlockSpec((tm, D), lambda i, k: (i, 0))],
        out_shape=[_sds((S, D), BF16)], operands=(a, w),
        scratch_shapes=[pltpu.VMEM((tm, D), F32)],
        name=name, params=_params(("arbitrary", "arbitrary"), 48), comm=comm)
    return outs[0], extra


def _mm_tn(a, b, *, ts, tn, name, comm=None):
    S, M = a.shape
    N = b.shape[1]
    n_s = S // ts

    def body(a_ref, b_ref, o_ref, acc_ref):
        s = pl.program_id(1)

        @pl.when(s == 0)
        def _():
            acc_ref[...] = jnp.zeros_like(acc_ref)

        acc_ref[...] = lax.dot_general(a_ref[...], b_ref[...], (((0,), (0,)), ((), ())),
                                       preferred_element_type=F32) + acc_ref[...]

        @pl.when(s == n_s - 1)
        def _():
            o_ref[...] = acc_ref[...].astype(BF16)

    outs, extra = _call(
        body, grid=(N // tn, n_s),
        in_specs=[pl.BlockSpec((ts, M), lambda j, s: (s, 0)), pl.BlockSpec((ts, tn), lambda j, s: (s, j))],
        out_specs=[pl.BlockSpec((M, tn), lambda j, s: (0, j))],
        out_shape=[_sds((M, N), BF16)], operands=(a, b),
        scratch_shapes=[pltpu.VMEM((M, tn), F32)],
        name=name, params=_params(("arbitrary", "arbitrary"), 48), comm=comm)
    return outs[0], extra


def _dw_reduce(order, a, b, nb, *, ts, name, comm=None):
    S, M = a.shape
    n_s = S // ts
    rows = 512

    def body(order_ref, a_ref, b_ref, sums_ref, from_sib_ref, from_chip_ref, acc, send_buf, mine_buf, recv_buf,
             sib_send, sib_recv, chip_send, chip_recv, dma_sems):
        t, s = pl.program_id(0), pl.program_id(1)
        x, y, c = _place()
        targets = [(1 - x, y, c), (x, 1 - y, c)]

        def to_sibling(k):
            return pltpu.make_async_remote_copy(
                src_ref=send_buf, dst_ref=from_sib_ref.at[k], send_sem=sib_send.at[k], recv_sem=sib_recv.at[k],
                device_id=(x, y, 1 - c), device_id_type=MESH)

        def to_chip(k):
            return pltpu.make_async_remote_copy(
                src_ref=sums_ref.at[k], dst_ref=from_chip_ref.at[k], send_sem=chip_send.at[k],
                recv_sem=chip_recv.at[k], device_id=targets[k], device_id_type=MESH)

        def finish(k):
            to_sibling(k).wait()
            get = pltpu.make_async_copy(from_sib_ref.at[k], recv_buf, dma_sems.at[0])
            get.start()
            get.wait()
            for r in range(0, M, rows):
                recv_buf[r:r + rows, :] = (mine_buf[r:r + rows, :].astype(F32)
                                           + recv_buf[r:r + rows, :].astype(F32)).astype(BF16)
            put = pltpu.make_async_copy(recv_buf, sums_ref.at[k], dma_sems.at[1])
            put.start()
            put.wait()
            if k < 2:
                to_chip(k).start()

        for k in range(3):
            @pl.when((t == k + 1) & (s == 0))
            def _(k=k):
                finish(k)

        @pl.when(s == 0)
        def _():
            acc[...] = jnp.zeros_like(acc)

        acc[...] = lax.dot_general(a_ref[...], b_ref[...], (((0,), (0,)), ((), ())),
                                   preferred_element_type=F32) + acc[...]

        @pl.when(s == n_s - 1)
        def _():
            for r in range(0, M, rows):
                lo, hi = acc[r:r + rows, :nb], acc[r:r + rows, nb:]
                send_buf[r:r + rows, :] = jnp.where(c == 0, hi, lo).astype(BF16)
                mine_buf[r:r + rows, :] = jnp.where(c == 0, lo, hi).astype(BF16)
            to_sibling(t).start()

        @pl.when((t == 3) & (s == n_s - 1))
        def _():
            finish(3)
            to_chip(0).wait()
            to_chip(1).wait()

    piece = _sds((4, M, nb), BF16)
    outs, extra = _call(
        body, grid=(4, n_s), prefetch=order,
        in_specs=[pl.BlockSpec((ts, M), lambda t, s, o: (s, 0)), pl.BlockSpec((ts, 2 * nb), lambda t, s, o: (s, o[t]))],
        out_specs=[ANY, ANY, ANY], out_shape=[piece, piece, _sds((3, M, nb), BF16)], operands=(a, b),
        scratch_shapes=[pltpu.VMEM((M, 2 * nb), F32), pltpu.VMEM((M, nb), BF16), pltpu.VMEM((M, nb), BF16),
                        pltpu.VMEM((M, nb), BF16), pltpu.SemaphoreType.DMA((4,)), pltpu.SemaphoreType.DMA((4,)),
                        pltpu.SemaphoreType.DMA((2,)), pltpu.SemaphoreType.DMA((2,)), pltpu.SemaphoreType.DMA((2,))],
        name=name, params=_params(("arbitrary", "arbitrary"), 56), comm=comm)
    return outs[0], outs[2], extra


def _diag_comm(sums, from_chip):
    def copy(ins, outs, sems):
        x, y, c = _place()
        return pltpu.make_async_remote_copy(
            src_ref=ins[0].at[2], dst_ref=outs[0].at[2], send_sem=sems[0].at[0], recv_sem=sems[1].at[0],
            device_id=(1 - x, 1 - y, c), device_id_type=MESH)

    def start(ins, outs, sems):
        copy(ins, outs, sems).start()

    def finish(ins, outs, sems):
        copy(ins, outs, sems).wait()

    sems = [pltpu.SemaphoreType.DMA((1,)), pltpu.SemaphoreType.DMA((1,))]
    return _Comm([sums, from_chip], [_sds(from_chip.shape, from_chip.dtype)], sems, start, finish, aliases={1: 0})


def _pre_bwd_o(dh, x1, dx2, y0, g_pre, g_post, *, tm, name, comm=None):
    S, D = x1.shape

    def body(dh_ref, x1_ref, dx2_ref, y0_ref, gpre_ref, gpost_ref, dx1_ref, dy0_ref, dgpre_ref, dgpost_ref):
        @pl.when(pl.program_id(0) == 0)
        def _():
            dgpre_ref[...] = jnp.zeros_like(dgpre_ref)
            dgpost_ref[...] = jnp.zeros_like(dgpost_ref)

        dh = dh_ref[...].astype(F32)
        x1 = x1_ref[...]
        r2 = _rms(x1)
        xn = x1 * r2
        dgpre_ref[...] += _colsum(dh * xn)
        dx1 = dx2_ref[...] + _norm_bwd(dh * gpre_ref[...], xn, r2)
        dx1_ref[...] = dx1
        y = y0_ref[...].astype(F32)
        r1 = _rms(y)
        n1 = y * r1
        dgpost_ref[...] += _colsum(dx1 * n1)
        dy0_ref[...] = _norm_bwd(dx1 * gpost_ref[...], n1, r1).astype(BF16)

    row = pl.BlockSpec((tm, D), lambda i: (i, 0))
    return _call(
        body, grid=(S // tm,),
        in_specs=[row, row, row, row, _const((1, D)), _const((1, D))],
        out_specs=[row, row, _const((1, D)), _const((1, D))],
        out_shape=[_sds((S, D), F32), _sds((S, D), BF16), _sds((1, D), F32), _sds((1, D), F32)],
        operands=(dh, x1, dx2, y0, g_pre, g_post),
        name=name, params=_params(("arbitrary",), 48), comm=comm)


def _pre_bwd_e(dh, x, dx1, g_pre, *, tm, name):
    S, D = x.shape

    def body(dh_ref, x_ref, dx1_ref, gpre_ref, gx_ref, dgpre_ref):
        @pl.when(pl.program_id(0) == 0)
        def _():
            dgpre_ref[...] = jnp.zeros_like(dgpre_ref)

        dh = dh_ref[...].astype(F32)
        xx = x_ref[...]
        r0 = _rms(xx)
        xn = xx * r0
        dgpre_ref[...] += _colsum(dh * xn)
        gx_ref[...] = dx1_ref[...] + _norm_bwd(dh * gpre_ref[...], xn, r0)

    row = pl.BlockSpec((tm, D), lambda i: (i, 0))
    return _call(
        body, grid=(S // tm,),
        in_specs=[row, row, row, _const((1, D))],
        out_specs=[row, _const((1, D))],
        out_shape=[_sds((S, D), F32), _sds((1, D), F32)],
        operands=(dh, x, dx1, g_pre), name=name, params=_params(("arbitrary",), 56))[0]


SUBLANES = 8


def _shift_copies(sh_ref, ext_ref, cs):
    for b in range(1, SUBLANES):
        sh_ref[b - 1] = ext_ref[pl.ds(b, sh_ref.shape[1]), cs]


def _rows_at(ext_ref, sh_ref, off, cs, tm):
    b = off % SUBLANES
    if b == 0 or sh_ref is None:
        return ext_ref[pl.ds(off, tm), cs]
    return sh_ref[b - 1, pl.ds(off - b, tm), :]


def _taps(ext_ref, w_ref, n_taps, base, cs, tm, sh_ref=None):
    acc = _rows_at(ext_ref, sh_ref, base, cs, tm) * w_ref[0:1, cs]
    for k in range(1, n_taps):
        acc = acc + _rows_at(ext_ref, sh_ref, base + k, cs, tm) * w_ref[k:k + 1, cs]
    return acc


def _taps_rev(ext_ref, w_ref, n_taps, cs, tm, sh_ref=None):
    acc = _rows_at(ext_ref, sh_ref, n_taps - 1, cs, tm) * w_ref[0:1, cs]
    for k in range(1, n_taps):
        acc = acc + _rows_at(ext_ref, sh_ref, n_taps - 1 - k, cs, tm) * w_ref[k:k + 1, cs]
    return acc


def _e_mix_fwd(p, wa, wb, bias, ln_g, ln_b, *, tm, name, comm=None):
    S = p.shape[0]
    W = p.shape[1] // 7
    nb = tm // HALO
    chunks = [slice(c * LANES, (c + 1) * LANES) for c in range(W // LANES)]

    def body(p_ref, hax_ref, hac_ref, hbv_ref, hbg_ref, wa_ref, wb_ref, bias_ref, lg_ref, lb_ref,
             u_ref, cb_ref, ext_ref, sh_ref):
        keep = (pl.program_id(0) > 0).astype(F32)
        col = lambda j, cs: p_ref[:, j * W + cs.start:j * W + cs.stop].astype(F32)

        ext_ref[0:HALO, :] = hax_ref[...].astype(F32) * hac_ref[...].astype(F32) * keep
        ext_ref[HALO:, :] = p_ref[:, 2 * W:3 * W].astype(F32) * p_ref[:, 0:W].astype(F32)
        for cs in chunks:
            conv = _taps(ext_ref, wa_ref, CONV_A, HALO - (CONV_A - 1), cs, tm)
            az = col(3, cs)
            u_ref[:, cs] = (col(1, cs) * conv * (az * _sig(az))).astype(BF16)

        ext_ref[0:HALO, :] = hbv_ref[...].astype(F32) * _sig(hbg_ref[...].astype(F32)) * keep
        ext_ref[HALO:, :] = p_ref[:, 4 * W:5 * W].astype(F32) * _sig(p_ref[:, 5 * W:6 * W].astype(F32))
        s1 = jnp.zeros((tm, LANES), F32)
        for cs in chunks:
            _shift_copies(sh_ref, ext_ref, cs)
            cb = _taps(ext_ref, wb_ref, CONV_B, HALO - (CONV_B - 1), cs, tm, sh_ref) + bias_ref[:, cs]
            cb_ref[:, cs] = cb
            s1 = s1 + cb
        mu = jnp.sum(s1, axis=-1, keepdims=True) * (1.0 / W)
        s2 = jnp.zeros((tm, LANES), F32)
        for cs in chunks:
            xc = cb_ref[:, cs] - mu
            s2 = s2 + xc * xc
        rs = lax.rsqrt(jnp.sum(s2, axis=-1, keepdims=True) * (1.0 / W) + EPS)
        for cs in chunks:
            lb = (cb_ref[:, cs] - mu) * rs * lg_ref[:, cs] + lb_ref[:, cs]
            bz = col(6, cs)
            u_ref[:, W + cs.start:W + cs.stop] = (lb * _sig(lb) * (bz * _sig(bz))).astype(BF16)

    prev = lambda j: pl.BlockSpec((HALO, W), lambda i: (jnp.maximum(i * nb - 1, 0), j))
    return _call(
        body, grid=(S // tm,),
        in_specs=[pl.BlockSpec((tm, 7 * W), lambda i: (i, 0)), prev(0), prev(2), prev(4), prev(5),
                  _const((CONV_A, W)), _const((CONV_B, W)), _const((1, W)), _const((1, W)), _const((1, W))],
        out_specs=[pl.BlockSpec((tm, 2 * W), lambda i: (i, 0)), pl.BlockSpec((tm, W), lambda i: (i, 0))],
        out_shape=[_sds((S, 2 * W), BF16), _sds((S, W), F32)],
        operands=(p, p, p, p, p, wa, wb, bias, ln_g, ln_b),
        scratch_shapes=[pltpu.VMEM((HALO + tm, W), F32),
                        pltpu.VMEM((SUBLANES - 1, HALO + tm - SUBLANES, LANES), F32)],
        name=name, params=_params(("arbitrary",), 48), comm=comm)


def _e_mix_bwd(du, p, cb, wa, wb, ln_g, ln_b, *, tm, name, comm=None):
    S = p.shape[0]
    W = p.shape[1] // 7
    nb = tm // HALO
    n_t = S // tm
    last_blk = S // HALO - 1
    chunks = [slice(c * LANES, (c + 1) * LANES) for c in range(W // LANES)]

    def body(du_ref, duf_ref, p_ref, fab_ref, faz_ref, fbz_ref, hax_ref, hac_ref, hbv_ref, hbg_ref,
             cb_ref, cbf_ref, wa_ref, wb_ref, lg_ref, lb_ref,
             dp_ref, dwa_ref, dwb_ref, dbias_ref, dlg_ref, dlb_ref, extd_ref, extg_ref, shd_ref, shg_ref):
        i = pl.program_id(0)
        keep_prev = (i > 0).astype(F32)
        keep_next = (i < n_t - 1).astype(F32)
        col = lambda j, cs: p_ref[:, j * W + cs.start:j * W + cs.stop].astype(F32)

        @pl.when(i == 0)
        def _():
            dwa_ref[...] = jnp.zeros_like(dwa_ref)
            dwb_ref[...] = jnp.zeros_like(dwb_ref)
            dbias_ref[...] = jnp.zeros_like(dbias_ref)
            dlg_ref[...] = jnp.zeros_like(dlg_ref)
            dlb_ref[...] = jnp.zeros_like(dlb_ref)

        def dcb_rows(rows, cb_rows_ref, dub, bz_of, dst0, scale, main):
            cbv = cb_rows_ref[...]
            mu = jnp.mean(cbv, axis=-1, keepdims=True)
            xc = cbv - mu
            rs = lax.rsqrt(jnp.mean(xc * xc, axis=-1, keepdims=True) + EPS)
            m1 = jnp.zeros((rows, LANES), F32)
            m2 = jnp.zeros((rows, LANES), F32)
            for cs in chunks:
                nbv = (cb_rows_ref[:, cs] - mu) * rs
                lb = nbv * lg_ref[:, cs] + lb_ref[:, cs]
                sl = _sig(lb)
                bz = bz_of(cs)
                sz = _sig(bz)
                dub_c = dub(cs)
                dlb = dub_c * (bz * sz) * _dsilu(lb, sl)
                if main:
                    dlg_ref[:, cs] += _colsum(dlb * nbv)
                    dlb_ref[:, cs] += _colsum(dlb)
                    dp_ref[:, 6 * W + cs.start:6 * W + cs.stop] = (dub_c * (lb * sl) * _dsilu(bz, sz)).astype(BF16)
                dnb = dlb * lg_ref[:, cs]
                extd_ref[dst0:dst0 + rows, cs] = dnb
                m1 = m1 + dnb
                m2 = m2 + dnb * nbv
            m1 = jnp.sum(m1, axis=-1, keepdims=True) * (1.0 / W)
            m2 = jnp.sum(m2, axis=-1, keepdims=True) * (1.0 / W)
            for cs in chunks:
                nbv = (cb_rows_ref[:, cs] - mu) * rs
                dcb = rs * (extd_ref[dst0:dst0 + rows, cs] - m1 - nbv * m2) * scale
                extd_ref[dst0:dst0 + rows, cs] = dcb
                if main:
                    dbias_ref[:, cs] += _colsum(dcb)

        dcb_rows(tm, cb_ref, lambda cs: du_ref[:, W + cs.start:W + cs.stop].astype(F32),
                 lambda cs: col(6, cs), 0, 1.0, True)
        dcb_rows(HALO, cbf_ref, lambda cs: duf_ref[:, W + cs.start:W + cs.stop].astype(F32),
                 lambda cs: fbz_ref[:, cs].astype(F32), tm, keep_next, False)

        extg_ref[0:HALO, :] = hbv_ref[...].astype(F32) * _sig(hbg_ref[...].astype(F32)) * keep_prev
        extg_ref[HALO:, :] = p_ref[:, 4 * W:5 * W].astype(F32) * _sig(p_ref[:, 5 * W:6 * W].astype(F32))
        base_b = HALO - (CONV_B - 1)
        for cs in chunks:
            _shift_copies(shd_ref, extd_ref, cs)
            _shift_copies(shg_ref, extg_ref, cs)
            dgb = _taps_rev(extd_ref, wb_ref, CONV_B, cs, tm, shd_ref)
            bv = col(4, cs)
            sg = _sig(col(5, cs))
            dp_ref[:, 4 * W + cs.start:4 * W + cs.stop] = (dgb * sg).astype(BF16)
            dp_ref[:, 5 * W + cs.start:5 * W + cs.stop] = (dgb * bv * sg * (1.0 - sg)).astype(BF16)
            dcb = extd_ref[0:tm, cs]
            for k in range(CONV_B):
                dwb_ref[k:k + 1, cs] += _colsum(dcb * _rows_at(extg_ref, shg_ref, base_b + k, cs, tm))

        extg_ref[0:HALO, :] = hax_ref[...].astype(F32) * hac_ref[...].astype(F32) * keep_prev
        extg_ref[HALO:, :] = p_ref[:, 2 * W:3 * W].astype(F32) * p_ref[:, 0:W].astype(F32)
        base_a = HALO - (CONV_A - 1)
        for cs in chunks:
            conv = _taps(extg_ref, wa_ref, CONV_A, base_a, cs, tm)
            az = col(3, cs)
            sz = _sig(az)
            ab = col(1, cs)
            dua = du_ref[:, cs].astype(F32)
            dya = dua * (az * sz)
            dp_ref[:, W + cs.start:W + cs.stop] = (dya * conv).astype(BF16)
            dp_ref[:, 3 * W + cs.start:3 * W + cs.stop] = (dua * (ab * conv) * _dsilu(az, sz)).astype(BF16)
            extd_ref[0:tm, cs] = dya * ab
            azf = faz_ref[:, cs].astype(F32)
            extd_ref[tm:tm + HALO, cs] = (duf_ref[:, cs].astype(F32) * (azf * _sig(azf))
                                          * fab_ref[:, cs].astype(F32) * keep_next)
        for cs in chunks:
            dca = _taps_rev(extd_ref, wa_ref, CONV_A, cs, tm)
            dp_ref[:, cs] = (dca * col(2, cs)).astype(BF16)
            dp_ref[:, 2 * W + cs.start:2 * W + cs.stop] = (dca * col(0, cs)).astype(BF16)
            dconv = extd_ref[0:tm, cs]
            for k in range(CONV_A):
                dwa_ref[k:k + 1, cs] += _colsum(dconv * extg_ref[pl.ds(base_a + k, tm), cs])

    prev = lambda j: pl.BlockSpec((HALO, W), lambda i: (jnp.maximum(i * nb - 1, 0), j))
    nxt = lambda j, w: pl.BlockSpec((HALO, w), lambda i: (jnp.minimum((i + 1) * nb, last_blk), j))
    row = lambda w: pl.BlockSpec((tm, w), lambda i: (i, 0))
    return _call(
        body, grid=(n_t,),
        in_specs=[row(2 * W), nxt(0, 2 * W), row(7 * W), nxt(1, W), nxt(3, W), nxt(6, W),
                  prev(0), prev(2), prev(4), prev(5), row(W), nxt(0, W),
                  _const((CONV_A, W)), _const((CONV_B, W)), _const((1, W)), _const((1, W))],
        out_specs=[row(7 * W), _const((CONV_A, W)), _const((CONV_B, W)), _const((1, W)), _const((1, W)), _const((1, W))],
        out_shape=[_sds((S, 7 * W), BF16), _sds((CONV_A, W), F32), _sds((CONV_B, W), F32),
                   _sds((1, W), F32), _sds((1, W), F32), _sds((1, W), F32)],
        operands=(du, du, p, p, p, p, p, p, p, p, cb, cb, wa, wb, ln_g, ln_b),
        scratch_shapes=[pltpu.VMEM((tm + HALO, W), F32), pltpu.VMEM((HALO + tm, W), F32),
                        pltpu.VMEM((SUBLANES - 1, HALO + tm - SUBLANES, LANES), F32),
                        pltpu.VMEM((SUBLANES - 1, HALO + tm - SUBLANES, LANES), F32)],
        name=name, params=_params(("arbitrary",), 52), comm=comm)


def _counts(i, tm, rows, off, win):
    t = i * tm + off + lax.broadcasted_iota(jnp.int32, (rows, 1), 0)
    return jnp.minimum(t + 1, win).astype(F32)


def _o_mix_fwd(q, cw, cb, cscale, *, tm, name):
    S = q.shape[0]
    WC = q.shape[1] // 2
    NG = len(POOL_WINDOWS)
    G = WC // NG
    nb = tm // PHALO

    def body(v_ref, z_ref, hv_ref, cw_ref, cb_ref, sc_ref, yy_ref, pooled_ref, gg_ref, ext_ref):
        i = pl.program_id(0)
        keep = (i > 0).astype(F32)
        for g, win in enumerate(POOL_WINDOWS):
            cs = slice(g * G, (g + 1) * G)
            v = v_ref[:, cs].astype(F32)
            ext_ref[0:PHALO, :] = hv_ref[:, cs].astype(F32) * keep
            ext_ref[PHALO:, :] = v
            s = v
            for j in range(1, win):
                s = s + ext_ref[pl.ds(PHALO - j, tm), :]
            pooled = (s / _counts(i, tm, tm, 0, win) - v).astype(BF16)
            pooled_ref[:, cs] = pooled
            gg = jnp.dot(pooled, cw_ref[g], preferred_element_type=F32) + cb_ref[:, cs]
            gg_ref[:, cs] = gg.astype(BF16)
            z = z_ref[:, cs].astype(F32)
            yy_ref[:, cs] = (gg * sc_ref[:, cs] * (z * _sig(z))).astype(BF16)

    row = lambda j: pl.BlockSpec((tm, WC), lambda i: (i, j))
    out = pl.BlockSpec((tm, WC), lambda i: (i, 0))
    return _call(
        body, grid=(S // tm,),
        in_specs=[row(0), row(1), pl.BlockSpec((PHALO, WC), lambda i: (jnp.maximum(i * nb - 1, 0), 0)),
                  _const((NG, G, G)), _const((1, WC)), _const((1, WC))],
        out_specs=[out, out, out],
        out_shape=[_sds((S, WC), BF16)] * 3, operands=(q, q, q, cw, cb, cscale),
        scratch_shapes=[pltpu.VMEM((PHALO + tm, G), F32)],
        name=name, params=_params(("arbitrary",), 40))[0]


def _o_mix_bwd(dyy, q, gg, pooled, cw, cscale, *, tm, name):
    S = q.shape[0]
    WC = q.shape[1] // 2
    NG = len(POOL_WINDOWS)
    G = WC // NG
    nb = tm // PHALO
    n_t = S // tm
    last_blk = S // PHALO - 1
    nt = (((1,), (1,)), ((), ()))
    tn = (((0,), (0,)), ((), ()))

    def body(dyy_ref, dyyf_ref, z_ref, zf_ref, gg_ref, pooled_ref, cw_ref, sc_ref,
             dq_ref, dcw_ref, dcb_ref, dsc_ref, ext_ref):
        i = pl.program_id(0)
        keep_next = (i < n_t - 1).astype(F32)

        @pl.when(i == 0)
        def _():
            dcw_ref[...] = jnp.zeros_like(dcw_ref)
            dcb_ref[...] = jnp.zeros_like(dcb_ref)
            dsc_ref[...] = jnp.zeros_like(dsc_ref)

        for g, win in enumerate(POOL_WINDOWS):
            cs = slice(g * G, (g + 1) * G)
            sc = sc_ref[:, cs]
            z = z_ref[:, cs].astype(F32)
            sz = _sig(z)
            dyy_c = dyy_ref[:, cs].astype(F32)
            ggv = gg_ref[:, cs].astype(F32)
            dyy0 = dyy_c * (z * sz)
            dq_ref[:, WC + cs.start:WC + cs.stop] = (dyy_c * (ggv * sc) * _dsilu(z, sz)).astype(BF16)
            dgg = dyy0 * sc
            dsc_ref[:, cs] += _colsum(dyy0 * ggv)
            dcb_ref[:, cs] += _colsum(dgg)
            dgg_b = dgg.astype(BF16)
            dcw_ref[g] += lax.dot_general(pooled_ref[:, cs], dgg_b, tn, preferred_element_type=F32)
            dpool = lax.dot_general(dgg_b, cw_ref[g], nt, preferred_element_type=F32)
            zf = zf_ref[:, cs].astype(F32)
            dgg_f = (dyyf_ref[:, cs].astype(F32) * (zf * _sig(zf)) * sc * keep_next).astype(BF16)
            dpool_f = lax.dot_general(dgg_f, cw_ref[g], nt, preferred_element_type=F32)
            ext_ref[0:tm, :] = dpool / _counts(i, tm, tm, 0, win)
            ext_ref[tm:tm + PHALO, :] = dpool_f / _counts(i, tm, PHALO, tm, win)
            dv = ext_ref[0:tm, :] - dpool
            for j in range(1, win):
                dv = dv + ext_ref[pl.ds(j, tm), :]
            dq_ref[:, cs] = dv.astype(BF16)

    row = lambda: pl.BlockSpec((tm, WC), lambda i: (i, 0))
    nxt = lambda j: pl.BlockSpec((PHALO, WC), lambda i: (jnp.minimum((i + 1) * nb, last_blk), j))
    return _call(
        body, grid=(n_t,),
        in_specs=[row(), nxt(0), pl.BlockSpec((tm, WC), lambda i: (i, 1)), nxt(1), row(), row(),
                  _const((NG, G, G)), _const((1, WC))],
        out_specs=[pl.BlockSpec((tm, 2 * WC), lambda i: (i, 0)), _const((NG, G, G)), _const((1, WC)), _const((1, WC))],
        out_shape=[_sds((S, 2 * WC), BF16), _sds((NG, G, G), F32), _sds((1, WC), F32), _sds((1, WC), F32)],
        operands=(dyy, dyy, q, q, gg, pooled, cw, cscale),
        scratch_shapes=[pltpu.VMEM((tm + PHALO, G), F32)],
        name=name, params=_params(("arbitrary",), 48))[0]


def _place():
    return lax.axis_index("x"), lax.axis_index("y"), lax.axis_index("c")


def _piece(ref, axis, size, index):
    start = index * size
    if axis == len(ref.shape) - 1:
        start = pl.multiple_of(start, LANES)
    idx = [slice(None)] * len(ref.shape)
    idx[axis] = pl.ds(start, size)
    return ref.at[tuple(idx)]


def _gather_copies(src, out, axis, size, send_sems, recv_sems, base, held=None):
    x, y, c = _place()
    sib, xn, yn = (x, y, 1 - c), (1 - x, y, c), (x, 1 - y, c)

    def blk(px, py, of=out):
        return _piece(of, axis, size, 4 * px + 2 * py + c)

    def half(ref, h):
        n = ref.shape[0] // 2
        return ref.at[pl.ds(h * n, n)]

    def rc(k, s, d, to):
        return pltpu.make_async_remote_copy(src_ref=s, dst_ref=d, send_sem=send_sems.at[base + k],
                                            recv_sem=recv_sems.at[base + k], device_id=to, device_id_type=MESH)

    own, xb, yb, db = blk(x, y), blk(1 - x, y), blk(x, 1 - y), blk(1 - x, 1 - y)
    got = out if held is None else held
    xs, ys, ds = blk(1 - x, y, got), blk(x, 1 - y, got), blk(1 - x, 1 - y, got)
    return [rc(0, src, own, sib), rc(1, src, own, xn), rc(2, src, own, yn),
            rc(3, half(xs, 0), half(xb, 0), yn), rc(4, half(ys, 1), half(yb, 1), xn),
            rc(5, xs, xb, sib), rc(6, ys, yb, sib), rc(7, ds, db, sib)]


N_GATHER = 8


def _gather_comm(shards, axes, phases):
    n = len(shards)
    if phases == "second":
        sizes = [s.shape[a] // N_DEV for s, a in zip(shards, axes)]
        full = [_sds(s.shape, s.dtype) for s in shards]
    else:
        sizes = [s.shape[a] for s, a in zip(shards, axes)]
        full = [_sds(s.shape[:a] + (N_DEV * s.shape[a],) + s.shape[a + 1:], s.dtype) for s, a in zip(shards, axes)]

    def plan(ins, outs, sems):
        x, y, c = _place()
        me = 4 * x + 2 * y + c
        if phases == "second":
            cps = [_gather_copies(_piece(ins[t], axes[t], sizes[t], me), outs[t], axes[t], sizes[t], sems[0], sems[1],
                                  N_GATHER * t, ins[t]) for t in range(n)]
        else:
            cps = [_gather_copies(sems[3 + t], outs[t], axes[t], sizes[t], sems[0], sems[1], N_GATHER * t)
                   for t in range(n)]
        mine = [pltpu.make_async_copy(sems[3 + t], _piece(outs[t], axes[t], sizes[t], me), sems[2].at[t])
                for t in range(n)] if phases != "second" else []
        return cps, mine

    def send_own(ins, outs, sems):
        cps, mine = plan(ins, outs, sems)
        for t in range(n):
            stage = pltpu.make_async_copy(ins[t], sems[3 + t], sems[2].at[t])
            stage.start()
            stage.wait()
            mine[t].start()
            for k in (0, 1, 2):
                cps[t][k].start()

    def pass_on(ins, outs, sems):
        cps, _ = plan(ins, outs, sems)
        for t in range(n):
            if phases == "all":
                cps[t][1].wait_recv()
            cps[t][3].start()
            cps[t][5].start()
        for t in range(n):
            if phases == "all":
                cps[t][2].wait_recv()
            cps[t][4].start()
            cps[t][6].start()

    def own_landed(ins, outs, sems):
        cps, mine = plan(ins, outs, sems)
        for t in range(n):
            for k in (0, 1, 2):
                cps[t][k].wait()
            mine[t].wait()

    def all_landed(ins, outs, sems):
        cps, mine = plan(ins, outs, sems)
        for t in range(n):
            cps[t][3].wait_recv()
            cps[t][4].wait_recv()
            cps[t][7].start()
        for t in range(n):
            for k in ((0, 5, 6, 7) if phases == "all" else (5, 6, 7)):
                cps[t][k].wait_recv()
            for k in (range(N_GATHER) if phases == "all" else range(3, N_GATHER)):
                cps[t][k].wait_send()
            if phases == "all":
                mine[t].wait()

    sems = [pltpu.SemaphoreType.DMA((N_GATHER * n,)), pltpu.SemaphoreType.DMA((N_GATHER * n,))]
    if phases != "second":
        sems.append(pltpu.SemaphoreType.DMA((n,)))
        sems += [pltpu.VMEM(s.shape, s.dtype) for s in shards]
    if phases == "all":
        return _Comm(shards, full, sems, send_own, all_landed, middle=pass_on)
    if phases == "first":
        return _Comm(shards, full, sems, send_own, own_landed)
    return _Comm(shards, full, sems, pass_on, all_landed, aliases={t: t for t in range(n)})


def _pair_comm(grads, axes, sizes):
    n = len(grads)
    outs_sds = [_sds((4,) + g.shape[:a] + (s,) + g.shape[a + 1:], g.dtype) for g, a, s in zip(grads, axes, sizes)]

    def copies(ins, outs, sems):
        send_sems, recv_sems = sems
        x, y, c = _place()
        return [pltpu.make_async_remote_copy(
            src_ref=_piece(ins[t], axes[t], sizes[t], 2 * qi + (1 - c)), dst_ref=outs[t].at[qi],
            send_sem=send_sems.at[4 * t + qi], recv_sem=recv_sems.at[4 * t + qi],
            device_id=(x, y, 1 - c), device_id_type=MESH) for t in range(n) for qi in range(4)]

    def start(ins, outs, sems):
        for cp in copies(ins, outs, sems):
            cp.start()

    def finish(ins, outs, sems):
        for cp in copies(ins, outs, sems):
            cp.wait()

    sems = [pltpu.SemaphoreType.DMA((4 * n,)), pltpu.SemaphoreType.DMA((4 * n,))]
    return _Comm(grads, outs_sds, sems, start, finish)


def _chip_comm(sums):
    n = len(sums)
    outs_sds = [_sds((3,) + s.shape[1:], s.dtype) for s in sums]

    def copies(ins, outs, sems):
        send_sems, recv_sems = sems
        x, y, c = _place()
        return [pltpu.make_async_remote_copy(
            src_ref=ins[t].at[2 * qx + qy], dst_ref=outs[t].at[j],
            send_sem=send_sems.at[3 * t + j], recv_sem=recv_sems.at[3 * t + j],
            device_id=(qx, qy, c), device_id_type=MESH)
            for t in range(n) for j, (qx, qy) in enumerate([(1 - x, y), (x, 1 - y), (1 - x, 1 - y)])]

    def start(ins, outs, sems):
        for cp in copies(ins, outs, sems):
            cp.start()

    def finish(ins, outs, sems):
        for cp in copies(ins, outs, sems):
            cp.wait()

    sems = [pltpu.SemaphoreType.DMA((3 * n,)), pltpu.SemaphoreType.DMA((3 * n,))]
    return _Comm(sums, outs_sds, sems, start, finish)


def _small_comm(small):
    def copies(ins, outs, sems):
        send_sems, recv_sems, local_sem = sems
        x, y, c = _place()
        mine = outs[0].at[4 * x + 2 * y + c]
        out = [pltpu.make_async_copy(ins[0], mine, local_sem.at[0])]
        for k in range(1, N_DEV):
            peer = (1 - x if k & 4 else x, 1 - y if k & 2 else y, 1 - c if k & 1 else c)
            out.append(pltpu.make_async_remote_copy(
                src_ref=ins[0], dst_ref=mine, send_sem=send_sems.at[k - 1], recv_sem=recv_sems.at[k - 1],
                device_id=peer, device_id_type=MESH))
        return out

    def start(ins, outs, sems):
        for cp in copies(ins, outs, sems):
            cp.start()

    def finish(ins, outs, sems):
        for cp in copies(ins, outs, sems):
            cp.wait()

    sems = [pltpu.SemaphoreType.DMA((N_DEV - 1,)), pltpu.SemaphoreType.DMA((N_DEV - 1,)), pltpu.SemaphoreType.DMA((1,))]
    return _Comm([small], [_sds((N_DEV,) + small.shape, small.dtype)], sems, start, finish)


def _small_scatter_comm(send):
    def copies(ins, outs, sems):
        send_sems, recv_sems, local_sem = sems
        x, y, c = _place()
        me = 4 * x + 2 * y + c
        out = [pltpu.make_async_copy(ins[0].at[me], outs[0].at[me], local_sem.at[0])]
        for k in range(1, N_DEV):
            px, py, pc = (1 - x if k & 4 else x, 1 - y if k & 2 else y, 1 - c if k & 1 else c)
            out.append(pltpu.make_async_remote_copy(
                src_ref=ins[0].at[4 * px + 2 * py + pc], dst_ref=outs[0].at[me], send_sem=send_sems.at[k - 1],
                recv_sem=recv_sems.at[k - 1], device_id=(px, py, pc), device_id_type=MESH))
        return out

    def start(ins, outs, sems):
        for cp in copies(ins, outs, sems):
            cp.start()

    def finish(ins, outs, sems):
        for cp in copies(ins, outs, sems):
            cp.wait()

    sems = [pltpu.SemaphoreType.DMA((N_DEV - 1,)), pltpu.SemaphoreType.DMA((N_DEV - 1,)), pltpu.SemaphoreType.DMA((1,))]
    return _Comm([send], [_sds(send.shape, send.dtype)], sems, start, finish)


def _pair_sum(c_idx, grad, recv, axis, size, split, *, name):
    nd = len(grad.shape)
    piece = grad.shape[:axis] + (size,) + grad.shape[axis + 1:]
    blk = (piece[0] // split,) + piece[1:]

    def g_map(q, r, c_ref):
        idx = [0] * nd
        idx[axis] = 2 * q + c_ref[0]
        idx[0] = idx[0] * split + r if axis == 0 else r
        return tuple(idx)

    def r_map(q, r, c_ref):
        return (q, r) + (0,) * (nd - 1)

    def body(c_ref, g_ref, r_ref, o_ref):
        o_ref[0] = (g_ref[...].astype(F32) + r_ref[0].astype(F32)).astype(BF16)

    return _call(
        body, grid=(4, split), prefetch=c_idx,
        in_specs=[pl.BlockSpec(blk, g_map), pl.BlockSpec((1,) + blk, r_map)],
        out_specs=[pl.BlockSpec((1,) + blk, r_map)], out_shape=[_sds((4,) + piece, BF16)],
        operands=(grad, recv), name=name, params=_params(("arbitrary", "arbitrary"), 32))[0][0]


def _adam_math(w, g, m, v):
    m = ADAM_B1 * m + (1.0 - ADAM_B1) * g
    v = ADAM_B2 * v + (1.0 - ADAM_B2) * (g * g)
    m_hat = m / (1.0 - ADAM_B1 ** ADAM_STEP)
    v_hat = v / (1.0 - ADAM_B2 ** ADAM_STEP)
    delta = -ADAM_LR * (m_hat / (jnp.sqrt(v_hat) + ADAM_EPS) + ADAM_WD * w)
    return delta, m, v


def _adam_big(q_idx, sums, recv, w, m, v, split, *, name, comm=None):
    shape = w.shape
    nd = len(shape)
    blk = (shape[0] // split,) + shape[1:]
    w_map = lambda r, q_ref: (r,) + (0,) * (nd - 1)
    s_map = lambda r, q_ref: (q_ref[0], r) + (0,) * (nd - 1)
    r_map = lambda r, q_ref: (0, r) + (0,) * (nd - 1)

    def body(q_ref, s_ref, r_ref, w_ref, m_ref, v_ref, g_ref, d_ref, nm_ref, nv_ref):
        g = s_ref[0].astype(F32) + r_ref[0].astype(F32) + r_ref[1].astype(F32) + r_ref[2].astype(F32)
        g_ref[...] = g
        d_ref[...], nm_ref[...], nv_ref[...] = _adam_math(w_ref[...], g, m_ref[...], v_ref[...])

    wspec = pl.BlockSpec(blk, w_map)
    return _call(
        body, grid=(split,), prefetch=q_idx,
        in_specs=[pl.BlockSpec((1,) + blk, s_map), pl.BlockSpec((3,) + blk, r_map), wspec, wspec, wspec],
        out_specs=[wspec] * 4, out_shape=[_sds(shape, F32)] * 4, operands=(sums, recv, w, m, v),
        name=name, params=_params(("arbitrary",), 32), comm=comm)


def _adam_small(parts, w, m, v, *, name):
    R = w.shape[0]

    def body(p_ref, w_ref, m_ref, v_ref, g_ref, d_ref, nm_ref, nv_ref):
        g = p_ref[0]
        for d in range(1, N_DEV):
            g = g + p_ref[d]
        g_ref[...] = g
        d_ref[...], nm_ref[...], nv_ref[...] = _adam_math(w_ref[...], g, m_ref[...], v_ref[...])

    whole = _const((R, LANES))
    return _call(
        body, grid=(1,), in_specs=[_const((N_DEV, R, LANES)), whole, whole, whole], out_specs=[whole] * 4,
        out_shape=[_sds((R, LANES), F32)] * 4, operands=(parts, w, m, v), name=name,
        params=_params(("arbitrary",), 32))[0]


def _pack(arrs):
    return jnp.concatenate([a.reshape(-1) for a in arrs]).reshape(-1, LANES)


def _unpack(packed, shapes):
    flat = packed.reshape(-1)
    out, off = [], 0
    for s in shapes:
        n = 1
        for d in s:
            n *= d
        out.append(flat[off:off + n].reshape(s))
        off += n
    return out


BIG = ("e_in", "e_out", "o_in", "o_cw", "o_out")
BIG_AXIS = dict(e_in=1, e_out=0, o_in=1, o_cw=1, o_out=0)
BIG_SPLIT = dict(e_in=8, e_out=4, o_in=4, o_cw=4, o_out=4)
REPLICATED = ("e_norm_pre", "e_norm_post", "e_b_conv_bias", "e_b_ln_g", "e_b_ln_b")
SHARDED = ("e_a_conv", "e_b_conv", "o_norm_pre", "o_norm_post", "o_c_b", "o_c_scale")
SMALL = REPLICATED + SHARDED


class _Exchange:
    def __init__(self, shards, small, order, c_idx):
        self.q_idx = order[:1]
        self.shards = shards
        self.small = small
        self.order = order
        self.c_idx = c_idx
        self.reduced = {}

    def gather(self, keys):
        return _gather_comm([self.shards[k] for k in keys], [BIG_AXIS[k] for k in keys], "all")

    def gather1(self, keys):
        return _gather_comm([self.shards[k] for k in keys], [BIG_AXIS[k] for k in keys], "first")

    def gather2(self, keys, firsts):
        return _gather_comm(firsts, [BIG_AXIS[k] for k in keys], "second")

    def pair(self, grads):
        keys = list(grads)
        return _pair_comm([grads[k] for k in keys], [BIG_AXIS[k] for k in keys],
                          [grads[k].shape[BIG_AXIS[k]] // N_DEV for k in keys])

    def pair_sums(self, grads, received):
        return {k: _pair_sum(self.c_idx, grads[k], r, BIG_AXIS[k], grads[k].shape[BIG_AXIS[k]] // N_DEV,
                             BIG_SPLIT[k], name="pair_sum_" + k) for k, r in zip(grads, received)}

    def chips(self, sums):
        return _chip_comm([sums[k] for k in sums])

    def done(self, sums, received):
        self.reduced.update({k: (sums[k], r, self.q_idx) for k, r in zip(sums, received)})


def _local_step(x, tgt, w_small, ex):
    S, D = x.shape
    tnt, tx, tw = min(TM_NT, S), min(TM_MIX, S), min(TM_WIDE, S)

    wt = {}
    p, h0, wt["e_in"], got = _gather_matmul(ex.order, x, w_small["e_norm_pre"], ex.shards["e_in"], tm=tw,
                                            name="e_in_fwd", comm=_small_comm(ex.small))
    per_dev = [_unpack(got[0][d], [w_small[k].shape for k in SHARDED]) for d in range(N_DEV)]
    sm = {k: w_small[k] for k in REPLICATED}
    for j, k in enumerate(SHARDED):
        sm[k] = jnp.concatenate([per_dev[d][j] for d in range(N_DEV)], axis=-1)
    n_groups = sm["o_c_b"].shape[0]
    sm["o_c_b"] = sm["o_c_b"].reshape(1, -1)

    W = p.shape[1] // 7
    (u, cb), got = _e_mix_fwd(p, sm["e_a_conv"], sm["e_b_conv"], sm["e_b_conv_bias"], sm["e_b_ln_g"],
                              sm["e_b_ln_b"], tm=tx, name="e_mix_fwd", comm=ex.gather(["e_out"]))
    wt["e_out"] = got[0]
    late = ["o_out", "o_cw"]
    (x1, y0), part = _out_norm_res(u, wt["e_out"], x, sm["e_norm_post"], tm=tw, name="e_out_fwd",
                                   comm=ex.gather1(late))
    q, h1, wt["o_in"], got = _gather_matmul(ex.order, x1, sm["o_norm_pre"], ex.shards["o_in"], tm=tw,
                                            name="o_in_fwd", comm=ex.gather2(late, part))
    wt.update(zip(late, got))
    yy, pooled, gg = _o_mix_fwd(q, wt["o_cw"], sm["o_c_b"], sm["o_c_scale"], tm=tw, name="o_mix_fwd")
    dout, dx2, dyy, lcol, dg_o_post = _out_loss(yy, wt["o_out"], x1, sm["o_norm_post"], tgt, tm=tx, name="o_out_loss")
    loss = (0.5 / D) * jnp.sum(lcol)

    dq, d_cw, d_cb, d_cscale = _o_mix_bwd(dyy, q, gg, pooled, wt["o_cw"], sm["o_c_scale"], tm=tw, name="o_mix_bwd")
    g_o_out, _ = _mm_tn(yy, dout, ts=tnt, tn=W, name="o_out_dw")
    ga = dict(o_out=g_o_out, o_cw=d_cw.astype(BF16))
    dh1, ra = _mm_nt(dq, wt["o_in"], tm=tnt, tk=W, name="o_in_bwd", comm=ex.pair(ga))
    sa = ex.pair_sums(ga, ra)
    (dx1, dy0, dg_o_pre, dg_e_post), ra = _pre_bwd_o(dh1, x1, dx2, y0, sm["o_norm_pre"], sm["e_norm_post"],
                                                     tm=tx, name="o_pre_bwd", comm=ex.chips(sa))
    ex.done(sa, ra)
    g_o_in, _ = _mm_tn(h1, dq, ts=tnt, tn=W, name="o_in_dw")
    gb = dict(o_in=g_o_in)
    du, rb = _mm_nt(dy0, wt["e_out"], tm=tnt, tk=W, name="e_out_bwd", comm=ex.pair(gb))
    sb = ex.pair_sums(gb, rb)
    g_e_out, _ = _mm_tn(u, dy0, ts=tnt, tn=W, name="e_out_dw")
    gc = dict(e_out=g_e_out)
    (dp, d_wa, d_wb, d_bias, d_lg, d_lb), rbc = _e_mix_bwd(
        du, p, cb, sm["e_a_conv"], sm["e_b_conv"], sm["e_b_ln_g"], sm["e_b_ln_b"], tm=tx, name="e_mix_bwd",
        comm=_merge(ex.chips(sb), ex.pair(gc)))
    ex.done(sb, rbc[:1])
    sc = ex.pair_sums(gc, rbc[1:])
    order_out = jnp.concatenate([ex.order[1:], ex.order[:1]])
    sd, from_chip, rc = _dw_reduce(order_out, h0, dp, ex.shards["e_in"].shape[1], ts=tw, name="e_in_dw",
                                   comm=ex.chips(sc))
    ex.done(sc, rc)
    dh0, rd = _mm_nt(dp, wt["e_in"], tm=tnt, tk=W, name="e_in_bwd", comm=_diag_comm(sd, from_chip))
    ex.reduced["e_in"] = (sd, rd[0], jnp.full((1,), 3, jnp.int32))
    grad_x, dg_e_pre = _pre_bwd_e(dh0, x, dx1, sm["e_norm_pre"], tm=tw, name="e_pre_bwd")

    small = dict(e_norm_pre=dg_e_pre, e_norm_post=dg_e_post, e_a_conv=d_wa, e_b_conv=d_wb, e_b_conv_bias=d_bias,
                 e_b_ln_g=d_lg, e_b_ln_b=d_lb, o_norm_pre=dg_o_pre, o_norm_post=dg_o_post,
                 o_c_b=d_cb.reshape(n_groups, -1), o_c_scale=d_cscale)
    return loss, grad_x, small


def kernel(x, e_norm_pre, e_norm_post, e_w_in, e_a_conv, e_b_conv, e_b_conv_bias, e_b_ln_g, e_b_ln_b, e_w_out, o_norm_pre, o_norm_post, o_w_in, o_c_w, o_c_b, o_c_scale, o_w_out, loss_target, m_e_norm_pre, m_e_norm_post, m_e_w_in, m_e_a_conv, m_e_b_conv, m_e_b_conv_bias, m_e_b_ln_g, m_e_b_ln_b, m_e_w_out, m_o_norm_pre, m_o_norm_post, m_o_w_in, m_o_c_w, m_o_c_b, m_o_c_scale, m_o_w_out, v_e_norm_pre, v_e_norm_post, v_e_w_in, v_e_a_conv, v_e_b_conv, v_e_b_conv_bias, v_e_b_ln_g, v_e_b_ln_b, v_e_w_out, v_o_norm_pre, v_o_norm_post, v_o_w_in, v_o_c_w, v_o_c_b, v_o_c_scale, v_o_w_out):
    xi, yi, ci = _place()
    w_big = dict(e_in=e_w_in[0], e_out=e_w_out[0], o_in=o_w_in[0], o_cw=o_c_w[0], o_out=o_w_out[0])
    m_big = dict(e_in=m_e_w_in[0], e_out=m_e_w_out[0], o_in=m_o_w_in[0], o_cw=m_o_c_w[0], o_out=m_o_w_out[0])
    v_big = dict(e_in=v_e_w_in[0], e_out=v_e_w_out[0], o_in=v_o_w_in[0], o_cw=v_o_c_w[0], o_out=v_o_w_out[0])
    w_small = dict(e_norm_pre=e_norm_pre, e_norm_post=e_norm_post, e_b_conv_bias=e_b_conv_bias, e_b_ln_g=e_b_ln_g,
                   e_b_ln_b=e_b_ln_b, e_a_conv=e_a_conv[0], e_b_conv=e_b_conv[0], o_norm_pre=o_norm_pre,
                   o_norm_post=o_norm_post, o_c_b=o_c_b[0], o_c_scale=o_c_scale)
    m_small = dict(e_norm_pre=m_e_norm_pre, e_norm_post=m_e_norm_post, e_b_conv_bias=m_e_b_conv_bias,
                   e_b_ln_g=m_e_b_ln_g, e_b_ln_b=m_e_b_ln_b, e_a_conv=m_e_a_conv[0], e_b_conv=m_e_b_conv[0],
                   o_norm_pre=m_o_norm_pre, o_norm_post=m_o_norm_post, o_c_b=m_o_c_b[0], o_c_scale=m_o_c_scale)
    v_small = dict(e_norm_pre=v_e_norm_pre, e_norm_post=v_e_norm_post, e_b_conv_bias=v_e_b_conv_bias,
                   e_b_ln_g=v_e_b_ln_g, e_b_ln_b=v_e_b_ln_b, e_a_conv=v_e_a_conv[0], e_b_conv=v_e_b_conv[0],
                   o_norm_pre=v_o_norm_pre, o_norm_post=v_o_norm_post, o_c_b=v_o_c_b[0], o_c_scale=v_o_c_scale)

    c_idx = jnp.reshape(ci, (1,)).astype(jnp.int32)
    order = jnp.stack([2 * xi + yi, 2 * (1 - xi) + yi, 2 * xi + (1 - yi), 2 * (1 - xi) + (1 - yi)]).astype(jnp.int32)
    ex = _Exchange({k: w_big[k].astype(BF16) for k in BIG}, _pack([w_small[k] for k in SHARDED]), order, c_idx)
    loss, grad_x, g_small = _local_step(x[0], loss_target[0], w_small, ex)

    big_out = {}
    for k in BIG:
        sums, received, q_idx = ex.reduced[k]
        big_out[k] = _adam_big(q_idx, sums, received, w_big[k], m_big[k], v_big[k], BIG_SPLIT[k], name="adam_" + k)[0]

    rep = _pack([g_small[k] for k in REPLICATED])
    loss_row = jnp.pad(jnp.reshape(loss, (1, 1)), ((0, 0), (0, LANES - 1)))
    blocks = []
    for k in SHARDED:
        r, n = w_small[k].shape
        blocks.append(g_small[k].reshape(r, N_DEV, n).transpose(1, 0, 2).reshape(N_DEV, r * n))
    blocks = jnp.concatenate(blocks, axis=1).reshape(N_DEV, -1, LANES)
    head = jnp.concatenate([rep, loss_row], axis=0)
    send = jnp.concatenate([jnp.broadcast_to(head[None], (N_DEV,) + head.shape), blocks], axis=1)
    parts = _run_comm(_small_scatter_comm(send), "small_grad_exchange")[0]

    def own_rows(d):
        return jnp.concatenate([_pack([d[k] for k in REPLICATED]), jnp.ones((1, LANES), F32),
                                _pack([d[k] for k in SHARDED])], axis=0)

    res_small = _adam_small(parts, own_rows(w_small), own_rows(m_small), own_rows(v_small), name="adam_small")
    n_rep = rep.shape[0]
    loss = res_small[0][n_rep, 0]
    small_out = {k: [] for k in SMALL}
    for packed in res_small:
        for k, t in zip(REPLICATED, _unpack(packed[:n_rep], [w_small[k].shape for k in REPLICATED])):
            small_out[k].append(t)
        for k, t in zip(SHARDED, _unpack(packed[n_rep + 1:], [w_small[k].shape for k in SHARDED])):
            small_out[k].append(t)

    big_of = dict(e_w_in="e_in", e_w_out="e_out", o_w_in="o_in", o_c_w="o_cw", o_w_out="o_out")
    stacked = ("e_a_conv", "e_b_conv", "o_c_b")

    def leaf(name, which):
        if name in big_of:
            return big_out[big_of[name]][which][None]
        t = small_out[name][which]
        return t[None] if name in stacked else t

    order = ("e_norm_pre", "e_norm_post", "e_w_in", "e_a_conv", "e_b_conv", "e_b_conv_bias", "e_b_ln_g", "e_b_ln_b",
             "e_w_out", "o_norm_pre", "o_norm_post", "o_w_in", "o_c_w", "o_c_b", "o_c_scale", "o_w_out")
    outs = [loss, grad_x[None]]
    for which in range(4):
        outs += [leaf(nm, which) for nm in order]
    return tuple(outs)
```

```python
import jax
import jax.numpy as jnp
from jax import lax
from jax.experimental import pallas as pl
from jax.experimental.pallas import tpu as pltpu

F32 = jnp.float32
BF16 = jnp.bfloat16
EPS = 1e-6
MESH = pl.DeviceIdType.MESH
ANY = pl.BlockSpec(memory_space=pl.ANY)

N_DEV = 8
HALO = 32
PHALO = 16
CONV_A = 3
CONV_B = 31
POOL_WINDOWS = (2, 4, 8, 16)
LANES = 128
MIB = 1024 * 1024

ADAM_LR = 0.001
ADAM_B1 = 0.9
ADAM_B2 = 0.999
ADAM_EPS = 1e-08
ADAM_WD = 0.01
ADAM_STEP = 10

TM_NT = 1024
TM_MIX = 256
TM_WIDE = 512


def _sds(shape, dtype):
    return jax.ShapeDtypeStruct(tuple(shape), dtype)


def _params(sem, vmem_mib):
    return pltpu.CompilerParams(dimension_semantics=sem, vmem_limit_bytes=vmem_mib * MIB)


def _const(shape, single=False):
    n = len(shape)
    if single:
        return pl.BlockSpec(shape, lambda *_: (0,) * n, pipeline_mode=pl.Buffered(1))
    return pl.BlockSpec(shape, lambda *_: (0,) * n)


def _sig(v):
    return jax.nn.sigmoid(v)


def _dsilu(v, s):
    return s * (1.0 + v * (1.0 - s))


def _rms(v):
    return lax.rsqrt(jnp.mean(v * v, axis=-1, keepdims=True) + EPS)


def _norm_bwd(dn, n, r):
    return r * (dn - n * jnp.mean(dn * n, axis=-1, keepdims=True))


def _colsum(v):
    return jnp.sum(v, axis=0, keepdims=True)


class _Comm:
    def __init__(self, inputs, out_shapes, sems, start, finish, aliases=None, middle=None):
        self.inputs, self.out_shapes, self.sems = list(inputs), list(out_shapes), list(sems)
        self.start, self.finish, self.middle = start, finish, middle
        self.aliases = dict(aliases or {})


def _merge(*comms):
    comms = [c for c in comms if c is not None]
    if len(comms) <= 1:
        return comms[0] if comms else None
    spans, i0, o0, s0, aliases = [], 0, 0, 0, {}
    for c in comms:
        spans.append((i0, o0, s0))
        aliases.update({i0 + k: o0 + v for k, v in c.aliases.items()})
        i0, o0, s0 = i0 + len(c.inputs), o0 + len(c.out_shapes), s0 + len(c.sems)

    def run(which):
        def fn(ins, outs, sems):
            for c, (i, o, s) in zip(comms, spans):
                hook = getattr(c, which)
                if hook is not None:
                    hook(ins[i:i + len(c.inputs)], outs[o:o + len(c.out_shapes)], sems[s:s + len(c.sems)])
        return fn

    return _Comm([a for c in comms for a in c.inputs], [a for c in comms for a in c.out_shapes],
                 [a for c in comms for a in c.sems], run("start"), run("finish"), aliases,
                 run("middle") if any(c.middle is not None for c in comms) else None)


def _call(body, *, grid, in_specs, out_specs, out_shape, operands, name, params, scratch_shapes=(), comm=None,
          prefetch=None, own_copies_first=False):
    n_p = 0 if prefetch is None else 1
    n_i, n_o, n_s = len(in_specs), len(out_specs), len(scratch_shapes)
    if comm is None:
        comm = _Comm([], [], [], None, None)
    c_i, c_o = len(comm.inputs), len(comm.out_shapes)

    def carrier(*refs):
        pre, refs = refs[:n_p], refs[n_p:]
        ins, cins = refs[:n_i], refs[n_i:n_i + c_i]
        outs = refs[n_i + c_i:n_i + c_i + n_o]
        couts = refs[n_i + c_i + n_o:n_i + c_i + n_o + c_o]
        scr = refs[n_i + c_i + n_o + c_o:n_i + c_i + n_o + c_o + n_s]
        csems = refs[n_i + c_i + n_o + c_o + n_s:]
        ids = [pl.program_id(d) for d in range(len(grid))]
        first = ids[0] == 0
        half = ids[0] == grid[0] // 2
        last = ids[0] == grid[0] - 1
        for d in range(1, len(grid)):
            first = first & (ids[d] == 0)
            half = half & (ids[d] == 0)
            last = last & (ids[d] == grid[d] - 1)

        def start():
            if comm.start is not None:
                @pl.when(first)
                def _():
                    comm.start(cins, couts, csems)

        if not own_copies_first:
            start()
        if comm.middle is not None:
            assert grid[0] >= 2

            @pl.when(half)
            def _():
                comm.middle(cins, couts, csems)

        body(*pre, *ins, *outs, *scr)
        if own_copies_first:
            start()

        if comm.finish is not None:
            @pl.when(last)
            def _():
                comm.finish(cins, couts, csems)

    specs = dict(grid=grid, in_specs=list(in_specs) + [ANY] * c_i, out_specs=list(out_specs) + [ANY] * c_o,
                 scratch_shapes=list(scratch_shapes) + comm.sems)
    if n_p:
        specs = dict(grid_spec=pltpu.PrefetchScalarGridSpec(num_scalar_prefetch=1, **specs))
    res = pl.pallas_call(
        carrier, out_shape=list(out_shape) + comm.out_shapes,
        input_output_aliases={n_p + n_i + k: n_o + v for k, v in comm.aliases.items()},
        name=name, compiler_params=params, **specs)(*(() if prefetch is None else (prefetch,)), *operands, *comm.inputs)
    return list(res[:n_o]), list(res[n_o:])


def _run_comm(comm, name):
    c_i, c_o = len(comm.inputs), len(comm.out_shapes)

    def body(*refs):
        ins, outs, sems = refs[:c_i], refs[c_i:c_i + c_o], refs[c_i + c_o:]
        comm.start(ins, outs, sems)
        comm.finish(ins, outs, sems)

    res = pl.pallas_call(
        body, in_specs=[ANY] * c_i, out_specs=[ANY] * c_o, out_shape=comm.out_shapes, scratch_shapes=comm.sems,
        input_output_aliases=comm.aliases, name=name)(*comm.inputs)
    return list(res)


def _gather_matmul(order, x, g, shard, *, tm, name, comm=None):
    S, K = x.shape
    nb = shard.shape[1]
    n_i = S // tm

    def body(order_ref, x_ref, g_ref, shard_ref, p_ref, h_ref, full_ref, hbuf, wbuf, stage, send_sems, recv_sems,
             dma_sems):
        j, i = pl.program_id(0), pl.program_id(1)
        px, py, pc = _place()
        cps = _gather_copies(stage, full_ref, 1, nb, send_sems, recv_sems, 0)
        own = pltpu.make_async_copy(stage, _piece(full_ref, 1, nb, 4 * px + 2 * py + pc), dma_sems.at[0])
        keep_h = pltpu.make_async_copy(hbuf, h_ref, dma_sems.at[2])

        def load(src, dst):
            cp = pltpu.make_async_copy(src, dst, dma_sems.at[1])
            cp.start()
            cp.wait()

        def load_pair(qx, qy):
            load(_piece(full_ref, 1, 2 * nb, 2 * qx + qy), wbuf)

        @pl.when((j == 0) & (i == 0))
        def _():
            load(shard_ref, stage)
            own.start()
            for k in (0, 1, 2):
                cps[k].start()

        @pl.when(j == 0)
        def _():
            xx = x_ref[...]
            hbuf[i] = ((xx * _rms(xx)) * g_ref[...]).astype(BF16)

        @pl.when((j == 0) & (i == 0))
        def _():
            own.wait()
            cps[0].wait_recv()
            load_pair(px, py)

        @pl.when((j == 1) & (i == 0))
        def _():
            keep_h.start()
            cps[1].wait_recv()
            cps[3].start()
            cps[5].start()
            cps[2].wait_recv()
            cps[4].start()
            cps[6].start()
            cps[5].wait_recv()
            load_pair(1 - px, py)

        @pl.when((j == 2) & (i == 0))
        def _():
            cps[6].wait_recv()
            load_pair(px, 1 - py)

        @pl.when((j == 3) & (i == 0))
        def _():
            cps[3].wait_recv()
            cps[4].wait_recv()
            cps[7].start()
            cps[7].wait_recv()
            load_pair(1 - px, 1 - py)

        p_ref[...] = jnp.dot(hbuf[i], wbuf[...], preferred_element_type=F32).astype(BF16)

        @pl.when((j == 3) & (i == n_i - 1))
        def _():
            for cp in cps:
                cp.wait_send()
            keep_h.wait()

    first_pass = lambda j, i, o: (jnp.where(j == 0, i, n_i - 1), 0)
    outs, extra = _call(
        body, grid=(4, n_i), prefetch=order,
        in_specs=[pl.BlockSpec((tm, K), first_pass), pl.BlockSpec((1, K), lambda j, i, o: (0, 0)), ANY],
        out_specs=[pl.BlockSpec((tm, 2 * nb), lambda j, i, o: (i, o[j])), ANY, ANY],
        out_shape=[_sds((S, N_DEV * nb), BF16), _sds((n_i, tm, K), BF16), _sds((K, N_DEV * nb), BF16)],
        operands=(x, g, shard),
        scratch_shapes=[pltpu.VMEM((n_i, tm, K), BF16), pltpu.VMEM((K, 2 * nb), BF16), pltpu.VMEM((K, nb), BF16),
                        pltpu.SemaphoreType.DMA((N_GATHER,)), pltpu.SemaphoreType.DMA((N_GATHER,)),
                        pltpu.SemaphoreType.DMA((3,))],
        name=name, params=_params(("arbitrary", "arbitrary"), 58), comm=comm, own_copies_first=True)
    return outs[0], outs[1].reshape(S, K), outs[2], extra


def _out_norm_res(u, w, x, g, *, tm, name, comm=None):
    S, K = u.shape
    D = w.shape[1]

    def body(u_ref, w_ref, x_ref, g_ref, x1_ref, y_ref):
        y = jnp.dot(u_ref[...], w_ref[...], preferred_element_type=F32)
        y_ref[...] = y.astype(BF16)
        x1_ref[...] = x_ref[...] + (y * _rms(y)) * g_ref[...]

    return _call(
        body, grid=(S // tm,),
        in_specs=[pl.BlockSpec((tm, K), lambda i: (i, 0)), _const((K, D), single=True),
                  pl.BlockSpec((tm, D), lambda i: (i, 0)), _const((1, D))],
        out_specs=[pl.BlockSpec((tm, D), lambda i: (i, 0)), pl.BlockSpec((tm, D), lambda i: (i, 0))],
        out_shape=[_sds((S, D), F32), _sds((S, D), BF16)], operands=(u, w, x, g),
        name=name, params=_params(("arbitrary",), 56), comm=comm)


def _out_loss(yy, w, x1, g, tgt, *, tm, name):
    S, K = yy.shape
    D = w.shape[1]

    def body(yy_ref, w_ref, x1_ref, g_ref, t_ref, dout_ref, dx2_ref, dyy_ref, lcol_ref, dg_ref):
        out = jnp.dot(yy_ref[...], w_ref[...], preferred_element_type=F32)
        r = _rms(out)
        n = out * r
        gg = g_ref[...]
        e = x1_ref[...] + n * gg - t_ref[...]
        dx2 = e * (1.0 / D)
        dx2_ref[...] = dx2
        dout = _norm_bwd(dx2 * gg, n, r).astype(BF16)
        dout_ref[...] = dout
        dyy_ref[...] = lax.dot_general(dout, w_ref[...], (((1,), (1,)), ((), ())),
                                       preferred_element_type=F32).astype(BF16)

        @pl.when(pl.program_id(0) == 0)
        def _():
            lcol_ref[...] = jnp.zeros_like(lcol_ref)
            dg_ref[...] = jnp.zeros_like(dg_ref)

        lcol_ref[...] += _colsum(e * e)
        dg_ref[...] += _colsum(dx2 * n)

    return _call(
        body, grid=(S // tm,),
        in_specs=[pl.BlockSpec((tm, K), lambda i: (i, 0)), _const((K, D), single=True),
                  pl.BlockSpec((tm, D), lambda i: (i, 0)), _const((1, D)),
                  pl.BlockSpec((tm, D), lambda i: (i, 0))],
        out_specs=[pl.BlockSpec((tm, D), lambda i: (i, 0)), pl.BlockSpec((tm, D), lambda i: (i, 0)),
                   pl.BlockSpec((tm, K), lambda i: (i, 0)), _const((1, D)), _const((1, D))],
        out_shape=[_sds((S, D), BF16), _sds((S, D), F32), _sds((S, K), BF16), _sds((1, D), F32), _sds((1, D), F32)],
        operands=(yy, w, x1, g, tgt), name=name, params=_params(("arbitrary",), 52))[0]


def _mm_nt(a, w, *, tm, tk, name, comm=None):
    S, N = a.shape
    D = w.shape[0]
    n_k = N // tk

    def body(a_ref, w_ref, o_ref, acc_ref):
        k = pl.program_id(1)

        @pl.when(k == 0)
        def _():
            acc_ref[...] = jnp.zeros_like(acc_ref)

        acc_ref[...] = lax.dot_general(a_ref[...], w_ref[...], (((1,), (1,)), ((), ())),
                                       preferred_element_type=F32) + acc_ref[...]

        @pl.when(k == n_k - 1)
        def _():
            o_ref[...] = acc_ref[...].astype(BF16)

    outs, extra = _call(
        body, grid=(S // tm, n_k),
        in_specs=[pl.BlockSpec((tm, tk), lambda i, k: (i, k)), pl.BlockSpec((D, tk), lambda i, k: (0, k))],
        out_specs=[pl.BlockSpec((tm, D), lambda i, k: (i, 0))],
        out_shape=[_sds((S, D), BF16)], operands=(a, w),
        scratch_shapes=[pltpu.VMEM((tm, D), F32)],
        name=name, params=_params(("arbitrary", "arbitrary"), 48), comm=comm)
    return outs[0], extra


def _mm_tn(a, b, *, ts, tn, name, comm=None):
    S, M = a.shape
    N = b.shape[1]
    n_s = S // ts

    def body(a_ref, b_ref, o_ref, acc_ref):
        s = pl.program_id(1)

        @pl.when(s == 0)
        def _():
            acc_ref[...] = jnp.zeros_like(acc_ref)

        acc_ref[...] = lax.dot_general(a_ref[...], b_ref[...], (((0,), (0,)), ((), ())),
                                       preferred_element_type=F32) + acc_ref[...]

        @pl.when(s == n_s - 1)
        def _():
            o_ref[...] = acc_ref[...].astype(BF16)

    outs, extra = _call(
        body, grid=(N // tn, n_s),
        in_specs=[pl.BlockSpec((ts, M), lambda j, s: (s, 0)), pl.BlockSpec((ts, tn), lambda j, s: (s, j))],
        out_specs=[pl.BlockSpec((M, tn), lambda j, s: (0, j))],
        out_shape=[_sds((M, N), BF16)], operands=(a, b),
        scratch_shapes=[pltpu.VMEM((M, tn), F32)],
        name=name, params=_params(("arbitrary", "arbitrary"), 48), comm=comm)
    return outs[0], extra


def _dw_reduce(order, a, b, nb, *, ts, name, comm=None):
    S, M = a.shape
    n_s = S // ts
    rows = 512

    def body(order_ref, a_ref, b_ref, sums_ref, from_sib_ref, from_chip_ref, acc, send_buf, mine_buf, recv_buf,
             sib_send, sib_recv, chip_send, chip_recv, dma_sems):
        t, s = pl.program_id(0), pl.program_id(1)
        x, y, c = _place()
        targets = [(1 - x, y, c), (x, 1 - y, c)]

        def to_sibling(k):
            return pltpu.make_async_remote_copy(
                src_ref=send_buf, dst_ref=from_sib_ref.at[k], send_sem=sib_send.at[k], recv_sem=sib_recv.at[k],
                device_id=(x, y, 1 - c), device_id_type=MESH)

        def to_chip(k):
            return pltpu.make_async_remote_copy(
                src_ref=sums_ref.at[k], dst_ref=from_chip_ref.at[k], send_sem=chip_send.at[k],
                recv_sem=chip_recv.at[k], device_id=targets[k], device_id_type=MESH)

        def finish(k):
            to_sibling(k).wait()
            get = pltpu.make_async_copy(from_sib_ref.at[k], recv_buf, dma_sems.at[0])
            get.start()
            get.wait()
            for r in range(0, M, rows):
                recv_buf[r:r + rows, :] = (mine_buf[r:r + rows, :].astype(F32)
                                           + recv_buf[r:r + rows, :].astype(F32)).astype(BF16)
            put = pltpu.make_async_copy(recv_buf, sums_ref.at[k], dma_sems.at[1])
            put.start()
            put.wait()
            if k < 2:
                to_chip(k).start()

        for k in range(3):
            @pl.when((t == k + 1) & (s == 0))
            def _(k=k):
                finish(k)

        @pl.when(s == 0)
        def _():
            acc[...] = jnp.zeros_like(acc)

        acc[...] = lax.dot_general(a_ref[...], b_ref[...], (((0,), (0,)), ((), ())),
                                   preferred_element_type=F32) + acc[...]

        @pl.when(s == n_s - 1)
        def _():
            for r in range(0, M, rows):
                lo, hi = acc[r:r + rows, :nb], acc[r:r + rows, nb:]
                send_buf[r:r + rows, :] = jnp.where(c == 0, hi, lo).astype(BF16)
                mine_buf[r:r + rows, :] = jnp.where(c == 0, lo, hi).astype(BF16)
            to_sibling(t).start()

        @pl.when((t == 3) & (s == n_s - 1))
        def _():
            finish(3)
            to_chip(0).wait()
            to_chip(1).wait()

    piece = _sds((4, M, nb), BF16)
    outs, extra = _call(
        body, grid=(4, n_s), prefetch=order,
        in_specs=[pl.BlockSpec((ts, M), lambda t, s, o: (s, 0)), pl.BlockSpec((ts, 2 * nb), lambda t, s, o: (s, o[t]))],
        out_specs=[ANY, ANY, ANY], out_shape=[piece, piece, _sds((3, M, nb), BF16)], operands=(a, b),
        scratch_shapes=[pltpu.VMEM((M, 2 * nb), F32), pltpu.VMEM((M, nb), BF16), pltpu.VMEM((M, nb), BF16),
                        pltpu.VMEM((M, nb), BF16), pltpu.SemaphoreType.DMA((4,)), pltpu.SemaphoreType.DMA((4,)),
                        pltpu.SemaphoreType.DMA((2,)), pltpu.SemaphoreType.DMA((2,)), pltpu.SemaphoreType.DMA((2,))],
        name=name, params=_params(("arbitrary", "arbitrary"), 56), comm=comm)
    return outs[0], outs[2], extra


def _diag_comm(sums, from_chip):
    def copy(ins, outs, sems):
        x, y, c = _place()
        return pltpu.make_async_remote_copy(
            src_ref=ins[0].at[2], dst_ref=outs[0].at[2], send_sem=sems[0].at[0], recv_sem=sems[1].at[0],
            device_id=(1 - x, 1 - y, c), device_id_type=MESH)

    def start(ins, outs, sems):
        copy(ins, outs, sems).start()

    def finish(ins, outs, sems):
        copy(ins, outs, sems).wait()

    sems = [pltpu.SemaphoreType.DMA((1,)), pltpu.SemaphoreType.DMA((1,))]
    return _Comm([sums, from_chip], [_sds(from_chip.shape, from_chip.dtype)], sems, start, finish, aliases={1: 0})


def _pre_bwd_o(dh, x1, dx2, y0, g_pre, g_post, *, tm, name, comm=None):
    S, D = x1.shape

    def body(dh_ref, x1_ref, dx2_ref, y0_ref, gpre_ref, gpost_ref, dx1_ref, dy0_ref, dgpre_ref, dgpost_ref):
        @pl.when(pl.program_id(0) == 0)
        def _():
            dgpre_ref[...] = jnp.zeros_like(dgpre_ref)
            dgpost_ref[...] = jnp.zeros_like(dgpost_ref)

        dh = dh_ref[...].astype(F32)
        x1 = x1_ref[...]
        r2 = _rms(x1)
        xn = x1 * r2
        dgpre_ref[...] += _colsum(dh * xn)
        dx1 = dx2_ref[...] + _norm_bwd(dh * gpre_ref[...], xn, r2)
        dx1_ref[...] = dx1
        y = y0_ref[...].astype(F32)
        r1 = _rms(y)
        n1 = y * r1
        dgpost_ref[...] += _colsum(dx1 * n1)
        dy0_ref[...] = _norm_bwd(dx1 * gpost_ref[...], n1, r1).astype(BF16)

    row = pl.BlockSpec((tm, D), lambda i: (i, 0))
    return _call(
        body, grid=(S // tm,),
        in_specs=[row, row, row, row, _const((1, D)), _const((1, D))],
        out_specs=[row, row, _const((1, D)), _const((1, D))],
        out_shape=[_sds((S, D), F32), _sds((S, D), BF16), _sds((1, D), F32), _sds((1, D), F32)],
        operands=(dh, x1, dx2, y0, g_pre, g_post),
        name=name, params=_params(("arbitrary",), 48), comm=comm)


def _pre_bwd_e(dh, x, dx1, g_pre, *, tm, name):
    S, D = x.shape

    def body(dh_ref, x_ref, dx1_ref, gpre_ref, gx_ref, dgpre_ref):
        @pl.when(pl.program_id(0) == 0)
        def _():
            dgpre_ref[...] = jnp.zeros_like(dgpre_ref)

        dh = dh_ref[...].astype(F32)
        xx = x_ref[...]
        r0 = _rms(xx)
        xn = xx * r0
        dgpre_ref[...] += _colsum(dh * xn)
        gx_ref[...] = dx1_ref[...] + _norm_bwd(dh * gpre_ref[...], xn, r0)

    row = pl.BlockSpec((tm, D), lambda i: (i, 0))
    return _call(
        body, grid=(S // tm,),
        in_specs=[row, row, row, _const((1, D))],
        out_specs=[row, _const((1, D))],
        out_shape=[_sds((S, D), F32), _sds((1, D), F32)],
        operands=(dh, x, dx1, g_pre), name=name, params=_params(("arbitrary",), 56))[0]


SUBLANES = 8


def _shift_copies(sh_ref, ext_ref, cs):
    for b in range(1, SUBLANES):
        sh_ref[b - 1] = ext_ref[pl.ds(b, sh_ref.shape[1]), cs]


def _rows_at(ext_ref, sh_ref, off, cs, tm):
    b = off % SUBLANES
    if b == 0 or sh_ref is None:
        return ext_ref[pl.ds(off, tm), cs]
    return sh_ref[b - 1, pl.ds(off - b, tm), :]


def _taps(ext_ref, w_ref, n_taps, base, cs, tm, sh_ref=None):
    acc = _rows_at(ext_ref, sh_ref, base, cs, tm) * w_ref[0:1, cs]
    for k in range(1, n_taps):
        acc = acc + _rows_at(ext_ref, sh_ref, base + k, cs, tm) * w_ref[k:k + 1, cs]
    return acc


def _taps_rev(ext_ref, w_ref, n_taps, cs, tm, sh_ref=None):
    acc = _rows_at(ext_ref, sh_ref, n_taps - 1, cs, tm) * w_ref[0:1, cs]
    for k in range(1, n_taps):
        acc = acc + _rows_at(ext_ref, sh_ref, n_taps - 1 - k, cs, tm) * w_ref[k:k + 1, cs]
    return acc


def _e_mix_fwd(p, wa, wb, bias, ln_g, ln_b, *, tm, name, comm=None):
    S = p.shape[0]
    W = p.shape[1] // 7
    nb = tm // HALO
    chunks = [slice(c * LANES, (c + 1) * LANES) for c in range(W // LANES)]

    def body(p_ref, hax_ref, hac_ref, hbv_ref, hbg_ref, wa_ref, wb_ref, bias_ref, lg_ref, lb_ref,
             u_ref, cb_ref, ext_ref, sh_ref):
        keep = (pl.program_id(0) > 0).astype(F32)
        col = lambda j, cs: p_ref[:, j * W + cs.start:j * W + cs.stop].astype(F32)

        ext_ref[0:HALO, :] = hax_ref[...].astype(F32) * hac_ref[...].astype(F32) * keep
        ext_ref[HALO:, :] = p_ref[:, 2 * W:3 * W].astype(F32) * p_ref[:, 0:W].astype(F32)
        for cs in chunks:
            conv = _taps(ext_ref, wa_ref, CONV_A, HALO - (CONV_A - 1), cs, tm)
            az = col(3, cs)
            u_ref[:, cs] = (col(1, cs) * conv * (az * _sig(az))).astype(BF16)

        ext_ref[0:HALO, :] = hbv_ref[...].astype(F32) * _sig(hbg_ref[...].astype(F32)) * keep
        ext_ref[HALO:, :] = p_ref[:, 4 * W:5 * W].astype(F32) * _sig(p_ref[:, 5 * W:6 * W].astype(F32))
        s1 = jnp.zeros((tm, LANES), F32)
        for cs in chunks:
            _shift_copies(sh_ref, ext_ref, cs)
            cb = _taps(ext_ref, wb_ref, CONV_B, HALO - (CONV_B - 1), cs, tm, sh_ref) + bias_ref[:, cs]
            cb_ref[:, cs] = cb
            s1 = s1 + cb
        mu = jnp.sum(s1, axis=-1, keepdims=True) * (1.0 / W)
        s2 = jnp.zeros((tm, LANES), F32)
        for cs in chunks:
            xc = cb_ref[:, cs] - mu
            s2 = s2 + xc * xc
        rs = lax.rsqrt(jnp.sum(s2, axis=-1, keepdims=True) * (1.0 / W) + EPS)
        for cs in chunks:
            lb = (cb_ref[:, cs] - mu) * rs * lg_ref[:, cs] + lb_ref[:, cs]
            bz = col(6, cs)
            u_ref[:, W + cs.start:W + cs.stop] = (lb * _sig(lb) * (bz * _sig(bz))).astype(BF16)

    prev = lambda j: pl.BlockSpec((HALO, W), lambda i: (jnp.maximum(i * nb - 1, 0), j))
    return _call(
        body, grid=(S // tm,),
        in_specs=[pl.BlockSpec((tm, 7 * W), lambda i: (i, 0)), prev(0), prev(2), prev(4), prev(5),
                  _const((CONV_A, W)), _const((CONV_B, W)), _const((1, W)), _const((1, W)), _const((1, W))],
        out_specs=[pl.BlockSpec((tm, 2 * W), lambda i: (i, 0)), pl.BlockSpec((tm, W), lambda i: (i, 0))],
        out_shape=[_sds((S, 2 * W), BF16), _sds((S, W), F32)],
        operands=(p, p, p, p, p, wa, wb, bias, ln_g, ln_b),
        scratch_shapes=[pltpu.VMEM((HALO + tm, W), F32),
                        pltpu.VMEM((SUBLANES - 1, HALO + tm - SUBLANES, LANES), F32)],
        name=name, params=_params(("arbitrary",), 48), comm=comm)


def _e_mix_bwd(du, p, cb, wa, wb, ln_g, ln_b, *, tm, name, comm=None):
    S = p.shape[0]
    W = p.shape[1] // 7
    nb = tm // HALO
    n_t = S // tm
    last_blk = S // HALO - 1
    chunks = [slice(c * LANES, (c + 1) * LANES) for c in range(W // LANES)]

    def body(du_ref, duf_ref, p_ref, fab_ref, faz_ref, fbz_ref, hax_ref, hac_ref, hbv_ref, hbg_ref,
             cb_ref, cbf_ref, wa_ref, wb_ref, lg_ref, lb_ref,
             dp_ref, dwa_ref, dwb_ref, dbias_ref, dlg_ref, dlb_ref, extd_ref, extg_ref, shd_ref, shg_ref):
        i = pl.program_id(0)
        keep_prev = (i > 0).astype(F32)
        keep_next = (i < n_t - 1).astype(F32)
        col = lambda j, cs: p_ref[:, j * W + cs.start:j * W + cs.stop].astype(F32)

        @pl.when(i == 0)
        def _():
            dwa_ref[...] = jnp.zeros_like(dwa_ref)
            dwb_ref[...] = jnp.zeros_like(dwb_ref)
            dbias_ref[...] = jnp.zeros_like(dbias_ref)
            dlg_ref[...] = jnp.zeros_like(dlg_ref)
            dlb_ref[...] = jnp.zeros_like(dlb_ref)

        def dcb_rows(rows, cb_rows_ref, dub, bz_of, dst0, scale, main):
            cbv = cb_rows_ref[...]
            mu = jnp.mean(cbv, axis=-1, keepdims=True)
            xc = cbv - mu
            rs = lax.rsqrt(jnp.mean(xc * xc, axis=-1, keepdims=True) + EPS)
            m1 = jnp.zeros((rows, LANES), F32)
            m2 = jnp.zeros((rows, LANES), F32)
            for cs in chunks:
                nbv = (cb_rows_ref[:, cs] - mu) * rs
                lb = nbv * lg_ref[:, cs] + lb_ref[:, cs]
                sl = _sig(lb)
                bz = bz_of(cs)
                sz = _sig(bz)
                dub_c = dub(cs)
                dlb = dub_c * (bz * sz) * _dsilu(lb, sl)
                if main:
                    dlg_ref[:, cs] += _colsum(dlb * nbv)
                    dlb_ref[:, cs] += _colsum(dlb)
                    dp_ref[:, 6 * W + cs.start:6 * W + cs.stop] = (dub_c * (lb * sl) * _dsilu(bz, sz)).astype(BF16)
                dnb = dlb * lg_ref[:, cs]
                extd_ref[dst0:dst0 + rows, cs] = dnb
                m1 = m1 + dnb
                m2 = m2 + dnb * nbv
            m1 = jnp.sum(m1, axis=-1, keepdims=True) * (1.0 / W)
            m2 = jnp.sum(m2, axis=-1, keepdims=True) * (1.0 / W)
            for cs in chunks:
                nbv = (cb_rows_ref[:, cs] - mu) * rs
                dcb = rs * (extd_ref[dst0:dst0 + rows, cs] - m1 - nbv * m2) * scale
                extd_ref[dst0:dst0 + rows, cs] = dcb
                if main:
                    dbias_ref[:, cs] += _colsum(dcb)

        dcb_rows(tm, cb_ref, lambda cs: du_ref[:, W + cs.start:W + cs.stop].astype(F32),
                 lambda cs: col(6, cs), 0, 1.0, True)
        dcb_rows(HALO, cbf_ref, lambda cs: duf_ref[:, W + cs.start:W + cs.stop].astype(F32),
                 lambda cs: fbz_ref[:, cs].astype(F32), tm, keep_next, False)

        extg_ref[0:HALO, :] = hbv_ref[...].astype(F32) * _sig(hbg_ref[...].astype(F32)) * keep_prev
        extg_ref[HALO:, :] = p_ref[:, 4 * W:5 * W].astype(F32) * _sig(p_ref[:, 5 * W:6 * W].astype(F32))
        base_b = HALO - (CONV_B - 1)
        for cs in chunks:
            _shift_copies(shd_ref, extd_ref, cs)
            _shift_copies(shg_ref, extg_ref, cs)
            dgb = _taps_rev(extd_ref, wb_ref, CONV_B, cs, tm, shd_ref)
            bv = col(4, cs)
            sg = _sig(col(5, cs))
            dp_ref[:, 4 * W + cs.start:4 * W + cs.stop] = (dgb * sg).astype(BF16)
            dp_ref[:, 5 * W + cs.start:5 * W + cs.stop] = (dgb * bv * sg * (1.0 - sg)).astype(BF16)
            dcb = extd_ref[0:tm, cs]
            for k in range(CONV_B):
                dwb_ref[k:k + 1, cs] += _colsum(dcb * _rows_at(extg_ref, shg_ref, base_b + k, cs, tm))

        extg_ref[0:HALO, :] = hax_ref[...].astype(F32) * hac_ref[...].astype(F32) * keep_prev
        extg_ref[HALO:, :] = p_ref[:, 2 * W:3 * W].astype(F32) * p_ref[:, 0:W].astype(F32)
        base_a = HALO - (CONV_A - 1)
        for cs in chunks:
            conv = _taps(extg_ref, wa_ref, CONV_A, base_a, cs, tm)
            az = col(3, cs)
            sz = _sig(az)
            ab = col(1, cs)
            dua = du_ref[:, cs].astype(F32)
            dya = dua * (az * sz)
            dp_ref[:, W + cs.start:W + cs.stop] = (dya * conv).astype(BF16)
            dp_ref[:, 3 * W + cs.start:3 * W + cs.stop] = (dua * (ab * conv) * _dsilu(az, sz)).astype(BF16)
            extd_ref[0:tm, cs] = dya * ab
            azf = faz_ref[:, cs].astype(F32)
            extd_ref[tm:tm + HALO, cs] = (duf_ref[:, cs].astype(F32) * (azf * _sig(azf))
                                          * fab_ref[:, cs].astype(F32) * keep_next)
        for cs in chunks:
            dca = _taps_rev(extd_ref, wa_ref, CONV_A, cs, tm)
            dp_ref[:, cs] = (dca * col(2, cs)).astype(BF16)
            dp_ref[:, 2 * W + cs.start:2 * W + cs.stop] = (dca * col(0, cs)).astype(BF16)
            dconv = extd_ref[0:tm, cs]
            for k in range(CONV_A):
                dwa_ref[k:k + 1, cs] += _colsum(dconv * extg_ref[pl.ds(base_a + k, tm), cs])

    prev = lambda j: pl.BlockSpec((HALO, W), lambda i: (jnp.maximum(i * nb - 1, 0), j))
    nxt = lambda j, w: pl.BlockSpec((HALO, w), lambda i: (jnp.minimum((i + 1) * nb, last_blk), j))
    row = lambda w: pl.BlockSpec((tm, w), lambda i: (i, 0))
    return _call(
        body, grid=(n_t,),
        in_specs=[row(2 * W), nxt(0, 2 * W), row(7 * W), nxt(1, W), nxt(3, W), nxt(6, W),
                  prev(0), prev(2), prev(4), prev(5), row(W), nxt(0, W),
                  _const((CONV_A, W)), _const((CONV_B, W)), _const((1, W)), _const((1, W))],
        out_specs=[row(7 * W), _const((CONV_A, W)), _const((CONV_B, W)), _const((1, W)), _const((1, W)), _const((1, W))],
        out_shape=[_sds((S, 7 * W), BF16), _sds((CONV_A, W), F32), _sds((CONV_B, W), F32),
                   _sds((1, W), F32), _sds((1, W), F32), _sds((1, W), F32)],
        operands=(du, du, p, p, p, p, p, p, p, p, cb, cb, wa, wb, ln_g, ln_b),
        scratch_shapes=[pltpu.VMEM((tm + HALO, W), F32), pltpu.VMEM((HALO + tm, W), F32),
                        pltpu.VMEM((SUBLANES - 1, HALO + tm - SUBLANES, LANES), F32),
                        pltpu.VMEM((SUBLANES - 1, HALO + tm - SUBLANES, LANES), F32)],
        name=name, params=_params(("arbitrary",), 52), comm=comm)


def _counts(i, tm, rows, off, win):
    t = i * tm + off + lax.broadcasted_iota(jnp.int32, (rows, 1), 0)
    return jnp.minimum(t + 1, win).astype(F32)


def _o_mix_fwd(q, cw, cb, cscale, *, tm, name):
    S = q.shape[0]
    WC = q.shape[1] // 2
    NG = len(POOL_WINDOWS)
    G = WC // NG
    nb = tm // PHALO

    def body(v_ref, z_ref, hv_ref, cw_ref, cb_ref, sc_ref, yy_ref, pooled_ref, gg_ref, ext_ref):
        i = pl.program_id(0)
        keep = (i > 0).astype(F32)
        for g, win in enumerate(POOL_WINDOWS):
            cs = slice(g * G, (g + 1) * G)
            v = v_ref[:, cs].astype(F32)
            ext_ref[0:PHALO, :] = hv_ref[:, cs].astype(F32) * keep
            ext_ref[PHALO:, :] = v
            s = v
            for j in range(1, win):
                s = s + ext_ref[pl.ds(PHALO - j, tm), :]
            pooled = (s / _counts(i, tm, tm, 0, win) - v).astype(BF16)
            pooled_ref[:, cs] = pooled
            gg = jnp.dot(pooled, cw_ref[g], preferred_element_type=F32) + cb_ref[:, cs]
            gg_ref[:, cs] = gg.astype(BF16)
            z = z_ref[:, cs].astype(F32)
            yy_ref[:, cs] = (gg * sc_ref[:, cs] * (z * _sig(z))).astype(BF16)

    row = lambda j: pl.BlockSpec((tm, WC), lambda i: (i, j))
    out = pl.BlockSpec((tm, WC), lambda i: (i, 0))
    return _call(
        body, grid=(S // tm,),
        in_specs=[row(0), row(1), pl.BlockSpec((PHALO, WC), lambda i: (jnp.maximum(i * nb - 1, 0), 0)),
                  _const((NG, G, G)), _const((1, WC)), _const((1, WC))],
        out_specs=[out, out, out],
        out_shape=[_sds((S, WC), BF16)] * 3, operands=(q, q, q, cw, cb, cscale),
        scratch_shapes=[pltpu.VMEM((PHALO + tm, G), F32)],
        name=name, params=_params(("arbitrary",), 40))[0]


def _o_mix_bwd(dyy, q, gg, pooled, cw, cscale, *, tm, name):
    S = q.shape[0]
    WC = q.shape[1] // 2
    NG = len(POOL_WINDOWS)
    G = WC // NG
    nb = tm // PHALO
    n_t = S // tm
    last_blk = S // PHALO - 1
    nt = (((1,), (1,)), ((), ()))
    tn = (((0,), (0,)), ((), ()))

    def body(dyy_ref, dyyf_ref, z_ref, zf_ref, gg_ref, pooled_ref, cw_ref, sc_ref,
             dq_ref, dcw_ref, dcb_ref, dsc_ref, ext_ref):
        i = pl.program_id(0)
        keep_next = (i < n_t - 1).astype(F32)

        @pl.when(i == 0)
        def _():
            dcw_ref[...] = jnp.zeros_like(dcw_ref)
            dcb_ref[...] = jnp.zeros_like(dcb_ref)
            dsc_ref[...] = jnp.zeros_like(dsc_ref)

        for g, win in enumerate(POOL_WINDOWS):
            cs = slice(g * G, (g + 1) * G)
            sc = sc_ref[:, cs]
            z = z_ref[:, cs].astype(F32)
            sz = _sig(z)
            dyy_c = dyy_ref[:, cs].astype(F32)
            ggv = gg_ref[:, cs].astype(F32)
            dyy0 = dyy_c * (z * sz)
            dq_ref[:, WC + cs.start:WC + cs.stop] = (dyy_c * (ggv * sc) * _dsilu(z, sz)).astype(BF16)
            dgg = dyy0 * sc
            dsc_ref[:, cs] += _colsum(dyy0 * ggv)
            dcb_ref[:, cs] += _colsum(dgg)
            dgg_b = dgg.astype(BF16)
            dcw_ref[g] += lax.dot_general(pooled_ref[:, cs], dgg_b, tn, preferred_element_type=F32)
            dpool = lax.dot_general(dgg_b, cw_ref[g], nt, preferred_element_type=F32)
            zf = zf_ref[:, cs].astype(F32)
            dgg_f = (dyyf_ref[:, cs].astype(F32) * (zf * _sig(zf)) * sc * keep_next).astype(BF16)
            dpool_f = lax.dot_general(dgg_f, cw_ref[g], nt, preferred_element_type=F32)
            ext_ref[0:tm, :] = dpool / _counts(i, tm, tm, 0, win)
            ext_ref[tm:tm + PHALO, :] = dpool_f / _counts(i, tm, PHALO, tm, win)
            dv = ext_ref[0:tm, :] - dpool
            for j in range(1, win):
                dv = dv + ext_ref[pl.ds(j, tm), :]
            dq_ref[:, cs] = dv.astype(BF16)

    row = lambda: pl.BlockSpec((tm, WC), lambda i: (i, 0))
    nxt = lambda j: pl.BlockSpec((PHALO, WC), lambda i: (jnp.minimum((i + 1) * nb, last_blk), j))
    return _call(
        body, grid=(n_t,),
        in_specs=[row(), nxt(0), pl.BlockSpec((tm, WC), lambda i: (i, 1)), nxt(1), row(), row(),
                  _const((NG, G, G)), _const((1, WC))],
        out_specs=[pl.BlockSpec((tm, 2 * WC), lambda i: (i, 0)), _const((NG, G, G)), _const((1, WC)), _const((1, WC))],
        out_shape=[_sds((S, 2 * WC), BF16), _sds((NG, G, G), F32), _sds((1, WC), F32), _sds((1, WC), F32)],
        operands=(dyy, dyy, q, q, gg, pooled, cw, cscale),
        scratch_shapes=[pltpu.VMEM((tm + PHALO, G), F32)],
        name=name, params=_params(("arbitrary",), 48))[0]


def _place():
    return lax.axis_index("x"), lax.axis_index("y"), lax.axis_index("c")


def _piece(ref, axis, size, index):
    start = index * size
    if axis == len(ref.shape) - 1:
        start = pl.multiple_of(start, LANES)
    idx = [slice(None)] * len(ref.shape)
    idx[axis] = pl.ds(start, size)
    return ref.at[tuple(idx)]


def _gather_copies(src, out, axis, size, send_sems, recv_sems, base, held=None):
    x, y, c = _place()
    sib, xn, yn = (x, y, 1 - c), (1 - x, y, c), (x, 1 - y, c)

    def blk(px, py, of=out):
        return _piece(of, axis, size, 4 * px + 2 * py + c)

    def half(ref, h):
        n = ref.shape[0] // 2
        return ref.at[pl.ds(h * n, n)]

    def rc(k, s, d, to):
        return pltpu.make_async_remote_copy(src_ref=s, dst_ref=d, send_sem=send_sems.at[base + k],
                                            recv_sem=recv_sems.at[base + k], device_id=to, device_id_type=MESH)

    own, xb, yb, db = blk(x, y), blk(1 - x, y), blk(x, 1 - y), blk(1 - x, 1 - y)
    got = out if held is None else held
    xs, ys, ds = blk(1 - x, y, got), blk(x, 1 - y, got), blk(1 - x, 1 - y, got)
    return [rc(0, src, own, sib), rc(1, src, own, xn), rc(2, src, own, yn),
            rc(3, half(xs, 0), half(xb, 0), yn), rc(4, half(ys, 1), half(yb, 1), xn),
            rc(5, xs, xb, sib), rc(6, ys, yb, sib), rc(7, ds, db, sib)]


N_GATHER = 8


def _gather_comm(shards, axes, phases):
    n = len(shards)
    if phases == "second":
        sizes = [s.shape[a] // N_DEV for s, a in zip(shards, axes)]
        full = [_sds(s.shape, s.dtype) for s in shards]
    else:
        sizes = [s.shape[a] for s, a in zip(shards, axes)]
        full = [_sds(s.shape[:a] + (N_DEV * s.shape[a],) + s.shape[a + 1:], s.dtype) for s, a in zip(shards, axes)]

    def plan(ins, outs, sems):
        x, y, c = _place()
        me = 4 * x + 2 * y + c
        if phases == "second":
            cps = [_gather_copies(_piece(ins[t], axes[t], sizes[t], me), outs[t], axes[t], sizes[t], sems[0], sems[1],
                                  N_GATHER * t, ins[t]) for t in range(n)]
        else:
            cps = [_gather_copies(sems[3 + t], outs[t], axes[t], sizes[t], sems[0], sems[1], N_GATHER * t)
                   for t in range(n)]
        mine = [pltpu.make_async_copy(sems[3 + t], _piece(outs[t], axes[t], sizes[t], me), sems[2].at[t])
                for t in range(n)] if phases != "second" else []
        return cps, mine

    def send_own(ins, outs, sems):
        cps, mine = plan(ins, outs, sems)
        for t in range(n):
            stage = pltpu.make_async_copy(ins[t], sems[3 + t], sems[2].at[t])
            stage.start()
            stage.wait()
            mine[t].start()
            for k in (0, 1, 2):
                cps[t][k].start()

    def pass_on(ins, outs, sems):
        cps, _ = plan(ins, outs, sems)
        for t in range(n):
            if phases == "all":
                cps[t][1].wait_recv()
            cps[t][3].start()
            cps[t][5].start()
        for t in range(n):
            if phases == "all":
                cps[t][2].wait_recv()
            cps[t][4].start()
            cps[t][6].start()

    def own_landed(ins, outs, sems):
        cps, mine = plan(ins, outs, sems)
        for t in range(n):
            for k in (0, 1, 2):
                cps[t][k].wait()
            mine[t].wait()

    def all_landed(ins, outs, sems):
        cps, mine = plan(ins, outs, sems)
        for t in range(n):
            cps[t][3].wait_recv()
            cps[t][4].wait_recv()
            cps[t][7].start()
        for t in range(n):
            for k in ((0, 5, 6, 7) if phases == "all" else (5, 6, 7)):
                cps[t][k].wait_recv()
            for k in (range(N_GATHER) if phases == "all" else range(3, N_GATHER)):
                cps[t][k].wait_send()
            if phases == "all":
                mine[t].wait()

    sems = [pltpu.SemaphoreType.DMA((N_GATHER * n,)), pltpu.SemaphoreType.DMA((N_GATHER * n,))]
    if phases != "second":
        sems.append(pltpu.SemaphoreType.DMA((n,)))
        sems += [pltpu.VMEM(s.shape, s.dtype) for s in shards]
    if phases == "all":
        return _Comm(shards, full, sems, send_own, all_landed, middle=pass_on)
    if phases == "first":
        return _Comm(shards, full, sems, send_own, own_landed)
    return _Comm(shards, full, sems, pass_on, all_landed, aliases={t: t for t in range(n)})


def _pair_comm(grads, axes, sizes):
    n = len(grads)
    outs_sds = [_sds((4,) + g.shape[:a] + (s,) + g.shape[a + 1:], g.dtype) for g, a, s in zip(grads, axes, sizes)]

    def copies(ins, outs, sems):
        send_sems, recv_sems = sems
        x, y, c = _place()
        return [pltpu.make_async_remote_copy(
            src_ref=_piece(ins[t], axes[t], sizes[t], 2 * qi + (1 - c)), dst_ref=outs[t].at[qi],
            send_sem=send_sems.at[4 * t + qi], recv_sem=recv_sems.at[4 * t + qi],
            device_id=(x, y, 1 - c), device_id_type=MESH) for t in range(n) for qi in range(4)]

    def start(ins, outs, sems):
        for cp in copies(ins, outs, sems):
            cp.start()

    def finish(ins, outs, sems):
        for cp in copies(ins, outs, sems):
            cp.wait()

    sems = [pltpu.SemaphoreType.DMA((4 * n,)), pltpu.SemaphoreType.DMA((4 * n,))]
    return _Comm(grads, outs_sds, sems, start, finish)


def _chip_comm(sums):
    n = len(sums)
    outs_sds = [_sds((3,) + s.shape[1:], s.dtype) for s in sums]

    def copies(ins, outs, sems):
        send_sems, recv_sems = sems
        x, y, c = _place()
        return [pltpu.make_async_remote_copy(
            src_ref=ins[t].at[2 * qx + qy], dst_ref=outs[t].at[j],
            send_sem=send_sems.at[3 * t + j], recv_sem=recv_sems.at[3 * t + j],
            device_id=(qx, qy, c), device_id_type=MESH)
            for t in range(n) for j, (qx, qy) in enumerate([(1 - x, y), (x, 1 - y), (1 - x, 1 - y)])]

    def start(ins, outs, sems):
        for cp in copies(ins, outs, sems):
            cp.start()

    def finish(ins, outs, sems):
        for cp in copies(ins, outs, sems):
            cp.wait()

    sems = [pltpu.SemaphoreType.DMA((3 * n,)), pltpu.SemaphoreType.DMA((3 * n,))]
    return _Comm(sums, outs_sds, sems, start, finish)


def _small_comm(small):
    def copies(ins, outs, sems):
        send_sems, recv_sems, local_sem = sems
        x, y, c = _place()
        mine = outs[0].at[4 * x + 2 * y + c]
        out = [pltpu.make_async_copy(ins[0], mine, local_sem.at[0])]
        for k in range(1, N_DEV):
            peer = (1 - x if k & 4 else x, 1 - y if k & 2 else y, 1 - c if k & 1 else c)
            out.append(pltpu.make_async_remote_copy(
                src_ref=ins[0], dst_ref=mine, send_sem=send_sems.at[k - 1], recv_sem=recv_sems.at[k - 1],
                device_id=peer, device_id_type=MESH))
        return out

    def start(ins, outs, sems):
        for cp in copies(ins, outs, sems):
            cp.start()

    def finish(ins, outs, sems):
        for cp in copies(ins, outs, sems):
            cp.wait()

    sems = [pltpu.SemaphoreType.DMA((N_DEV - 1,)), pltpu.SemaphoreType.DMA((N_DEV - 1,)), pltpu.SemaphoreType.DMA((1,))]
    return _Comm([small], [_sds((N_DEV,) + small.shape, small.dtype)], sems, start, finish)


def _small_scatter_comm(send):
    def copies(ins, outs, sems):
        send_sems, recv_sems, local_sem = sems
        x, y, c = _place()
        me = 4 * x + 2 * y + c
        out = [pltpu.make_async_copy(ins[0].at[me], outs[0].at[me], local_sem.at[0])]
        for k in range(1, N_DEV):
            px, py, pc = (1 - x if k & 4 else x, 1 - y if k & 2 else y, 1 - c if k & 1 else c)
            out.append(pltpu.make_async_remote_copy(
                src_ref=ins[0].at[4 * px + 2 * py + pc], dst_ref=outs[0].at[me], send_sem=send_sems.at[k - 1],
                recv_sem=recv_sems.at[k - 1], device_id=(px, py, pc), device_id_type=MESH))
        return out

    def start(ins, outs, sems):
        for cp in copies(ins, outs, sems):
            cp.start()

    def finish(ins, outs, sems):
        for cp in copies(ins, outs, sems):
            cp.wait()

    sems = [pltpu.SemaphoreType.DMA((N_DEV - 1,)), pltpu.SemaphoreType.DMA((N_DEV - 1,)), pltpu.SemaphoreType.DMA((1,))]
    return _Comm([send], [_sds(send.shape, send.dtype)], sems, start, finish)


def _pair_sum(c_idx, grad, recv, axis, size, split, *, name):
    nd = len(grad.shape)
    piece = grad.shape[:axis] + (size,) + grad.shape[axis + 1:]
    blk = (piece[0] // split,) + piece[1:]

    def g_map(q, r, c_ref):
        idx = [0] * nd
        idx[axis] = 2 * q + c_ref[0]
        idx[0] = idx[0] * split + r if axis == 0 else r
        return tuple(idx)

    def r_map(q, r, c_ref):
        return (q, r) + (0,) * (nd - 1)

    def body(c_ref, g_ref, r_ref, o_ref):
        o_ref[0] = (g_ref[...].astype(F32) + r_ref[0].astype(F32)).astype(BF16)

    return _call(
        body, grid=(4, split), prefetch=c_idx,
        in_specs=[pl.BlockSpec(blk, g_map), pl.BlockSpec((1,) + blk, r_map)],
        out_specs=[pl.BlockSpec((1,) + blk, r_map)], out_shape=[_sds((4,) + piece, BF16)],
        operands=(grad, recv), name=name, params=_params(("arbitrary", "arbitrary"), 32))[0][0]


def _adam_math(w, g, m, v):
    m = ADAM_B1 * m + (1.0 - ADAM_B1) * g
    v = ADAM_B2 * v + (1.0 - ADAM_B2) * (g * g)
    m_hat = m / (1.0 - ADAM_B1 ** ADAM_STEP)
    v_hat = v / (1.0 - ADAM_B2 ** ADAM_STEP)
    delta = -ADAM_LR * (m_hat / (jnp.sqrt(v_hat) + ADAM_EPS) + ADAM_WD * w)
    return delta, m, v


def _adam_big(q_idx, sums, recv, w, m, v, split, *, name, comm=None):
    shape = w.shape
    nd = len(shape)
    blk = (shape[0] // split,) + shape[1:]
    w_map = lambda r, q_ref: (r,) + (0,) * (nd - 1)
    s_map = lambda r, q_ref: (q_ref[0], r) + (0,) * (nd - 1)
    r_map = lambda r, q_ref: (0, r) + (0,) * (nd - 1)

    def body(q_ref, s_ref, r_ref, w_ref, m_ref, v_ref, g_ref, d_ref, nm_ref, nv_ref):
        g = s_ref[0].astype(F32) + r_ref[0].astype(F32) + r_ref[1].astype(F32) + r_ref[2].astype(F32)
        g_ref[...] = g
        d_ref[...], nm_ref[...], nv_ref[...] = _adam_math(w_ref[...], g, m_ref[...], v_ref[...])

    wspec = pl.BlockSpec(blk, w_map)
    return _call(
        body, grid=(split,), prefetch=q_idx,
        in_specs=[pl.BlockSpec((1,) + blk, s_map), pl.BlockSpec((3,) + blk, r_map), wspec, wspec, wspec],
        out_specs=[wspec] * 4, out_shape=[_sds(shape, F32)] * 4, operands=(sums, recv, w, m, v),
        name=name, params=_params(("arbitrary",), 32), comm=comm)


def _adam_small(parts, w, m, v, *, name):
    R = w.shape[0]

    def body(p_ref, w_ref, m_ref, v_ref, g_ref, d_ref, nm_ref, nv_ref):
        g = p_ref[0]
        for d in range(1, N_DEV):
            g = g + p_ref[d]
        g_ref[...] = g
        d_ref[...], nm_ref[...], nv_ref[...] = _adam_math(w_ref[...], g, m_ref[...], v_ref[...])

    whole = _const((R, LANES))
    return _call(
        body, grid=(1,), in_specs=[_const((N_DEV, R, LANES)), whole, whole, whole], out_specs=[whole] * 4,
        out_shape=[_sds((R, LANES), F32)] * 4, operands=(parts, w, m, v), name=name,
        params=_params(("arbitrary",), 32))[0]


def _pack(arrs):
    return jnp.concatenate([a.reshape(-1) for a in arrs]).reshape(-1, LANES)


def _unpack(packed, shapes):
    flat = packed.reshape(-1)
    out, off = [], 0
    for s in shapes:
        n = 1
        for d in s:
            n *= d
        out.append(flat[off:off + n].reshape(s))
        off += n
    return out


BIG = ("e_in", "e_out", "o_in", "o_cw", "o_out")
BIG_AXIS = dict(e_in=1, e_out=0, o_in=1, o_cw=1, o_out=0)
BIG_SPLIT = dict(e_in=8, e_out=4, o_in=4, o_cw=4, o_out=4)
REPLICATED = ("e_norm_pre", "e_norm_post", "e_b_conv_bias", "e_b_ln_g", "e_b_ln_b")
SHARDED = ("e_a_conv", "e_b_conv", "o_norm_pre", "o_norm_post", "o_c_b", "o_c_scale")
SMALL = REPLICATED + SHARDED


class _Exchange:
    def __init__(self, shards, small, order, c_idx):
        self.q_idx = order[:1]
        self.shards = shards
        self.small = small
        self.order = order
        self.c_idx = c_idx
        self.reduced = {}

    def gather(self, keys):
        return _gather_comm([self.shards[k] for k in keys], [BIG_AXIS[k] for k in keys], "all")

    def gather1(self, keys):
        return _gather_comm([self.shards[k] for k in keys], [BIG_AXIS[k] for k in keys], "first")

    def gather2(self, keys, firsts):
        return _gather_comm(firsts, [BIG_AXIS[k] for k in keys], "second")

    def pair(self, grads):
        keys = list(grads)
        return _pair_comm([grads[k] for k in keys], [BIG_AXIS[k] for k in keys],
                          [grads[k].shape[BIG_AXIS[k]] // N_DEV for k in keys])

    def pair_sums(self, grads, received):
        return {k: _pair_sum(self.c_idx, grads[k], r, BIG_AXIS[k], grads[k].shape[BIG_AXIS[k]] // N_DEV,
                             BIG_SPLIT[k], name="pair_sum_" + k) for k, r in zip(grads, received)}

    def chips(self, sums):
        return _chip_comm([sums[k] for k in sums])

    def done(self, sums, received):
        self.reduced.update({k: (sums[k], r, self.q_idx) for k, r in zip(sums, received)})


def _local_step(x, tgt, w_small, ex):
    S, D = x.shape
    tnt, tx, tw = min(TM_NT, S), min(TM_MIX, S), min(TM_WIDE, S)

    wt = {}
    p, h0, wt["e_in"], got = _gather_matmul(ex.order, x, w_small["e_norm_pre"], ex.shards["e_in"], tm=tw,
                                            name="e_in_fwd", comm=_small_comm(ex.small))
    per_dev = [_unpack(got[0][d], [w_small[k].shape for k in SHARDED]) for d in range(N_DEV)]
    sm = {k: w_small[k] for k in REPLICATED}
    for j, k in enumerate(SHARDED):
        sm[k] = jnp.concatenate([per_dev[d][j] for d in range(N_DEV)], axis=-1)
    n_groups = sm["o_c_b"].shape[0]
    sm["o_c_b"] = sm["o_c_b"].reshape(1, -1)

    W = p.shape[1] // 7
    (u, cb), got = _e_mix_fwd(p, sm["e_a_conv"], sm["e_b_conv"], sm["e_b_conv_bias"], sm["e_b_ln_g"],
                              sm["e_b_ln_b"], tm=tx, name="e_mix_fwd", comm=ex.gather(["e_out"]))
    wt["e_out"] = got[0]
    late = ["o_out", "o_cw"]
    (x1, y0), part = _out_norm_res(u, wt["e_out"], x, sm["e_norm_post"], tm=tw, name="e_out_fwd",
                                   comm=ex.gather1(late))
    q, h1, wt["o_in"], got = _gather_matmul(ex.order, x1, sm["o_norm_pre"], ex.shards["o_in"], tm=tw,
                                            name="o_in_fwd", comm=ex.gather2(late, part))
    wt.update(zip(late, got))
    yy, pooled, gg = _o_mix_fwd(q, wt["o_cw"], sm["o_c_b"], sm["o_c_scale"], tm=tw, name="o_mix_fwd")
    dout, dx2, dyy, lcol, dg_o_post = _out_loss(yy, wt["o_out"], x1, sm["o_norm_post"], tgt, tm=tx, name="o_out_loss")
    loss = (0.5 / D) * jnp.sum(lcol)

    dq, d_cw, d_cb, d_cscale = _o_mix_bwd(dyy, q, gg, pooled, wt["o_cw"], sm["o_c_scale"], tm=tw, name="o_mix_bwd")
    g_o_out, _ = _mm_tn(yy, dout, ts=tnt, tn=W, name="o_out_dw")
    ga = dict(o_out=g_o_out, o_cw=d_cw.astype(BF16))
    dh1, ra = _mm_nt(dq, wt["o_in"], tm=tnt, tk=W, name="o_in_bwd", comm=ex.pair(ga))
    sa = ex.pair_sums(ga, ra)
    (dx1, dy0, dg_o_pre, dg_e_post), ra = _pre_bwd_o(dh1, x1, dx2, y0, sm["o_norm_pre"], sm["e_norm_post"],
                                                     tm=tx, name="o_pre_bwd", comm=ex.chips(sa))
    ex.done(sa, ra)
    g_o_in, _ = _mm_tn(h1, dq, ts=tnt, tn=W, name="o_in_dw")
    gb = dict(o_in=g_o_in)
    du, rb = _mm_nt(dy0, wt["e_out"], tm=tnt, tk=W, name="e_out_bwd", comm=ex.pair(gb))
    sb = ex.pair_sums(gb, rb)
    g_e_out, _ = _mm_tn(u, dy0, ts=tnt, tn=W, name="e_out_dw")
    gc = dict(e_out=g_e_out)
    (dp, d_wa, d_wb, d_bias, d_lg, d_lb), rbc = _e_mix_bwd(
        du, p, cb, sm["e_a_conv"], sm["e_b_conv"], sm["e_b_ln_g"], sm["e_b_ln_b"], tm=tx, name="e_mix_bwd",
        comm=_merge(ex.chips(sb), ex.pair(gc)))
    ex.done(sb, rbc[:1])
    sc = ex.pair_sums(gc, rbc[1:])
    order_out = jnp.concatenate([ex.order[1:], ex.order[:1]])
    sd, from_chip, rc = _dw_reduce(order_out, h0, dp, ex.shards["e_in"].shape[1], ts=tnt, name="e_in_dw",
                                   comm=ex.chips(sc))
    ex.done(sc, rc)
    dh0, rd = _mm_nt(dp, wt["e_in"], tm=tnt, tk=W, name="e_in_bwd", comm=_diag_comm(sd, from_chip))
    ex.reduced["e_in"] = (sd, rd[0], jnp.full((1,), 3, jnp.int32))
    grad_x, dg_e_pre = _pre_bwd_e(dh0, x, dx1, sm["e_norm_pre"], tm=tw, name="e_pre_bwd")

    small = dict(e_norm_pre=dg_e_pre, e_norm_post=dg_e_post, e_a_conv=d_wa, e_b_conv=d_wb, e_b_conv_bias=d_bias,
                 e_b_ln_g=d_lg, e_b_ln_b=d_lb, o_norm_pre=dg_o_pre, o_norm_post=dg_o_post,
                 o_c_b=d_cb.reshape(n_groups, -1), o_c_scale=d_cscale)
    return loss, grad_x, small


def kernel(x, e_norm_pre, e_norm_post, e_w_in, e_a_conv, e_b_conv, e_b_conv_bias, e_b_ln_g, e_b_ln_b, e_w_out, o_norm_pre, o_norm_post, o_w_in, o_c_w, o_c_b, o_c_scale, o_w_out, loss_target, m_e_norm_pre, m_e_norm_post, m_e_w_in, m_e_a_conv, m_e_b_conv, m_e_b_conv_bias, m_e_b_ln_g, m_e_b_ln_b, m_e_w_out, m_o_norm_pre, m_o_norm_post, m_o_w_in, m_o_c_w, m_o_c_b, m_o_c_scale, m_o_w_out, v_e_norm_pre, v_e_norm_post, v_e_w_in, v_e_a_conv, v_e_b_conv, v_e_b_conv_bias, v_e_b_ln_g, v_e_b_ln_b, v_e_w_out, v_o_norm_pre, v_o_norm_post, v_o_w_in, v_o_c_w, v_o_c_b, v_o_c_scale, v_o_w_out):
    xi, yi, ci = _place()
    w_big = dict(e_in=e_w_in[0], e_out=e_w_out[0], o_in=o_w_in[0], o_cw=o_c_w[0], o_out=o_w_out[0])
    m_big = dict(e_in=m_e_w_in[0], e_out=m_e_w_out[0], o_in=m_o_w_in[0], o_cw=m_o_c_w[0], o_out=m_o_w_out[0])
    v_big = dict(e_in=v_e_w_in[0], e_out=v_e_w_out[0], o_in=v_o_w_in[0], o_cw=v_o_c_w[0], o_out=v_o_w_out[0])
    w_small = dict(e_norm_pre=e_norm_pre, e_norm_post=e_norm_post, e_b_conv_bias=e_b_conv_bias, e_b_ln_g=e_b_ln_g,
                   e_b_ln_b=e_b_ln_b, e_a_conv=e_a_conv[0], e_b_conv=e_b_conv[0], o_norm_pre=o_norm_pre,
                   o_norm_post=o_norm_post, o_c_b=o_c_b[0], o_c_scale=o_c_scale)
    m_small = dict(e_norm_pre=m_e_norm_pre, e_norm_post=m_e_norm_post, e_b_conv_bias=m_e_b_conv_bias,
                   e_b_ln_g=m_e_b_ln_g, e_b_ln_b=m_e_b_ln_b, e_a_conv=m_e_a_conv[0], e_b_conv=m_e_b_conv[0],
                   o_norm_pre=m_o_norm_pre, o_norm_post=m_o_norm_post, o_c_b=m_o_c_b[0], o_c_scale=m_o_c_scale)
    v_small = dict(e_norm_pre=v_e_norm_pre, e_norm_post=v_e_norm_post, e_b_conv_bias=v_e_b_conv_bias,
                   e_b_ln_g=v_e_b_ln_g, e_b_ln_b=v_e_b_ln_b, e_a_conv=v_e_a_conv[0], e_b_conv=v_e_b_conv[0],
                   o_norm_pre=v_o_norm_pre, o_norm_post=v_o_norm_post, o_c_b=v_o_c_b[0], o_c_scale=v_o_c_scale)

    c_idx = jnp.reshape(ci, (1,)).astype(jnp.int32)
    order = jnp.stack([2 * xi + yi, 2 * (1 - xi) + yi, 2 * xi + (1 - yi), 2 * (1 - xi) + (1 - yi)]).astype(jnp.int32)
    ex = _Exchange({k: w_big[k].astype(BF16) for k in BIG}, _pack([w_small[k] for k in SHARDED]), order, c_idx)
    loss, grad_x, g_small = _local_step(x[0], loss_target[0], w_small, ex)

    big_out = {}
    for k in BIG:
        sums, received, q_idx = ex.reduced[k]
        big_out[k] = _adam_big(q_idx, sums, received, w_big[k], m_big[k], v_big[k], BIG_SPLIT[k], name="adam_" + k)[0]

    rep = _pack([g_small[k] for k in REPLICATED])
    loss_row = jnp.pad(jnp.reshape(loss, (1, 1)), ((0, 0), (0, LANES - 1)))
    blocks = []
    for k in SHARDED:
        r, n = w_small[k].shape
        blocks.append(g_small[k].reshape(r, N_DEV, n).transpose(1, 0, 2).reshape(N_DEV, r * n))
    blocks = jnp.concatenate(blocks, axis=1).reshape(N_DEV, -1, LANES)
    head = jnp.concatenate([rep, loss_row], axis=0)
    send = jnp.concatenate([jnp.broadcast_to(head[None], (N_DEV,) + head.shape), blocks], axis=1)
    parts = _run_comm(_small_scatter_comm(send), "small_grad_exchange")[0]

    def own_rows(d):
        return jnp.concatenate([_pack([d[k] for k in REPLICATED]), jnp.ones((1, LANES), F32),
                                _pack([d[k] for k in SHARDED])], axis=0)

    res_small = _adam_small(parts, own_rows(w_small), own_rows(m_small), own_rows(v_small), name="adam_small")
    n_rep = rep.shape[0]
    loss = res_small[0][n_rep, 0]
    small_out = {k: [] for k in SMALL}
    for packed in res_small:
        for k, t in zip(REPLICATED, _unpack(packed[:n_rep], [w_small[k].shape for k in REPLICATED])):
            small_out[k].append(t)
        for k, t in zip(SHARDED, _unpack(packed[n_rep + 1:], [w_small[k].shape for k in SHARDED])):
            small_out[k].append(t)

    big_of = dict(e_w_in="e_in", e_w_out="e_out", o_w_in="o_in", o_c_w="o_cw", o_w_out="o_out")
    stacked = ("e_a_conv", "e_b_conv", "o_c_b")

    def leaf(name, which):
        if name in big_of:
            return big_out[big_of[name]][which][None]
        t = small_out[name][which]
        return t[None] if name in stacked else t

    order = ("e_norm_pre", "e_norm_post", "e_w_in", "e_a_conv", "e_b_conv", "e_b_conv_bias", "e_b_ln_g", "e_b_ln_b",
             "e_w_out", "o_norm_pre", "o_norm_post", "o_w_in", "o_c_w", "o_c_b", "o_c_scale", "o_w_out")
    outs = [loss, grad_x[None]]
    for which in range(4):
        outs += [leaf(nm, which) for nm in order]
    return tuple(outs)
```

```python
import jax
import jax.numpy as jnp
from jax import lax
from jax.experimental import pallas as pl
from jax.experimental.pallas import tpu as pltpu

F32 = jnp.float32
BF16 = jnp.bfloat16
EPS = 1e-6
MESH = pl.DeviceIdType.MESH
ANY = pl.BlockSpec(memory_space=pl.ANY)

N_DEV = 8
HALO = 32
PHALO = 16
CONV_A = 3
CONV_B = 31
POOL_WINDOWS = (2, 4, 8, 16)
LANES = 128
MIB = 1024 * 1024

ADAM_LR = 0.001
ADAM_B1 = 0.9
ADAM_B2 = 0.999
ADAM_EPS = 1e-08
ADAM_WD = 0.01
ADAM_STEP = 10

TM_NT = 1024
TM_MIX = 256
TM_WIDE = 512


def _sds(shape, dtype):
    return jax.ShapeDtypeStruct(tuple(shape), dtype)


def _params(sem, vmem_mib):
    return pltpu.CompilerParams(dimension_semantics=sem, vmem_limit_bytes=vmem_mib * MIB)


def _const(shape, single=False):
    n = len(shape)
    if single:
        return pl.BlockSpec(shape, lambda *_: (0,) * n, pipeline_mode=pl.Buffered(1))
    return pl.BlockSpec(shape, lambda *_: (0,) * n)


def _sig(v):
    return jax.nn.sigmoid(v)


def _dsilu(v, s):
    return s * (1.0 + v * (1.0 - s))


def _rms(v):
    return lax.rsqrt(jnp.mean(v * v, axis=-1, keepdims=True) + EPS)


def _norm_bwd(dn, n, r):
    return r * (dn - n * jnp.mean(dn * n, axis=-1, keepdims=True))


def _colsum(v):
    return jnp.sum(v, axis=0, keepdims=True)


class _Comm:
    def __init__(self, inputs, out_shapes, sems, start, finish, aliases=None, middle=None):
        self.inputs, self.out_shapes, self.sems = list(inputs), list(out_shapes), list(sems)
        self.start, self.finish, self.middle = start, finish, middle
        self.aliases = dict(aliases or {})


def _merge(*comms):
    comms = [c for c in comms if c is not None]
    if len(comms) <= 1:
        return comms[0] if comms else None
    spans, i0, o0, s0, aliases = [], 0, 0, 0, {}
    for c in comms:
        spans.append((i0, o0, s0))
        aliases.update({i0 + k: o0 + v for k, v in c.aliases.items()})
        i0, o0, s0 = i0 + len(c.inputs), o0 + len(c.out_shapes), s0 + len(c.sems)

    def run(which):
        def fn(ins, outs, sems):
            for c, (i, o, s) in zip(comms, spans):
                hook = getattr(c, which)
                if hook is not None:
                    hook(ins[i:i + len(c.inputs)], outs[o:o + len(c.out_shapes)], sems[s:s + len(c.sems)])
        return fn

    return _Comm([a for c in comms for a in c.inputs], [a for c in comms for a in c.out_shapes],
                 [a for c in comms for a in c.sems], run("start"), run("finish"), aliases,
                 run("middle") if any(c.middle is not None for c in comms) else None)


def _call(body, *, grid, in_specs, out_specs, out_shape, operands, name, params, scratch_shapes=(), comm=None,
          prefetch=None, own_copies_first=False):
    n_p = 0 if prefetch is None else 1
    n_i, n_o, n_s = len(in_specs), len(out_specs), len(scratch_shapes)
    if comm is None:
        comm = _Comm([], [], [], None, None)
    c_i, c_o = len(comm.inputs), len(comm.out_shapes)

    def carrier(*refs):
        pre, refs = refs[:n_p], refs[n_p:]
        ins, cins = refs[:n_i], refs[n_i:n_i + c_i]
        outs = refs[n_i + c_i:n_i + c_i + n_o]
        couts = refs[n_i + c_i + n_o:n_i + c_i + n_o + c_o]
        scr = refs[n_i + c_i + n_o + c_o:n_i + c_i + n_o + c_o + n_s]
        csems = refs[n_i + c_i + n_o + c_o + n_s:]
        ids = [pl.program_id(d) for d in range(len(grid))]
        first = ids[0] == 0
        half = ids[0] == grid[0] // 2
        last = ids[0] == grid[0] - 1
        for d in range(1, len(grid)):
            first = first & (ids[d] == 0)
            half = half & (ids[d] == 0)
            last = last & (ids[d] == grid[d] - 1)

        def start():
            if comm.start is not None:
                @pl.when(first)
                def _():
                    comm.start(cins, couts, csems)

        if not own_copies_first:
            start()
        if comm.middle is not None:
            assert grid[0] >= 2

            @pl.when(half)
            def _():
                comm.middle(cins, couts, csems)

        body(*pre, *ins, *outs, *scr)
        if own_copies_first:
            start()

        if comm.finish is not None:
            @pl.when(last)
            def _():
                comm.finish(cins, couts, csems)

    specs = dict(grid=grid, in_specs=list(in_specs) + [ANY] * c_i, out_specs=list(out_specs) + [ANY] * c_o,
                 scratch_shapes=list(scratch_shapes) + comm.sems)
    if n_p:
        specs = dict(grid_spec=pltpu.PrefetchScalarGridSpec(num_scalar_prefetch=1, **specs))
    res = pl.pallas_call(
        carrier, out_shape=list(out_shape) + comm.out_shapes,
        input_output_aliases={n_p + n_i + k: n_o + v for k, v in comm.aliases.items()},
        name=name, compiler_params=params, **specs)(*(() if prefetch is None else (prefetch,)), *operands, *comm.inputs)
    return list(res[:n_o]), list(res[n_o:])


def _run_comm(comm, name):
    c_i, c_o = len(comm.inputs), len(comm.out_shapes)

    def body(*refs):
        ins, outs, sems = refs[:c_i], refs[c_i:c_i + c_o], refs[c_i + c_o:]
        comm.start(ins, outs, sems)
        comm.finish(ins, outs, sems)

    res = pl.pallas_call(
        body, in_specs=[ANY] * c_i, out_specs=[ANY] * c_o, out_shape=comm.out_shapes, scratch_shapes=comm.sems,
        input_output_aliases=comm.aliases, name=name)(*comm.inputs)
    return list(res)


def _gather_matmul(order, x, g, shard, *, tm, name, comm=None):
    S, K = x.shape
    nb = shard.shape[1]
    n_i = S // tm

    def body(order_ref, x_ref, g_ref, shard_ref, p_ref, h_ref, full_ref, hbuf, wbuf, stage, send_sems, recv_sems,
             dma_sems):
        j, i = pl.program_id(0), pl.program_id(1)
        px, py, pc = _place()
        cps = _gather_copies(stage, full_ref, 1, nb, send_sems, recv_sems, 0)
        own = pltpu.make_async_copy(stage, _piece(full_ref, 1, nb, 4 * px + 2 * py + pc), dma_sems.at[0])
        keep_h = pltpu.make_async_copy(hbuf, h_ref, dma_sems.at[2])

        def load(src, dst):
            cp = pltpu.make_async_copy(src, dst, dma_sems.at[1])
            cp.start()
            cp.wait()

        def load_pair(qx, qy):
            load(_piece(full_ref, 1, 2 * nb, 2 * qx + qy), wbuf)

        @pl.when((j == 0) & (i == 0))
        def _():
            load(shard_ref, stage)
            own.start()
            for k in (0, 1, 2):
                cps[k].start()

        @pl.when(j == 0)
        def _():
            xx = x_ref[...]
            hbuf[i] = ((xx * _rms(xx)) * g_ref[...]).astype(BF16)

        @pl.when((j == 0) & (i == 0))
        def _():
            own.wait()
            cps[0].wait_recv()
            load_pair(px, py)

        @pl.when((j == 1) & (i == 0))
        def _():
            keep_h.start()
            cps[1].wait_recv()
            cps[3].start()
            cps[5].start()
            cps[2].wait_recv()
            cps[4].start()
            cps[6].start()
            cps[5].wait_recv()
            load_pair(1 - px, py)

        @pl.when((j == 2) & (i == 0))
        def _():
            cps[6].wait_recv()
            load_pair(px, 1 - py)

        @pl.when((j == 3) & (i == 0))
        def _():
            cps[3].wait_recv()
            cps[4].wait_recv()
            cps[7].start()
            cps[7].wait_recv()
            load_pair(1 - px, 1 - py)

        p_ref[...] = jnp.dot(hbuf[i], wbuf[...], preferred_element_type=F32).astype(BF16)

        @pl.when((j == 3) & (i == n_i - 1))
        def _():
            for cp in cps:
                cp.wait_send()
            keep_h.wait()

    first_pass = lambda j, i, o: (jnp.where(j == 0, i, n_i - 1), 0)
    outs, extra = _call(
        body, grid=(4, n_i), prefetch=order,
        in_specs=[pl.BlockSpec((tm, K), first_pass), pl.BlockSpec((1, K), lambda j, i, o: (0, 0)), ANY],
        out_specs=[pl.BlockSpec((tm, 2 * nb), lambda j, i, o: (i, o[j])), ANY, ANY],
        out_shape=[_sds((S, N_DEV * nb), BF16), _sds((n_i, tm, K), BF16), _sds((K, N_DEV * nb), BF16)],
        operands=(x, g, shard),
        scratch_shapes=[pltpu.VMEM((n_i, tm, K), BF16), pltpu.VMEM((K, 2 * nb), BF16), pltpu.VMEM((K, nb), BF16),
                        pltpu.SemaphoreType.DMA((N_GATHER,)), pltpu.SemaphoreType.DMA((N_GATHER,)),
                        pltpu.SemaphoreType.DMA((3,))],
        name=name, params=_params(("arbitrary", "arbitrary"), 58), comm=comm, own_copies_first=True)
    return outs[0], outs[1].reshape(S, K), outs[2], extra


def _out_norm_res(u, w, x, g, *, tm, name, comm=None):
    S, K = u.shape
    D = w.shape[1]

    def body(u_ref, w_ref, x_ref, g_ref, x1_ref, y_ref):
        y = jnp.dot(u_ref[...], w_ref[...], preferred_element_type=F32)
        y_ref[...] = y.astype(BF16)
        x1_ref[...] = x_ref[...] + (y * _rms(y)) * g_ref[...]

    return _call(
        body, grid=(S // tm,),
        in_specs=[pl.BlockSpec((tm, K), lambda i: (i, 0)), _const((K, D), single=True),
                  pl.BlockSpec((tm, D), lambda i: (i, 0)), _const((1, D))],
        out_specs=[pl.BlockSpec((tm, D), lambda i: (i, 0)), pl.BlockSpec((tm, D), lambda i: (i, 0))],
        out_shape=[_sds((S, D), F32), _sds((S, D), BF16)], operands=(u, w, x, g),
        name=name, params=_params(("arbitrary",), 56), comm=comm)


def _out_loss(yy, w, x1, g, tgt, *, tm, name):
    S, K = yy.shape
    D = w.shape[1]

    def body(yy_ref, w_ref, x1_ref, g_ref, t_ref, dout_ref, dx2_ref, dyy_ref, lcol_ref, dg_ref):
        out = jnp.dot(yy_ref[...], w_ref[...], preferred_element_type=F32)
        r = _rms(out)
        n = out * r
        gg = g_ref[...]
        e = x1_ref[...] + n * gg - t_ref[...]
        dx2 = e * (1.0 / D)
        dx2_ref[...] = dx2
        dout = _norm_bwd(dx2 * gg, n, r).astype(BF16)
        dout_ref[...] = dout
        dyy_ref[...] = lax.dot_general(dout, w_ref[...], (((1,), (1,)), ((), ())),
                                       preferred_element_type=F32).astype(BF16)

        @pl.when(pl.program_id(0) == 0)
        def _():
            lcol_ref[...] = jnp.zeros_like(lcol_ref)
            dg_ref[...] = jnp.zeros_like(dg_ref)

        lcol_ref[...] += _colsum(e * e)
        dg_ref[...] += _colsum(dx2 * n)

    return _call(
        body, grid=(S // tm,),
        in_specs=[pl.BlockSpec((tm, K), lambda i: (i, 0)), _const((K, D), single=True),
                  pl.BlockSpec((tm, D), lambda i: (i, 0)), _const((1, D)),
                  pl.BlockSpec((tm, D), lambda i: (i, 0))],
        out_specs=[pl.BlockSpec((tm, D), lambda i: (i, 0)), pl.BlockSpec((tm, D), lambda i: (i, 0)),
                   pl.BlockSpec((tm, K), lambda i: (i, 0)), _const((1, D)), _const((1, D))],
        out_shape=[_sds((S, D), BF16), _sds((S, D), F32), _sds((S, K), BF16), _sds((1, D), F32), _sds((1, D), F32)],
        operands=(yy, w, x1, g, tgt), name=name, params=_params(("arbitrary",), 52))[0]


def _mm_nt(a, w, *, tm, tk, name, comm=None):
    S, N = a.shape
    D = w.shape[0]
    n_k = N // tk

    def body(a_ref, w_ref, o_ref, acc_ref):
        k = pl.program_id(1)

        @pl.when(k == 0)
        def _():
            acc_ref[...] = jnp.zeros_like(acc_ref)

        acc_ref[...] = lax.dot_general(a_ref[...], w_ref[...], (((1,), (1,)), ((), ())),
                                       preferred_element_type=F32) + acc_ref[...]

        @pl.when(k == n_k - 1)
        def _():
            o_ref[...] = acc_ref[...].astype(BF16)

    outs, extra = _call(
        body, grid=(S // tm, n_k),
        in_specs=[pl.BlockSpec((tm, tk), lambda i, k: (i, k)), pl.BlockSpec((D, tk), lambda i, k: (0, k))],
        out_specs=[pl.BlockSpec((tm, D), lambda i, k: (i, 0))],
        out_shape=[_sds((S, D), BF16)], operands=(a, w),
        scratch_shapes=[pltpu.VMEM((tm, D), F32)],
        name=name, params=_params(("arbitrary", "arbitrary"), 48), comm=comm)
    return outs[0], extra


def _mm_tn(a, b, *, ts, tn, name, comm=None):
    S, M = a.shape
    N = b.shape[1]
    n_s = S // ts

    def body(a_ref, b_ref, o_ref, acc_ref):
        s = pl.program_id(1)

        @pl.when(s == 0)
        def _():
            acc_ref[...] = jnp.zeros_like(acc_ref)

        acc_ref[...] = lax.dot_general(a_ref[...], b_ref[...], (((0,), (0,)), ((), ())),
                                       preferred_element_type=F32) + acc_ref[...]

        @pl.when(s == n_s - 1)
        def _():
            o_ref[...] = acc_ref[...].astype(BF16)

    outs, extra = _call(
        body, grid=(N // tn, n_s),
        in_specs=[pl.BlockSpec((ts, M), lambda j, s: (s, 0)), pl.BlockSpec((ts, tn), lambda j, s: (s, j))],
        out_specs=[pl.BlockSpec((M, tn), lambda j, s: (0, j))],
        out_shape=[_sds((M, N), BF16)], operands=(a, b),
        scratch_shapes=[pltpu.VMEM((M, tn), F32)],
        name=name, params=_params(("arbitrary", "arbitrary"), 48), comm=comm)
    return outs[0], extra


DW_SLOT = (0, 1, 0, 1, 2, 2, 3, 3)
DW_HALF = (0, 0, 1, 1, 0, 1, 0, 1)


def _dw_reduce(order, a, b, nb, *, ts, name, comm=None):
    S, M = a.shape
    n_s = S // ts
    mh = M // 2
    n_t = len(DW_SLOT)
    rows = 512

    def body(order_ref, a_ref, b_ref, sums_ref, from_sib_ref, from_chip_ref, acc, send_buf, mine_buf, recv_buf,
             sib_send, sib_recv, chip_send, chip_recv, dma_sems):
        t, s = pl.program_id(0), pl.program_id(1)
        x, y, c = _place()
        targets = [(1 - x, y, c), (x, 1 - y, c)]

        def to_sibling(k):
            return pltpu.make_async_remote_copy(
                src_ref=send_buf, dst_ref=from_sib_ref.at[k], send_sem=sib_send.at[k], recv_sem=sib_recv.at[k],
                device_id=(x, y, 1 - c), device_id_type=MESH)

        def to_chip(k):
            at = (DW_SLOT[k], DW_HALF[k])
            return pltpu.make_async_remote_copy(
                src_ref=sums_ref.at[at], dst_ref=from_chip_ref.at[at], send_sem=chip_send.at[k],
                recv_sem=chip_recv.at[k], device_id=targets[DW_SLOT[k]], device_id_type=MESH)

        def finish(k):
            to_sibling(k).wait()
            get = pltpu.make_async_copy(from_sib_ref.at[k], recv_buf, dma_sems.at[0])
            get.start()
            get.wait()
            for r in range(0, mh, rows):
                recv_buf[r:r + rows, :] = (mine_buf[r:r + rows, :].astype(F32)
                                           + recv_buf[r:r + rows, :].astype(F32)).astype(BF16)
            put = pltpu.make_async_copy(recv_buf, sums_ref.at[DW_SLOT[k], DW_HALF[k]], dma_sems.at[1])
            put.start()
            put.wait()
            if k < 4:
                to_chip(k).start()

        for k in range(n_t - 1):
            @pl.when((t == k + 1) & (s == 0))
            def _(k=k):
                finish(k)

        @pl.when(s == 0)
        def _():
            acc[...] = jnp.zeros_like(acc)

        acc[...] = lax.dot_general(a_ref[...], b_ref[...], (((0,), (0,)), ((), ())),
                                   preferred_element_type=F32) + acc[...]

        @pl.when(s == n_s - 1)
        def _():
            for r in range(0, mh, rows):
                lo, hi = acc[r:r + rows, :nb], acc[r:r + rows, nb:]
                send_buf[r:r + rows, :] = jnp.where(c == 0, hi, lo).astype(BF16)
                mine_buf[r:r + rows, :] = jnp.where(c == 0, lo, hi).astype(BF16)
            to_sibling(t).start()

        @pl.when((t == n_t - 1) & (s == n_s - 1))
        def _():
            finish(n_t - 1)
            for k in range(4):
                to_chip(k).wait()

    slot = lambda t: jnp.where(t < 4, t & 1, 2 + ((t - 4) >> 1))
    half = lambda t: jnp.where(t < 4, (t >> 1) & 1, t & 1)
    outs, extra = _call(
        body, grid=(n_t, n_s), prefetch=order,
        in_specs=[pl.BlockSpec((ts, mh), lambda t, s, o: (s, half(t))),
                  pl.BlockSpec((ts, 2 * nb), lambda t, s, o: (s, o[slot(t)]))],
        out_specs=[ANY, ANY, ANY],
        out_shape=[_sds((4, 2, mh, nb), BF16), _sds((n_t, mh, nb), BF16), _sds((3, 2, mh, nb), BF16)], operands=(a, b),
        scratch_shapes=[pltpu.VMEM((mh, 2 * nb), F32), pltpu.VMEM((mh, nb), BF16), pltpu.VMEM((mh, nb), BF16),
                        pltpu.VMEM((mh, nb), BF16), pltpu.SemaphoreType.DMA((n_t,)), pltpu.SemaphoreType.DMA((n_t,)),
                        pltpu.SemaphoreType.DMA((4,)), pltpu.SemaphoreType.DMA((4,)), pltpu.SemaphoreType.DMA((2,))],
        name=name, params=_params(("arbitrary", "arbitrary"), 48), comm=comm)
    return outs[0].reshape(4, M, nb), outs[2].reshape(3, M, nb), extra


def _diag_comm(sums, from_chip):
    def copy(ins, outs, sems):
        x, y, c = _place()
        return pltpu.make_async_remote_copy(
            src_ref=ins[0].at[2], dst_ref=outs[0].at[2], send_sem=sems[0].at[0], recv_sem=sems[1].at[0],
            device_id=(1 - x, 1 - y, c), device_id_type=MESH)

    def start(ins, outs, sems):
        copy(ins, outs, sems).start()

    def finish(ins, outs, sems):
        copy(ins, outs, sems).wait()

    sems = [pltpu.SemaphoreType.DMA((1,)), pltpu.SemaphoreType.DMA((1,))]
    return _Comm([sums, from_chip], [_sds(from_chip.shape, from_chip.dtype)], sems, start, finish, aliases={1: 0})


def _pre_bwd_o(dh, x1, dx2, y0, g_pre, g_post, *, tm, name, comm=None):
    S, D = x1.shape

    def body(dh_ref, x1_ref, dx2_ref, y0_ref, gpre_ref, gpost_ref, dx1_ref, dy0_ref, dgpre_ref, dgpost_ref):
        @pl.when(pl.program_id(0) == 0)
        def _():
            dgpre_ref[...] = jnp.zeros_like(dgpre_ref)
            dgpost_ref[...] = jnp.zeros_like(dgpost_ref)

        dh = dh_ref[...].astype(F32)
        x1 = x1_ref[...]
        r2 = _rms(x1)
        xn = x1 * r2
        dgpre_ref[...] += _colsum(dh * xn)
        dx1 = dx2_ref[...] + _norm_bwd(dh * gpre_ref[...], xn, r2)
        dx1_ref[...] = dx1
        y = y0_ref[...].astype(F32)
        r1 = _rms(y)
        n1 = y * r1
        dgpost_ref[...] += _colsum(dx1 * n1)
        dy0_ref[...] = _norm_bwd(dx1 * gpost_ref[...], n1, r1).astype(BF16)

    row = pl.BlockSpec((tm, D), lambda i: (i, 0))
    return _call(
        body, grid=(S // tm,),
        in_specs=[row, row, row, row, _const((1, D)), _const((1, D))],
        out_specs=[row, row, _const((1, D)), _const((1, D))],
        out_shape=[_sds((S, D), F32), _sds((S, D), BF16), _sds((1, D), F32), _sds((1, D), F32)],
        operands=(dh, x1, dx2, y0, g_pre, g_post),
        name=name, params=_params(("arbitrary",), 48), comm=comm)


def _pre_bwd_e(dh, x, dx1, g_pre, *, tm, name):
    S, D = x.shape

    def body(dh_ref, x_ref, dx1_ref, gpre_ref, gx_ref, dgpre_ref):
        @pl.when(pl.program_id(0) == 0)
        def _():
            dgpre_ref[...] = jnp.zeros_like(dgpre_ref)

        dh = dh_ref[...].astype(F32)
        xx = x_ref[...]
        r0 = _rms(xx)
        xn = xx * r0
        dgpre_ref[...] += _colsum(dh * xn)
        gx_ref[...] = dx1_ref[...] + _norm_bwd(dh * gpre_ref[...], xn, r0)

    row = pl.BlockSpec((tm, D), lambda i: (i, 0))
    return _call(
        body, grid=(S // tm,),
        in_specs=[row, row, row, _const((1, D))],
        out_specs=[row, _const((1, D))],
        out_shape=[_sds((S, D), F32), _sds((1, D), F32)],
        operands=(dh, x, dx1, g_pre), name=name, params=_params(("arbitrary",), 56))[0]


SUBLANES = 8


def _shift_copies(sh_ref, ext_ref, cs):
    for b in range(1, SUBLANES):
        sh_ref[b - 1] = ext_ref[pl.ds(b, sh_ref.shape[1]), cs]


def _rows_at(ext_ref, sh_ref, off, cs, tm):
    b = off % SUBLANES
    if b == 0 or sh_ref is None:
        return ext_ref[pl.ds(off, tm), cs]
    return sh_ref[b - 1, pl.ds(off - b, tm), :]


def _taps(ext_ref, w_ref, n_taps, base, cs, tm, sh_ref=None):
    acc = _rows_at(ext_ref, sh_ref, base, cs, tm) * w_ref[0:1, cs]
    for k in range(1, n_taps):
        acc = acc + _rows_at(ext_ref, sh_ref, base + k, cs, tm) * w_ref[k:k + 1, cs]
    return acc


def _taps_rev(ext_ref, w_ref, n_taps, cs, tm, sh_ref=None):
    acc = _rows_at(ext_ref, sh_ref, n_taps - 1, cs, tm) * w_ref[0:1, cs]
    for k in range(1, n_taps):
        acc = acc + _rows_at(ext_ref, sh_ref, n_taps - 1 - k, cs, tm) * w_ref[k:k + 1, cs]
    return acc


def _e_mix_fwd(p, wa, wb, bias, ln_g, ln_b, *, tm, name, comm=None):
    S = p.shape[0]
    W = p.shape[1] // 7
    nb = tm // HALO
    chunks = [slice(c * LANES, (c + 1) * LANES) for c in range(W // LANES)]

    def body(p_ref, hax_ref, hac_ref, hbv_ref, hbg_ref, wa_ref, wb_ref, bias_ref, lg_ref, lb_ref,
             u_ref, cb_ref, ext_ref, sh_ref):
        keep = (pl.program_id(0) > 0).astype(F32)
        col = lambda j, cs: p_ref[:, j * W + cs.start:j * W + cs.stop].astype(F32)

        ext_ref[0:HALO, :] = hax_ref[...].astype(F32) * hac_ref[...].astype(F32) * keep
        ext_ref[HALO:, :] = p_ref[:, 2 * W:3 * W].astype(F32) * p_ref[:, 0:W].astype(F32)
        for cs in chunks:
            conv = _taps(ext_ref, wa_ref, CONV_A, HALO - (CONV_A - 1), cs, tm)
            az = col(3, cs)
            u_ref[:, cs] = (col(1, cs) * conv * (az * _sig(az))).astype(BF16)

        ext_ref[0:HALO, :] = hbv_ref[...].astype(F32) * _sig(hbg_ref[...].astype(F32)) * keep
        ext_ref[HALO:, :] = p_ref[:, 4 * W:5 * W].astype(F32) * _sig(p_ref[:, 5 * W:6 * W].astype(F32))
        s1 = jnp.zeros((tm, LANES), F32)
        for cs in chunks:
            _shift_copies(sh_ref, ext_ref, cs)
            cb = _taps(ext_ref, wb_ref, CONV_B, HALO - (CONV_B - 1), cs, tm, sh_ref) + bias_ref[:, cs]
            cb_ref[:, cs] = cb
            s1 = s1 + cb
        mu = jnp.sum(s1, axis=-1, keepdims=True) * (1.0 / W)
        s2 = jnp.zeros((tm, LANES), F32)
        for cs in chunks:
            xc = cb_ref[:, cs] - mu
            s2 = s2 + xc * xc
        rs = lax.rsqrt(jnp.sum(s2, axis=-1, keepdims=True) * (1.0 / W) + EPS)
        for cs in chunks:
            lb = (cb_ref[:, cs] - mu) * rs * lg_ref[:, cs] + lb_ref[:, cs]
            bz = col(6, cs)
            u_ref[:, W + cs.start:W + cs.stop] = (lb * _sig(lb) * (bz * _sig(bz))).astype(BF16)

    prev = lambda j: pl.BlockSpec((HALO, W), lambda i: (jnp.maximum(i * nb - 1, 0), j))
    return _call(
        body, grid=(S // tm,),
        in_specs=[pl.BlockSpec((tm, 7 * W), lambda i: (i, 0)), prev(0), prev(2), prev(4), prev(5),
                  _const((CONV_A, W)), _const((CONV_B, W)), _const((1, W)), _const((1, W)), _const((1, W))],
        out_specs=[pl.BlockSpec((tm, 2 * W), lambda i: (i, 0)), pl.BlockSpec((tm, W), lambda i: (i, 0))],
        out_shape=[_sds((S, 2 * W), BF16), _sds((S, W), F32)],
        operands=(p, p, p, p, p, wa, wb, bias, ln_g, ln_b),
        scratch_shapes=[pltpu.VMEM((HALO + tm, W), F32),
                        pltpu.VMEM((SUBLANES - 1, HALO + tm - SUBLANES, LANES), F32)],
        name=name, params=_params(("arbitrary",), 48), comm=comm)


def _e_mix_bwd(du, p, cb, wa, wb, ln_g, ln_b, *, tm, name, comm=None):
    S = p.shape[0]
    W = p.shape[1] // 7
    nb = tm // HALO
    n_t = S // tm
    last_blk = S // HALO - 1
    chunks = [slice(c * LANES, (c + 1) * LANES) for c in range(W // LANES)]

    def body(du_ref, duf_ref, p_ref, fab_ref, faz_ref, fbz_ref, hax_ref, hac_ref, hbv_ref, hbg_ref,
             cb_ref, cbf_ref, wa_ref, wb_ref, lg_ref, lb_ref,
             dp_ref, dwa_ref, dwb_ref, dbias_ref, dlg_ref, dlb_ref, extd_ref, extg_ref, shd_ref, shg_ref):
        i = pl.program_id(0)
        keep_prev = (i > 0).astype(F32)
        keep_next = (i < n_t - 1).astype(F32)
        col = lambda j, cs: p_ref[:, j * W + cs.start:j * W + cs.stop].astype(F32)

        @pl.when(i == 0)
        def _():
            dwa_ref[...] = jnp.zeros_like(dwa_ref)
            dwb_ref[...] = jnp.zeros_like(dwb_ref)
            dbias_ref[...] = jnp.zeros_like(dbias_ref)
            dlg_ref[...] = jnp.zeros_like(dlg_ref)
            dlb_ref[...] = jnp.zeros_like(dlb_ref)

        def dcb_rows(rows, cb_rows_ref, dub, bz_of, dst0, scale, main):
            cbv = cb_rows_ref[...]
            mu = jnp.mean(cbv, axis=-1, keepdims=True)
            xc = cbv - mu
            rs = lax.rsqrt(jnp.mean(xc * xc, axis=-1, keepdims=True) + EPS)
            m1 = jnp.zeros((rows, LANES), F32)
            m2 = jnp.zeros((rows, LANES), F32)
            for cs in chunks:
                nbv = (cb_rows_ref[:, cs] - mu) * rs
                lb = nbv * lg_ref[:, cs] + lb_ref[:, cs]
                sl = _sig(lb)
                bz = bz_of(cs)
                sz = _sig(bz)
                dub_c = dub(cs)
                dlb = dub_c * (bz * sz) * _dsilu(lb, sl)
                if main:
                    dlg_ref[:, cs] += _colsum(dlb * nbv)
                    dlb_ref[:, cs] += _colsum(dlb)
                    dp_ref[:, 6 * W + cs.start:6 * W + cs.stop] = (dub_c * (lb * sl) * _dsilu(bz, sz)).astype(BF16)
                dnb = dlb * lg_ref[:, cs]
                extd_ref[dst0:dst0 + rows, cs] = dnb
                m1 = m1 + dnb
                m2 = m2 + dnb * nbv
            m1 = jnp.sum(m1, axis=-1, keepdims=True) * (1.0 / W)
            m2 = jnp.sum(m2, axis=-1, keepdims=True) * (1.0 / W)
            for cs in chunks:
                nbv = (cb_rows_ref[:, cs] - mu) * rs
                dcb = rs * (extd_ref[dst0:dst0 + rows, cs] - m1 - nbv * m2) * scale
                extd_ref[dst0:dst0 + rows, cs] = dcb
                if main:
                    dbias_ref[:, cs] += _colsum(dcb)

        dcb_rows(tm, cb_ref, lambda cs: du_ref[:, W + cs.start:W + cs.stop].astype(F32),
                 lambda cs: col(6, cs), 0, 1.0, True)
        dcb_rows(HALO, cbf_ref, lambda cs: duf_ref[:, W + cs.start:W + cs.stop].astype(F32),
                 lambda cs: fbz_ref[:, cs].astype(F32), tm, keep_next, False)

        extg_ref[0:HALO, :] = hbv_ref[...].astype(F32) * _sig(hbg_ref[...].astype(F32)) * keep_prev
        extg_ref[HALO:, :] = p_ref[:, 4 * W:5 * W].astype(F32) * _sig(p_ref[:, 5 * W:6 * W].astype(F32))
        base_b = HALO - (CONV_B - 1)
        for cs in chunks:
            _shift_copies(shd_ref, extd_ref, cs)
            _shift_copies(shg_ref, extg_ref, cs)
            dgb = _taps_rev(extd_ref, wb_ref, CONV_B, cs, tm, shd_ref)
            bv = col(4, cs)
            sg = _sig(col(5, cs))
            dp_ref[:, 4 * W + cs.start:4 * W + cs.stop] = (dgb * sg).astype(BF16)
            dp_ref[:, 5 * W + cs.start:5 * W + cs.stop] = (dgb * bv * sg * (1.0 - sg)).astype(BF16)
            dcb = extd_ref[0:tm, cs]
            for k in range(CONV_B):
                dwb_ref[k:k + 1, cs] += _colsum(dcb * _rows_at(extg_ref, shg_ref, base_b + k, cs, tm))

        extg_ref[0:HALO, :] = hax_ref[...].astype(F32) * hac_ref[...].astype(F32) * keep_prev
        extg_ref[HALO:, :] = p_ref[:, 2 * W:3 * W].astype(F32) * p_ref[:, 0:W].astype(F32)
        base_a = HALO - (CONV_A - 1)
        for cs in chunks:
            conv = _taps(extg_ref, wa_ref, CONV_A, base_a, cs, tm)
            az = col(3, cs)
            sz = _sig(az)
            ab = col(1, cs)
            dua = du_ref[:, cs].astype(F32)
            dya = dua * (az * sz)
            dp_ref[:, W + cs.start:W + cs.stop] = (dya * conv).astype(BF16)
            dp_ref[:, 3 * W + cs.start:3 * W + cs.stop] = (dua * (ab * conv) * _dsilu(az, sz)).astype(BF16)
            extd_ref[0:tm, cs] = dya * ab
            azf = faz_ref[:, cs].astype(F32)
            extd_ref[tm:tm + HALO, cs] = (duf_ref[:, cs].astype(F32) * (azf * _sig(azf))
                                          * fab_ref[:, cs].astype(F32) * keep_next)
        for cs in chunks:
            dca = _taps_rev(extd_ref, wa_ref, CONV_A, cs, tm)
            dp_ref[:, cs] = (dca * col(2, cs)).astype(BF16)
            dp_ref[:, 2 * W + cs.start:2 * W + cs.stop] = (dca * col(0, cs)).astype(BF16)
            dconv = extd_ref[0:tm, cs]
            for k in range(CONV_A):
                dwa_ref[k:k + 1, cs] += _colsum(dconv * extg_ref[pl.ds(base_a + k, tm), cs])

    prev = lambda j: pl.BlockSpec((HALO, W), lambda i: (jnp.maximum(i * nb - 1, 0), j))
    nxt = lambda j, w: pl.BlockSpec((HALO, w), lambda i: (jnp.minimum((i + 1) * nb, last_blk), j))
    row = lambda w: pl.BlockSpec((tm, w), lambda i: (i, 0))
    return _call(
        body, grid=(n_t,),
        in_specs=[row(2 * W), nxt(0, 2 * W), row(7 * W), nxt(1, W), nxt(3, W), nxt(6, W),
                  prev(0), prev(2), prev(4), prev(5), row(W), nxt(0, W),
                  _const((CONV_A, W)), _const((CONV_B, W)), _const((1, W)), _const((1, W))],
        out_specs=[row(7 * W), _const((CONV_A, W)), _const((CONV_B, W)), _const((1, W)), _const((1, W)), _const((1, W))],
        out_shape=[_sds((S, 7 * W), BF16), _sds((CONV_A, W), F32), _sds((CONV_B, W), F32),
                   _sds((1, W), F32), _sds((1, W), F32), _sds((1, W), F32)],
        operands=(du, du, p, p, p, p, p, p, p, p, cb, cb, wa, wb, ln_g, ln_b),
        scratch_shapes=[pltpu.VMEM((tm + HALO, W), F32), pltpu.VMEM((HALO + tm, W), F32),
                        pltpu.VMEM((SUBLANES - 1, HALO + tm - SUBLANES, LANES), F32),
                        pltpu.VMEM((SUBLANES - 1, HALO + tm - SUBLANES, LANES), F32)],
        name=name, params=_params(("arbitrary",), 52), comm=comm)


def _counts(i, tm, rows, off, win):
    t = i * tm + off + lax.broadcasted_iota(jnp.int32, (rows, 1), 0)
    return jnp.minimum(t + 1, win).astype(F32)


def _o_mix_fwd(q, cw, cb, cscale, *, tm, name):
    S = q.shape[0]
    WC = q.shape[1] // 2
    NG = len(POOL_WINDOWS)
    G = WC // NG
    nb = tm // PHALO

    def body(v_ref, z_ref, hv_ref, cw_ref, cb_ref, sc_ref, yy_ref, pooled_ref, gg_ref, ext_ref):
        i = pl.program_id(0)
        keep = (i > 0).astype(F32)
        for g, win in enumerate(POOL_WINDOWS):
            cs = slice(g * G, (g + 1) * G)
            v = v_ref[:, cs].astype(F32)
            ext_ref[0:PHALO, :] = hv_ref[:, cs].astype(F32) * keep
            ext_ref[PHALO:, :] = v
            s = v
            for j in range(1, win):
                s = s + ext_ref[pl.ds(PHALO - j, tm), :]
            pooled = (s / _counts(i, tm, tm, 0, win) - v).astype(BF16)
            pooled_ref[:, cs] = pooled
            gg = jnp.dot(pooled, cw_ref[g], preferred_element_type=F32) + cb_ref[:, cs]
            gg_ref[:, cs] = gg.astype(BF16)
            z = z_ref[:, cs].astype(F32)
            yy_ref[:, cs] = (gg * sc_ref[:, cs] * (z * _sig(z))).astype(BF16)

    row = lambda j: pl.BlockSpec((tm, WC), lambda i: (i, j))
    out = pl.BlockSpec((tm, WC), lambda i: (i, 0))
    return _call(
        body, grid=(S // tm,),
        in_specs=[row(0), row(1), pl.BlockSpec((PHALO, WC), lambda i: (jnp.maximum(i * nb - 1, 0), 0)),
                  _const((NG, G, G)), _const((1, WC)), _const((1, WC))],
        out_specs=[out, out, out],
        out_shape=[_sds((S, WC), BF16)] * 3, operands=(q, q, q, cw, cb, cscale),
        scratch_shapes=[pltpu.VMEM((PHALO + tm, G), F32)],
        name=name, params=_params(("arbitrary",), 40))[0]


def _o_mix_bwd(dyy, q, gg, pooled, cw, cscale, *, tm, name):
    S = q.shape[0]
    WC = q.shape[1] // 2
    NG = len(POOL_WINDOWS)
    G = WC // NG
    nb = tm // PHALO
    n_t = S // tm
    last_blk = S // PHALO - 1
    nt = (((1,), (1,)), ((), ()))
    tn = (((0,), (0,)), ((), ()))

    def body(dyy_ref, dyyf_ref, z_ref, zf_ref, gg_ref, pooled_ref, cw_ref, sc_ref,
             dq_ref, dcw_ref, dcb_ref, dsc_ref, ext_ref):
        i = pl.program_id(0)
        keep_next = (i < n_t - 1).astype(F32)

        @pl.when(i == 0)
        def _():
            dcw_ref[...] = jnp.zeros_like(dcw_ref)
            dcb_ref[...] = jnp.zeros_like(dcb_ref)
            dsc_ref[...] = jnp.zeros_like(dsc_ref)

        for g, win in enumerate(POOL_WINDOWS):
            cs = slice(g * G, (g + 1) * G)
            sc = sc_ref[:, cs]
            z = z_ref[:, cs].astype(F32)
            sz = _sig(z)
            dyy_c = dyy_ref[:, cs].astype(F32)
            ggv = gg_ref[:, cs].astype(F32)
            dyy0 = dyy_c * (z * sz)
            dq_ref[:, WC + cs.start:WC + cs.stop] = (dyy_c * (ggv * sc) * _dsilu(z, sz)).astype(BF16)
            dgg = dyy0 * sc
            dsc_ref[:, cs] += _colsum(dyy0 * ggv)
            dcb_ref[:, cs] += _colsum(dgg)
            dgg_b = dgg.astype(BF16)
            dcw_ref[g] += lax.dot_general(pooled_ref[:, cs], dgg_b, tn, preferred_element_type=F32)
            dpool = lax.dot_general(dgg_b, cw_ref[g], nt, preferred_element_type=F32)
            zf = zf_ref[:, cs].astype(F32)
            dgg_f = (dyyf_ref[:, cs].astype(F32) * (zf * _sig(zf)) * sc * keep_next).astype(BF16)
            dpool_f = lax.dot_general(dgg_f, cw_ref[g], nt, preferred_element_type=F32)
            ext_ref[0:tm, :] = dpool / _counts(i, tm, tm, 0, win)
            ext_ref[tm:tm + PHALO, :] = dpool_f / _counts(i, tm, PHALO, tm, win)
            dv = ext_ref[0:tm, :] - dpool
            for j in range(1, win):
                dv = dv + ext_ref[pl.ds(j, tm), :]
            dq_ref[:, cs] = dv.astype(BF16)

    row = lambda: pl.BlockSpec((tm, WC), lambda i: (i, 0))
    nxt = lambda j: pl.BlockSpec((PHALO, WC), lambda i: (jnp.minimum((i + 1) * nb, last_blk), j))
    return _call(
        body, grid=(n_t,),
        in_specs=[row(), nxt(0), pl.BlockSpec((tm, WC), lambda i: (i, 1)), nxt(1), row(), row(),
                  _const((NG, G, G)), _const((1, WC))],
        out_specs=[pl.BlockSpec((tm, 2 * WC), lambda i: (i, 0)), _const((NG, G, G)), _const((1, WC)), _const((1, WC))],
        out_shape=[_sds((S, 2 * WC), BF16), _sds((NG, G, G), F32), _sds((1, WC), F32), _sds((1, WC), F32)],
        operands=(dyy, dyy, q, q, gg, pooled, cw, cscale),
        scratch_shapes=[pltpu.VMEM((tm + PHALO, G), F32)],
        name=name, params=_params(("arbitrary",), 48))[0]


def _place():
    return lax.axis_index("x"), lax.axis_index("y"), lax.axis_index("c")


def _piece(ref, axis, size, index):
    start = index * size
    if axis == len(ref.shape) - 1:
        start = pl.multiple_of(start, LANES)
    idx = [slice(None)] * len(ref.shape)
    idx[axis] = pl.ds(start, size)
    return ref.at[tuple(idx)]


def _gather_copies(src, out, axis, size, send_sems, recv_sems, base, held=None):
    x, y, c = _place()
    sib, xn, yn = (x, y, 1 - c), (1 - x, y, c), (x, 1 - y, c)

    def blk(px, py, of=out):
        return _piece(of, axis, size, 4 * px + 2 * py + c)

    def half(ref, h):
        n = ref.shape[0] // 2
        return ref.at[pl.ds(h * n, n)]

    def rc(k, s, d, to):
        return pltpu.make_async_remote_copy(src_ref=s, dst_ref=d, send_sem=send_sems.at[base + k],
                                            recv_sem=recv_sems.at[base + k], device_id=to, device_id_type=MESH)

    own, xb, yb, db = blk(x, y), blk(1 - x, y), blk(x, 1 - y), blk(1 - x, 1 - y)
    got = out if held is None else held
    xs, ys, ds = blk(1 - x, y, got), blk(x, 1 - y, got), blk(1 - x, 1 - y, got)
    return [rc(0, src, own, sib), rc(1, src, own, xn), rc(2, src, own, yn),
            rc(3, half(xs, 0), half(xb, 0), yn), rc(4, half(ys, 1), half(yb, 1), xn),
            rc(5, xs, xb, sib), rc(6, ys, yb, sib), rc(7, ds, db, sib)]


N_GATHER = 8


def _gather_comm(shards, axes, phases):
    n = len(shards)
    if phases == "second":
        sizes = [s.shape[a] // N_DEV for s, a in zip(shards, axes)]
        full = [_sds(s.shape, s.dtype) for s in shards]
    else:
        sizes = [s.shape[a] for s, a in zip(shards, axes)]
        full = [_sds(s.shape[:a] + (N_DEV * s.shape[a],) + s.shape[a + 1:], s.dtype) for s, a in zip(shards, axes)]

    def plan(ins, outs, sems):
        x, y, c = _place()
        me = 4 * x + 2 * y + c
        if phases == "second":
            cps = [_gather_copies(_piece(ins[t], axes[t], sizes[t], me), outs[t], axes[t], sizes[t], sems[0], sems[1],
                                  N_GATHER * t, ins[t]) for t in range(n)]
        else:
            cps = [_gather_copies(sems[3 + t], outs[t], axes[t], sizes[t], sems[0], sems[1], N_GATHER * t)
                   for t in range(n)]
        mine = [pltpu.make_async_copy(sems[3 + t], _piece(outs[t], axes[t], sizes[t], me), sems[2].at[t])
                for t in range(n)] if phases != "second" else []
        return cps, mine

    def send_own(ins, outs, sems):
        cps, mine = plan(ins, outs, sems)
        for t in range(n):
            stage = pltpu.make_async_copy(ins[t], sems[3 + t], sems[2].at[t])
            stage.start()
            stage.wait()
            mine[t].start()
            for k in (0, 1, 2):
                cps[t][k].start()

    def pass_on(ins, outs, sems):
        cps, _ = plan(ins, outs, sems)
        for t in range(n):
            if phases == "all":
                cps[t][1].wait_recv()
            cps[t][3].start()
            cps[t][5].start()
        for t in range(n):
            if phases == "all":
                cps[t][2].wait_recv()
            cps[t][4].start()
            cps[t][6].start()

    def own_landed(ins, outs, sems):
        cps, mine = plan(ins, outs, sems)
        for t in range(n):
            for k in (0, 1, 2):
                cps[t][k].wait()
            mine[t].wait()

    def all_landed(ins, outs, sems):
        cps, mine = plan(ins, outs, sems)
        for t in range(n):
            cps[t][3].wait_recv()
            cps[t][4].wait_recv()
            cps[t][7].start()
        for t in range(n):
            for k in ((0, 5, 6, 7) if phases == "all" else (5, 6, 7)):
                cps[t][k].wait_recv()
            for k in (range(N_GATHER) if phases == "all" else range(3, N_GATHER)):
                cps[t][k].wait_send()
            if phases == "all":
                mine[t].wait()

    sems = [pltpu.SemaphoreType.DMA((N_GATHER * n,)), pltpu.SemaphoreType.DMA((N_GATHER * n,))]
    if phases != "second":
        sems.append(pltpu.SemaphoreType.DMA((n,)))
        sems += [pltpu.VMEM(s.shape, s.dtype) for s in shards]
    if phases == "all":
        return _Comm(shards, full, sems, send_own, all_landed, middle=pass_on)
    if phases == "first":
        return _Comm(shards, full, sems, send_own, own_landed)
    return _Comm(shards, full, sems, pass_on, all_landed, aliases={t: t for t in range(n)})


def _pair_comm(grads, axes, sizes):
    n = len(grads)
    outs_sds = [_sds((4,) + g.shape[:a] + (s,) + g.shape[a + 1:], g.dtype) for g, a, s in zip(grads, axes, sizes)]

    def copies(ins, outs, sems):
        send_sems, recv_sems = sems
        x, y, c = _place()
        return [pltpu.make_async_remote_copy(
            src_ref=_piece(ins[t], axes[t], sizes[t], 2 * qi + (1 - c)), dst_ref=outs[t].at[qi],
            send_sem=send_sems.at[4 * t + qi], recv_sem=recv_sems.at[4 * t + qi],
            device_id=(x, y, 1 - c), device_id_type=MESH) for t in range(n) for qi in range(4)]

    def start(ins, outs, sems):
        for cp in copies(ins, outs, sems):
            cp.start()

    def finish(ins, outs, sems):
        for cp in copies(ins, outs, sems):
            cp.wait()

    sems = [pltpu.SemaphoreType.DMA((4 * n,)), pltpu.SemaphoreType.DMA((4 * n,))]
    return _Comm(grads, outs_sds, sems, start, finish)


def _chip_comm(sums):
    n = len(sums)
    outs_sds = [_sds((3,) + s.shape[1:], s.dtype) for s in sums]

    def copies(ins, outs, sems):
        send_sems, recv_sems = sems
        x, y, c = _place()
        return [pltpu.make_async_remote_copy(
            src_ref=ins[t].at[2 * qx + qy], dst_ref=outs[t].at[j],
            send_sem=send_sems.at[3 * t + j], recv_sem=recv_sems.at[3 * t + j],
            device_id=(qx, qy, c), device_id_type=MESH)
            for t in range(n) for j, (qx, qy) in enumerate([(1 - x, y), (x, 1 - y), (1 - x, 1 - y)])]

    def start(ins, outs, sems):
        for cp in copies(ins, outs, sems):
            cp.start()

    def finish(ins, outs, sems):
        for cp in copies(ins, outs, sems):
            cp.wait()

    sems = [pltpu.SemaphoreType.DMA((3 * n,)), pltpu.SemaphoreType.DMA((3 * n,))]
    return _Comm(sums, outs_sds, sems, start, finish)


def _small_comm(small):
    def copies(ins, outs, sems):
        send_sems, recv_sems, local_sem = sems
        x, y, c = _place()
        mine = outs[0].at[4 * x + 2 * y + c]
        out = [pltpu.make_async_copy(ins[0], mine, local_sem.at[0])]
        for k in range(1, N_DEV):
            peer = (1 - x if k & 4 else x, 1 - y if k & 2 else y, 1 - c if k & 1 else c)
            out.append(pltpu.make_async_remote_copy(
                src_ref=ins[0], dst_ref=mine, send_sem=send_sems.at[k - 1], recv_sem=recv_sems.at[k - 1],
                device_id=peer, device_id_type=MESH))
        return out

    def start(ins, outs, sems):
        for cp in copies(ins, outs, sems):
            cp.start()

    def finish(ins, outs, sems):
        for cp in copies(ins, outs, sems):
            cp.wait()

    sems = [pltpu.SemaphoreType.DMA((N_DEV - 1,)), pltpu.SemaphoreType.DMA((N_DEV - 1,)), pltpu.SemaphoreType.DMA((1,))]
    return _Comm([small], [_sds((N_DEV,) + small.shape, small.dtype)], sems, start, finish)


def _small_scatter_comm(send):
    def copies(ins, outs, sems):
        send_sems, recv_sems, local_sem = sems
        x, y, c = _place()
        me = 4 * x + 2 * y + c
        out = [pltpu.make_async_copy(ins[0].at[me], outs[0].at[me], local_sem.at[0])]
        for k in range(1, N_DEV):
            px, py, pc = (1 - x if k & 4 else x, 1 - y if k & 2 else y, 1 - c if k & 1 else c)
            out.append(pltpu.make_async_remote_copy(
                src_ref=ins[0].at[4 * px + 2 * py + pc], dst_ref=outs[0].at[me], send_sem=send_sems.at[k - 1],
                recv_sem=recv_sems.at[k - 1], device_id=(px, py, pc), device_id_type=MESH))
        return out

    def start(ins, outs, sems):
        for cp in copies(ins, outs, sems):
            cp.start()

    def finish(ins, outs, sems):
        for cp in copies(ins, outs, sems):
            cp.wait()

    sems = [pltpu.SemaphoreType.DMA((N_DEV - 1,)), pltpu.SemaphoreType.DMA((N_DEV - 1,)), pltpu.SemaphoreType.DMA((1,))]
    return _Comm([send], [_sds(send.shape, send.dtype)], sems, start, finish)


def _pair_sum(c_idx, grad, recv, axis, size, split, *, name):
    nd = len(grad.shape)
    piece = grad.shape[:axis] + (size,) + grad.shape[axis + 1:]
    blk = (piece[0] // split,) + piece[1:]

    def g_map(q, r, c_ref):
        idx = [0] * nd
        idx[axis] = 2 * q + c_ref[0]
        idx[0] = idx[0] * split + r if axis == 0 else r
        return tuple(idx)

    def r_map(q, r, c_ref):
        return (q, r) + (0,) * (nd - 1)

    def body(c_ref, g_ref, r_ref, o_ref):
        o_ref[0] = (g_ref[...].astype(F32) + r_ref[0].astype(F32)).astype(BF16)

    return _call(
        body, grid=(4, split), prefetch=c_idx,
        in_specs=[pl.BlockSpec(blk, g_map), pl.BlockSpec((1,) + blk, r_map)],
        out_specs=[pl.BlockSpec((1,) + blk, r_map)], out_shape=[_sds((4,) + piece, BF16)],
        operands=(grad, recv), name=name, params=_params(("arbitrary", "arbitrary"), 32))[0][0]


def _adam_math(w, g, m, v):
    m = ADAM_B1 * m + (1.0 - ADAM_B1) * g
    v = ADAM_B2 * v + (1.0 - ADAM_B2) * (g * g)
    m_hat = m / (1.0 - ADAM_B1 ** ADAM_STEP)
    v_hat = v / (1.0 - ADAM_B2 ** ADAM_STEP)
    delta = -ADAM_LR * (m_hat / (jnp.sqrt(v_hat) + ADAM_EPS) + ADAM_WD * w)
    return delta, m, v


def _adam_big(q_idx, sums, recv, w, m, v, split, *, name, comm=None):
    shape = w.shape
    nd = len(shape)
    blk = (shape[0] // split,) + shape[1:]
    w_map = lambda r, q_ref: (r,) + (0,) * (nd - 1)
    s_map = lambda r, q_ref: (q_ref[0], r) + (0,) * (nd - 1)
    r_map = lambda r, q_ref: (0, r) + (0,) * (nd - 1)

    def body(q_ref, s_ref, r_ref, w_ref, m_ref, v_ref, g_ref, d_ref, nm_ref, nv_ref):
        g = s_ref[0].astype(F32) + r_ref[0].astype(F32) + r_ref[1].astype(F32) + r_ref[2].astype(F32)
        g_ref[...] = g
        d_ref[...], nm_ref[...], nv_ref[...] = _adam_math(w_ref[...], g, m_ref[...], v_ref[...])

    wspec = pl.BlockSpec(blk, w_map)
    return _call(
        body, grid=(split,), prefetch=q_idx,
        in_specs=[pl.BlockSpec((1,) + blk, s_map), pl.BlockSpec((3,) + blk, r_map), wspec, wspec, wspec],
        out_specs=[wspec] * 4, out_shape=[_sds(shape, F32)] * 4, operands=(sums, recv, w, m, v),
        name=name, params=_params(("arbitrary",), 32), comm=comm)


def _adam_small(parts, w, m, v, *, name):
    R = w.shape[0]

    def body(p_ref, w_ref, m_ref, v_ref, g_ref, d_ref, nm_ref, nv_ref):
        g = p_ref[0]
        for d in range(1, N_DEV):
            g = g + p_ref[d]
        g_ref[...] = g
        d_ref[...], nm_ref[...], nv_ref[...] = _adam_math(w_ref[...], g, m_ref[...], v_ref[...])

    whole = _const((R, LANES))
    return _call(
        body, grid=(1,), in_specs=[_const((N_DEV, R, LANES)), whole, whole, whole], out_specs=[whole] * 4,
        out_shape=[_sds((R, LANES), F32)] * 4, operands=(parts, w, m, v), name=name,
        params=_params(("arbitrary",), 32))[0]


def _pack(arrs):
    return jnp.concatenate([a.reshape(-1) for a in arrs]).reshape(-1, LANES)


def _unpack(packed, shapes):
    flat = packed.reshape(-1)
    out, off = [], 0
    for s in shapes:
        n = 1
        for d in s:
            n *= d
        out.append(flat[off:off + n].reshape(s))
        off += n
    return out


BIG = ("e_in", "e_out", "o_in", "o_cw", "o_out")
BIG_AXIS = dict(e_in=1, e_out=0, o_in=1, o_cw=1, o_out=0)
BIG_SPLIT = dict(e_in=8, e_out=4, o_in=4, o_cw=4, o_out=4)
REPLICATED = ("e_norm_pre", "e_norm_post", "e_b_conv_bias", "e_b_ln_g", "e_b_ln_b")
SHARDED = ("e_a_conv", "e_b_conv", "o_norm_pre", "o_norm_post", "o_c_b", "o_c_scale")
SMALL = REPLICATED + SHARDED


class _Exchange:
    def __init__(self, shards, small, order, c_idx):
        self.q_idx = order[:1]
        self.shards = shards
        self.small = small
        self.order = order
        self.c_idx = c_idx
        self.reduced = {}

    def gather(self, keys):
        return _gather_comm([self.shards[k] for k in keys], [BIG_AXIS[k] for k in keys], "all")

    def gather1(self, keys):
        return _gather_comm([self.shards[k] for k in keys], [BIG_AXIS[k] for k in keys], "first")

    def gather2(self, keys, firsts):
        return _gather_comm(firsts, [BIG_AXIS[k] for k in keys], "second")

    def pair(self, grads):
        keys = list(grads)
        return _pair_comm([grads[k] for k in keys], [BIG_AXIS[k] for k in keys],
                          [grads[k].shape[BIG_AXIS[k]] // N_DEV for k in keys])

    def pair_sums(self, grads, received):
        return {k: _pair_sum(self.c_idx, grads[k], r, BIG_AXIS[k], grads[k].shape[BIG_AXIS[k]] // N_DEV,
                             BIG_SPLIT[k], name="pair_sum_" + k) for k, r in zip(grads, received)}

    def chips(self, sums):
        return _chip_comm([sums[k] for k in sums])

    def done(self, sums, received):
        self.reduced.update({k: (sums[k], r, self.q_idx) for k, r in zip(sums, received)})


def _local_step(x, tgt, w_small, ex):
    S, D = x.shape
    tnt, tx, tw = min(TM_NT, S), min(TM_MIX, S), min(TM_WIDE, S)

    wt = {}
    p, h0, wt["e_in"], got = _gather_matmul(ex.order, x, w_small["e_norm_pre"], ex.shards["e_in"], tm=tw,
                                            name="e_in_fwd", comm=_small_comm(ex.small))
    per_dev = [_unpack(got[0][d], [w_small[k].shape for k in SHARDED]) for d in range(N_DEV)]
    sm = {k: w_small[k] for k in REPLICATED}
    for j, k in enumerate(SHARDED):
        sm[k] = jnp.concatenate([per_dev[d][j] for d in range(N_DEV)], axis=-1)
    n_groups = sm["o_c_b"].shape[0]
    sm["o_c_b"] = sm["o_c_b"].reshape(1, -1)

    W = p.shape[1] // 7
    (u, cb), got = _e_mix_fwd(p, sm["e_a_conv"], sm["e_b_conv"], sm["e_b_conv_bias"], sm["e_b_ln_g"],
                              sm["e_b_ln_b"], tm=tx, name="e_mix_fwd", comm=ex.gather(["e_out"]))
    wt["e_out"] = got[0]
    late = ["o_out", "o_cw"]
    (x1, y0), part = _out_norm_res(u, wt["e_out"], x, sm["e_norm_post"], tm=tw, name="e_out_fwd",
                                   comm=ex.gather1(late))
    q, h1, wt["o_in"], got = _gather_matmul(ex.order, x1, sm["o_norm_pre"], ex.shards["o_in"], tm=tw,
                                            name="o_in_fwd", comm=ex.gather2(late, part))
    wt.update(zip(late, got))
    yy, pooled, gg = _o_mix_fwd(q, wt["o_cw"], sm["o_c_b"], sm["o_c_scale"], tm=tw, name="o_mix_fwd")
    dout, dx2, dyy, lcol, dg_o_post = _out_loss(yy, wt["o_out"], x1, sm["o_norm_post"], tgt, tm=tx, name="o_out_loss")
    loss = (0.5 / D) * jnp.sum(lcol)

    dq, d_cw, d_cb, d_cscale = _o_mix_bwd(dyy, q, gg, pooled, wt["o_cw"], sm["o_c_scale"], tm=tw, name="o_mix_bwd")
    g_o_out, _ = _mm_tn(yy, dout, ts=tnt, tn=W, name="o_out_dw")
    ga = dict(o_out=g_o_out, o_cw=d_cw.astype(BF16))
    dh1, ra = _mm_nt(dq, wt["o_in"], tm=tnt, tk=W, name="o_in_bwd", comm=ex.pair(ga))
    sa = ex.pair_sums(ga, ra)
    (dx1, dy0, dg_o_pre, dg_e_post), ra = _pre_bwd_o(dh1, x1, dx2, y0, sm["o_norm_pre"], sm["e_norm_post"],
                                                     tm=tx, name="o_pre_bwd", comm=ex.chips(sa))
    ex.done(sa, ra)
    g_o_in, _ = _mm_tn(h1, dq, ts=tnt, tn=W, name="o_in_dw")
    gb = dict(o_in=g_o_in)
    du, rb = _mm_nt(dy0, wt["e_out"], tm=tnt, tk=W, name="e_out_bwd", comm=ex.pair(gb))
    sb = ex.pair_sums(gb, rb)
    g_e_out, _ = _mm_tn(u, dy0, ts=tnt, tn=W, name="e_out_dw")
    gc = dict(e_out=g_e_out)
    (dp, d_wa, d_wb, d_bias, d_lg, d_lb), rbc = _e_mix_bwd(
        du, p, cb, sm["e_a_conv"], sm["e_b_conv"], sm["e_b_ln_g"], sm["e_b_ln_b"], tm=tx, name="e_mix_bwd",
        comm=_merge(ex.chips(sb), ex.pair(gc)))
    ex.done(sb, rbc[:1])
    sc = ex.pair_sums(gc, rbc[1:])
    order_out = jnp.concatenate([ex.order[1:], ex.order[:1]])
    sd, from_chip, rc = _dw_reduce(order_out, h0, dp, ex.shards["e_in"].shape[1], ts=tnt, name="e_in_dw",
                                   comm=ex.chips(sc))
    ex.done(sc, rc)
    dh0, rd = _mm_nt(dp, wt["e_in"], tm=tnt, tk=W, name="e_in_bwd", comm=_diag_comm(sd, from_chip))
    ex.reduced["e_in"] = (sd, rd[0], jnp.full((1,), 3, jnp.int32))
    grad_x, dg_e_pre = _pre_bwd_e(dh0, x, dx1, sm["e_norm_pre"], tm=tw, name="e_pre_bwd")

    small = dict(e_norm_pre=dg_e_pre, e_norm_post=dg_e_post, e_a_conv=d_wa, e_b_conv=d_wb, e_b_conv_bias=d_bias,
                 e_b_ln_g=d_lg, e_b_ln_b=d_lb, o_norm_pre=dg_o_pre, o_norm_post=dg_o_post,
                 o_c_b=d_cb.reshape(n_groups, -1), o_c_scale=d_cscale)
    return loss, grad_x, small


def kernel(x, e_norm_pre, e_norm_post, e_w_in, e_a_conv, e_b_conv, e_b_conv_bias, e_b_ln_g, e_b_ln_b, e_w_out, o_norm_pre, o_norm_post, o_w_in, o_c_w, o_c_b, o_c_scale, o_w_out, loss_target, m_e_norm_pre, m_e_norm_post, m_e_w_in, m_e_a_conv, m_e_b_conv, m_e_b_conv_bias, m_e_b_ln_g, m_e_b_ln_b, m_e_w_out, m_o_norm_pre, m_o_norm_post, m_o_w_in, m_o_c_w, m_o_c_b, m_o_c_scale, m_o_w_out, v_e_norm_pre, v_e_norm_post, v_e_w_in, v_e_a_conv, v_e_b_conv, v_e_b_conv_bias, v_e_b_ln_g, v_e_b_ln_b, v_e_w_out, v_o_norm_pre, v_o_norm_post, v_o_w_in, v_o_c_w, v_o_c_b, v_o_c_scale, v_o_w_out):
    xi, yi, ci = _place()
    w_big = dict(e_in=e_w_in[0], e_out=e_w_out[0], o_in=o_w_in[0], o_cw=o_c_w[0], o_out=o_w_out[0])
    m_big = dict(e_in=m_e_w_in[0], e_out=m_e_w_out[0], o_in=m_o_w_in[0], o_cw=m_o_c_w[0], o_out=m_o_w_out[0])
    v_big = dict(e_in=v_e_w_in[0], e_out=v_e_w_out[0], o_in=v_o_w_in[0], o_cw=v_o_c_w[0], o_out=v_o_w_out[0])
    w_small = dict(e_norm_pre=e_norm_pre, e_norm_post=e_norm_post, e_b_conv_bias=e_b_conv_bias, e_b_ln_g=e_b_ln_g,
                   e_b_ln_b=e_b_ln_b, e_a_conv=e_a_conv[0], e_b_conv=e_b_conv[0], o_norm_pre=o_norm_pre,
                   o_norm_post=o_norm_post, o_c_b=o_c_b[0], o_c_scale=o_c_scale)
    m_small = dict(e_norm_pre=m_e_norm_pre, e_norm_post=m_e_norm_post, e_b_conv_bias=m_e_b_conv_bias,
                   e_b_ln_g=m_e_b_ln_g, e_b_ln_b=m_e_b_ln_b, e_a_conv=m_e_a_conv[0], e_b_conv=m_e_b_conv[0],
                   o_norm_pre=m_o_norm_pre, o_norm_post=m_o_norm_post, o_c_b=m_o_c_b[0], o_c_scale=m_o_c_scale)
    v_small = dict(e_norm_pre=v_e_norm_pre, e_norm_post=v_e_norm_post, e_b_conv_bias=v_e_b_conv_bias,
                   e_b_ln_g=v_e_b_ln_g, e_b_ln_b=v_e_b_ln_b, e_a_conv=v_e_a_conv[0], e_b_conv=v_e_b_conv[0],
                   o_norm_pre=v_o_norm_pre, o_norm_post=v_o_norm_post, o_c_b=v_o_c_b[0], o_c_scale=v_o_c_scale)

    c_idx = jnp.reshape(ci, (1,)).astype(jnp.int32)
    order = jnp.stack([2 * xi + yi, 2 * (1 - xi) + yi, 2 * xi + (1 - yi), 2 * (1 - xi) + (1 - yi)]).astype(jnp.int32)
    ex = _Exchange({k: w_big[k].astype(BF16) for k in BIG}, _pack([w_small[k] for k in SHARDED]), order, c_idx)
    loss, grad_x, g_small = _local_step(x[0], loss_target[0], w_small, ex)

    big_out = {}
    for k in BIG:
        sums, received, q_idx = ex.reduced[k]
        big_out[k] = _adam_big(q_idx, sums, received, w_big[k], m_big[k], v_big[k], BIG_SPLIT[k], name="adam_" + k)[0]

    rep = _pack([g_small[k] for k in REPLICATED])
    loss_row = jnp.pad(jnp.reshape(loss, (1, 1)), ((0, 0), (0, LANES - 1)))
    blocks = []
    for k in SHARDED:
        r, n = w_small[k].shape
        blocks.append(g_small[k].reshape(r, N_DEV, n).transpose(1, 0, 2).reshape(N_DEV, r * n))
    blocks = jnp.concatenate(blocks, axis=1).reshape(N_DEV, -1, LANES)
    head = jnp.concatenate([rep, loss_row], axis=0)
    send = jnp.concatenate([jnp.broadcast_to(head[None], (N_DEV,) + head.shape), blocks], axis=1)
    parts = _run_comm(_small_scatter_comm(send), "small_grad_exchange")[0]

    def own_rows(d):
        return jnp.concatenate([_pack([d[k] for k in REPLICATED]), jnp.ones((1, LANES), F32),
                                _pack([d[k] for k in SHARDED])], axis=0)

    res_small = _adam_small(parts, own_rows(w_small), own_rows(m_small), own_rows(v_small), name="adam_small")
    n_rep = rep.shape[0]
    loss = res_small[0][n_rep, 0]
    small_out = {k: [] for k in SMALL}
    for packed in res_small:
        for k, t in zip(REPLICATED, _unpack(packed[:n_rep], [w_small[k].shape for k in REPLICATED])):
            small_out[k].append(t)
        for k, t in zip(SHARDED, _unpack(packed[n_rep + 1:], [w_small[k].shape for k in SHARDED])):
            small_out[k].append(t)

    big_of = dict(e_w_in="e_in", e_w_out="e_out", o_w_in="o_in", o_c_w="o_cw", o_w_out="o_out")
    stacked = ("e_a_conv", "e_b_conv", "o_c_b")

    def leaf(name, which):
        if name in big_of:
            return big_out[big_of[name]][which][None]
        t = small_out[name][which]
        return t[None] if name in stacked else t

    order = ("e_norm_pre", "e_norm_post", "e_w_in", "e_a_conv", "e_b_conv", "e_b_conv_bias", "e_b_ln_g", "e_b_ln_b",
             "e_w_out", "o_norm_pre", "o_norm_post", "o_w_in", "o_c_w", "o_c_b", "o_c_scale", "o_w_out")
    outs = [loss, grad_x[None]]
    for which in range(4):
        outs += [leaf(nm, which) for nm in order]
    return tuple(outs)
```

```python
import jax
import jax.numpy as jnp
from jax import lax
from jax.experimental import pallas as pl
from jax.experimental.pallas import tpu as pltpu

F32 = jnp.float32
BF16 = jnp.bfloat16
EPS = 1e-6
MESH = pl.DeviceIdType.MESH
ANY = pl.BlockSpec(memory_space=pl.ANY)

N_DEV = 8
HALO = 32
PHALO = 16
CONV_A = 3
CONV_B = 31
POOL_WINDOWS = (2, 4, 8, 16)
LANES = 128
MIB = 1024 * 1024

ADAM_LR = 0.001
ADAM_B1 = 0.9
ADAM_B2 = 0.999
ADAM_EPS = 1e-08
ADAM_WD = 0.01
ADAM_STEP = 10

TM_NT = 1024
TM_MIX = 256
TM_WIDE = 512


def _sds(shape, dtype):
    return jax.ShapeDtypeStruct(tuple(shape), dtype)


def _params(sem, vmem_mib):
    return pltpu.CompilerParams(dimension_semantics=sem, vmem_limit_bytes=vmem_mib * MIB)


def _const(shape, single=False):
    n = len(shape)
    if single:
        return pl.BlockSpec(shape, lambda *_: (0,) * n, pipeline_mode=pl.Buffered(1))
    return pl.BlockSpec(shape, lambda *_: (0,) * n)


def _sig(v):
    return jax.nn.sigmoid(v)


def _dsilu(v, s):
    return s * (1.0 + v * (1.0 - s))


def _rms(v):
    return lax.rsqrt(jnp.mean(v * v, axis=-1, keepdims=True) + EPS)


def _norm_bwd(dn, n, r):
    return r * (dn - n * jnp.mean(dn * n, axis=-1, keepdims=True))


def _colsum(v):
    return jnp.sum(v, axis=0, keepdims=True)


class _Comm:
    def __init__(self, inputs, out_shapes, sems, start, finish, aliases=None, middle=None):
        self.inputs, self.out_shapes, self.sems = list(inputs), list(out_shapes), list(sems)
        self.start, self.finish, self.middle = start, finish, middle
        self.aliases = dict(aliases or {})


def _merge(*comms):
    comms = [c for c in comms if c is not None]
    if len(comms) <= 1:
        return comms[0] if comms else None
    spans, i0, o0, s0, aliases = [], 0, 0, 0, {}
    for c in comms:
        spans.append((i0, o0, s0))
        aliases.update({i0 + k: o0 + v for k, v in c.aliases.items()})
        i0, o0, s0 = i0 + len(c.inputs), o0 + len(c.out_shapes), s0 + len(c.sems)

    def run(which):
        def fn(ins, outs, sems):
            for c, (i, o, s) in zip(comms, spans):
                hook = getattr(c, which)
                if hook is not None:
                    hook(ins[i:i + len(c.inputs)], outs[o:o + len(c.out_shapes)], sems[s:s + len(c.sems)])
        return fn

    return _Comm([a for c in comms for a in c.inputs], [a for c in comms for a in c.out_shapes],
                 [a for c in comms for a in c.sems], run("start"), run("finish"), aliases,
                 run("middle") if any(c.middle is not None for c in comms) else None)


def _call(body, *, grid, in_specs, out_specs, out_shape, operands, name, params, scratch_shapes=(), comm=None,
          prefetch=None, own_copies_first=False):
    n_p = 0 if prefetch is None else 1
    n_i, n_o, n_s = len(in_specs), len(out_specs), len(scratch_shapes)
    if comm is None:
        comm = _Comm([], [], [], None, None)
    c_i, c_o = len(comm.inputs), len(comm.out_shapes)

    def carrier(*refs):
        pre, refs = refs[:n_p], refs[n_p:]
        ins, cins = refs[:n_i], refs[n_i:n_i + c_i]
        outs = refs[n_i + c_i:n_i + c_i + n_o]
        couts = refs[n_i + c_i + n_o:n_i + c_i + n_o + c_o]
        scr = refs[n_i + c_i + n_o + c_o:n_i + c_i + n_o + c_o + n_s]
        csems = refs[n_i + c_i + n_o + c_o + n_s:]
        ids = [pl.program_id(d) for d in range(len(grid))]
        first = ids[0] == 0
        half = ids[0] == grid[0] // 2
        last = ids[0] == grid[0] - 1
        for d in range(1, len(grid)):
            first = first & (ids[d] == 0)
            half = half & (ids[d] == 0)
            last = last & (ids[d] == grid[d] - 1)

        def start():
            if comm.start is not None:
                @pl.when(first)
                def _():
                    comm.start(cins, couts, csems)

        if not own_copies_first:
            start()
        if comm.middle is not None:
            assert grid[0] >= 2

            @pl.when(half)
            def _():
                comm.middle(cins, couts, csems)

        body(*pre, *ins, *outs, *scr)
        if own_copies_first:
            start()

        if comm.finish is not None:
            @pl.when(last)
            def _():
                comm.finish(cins, couts, csems)

    specs = dict(grid=grid, in_specs=list(in_specs) + [ANY] * c_i, out_specs=list(out_specs) + [ANY] * c_o,
                 scratch_shapes=list(scratch_shapes) + comm.sems)
    if n_p:
        specs = dict(grid_spec=pltpu.PrefetchScalarGridSpec(num_scalar_prefetch=1, **specs))
    res = pl.pallas_call(
        carrier, out_shape=list(out_shape) + comm.out_shapes,
        input_output_aliases={n_p + n_i + k: n_o + v for k, v in comm.aliases.items()},
        name=name, compiler_params=params, **specs)(*(() if prefetch is None else (prefetch,)), *operands, *comm.inputs)
    return list(res[:n_o]), list(res[n_o:])


def _run_comm(comm, name):
    c_i, c_o = len(comm.inputs), len(comm.out_shapes)

    def body(*refs):
        ins, outs, sems = refs[:c_i], refs[c_i:c_i + c_o], refs[c_i + c_o:]
        comm.start(ins, outs, sems)
        comm.finish(ins, outs, sems)

    res = pl.pallas_call(
        body, in_specs=[ANY] * c_i, out_specs=[ANY] * c_o, out_shape=comm.out_shapes, scratch_shapes=comm.sems,
        input_output_aliases=comm.aliases, name=name)(*comm.inputs)
    return list(res)


def _gather_matmul(order, x, g, shard, *, tm, name, comm=None):
    S, K = x.shape
    nb = shard.shape[1]
    n_i = S // tm

    def body(order_ref, x_ref, g_ref, shard_ref, p_ref, h_ref, full_ref, hbuf, wbuf, stage, send_sems, recv_sems,
             dma_sems):
        j, i = pl.program_id(0), pl.program_id(1)
        px, py, pc = _place()
        cps = _gather_copies(stage, full_ref, 1, nb, send_sems, recv_sems, 0)
        own = pltpu.make_async_copy(stage, _piece(full_ref, 1, nb, 4 * px + 2 * py + pc), dma_sems.at[0])
        keep_h = pltpu.make_async_copy(hbuf, h_ref, dma_sems.at[2])

        def load(src, dst):
            cp = pltpu.make_async_copy(src, dst, dma_sems.at[1])
            cp.start()
            cp.wait()

        def load_pair(qx, qy):
            load(_piece(full_ref, 1, 2 * nb, 2 * qx + qy), wbuf)

        @pl.when((j == 0) & (i == 0))
        def _():
            load(shard_ref, stage)
            own.start()
            for k in (0, 1, 2):
                cps[k].start()

        @pl.when(j == 0)
        def _():
            xx = x_ref[...]
            hbuf[i] = ((xx * _rms(xx)) * g_ref[...]).astype(BF16)

        @pl.when((j == 0) & (i == 0))
        def _():
            own.wait()
            cps[0].wait_recv()
            load_pair(px, py)

        @pl.when((j == 1) & (i == 0))
        def _():
            keep_h.start()
            cps[1].wait_recv()
            cps[3].start()
            cps[5].start()
            cps[2].wait_recv()
            cps[4].start()
            cps[6].start()
            cps[5].wait_recv()
            load_pair(1 - px, py)

        @pl.when((j == 2) & (i == 0))
        def _():
            cps[6].wait_recv()
            load_pair(px, 1 - py)

        @pl.when((j == 3) & (i == 0))
        def _():
            cps[3].wait_recv()
            cps[4].wait_recv()
            cps[7].start()
            cps[7].wait_recv()
            load_pair(1 - px, 1 - py)

        p_ref[...] = jnp.dot(hbuf[i], wbuf[...], preferred_element_type=F32).astype(BF16)

        @pl.when((j == 3) & (i == n_i - 1))
        def _():
            for cp in cps:
                cp.wait_send()
            keep_h.wait()

    first_pass = lambda j, i, o: (jnp.where(j == 0, i, n_i - 1), 0)
    outs, extra = _call(
        body, grid=(4, n_i), prefetch=order,
        in_specs=[pl.BlockSpec((tm, K), first_pass), pl.BlockSpec((1, K), lambda j, i, o: (0, 0)), ANY],
        out_specs=[pl.BlockSpec((tm, 2 * nb), lambda j, i, o: (i, o[j])), ANY, ANY],
        out_shape=[_sds((S, N_DEV * nb), BF16), _sds((n_i, tm, K), BF16), _sds((K, N_DEV * nb), BF16)],
        operands=(x, g, shard),
        scratch_shapes=[pltpu.VMEM((n_i, tm, K), BF16), pltpu.VMEM((K, 2 * nb), BF16), pltpu.VMEM((K, nb), BF16),
                        pltpu.SemaphoreType.DMA((N_GATHER,)), pltpu.SemaphoreType.DMA((N_GATHER,)),
                        pltpu.SemaphoreType.DMA((3,))],
        name=name, params=_params(("arbitrary", "arbitrary"), 58), comm=comm, own_copies_first=True)
    return outs[0], outs[1].reshape(S, K), outs[2], extra


def _out_norm_res(u, w, x, g, *, tm, name, comm=None):
    S, K = u.shape
    D = w.shape[1]

    def body(u_ref, w_ref, x_ref, g_ref, x1_ref, y_ref):
        y = jnp.dot(u_ref[...], w_ref[...], preferred_element_type=F32)
        y_ref[...] = y.astype(BF16)
        x1_ref[...] = x_ref[...] + (y * _rms(y)) * g_ref[...]

    return _call(
        body, grid=(S // tm,),
        in_specs=[pl.BlockSpec((tm, K), lambda i: (i, 0)), _const((K, D), single=True),
                  pl.BlockSpec((tm, D), lambda i: (i, 0)), _const((1, D))],
        out_specs=[pl.BlockSpec((tm, D), lambda i: (i, 0)), pl.BlockSpec((tm, D), lambda i: (i, 0))],
        out_shape=[_sds((S, D), F32), _sds((S, D), BF16)], operands=(u, w, x, g),
        name=name, params=_params(("arbitrary",), 56), comm=comm)


def _out_loss(yy, w, x1, g, tgt, *, tm, name):
    S, K = yy.shape
    D = w.shape[1]

    def body(yy_ref, w_ref, x1_ref, g_ref, t_ref, dout_ref, dx2_ref, dyy_ref, lcol_ref, dg_ref):
        out = jnp.dot(yy_ref[...], w_ref[...], preferred_element_type=F32)
        r = _rms(out)
        n = out * r
        gg = g_ref[...]
        e = x1_ref[...] + n * gg - t_ref[...]
        dx2 = e * (1.0 / D)
        dx2_ref[...] = dx2
        dout = _norm_bwd(dx2 * gg, n, r).astype(BF16)
        dout_ref[...] = dout
        dyy_ref[...] = lax.dot_general(dout, w_ref[...], (((1,), (1,)), ((), ())),
                                       preferred_element_type=F32).astype(BF16)

        @pl.when(pl.program_id(0) == 0)
        def _():
            lcol_ref[...] = jnp.zeros_like(lcol_ref)
            dg_ref[...] = jnp.zeros_like(dg_ref)

        lcol_ref[...] += _colsum(e * e)
        dg_ref[...] += _colsum(dx2 * n)

    return _call(
        body, grid=(S // tm,),
        in_specs=[pl.BlockSpec((tm, K), lambda i: (i, 0)), _const((K, D), single=True),
                  pl.BlockSpec((tm, D), lambda i: (i, 0)), _const((1, D)),
                  pl.BlockSpec((tm, D), lambda i: (i, 0))],
        out_specs=[pl.BlockSpec((tm, D), lambda i: (i, 0)), pl.BlockSpec((tm, D), lambda i: (i, 0)),
                   pl.BlockSpec((tm, K), lambda i: (i, 0)), _const((1, D)), _const((1, D))],
        out_shape=[_sds((S, D), BF16), _sds((S, D), F32), _sds((S, K), BF16), _sds((1, D), F32), _sds((1, D), F32)],
        operands=(yy, w, x1, g, tgt), name=name, params=_params(("arbitrary",), 52))[0]


def _mm_nt(a, w, *, tm, tk, name, comm=None):
    S, N = a.shape
    D = w.shape[0]
    n_k = N // tk

    def body(a_ref, w_ref, o_ref, acc_ref):
        k = pl.program_id(1)

        @pl.when(k == 0)
        def _():
            acc_ref[...] = jnp.zeros_like(acc_ref)

        acc_ref[...] = lax.dot_general(a_ref[...], w_ref[...], (((1,), (1,)), ((), ())),
                                       preferred_element_type=F32) + acc_ref[...]

        @pl.when(k == n_k - 1)
        def _():
            o_ref[...] = acc_ref[...].astype(BF16)

    outs, extra = _call(
        body, grid=(S // tm, n_k),
        in_specs=[pl.BlockSpec((tm, tk), lambda i, k: (i, k)), pl.BlockSpec((D, tk), lambda i, k: (0, k))],
        out_specs=[pl.BlockSpec((tm, D), lambda i, k: (i, 0))],
        out_shape=[_sds((S, D), BF16)], operands=(a, w),
        scratch_shapes=[pltpu.VMEM((tm, D), F32)],
        name=name, params=_params(("arbitrary", "arbitrary"), 48), comm=comm)
    return outs[0], extra


def _mm_tn(a, b, *, ts, tn, name, comm=None):
    S, M = a.shape
    N = b.shape[1]
    n_s = S // ts

    def body(a_ref, b_ref, o_ref, acc_ref):
        s = pl.program_id(1)

        @pl.when(s == 0)
        def _():
            acc_ref[...] = jnp.zeros_like(acc_ref)

        acc_ref[...] = lax.dot_general(a_ref[...], b_ref[...], (((0,), (0,)), ((), ())),
                                       preferred_element_type=F32) + acc_ref[...]

        @pl.when(s == n_s - 1)
        def _():
            o_ref[...] = acc_ref[...].astype(BF16)

    outs, extra = _call(
        body, grid=(N // tn, n_s),
        in_specs=[pl.BlockSpec((ts, M), lambda j, s: (s, 0)), pl.BlockSpec((ts, tn), lambda j, s: (s, j))],
        out_specs=[pl.BlockSpec((M, tn), lambda j, s: (0, j))],
        out_shape=[_sds((M, N), BF16)], operands=(a, b),
        scratch_shapes=[pltpu.VMEM((M, tn), F32)],
        name=name, params=_params(("arbitrary", "arbitrary"), 48), comm=comm)
    return outs[0], extra


def _dw_reduce(order, a, b, nb, *, ts, name, comm=None):
    S, M = a.shape
    n_s = S // ts
    rows = 512

    def body(order_ref, a_ref, b_ref, sums_ref, from_sib_ref, from_chip_ref, acc, send_buf, mine_buf, recv_buf,
             sib_send, sib_recv, chip_send, chip_recv, dma_sems):
        t, s = pl.program_id(0), pl.program_id(1)
        x, y, c = _place()
        targets = [(1 - x, y, c), (x, 1 - y, c)]

        def to_sibling(k):
            return pltpu.make_async_remote_copy(
                src_ref=send_buf, dst_ref=from_sib_ref.at[k], send_sem=sib_send.at[k], recv_sem=sib_recv.at[k],
                device_id=(x, y, 1 - c), device_id_type=MESH)

        def to_chip(k):
            return pltpu.make_async_remote_copy(
                src_ref=sums_ref.at[k], dst_ref=from_chip_ref.at[k], send_sem=chip_send.at[k],
                recv_sem=chip_recv.at[k], device_id=targets[k], device_id_type=MESH)

        def finish(k):
            to_sibling(k).wait()
            get = pltpu.make_async_copy(from_sib_ref.at[k], recv_buf, dma_sems.at[0])
            get.start()
            get.wait()
            for r in range(0, M, rows):
                recv_buf[r:r + rows, :] = (mine_buf[r:r + rows, :].astype(F32)
                                           + recv_buf[r:r + rows, :].astype(F32)).astype(BF16)
            put = pltpu.make_async_copy(recv_buf, sums_ref.at[k], dma_sems.at[1])
            put.start()
            put.wait()
            if k < 2:
                to_chip(k).start()

        for k in range(3):
            @pl.when((t == k + 1) & (s == 0))
            def _(k=k):
                finish(k)

        @pl.when(s == 0)
        def _():
            acc[...] = jnp.zeros_like(acc)

        acc[...] = lax.dot_general(a_ref[...], b_ref[...], (((0,), (0,)), ((), ())),
                                   preferred_element_type=F32) + acc[...]

        @pl.when(s == n_s - 1)
        def _():
            for r in range(0, M, rows):
                lo, hi = acc[r:r + rows, :nb], acc[r:r + rows, nb:]
                send_buf[r:r + rows, :] = jnp.where(c == 0, hi, lo).astype(BF16)
                mine_buf[r:r + rows, :] = jnp.where(c == 0, lo, hi).astype(BF16)
            to_sibling(t).start()

        @pl.when((t == 3) & (s == n_s - 1))
        def _():
            finish(3)
            to_chip(0).wait()
            to_chip(1).wait()

    piece = _sds((4, M, nb), BF16)
    outs, extra = _call(
        body, grid=(4, n_s), prefetch=order,
        in_specs=[pl.BlockSpec((ts, M), lambda t, s, o: (s, 0)), pl.BlockSpec((ts, 2 * nb), lambda t, s, o: (s, o[t]))],
        out_specs=[ANY, ANY, ANY], out_shape=[piece, piece, _sds((3, M, nb), BF16)], operands=(a, b),
        scratch_shapes=[pltpu.VMEM((M, 2 * nb), F32), pltpu.VMEM((M, nb), BF16), pltpu.VMEM((M, nb), BF16),
                        pltpu.VMEM((M, nb), BF16), pltpu.SemaphoreType.DMA((4,)), pltpu.SemaphoreType.DMA((4,)),
                        pltpu.SemaphoreType.DMA((2,)), pltpu.SemaphoreType.DMA((2,)), pltpu.SemaphoreType.DMA((2,))],
        name=name, params=_params(("arbitrary", "arbitrary"), 56), comm=comm)
    return outs[0], outs[2], extra


def _diag_comm(sums, from_chip):
    def copy(ins, outs, sems):
        x, y, c = _place()
        return pltpu.make_async_remote_copy(
            src_ref=ins[0].at[2], dst_ref=outs[0].at[2], send_sem=sems[0].at[0], recv_sem=sems[1].at[0],
            device_id=(1 - x, 1 - y, c), device_id_type=MESH)

    def start(ins, outs, sems):
        copy(ins, outs, sems).start()

    def finish(ins, outs, sems):
        copy(ins, outs, sems).wait()

    sems = [pltpu.SemaphoreType.DMA((1,)), pltpu.SemaphoreType.DMA((1,))]
    return _Comm([sums, from_chip], [_sds(from_chip.shape, from_chip.dtype)], sems, start, finish, aliases={1: 0})


def _pre_bwd_o(dh, x1, dx2, y0, g_pre, g_post, *, tm, name, comm=None):
    S, D = x1.shape

    def body(dh_ref, x1_ref, dx2_ref, y0_ref, gpre_ref, gpost_ref, dx1_ref, dy0_ref, dgpre_ref, dgpost_ref):
        @pl.when(pl.program_id(0) == 0)
        def _():
            dgpre_ref[...] = jnp.zeros_like(dgpre_ref)
            dgpost_ref[...] = jnp.zeros_like(dgpost_ref)

        dh = dh_ref[...].astype(F32)
        x1 = x1_ref[...]
        r2 = _rms(x1)
        xn = x1 * r2
        dgpre_ref[...] += _colsum(dh * xn)
        dx1 = dx2_ref[...] + _norm_bwd(dh * gpre_ref[...], xn, r2)
        dx1_ref[...] = dx1
        y = y0_ref[...].astype(F32)
        r1 = _rms(y)
        n1 = y * r1
        dgpost_ref[...] += _colsum(dx1 * n1)
        dy0_ref[...] = _norm_bwd(dx1 * gpost_ref[...], n1, r1).astype(BF16)

    row = pl.BlockSpec((tm, D), lambda i: (i, 0))
    return _call(
        body, grid=(S // tm,),
        in_specs=[row, row, row, row, _const((1, D)), _const((1, D))],
        out_specs=[row, row, _const((1, D)), _const((1, D))],
        out_shape=[_sds((S, D), F32), _sds((S, D), BF16), _sds((1, D), F32), _sds((1, D), F32)],
        operands=(dh, x1, dx2, y0, g_pre, g_post),
        name=name, params=_params(("arbitrary",), 48), comm=comm)


def _pre_bwd_e(dh, x, dx1, g_pre, *, tm, name):
    S, D = x.shape

    def body(dh_ref, x_ref, dx1_ref, gpre_ref, gx_ref, dgpre_ref):
        @pl.when(pl.program_id(0) == 0)
        def _():
            dgpre_ref[...] = jnp.zeros_like(dgpre_ref)

        dh = dh_ref[...].astype(F32)
        xx = x_ref[...]
        r0 = _rms(xx)
        xn = xx * r0
        dgpre_ref[...] += _colsum(dh * xn)
        gx_ref[...] = dx1_ref[...] + _norm_bwd(dh * gpre_ref[...], xn, r0)

    row = pl.BlockSpec((tm, D), lambda i: (i, 0))
    return _call(
        body, grid=(S // tm,),
        in_specs=[row, row, row, _const((1, D))],
        out_specs=[row, _const((1, D))],
        out_shape=[_sds((S, D), F32), _sds((1, D), F32)],
        operands=(dh, x, dx1, g_pre), name=name, params=_params(("arbitrary",), 56))[0]


SUBLANES = 8


def _shift_copies(sh_ref, ext_ref, cs):
    for b in range(1, SUBLANES):
        sh_ref[b - 1] = ext_ref[pl.ds(b, sh_ref.shape[1]), cs]


def _rows_at(ext_ref, sh_ref, off, cs, tm):
    b = off % SUBLANES
    if b == 0 or sh_ref is None:
        return ext_ref[pl.ds(off, tm), cs]
    return sh_ref[b - 1, pl.ds(off - b, tm), :]


def _taps(ext_ref, w_ref, n_taps, base, cs, tm, sh_ref=None):
    acc = _rows_at(ext_ref, sh_ref, base, cs, tm) * w_ref[0:1, cs]
    for k in range(1, n_taps):
        acc = acc + _rows_at(ext_ref, sh_ref, base + k, cs, tm) * w_ref[k:k + 1, cs]
    return acc


def _taps_rev(ext_ref, w_ref, n_taps, cs, tm, sh_ref=None):
    acc = _rows_at(ext_ref, sh_ref, n_taps - 1, cs, tm) * w_ref[0:1, cs]
    for k in range(1, n_taps):
        acc = acc + _rows_at(ext_ref, sh_ref, n_taps - 1 - k, cs, tm) * w_ref[k:k + 1, cs]
    return acc


def _e_mix_fwd(p, wa, wb, bias, ln_g, ln_b, *, tm, name, comm=None):
    S = p.shape[0]
    W = p.shape[1] // 7
    nb = tm // HALO
    chunks = [slice(c * LANES, (c + 1) * LANES) for c in range(W // LANES)]

    def body(p_ref, hax_ref, hac_ref, hbv_ref, hbg_ref, wa_ref, wb_ref, bias_ref, lg_ref, lb_ref,
             u_ref, cb_ref, ext_ref, sh_ref):
        keep = (pl.program_id(0) > 0).astype(F32)
        col = lambda j, cs: p_ref[:, j * W + cs.start:j * W + cs.stop].astype(F32)

        ext_ref[0:HALO, :] = hax_ref[...].astype(F32) * hac_ref[...].astype(F32) * keep
        ext_ref[HALO:, :] = p_ref[:, 2 * W:3 * W].astype(F32) * p_ref[:, 0:W].astype(F32)
        for cs in chunks:
            conv = _taps(ext_ref, wa_ref, CONV_A, HALO - (CONV_A - 1), cs, tm)
            az = col(3, cs)
            u_ref[:, cs] = (col(1, cs) * conv * (az * _sig(az))).astype(BF16)

        ext_ref[0:HALO, :] = hbv_ref[...].astype(F32) * _sig(hbg_ref[...].astype(F32)) * keep
        ext_ref[HALO:, :] = p_ref[:, 4 * W:5 * W].astype(F32) * _sig(p_ref[:, 5 * W:6 * W].astype(F32))
        s1 = jnp.zeros((tm, LANES), F32)
        for cs in chunks:
            _shift_copies(sh_ref, ext_ref, cs)
            cb = _taps(ext_ref, wb_ref, CONV_B, HALO - (CONV_B - 1), cs, tm, sh_ref) + bias_ref[:, cs]
            cb_ref[:, cs] = cb
            s1 = s1 + cb
        mu = jnp.sum(s1, axis=-1, keepdims=True) * (1.0 / W)
        s2 = jnp.zeros((tm, LANES), F32)
        for cs in chunks:
            xc = cb_ref[:, cs] - mu
            s2 = s2 + xc * xc
        rs = lax.rsqrt(jnp.sum(s2, axis=-1, keepdims=True) * (1.0 / W) + EPS)
        for cs in chunks:
            lb = (cb_ref[:, cs] - mu) * rs * lg_ref[:, cs] + lb_ref[:, cs]
            bz = col(6, cs)
            u_ref[:, W + cs.start:W + cs.stop] = (lb * _sig(lb) * (bz * _sig(bz))).astype(BF16)

    prev = lambda j: pl.BlockSpec((HALO, W), lambda i: (jnp.maximum(i * nb - 1, 0), j))
    return _call(
        body, grid=(S // tm,),
        in_specs=[pl.BlockSpec((tm, 7 * W), lambda i: (i, 0)), prev(0), prev(2), prev(4), prev(5),
                  _const((CONV_A, W)), _const((CONV_B, W)), _const((1, W)), _const((1, W)), _const((1, W))],
        out_specs=[pl.BlockSpec((tm, 2 * W), lambda i: (i, 0)), pl.BlockSpec((tm, W), lambda i: (i, 0))],
        out_shape=[_sds((S, 2 * W), BF16), _sds((S, W), F32)],
        operands=(p, p, p, p, p, wa, wb, bias, ln_g, ln_b),
        scratch_shapes=[pltpu.VMEM((HALO + tm, W), F32),
                        pltpu.VMEM((SUBLANES - 1, HALO + tm - SUBLANES, LANES), F32)],
        name=name, params=_params(("arbitrary",), 48), comm=comm)


def _e_mix_bwd(du, p, cb, wa, wb, ln_g, ln_b, *, tm, name, comm=None):
    S = p.shape[0]
    W = p.shape[1] // 7
    nb = tm // HALO
    n_t = S // tm
    last_blk = S // HALO - 1
    chunks = [slice(c * LANES, (c + 1) * LANES) for c in range(W // LANES)]

    def body(du_ref, duf_ref, p_ref, fab_ref, faz_ref, fbz_ref, hax_ref, hac_ref, hbv_ref, hbg_ref,
             cb_ref, cbf_ref, wa_ref, wb_ref, lg_ref, lb_ref,
             dp_ref, dwa_ref, dwb_ref, dbias_ref, dlg_ref, dlb_ref, extd_ref, extg_ref, shd_ref, shg_ref):
        i = pl.program_id(0)
        keep_prev = (i > 0).astype(F32)
        keep_next = (i < n_t - 1).astype(F32)
        col = lambda j, cs: p_ref[:, j * W + cs.start:j * W + cs.stop].astype(F32)

        @pl.when(i == 0)
        def _():
            dwa_ref[...] = jnp.zeros_like(dwa_ref)
            dwb_ref[...] = jnp.zeros_like(dwb_ref)
            dbias_ref[...] = jnp.zeros_like(dbias_ref)
            dlg_ref[...] = jnp.zeros_like(dlg_ref)
            dlb_ref[...] = jnp.zeros_like(dlb_ref)

        def dcb_rows(rows, cb_rows_ref, dub, bz_of, dst0, scale, main):
            cbv = cb_rows_ref[...]
            mu = jnp.mean(cbv, axis=-1, keepdims=True)
            xc = cbv - mu
            rs = lax.rsqrt(jnp.mean(xc * xc, axis=-1, keepdims=True) + EPS)
            m1 = jnp.zeros((rows, LANES), F32)
            m2 = jnp.zeros((rows, LANES), F32)
            for cs in chunks:
                nbv = (cb_rows_ref[:, cs] - mu) * rs
                lb = nbv * lg_ref[:, cs] + lb_ref[:, cs]
                sl = _sig(lb)
                bz = bz_of(cs)
                sz = _sig(bz)
                dub_c = dub(cs)
                dlb = dub_c * (bz * sz) * _dsilu(lb, sl)
                if main:
                    dlg_ref[:, cs] += _colsum(dlb * nbv)
                    dlb_ref[:, cs] += _colsum(dlb)
                    dp_ref[:, 6 * W + cs.start:6 * W + cs.stop] = (dub_c * (lb * sl) * _dsilu(bz, sz)).astype(BF16)
                dnb = dlb * lg_ref[:, cs]
                extd_ref[dst0:dst0 + rows, cs] = dnb
                m1 = m1 + dnb
                m2 = m2 + dnb * nbv
            m1 = jnp.sum(m1, axis=-1, keepdims=True) * (1.0 / W)
            m2 = jnp.sum(m2, axis=-1, keepdims=True) * (1.0 / W)
            for cs in chunks:
                nbv = (cb_rows_ref[:, cs] - mu) * rs
                dcb = rs * (extd_ref[dst0:dst0 + rows, cs] - m1 - nbv * m2) * scale
                extd_ref[dst0:dst0 + rows, cs] = dcb
                if main:
                    dbias_ref[:, cs] += _colsum(dcb)

        dcb_rows(tm, cb_ref, lambda cs: du_ref[:, W + cs.start:W + cs.stop].astype(F32),
                 lambda cs: col(6, cs), 0, 1.0, True)
        dcb_rows(HALO, cbf_ref, lambda cs: duf_ref[:, W + cs.start:W + cs.stop].astype(F32),
                 lambda cs: fbz_ref[:, cs].astype(F32), tm, keep_next, False)

        extg_ref[0:HALO, :] = hbv_ref[...].astype(F32) * _sig(hbg_ref[...].astype(F32)) * keep_prev
        extg_ref[HALO:, :] = p_ref[:, 4 * W:5 * W].astype(F32) * _sig(p_ref[:, 5 * W:6 * W].astype(F32))
        base_b = HALO - (CONV_B - 1)
        for cs in chunks:
            _shift_copies(shd_ref, extd_ref, cs)
            _shift_copies(shg_ref, extg_ref, cs)
            dgb = _taps_rev(extd_ref, wb_ref, CONV_B, cs, tm, shd_ref)
            bv = col(4, cs)
            sg = _sig(col(5, cs))
            dp_ref[:, 4 * W + cs.start:4 * W + cs.stop] = (dgb * sg).astype(BF16)
            dp_ref[:, 5 * W + cs.start:5 * W + cs.stop] = (dgb * bv * sg * (1.0 - sg)).astype(BF16)
            dcb = extd_ref[0:tm, cs]
            for k in range(CONV_B):
                dwb_ref[k:k + 1, cs] += _colsum(dcb * _rows_at(extg_ref, shg_ref, base_b + k, cs, tm))

        extg_ref[0:HALO, :] = hax_ref[...].astype(F32) * hac_ref[...].astype(F32) * keep_prev
        extg_ref[HALO:, :] = p_ref[:, 2 * W:3 * W].astype(F32) * p_ref[:, 0:W].astype(F32)
        base_a = HALO - (CONV_A - 1)
        for cs in chunks:
            conv = _taps(extg_ref, wa_ref, CONV_A, base_a, cs, tm)
            az = col(3, cs)
            sz = _sig(az)
            ab = col(1, cs)
            dua = du_ref[:, cs].astype(F32)
            dya = dua * (az * sz)
            dp_ref[:, W + cs.start:W + cs.stop] = (dya * conv).astype(BF16)
            dp_ref[:, 3 * W + cs.start:3 * W + cs.stop] = (dua * (ab * conv) * _dsilu(az, sz)).astype(BF16)
            extd_ref[0:tm, cs] = dya * ab
            azf = faz_ref[:, cs].astype(F32)
            extd_ref[tm:tm + HALO, cs] = (duf_ref[:, cs].astype(F32) * (azf * _sig(azf))
                                          * fab_ref[:, cs].astype(F32) * keep_next)
        for cs in chunks:
            dca = _taps_rev(extd_ref, wa_ref, CONV_A, cs, tm)
            dp_ref[:, cs] = (dca * col(2, cs)).astype(BF16)
            dp_ref[:, 2 * W + cs.start:2 * W + cs.stop] = (dca * col(0, cs)).astype(BF16)
            dconv = extd_ref[0:tm, cs]
            for k in range(CONV_A):
                dwa_ref[k:k + 1, cs] += _colsum(dconv * extg_ref[pl.ds(base_a + k, tm), cs])

    prev = lambda j: pl.BlockSpec((HALO, W), lambda i: (jnp.maximum(i * nb - 1, 0), j))
    nxt = lambda j, w: pl.BlockSpec((HALO, w), lambda i: (jnp.minimum((i + 1) * nb, last_blk), j))
    row = lambda w: pl.BlockSpec((tm, w), lambda i: (i, 0))
    return _call(
        body, grid=(n_t,),
        in_specs=[row(2 * W), nxt(0, 2 * W), row(7 * W), nxt(1, W), nxt(3, W), nxt(6, W),
                  prev(0), prev(2), prev(4), prev(5), row(W), nxt(0, W),
                  _const((CONV_A, W)), _const((CONV_B, W)), _const((1, W)), _const((1, W))],
        out_specs=[row(7 * W), _const((CONV_A, W)), _const((CONV_B, W)), _const((1, W)), _const((1, W)), _const((1, W))],
        out_shape=[_sds((S, 7 * W), BF16), _sds((CONV_A, W), F32), _sds((CONV_B, W), F32),
                   _sds((1, W), F32), _sds((1, W), F32), _sds((1, W), F32)],
        operands=(du, du, p, p, p, p, p, p, p, p, cb, cb, wa, wb, ln_g, ln_b),
        scratch_shapes=[pltpu.VMEM((tm + HALO, W), F32), pltpu.VMEM((HALO + tm, W), F32),
                        pltpu.VMEM((SUBLANES - 1, HALO + tm - SUBLANES, LANES), F32),
                        pltpu.VMEM((SUBLANES - 1, HALO + tm - SUBLANES, LANES), F32)],
        name=name, params=_params(("arbitrary",), 52), comm=comm)


def _counts(i, tm, rows, off, win):
    t = i * tm + off + lax.broadcasted_iota(jnp.int32, (rows, 1), 0)
    return jnp.minimum(t + 1, win).astype(F32)


def _o_mix_fwd(q, cw, cb, cscale, *, tm, name):
    S = q.shape[0]
    WC = q.shape[1] // 2
    NG = len(POOL_WINDOWS)
    G = WC // NG
    nb = tm // PHALO

    def body(v_ref, z_ref, hv_ref, cw_ref, cb_ref, sc_ref, yy_ref, pooled_ref, gg_ref, ext_ref):
        i = pl.program_id(0)
        keep = (i > 0).astype(F32)
        for g, win in enumerate(POOL_WINDOWS):
            cs = slice(g * G, (g + 1) * G)
            v = v_ref[:, cs].astype(F32)
            ext_ref[0:PHALO, :] = hv_ref[:, cs].astype(F32) * keep
            ext_ref[PHALO:, :] = v
            s = v
            for j in range(1, win):
                s = s + ext_ref[pl.ds(PHALO - j, tm), :]
            pooled = (s / _counts(i, tm, tm, 0, win) - v).astype(BF16)
            pooled_ref[:, cs] = pooled
            gg = jnp.dot(pooled, cw_ref[g], preferred_element_type=F32) + cb_ref[:, cs]
            gg_ref[:, cs] = gg.astype(BF16)
            z = z_ref[:, cs].astype(F32)
            yy_ref[:, cs] = (gg * sc_ref[:, cs] * (z * _sig(z))).astype(BF16)

    row = lambda j: pl.BlockSpec((tm, WC), lambda i: (i, j))
    out = pl.BlockSpec((tm, WC), lambda i: (i, 0))
    return _call(
        body, grid=(S // tm,),
        in_specs=[row(0), row(1), pl.BlockSpec((PHALO, WC), lambda i: (jnp.maximum(i * nb - 1, 0), 0)),
                  _const((NG, G, G)), _const((1, WC)), _const((1, WC))],
        out_specs=[out, out, out],
        out_shape=[_sds((S, WC), BF16)] * 3, operands=(q, q, q, cw, cb, cscale),
        scratch_shapes=[pltpu.VMEM((PHALO + tm, G), F32)],
        name=name, params=_params(("arbitrary",), 40))[0]


def _o_mix_bwd(dyy, q, gg, pooled, cw, cscale, *, tm, name):
    S = q.shape[0]
    WC = q.shape[1] // 2
    NG = len(POOL_WINDOWS)
    G = WC // NG
    nb = tm // PHALO
    n_t = S // tm
    last_blk = S // PHALO - 1
    nt = (((1,), (1,)), ((), ()))
    tn = (((0,), (0,)), ((), ()))

    def body(dyy_ref, dyyf_ref, z_ref, zf_ref, gg_ref, pooled_ref, cw_ref, sc_ref,
             dq_ref, dcw_ref, dcb_ref, dsc_ref, ext_ref):
        i = pl.program_id(0)
        keep_next = (i < n_t - 1).astype(F32)

        @pl.when(i == 0)
        def _():
            dcw_ref[...] = jnp.zeros_like(dcw_ref)
            dcb_ref[...] = jnp.zeros_like(dcb_ref)
            dsc_ref[...] = jnp.zeros_like(dsc_ref)

        for g, win in enumerate(POOL_WINDOWS):
            cs = slice(g * G, (g + 1) * G)
            sc = sc_ref[:, cs]
            z = z_ref[:, cs].astype(F32)
            sz = _sig(z)
            dyy_c = dyy_ref[:, cs].astype(F32)
            ggv = gg_ref[:, cs].astype(F32)
            dyy0 = dyy_c * (z * sz)
            dq_ref[:, WC + cs.start:WC + cs.stop] = (dyy_c * (ggv * sc) * _dsilu(z, sz)).astype(BF16)
            dgg = dyy0 * sc
            dsc_ref[:, cs] += _colsum(dyy0 * ggv)
            dcb_ref[:, cs] += _colsum(dgg)
            dgg_b = dgg.astype(BF16)
            dcw_ref[g] += lax.dot_general(pooled_ref[:, cs], dgg_b, tn, preferred_element_type=F32)
            dpool = lax.dot_general(dgg_b, cw_ref[g], nt, preferred_element_type=F32)
            zf = zf_ref[:, cs].astype(F32)
            dgg_f = (dyyf_ref[:, cs].astype(F32) * (zf * _sig(zf)) * sc * keep_next).astype(BF16)
            dpool_f = lax.dot_general(dgg_f, cw_ref[g], nt, preferred_element_type=F32)
            ext_ref[0:tm, :] = dpool / _counts(i, tm, tm, 0, win)
            ext_ref[tm:tm + PHALO, :] = dpool_f / _counts(i, tm, PHALO, tm, win)
            dv = ext_ref[0:tm, :] - dpool
            for j in range(1, win):
                dv = dv + ext_ref[pl.ds(j, tm), :]
            dq_ref[:, cs] = dv.astype(BF16)

    row = lambda: pl.BlockSpec((tm, WC), lambda i: (i, 0))
    nxt = lambda j: pl.BlockSpec((PHALO, WC), lambda i: (jnp.minimum((i + 1) * nb, last_blk), j))
    return _call(
        body, grid=(n_t,),
        in_specs=[row(), nxt(0), pl.BlockSpec((tm, WC), lambda i: (i, 1)), nxt(1), row(), row(),
                  _const((NG, G, G)), _const((1, WC))],
        out_specs=[pl.BlockSpec((tm, 2 * WC), lambda i: (i, 0)), _const((NG, G, G)), _const((1, WC)), _const((1, WC))],
        out_shape=[_sds((S, 2 * WC), BF16), _sds((NG, G, G), F32), _sds((1, WC), F32), _sds((1, WC), F32)],
        operands=(dyy, dyy, q, q, gg, pooled, cw, cscale),
        scratch_shapes=[pltpu.VMEM((tm + PHALO, G), F32)],
        name=name, params=_params(("arbitrary",), 48))[0]


def _place():
    return lax.axis_index("x"), lax.axis_index("y"), lax.axis_index("c")


def _piece(ref, axis, size, index):
    start = index * size
    if axis == len(ref.shape) - 1:
        start = pl.multiple_of(start, LANES)
    idx = [slice(None)] * len(ref.shape)
    idx[axis] = pl.ds(start, size)
    return ref.at[tuple(idx)]


def _gather_copies(src, out, axis, size, send_sems, recv_sems, base, held=None):
    x, y, c = _place()
    sib, xn, yn = (x, y, 1 - c), (1 - x, y, c), (x, 1 - y, c)

    def blk(px, py, of=out):
        return _piece(of, axis, size, 4 * px + 2 * py + c)

    def half(ref, h):
        n = ref.shape[0] // 2
        return ref.at[pl.ds(h * n, n)]

    def rc(k, s, d, to):
        return pltpu.make_async_remote_copy(src_ref=s, dst_ref=d, send_sem=send_sems.at[base + k],
                                            recv_sem=recv_sems.at[base + k], device_id=to, device_id_type=MESH)

    own, xb, yb, db = blk(x, y), blk(1 - x, y), blk(x, 1 - y), blk(1 - x, 1 - y)
    got = out if held is None else held
    xs, ys, ds = blk(1 - x, y, got), blk(x, 1 - y, got), blk(1 - x, 1 - y, got)
    return [rc(0, src, own, sib), rc(1, src, own, xn), rc(2, src, own, yn),
            rc(3, half(xs, 0), half(xb, 0), yn), rc(4, half(ys, 1), half(yb, 1), xn),
            rc(5, xs, xb, sib), rc(6, ys, yb, sib), rc(7, ds, db, sib)]


N_GATHER = 8


def _gather_comm(shards, axes, phases):
    n = len(shards)
    if phases == "second":
        sizes = [s.shape[a] // N_DEV for s, a in zip(shards, axes)]
        full = [_sds(s.shape, s.dtype) for s in shards]
    else:
        sizes = [s.shape[a] for s, a in zip(shards, axes)]
        full = [_sds(s.shape[:a] + (N_DEV * s.shape[a],) + s.shape[a + 1:], s.dtype) for s, a in zip(shards, axes)]

    def plan(ins, outs, sems):
        x, y, c = _place()
        me = 4 * x + 2 * y + c
        if phases == "second":
            cps = [_gather_copies(_piece(ins[t], axes[t], sizes[t], me), outs[t], axes[t], sizes[t], sems[0], sems[1],
                                  N_GATHER * t, ins[t]) for t in range(n)]
        else:
            cps = [_gather_copies(sems[3 + t], outs[t], axes[t], sizes[t], sems[0], sems[1], N_GATHER * t)
                   for t in range(n)]
        mine = [pltpu.make_async_copy(sems[3 + t], _piece(outs[t], axes[t], sizes[t], me), sems[2].at[t])
                for t in range(n)] if phases != "second" else []
        return cps, mine

    def send_own(ins, outs, sems):
        cps, mine = plan(ins, outs, sems)
        for t in range(n):
            stage = pltpu.make_async_copy(ins[t], sems[3 + t], sems[2].at[t])
            stage.start()
            stage.wait()
            mine[t].start()
            for k in (0, 1, 2):
                cps[t][k].start()

    def pass_on(ins, outs, sems):
        cps, _ = plan(ins, outs, sems)
        for t in range(n):
            if phases == "all":
                cps[t][1].wait_recv()
            cps[t][3].start()
            cps[t][5].start()
        for t in range(n):
            if phases == "all":
                cps[t][2].wait_recv()
            cps[t][4].start()
            cps[t][6].start()

    def own_landed(ins, outs, sems):
        cps, mine = plan(ins, outs, sems)
        for t in range(n):
            for k in (0, 1, 2):
                cps[t][k].wait()
            mine[t].wait()

    def all_landed(ins, outs, sems):
        cps, mine = plan(ins, outs, sems)
        for t in range(n):
            cps[t][3].wait_recv()
            cps[t][4].wait_recv()
            cps[t][7].start()
        for t in range(n):
            for k in ((0, 5, 6, 7) if phases == "all" else (5, 6, 7)):
                cps[t][k].wait_recv()
            for k in (range(N_GATHER) if phases == "all" else range(3, N_GATHER)):
                cps[t][k].wait_send()
            if phases == "all":
                mine[t].wait()

    sems = [pltpu.SemaphoreType.DMA((N_GATHER * n,)), pltpu.SemaphoreType.DMA((N_GATHER * n,))]
    if phases != "second":
        sems.append(pltpu.SemaphoreType.DMA((n,)))
        sems += [pltpu.VMEM(s.shape, s.dtype) for s in shards]
    if phases == "all":
        return _Comm(shards, full, sems, send_own, all_landed, middle=pass_on)
    if phases == "first":
        return _Comm(shards, full, sems, send_own, own_landed)
    return _Comm(shards, full, sems, pass_on, all_landed, aliases={t: t for t in range(n)})


def _pair_comm(grads, axes, sizes):
    n = len(grads)
    outs_sds = [_sds((4,) + g.shape[:a] + (s,) + g.shape[a + 1:], g.dtype) for g, a, s in zip(grads, axes, sizes)]

    def copies(ins, outs, sems):
        send_sems, recv_sems = sems
        x, y, c = _place()
        return [pltpu.make_async_remote_copy(
            src_ref=_piece(ins[t], axes[t], sizes[t], 2 * qi + (1 - c)), dst_ref=outs[t].at[qi],
            send_sem=send_sems.at[4 * t + qi], recv_sem=recv_sems.at[4 * t + qi],
            device_id=(x, y, 1 - c), device_id_type=MESH) for t in range(n) for qi in range(4)]

    def start(ins, outs, sems):
        for cp in copies(ins, outs, sems):
            cp.start()

    def finish(ins, outs, sems):
        for cp in copies(ins, outs, sems):
            cp.wait()

    sems = [pltpu.SemaphoreType.DMA((4 * n,)), pltpu.SemaphoreType.DMA((4 * n,))]
    return _Comm(grads, outs_sds, sems, start, finish)


def _chip_comm(sums):
    n = len(sums)
    outs_sds = [_sds((3,) + s.shape[1:], s.dtype) for s in sums]

    def copies(ins, outs, sems):
        send_sems, recv_sems = sems
        x, y, c = _place()
        return [pltpu.make_async_remote_copy(
            src_ref=ins[t].at[2 * qx + qy], dst_ref=outs[t].at[j],
            send_sem=send_sems.at[3 * t + j], recv_sem=recv_sems.at[3 * t + j],
            device_id=(qx, qy, c), device_id_type=MESH)
            for t in range(n) for j, (qx, qy) in enumerate([(1 - x, y), (x, 1 - y), (1 - x, 1 - y)])]

    def start(ins, outs, sems):
        for cp in copies(ins, outs, sems):
            cp.start()

    def finish(ins, outs, sems):
        for cp in copies(ins, outs, sems):
            cp.wait()

    sems = [pltpu.SemaphoreType.DMA((3 * n,)), pltpu.SemaphoreType.DMA((3 * n,))]
    return _Comm(sums, outs_sds, sems, start, finish)


def _small_comm(small):
    def copies(ins, outs, sems):
        send_sems, recv_sems, local_sem = sems
        x, y, c = _place()
        mine = outs[0].at[4 * x + 2 * y + c]
        out = [pltpu.make_async_copy(ins[0], mine, local_sem.at[0])]
        for k in range(1, N_DEV):
            peer = (1 - x if k & 4 else x, 1 - y if k & 2 else y, 1 - c if k & 1 else c)
            out.append(pltpu.make_async_remote_copy(
                src_ref=ins[0], dst_ref=mine, send_sem=send_sems.at[k - 1], recv_sem=recv_sems.at[k - 1],
                device_id=peer, device_id_type=MESH))
        return out

    def start(ins, outs, sems):
        for cp in copies(ins, outs, sems):
            cp.start()

    def finish(ins, outs, sems):
        for cp in copies(ins, outs, sems):
            cp.wait()

    sems = [pltpu.SemaphoreType.DMA((N_DEV - 1,)), pltpu.SemaphoreType.DMA((N_DEV - 1,)), pltpu.SemaphoreType.DMA((1,))]
    return _Comm([small], [_sds((N_DEV,) + small.shape, small.dtype)], sems, start, finish)


def _small_scatter_comm(send):
    def copies(ins, outs, sems):
        send_sems, recv_sems, local_sem = sems
        x, y, c = _place()
        me = 4 * x + 2 * y + c
        out = [pltpu.make_async_copy(ins[0].at[me], outs[0].at[me], local_sem.at[0])]
        for k in range(1, N_DEV):
            px, py, pc = (1 - x if k & 4 else x, 1 - y if k & 2 else y, 1 - c if k & 1 else c)
            out.append(pltpu.make_async_remote_copy(
                src_ref=ins[0].at[4 * px + 2 * py + pc], dst_ref=outs[0].at[me], send_sem=send_sems.at[k - 1],
                recv_sem=recv_sems.at[k - 1], device_id=(px, py, pc), device_id_type=MESH))
        return out

    def start(ins, outs, sems):
        for cp in copies(ins, outs, sems):
            cp.start()

    def finish(ins, outs, sems):
        for cp in copies(ins, outs, sems):
            cp.wait()

    sems = [pltpu.SemaphoreType.DMA((N_DEV - 1,)), pltpu.SemaphoreType.DMA((N_DEV - 1,)), pltpu.SemaphoreType.DMA((1,))]
    return _Comm([send], [_sds(send.shape, send.dtype)], sems, start, finish)


def _pair_sum(c_idx, grad, recv, axis, size, split, *, name):
    nd = len(grad.shape)
    piece = grad.shape[:axis] + (size,) + grad.shape[axis + 1:]
    blk = (piece[0] // split,) + piece[1:]

    def g_map(q, r, c_ref):
        idx = [0] * nd
        idx[axis] = 2 * q + c_ref[0]
        idx[0] = idx[0] * split + r if axis == 0 else r
        return tuple(idx)

    def r_map(q, r, c_ref):
        return (q, r) + (0,) * (nd - 1)

    def body(c_ref, g_ref, r_ref, o_ref):
        o_ref[0] = (g_ref[...].astype(F32) + r_ref[0].astype(F32)).astype(BF16)

    return _call(
        body, grid=(4, split), prefetch=c_idx,
        in_specs=[pl.BlockSpec(blk, g_map), pl.BlockSpec((1,) + blk, r_map)],
        out_specs=[pl.BlockSpec((1,) + blk, r_map)], out_shape=[_sds((4,) + piece, BF16)],
        operands=(grad, recv), name=name, params=_params(("arbitrary", "arbitrary"), 32))[0][0]


def _adam_math(w, g, m, v):
    m = ADAM_B1 * m + (1.0 - ADAM_B1) * g
    v = ADAM_B2 * v + (1.0 - ADAM_B2) * (g * g)
    m_hat = m / (1.0 - ADAM_B1 ** ADAM_STEP)
    v_hat = v / (1.0 - ADAM_B2 ** ADAM_STEP)
    delta = -ADAM_LR * (m_hat / (jnp.sqrt(v_hat) + ADAM_EPS) + ADAM_WD * w)
    return delta, m, v


def _adam_big(q_idx, sums, recv, w, m, v, split, *, name, comm=None):
    shape = w.shape
    nd = len(shape)
    blk = (shape[0] // split,) + shape[1:]
    w_map = lambda r, q_ref: (r,) + (0,) * (nd - 1)
    s_map = lambda r, q_ref: (q_ref[0], r) + (0,) * (nd - 1)
    r_map = lambda r, q_ref: (0, r) + (0,) * (nd - 1)

    def body(q_ref, s_ref, r_ref, w_ref, m_ref, v_ref, g_ref, d_ref, nm_ref, nv_ref):
        g = s_ref[0].astype(F32) + r_ref[0].astype(F32) + r_ref[1].astype(F32) + r_ref[2].astype(F32)
        g_ref[...] = g
        d_ref[...], nm_ref[...], nv_ref[...] = _adam_math(w_ref[...], g, m_ref[...], v_ref[...])

    wspec = pl.BlockSpec(blk, w_map)
    return _call(
        body, grid=(split,), prefetch=q_idx,
        in_specs=[pl.BlockSpec((1,) + blk, s_map), pl.BlockSpec((3,) + blk, r_map), wspec, wspec, wspec],
        out_specs=[wspec] * 4, out_shape=[_sds(shape, F32)] * 4, operands=(sums, recv, w, m, v),
        name=name, params=_params(("arbitrary",), 32), comm=comm)


def _adam_small(parts, w, m, v, *, name):
    R = w.shape[0]

    def body(p_ref, w_ref, m_ref, v_ref, g_ref, d_ref, nm_ref, nv_ref):
        g = p_ref[0]
        for d in range(1, N_DEV):
            g = g + p_ref[d]
        g_ref[...] = g
        d_ref[...], nm_ref[...], nv_ref[...] = _adam_math(w_ref[...], g, m_ref[...], v_ref[...])

    whole = _const((R, LANES))
    return _call(
        body, grid=(1,), in_specs=[_const((N_DEV, R, LANES)), whole, whole, whole], out_specs=[whole] * 4,
        out_shape=[_sds((R, LANES), F32)] * 4, operands=(parts, w, m, v), name=name,
        params=_params(("arbitrary",), 32))[0]


def _pack(arrs):
    return jnp.concatenate([a.reshape(-1) for a in arrs]).reshape(-1, LANES)


def _unpack(packed, shapes):
    flat = packed.reshape(-1)
    out, off = [], 0
    for s in shapes:
        n = 1
        for d in s:
            n *= d
        out.append(flat[off:off + n].reshape(s))
        off += n
    return out


BIG = ("e_in", "e_out", "o_in", "o_cw", "o_out")
BIG_AXIS = dict(e_in=1, e_out=0, o_in=1, o_cw=1, o_out=0)
BIG_SPLIT = dict(e_in=8, e_out=4, o_in=4, o_cw=4, o_out=4)
REPLICATED = ("e_norm_pre", "e_norm_post", "e_b_conv_bias", "e_b_ln_g", "e_b_ln_b")
SHARDED = ("e_a_conv", "e_b_conv", "o_norm_pre", "o_norm_post", "o_c_b", "o_c_scale")
SMALL = REPLICATED + SHARDED


class _Exchange:
    def __init__(self, shards, small, order, c_idx):
        self.q_idx = order[:1]
        self.shards = shards
        self.small = small
        self.order = order
        self.c_idx = c_idx
        self.reduced = {}

    def gather(self, keys):
        return _gather_comm([self.shards[k] for k in keys], [BIG_AXIS[k] for k in keys], "all")

    def gather1(self, keys):
        return _gather_comm([self.shards[k] for k in keys], [BIG_AXIS[k] for k in keys], "first")

    def gather2(self, keys, firsts):
        return _gather_comm(firsts, [BIG_AXIS[k] for k in keys], "second")

    def pair(self, grads):
        keys = list(grads)
        return _pair_comm([grads[k] for k in keys], [BIG_AXIS[k] for k in keys],
                          [grads[k].shape[BIG_AXIS[k]] // N_DEV for k in keys])

    def pair_sums(self, grads, received):
        return {k: _pair_sum(self.c_idx, grads[k], r, BIG_AXIS[k], grads[k].shape[BIG_AXIS[k]] // N_DEV,
                             BIG_SPLIT[k], name="pair_sum_" + k) for k, r in zip(grads, received)}

    def chips(self, sums):
        return _chip_comm([sums[k] for k in sums])

    def done(self, sums, received):
        self.reduced.update({k: (sums[k], r, self.q_idx) for k, r in zip(sums, received)})


def _local_step(x, tgt, w_small, ex):
    S, D = x.shape
    tnt, tx, tw = min(TM_NT, S), min(TM_MIX, S), min(TM_WIDE, S)

    wt = {}
    p, h0, wt["e_in"], got = _gather_matmul(ex.order, x, w_small["e_norm_pre"], ex.shards["e_in"], tm=tw,
                                            name="e_in_fwd", comm=_merge(_small_comm(ex.small), ex.gather(["e_out"])))
    wt["e_out"] = got[1]
    per_dev = [_unpack(got[0][d], [w_small[k].shape for k in SHARDED]) for d in range(N_DEV)]
    sm = {k: w_small[k] for k in REPLICATED}
    for j, k in enumerate(SHARDED):
        sm[k] = jnp.concatenate([per_dev[d][j] for d in range(N_DEV)], axis=-1)
    n_groups = sm["o_c_b"].shape[0]
    sm["o_c_b"] = sm["o_c_b"].reshape(1, -1)

    W = p.shape[1] // 7
    (u, cb), _ = _e_mix_fwd(p, sm["e_a_conv"], sm["e_b_conv"], sm["e_b_conv_bias"], sm["e_b_ln_g"],
                            sm["e_b_ln_b"], tm=tx, name="e_mix_fwd")
    late = ["o_out", "o_cw"]
    (x1, y0), part = _out_norm_res(u, wt["e_out"], x, sm["e_norm_post"], tm=tw, name="e_out_fwd",
                                   comm=ex.gather1(late))
    q, h1, wt["o_in"], got = _gather_matmul(ex.order, x1, sm["o_norm_pre"], ex.shards["o_in"], tm=tw,
                                            name="o_in_fwd", comm=ex.gather2(late, part))
    wt.update(zip(late, got))
    yy, pooled, gg = _o_mix_fwd(q, wt["o_cw"], sm["o_c_b"], sm["o_c_scale"], tm=tw, name="o_mix_fwd")
    dout, dx2, dyy, lcol, dg_o_post = _out_loss(yy, wt["o_out"], x1, sm["o_norm_post"], tgt, tm=tx, name="o_out_loss")
    loss = (0.5 / D) * jnp.sum(lcol)

    dq, d_cw, d_cb, d_cscale = _o_mix_bwd(dyy, q, gg, pooled, wt["o_cw"], sm["o_c_scale"], tm=tw, name="o_mix_bwd")
    g_o_out, _ = _mm_tn(yy, dout, ts=tnt, tn=W, name="o_out_dw")
    ga = dict(o_out=g_o_out, o_cw=d_cw.astype(BF16))
    dh1, ra = _mm_nt(dq, wt["o_in"], tm=tnt, tk=W, name="o_in_bwd", comm=ex.pair(ga))
    sa = ex.pair_sums(ga, ra)
    (dx1, dy0, dg_o_pre, dg_e_post), ra = _pre_bwd_o(dh1, x1, dx2, y0, sm["o_norm_pre"], sm["e_norm_post"],
                                                     tm=tx, name="o_pre_bwd", comm=ex.chips(sa))
    ex.done(sa, ra)
    g_o_in, _ = _mm_tn(h1, dq, ts=tnt, tn=W, name="o_in_dw")
    gb = dict(o_in=g_o_in)
    du, rb = _mm_nt(dy0, wt["e_out"], tm=tnt, tk=W, name="e_out_bwd", comm=ex.pair(gb))
    sb = ex.pair_sums(gb, rb)
    g_e_out, _ = _mm_tn(u, dy0, ts=tnt, tn=W, name="e_out_dw")
    gc = dict(e_out=g_e_out)
    (dp, d_wa, d_wb, d_bias, d_lg, d_lb), rbc = _e_mix_bwd(
        du, p, cb, sm["e_a_conv"], sm["e_b_conv"], sm["e_b_ln_g"], sm["e_b_ln_b"], tm=tx, name="e_mix_bwd",
        comm=_merge(ex.chips(sb), ex.pair(gc)))
    ex.done(sb, rbc[:1])
    sc = ex.pair_sums(gc, rbc[1:])
    order_out = jnp.concatenate([ex.order[1:], ex.order[:1]])
    sd, from_chip, rc = _dw_reduce(order_out, h0, dp, ex.shards["e_in"].shape[1], ts=tnt, name="e_in_dw",
                                   comm=ex.chips(sc))
    ex.done(sc, rc)
    dh0, rd = _mm_nt(dp, wt["e_in"], tm=tnt, tk=W, name="e_in_bwd", comm=_diag_comm(sd, from_chip))
    ex.reduced["e_in"] = (sd, rd[0], jnp.full((1,), 3, jnp.int32))
    grad_x, dg_e_pre = _pre_bwd_e(dh0, x, dx1, sm["e_norm_pre"], tm=tw, name="e_pre_bwd")

    small = dict(e_norm_pre=dg_e_pre, e_norm_post=dg_e_post, e_a_conv=d_wa, e_b_conv=d_wb, e_b_conv_bias=d_bias,
                 e_b_ln_g=d_lg, e_b_ln_b=d_lb, o_norm_pre=dg_o_pre, o_norm_post=dg_o_post,
                 o_c_b=d_cb.reshape(n_groups, -1), o_c_scale=d_cscale)
    return loss, grad_x, small


def kernel(x, e_norm_pre, e_norm_post, e_w_in, e_a_conv, e_b_conv, e_b_conv_bias, e_b_ln_g, e_b_ln_b, e_w_out, o_norm_pre, o_norm_post, o_w_in, o_c_w, o_c_b, o_c_scale, o_w_out, loss_target, m_e_norm_pre, m_e_norm_post, m_e_w_in, m_e_a_conv, m_e_b_conv, m_e_b_conv_bias, m_e_b_ln_g, m_e_b_ln_b, m_e_w_out, m_o_norm_pre, m_o_norm_post, m_o_w_in, m_o_c_w, m_o_c_b, m_o_c_scale, m_o_w_out, v_e_norm_pre, v_e_norm_post, v_e_w_in, v_e_a_conv, v_e_b_conv, v_e_b_conv_bias, v_e_b_ln_g, v_e_b_ln_b, v_e_w_out, v_o_norm_pre, v_o_norm_post, v_o_w_in, v_o_c_w, v_o_c_b, v_o_c_scale, v_o_w_out):
    xi, yi, ci = _place()
    w_big = dict(e_in=e_w_in[0], e_out=e_w_out[0], o_in=o_w_in[0], o_cw=o_c_w[0], o_out=o_w_out[0])
    m_big = dict(e_in=m_e_w_in[0], e_out=m_e_w_out[0], o_in=m_o_w_in[0], o_cw=m_o_c_w[0], o_out=m_o_w_out[0])
    v_big = dict(e_in=v_e_w_in[0], e_out=v_e_w_out[0], o_in=v_o_w_in[0], o_cw=v_o_c_w[0], o_out=v_o_w_out[0])
    w_small = dict(e_norm_pre=e_norm_pre, e_norm_post=e_norm_post, e_b_conv_bias=e_b_conv_bias, e_b_ln_g=e_b_ln_g,
                   e_b_ln_b=e_b_ln_b, e_a_conv=e_a_conv[0], e_b_conv=e_b_conv[0], o_norm_pre=o_norm_pre,
                   o_norm_post=o_norm_post, o_c_b=o_c_b[0], o_c_scale=o_c_scale)
    m_small = dict(e_norm_pre=m_e_norm_pre, e_norm_post=m_e_norm_post, e_b_conv_bias=m_e_b_conv_bias,
                   e_b_ln_g=m_e_b_ln_g, e_b_ln_b=m_e_b_ln_b, e_a_conv=m_e_a_conv[0], e_b_conv=m_e_b_conv[0],
                   o_norm_pre=m_o_norm_pre, o_norm_post=m_o_norm_post, o_c_b=m_o_c_b[0], o_c_scale=m_o_c_scale)
    v_small = dict(e_norm_pre=v_e_norm_pre, e_norm_post=v_e_norm_post, e_b_conv_bias=v_e_b_conv_bias,
                   e_b_ln_g=v_e_b_ln_g, e_b_ln_b=v_e_b_ln_b, e_a_conv=v_e_a_conv[0], e_b_conv=v_e_b_conv[0],
                   o_norm_pre=v_o_norm_pre, o_norm_post=v_o_norm_post, o_c_b=v_o_c_b[0], o_c_scale=v_o_c_scale)

    c_idx = jnp.reshape(ci, (1,)).astype(jnp.int32)
    order = jnp.stack([2 * xi + yi, 2 * (1 - xi) + yi, 2 * xi + (1 - yi), 2 * (1 - xi) + (1 - yi)]).astype(jnp.int32)
    ex = _Exchange({k: w_big[k].astype(BF16) for k in BIG}, _pack([w_small[k] for k in SHARDED]), order, c_idx)
    loss, grad_x, g_small = _local_step(x[0], loss_target[0], w_small, ex)

    big_out = {}
    for k in BIG:
        sums, received, q_idx = ex.reduced[k]
        big_out[k] = _adam_big(q_idx, sums, received, w_big[k], m_big[k], v_big[k], BIG_SPLIT[k], name="adam_" + k)[0]

    rep = _pack([g_small[k] for k in REPLICATED])
    loss_row = jnp.pad(jnp.reshape(loss, (1, 1)), ((0, 0), (0, LANES - 1)))
    blocks = []
    for k in SHARDED:
        r, n = w_small[k].shape
        blocks.append(g_small[k].reshape(r, N_DEV, n).transpose(1, 0, 2).reshape(N_DEV, r * n))
    blocks = jnp.concatenate(blocks, axis=1).reshape(N_DEV, -1, LANES)
    head = jnp.concatenate([rep, loss_row], axis=0)
    send = jnp.concatenate([jnp.broadcast_to(head[None], (N_DEV,) + head.shape), blocks], axis=1)
    parts = _run_comm(_small_scatter_comm(send), "small_grad_exchange")[0]

    def own_rows(d):
        return jnp.concatenate([_pack([d[k] for k in REPLICATED]), jnp.ones((1, LANES), F32),
                                _pack([d[k] for k in SHARDED])], axis=0)

    res_small = _adam_small(parts, own_rows(w_small), own_rows(m_small), own_rows(v_small), name="adam_small")
    n_rep = rep.shape[0]
    loss = res_small[0][n_rep, 0]
    small_out = {k: [] for k in SMALL}
    for packed in res_small:
        for k, t in zip(REPLICATED, _unpack(packed[:n_rep], [w_small[k].shape for k in REPLICATED])):
            small_out[k].append(t)
        for k, t in zip(SHARDED, _unpack(packed[n_rep + 1:], [w_small[k].shape for k in SHARDED])):
            small_out[k].append(t)

    big_of = dict(e_w_in="e_in", e_w_out="e_out", o_w_in="o_in", o_c_w="o_cw", o_w_out="o_out")
    stacked = ("e_a_conv", "e_b_conv", "o_c_b")

    def leaf(name, which):
        if name in big_of:
            return big_out[big_of[name]][which][None]
        t = small_out[name][which]
        return t[None] if name in stacked else t

    order = ("e_norm_pre", "e_norm_post", "e_w_in", "e_a_conv", "e_b_conv", "e_b_conv_bias", "e_b_ln_g", "e_b_ln_b",
             "e_w_out", "o_norm_pre", "o_norm_post", "o_w_in", "o_c_w", "o_c_b", "o_c_scale", "o_w_out")
    outs = [loss, grad_x[None]]
    for which in range(4):
        outs += [leaf(nm, which) for nm in order]
    return tuple(outs)
```

```python
import jax
import jax.numpy as jnp
from jax import lax
from jax.experimental import pallas as pl
from jax.experimental.pallas import tpu as pltpu

F32 = jnp.float32
BF16 = jnp.bfloat16
EPS = 1e-6
MESH = pl.DeviceIdType.MESH
ANY = pl.BlockSpec(memory_space=pl.ANY)

N_DEV = 8
HALO = 32
PHALO = 16
CONV_A = 3
CONV_B = 31
POOL_WINDOWS = (2, 4, 8, 16)
LANES = 128
MIB = 1024 * 1024

ADAM_LR = 0.001
ADAM_B1 = 0.9
ADAM_B2 = 0.999
ADAM_EPS = 1e-08
ADAM_WD = 0.01
ADAM_STEP = 10

TM_NT = 1024
TM_MIX = 256
TM_WIDE = 512


def _sds(shape, dtype):
    return jax.ShapeDtypeStruct(tuple(shape), dtype)


def _params(sem, vmem_mib):
    return pltpu.CompilerParams(dimension_semantics=sem, vmem_limit_bytes=vmem_mib * MIB)


def _const(shape, single=False):
    n = len(shape)
    if single:
        return pl.BlockSpec(shape, lambda *_: (0,) * n, pipeline_mode=pl.Buffered(1))
    return pl.BlockSpec(shape, lambda *_: (0,) * n)


def _sig(v):
    return jax.nn.sigmoid(v)


def _dsilu(v, s):
    return s * (1.0 + v * (1.0 - s))


def _rms(v):
    return lax.rsqrt(jnp.mean(v * v, axis=-1, keepdims=True) + EPS)


def _norm_bwd(dn, n, r):
    return r * (dn - n * jnp.mean(dn * n, axis=-1, keepdims=True))


def _colsum(v):
    return jnp.sum(v, axis=0, keepdims=True)


class _Comm:
    def __init__(self, inputs, out_shapes, sems, start, finish, aliases=None, middle=None):
        self.inputs, self.out_shapes, self.sems = list(inputs), list(out_shapes), list(sems)
        self.start, self.finish, self.middle = start, finish, middle
        self.aliases = dict(aliases or {})


def _merge(*comms):
    comms = [c for c in comms if c is not None]
    if len(comms) <= 1:
        return comms[0] if comms else None
    spans, i0, o0, s0, aliases = [], 0, 0, 0, {}
    for c in comms:
        spans.append((i0, o0, s0))
        aliases.update({i0 + k: o0 + v for k, v in c.aliases.items()})
        i0, o0, s0 = i0 + len(c.inputs), o0 + len(c.out_shapes), s0 + len(c.sems)

    def run(which):
        def fn(ins, outs, sems):
            for c, (i, o, s) in zip(comms, spans):
                hook = getattr(c, which)
                if hook is not None:
                    hook(ins[i:i + len(c.inputs)], outs[o:o + len(c.out_shapes)], sems[s:s + len(c.sems)])
        return fn

    return _Comm([a for c in comms for a in c.inputs], [a for c in comms for a in c.out_shapes],
                 [a for c in comms for a in c.sems], run("start"), run("finish"), aliases,
                 run("middle") if any(c.middle is not None for c in comms) else None)


def _call(body, *, grid, in_specs, out_specs, out_shape, operands, name, params, scratch_shapes=(), comm=None,
          prefetch=None, own_copies_first=False):
    n_p = 0 if prefetch is None else 1
    n_i, n_o, n_s = len(in_specs), len(out_specs), len(scratch_shapes)
    if comm is None:
        comm = _Comm([], [], [], None, None)
    c_i, c_o = len(comm.inputs), len(comm.out_shapes)

    def carrier(*refs):
        pre, refs = refs[:n_p], refs[n_p:]
        ins, cins = refs[:n_i], refs[n_i:n_i + c_i]
        outs = refs[n_i + c_i:n_i + c_i + n_o]
        couts = refs[n_i + c_i + n_o:n_i + c_i + n_o + c_o]
        scr = refs[n_i + c_i + n_o + c_o:n_i + c_i + n_o + c_o + n_s]
        csems = refs[n_i + c_i + n_o + c_o + n_s:]
        ids = [pl.program_id(d) for d in range(len(grid))]
        first = ids[0] == 0
        half = ids[0] == grid[0] // 2
        last = ids[0] == grid[0] - 1
        for d in range(1, len(grid)):
            first = first & (ids[d] == 0)
            half = half & (ids[d] == 0)
            last = last & (ids[d] == grid[d] - 1)

        def start():
            if comm.start is not None:
                @pl.when(first)
                def _():
                    comm.start(cins, couts, csems)

        if not own_copies_first:
            start()
        if comm.middle is not None:
            assert grid[0] >= 2

            @pl.when(half)
            def _():
                comm.middle(cins, couts, csems)

        body(*pre, *ins, *outs, *scr)
        if own_copies_first:
            start()

        if comm.finish is not None:
            @pl.when(last)
            def _():
                comm.finish(cins, couts, csems)

    specs = dict(grid=grid, in_specs=list(in_specs) + [ANY] * c_i, out_specs=list(out_specs) + [ANY] * c_o,
                 scratch_shapes=list(scratch_shapes) + comm.sems)
    if n_p:
        specs = dict(grid_spec=pltpu.PrefetchScalarGridSpec(num_scalar_prefetch=1, **specs))
    res = pl.pallas_call(
        carrier, out_shape=list(out_shape) + comm.out_shapes,
        input_output_aliases={n_p + n_i + k: n_o + v for k, v in comm.aliases.items()},
        name=name, compiler_params=params, **specs)(*(() if prefetch is None else (prefetch,)), *operands, *comm.inputs)
    return list(res[:n_o]), list(res[n_o:])


def _run_comm(comm, name):
    c_i, c_o = len(comm.inputs), len(comm.out_shapes)

    def body(*refs):
        ins, outs, sems = refs[:c_i], refs[c_i:c_i + c_o], refs[c_i + c_o:]
        comm.start(ins, outs, sems)
        comm.finish(ins, outs, sems)

    res = pl.pallas_call(
        body, in_specs=[ANY] * c_i, out_specs=[ANY] * c_o, out_shape=comm.out_shapes, scratch_shapes=comm.sems,
        input_output_aliases=comm.aliases, name=name)(*comm.inputs)
    return list(res)


def _gather_matmul(order, x, g, shard, *, tm, name, comm=None):
    S, K = x.shape
    nb = shard.shape[1]
    n_i = S // tm

    def body(order_ref, x_ref, g_ref, shard_ref, p_ref, h_ref, full_ref, hbuf, wbuf, stage, send_sems, recv_sems,
             dma_sems):
        j, i = pl.program_id(0), pl.program_id(1)
        px, py, pc = _place()
        cps = _gather_copies(stage, full_ref, 1, nb, send_sems, recv_sems, 0)
        own = pltpu.make_async_copy(stage, _piece(full_ref, 1, nb, 4 * px + 2 * py + pc), dma_sems.at[0])
        keep_h = pltpu.make_async_copy(hbuf, h_ref, dma_sems.at[2])

        def load(src, dst):
            cp = pltpu.make_async_copy(src, dst, dma_sems.at[1])
            cp.start()
            cp.wait()

        def load_pair(qx, qy):
            load(_piece(full_ref, 1, 2 * nb, 2 * qx + qy), wbuf)

        @pl.when((j == 0) & (i == 0))
        def _():
            load(shard_ref, stage)
            own.start()
            for k in (0, 1, 2):
                cps[k].start()

        @pl.when(j == 0)
        def _():
            xx = x_ref[...]
            hbuf[i] = ((xx * _rms(xx)) * g_ref[...]).astype(BF16)

        @pl.when((j == 0) & (i == 0))
        def _():
            own.wait()
            cps[0].wait_recv()
            load_pair(px, py)

        @pl.when((j == 1) & (i == 0))
        def _():
            keep_h.start()
            cps[1].wait_recv()
            cps[3].start()
            cps[5].start()
            cps[2].wait_recv()
            cps[4].start()
            cps[6].start()
            cps[5].wait_recv()
            load_pair(1 - px, py)

        @pl.when((j == 2) & (i == 0))
        def _():
            cps[6].wait_recv()
            load_pair(px, 1 - py)

        @pl.when((j == 3) & (i == 0))
        def _():
            cps[3].wait_recv()
            cps[4].wait_recv()
            cps[7].start()
            cps[7].wait_recv()
            load_pair(1 - px, 1 - py)

        p_ref[...] = jnp.dot(hbuf[i], wbuf[...], preferred_element_type=F32).astype(BF16)

        @pl.when((j == 3) & (i == n_i - 1))
        def _():
            for cp in cps:
                cp.wait_send()
            keep_h.wait()

    first_pass = lambda j, i, o: (jnp.where(j == 0, i, n_i - 1), 0)
    outs, extra = _call(
        body, grid=(4, n_i), prefetch=order,
        in_specs=[pl.BlockSpec((tm, K), first_pass), pl.BlockSpec((1, K), lambda j, i, o: (0, 0)), ANY],
        out_specs=[pl.BlockSpec((tm, 2 * nb), lambda j, i, o: (i, o[j])), ANY, ANY],
        out_shape=[_sds((S, N_DEV * nb), BF16), _sds((n_i, tm, K), BF16), _sds((K, N_DEV * nb), BF16)],
        operands=(x, g, shard),
        scratch_shapes=[pltpu.VMEM((n_i, tm, K), BF16), pltpu.VMEM((K, 2 * nb), BF16), pltpu.VMEM((K, nb), BF16),
                        pltpu.SemaphoreType.DMA((N_GATHER,)), pltpu.SemaphoreType.DMA((N_GATHER,)),
                        pltpu.SemaphoreType.DMA((3,))],
        name=name, params=_params(("arbitrary", "arbitrary"), 58), comm=comm, own_copies_first=True)
    return outs[0], outs[1].reshape(S, K), outs[2], extra


def _out_norm_res(u, w, x, g, *, tm, name, comm=None):
    S, K = u.shape
    D = w.shape[1]

    def body(u_ref, w_ref, x_ref, g_ref, x1_ref, y_ref):
        y = jnp.dot(u_ref[...], w_ref[...], preferred_element_type=F32)
        y_ref[...] = y.astype(BF16)
        x1_ref[...] = x_ref[...] + (y * _rms(y)) * g_ref[...]

    return _call(
        body, grid=(S // tm,),
        in_specs=[pl.BlockSpec((tm, K), lambda i: (i, 0)), _const((K, D), single=True),
                  pl.BlockSpec((tm, D), lambda i: (i, 0)), _const((1, D))],
        out_specs=[pl.BlockSpec((tm, D), lambda i: (i, 0)), pl.BlockSpec((tm, D), lambda i: (i, 0))],
        out_shape=[_sds((S, D), F32), _sds((S, D), BF16)], operands=(u, w, x, g),
        name=name, params=_params(("arbitrary",), 56), comm=comm)


def _out_loss(yy, w, x1, g, tgt, *, tm, name):
    S, K = yy.shape
    D = w.shape[1]

    def body(yy_ref, w_ref, x1_ref, g_ref, t_ref, dout_ref, dx2_ref, dyy_ref, lcol_ref, dg_ref):
        out = jnp.dot(yy_ref[...], w_ref[...], preferred_element_type=F32)
        r = _rms(out)
        n = out * r
        gg = g_ref[...]
        e = x1_ref[...] + n * gg - t_ref[...]
        dx2 = e * (1.0 / D)
        dx2_ref[...] = dx2
        dout = _norm_bwd(dx2 * gg, n, r).astype(BF16)
        dout_ref[...] = dout
        dyy_ref[...] = lax.dot_general(dout, w_ref[...], (((1,), (1,)), ((), ())),
                                       preferred_element_type=F32).astype(BF16)

        @pl.when(pl.program_id(0) == 0)
        def _():
            lcol_ref[...] = jnp.zeros_like(lcol_ref)
            dg_ref[...] = jnp.zeros_like(dg_ref)

        lcol_ref[...] += _colsum(e * e)
        dg_ref[...] += _colsum(dx2 * n)

    return _call(
        body, grid=(S // tm,),
        in_specs=[pl.BlockSpec((tm, K), lambda i: (i, 0)), _const((K, D), single=True),
                  pl.BlockSpec((tm, D), lambda i: (i, 0)), _const((1, D)),
                  pl.BlockSpec((tm, D), lambda i: (i, 0))],
        out_specs=[pl.BlockSpec((tm, D), lambda i: (i, 0)), pl.BlockSpec((tm, D), lambda i: (i, 0)),
                   pl.BlockSpec((tm, K), lambda i: (i, 0)), _const((1, D)), _const((1, D))],
        out_shape=[_sds((S, D), BF16), _sds((S, D), F32), _sds((S, K), BF16), _sds((1, D), F32), _sds((1, D), F32)],
        operands=(yy, w, x1, g, tgt), name=name, params=_params(("arbitrary",), 52))[0]


def _mm_nt(a, w, *, tm, tk, name, comm=None):
    S, N = a.shape
    D = w.shape[0]
    n_k = N // tk

    def body(a_ref, w_ref, o_ref, acc_ref):
        k = pl.program_id(1)

        @pl.when(k == 0)
        def _():
            acc_ref[...] = jnp.zeros_like(acc_ref)

        acc_ref[...] = lax.dot_general(a_ref[...], w_ref[...], (((1,), (1,)), ((), ())),
                                       preferred_element_type=F32) + acc_ref[...]

        @pl.when(k == n_k - 1)
        def _():
            o_ref[...] = acc_ref[...].astype(BF16)

    outs, extra = _call(
        body, grid=(S // tm, n_k),
        in_specs=[pl.BlockSpec((tm, tk), lambda i, k: (i, k)), pl.BlockSpec((D, tk), lambda i, k: (0, k))],
        out_specs=[pl.BlockSpec((tm, D), lambda i, k: (i, 0))],
        out_shape=[_sds((S, D), BF16)], operands=(a, w),
        scratch_shapes=[pltpu.VMEM((tm, D), F32)],
        name=name, params=_params(("arbitrary", "arbitrary"), 36), comm=comm)
    return outs[0], extra


def _mm_tn(a, b, *, ts, tn, name, comm=None):
    S, M = a.shape
    N = b.shape[1]
    n_s = S // ts

    def body(a_ref, b_ref, o_ref, acc_ref):
        s = pl.program_id(1)

        @pl.when(s == 0)
        def _():
            acc_ref[...] = jnp.zeros_like(acc_ref)

        acc_ref[...] = lax.dot_general(a_ref[...], b_ref[...], (((0,), (0,)), ((), ())),
                                       preferred_element_type=F32) + acc_ref[...]

        @pl.when(s == n_s - 1)
        def _():
            o_ref[...] = acc_ref[...].astype(BF16)

    outs, extra = _call(
        body, grid=(N // tn, n_s),
        in_specs=[pl.BlockSpec((ts, M), lambda j, s: (s, 0)), pl.BlockSpec((ts, tn), lambda j, s: (s, j))],
        out_specs=[pl.BlockSpec((M, tn), lambda j, s: (0, j))],
        out_shape=[_sds((M, N), BF16)], operands=(a, b),
        scratch_shapes=[pltpu.VMEM((M, tn), F32)],
        name=name, params=_params(("arbitrary", "arbitrary"), 36), comm=comm)
    return outs[0], extra


def _dw_reduce(order, a, b, nb, *, ts, name, comm=None):
    S, M = a.shape
    n_s = S // ts
    rows = 512

    def body(order_ref, a_ref, b_ref, sums_ref, from_sib_ref, from_chip_ref, acc, send_buf, mine_buf, recv_buf,
             sib_send, sib_recv, chip_send, chip_recv, dma_sems):
        t, s = pl.program_id(0), pl.program_id(1)
        x, y, c = _place()
        targets = [(1 - x, y, c), (x, 1 - y, c)]

        def to_sibling(k):
            return pltpu.make_async_remote_copy(
                src_ref=send_buf, dst_ref=from_sib_ref.at[k], send_sem=sib_send.at[k], recv_sem=sib_recv.at[k],
                device_id=(x, y, 1 - c), device_id_type=MESH)

        def to_chip(k):
            return pltpu.make_async_remote_copy(
                src_ref=sums_ref.at[k], dst_ref=from_chip_ref.at[k], send_sem=chip_send.at[k],
                recv_sem=chip_recv.at[k], device_id=targets[k], device_id_type=MESH)

        def finish(k):
            to_sibling(k).wait()
            get = pltpu.make_async_copy(from_sib_ref.at[k], recv_buf, dma_sems.at[0])
            get.start()
            get.wait()
            for r in range(0, M, rows):
                recv_buf[r:r + rows, :] = (mine_buf[r:r + rows, :].astype(F32)
                                           + recv_buf[r:r + rows, :].astype(F32)).astype(BF16)
            put = pltpu.make_async_copy(recv_buf, sums_ref.at[k], dma_sems.at[1])
            put.start()
            put.wait()
            if k < 2:
                to_chip(k).start()

        for k in range(3):
            @pl.when((t == k + 1) & (s == 0))
            def _(k=k):
                finish(k)

        @pl.when(s == 0)
        def _():
            acc[...] = jnp.zeros_like(acc)

        acc[...] = lax.dot_general(a_ref[...], b_ref[...], (((0,), (0,)), ((), ())),
                                   preferred_element_type=F32) + acc[...]

        @pl.when(s == n_s - 1)
        def _():
            for r in range(0, M, rows):
                lo, hi = acc[r:r + rows, :nb], acc[r:r + rows, nb:]
                send_buf[r:r + rows, :] = jnp.where(c == 0, hi, lo).astype(BF16)
                mine_buf[r:r + rows, :] = jnp.where(c == 0, lo, hi).astype(BF16)
            to_sibling(t).start()

        @pl.when((t == 3) & (s == n_s - 1))
        def _():
            finish(3)
            to_chip(0).wait()
            to_chip(1).wait()

    piece = _sds((4, M, nb), BF16)
    outs, extra = _call(
        body, grid=(4, n_s), prefetch=order,
        in_specs=[pl.BlockSpec((ts, M), lambda t, s, o: (s, 0)), pl.BlockSpec((ts, 2 * nb), lambda t, s, o: (s, o[t]))],
        out_specs=[ANY, ANY, ANY], out_shape=[piece, piece, _sds((3, M, nb), BF16)], operands=(a, b),
        scratch_shapes=[pltpu.VMEM((M, 2 * nb), F32), pltpu.VMEM((M, nb), BF16), pltpu.VMEM((M, nb), BF16),
                        pltpu.VMEM((M, nb), BF16), pltpu.SemaphoreType.DMA((4,)), pltpu.SemaphoreType.DMA((4,)),
                        pltpu.SemaphoreType.DMA((2,)), pltpu.SemaphoreType.DMA((2,)), pltpu.SemaphoreType.DMA((2,))],
        name=name, params=_params(("arbitrary", "arbitrary"), 56), comm=comm)
    return outs[0], outs[2], extra


def _diag_comm(sums, from_chip):
    def copy(ins, outs, sems):
        x, y, c = _place()
        return pltpu.make_async_remote_copy(
            src_ref=ins[0].at[2], dst_ref=outs[0].at[2], send_sem=sems[0].at[0], recv_sem=sems[1].at[0],
            device_id=(1 - x, 1 - y, c), device_id_type=MESH)

    def start(ins, outs, sems):
        copy(ins, outs, sems).start()

    def finish(ins, outs, sems):
        copy(ins, outs, sems).wait()

    sems = [pltpu.SemaphoreType.DMA((1,)), pltpu.SemaphoreType.DMA((1,))]
    return _Comm([sums, from_chip], [_sds(from_chip.shape, from_chip.dtype)], sems, start, finish, aliases={1: 0})


def _pre_bwd_o(dh, x1, dx2, y0, g_pre, g_post, *, tm, name, comm=None):
    S, D = x1.shape

    def body(dh_ref, x1_ref, dx2_ref, y0_ref, gpre_ref, gpost_ref, dx1_ref, dy0_ref, dgpre_ref, dgpost_ref):
        @pl.when(pl.program_id(0) == 0)
        def _():
            dgpre_ref[...] = jnp.zeros_like(dgpre_ref)
            dgpost_ref[...] = jnp.zeros_like(dgpost_ref)

        dh = dh_ref[...].astype(F32)
        x1 = x1_ref[...]
        r2 = _rms(x1)
        xn = x1 * r2
        dgpre_ref[...] += _colsum(dh * xn)
        dx1 = dx2_ref[...] + _norm_bwd(dh * gpre_ref[...], xn, r2)
        dx1_ref[...] = dx1
        y = y0_ref[...].astype(F32)
        r1 = _rms(y)
        n1 = y * r1
        dgpost_ref[...] += _colsum(dx1 * n1)
        dy0_ref[...] = _norm_bwd(dx1 * gpost_ref[...], n1, r1).astype(BF16)

    row = pl.BlockSpec((tm, D), lambda i: (i, 0))
    return _call(
        body, grid=(S // tm,),
        in_specs=[row, row, row, row, _const((1, D)), _const((1, D))],
        out_specs=[row, row, _const((1, D)), _const((1, D))],
        out_shape=[_sds((S, D), F32), _sds((S, D), BF16), _sds((1, D), F32), _sds((1, D), F32)],
        operands=(dh, x1, dx2, y0, g_pre, g_post),
        name=name, params=_params(("arbitrary",), 36), comm=comm)


def _pre_bwd_e(dh, x, dx1, g_pre, *, tm, name):
    S, D = x.shape

    def body(dh_ref, x_ref, dx1_ref, gpre_ref, gx_ref, dgpre_ref):
        @pl.when(pl.program_id(0) == 0)
        def _():
            dgpre_ref[...] = jnp.zeros_like(dgpre_ref)

        dh = dh_ref[...].astype(F32)
        xx = x_ref[...]
        r0 = _rms(xx)
        xn = xx * r0
        dgpre_ref[...] += _colsum(dh * xn)
        gx_ref[...] = dx1_ref[...] + _norm_bwd(dh * gpre_ref[...], xn, r0)

    row = pl.BlockSpec((tm, D), lambda i: (i, 0))
    return _call(
        body, grid=(S // tm,),
        in_specs=[row, row, row, _const((1, D))],
        out_specs=[row, _const((1, D))],
        out_shape=[_sds((S, D), F32), _sds((1, D), F32)],
        operands=(dh, x, dx1, g_pre), name=name, params=_params(("arbitrary",), 56))[0]


SUBLANES = 8


def _shift_copies(sh_ref, ext_ref, cs):
    for b in range(1, SUBLANES):
        sh_ref[b - 1] = ext_ref[pl.ds(b, sh_ref.shape[1]), cs]


def _rows_at(ext_ref, sh_ref, off, cs, tm):
    b = off % SUBLANES
    if b == 0 or sh_ref is None:
        return ext_ref[pl.ds(off, tm), cs]
    return sh_ref[b - 1, pl.ds(off - b, tm), :]


def _taps(ext_ref, w_ref, n_taps, base, cs, tm, sh_ref=None):
    acc = _rows_at(ext_ref, sh_ref, base, cs, tm) * w_ref[0:1, cs]
    for k in range(1, n_taps):
        acc = acc + _rows_at(ext_ref, sh_ref, base + k, cs, tm) * w_ref[k:k + 1, cs]
    return acc


def _taps_rev(ext_ref, w_ref, n_taps, cs, tm, sh_ref=None):
    acc = _rows_at(ext_ref, sh_ref, n_taps - 1, cs, tm) * w_ref[0:1, cs]
    for k in range(1, n_taps):
        acc = acc + _rows_at(ext_ref, sh_ref, n_taps - 1 - k, cs, tm) * w_ref[k:k + 1, cs]
    return acc


def _e_mix_fwd(p, wa, wb, bias, ln_g, ln_b, *, tm, name, comm=None):
    S = p.shape[0]
    W = p.shape[1] // 7
    nb = tm // HALO
    chunks = [slice(c * LANES, (c + 1) * LANES) for c in range(W // LANES)]

    def body(p_ref, hax_ref, hac_ref, hbv_ref, hbg_ref, wa_ref, wb_ref, bias_ref, lg_ref, lb_ref,
             u_ref, cb_ref, ext_ref, sh_ref):
        keep = (pl.program_id(0) > 0).astype(F32)
        col = lambda j, cs: p_ref[:, j * W + cs.start:j * W + cs.stop].astype(F32)

        ext_ref[0:HALO, :] = hax_ref[...].astype(F32) * hac_ref[...].astype(F32) * keep
        ext_ref[HALO:, :] = p_ref[:, 2 * W:3 * W].astype(F32) * p_ref[:, 0:W].astype(F32)
        for cs in chunks:
            conv = _taps(ext_ref, wa_ref, CONV_A, HALO - (CONV_A - 1), cs, tm)
            az = col(3, cs)
            u_ref[:, cs] = (col(1, cs) * conv * (az * _sig(az))).astype(BF16)

        ext_ref[0:HALO, :] = hbv_ref[...].astype(F32) * _sig(hbg_ref[...].astype(F32)) * keep
        ext_ref[HALO:, :] = p_ref[:, 4 * W:5 * W].astype(F32) * _sig(p_ref[:, 5 * W:6 * W].astype(F32))
        s1 = jnp.zeros((tm, LANES), F32)
        for cs in chunks:
            _shift_copies(sh_ref, ext_ref, cs)
            cb = _taps(ext_ref, wb_ref, CONV_B, HALO - (CONV_B - 1), cs, tm, sh_ref) + bias_ref[:, cs]
            cb_ref[:, cs] = cb
            s1 = s1 + cb
        mu = jnp.sum(s1, axis=-1, keepdims=True) * (1.0 / W)
        s2 = jnp.zeros((tm, LANES), F32)
        for cs in chunks:
            xc = cb_ref[:, cs] - mu
            s2 = s2 + xc * xc
        rs = lax.rsqrt(jnp.sum(s2, axis=-1, keepdims=True) * (1.0 / W) + EPS)
        for cs in chunks:
            lb = (cb_ref[:, cs] - mu) * rs * lg_ref[:, cs] + lb_ref[:, cs]
            bz = col(6, cs)
            u_ref[:, W + cs.start:W + cs.stop] = (lb * _sig(lb) * (bz * _sig(bz))).astype(BF16)

    prev = lambda j: pl.BlockSpec((HALO, W), lambda i: (jnp.maximum(i * nb - 1, 0), j))
    return _call(
        body, grid=(S // tm,),
        in_specs=[pl.BlockSpec((tm, 7 * W), lambda i: (i, 0)), prev(0), prev(2), prev(4), prev(5),
                  _const((CONV_A, W)), _const((CONV_B, W)), _const((1, W)), _const((1, W)), _const((1, W))],
        out_specs=[pl.BlockSpec((tm, 2 * W), lambda i: (i, 0)), pl.BlockSpec((tm, W), lambda i: (i, 0))],
        out_shape=[_sds((S, 2 * W), BF16), _sds((S, W), F32)],
        operands=(p, p, p, p, p, wa, wb, bias, ln_g, ln_b),
        scratch_shapes=[pltpu.VMEM((HALO + tm, W), F32),
                        pltpu.VMEM((SUBLANES - 1, HALO + tm - SUBLANES, LANES), F32)],
        name=name, params=_params(("arbitrary",), 36), comm=comm)


def _e_mix_bwd(du, p, cb, wa, wb, ln_g, ln_b, *, tm, name, comm=None):
    S = p.shape[0]
    W = p.shape[1] // 7
    nb = tm // HALO
    n_t = S // tm
    last_blk = S // HALO - 1
    chunks = [slice(c * LANES, (c + 1) * LANES) for c in range(W // LANES)]

    def body(du_ref, duf_ref, p_ref, fab_ref, faz_ref, fbz_ref, hax_ref, hac_ref, hbv_ref, hbg_ref,
             cb_ref, cbf_ref, wa_ref, wb_ref, lg_ref, lb_ref,
             dp_ref, dwa_ref, dwb_ref, dbias_ref, dlg_ref, dlb_ref, extd_ref, extg_ref, shd_ref, shg_ref):
        i = pl.program_id(0)
        keep_prev = (i > 0).astype(F32)
        keep_next = (i < n_t - 1).astype(F32)
        col = lambda j, cs: p_ref[:, j * W + cs.start:j * W + cs.stop].astype(F32)

        @pl.when(i == 0)
        def _():
            dwa_ref[...] = jnp.zeros_like(dwa_ref)
            dwb_ref[...] = jnp.zeros_like(dwb_ref)
            dbias_ref[...] = jnp.zeros_like(dbias_ref)
            dlg_ref[...] = jnp.zeros_like(dlg_ref)
            dlb_ref[...] = jnp.zeros_like(dlb_ref)

        def dcb_rows(rows, cb_rows_ref, dub, bz_of, dst0, scale, main):
            cbv = cb_rows_ref[...]
            mu = jnp.mean(cbv, axis=-1, keepdims=True)
            xc = cbv - mu
            rs = lax.rsqrt(jnp.mean(xc * xc, axis=-1, keepdims=True) + EPS)
            m1 = jnp.zeros((rows, LANES), F32)
            m2 = jnp.zeros((rows, LANES), F32)
            for cs in chunks:
                nbv = (cb_rows_ref[:, cs] - mu) * rs
                lb = nbv * lg_ref[:, cs] + lb_ref[:, cs]
                sl = _sig(lb)
                bz = bz_of(cs)
                sz = _sig(bz)
                dub_c = dub(cs)
                dlb = dub_c * (bz * sz) * _dsilu(lb, sl)
                if main:
                    dlg_ref[:, cs] += _colsum(dlb * nbv)
                    dlb_ref[:, cs] += _colsum(dlb)
                    dp_ref[:, 6 * W + cs.start:6 * W + cs.stop] = (dub_c * (lb * sl) * _dsilu(bz, sz)).astype(BF16)
                dnb = dlb * lg_ref[:, cs]
                extd_ref[dst0:dst0 + rows, cs] = dnb
                m1 = m1 + dnb
                m2 = m2 + dnb * nbv
            m1 = jnp.sum(m1, axis=-1, keepdims=True) * (1.0 / W)
            m2 = jnp.sum(m2, axis=-1, keepdims=True) * (1.0 / W)
            for cs in chunks:
                nbv = (cb_rows_ref[:, cs] - mu) * rs
                dcb = rs * (extd_ref[dst0:dst0 + rows, cs] - m1 - nbv * m2) * scale
                extd_ref[dst0:dst0 + rows, cs] = dcb
                if main:
                    dbias_ref[:, cs] += _colsum(dcb)

        dcb_rows(tm, cb_ref, lambda cs: du_ref[:, W + cs.start:W + cs.stop].astype(F32),
                 lambda cs: col(6, cs), 0, 1.0, True)
        dcb_rows(HALO, cbf_ref, lambda cs: duf_ref[:, W + cs.start:W + cs.stop].astype(F32),
                 lambda cs: fbz_ref[:, cs].astype(F32), tm, keep_next, False)

        extg_ref[0:HALO, :] = hbv_ref[...].astype(F32) * _sig(hbg_ref[...].astype(F32)) * keep_prev
        extg_ref[HALO:, :] = p_ref[:, 4 * W:5 * W].astype(F32) * _sig(p_ref[:, 5 * W:6 * W].astype(F32))
        base_b = HALO - (CONV_B - 1)
        for cs in chunks:
            _shift_copies(shd_ref, extd_ref, cs)
            _shift_copies(shg_ref, extg_ref, cs)
            dgb = _taps_rev(extd_ref, wb_ref, CONV_B, cs, tm, shd_ref)
            bv = col(4, cs)
            sg = _sig(col(5, cs))
            dp_ref[:, 4 * W + cs.start:4 * W + cs.stop] = (dgb * sg).astype(BF16)
            dp_ref[:, 5 * W + cs.start:5 * W + cs.stop] = (dgb * bv * sg * (1.0 - sg)).astype(BF16)
            dcb = extd_ref[0:tm, cs]
            for k in range(CONV_B):
                dwb_ref[k:k + 1, cs] += _colsum(dcb * _rows_at(extg_ref, shg_ref, base_b + k, cs, tm))

        extg_ref[0:HALO, :] = hax_ref[...].astype(F32) * hac_ref[...].astype(F32) * keep_prev
        extg_ref[HALO:, :] = p_ref[:, 2 * W:3 * W].astype(F32) * p_ref[:, 0:W].astype(F32)
        base_a = HALO - (CONV_A - 1)
        for cs in chunks:
            conv = _taps(extg_ref, wa_ref, CONV_A, base_a, cs, tm)
            az = col(3, cs)
            sz = _sig(az)
            ab = col(1, cs)
            dua = du_ref[:, cs].astype(F32)
            dya = dua * (az * sz)
            dp_ref[:, W + cs.start:W + cs.stop] = (dya * conv).astype(BF16)
            dp_ref[:, 3 * W + cs.start:3 * W + cs.stop] = (dua * (ab * conv) * _dsilu(az, sz)).astype(BF16)
            extd_ref[0:tm, cs] = dya * ab
            azf = faz_ref[:, cs].astype(F32)
            extd_ref[tm:tm + HALO, cs] = (duf_ref[:, cs].astype(F32) * (azf * _sig(azf))
                                          * fab_ref[:, cs].astype(F32) * keep_next)
        for cs in chunks:
            dca = _taps_rev(extd_ref, wa_ref, CONV_A, cs, tm)
            dp_ref[:, cs] = (dca * col(2, cs)).astype(BF16)
            dp_ref[:, 2 * W + cs.start:2 * W + cs.stop] = (dca * col(0, cs)).astype(BF16)
            dconv = extd_ref[0:tm, cs]
            for k in range(CONV_A):
                dwa_ref[k:k + 1, cs] += _colsum(dconv * extg_ref[pl.ds(base_a + k, tm), cs])

    prev = lambda j: pl.BlockSpec((HALO, W), lambda i: (jnp.maximum(i * nb - 1, 0), j))
    nxt = lambda j, w: pl.BlockSpec((HALO, w), lambda i: (jnp.minimum((i + 1) * nb, last_blk), j))
    row = lambda w: pl.BlockSpec((tm, w), lambda i: (i, 0))
    return _call(
        body, grid=(n_t,),
        in_specs=[row(2 * W), nxt(0, 2 * W), row(7 * W), nxt(1, W), nxt(3, W), nxt(6, W),
                  prev(0), prev(2), prev(4), prev(5), row(W), nxt(0, W),
                  _const((CONV_A, W)), _const((CONV_B, W)), _const((1, W)), _const((1, W))],
        out_specs=[row(7 * W), _const((CONV_A, W)), _const((CONV_B, W)), _const((1, W)), _const((1, W)), _const((1, W))],
        out_shape=[_sds((S, 7 * W), BF16), _sds((CONV_A, W), F32), _sds((CONV_B, W), F32),
                   _sds((1, W), F32), _sds((1, W), F32), _sds((1, W), F32)],
        operands=(du, du, p, p, p, p, p, p, p, p, cb, cb, wa, wb, ln_g, ln_b),
        scratch_shapes=[pltpu.VMEM((tm + HALO, W), F32), pltpu.VMEM((HALO + tm, W), F32),
                        pltpu.VMEM((SUBLANES - 1, HALO + tm - SUBLANES, LANES), F32),
                        pltpu.VMEM((SUBLANES - 1, HALO + tm - SUBLANES, LANES), F32)],
        name=name, params=_params(("arbitrary",), 40), comm=comm)


def _counts(i, tm, rows, off, win):
    t = i * tm + off + lax.broadcasted_iota(jnp.int32, (rows, 1), 0)
    return jnp.minimum(t + 1, win).astype(F32)


def _o_mix_fwd(q, cw, cb, cscale, *, tm, name):
    S = q.shape[0]
    WC = q.shape[1] // 2
    NG = len(POOL_WINDOWS)
    G = WC // NG
    nb = tm // PHALO

    def body(v_ref, z_ref, hv_ref, cw_ref, cb_ref, sc_ref, yy_ref, pooled_ref, gg_ref, ext_ref):
        i = pl.program_id(0)
        keep = (i > 0).astype(F32)
        for g, win in enumerate(POOL_WINDOWS):
            cs = slice(g * G, (g + 1) * G)
            v = v_ref[:, cs].astype(F32)
            ext_ref[0:PHALO, :] = hv_ref[:, cs].astype(F32) * keep
            ext_ref[PHALO:, :] = v
            s = v
            for j in range(1, win):
                s = s + ext_ref[pl.ds(PHALO - j, tm), :]
            pooled = (s / _counts(i, tm, tm, 0, win) - v).astype(BF16)
            pooled_ref[:, cs] = pooled
            gg = jnp.dot(pooled, cw_ref[g], preferred_element_type=F32) + cb_ref[:, cs]
            gg_ref[:, cs] = gg.astype(BF16)
            z = z_ref[:, cs].astype(F32)
            yy_ref[:, cs] = (gg * sc_ref[:, cs] * (z * _sig(z))).astype(BF16)

    row = lambda j: pl.BlockSpec((tm, WC), lambda i: (i, j))
    out = pl.BlockSpec((tm, WC), lambda i: (i, 0))
    return _call(
        body, grid=(S // tm,),
        in_specs=[row(0), row(1), pl.BlockSpec((PHALO, WC), lambda i: (jnp.maximum(i * nb - 1, 0), 0)),
                  _const((NG, G, G)), _const((1, WC)), _const((1, WC))],
        out_specs=[out, out, out],
        out_shape=[_sds((S, WC), BF16)] * 3, operands=(q, q, q, cw, cb, cscale),
        scratch_shapes=[pltpu.VMEM((PHALO + tm, G), F32)],
        name=name, params=_params(("arbitrary",), 40))[0]


def _o_mix_bwd(dyy, q, gg, pooled, cw, cscale, *, tm, name):
    S = q.shape[0]
    WC = q.shape[1] // 2
    NG = len(POOL_WINDOWS)
    G = WC // NG
    nb = tm // PHALO
    n_t = S // tm
    last_blk = S // PHALO - 1
    nt = (((1,), (1,)), ((), ()))
    tn = (((0,), (0,)), ((), ()))

    def body(dyy_ref, dyyf_ref, z_ref, zf_ref, gg_ref, pooled_ref, cw_ref, sc_ref,
             dq_ref, dcw_ref, dcb_ref, dsc_ref, ext_ref):
        i = pl.program_id(0)
        keep_next = (i < n_t - 1).astype(F32)

        @pl.when(i == 0)
        def _():
            dcw_ref[...] = jnp.zeros_like(dcw_ref)
            dcb_ref[...] = jnp.zeros_like(dcb_ref)
            dsc_ref[...] = jnp.zeros_like(dsc_ref)

        for g, win in enumerate(POOL_WINDOWS):
            cs = slice(g * G, (g + 1) * G)
            sc = sc_ref[:, cs]
            z = z_ref[:, cs].astype(F32)
            sz = _sig(z)
            dyy_c = dyy_ref[:, cs].astype(F32)
            ggv = gg_ref[:, cs].astype(F32)
            dyy0 = dyy_c * (z * sz)
            dq_ref[:, WC + cs.start:WC + cs.stop] = (dyy_c * (ggv * sc) * _dsilu(z, sz)).astype(BF16)
            dgg = dyy0 * sc
            dsc_ref[:, cs] += _colsum(dyy0 * ggv)
            dcb_ref[:, cs] += _colsum(dgg)
            dgg_b = dgg.astype(BF16)
            dcw_ref[g] += lax.dot_general(pooled_ref[:, cs], dgg_b, tn, preferred_element_type=F32)
            dpool = lax.dot_general(dgg_b, cw_ref[g], nt, preferred_element_type=F32)
            zf = zf_ref[:, cs].astype(F32)
            dgg_f = (dyyf_ref[:, cs].astype(F32) * (zf * _sig(zf)) * sc * keep_next).astype(BF16)
            dpool_f = lax.dot_general(dgg_f, cw_ref[g], nt, preferred_element_type=F32)
            ext_ref[0:tm, :] = dpool / _counts(i, tm, tm, 0, win)
            ext_ref[tm:tm + PHALO, :] = dpool_f / _counts(i, tm, PHALO, tm, win)
            dv = ext_ref[0:tm, :] - dpool
            for j in range(1, win):
                dv = dv + ext_ref[pl.ds(j, tm), :]
            dq_ref[:, cs] = dv.astype(BF16)

    row = lambda: pl.BlockSpec((tm, WC), lambda i: (i, 0))
    nxt = lambda j: pl.BlockSpec((PHALO, WC), lambda i: (jnp.minimum((i + 1) * nb, last_blk), j))
    return _call(
        body, grid=(n_t,),
        in_specs=[row(), nxt(0), pl.BlockSpec((tm, WC), lambda i: (i, 1)), nxt(1), row(), row(),
                  _const((NG, G, G)), _const((1, WC))],
        out_specs=[pl.BlockSpec((tm, 2 * WC), lambda i: (i, 0)), _const((NG, G, G)), _const((1, WC)), _const((1, WC))],
        out_shape=[_sds((S, 2 * WC), BF16), _sds((NG, G, G), F32), _sds((1, WC), F32), _sds((1, WC), F32)],
        operands=(dyy, dyy, q, q, gg, pooled, cw, cscale),
        scratch_shapes=[pltpu.VMEM((tm + PHALO, G), F32)],
        name=name, params=_params(("arbitrary",), 40))[0]


def _place():
    return lax.axis_index("x"), lax.axis_index("y"), lax.axis_index("c")


def _piece(ref, axis, size, index):
    start = index * size
    if axis == len(ref.shape) - 1:
        start = pl.multiple_of(start, LANES)
    idx = [slice(None)] * len(ref.shape)
    idx[axis] = pl.ds(start, size)
    return ref.at[tuple(idx)]


def _gather_copies(src, out, axis, size, send_sems, recv_sems, base, held=None):
    x, y, c = _place()
    sib, xn, yn = (x, y, 1 - c), (1 - x, y, c), (x, 1 - y, c)

    def blk(px, py, of=out):
        return _piece(of, axis, size, 4 * px + 2 * py + c)

    def half(ref, h):
        n = ref.shape[0] // 2
        return ref.at[pl.ds(h * n, n)]

    def rc(k, s, d, to):
        return pltpu.make_async_remote_copy(src_ref=s, dst_ref=d, send_sem=send_sems.at[base + k],
                                            recv_sem=recv_sems.at[base + k], device_id=to, device_id_type=MESH)

    own, xb, yb, db = blk(x, y), blk(1 - x, y), blk(x, 1 - y), blk(1 - x, 1 - y)
    got = out if held is None else held
    xs, ys, ds = blk(1 - x, y, got), blk(x, 1 - y, got), blk(1 - x, 1 - y, got)
    return [rc(0, src, own, sib), rc(1, src, own, xn), rc(2, src, own, yn),
            rc(3, half(xs, 0), half(xb, 0), yn), rc(4, half(ys, 1), half(yb, 1), xn),
            rc(5, xs, xb, sib), rc(6, ys, yb, sib), rc(7, ds, db, sib)]


N_GATHER = 8


def _gather_comm(shards, axes, phases):
    n = len(shards)
    if phases == "second":
        sizes = [s.shape[a] // N_DEV for s, a in zip(shards, axes)]
        full = [_sds(s.shape, s.dtype) for s in shards]
    else:
        sizes = [s.shape[a] for s, a in zip(shards, axes)]
        full = [_sds(s.shape[:a] + (N_DEV * s.shape[a],) + s.shape[a + 1:], s.dtype) for s, a in zip(shards, axes)]

    def plan(ins, outs, sems):
        x, y, c = _place()
        me = 4 * x + 2 * y + c
        if phases == "second":
            cps = [_gather_copies(_piece(ins[t], axes[t], sizes[t], me), outs[t], axes[t], sizes[t], sems[0], sems[1],
                                  N_GATHER * t, ins[t]) for t in range(n)]
        else:
            cps = [_gather_copies(sems[3 + t], outs[t], axes[t], sizes[t], sems[0], sems[1], N_GATHER * t)
                   for t in range(n)]
        mine = [pltpu.make_async_copy(sems[3 + t], _piece(outs[t], axes[t], sizes[t], me), sems[2].at[t])
                for t in range(n)] if phases != "second" else []
        return cps, mine

    def send_own(ins, outs, sems):
        cps, mine = plan(ins, outs, sems)
        for t in range(n):
            stage = pltpu.make_async_copy(ins[t], sems[3 + t], sems[2].at[t])
            stage.start()
            stage.wait()
            mine[t].start()
            for k in (0, 1, 2):
                cps[t][k].start()

    def pass_on(ins, outs, sems):
        cps, _ = plan(ins, outs, sems)
        for t in range(n):
            if phases == "all":
                cps[t][1].wait_recv()
            cps[t][3].start()
            cps[t][5].start()
        for t in range(n):
            if phases == "all":
                cps[t][2].wait_recv()
            cps[t][4].start()
            cps[t][6].start()

    def own_landed(ins, outs, sems):
        cps, mine = plan(ins, outs, sems)
        for t in range(n):
            for k in (0, 1, 2):
                cps[t][k].wait()
            mine[t].wait()

    def all_landed(ins, outs, sems):
        cps, mine = plan(ins, outs, sems)
        for t in range(n):
            cps[t][3].wait_recv()
            cps[t][4].wait_recv()
            cps[t][7].start()
        for t in range(n):
            for k in ((0, 5, 6, 7) if phases == "all" else (5, 6, 7)):
                cps[t][k].wait_recv()
            for k in (range(N_GATHER) if phases == "all" else range(3, N_GATHER)):
                cps[t][k].wait_send()
            if phases == "all":
                mine[t].wait()

    sems = [pltpu.SemaphoreType.DMA((N_GATHER * n,)), pltpu.SemaphoreType.DMA((N_GATHER * n,))]
    if phases != "second":
        sems.append(pltpu.SemaphoreType.DMA((n,)))
        sems += [pltpu.VMEM(s.shape, s.dtype) for s in shards]
    if phases == "all":
        return _Comm(shards, full, sems, send_own, all_landed, middle=pass_on)
    if phases == "first":
        return _Comm(shards, full, sems, send_own, own_landed)
    return _Comm(shards, full, sems, pass_on, all_landed, aliases={t: t for t in range(n)})


def _pair_comm(grads, axes, sizes):
    n = len(grads)
    outs_sds = [_sds((4,) + g.shape[:a] + (s,) + g.shape[a + 1:], g.dtype) for g, a, s in zip(grads, axes, sizes)]

    def copies(ins, outs, sems):
        send_sems, recv_sems = sems
        x, y, c = _place()
        return [pltpu.make_async_remote_copy(
            src_ref=_piece(ins[t], axes[t], sizes[t], 2 * qi + (1 - c)), dst_ref=outs[t].at[qi],
            send_sem=send_sems.at[4 * t + qi], recv_sem=recv_sems.at[4 * t + qi],
            device_id=(x, y, 1 - c), device_id_type=MESH) for t in range(n) for qi in range(4)]

    def start(ins, outs, sems):
        for cp in copies(ins, outs, sems):
            cp.start()

    def finish(ins, outs, sems):
        for cp in copies(ins, outs, sems):
            cp.wait()

    sems = [pltpu.SemaphoreType.DMA((4 * n,)), pltpu.SemaphoreType.DMA((4 * n,))]
    return _Comm(grads, outs_sds, sems, start, finish)


def _chip_comm(sums):
    n = len(sums)
    outs_sds = [_sds((3,) + s.shape[1:], s.dtype) for s in sums]

    def copies(ins, outs, sems):
        send_sems, recv_sems = sems
        x, y, c = _place()
        return [pltpu.make_async_remote_copy(
            src_ref=ins[t].at[2 * qx + qy], dst_ref=outs[t].at[j],
            send_sem=send_sems.at[3 * t + j], recv_sem=recv_sems.at[3 * t + j],
            device_id=(qx, qy, c), device_id_type=MESH)
            for t in range(n) for j, (qx, qy) in enumerate([(1 - x, y), (x, 1 - y), (1 - x, 1 - y)])]

    def start(ins, outs, sems):
        for cp in copies(ins, outs, sems):
            cp.start()

    def finish(ins, outs, sems):
        for cp in copies(ins, outs, sems):
            cp.wait()

    sems = [pltpu.SemaphoreType.DMA((3 * n,)), pltpu.SemaphoreType.DMA((3 * n,))]
    return _Comm(sums, outs_sds, sems, start, finish)


def _small_comm(small):
    def copies(ins, outs, sems):
        send_sems, recv_sems, local_sem = sems
        x, y, c = _place()
        mine = outs[0].at[4 * x + 2 * y + c]
        out = [pltpu.make_async_copy(ins[0], mine, local_sem.at[0])]
        for k in range(1, N_DEV):
            peer = (1 - x if k & 4 else x, 1 - y if k & 2 else y, 1 - c if k & 1 else c)
            out.append(pltpu.make_async_remote_copy(
                src_ref=ins[0], dst_ref=mine, send_sem=send_sems.at[k - 1], recv_sem=recv_sems.at[k - 1],
                device_id=peer, device_id_type=MESH))
        return out

    def start(ins, outs, sems):
        for cp in copies(ins, outs, sems):
            cp.start()

    def finish(ins, outs, sems):
        for cp in copies(ins, outs, sems):
            cp.wait()

    sems = [pltpu.SemaphoreType.DMA((N_DEV - 1,)), pltpu.SemaphoreType.DMA((N_DEV - 1,)), pltpu.SemaphoreType.DMA((1,))]
    return _Comm([small], [_sds((N_DEV,) + small.shape, small.dtype)], sems, start, finish)


def _small_scatter_comm(send):
    def copies(ins, outs, sems):
        send_sems, recv_sems, local_sem = sems
        x, y, c = _place()
        me = 4 * x + 2 * y + c
        out = [pltpu.make_async_copy(ins[0].at[me], outs[0].at[me], local_sem.at[0])]
        for k in range(1, N_DEV):
            px, py, pc = (1 - x if k & 4 else x, 1 - y if k & 2 else y, 1 - c if k & 1 else c)
            out.append(pltpu.make_async_remote_copy(
                src_ref=ins[0].at[4 * px + 2 * py + pc], dst_ref=outs[0].at[me], send_sem=send_sems.at[k - 1],
                recv_sem=recv_sems.at[k - 1], device_id=(px, py, pc), device_id_type=MESH))
        return out

    def start(ins, outs, sems):
        for cp in copies(ins, outs, sems):
            cp.start()

    def finish(ins, outs, sems):
        for cp in copies(ins, outs, sems):
            cp.wait()

    sems = [pltpu.SemaphoreType.DMA((N_DEV - 1,)), pltpu.SemaphoreType.DMA((N_DEV - 1,)), pltpu.SemaphoreType.DMA((1,))]
    return _Comm([send], [_sds(send.shape, send.dtype)], sems, start, finish)


def _pair_sum(c_idx, grad, recv, axis, size, split, *, name):
    nd = len(grad.shape)
    piece = grad.shape[:axis] + (size,) + grad.shape[axis + 1:]
    blk = (piece[0] // split,) + piece[1:]

    def g_map(q, r, c_ref):
        idx = [0] * nd
        idx[axis] = 2 * q + c_ref[0]
        idx[0] = idx[0] * split + r if axis == 0 else r
        return tuple(idx)

    def r_map(q, r, c_ref):
        return (q, r) + (0,) * (nd - 1)

    def body(c_ref, g_ref, r_ref, o_ref):
        o_ref[0] = (g_ref[...].astype(F32) + r_ref[0].astype(F32)).astype(BF16)

    return _call(
        body, grid=(4, split), prefetch=c_idx,
        in_specs=[pl.BlockSpec(blk, g_map), pl.BlockSpec((1,) + blk, r_map)],
        out_specs=[pl.BlockSpec((1,) + blk, r_map)], out_shape=[_sds((4,) + piece, BF16)],
        operands=(grad, recv), name=name, params=_params(("arbitrary", "arbitrary"), 32))[0][0]


def _adam_math(w, g, m, v):
    m = ADAM_B1 * m + (1.0 - ADAM_B1) * g
    v = ADAM_B2 * v + (1.0 - ADAM_B2) * (g * g)
    m_hat = m / (1.0 - ADAM_B1 ** ADAM_STEP)
    v_hat = v / (1.0 - ADAM_B2 ** ADAM_STEP)
    delta = -ADAM_LR * (m_hat / (jnp.sqrt(v_hat) + ADAM_EPS) + ADAM_WD * w)
    return delta, m, v


def _adam_big(q_idx, sums, recv, w, m, v, split, *, name, comm=None):
    shape = w.shape
    nd = len(shape)
    blk = (shape[0] // split,) + shape[1:]
    w_map = lambda r, q_ref: (r,) + (0,) * (nd - 1)
    s_map = lambda r, q_ref: (q_ref[0], r) + (0,) * (nd - 1)
    r_map = lambda r, q_ref: (0, r) + (0,) * (nd - 1)

    def body(q_ref, s_ref, r_ref, w_ref, m_ref, v_ref, g_ref, d_ref, nm_ref, nv_ref):
        g = s_ref[0].astype(F32) + r_ref[0].astype(F32) + r_ref[1].astype(F32) + r_ref[2].astype(F32)
        g_ref[...] = g
        d_ref[...], nm_ref[...], nv_ref[...] = _adam_math(w_ref[...], g, m_ref[...], v_ref[...])

    wspec = pl.BlockSpec(blk, w_map)
    return _call(
        body, grid=(split,), prefetch=q_idx,
        in_specs=[pl.BlockSpec((1,) + blk, s_map), pl.BlockSpec((3,) + blk, r_map), wspec, wspec, wspec],
        out_specs=[wspec] * 4, out_shape=[_sds(shape, F32)] * 4, operands=(sums, recv, w, m, v),
        name=name, params=_params(("arbitrary",), 32), comm=comm)


def _adam_small(parts, w, m, v, *, name):
    R = w.shape[0]

    def body(p_ref, w_ref, m_ref, v_ref, g_ref, d_ref, nm_ref, nv_ref):
        g = p_ref[0]
        for d in range(1, N_DEV):
            g = g + p_ref[d]
        g_ref[...] = g
        d_ref[...], nm_ref[...], nv_ref[...] = _adam_math(w_ref[...], g, m_ref[...], v_ref[...])

    whole = _const((R, LANES))
    return _call(
        body, grid=(1,), in_specs=[_const((N_DEV, R, LANES)), whole, whole, whole], out_specs=[whole] * 4,
        out_shape=[_sds((R, LANES), F32)] * 4, operands=(parts, w, m, v), name=name,
        params=_params(("arbitrary",), 32))[0]


def _pack(arrs):
    return jnp.concatenate([a.reshape(-1) for a in arrs]).reshape(-1, LANES)


def _unpack(packed, shapes):
    flat = packed.reshape(-1)
    out, off = [], 0
    for s in shapes:
        n = 1
        for d in s:
            n *= d
        out.append(flat[off:off + n].reshape(s))
        off += n
    return out


BIG = ("e_in", "e_out", "o_in", "o_cw", "o_out")
BIG_AXIS = dict(e_in=1, e_out=0, o_in=1, o_cw=1, o_out=0)
BIG_SPLIT = dict(e_in=8, e_out=4, o_in=4, o_cw=4, o_out=4)
REPLICATED = ("e_norm_pre", "e_norm_post", "e_b_conv_bias", "e_b_ln_g", "e_b_ln_b")
SHARDED = ("e_a_conv", "e_b_conv", "o_norm_pre", "o_norm_post", "o_c_b", "o_c_scale")
SMALL = REPLICATED + SHARDED


class _Exchange:
    def __init__(self, shards, small, order, c_idx):
        self.q_idx = order[:1]
        self.shards = shards
        self.small = small
        self.order = order
        self.c_idx = c_idx
        self.reduced = {}

    def gather(self, keys):
        return _gather_comm([self.shards[k] for k in keys], [BIG_AXIS[k] for k in keys], "all")

    def gather1(self, keys):
        return _gather_comm([self.shards[k] for k in keys], [BIG_AXIS[k] for k in keys], "first")

    def gather2(self, keys, firsts):
        return _gather_comm(firsts, [BIG_AXIS[k] for k in keys], "second")

    def pair(self, grads):
        keys = list(grads)
        return _pair_comm([grads[k] for k in keys], [BIG_AXIS[k] for k in keys],
                          [grads[k].shape[BIG_AXIS[k]] // N_DEV for k in keys])

    def pair_sums(self, grads, received):
        return {k: _pair_sum(self.c_idx, grads[k], r, BIG_AXIS[k], grads[k].shape[BIG_AXIS[k]] // N_DEV,
                             BIG_SPLIT[k], name="pair_sum_" + k) for k, r in zip(grads, received)}

    def chips(self, sums):
        return _chip_comm([sums[k] for k in sums])

    def done(self, sums, received):
        self.reduced.update({k: (sums[k], r, self.q_idx) for k, r in zip(sums, received)})


def _local_step(x, tgt, w_small, ex):
    S, D = x.shape
    tnt, tx, tw = min(TM_NT, S), min(TM_MIX, S), min(TM_WIDE, S)

    wt = {}
    p, h0, wt["e_in"], got = _gather_matmul(ex.order, x, w_small["e_norm_pre"], ex.shards["e_in"], tm=tw,
                                            name="e_in_fwd", comm=_merge(_small_comm(ex.small), ex.gather(["e_out"])))
    wt["e_out"] = got[1]
    per_dev = [_unpack(got[0][d], [w_small[k].shape for k in SHARDED]) for d in range(N_DEV)]
    sm = {k: w_small[k] for k in REPLICATED}
    for j, k in enumerate(SHARDED):
        sm[k] = jnp.concatenate([per_dev[d][j] for d in range(N_DEV)], axis=-1)
    n_groups = sm["o_c_b"].shape[0]
    sm["o_c_b"] = sm["o_c_b"].reshape(1, -1)

    W = p.shape[1] // 7
    (u, cb), _ = _e_mix_fwd(p, sm["e_a_conv"], sm["e_b_conv"], sm["e_b_conv_bias"], sm["e_b_ln_g"],
                            sm["e_b_ln_b"], tm=tx, name="e_mix_fwd")
    late = ["o_out", "o_cw"]
    (x1, y0), part = _out_norm_res(u, wt["e_out"], x, sm["e_norm_post"], tm=tw, name="e_out_fwd",
                                   comm=ex.gather1(late))
    q, h1, wt["o_in"], got = _gather_matmul(ex.order, x1, sm["o_norm_pre"], ex.shards["o_in"], tm=tw,
                                            name="o_in_fwd", comm=ex.gather2(late, part))
    wt.update(zip(late, got))
    yy, pooled, gg = _o_mix_fwd(q, wt["o_cw"], sm["o_c_b"], sm["o_c_scale"], tm=tw, name="o_mix_fwd")
    dout, dx2, dyy, lcol, dg_o_post = _out_loss(yy, wt["o_out"], x1, sm["o_norm_post"], tgt, tm=tx, name="o_out_loss")
    loss = (0.5 / D) * jnp.sum(lcol)

    dq, d_cw, d_cb, d_cscale = _o_mix_bwd(dyy, q, gg, pooled, wt["o_cw"], sm["o_c_scale"], tm=tw, name="o_mix_bwd")
    g_o_out, _ = _mm_tn(yy, dout, ts=tnt, tn=W, name="o_out_dw")
    ga = dict(o_out=g_o_out, o_cw=d_cw.astype(BF16))
    dh1, ra = _mm_nt(dq, wt["o_in"], tm=tnt, tk=W, name="o_in_bwd", comm=ex.pair(ga))
    sa = ex.pair_sums(ga, ra)
    (dx1, dy0, dg_o_pre, dg_e_post), ra = _pre_bwd_o(dh1, x1, dx2, y0, sm["o_norm_pre"], sm["e_norm_post"],
                                                     tm=tx, name="o_pre_bwd", comm=ex.chips(sa))
    ex.done(sa, ra)
    g_o_in, _ = _mm_tn(h1, dq, ts=tnt, tn=W, name="o_in_dw")
    gb = dict(o_in=g_o_in)
    du, rb = _mm_nt(dy0, wt["e_out"], tm=tnt, tk=W, name="e_out_bwd", comm=ex.pair(gb))
    sb = ex.pair_sums(gb, rb)
    g_e_out, _ = _mm_tn(u, dy0, ts=tnt, tn=W, name="e_out_dw")
    gc = dict(e_out=g_e_out)
    (dp, d_wa, d_wb, d_bias, d_lg, d_lb), rbc = _e_mix_bwd(
        du, p, cb, sm["e_a_conv"], sm["e_b_conv"], sm["e_b_ln_g"], sm["e_b_ln_b"], tm=tx, name="e_mix_bwd",
        comm=_merge(ex.chips(sb), ex.pair(gc)))
    ex.done(sb, rbc[:1])
    sc = ex.pair_sums(gc, rbc[1:])
    order_out = jnp.concatenate([ex.order[1:], ex.order[:1]])
    sd, from_chip, rc = _dw_reduce(order_out, h0, dp, ex.shards["e_in"].shape[1], ts=tnt, name="e_in_dw",
                                   comm=ex.chips(sc))
    ex.done(sc, rc)
    dh0, rd = _mm_nt(dp, wt["e_in"], tm=tnt, tk=W, name="e_in_bwd", comm=_diag_comm(sd, from_chip))
    ex.reduced["e_in"] = (sd, rd[0], jnp.full((1,), 3, jnp.int32))
    grad_x, dg_e_pre = _pre_bwd_e(dh0, x, dx1, sm["e_norm_pre"], tm=tw, name="e_pre_bwd")

    small = dict(e_norm_pre=dg_e_pre, e_norm_post=dg_e_post, e_a_conv=d_wa, e_b_conv=d_wb, e_b_conv_bias=d_bias,
                 e_b_ln_g=d_lg, e_b_ln_b=d_lb, o_norm_pre=dg_o_pre, o_norm_post=dg_o_post,
                 o_c_b=d_cb.reshape(n_groups, -1), o_c_scale=d_cscale)
    return loss, grad_x, small


def kernel(x, e_norm_pre, e_norm_post, e_w_in, e_a_conv, e_b_conv, e_b_conv_bias, e_b_ln_g, e_b_ln_b, e_w_out, o_norm_pre, o_norm_post, o_w_in, o_c_w, o_c_b, o_c_scale, o_w_out, loss_target, m_e_norm_pre, m_e_norm_post, m_e_w_in, m_e_a_conv, m_e_b_conv, m_e_b_conv_bias, m_e_b_ln_g, m_e_b_ln_b, m_e_w_out, m_o_norm_pre, m_o_norm_post, m_o_w_in, m_o_c_w, m_o_c_b, m_o_c_scale, m_o_w_out, v_e_norm_pre, v_e_norm_post, v_e_w_in, v_e_a_conv, v_e_b_conv, v_e_b_conv_bias, v_e_b_ln_g, v_e_b_ln_b, v_e_w_out, v_o_norm_pre, v_o_norm_post, v_o_w_in, v_o_c_w, v_o_c_b, v_o_c_scale, v_o_w_out):
    xi, yi, ci = _place()
    w_big = dict(e_in=e_w_in[0], e_out=e_w_out[0], o_in=o_w_in[0], o_cw=o_c_w[0], o_out=o_w_out[0])
    m_big = dict(e_in=m_e_w_in[0], e_out=m_e_w_out[0], o_in=m_o_w_in[0], o_cw=m_o_c_w[0], o_out=m_o_w_out[0])
    v_big = dict(e_in=v_e_w_in[0], e_out=v_e_w_out[0], o_in=v_o_w_in[0], o_cw=v_o_c_w[0], o_out=v_o_w_out[0])
    w_small = dict(e_norm_pre=e_norm_pre, e_norm_post=e_norm_post, e_b_conv_bias=e_b_conv_bias, e_b_ln_g=e_b_ln_g,
                   e_b_ln_b=e_b_ln_b, e_a_conv=e_a_conv[0], e_b_conv=e_b_conv[0], o_norm_pre=o_norm_pre,
                   o_norm_post=o_norm_post, o_c_b=o_c_b[0], o_c_scale=o_c_scale)
    m_small = dict(e_norm_pre=m_e_norm_pre, e_norm_post=m_e_norm_post, e_b_conv_bias=m_e_b_conv_bias,
                   e_b_ln_g=m_e_b_ln_g, e_b_ln_b=m_e_b_ln_b, e_a_conv=m_e_a_conv[0], e_b_conv=m_e_b_conv[0],
                   o_norm_pre=m_o_norm_pre, o_norm_post=m_o_norm_post, o_c_b=m_o_c_b[0], o_c_scale=m_o_c_scale)
    v_small = dict(e_norm_pre=v_e_norm_pre, e_norm_post=v_e_norm_post, e_b_conv_bias=v_e_b_conv_bias,
                   e_b_ln_g=v_e_b_ln_g, e_b_ln_b=v_e_b_ln_b, e_a_conv=v_e_a_conv[0], e_b_conv=v_e_b_conv[0],
                   o_norm_pre=v_o_norm_pre, o_norm_post=v_o_norm_post, o_c_b=v_o_c_b[0], o_c_scale=v_o_c_scale)

    c_idx = jnp.reshape(ci, (1,)).astype(jnp.int32)
    order = jnp.stack([2 * xi + yi, 2 * (1 - xi) + yi, 2 * xi + (1 - yi), 2 * (1 - xi) + (1 - yi)]).astype(jnp.int32)
    ex = _Exchange({k: w_big[k].astype(BF16) for k in BIG}, _pack([w_small[k] for k in SHARDED]), order, c_idx)
    loss, grad_x, g_small = _local_step(x[0], loss_target[0], w_small, ex)

    big_out = {}
    for k in BIG:
        sums, received, q_idx = ex.reduced[k]
        big_out[k] = _adam_big(q_idx, sums, received, w_big[k], m_big[k], v_big[k], BIG_SPLIT[k], name="adam_" + k)[0]

    rep = _pack([g_small[k] for k in REPLICATED])
    loss_row = jnp.pad(jnp.reshape(loss, (1, 1)), ((0, 0), (0, LANES - 1)))
    blocks = []
    for k in SHARDED:
        r, n = w_small[k].shape
        blocks.append(g_small[k].reshape(r, N_DEV, n).transpose(1, 0, 2).reshape(N_DEV, r * n))
    blocks = jnp.concatenate(blocks, axis=1).reshape(N_DEV, -1, LANES)
    head = jnp.concatenate([rep, loss_row], axis=0)
    send = jnp.concatenate([jnp.broadcast_to(head[None], (N_DEV,) + head.shape), blocks], axis=1)
    parts = _run_comm(_small_scatter_comm(send), "small_grad_exchange")[0]

    def own_rows(d):
        return jnp.concatenate([_pack([d[k] for k in REPLICATED]), jnp.ones((1, LANES), F32),
                                _pack([d[k] for k in SHARDED])], axis=0)

    res_small = _adam_small(parts, own_rows(w_small), own_rows(m_small), own_rows(v_small), name="adam_small")
    n_rep = rep.shape[0]
    loss = res_small[0][n_rep, 0]
    small_out = {k: [] for k in SMALL}
    for packed in res_small:
        for k, t in zip(REPLICATED, _unpack(packed[:n_rep], [w_small[k].shape for k in REPLICATED])):
            small_out[k].append(t)
        for k, t in zip(SHARDED, _unpack(packed[n_rep + 1:], [w_small[k].shape for k in SHARDED])):
            small_out[k].append(t)

    big_of = dict(e_w_in="e_in", e_w_out="e_out", o_w_in="o_in", o_c_w="o_cw", o_w_out="o_out")
    stacked = ("e_a_conv", "e_b_conv", "o_c_b")

    def leaf(name, which):
        if name in big_of:
            return big_out[big_of[name]][which][None]
        t = small_out[name][which]
        return t[None] if name in stacked else t

    order = ("e_norm_pre", "e_norm_post", "e_w_in", "e_a_conv", "e_b_conv", "e_b_conv_bias", "e_b_ln_g", "e_b_ln_b",
             "e_w_out", "o_norm_pre", "o_norm_post", "o_w_in", "o_c_w", "o_c_b", "o_c_scale", "o_w_out")
    outs = [loss, grad_x[None]]
    for which in range(4):
        outs += [leaf(nm, which) for nm in order]
    return tuple(outs)
```

```python
import jax
import jax.numpy as jnp
from jax import lax
from jax.experimental import pallas as pl
from jax.experimental.pallas import tpu as pltpu

F32 = jnp.float32
BF16 = jnp.bfloat16
EPS = 1e-6
MESH = pl.DeviceIdType.MESH
ANY = pl.BlockSpec(memory_space=pl.ANY)

N_DEV = 8
HALO = 32
PHALO = 16
CONV_A = 3
CONV_B = 31
POOL_WINDOWS = (2, 4, 8, 16)
LANES = 128
MIB = 1024 * 1024

ADAM_LR = 0.001
ADAM_B1 = 0.9
ADAM_B2 = 0.999
ADAM_EPS = 1e-08
ADAM_WD = 0.01
ADAM_STEP = 10

TM_NT = 1024
TM_MIX = 256
TM_WIDE = 512


def _sds(shape, dtype):
    return jax.ShapeDtypeStruct(tuple(shape), dtype)


def _params(sem, vmem_mib):
    return pltpu.CompilerParams(dimension_semantics=sem, vmem_limit_bytes=vmem_mib * MIB)


def _const(shape, single=False):
    n = len(shape)
    if single:
        return pl.BlockSpec(shape, lambda *_: (0,) * n, pipeline_mode=pl.Buffered(1))
    return pl.BlockSpec(shape, lambda *_: (0,) * n)


def _sig(v):
    return jax.nn.sigmoid(v)


def _dsilu(v, s):
    return s * (1.0 + v * (1.0 - s))


def _rms(v):
    return lax.rsqrt(jnp.mean(v * v, axis=-1, keepdims=True) + EPS)


def _norm_bwd(dn, n, r):
    return r * (dn - n * jnp.mean(dn * n, axis=-1, keepdims=True))


def _colsum(v):
    return jnp.sum(v, axis=0, keepdims=True)


class _Comm:
    def __init__(self, inputs, out_shapes, sems, start, finish, aliases=None, middle=None):
        self.inputs, self.out_shapes, self.sems = list(inputs), list(out_shapes), list(sems)
        self.start, self.finish, self.middle = start, finish, middle
        self.aliases = dict(aliases or {})


def _merge(*comms):
    comms = [c for c in comms if c is not None]
    if len(comms) <= 1:
        return comms[0] if comms else None
    spans, i0, o0, s0, aliases = [], 0, 0, 0, {}
    for c in comms:
        spans.append((i0, o0, s0))
        aliases.update({i0 + k: o0 + v for k, v in c.aliases.items()})
        i0, o0, s0 = i0 + len(c.inputs), o0 + len(c.out_shapes), s0 + len(c.sems)

    def run(which):
        def fn(ins, outs, sems):
            for c, (i, o, s) in zip(comms, spans):
                hook = getattr(c, which)
                if hook is not None:
                    hook(ins[i:i + len(c.inputs)], outs[o:o + len(c.out_shapes)], sems[s:s + len(c.sems)])
        return fn

    return _Comm([a for c in comms for a in c.inputs], [a for c in comms for a in c.out_shapes],
                 [a for c in comms for a in c.sems], run("start"), run("finish"), aliases,
                 run("middle") if any(c.middle is not None for c in comms) else None)


def _call(body, *, grid, in_specs, out_specs, out_shape, operands, name, params, scratch_shapes=(), comm=None,
          prefetch=None, own_copies_first=False):
    n_p = 0 if prefetch is None else 1
    n_i, n_o, n_s = len(in_specs), len(out_specs), len(scratch_shapes)
    if comm is None:
        comm = _Comm([], [], [], None, None)
    c_i, c_o = len(comm.inputs), len(comm.out_shapes)

    def carrier(*refs):
        pre, refs = refs[:n_p], refs[n_p:]
        ins, cins = refs[:n_i], refs[n_i:n_i + c_i]
        outs = refs[n_i + c_i:n_i + c_i + n_o]
        couts = refs[n_i + c_i + n_o:n_i + c_i + n_o + c_o]
        scr = refs[n_i + c_i + n_o + c_o:n_i + c_i + n_o + c_o + n_s]
        csems = refs[n_i + c_i + n_o + c_o + n_s:]
        ids = [pl.program_id(d) for d in range(len(grid))]
        first = ids[0] == 0
        half = ids[0] == grid[0] // 2
        last = ids[0] == grid[0] - 1
        for d in range(1, len(grid)):
            first = first & (ids[d] == 0)
            half = half & (ids[d] == 0)
            last = last & (ids[d] == grid[d] - 1)

        def start():
            if comm.start is not None:
                @pl.when(first)
                def _():
                    comm.start(cins, couts, csems)

        if not own_copies_first:
            start()
        if comm.middle is not None:
            assert grid[0] >= 2

            @pl.when(half)
            def _():
                comm.middle(cins, couts, csems)

        body(*pre, *ins, *outs, *scr)
        if own_copies_first:
            start()

        if comm.finish is not None:
            @pl.when(last)
            def _():
                comm.finish(cins, couts, csems)

    specs = dict(grid=grid, in_specs=list(in_specs) + [ANY] * c_i, out_specs=list(out_specs) + [ANY] * c_o,
                 scratch_shapes=list(scratch_shapes) + comm.sems)
    if n_p:
        specs = dict(grid_spec=pltpu.PrefetchScalarGridSpec(num_scalar_prefetch=1, **specs))
    res = pl.pallas_call(
        carrier, out_shape=list(out_shape) + comm.out_shapes,
        input_output_aliases={n_p + n_i + k: n_o + v for k, v in comm.aliases.items()},
        name=name, compiler_params=params, **specs)(*(() if prefetch is None else (prefetch,)), *operands, *comm.inputs)
    return list(res[:n_o]), list(res[n_o:])


def _run_comm(comm, name):
    c_i, c_o = len(comm.inputs), len(comm.out_shapes)

    def body(*refs):
        ins, outs, sems = refs[:c_i], refs[c_i:c_i + c_o], refs[c_i + c_o:]
        comm.start(ins, outs, sems)
        comm.finish(ins, outs, sems)

    res = pl.pallas_call(
        body, in_specs=[ANY] * c_i, out_specs=[ANY] * c_o, out_shape=comm.out_shapes, scratch_shapes=comm.sems,
        input_output_aliases=comm.aliases, name=name)(*comm.inputs)
    return list(res)


def _gather_matmul(order, x, g, shard, *, tm, name, comm=None):
    S, K = x.shape
    nb = shard.shape[1]
    n_i = S // tm

    def body(order_ref, x_ref, g_ref, shard_ref, p_ref, h_ref, full_ref, hbuf, wbuf, stage, send_sems, recv_sems,
             dma_sems):
        j, i = pl.program_id(0), pl.program_id(1)
        px, py, pc = _place()
        cps = _gather_copies(stage, full_ref, 1, nb, send_sems, recv_sems, 0)
        own = pltpu.make_async_copy(stage, _piece(full_ref, 1, nb, 4 * px + 2 * py + pc), dma_sems.at[0])
        keep_h = pltpu.make_async_copy(hbuf, h_ref, dma_sems.at[2])

        def load(src, dst):
            cp = pltpu.make_async_copy(src, dst, dma_sems.at[1])
            cp.start()
            cp.wait()

        def load_pair(qx, qy):
            load(_piece(full_ref, 1, 2 * nb, 2 * qx + qy), wbuf)

        @pl.when((j == 0) & (i == 0))
        def _():
            load(shard_ref, stage)
            own.start()
            for k in (0, 1, 2):
                cps[k].start()

        @pl.when(j == 0)
        def _():
            xx = x_ref[...]
            hbuf[i] = ((xx * _rms(xx)) * g_ref[...]).astype(BF16)

        @pl.when((j == 0) & (i == 0))
        def _():
            own.wait()
            cps[0].wait_recv()
            load_pair(px, py)

        @pl.when((j == 1) & (i == 0))
        def _():
            keep_h.start()
            cps[1].wait_recv()
            cps[3].start()
            cps[5].start()
            cps[2].wait_recv()
            cps[4].start()
            cps[6].start()
            cps[5].wait_recv()
            load_pair(1 - px, py)

        @pl.when((j == 2) & (i == 0))
        def _():
            cps[6].wait_recv()
            load_pair(px, 1 - py)

        @pl.when((j == 3) & (i == 0))
        def _():
            cps[3].wait_recv()
            cps[4].wait_recv()
            cps[7].start()
            cps[7].wait_recv()
            load_pair(1 - px, 1 - py)

        p_ref[...] = jnp.dot(hbuf[i], wbuf[...], preferred_element_type=F32).astype(BF16)

        @pl.when((j == 3) & (i == n_i - 1))
        def _():
            for cp in cps:
                cp.wait_send()
            keep_h.wait()

    first_pass = lambda j, i, o: (jnp.where(j == 0, i, n_i - 1), 0)
    outs, extra = _call(
        body, grid=(4, n_i), prefetch=order,
        in_specs=[pl.BlockSpec((tm, K), first_pass), pl.BlockSpec((1, K), lambda j, i, o: (0, 0)), ANY],
        out_specs=[pl.BlockSpec((tm, 2 * nb), lambda j, i, o: (i, o[j])), ANY, ANY],
        out_shape=[_sds((S, N_DEV * nb), BF16), _sds((n_i, tm, K), BF16), _sds((K, N_DEV * nb), BF16)],
        operands=(x, g, shard),
        scratch_shapes=[pltpu.VMEM((n_i, tm, K), BF16), pltpu.VMEM((K, 2 * nb), BF16), pltpu.VMEM((K, nb), BF16),
                        pltpu.SemaphoreType.DMA((N_GATHER,)), pltpu.SemaphoreType.DMA((N_GATHER,)),
                        pltpu.SemaphoreType.DMA((3,))],
        name=name, params=_params(("arbitrary", "arbitrary"), 58), comm=comm, own_copies_first=True)
    return outs[0], outs[1].reshape(S, K), outs[2], extra


def _out_norm_res(u, w, x, g, *, tm, name, comm=None):
    S, K = u.shape
    D = w.shape[1]

    def body(u_ref, w_ref, x_ref, g_ref, x1_ref, y_ref):
        y = jnp.dot(u_ref[...], w_ref[...], preferred_element_type=F32)
        y_ref[...] = y.astype(BF16)
        x1_ref[...] = x_ref[...] + (y * _rms(y)) * g_ref[...]

    return _call(
        body, grid=(S // tm,),
        in_specs=[pl.BlockSpec((tm, K), lambda i: (i, 0)), _const((K, D), single=True),
                  pl.BlockSpec((tm, D), lambda i: (i, 0)), _const((1, D))],
        out_specs=[pl.BlockSpec((tm, D), lambda i: (i, 0)), pl.BlockSpec((tm, D), lambda i: (i, 0))],
        out_shape=[_sds((S, D), F32), _sds((S, D), BF16)], operands=(u, w, x, g),
        name=name, params=_params(("arbitrary",), 56), comm=comm)


def _out_loss(yy, w, x1, g, tgt, *, tm, name):
    S, K = yy.shape
    D = w.shape[1]

    def body(yy_ref, w_ref, x1_ref, g_ref, t_ref, dout_ref, dx2_ref, dyy_ref, lcol_ref, dg_ref):
        out = jnp.dot(yy_ref[...], w_ref[...], preferred_element_type=F32)
        r = _rms(out)
        n = out * r
        gg = g_ref[...]
        e = x1_ref[...] + n * gg - t_ref[...]
        dx2 = e * (1.0 / D)
        dx2_ref[...] = dx2
        dout = _norm_bwd(dx2 * gg, n, r).astype(BF16)
        dout_ref[...] = dout
        dyy_ref[...] = lax.dot_general(dout, w_ref[...], (((1,), (1,)), ((), ())),
                                       preferred_element_type=F32).astype(BF16)

        @pl.when(pl.program_id(0) == 0)
        def _():
            lcol_ref[...] = jnp.zeros_like(lcol_ref)
            dg_ref[...] = jnp.zeros_like(dg_ref)

        lcol_ref[...] += _colsum(e * e)
        dg_ref[...] += _colsum(dx2 * n)

    return _call(
        body, grid=(S // tm,),
        in_specs=[pl.BlockSpec((tm, K), lambda i: (i, 0)), _const((K, D), single=True),
                  pl.BlockSpec((tm, D), lambda i: (i, 0)), _const((1, D)),
                  pl.BlockSpec((tm, D), lambda i: (i, 0))],
        out_specs=[pl.BlockSpec((tm, D), lambda i: (i, 0)), pl.BlockSpec((tm, D), lambda i: (i, 0)),
                   pl.BlockSpec((tm, K), lambda i: (i, 0)), _const((1, D)), _const((1, D))],
        out_shape=[_sds((S, D), BF16), _sds((S, D), F32), _sds((S, K), BF16), _sds((1, D), F32), _sds((1, D), F32)],
        operands=(yy, w, x1, g, tgt), name=name, params=_params(("arbitrary",), 52))[0]


def _mm_nt(a, w, *, tm, tk, name, comm=None):
    S, N = a.shape
    D = w.shape[0]
    n_k = N // tk

    def body(a_ref, w_ref, o_ref, acc_ref):
        k = pl.program_id(1)

        @pl.when(k == 0)
        def _():
            acc_ref[...] = jnp.zeros_like(acc_ref)

        acc_ref[...] = lax.dot_general(a_ref[...], w_ref[...], (((1,), (1,)), ((), ())),
                                       preferred_element_type=F32) + acc_ref[...]

        @pl.when(k == n_k - 1)
        def _():
            o_ref[...] = acc_ref[...].astype(BF16)

    outs, extra = _call(
        body, grid=(S // tm, n_k),
        in_specs=[pl.BlockSpec((tm, tk), lambda i, k: (i, k)), pl.BlockSpec((D, tk), lambda i, k: (0, k))],
        out_specs=[pl.BlockSpec((tm, D), lambda i, k: (i, 0))],
        out_shape=[_sds((S, D), BF16)], operands=(a, w),
        scratch_shapes=[pltpu.VMEM((tm, D), F32)],
        name=name, params=_params(("arbitrary", "arbitrary"), 48), comm=comm)
    return outs[0], extra


def _mm_tn(a, b, *, ts, tn, name, comm=None):
    S, M = a.shape
    N = b.shape[1]
    n_s = S // ts

    def body(a_ref, b_ref, o_ref, acc_ref):
        s = pl.program_id(1)

        @pl.when(s == 0)
        def _():
            acc_ref[...] = jnp.zeros_like(acc_ref)

        acc_ref[...] = lax.dot_general(a_ref[...], b_ref[...], (((0,), (0,)), ((), ())),
                                       preferred_element_type=F32) + acc_ref[...]

        @pl.when(s == n_s - 1)
        def _():
            o_ref[...] = acc_ref[...].astype(BF16)

    outs, extra = _call(
        body, grid=(N // tn, n_s),
        in_specs=[pl.BlockSpec((ts, M), lambda j, s: (s, 0)), pl.BlockSpec((ts, tn), lambda j, s: (s, j))],
        out_specs=[pl.BlockSpec((M, tn), lambda j, s: (0, j))],
        out_shape=[_sds((M, N), BF16)], operands=(a, b),
        scratch_shapes=[pltpu.VMEM((M, tn), F32)],
        name=name, params=_params(("arbitrary", "arbitrary"), 48), comm=comm)
    return outs[0], extra


def _dw_reduce(order, a, b, nb, *, ts, name, comm=None):
    S, M = a.shape
    n_s = S // ts
    rows = 512

    def body(order_ref, a_ref, b_ref, sums_ref, from_sib_ref, from_chip_ref, acc, send_buf, mine_buf, recv_buf,
             sib_send, sib_recv, chip_send, chip_recv, dma_sems):
        t, s = pl.program_id(0), pl.program_id(1)
        x, y, c = _place()
        targets = [(1 - x, y, c), (x, 1 - y, c)]

        def to_sibling(k):
            return pltpu.make_async_remote_copy(
                src_ref=send_buf, dst_ref=from_sib_ref.at[k], send_sem=sib_send.at[k], recv_sem=sib_recv.at[k],
                device_id=(x, y, 1 - c), device_id_type=MESH)

        def to_chip(k):
            return pltpu.make_async_remote_copy(
                src_ref=sums_ref.at[k], dst_ref=from_chip_ref.at[k], send_sem=chip_send.at[k],
                recv_sem=chip_recv.at[k], device_id=targets[k], device_id_type=MESH)

        def finish(k):
            to_sibling(k).wait()
            get = pltpu.make_async_copy(from_sib_ref.at[k], recv_buf, dma_sems.at[0])
            get.start()
            get.wait()
            for r in range(0, M, rows):
                recv_buf[r:r + rows, :] = (mine_buf[r:r + rows, :].astype(F32)
                                           + recv_buf[r:r + rows, :].astype(F32)).astype(BF16)
            put = pltpu.make_async_copy(recv_buf, sums_ref.at[k], dma_sems.at[1])
            put.start()
            put.wait()
            if k < 2:
                to_chip(k).start()

        for k in range(3):
            @pl.when((t == k + 1) & (s == min(1, n_s - 1)))
            def _(k=k):
                finish(k)

        @pl.when(s == 0)
        def _():
            acc[...] = jnp.zeros_like(acc)

        acc[...] = lax.dot_general(a_ref[...], b_ref[...], (((0,), (0,)), ((), ())),
                                   preferred_element_type=F32) + acc[...]

        @pl.when(s == n_s - 1)
        def _():
            for r in range(0, M, rows):
                lo, hi = acc[r:r + rows, :nb], acc[r:r + rows, nb:]
                send_buf[r:r + rows, :] = jnp.where(c == 0, hi, lo).astype(BF16)
                mine_buf[r:r + rows, :] = jnp.where(c == 0, lo, hi).astype(BF16)
            to_sibling(t).start()

        @pl.when((t == 3) & (s == n_s - 1))
        def _():
            finish(3)
            to_chip(0).wait()
            to_chip(1).wait()

    piece = _sds((4, M, nb), BF16)
    outs, extra = _call(
        body, grid=(4, n_s), prefetch=order,
        in_specs=[pl.BlockSpec((ts, M), lambda t, s, o: (s, 0)), pl.BlockSpec((ts, 2 * nb), lambda t, s, o: (s, o[t]))],
        out_specs=[ANY, ANY, ANY], out_shape=[piece, piece, _sds((3, M, nb), BF16)], operands=(a, b),
        scratch_shapes=[pltpu.VMEM((M, 2 * nb), F32), pltpu.VMEM((M, nb), BF16), pltpu.VMEM((M, nb), BF16),
                        pltpu.VMEM((M, nb), BF16), pltpu.SemaphoreType.DMA((4,)), pltpu.SemaphoreType.DMA((4,)),
                        pltpu.SemaphoreType.DMA((2,)), pltpu.SemaphoreType.DMA((2,)), pltpu.SemaphoreType.DMA((2,))],
        name=name, params=_params(("arbitrary", "arbitrary"), 56), comm=comm)
    return outs[0], outs[2], extra


def _diag_comm(sums, from_chip):
    def copy(ins, outs, sems):
        x, y, c = _place()
        return pltpu.make_async_remote_copy(
            src_ref=ins[0].at[2], dst_ref=outs[0].at[2], send_sem=sems[0].at[0], recv_sem=sems[1].at[0],
            device_id=(1 - x, 1 - y, c), device_id_type=MESH)

    def start(ins, outs, sems):
        copy(ins, outs, sems).start()

    def finish(ins, outs, sems):
        copy(ins, outs, sems).wait()

    sems = [pltpu.SemaphoreType.DMA((1,)), pltpu.SemaphoreType.DMA((1,))]
    return _Comm([sums, from_chip], [_sds(from_chip.shape, from_chip.dtype)], sems, start, finish, aliases={1: 0})


def _pre_bwd_o(dh, x1, dx2, y0, g_pre, g_post, *, tm, name, comm=None):
    S, D = x1.shape

    def body(dh_ref, x1_ref, dx2_ref, y0_ref, gpre_ref, gpost_ref, dx1_ref, dy0_ref, dgpre_ref, dgpost_ref):
        @pl.when(pl.program_id(0) == 0)
        def _():
            dgpre_ref[...] = jnp.zeros_like(dgpre_ref)
            dgpost_ref[...] = jnp.zeros_like(dgpost_ref)

        dh = dh_ref[...].astype(F32)
        x1 = x1_ref[...]
        r2 = _rms(x1)
        xn = x1 * r2
        dgpre_ref[...] += _colsum(dh * xn)
        dx1 = dx2_ref[...] + _norm_bwd(dh * gpre_ref[...], xn, r2)
        dx1_ref[...] = dx1
        y = y0_ref[...].astype(F32)
        r1 = _rms(y)
        n1 = y * r1
        dgpost_ref[...] += _colsum(dx1 * n1)
        dy0_ref[...] = _norm_bwd(dx1 * gpost_ref[...], n1, r1).astype(BF16)

    row = pl.BlockSpec((tm, D), lambda i: (i, 0))
    return _call(
        body, grid=(S // tm,),
        in_specs=[row, row, row, row, _const((1, D)), _const((1, D))],
        out_specs=[row, row, _const((1, D)), _const((1, D))],
        out_shape=[_sds((S, D), F32), _sds((S, D), BF16), _sds((1, D), F32), _sds((1, D), F32)],
        operands=(dh, x1, dx2, y0, g_pre, g_post),
        name=name, params=_params(("arbitrary",), 48), comm=comm)


def _pre_bwd_e(dh, x, dx1, g_pre, *, tm, name):
    S, D = x.shape

    def body(dh_ref, x_ref, dx1_ref, gpre_ref, gx_ref, dgpre_ref):
        @pl.when(pl.program_id(0) == 0)
        def _():
            dgpre_ref[...] = jnp.zeros_like(dgpre_ref)

        dh = dh_ref[...].astype(F32)
        xx = x_ref[...]
        r0 = _rms(xx)
        xn = xx * r0
        dgpre_ref[...] += _colsum(dh * xn)
        gx_ref[...] = dx1_ref[...] + _norm_bwd(dh * gpre_ref[...], xn, r0)

    row = pl.BlockSpec((tm, D), lambda i: (i, 0))
    return _call(
        body, grid=(S // tm,),
        in_specs=[row, row, row, _const((1, D))],
        out_specs=[row, _const((1, D))],
        out_shape=[_sds((S, D), F32), _sds((1, D), F32)],
        operands=(dh, x, dx1, g_pre), name=name, params=_params(("arbitrary",), 56))[0]


SUBLANES = 8


def _shift_copies(sh_ref, ext_ref, cs):
    for b in range(1, SUBLANES):
        sh_ref[b - 1] = ext_ref[pl.ds(b, sh_ref.shape[1]), cs]


def _rows_at(ext_ref, sh_ref, off, cs, tm):
    b = off % SUBLANES
    if b == 0 or sh_ref is None:
        return ext_ref[pl.ds(off, tm), cs]
    return sh_ref[b - 1, pl.ds(off - b, tm), :]


def _taps(ext_ref, w_ref, n_taps, base, cs, tm, sh_ref=None):
    acc = _rows_at(ext_ref, sh_ref, base, cs, tm) * w_ref[0:1, cs]
    for k in range(1, n_taps):
        acc = acc + _rows_at(ext_ref, sh_ref, base + k, cs, tm) * w_ref[k:k + 1, cs]
    return acc


def _taps_rev(ext_ref, w_ref, n_taps, cs, tm, sh_ref=None):
    acc = _rows_at(ext_ref, sh_ref, n_taps - 1, cs, tm) * w_ref[0:1, cs]
    for k in range(1, n_taps):
        acc = acc + _rows_at(ext_ref, sh_ref, n_taps - 1 - k, cs, tm) * w_ref[k:k + 1, cs]
    return acc


def _e_mix_fwd(p, wa, wb, bias, ln_g, ln_b, *, tm, name, comm=None):
    S = p.shape[0]
    W = p.shape[1] // 7
    nb = tm // HALO
    chunks = [slice(c * LANES, (c + 1) * LANES) for c in range(W // LANES)]

    def body(p_ref, hax_ref, hac_ref, hbv_ref, hbg_ref, wa_ref, wb_ref, bias_ref, lg_ref, lb_ref,
             u_ref, cb_ref, ext_ref, sh_ref):
        keep = (pl.program_id(0) > 0).astype(F32)
        col = lambda j, cs: p_ref[:, j * W + cs.start:j * W + cs.stop].astype(F32)

        ext_ref[0:HALO, :] = hax_ref[...].astype(F32) * hac_ref[...].astype(F32) * keep
        ext_ref[HALO:, :] = p_ref[:, 2 * W:3 * W].astype(F32) * p_ref[:, 0:W].astype(F32)
        for cs in chunks:
            conv = _taps(ext_ref, wa_ref, CONV_A, HALO - (CONV_A - 1), cs, tm)
            az = col(3, cs)
            u_ref[:, cs] = (col(1, cs) * conv * (az * _sig(az))).astype(BF16)

        ext_ref[0:HALO, :] = hbv_ref[...].astype(F32) * _sig(hbg_ref[...].astype(F32)) * keep
        ext_ref[HALO:, :] = p_ref[:, 4 * W:5 * W].astype(F32) * _sig(p_ref[:, 5 * W:6 * W].astype(F32))
        s1 = jnp.zeros((tm, LANES), F32)
        for cs in chunks:
            _shift_copies(sh_ref, ext_ref, cs)
            cb = _taps(ext_ref, wb_ref, CONV_B, HALO - (CONV_B - 1), cs, tm, sh_ref) + bias_ref[:, cs]
            cb_ref[:, cs] = cb
            s1 = s1 + cb
        mu = jnp.sum(s1, axis=-1, keepdims=True) * (1.0 / W)
        s2 = jnp.zeros((tm, LANES), F32)
        for cs in chunks:
            xc = cb_ref[:, cs] - mu
            s2 = s2 + xc * xc
        rs = lax.rsqrt(jnp.sum(s2, axis=-1, keepdims=True) * (1.0 / W) + EPS)
        for cs in chunks:
            lb = (cb_ref[:, cs] - mu) * rs * lg_ref[:, cs] + lb_ref[:, cs]
            bz = col(6, cs)
            u_ref[:, W + cs.start:W + cs.stop] = (lb * _sig(lb) * (bz * _sig(bz))).astype(BF16)

    prev = lambda j: pl.BlockSpec((HALO, W), lambda i: (jnp.maximum(i * nb - 1, 0), j))
    return _call(
        body, grid=(S // tm,),
        in_specs=[pl.BlockSpec((tm, 7 * W), lambda i: (i, 0)), prev(0), prev(2), prev(4), prev(5),
                  _const((CONV_A, W)), _const((CONV_B, W)), _const((1, W)), _const((1, W)), _const((1, W))],
        out_specs=[pl.BlockSpec((tm, 2 * W), lambda i: (i, 0)), pl.BlockSpec((tm, W), lambda i: (i, 0))],
        out_shape=[_sds((S, 2 * W), BF16), _sds((S, W), F32)],
        operands=(p, p, p, p, p, wa, wb, bias, ln_g, ln_b),
        scratch_shapes=[pltpu.VMEM((HALO + tm, W), F32),
                        pltpu.VMEM((SUBLANES - 1, HALO + tm - SUBLANES, LANES), F32)],
        name=name, params=_params(("arbitrary",), 48), comm=comm)


def _e_mix_bwd(du, p, cb, wa, wb, ln_g, ln_b, *, tm, name, comm=None):
    S = p.shape[0]
    W = p.shape[1] // 7
    nb = tm // HALO
    n_t = S // tm
    last_blk = S // HALO - 1
    chunks = [slice(c * LANES, (c + 1) * LANES) for c in range(W // LANES)]

    def body(du_ref, duf_ref, p_ref, fab_ref, faz_ref, fbz_ref, hax_ref, hac_ref, hbv_ref, hbg_ref,
             cb_ref, cbf_ref, wa_ref, wb_ref, lg_ref, lb_ref,
             dp_ref, dwa_ref, dwb_ref, dbias_ref, dlg_ref, dlb_ref, extd_ref, extg_ref, shd_ref, shg_ref):
        i = pl.program_id(0)
        keep_prev = (i > 0).astype(F32)
        keep_next = (i < n_t - 1).astype(F32)
        col = lambda j, cs: p_ref[:, j * W + cs.start:j * W + cs.stop].astype(F32)

        @pl.when(i == 0)
        def _():
            dwa_ref[...] = jnp.zeros_like(dwa_ref)
            dwb_ref[...] = jnp.zeros_like(dwb_ref)
            dbias_ref[...] = jnp.zeros_like(dbias_ref)
            dlg_ref[...] = jnp.zeros_like(dlg_ref)
            dlb_ref[...] = jnp.zeros_like(dlb_ref)

        def dcb_rows(rows, cb_rows_ref, dub, bz_of, dst0, scale, main):
            cbv = cb_rows_ref[...]
            mu = jnp.mean(cbv, axis=-1, keepdims=True)
            xc = cbv - mu
            rs = lax.rsqrt(jnp.mean(xc * xc, axis=-1, keepdims=True) + EPS)
            m1 = jnp.zeros((rows, LANES), F32)
            m2 = jnp.zeros((rows, LANES), F32)
            for cs in chunks:
                nbv = (cb_rows_ref[:, cs] - mu) * rs
                lb = nbv * lg_ref[:, cs] + lb_ref[:, cs]
                sl = _sig(lb)
                bz = bz_of(cs)
                sz = _sig(bz)
                dub_c = dub(cs)
                dlb = dub_c * (bz * sz) * _dsilu(lb, sl)
                if main:
                    dlg_ref[:, cs] += _colsum(dlb * nbv)
                    dlb_ref[:, cs] += _colsum(dlb)
                    dp_ref[:, 6 * W + cs.start:6 * W + cs.stop] = (dub_c * (lb * sl) * _dsilu(bz, sz)).astype(BF16)
                dnb = dlb * lg_ref[:, cs]
                extd_ref[dst0:dst0 + rows, cs] = dnb
                m1 = m1 + dnb
                m2 = m2 + dnb * nbv
            m1 = jnp.sum(m1, axis=-1, keepdims=True) * (1.0 / W)
            m2 = jnp.sum(m2, axis=-1, keepdims=True) * (1.0 / W)
            for cs in chunks:
                nbv = (cb_rows_ref[:, cs] - mu) * rs
                dcb = rs * (extd_ref[dst0:dst0 + rows, cs] - m1 - nbv * m2) * scale
                extd_ref[dst0:dst0 + rows, cs] = dcb
                if main:
                    dbias_ref[:, cs] += _colsum(dcb)

        dcb_rows(tm, cb_ref, lambda cs: du_ref[:, W + cs.start:W + cs.stop].astype(F32),
                 lambda cs: col(6, cs), 0, 1.0, True)
        dcb_rows(HALO, cbf_ref, lambda cs: duf_ref[:, W + cs.start:W + cs.stop].astype(F32),
                 lambda cs: fbz_ref[:, cs].astype(F32), tm, keep_next, False)

        extg_ref[0:HALO, :] = hbv_ref[...].astype(F32) * _sig(hbg_ref[...].astype(F32)) * keep_prev
        extg_ref[HALO:, :] = p_ref[:, 4 * W:5 * W].astype(F32) * _sig(p_ref[:, 5 * W:6 * W].astype(F32))
        base_b = HALO - (CONV_B - 1)
        for cs in chunks:
            _shift_copies(shd_ref, extd_ref, cs)
            _shift_copies(shg_ref, extg_ref, cs)
            dgb = _taps_rev(extd_ref, wb_ref, CONV_B, cs, tm, shd_ref)
            bv = col(4, cs)
            sg = _sig(col(5, cs))
            dp_ref[:, 4 * W + cs.start:4 * W + cs.stop] = (dgb * sg).astype(BF16)
            dp_ref[:, 5 * W + cs.start:5 * W + cs.stop] = (dgb * bv * sg * (1.0 - sg)).astype(BF16)
            dcb = extd_ref[0:tm, cs]
            for k in range(CONV_B):
                dwb_ref[k:k + 1, cs] += _colsum(dcb * _rows_at(extg_ref, shg_ref, base_b + k, cs, tm))

        extg_ref[0:HALO, :] = hax_ref[...].astype(F32) * hac_ref[...].astype(F32) * keep_prev
        extg_ref[HALO:, :] = p_ref[:, 2 * W:3 * W].astype(F32) * p_ref[:, 0:W].astype(F32)
        base_a = HALO - (CONV_A - 1)
        for cs in chunks:
            conv = _taps(extg_ref, wa_ref, CONV_A, base_a, cs, tm)
            az = col(3, cs)
            sz = _sig(az)
            ab = col(1, cs)
            dua = du_ref[:, cs].astype(F32)
            dya = dua * (az * sz)
            dp_ref[:, W + cs.start:W + cs.stop] = (dya * conv).astype(BF16)
            dp_ref[:, 3 * W + cs.start:3 * W + cs.stop] = (dua * (ab * conv) * _dsilu(az, sz)).astype(BF16)
            extd_ref[0:tm, cs] = dya * ab
            azf = faz_ref[:, cs].astype(F32)
            extd_ref[tm:tm + HALO, cs] = (duf_ref[:, cs].astype(F32) * (azf * _sig(azf))
                                          * fab_ref[:, cs].astype(F32) * keep_next)
        for cs in chunks:
            dca = _taps_rev(extd_ref, wa_ref, CONV_A, cs, tm)
            dp_ref[:, cs] = (dca * col(2, cs)).astype(BF16)
            dp_ref[:, 2 * W + cs.start:2 * W + cs.stop] = (dca * col(0, cs)).astype(BF16)
            dconv = extd_ref[0:tm, cs]
            for k in range(CONV_A):
                dwa_ref[k:k + 1, cs] += _colsum(dconv * extg_ref[pl.ds(base_a + k, tm), cs])

    prev = lambda j: pl.BlockSpec((HALO, W), lambda i: (jnp.maximum(i * nb - 1, 0), j))
    nxt = lambda j, w: pl.BlockSpec((HALO, w), lambda i: (jnp.minimum((i + 1) * nb, last_blk), j))
    row = lambda w: pl.BlockSpec((tm, w), lambda i: (i, 0))
    return _call(
        body, grid=(n_t,),
        in_specs=[row(2 * W), nxt(0, 2 * W), row(7 * W), nxt(1, W), nxt(3, W), nxt(6, W),
                  prev(0), prev(2), prev(4), prev(5), row(W), nxt(0, W),
                  _const((CONV_A, W)), _const((CONV_B, W)), _const((1, W)), _const((1, W))],
        out_specs=[row(7 * W), _const((CONV_A, W)), _const((CONV_B, W)), _const((1, W)), _const((1, W)), _const((1, W))],
        out_shape=[_sds((S, 7 * W), BF16), _sds((CONV_A, W), F32), _sds((CONV_B, W), F32),
                   _sds((1, W), F32), _sds((1, W), F32), _sds((1, W), F32)],
        operands=(du, du, p, p, p, p, p, p, p, p, cb, cb, wa, wb, ln_g, ln_b),
        scratch_shapes=[pltpu.VMEM((tm + HALO, W), F32), pltpu.VMEM((HALO + tm, W), F32),
                        pltpu.VMEM((SUBLANES - 1, HALO + tm - SUBLANES, LANES), F32),
                        pltpu.VMEM((SUBLANES - 1, HALO + tm - SUBLANES, LANES), F32)],
        name=name, params=_params(("arbitrary",), 52), comm=comm)


def _counts(i, tm, rows, off, win):
    t = i * tm + off + lax.broadcasted_iota(jnp.int32, (rows, 1), 0)
    return jnp.minimum(t + 1, win).astype(F32)


def _o_mix_fwd(q, cw, cb, cscale, *, tm, name):
    S = q.shape[0]
    WC = q.shape[1] // 2
    NG = len(POOL_WINDOWS)
    G = WC // NG
    nb = tm // PHALO

    def body(v_ref, z_ref, hv_ref, cw_ref, cb_ref, sc_ref, yy_ref, pooled_ref, gg_ref, ext_ref):
        i = pl.program_id(0)
        keep = (i > 0).astype(F32)
        for g, win in enumerate(POOL_WINDOWS):
            cs = slice(g * G, (g + 1) * G)
            v = v_ref[:, cs].astype(F32)
            ext_ref[0:PHALO, :] = hv_ref[:, cs].astype(F32) * keep
            ext_ref[PHALO:, :] = v
            s = v
            for j in range(1, win):
                s = s + ext_ref[pl.ds(PHALO - j, tm), :]
            pooled = (s / _counts(i, tm, tm, 0, win) - v).astype(BF16)
            pooled_ref[:, cs] = pooled
            gg = jnp.dot(pooled, cw_ref[g], preferred_element_type=F32) + cb_ref[:, cs]
            gg_ref[:, cs] = gg.astype(BF16)
            z = z_ref[:, cs].astype(F32)
            yy_ref[:, cs] = (gg * sc_ref[:, cs] * (z * _sig(z))).astype(BF16)

    row = lambda j: pl.BlockSpec((tm, WC), lambda i: (i, j))
    out = pl.BlockSpec((tm, WC), lambda i: (i, 0))
    return _call(
        body, grid=(S // tm,),
        in_specs=[row(0), row(1), pl.BlockSpec((PHALO, WC), lambda i: (jnp.maximum(i * nb - 1, 0), 0)),
                  _const((NG, G, G)), _const((1, WC)), _const((1, WC))],
        out_specs=[out, out, out],
        out_shape=[_sds((S, WC), BF16)] * 3, operands=(q, q, q, cw, cb, cscale),
        scratch_shapes=[pltpu.VMEM((PHALO + tm, G), F32)],
        name=name, params=_params(("arbitrary",), 40))[0]


def _o_mix_bwd(dyy, q, gg, pooled, cw, cscale, *, tm, name):
    S = q.shape[0]
    WC = q.shape[1] // 2
    NG = len(POOL_WINDOWS)
    G = WC // NG
    nb = tm // PHALO
    n_t = S // tm
    last_blk = S // PHALO - 1
    nt = (((1,), (1,)), ((), ()))
    tn = (((0,), (0,)), ((), ()))

    def body(dyy_ref, dyyf_ref, z_ref, zf_ref, gg_ref, pooled_ref, cw_ref, sc_ref,
             dq_ref, dcw_ref, dcb_ref, dsc_ref, ext_ref):
        i = pl.program_id(0)
        keep_next = (i < n_t - 1).astype(F32)

        @pl.when(i == 0)
        def _():
            dcw_ref[...] = jnp.zeros_like(dcw_ref)
            dcb_ref[...] = jnp.zeros_like(dcb_ref)
            dsc_ref[...] = jnp.zeros_like(dsc_ref)

        for g, win in enumerate(POOL_WINDOWS):
            cs = slice(g * G, (g + 1) * G)
            sc = sc_ref[:, cs]
            z = z_ref[:, cs].astype(F32)
            sz = _sig(z)
            dyy_c = dyy_ref[:, cs].astype(F32)
            ggv = gg_ref[:, cs].astype(F32)
            dyy0 = dyy_c * (z * sz)
            dq_ref[:, WC + cs.start:WC + cs.stop] = (dyy_c * (ggv * sc) * _dsilu(z, sz)).astype(BF16)
            dgg = dyy0 * sc
            dsc_ref[:, cs] += _colsum(dyy0 * ggv)
            dcb_ref[:, cs] += _colsum(dgg)
            dgg_b = dgg.astype(BF16)
            dcw_ref[g] += lax.dot_general(pooled_ref[:, cs], dgg_b, tn, preferred_element_type=F32)
            dpool = lax.dot_general(dgg_b, cw_ref[g], nt, preferred_element_type=F32)
            zf = zf_ref[:, cs].astype(F32)
            dgg_f = (dyyf_ref[:, cs].astype(F32) * (zf * _sig(zf)) * sc * keep_next).astype(BF16)
            dpool_f = lax.dot_general(dgg_f, cw_ref[g], nt, preferred_element_type=F32)
            ext_ref[0:tm, :] = dpool / _counts(i, tm, tm, 0, win)
            ext_ref[tm:tm + PHALO, :] = dpool_f / _counts(i, tm, PHALO, tm, win)
            dv = ext_ref[0:tm, :] - dpool
            for j in range(1, win):
                dv = dv + ext_ref[pl.ds(j, tm), :]
            dq_ref[:, cs] = dv.astype(BF16)

    row = lambda: pl.BlockSpec((tm, WC), lambda i: (i, 0))
    nxt = lambda j: pl.BlockSpec((PHALO, WC), lambda i: (jnp.minimum((i + 1) * nb, last_blk), j))
    return _call(
        body, grid=(n_t,),
        in_specs=[row(), nxt(0), pl.BlockSpec((tm, WC), lambda i: (i, 1)), nxt(1), row(), row(),
                  _const((NG, G, G)), _const((1, WC))],
        out_specs=[pl.BlockSpec((tm, 2 * WC), lambda i: (i, 0)), _const((NG, G, G)), _const((1, WC)), _const((1, WC))],
        out_shape=[_sds((S, 2 * WC), BF16), _sds((NG, G, G), F32), _sds((1, WC), F32), _sds((1, WC), F32)],
        operands=(dyy, dyy, q, q, gg, pooled, cw, cscale),
        scratch_shapes=[pltpu.VMEM((tm + PHALO, G), F32)],
        name=name, params=_params(("arbitrary",), 48))[0]


def _place():
    return lax.axis_index("x"), lax.axis_index("y"), lax.axis_index("c")


def _piece(ref, axis, size, index):
    start = index * size
    if axis == len(ref.shape) - 1:
        start = pl.multiple_of(start, LANES)
    idx = [slice(None)] * len(ref.shape)
    idx[axis] = pl.ds(start, size)
    return ref.at[tuple(idx)]


def _gather_copies(src, out, axis, size, send_sems, recv_sems, base, held=None):
    x, y, c = _place()
    sib, xn, yn = (x, y, 1 - c), (1 - x, y, c), (x, 1 - y, c)

    def blk(px, py, of=out):
        return _piece(of, axis, size, 4 * px + 2 * py + c)

    def half(ref, h):
        n = ref.shape[0] // 2
        return ref.at[pl.ds(h * n, n)]

    def rc(k, s, d, to):
        return pltpu.make_async_remote_copy(src_ref=s, dst_ref=d, send_sem=send_sems.at[base + k],
                                            recv_sem=recv_sems.at[base + k], device_id=to, device_id_type=MESH)

    own, xb, yb, db = blk(x, y), blk(1 - x, y), blk(x, 1 - y), blk(1 - x, 1 - y)
    got = out if held is None else held
    xs, ys, ds = blk(1 - x, y, got), blk(x, 1 - y, got), blk(1 - x, 1 - y, got)
    return [rc(0, src, own, sib), rc(1, src, own, xn), rc(2, src, own, yn),
            rc(3, half(xs, 0), half(xb, 0), yn), rc(4, half(ys, 1), half(yb, 1), xn),
            rc(5, xs, xb, sib), rc(6, ys, yb, sib), rc(7, ds, db, sib)]


N_GATHER = 8


def _gather_comm(shards, axes, phases):
    n = len(shards)
    if phases == "second":
        sizes = [s.shape[a] // N_DEV for s, a in zip(shards, axes)]
        full = [_sds(s.shape, s.dtype) for s in shards]
    else:
        sizes = [s.shape[a] for s, a in zip(shards, axes)]
        full = [_sds(s.shape[:a] + (N_DEV * s.shape[a],) + s.shape[a + 1:], s.dtype) for s, a in zip(shards, axes)]

    def plan(ins, outs, sems):
        x, y, c = _place()
        me = 4 * x + 2 * y + c
        if phases == "second":
            cps = [_gather_copies(_piece(ins[t], axes[t], sizes[t], me), outs[t], axes[t], sizes[t], sems[0], sems[1],
                                  N_GATHER * t, ins[t]) for t in range(n)]
        else:
            cps = [_gather_copies(sems[3 + t], outs[t], axes[t], sizes[t], sems[0], sems[1], N_GATHER * t)
                   for t in range(n)]
        mine = [pltpu.make_async_copy(sems[3 + t], _piece(outs[t], axes[t], sizes[t], me), sems[2].at[t])
                for t in range(n)] if phases != "second" else []
        return cps, mine

    def send_own(ins, outs, sems):
        cps, mine = plan(ins, outs, sems)
        for t in range(n):
            stage = pltpu.make_async_copy(ins[t], sems[3 + t], sems[2].at[t])
            stage.start()
            stage.wait()
            mine[t].start()
            for k in (0, 1, 2):
                cps[t][k].start()

    def pass_on(ins, outs, sems):
        cps, _ = plan(ins, outs, sems)
        for t in range(n):
            if phases == "all":
                cps[t][1].wait_recv()
            cps[t][3].start()
            cps[t][5].start()
        for t in range(n):
            if phases == "all":
                cps[t][2].wait_recv()
            cps[t][4].start()
            cps[t][6].start()

    def own_landed(ins, outs, sems):
        cps, mine = plan(ins, outs, sems)
        for t in range(n):
            for k in (0, 1, 2):
                cps[t][k].wait()
            mine[t].wait()

    def all_landed(ins, outs, sems):
        cps, mine = plan(ins, outs, sems)
        for t in range(n):
            cps[t][3].wait_recv()
            cps[t][4].wait_recv()
            cps[t][7].start()
        for t in range(n):
            for k in ((0, 5, 6, 7) if phases == "all" else (5, 6, 7)):
                cps[t][k].wait_recv()
            for k in (range(N_GATHER) if phases == "all" else range(3, N_GATHER)):
                cps[t][k].wait_send()
            if phases == "all":
                mine[t].wait()

    sems = [pltpu.SemaphoreType.DMA((N_GATHER * n,)), pltpu.SemaphoreType.DMA((N_GATHER * n,))]
    if phases != "second":
        sems.append(pltpu.SemaphoreType.DMA((n,)))
        sems += [pltpu.VMEM(s.shape, s.dtype) for s in shards]
    if phases == "all":
        return _Comm(shards, full, sems, send_own, all_landed, middle=pass_on)
    if phases == "first":
        return _Comm(shards, full, sems, send_own, own_landed)
    return _Comm(shards, full, sems, pass_on, all_landed, aliases={t: t for t in range(n)})


def _pair_comm(grads, axes, sizes):
    n = len(grads)
    outs_sds = [_sds((4,) + g.shape[:a] + (s,) + g.shape[a + 1:], g.dtype) for g, a, s in zip(grads, axes, sizes)]

    def copies(ins, outs, sems):
        send_sems, recv_sems = sems
        x, y, c = _place()
        return [pltpu.make_async_remote_copy(
            src_ref=_piece(ins[t], axes[t], sizes[t], 2 * qi + (1 - c)), dst_ref=outs[t].at[qi],
            send_sem=send_sems.at[4 * t + qi], recv_sem=recv_sems.at[4 * t + qi],
            device_id=(x, y, 1 - c), device_id_type=MESH) for t in range(n) for qi in range(4)]

    def start(ins, outs, sems):
        for cp in copies(ins, outs, sems):
            cp.start()

    def finish(ins, outs, sems):
        for cp in copies(ins, outs, sems):
            cp.wait()

    sems = [pltpu.SemaphoreType.DMA((4 * n,)), pltpu.SemaphoreType.DMA((4 * n,))]
    return _Comm(grads, outs_sds, sems, start, finish)


def _chip_comm(sums):
    n = len(sums)
    outs_sds = [_sds((3,) + s.shape[1:], s.dtype) for s in sums]

    def copies(ins, outs, sems):
        send_sems, recv_sems = sems
        x, y, c = _place()
        return [pltpu.make_async_remote_copy(
            src_ref=ins[t].at[2 * qx + qy], dst_ref=outs[t].at[j],
            send_sem=send_sems.at[3 * t + j], recv_sem=recv_sems.at[3 * t + j],
            device_id=(qx, qy, c), device_id_type=MESH)
            for t in range(n) for j, (qx, qy) in enumerate([(1 - x, y), (x, 1 - y), (1 - x, 1 - y)])]

    def start(ins, outs, sems):
        for cp in copies(ins, outs, sems):
            cp.start()

    def finish(ins, outs, sems):
        for cp in copies(ins, outs, sems):
            cp.wait()

    sems = [pltpu.SemaphoreType.DMA((3 * n,)), pltpu.SemaphoreType.DMA((3 * n,))]
    return _Comm(sums, outs_sds, sems, start, finish)


def _small_comm(small):
    def copies(ins, outs, sems):
        send_sems, recv_sems, local_sem = sems
        x, y, c = _place()
        mine = outs[0].at[4 * x + 2 * y + c]
        out = [pltpu.make_async_copy(ins[0], mine, local_sem.at[0])]
        for k in range(1, N_DEV):
            peer = (1 - x if k & 4 else x, 1 - y if k & 2 else y, 1 - c if k & 1 else c)
            out.append(pltpu.make_async_remote_copy(
                src_ref=ins[0], dst_ref=mine, send_sem=send_sems.at[k - 1], recv_sem=recv_sems.at[k - 1],
                device_id=peer, device_id_type=MESH))
        return out

    def start(ins, outs, sems):
        for cp in copies(ins, outs, sems):
            cp.start()

    def finish(ins, outs, sems):
        for cp in copies(ins, outs, sems):
            cp.wait()

    sems = [pltpu.SemaphoreType.DMA((N_DEV - 1,)), pltpu.SemaphoreType.DMA((N_DEV - 1,)), pltpu.SemaphoreType.DMA((1,))]
    return _Comm([small], [_sds((N_DEV,) + small.shape, small.dtype)], sems, start, finish)


def _small_scatter_comm(send):
    def copies(ins, outs, sems):
        send_sems, recv_sems, local_sem = sems
        x, y, c = _place()
        me = 4 * x + 2 * y + c
        out = [pltpu.make_async_copy(ins[0].at[me], outs[0].at[me], local_sem.at[0])]
        for k in range(1, N_DEV):
            px, py, pc = (1 - x if k & 4 else x, 1 - y if k & 2 else y, 1 - c if k & 1 else c)
            out.append(pltpu.make_async_remote_copy(
                src_ref=ins[0].at[4 * px + 2 * py + pc], dst_ref=outs[0].at[me], send_sem=send_sems.at[k - 1],
                recv_sem=recv_sems.at[k - 1], device_id=(px, py, pc), device_id_type=MESH))
        return out

    def start(ins, outs, sems):
        for cp in copies(ins, outs, sems):
            cp.start()

    def finish(ins, outs, sems):
        for cp in copies(ins, outs, sems):
            cp.wait()

    sems = [pltpu.SemaphoreType.DMA((N_DEV - 1,)), pltpu.SemaphoreType.DMA((N_DEV - 1,)), pltpu.SemaphoreType.DMA((1,))]
    return _Comm([send], [_sds(send.shape, send.dtype)], sems, start, finish)


def _pair_sum(c_idx, grad, recv, axis, size, split, *, name):
    nd = len(grad.shape)
    piece = grad.shape[:axis] + (size,) + grad.shape[axis + 1:]
    blk = (piece[0] // split,) + piece[1:]

    def g_map(q, r, c_ref):
        idx = [0] * nd
        idx[axis] = 2 * q + c_ref[0]
        idx[0] = idx[0] * split + r if axis == 0 else r
        return tuple(idx)

    def r_map(q, r, c_ref):
        return (q, r) + (0,) * (nd - 1)

    def body(c_ref, g_ref, r_ref, o_ref):
        o_ref[0] = (g_ref[...].astype(F32) + r_ref[0].astype(F32)).astype(BF16)

    return _call(
        body, grid=(4, split), prefetch=c_idx,
        in_specs=[pl.BlockSpec(blk, g_map), pl.BlockSpec((1,) + blk, r_map)],
        out_specs=[pl.BlockSpec((1,) + blk, r_map)], out_shape=[_sds((4,) + piece, BF16)],
        operands=(grad, recv), name=name, params=_params(("arbitrary", "arbitrary"), 32))[0][0]


def _adam_math(w, g, m, v):
    m = ADAM_B1 * m + (1.0 - ADAM_B1) * g
    v = ADAM_B2 * v + (1.0 - ADAM_B2) * (g * g)
    m_hat = m / (1.0 - ADAM_B1 ** ADAM_STEP)
    v_hat = v / (1.0 - ADAM_B2 ** ADAM_STEP)
    delta = -ADAM_LR * (m_hat / (jnp.sqrt(v_hat) + ADAM_EPS) + ADAM_WD * w)
    return delta, m, v


def _adam_big(q_idx, sums, recv, w, m, v, split, *, name, comm=None):
    shape = w.shape
    nd = len(shape)
    blk = (shape[0] // split,) + shape[1:]
    w_map = lambda r, q_ref: (r,) + (0,) * (nd - 1)
    s_map = lambda r, q_ref: (q_ref[0], r) + (0,) * (nd - 1)
    r_map = lambda r, q_ref: (0, r) + (0,) * (nd - 1)

    def body(q_ref, s_ref, r_ref, w_ref, m_ref, v_ref, g_ref, d_ref, nm_ref, nv_ref):
        g = s_ref[0].astype(F32) + r_ref[0].astype(F32) + r_ref[1].astype(F32) + r_ref[2].astype(F32)
        g_ref[...] = g
        d_ref[...], nm_ref[...], nv_ref[...] = _adam_math(w_ref[...], g, m_ref[...], v_ref[...])

    wspec = pl.BlockSpec(blk, w_map)
    return _call(
        body, grid=(split,), prefetch=q_idx,
        in_specs=[pl.BlockSpec((1,) + blk, s_map), pl.BlockSpec((3,) + blk, r_map), wspec, wspec, wspec],
        out_specs=[wspec] * 4, out_shape=[_sds(shape, F32)] * 4, operands=(sums, recv, w, m, v),
        name=name, params=_params(("arbitrary",), 32), comm=comm)


def _adam_small(parts, w, m, v, *, name):
    R = w.shape[0]

    def body(p_ref, w_ref, m_ref, v_ref, g_ref, d_ref, nm_ref, nv_ref):
        g = p_ref[0]
        for d in range(1, N_DEV):
            g = g + p_ref[d]
        g_ref[...] = g
        d_ref[...], nm_ref[...], nv_ref[...] = _adam_math(w_ref[...], g, m_ref[...], v_ref[...])

    whole = _const((R, LANES))
    return _call(
        body, grid=(1,), in_specs=[_const((N_DEV, R, LANES)), whole, whole, whole], out_specs=[whole] * 4,
        out_shape=[_sds((R, LANES), F32)] * 4, operands=(parts, w, m, v), name=name,
        params=_params(("arbitrary",), 32))[0]


def _pack(arrs):
    return jnp.concatenate([a.reshape(-1) for a in arrs]).reshape(-1, LANES)


def _unpack(packed, shapes):
    flat = packed.reshape(-1)
    out, off = [], 0
    for s in shapes:
        n = 1
        for d in s:
            n *= d
        out.append(flat[off:off + n].reshape(s))
        off += n
    return out


BIG = ("e_in", "e_out", "o_in", "o_cw", "o_out")
BIG_AXIS = dict(e_in=1, e_out=0, o_in=1, o_cw=1, o_out=0)
BIG_SPLIT = dict(e_in=8, e_out=4, o_in=4, o_cw=4, o_out=4)
REPLICATED = ("e_norm_pre", "e_norm_post", "e_b_conv_bias", "e_b_ln_g", "e_b_ln_b")
SHARDED = ("e_a_conv", "e_b_conv", "o_norm_pre", "o_norm_post", "o_c_b", "o_c_scale")
SMALL = REPLICATED + SHARDED


class _Exchange:
    def __init__(self, shards, small, order, c_idx):
        self.q_idx = order[:1]
        self.shards = shards
        self.small = small
        self.order = order
        self.c_idx = c_idx
        self.reduced = {}

    def gather(self, keys):
        return _gather_comm([self.shards[k] for k in keys], [BIG_AXIS[k] for k in keys], "all")

    def gather1(self, keys):
        return _gather_comm([self.shards[k] for k in keys], [BIG_AXIS[k] for k in keys], "first")

    def gather2(self, keys, firsts):
        return _gather_comm(firsts, [BIG_AXIS[k] for k in keys], "second")

    def pair(self, grads):
        keys = list(grads)
        return _pair_comm([grads[k] for k in keys], [BIG_AXIS[k] for k in keys],
                          [grads[k].shape[BIG_AXIS[k]] // N_DEV for k in keys])

    def pair_sums(self, grads, received):
        return {k: _pair_sum(self.c_idx, grads[k], r, BIG_AXIS[k], grads[k].shape[BIG_AXIS[k]] // N_DEV,
                             BIG_SPLIT[k], name="pair_sum_" + k) for k, r in zip(grads, received)}

    def chips(self, sums):
        return _chip_comm([sums[k] for k in sums])

    def done(self, sums, received):
        self.reduced.update({k: (sums[k], r, self.q_idx) for k, r in zip(sums, received)})


def _local_step(x, tgt, w_small, ex):
    S, D = x.shape
    tnt, tx, tw = min(TM_NT, S), min(TM_MIX, S), min(TM_WIDE, S)

    wt = {}
    p, h0, wt["e_in"], got = _gather_matmul(ex.order, x, w_small["e_norm_pre"], ex.shards["e_in"], tm=tw,
                                            name="e_in_fwd", comm=_small_comm(ex.small))
    per_dev = [_unpack(got[0][d], [w_small[k].shape for k in SHARDED]) for d in range(N_DEV)]
    sm = {k: w_small[k] for k in REPLICATED}
    for j, k in enumerate(SHARDED):
        sm[k] = jnp.concatenate([per_dev[d][j] for d in range(N_DEV)], axis=-1)
    n_groups = sm["o_c_b"].shape[0]
    sm["o_c_b"] = sm["o_c_b"].reshape(1, -1)

    W = p.shape[1] // 7
    (u, cb), got = _e_mix_fwd(p, sm["e_a_conv"], sm["e_b_conv"], sm["e_b_conv_bias"], sm["e_b_ln_g"],
                              sm["e_b_ln_b"], tm=tx, name="e_mix_fwd", comm=ex.gather(["e_out"]))
    wt["e_out"] = got[0]
    late = ["o_out", "o_cw"]
    (x1, y0), part = _out_norm_res(u, wt["e_out"], x, sm["e_norm_post"], tm=tw, name="e_out_fwd",
                                   comm=ex.gather1(late))
    q, h1, wt["o_in"], got = _gather_matmul(ex.order, x1, sm["o_norm_pre"], ex.shards["o_in"], tm=tw,
                                            name="o_in_fwd", comm=ex.gather2(late, part))
    wt.update(zip(late, got))
    yy, pooled, gg = _o_mix_fwd(q, wt["o_cw"], sm["o_c_b"], sm["o_c_scale"], tm=tw, name="o_mix_fwd")
    dout, dx2, dyy, lcol, dg_o_post = _out_loss(yy, wt["o_out"], x1, sm["o_norm_post"], tgt, tm=tx, name="o_out_loss")
    loss = (0.5 / D) * jnp.sum(lcol)

    dq, d_cw, d_cb, d_cscale = _o_mix_bwd(dyy, q, gg, pooled, wt["o_cw"], sm["o_c_scale"], tm=tw, name="o_mix_bwd")
    g_o_out, _ = _mm_tn(yy, dout, ts=tnt, tn=W, name="o_out_dw")
    ga = dict(o_out=g_o_out, o_cw=d_cw.astype(BF16))
    dh1, ra = _mm_nt(dq, wt["o_in"], tm=tnt, tk=W, name="o_in_bwd", comm=ex.pair(ga))
    sa = ex.pair_sums(ga, ra)
    (dx1, dy0, dg_o_pre, dg_e_post), ra = _pre_bwd_o(dh1, x1, dx2, y0, sm["o_norm_pre"], sm["e_norm_post"],
                                                     tm=tx, name="o_pre_bwd", comm=ex.chips(sa))
    ex.done(sa, ra)
    g_o_in, _ = _mm_tn(h1, dq, ts=tnt, tn=W, name="o_in_dw")
    gb = dict(o_in=g_o_in)
    du, rb = _mm_nt(dy0, wt["e_out"], tm=tnt, tk=W, name="e_out_bwd", comm=ex.pair(gb))
    sb = ex.pair_sums(gb, rb)
    g_e_out, _ = _mm_tn(u, dy0, ts=tnt, tn=W, name="e_out_dw")
    gc = dict(e_out=g_e_out)
    (dp, d_wa, d_wb, d_bias, d_lg, d_lb), rbc = _e_mix_bwd(
        du, p, cb, sm["e_a_conv"], sm["e_b_conv"], sm["e_b_ln_g"], sm["e_b_ln_b"], tm=tx, name="e_mix_bwd",
        comm=_merge(ex.chips(sb), ex.pair(gc)))
    ex.done(sb, rbc[:1])
    sc = ex.pair_sums(gc, rbc[1:])
    order_out = jnp.concatenate([ex.order[1:], ex.order[:1]])
    sd, from_chip, rc = _dw_reduce(order_out, h0, dp, ex.shards["e_in"].shape[1], ts=tnt, name="e_in_dw",
                                   comm=ex.chips(sc))
    ex.done(sc, rc)
    dh0, rd = _mm_nt(dp, wt["e_in"], tm=tnt, tk=W, name="e_in_bwd", comm=_diag_comm(sd, from_chip))
    ex.reduced["e_in"] = (sd, rd[0], jnp.full((1,), 3, jnp.int32))
    grad_x, dg_e_pre = _pre_bwd_e(dh0, x, dx1, sm["e_norm_pre"], tm=tw, name="e_pre_bwd")

    small = dict(e_norm_pre=dg_e_pre, e_norm_post=dg_e_post, e_a_conv=d_wa, e_b_conv=d_wb, e_b_conv_bias=d_bias,
                 e_b_ln_g=d_lg, e_b_ln_b=d_lb, o_norm_pre=dg_o_pre, o_norm_post=dg_o_post,
                 o_c_b=d_cb.reshape(n_groups, -1), o_c_scale=d_cscale)
    return loss, grad_x, small


def kernel(x, e_norm_pre, e_norm_post, e_w_in, e_a_conv, e_b_conv, e_b_conv_bias, e_b_ln_g, e_b_ln_b, e_w_out, o_norm_pre, o_norm_post, o_w_in, o_c_w, o_c_b, o_c_scale, o_w_out, loss_target, m_e_norm_pre, m_e_norm_post, m_e_w_in, m_e_a_conv, m_e_b_conv, m_e_b_conv_bias, m_e_b_ln_g, m_e_b_ln_b, m_e_w_out, m_o_norm_pre, m_o_norm_post, m_o_w_in, m_o_c_w, m_o_c_b, m_o_c_scale, m_o_w_out, v_e_norm_pre, v_e_norm_post, v_e_w_in, v_e_a_conv, v_e_b_conv, v_e_b_conv_bias, v_e_b_ln_g, v_e_b_ln_b, v_e_w_out, v_o_norm_pre, v_o_norm_post, v_o_w_in, v_o_c_w, v_o_c_b, v_o_c_scale, v_o_w_out):
    xi, yi, ci = _place()
    w_big = dict(e_in=e_w_in[0], e_out=e_w_out[0], o_in=o_w_in[0], o_cw=o_c_w[0], o_out=o_w_out[0])
    m_big = dict(e_in=m_e_w_in[0], e_out=m_e_w_out[0], o_in=m_o_w_in[0], o_cw=m_o_c_w[0], o_out=m_o_w_out[0])
    v_big = dict(e_in=v_e_w_in[0], e_out=v_e_w_out[0], o_in=v_o_w_in[0], o_cw=v_o_c_w[0], o_out=v_o_w_out[0])
    w_small = dict(e_norm_pre=e_norm_pre, e_norm_post=e_norm_post, e_b_conv_bias=e_b_conv_bias, e_b_ln_g=e_b_ln_g,
                   e_b_ln_b=e_b_ln_b, e_a_conv=e_a_conv[0], e_b_conv=e_b_conv[0], o_norm_pre=o_norm_pre,
                   o_norm_post=o_norm_post, o_c_b=o_c_b[0], o_c_scale=o_c_scale)
    m_small = dict(e_norm_pre=m_e_norm_pre, e_norm_post=m_e_norm_post, e_b_conv_bias=m_e_b_conv_bias,
                   e_b_ln_g=m_e_b_ln_g, e_b_ln_b=m_e_b_ln_b, e_a_conv=m_e_a_conv[0], e_b_conv=m_e_b_conv[0],
                   o_norm_pre=m_o_norm_pre, o_norm_post=m_o_norm_post, o_c_b=m_o_c_b[0], o_c_scale=m_o_c_scale)
    v_small = dict(e_norm_pre=v_e_norm_pre, e_norm_post=v_e_norm_post, e_b_conv_bias=v_e_b_conv_bias,
                   e_b_ln_g=v_e_b_ln_g, e_b_ln_b=v_e_b_ln_b, e_a_conv=v_e_a_conv[0], e_b_conv=v_e_b_conv[0],
                   o_norm_pre=v_o_norm_pre, o_norm_post=v_o_norm_post, o_c_b=v_o_c_b[0], o_c_scale=v_o_c_scale)

    c_idx = jnp.reshape(ci, (1,)).astype(jnp.int32)
    order = jnp.stack([2 * xi + yi, 2 * (1 - xi) + yi, 2 * xi + (1 - yi), 2 * (1 - xi) + (1 - yi)]).astype(jnp.int32)
    ex = _Exchange({k: w_big[k].astype(BF16) for k in BIG}, _pack([w_small[k] for k in SHARDED]), order, c_idx)
    loss, grad_x, g_small = _local_step(x[0], loss_target[0], w_small, ex)

    big_out = {}
    for k in BIG:
        sums, received, q_idx = ex.reduced[k]
        big_out[k] = _adam_big(q_idx, sums, received, w_big[k], m_big[k], v_big[k], BIG_SPLIT[k], name="adam_" + k)[0]

    rep = _pack([g_small[k] for k in REPLICATED])
    loss_row = jnp.pad(jnp.reshape(loss, (1, 1)), ((0, 0), (0, LANES - 1)))
    blocks = []
    for k in SHARDED:
        r, n = w_small[k].shape
        blocks.append(g_small[k].reshape(r, N_DEV, n).transpose(1, 0, 2).reshape(N_DEV, r * n))
    blocks = jnp.concatenate(blocks, axis=1).reshape(N_DEV, -1, LANES)
    head = jnp.concatenate([rep, loss_row], axis=0)
    send = jnp.concatenate([jnp.broadcast_to(head[None], (N_DEV,) + head.shape), blocks], axis=1)
    parts = _run_comm(_small_scatter_comm(send), "small_grad_exchange")[0]

    def own_rows(d):
        return jnp.concatenate([_pack([d[k] for k in REPLICATED]), jnp.ones((1, LANES), F32),
                                _pack([d[k] for k in SHARDED])], axis=0)

    res_small = _adam_small(parts, own_rows(w_small), own_rows(m_small), own_rows(v_small), name="adam_small")
    n_rep = rep.shape[0]
    loss = res_small[0][n_rep, 0]
    small_out = {k: [] for k in SMALL}
    for packed in res_small:
        for k, t in zip(REPLICATED, _unpack(packed[:n_rep], [w_small[k].shape for k in REPLICATED])):
            small_out[k].append(t)
        for k, t in zip(SHARDED, _unpack(packed[n_rep + 1:], [w_small[k].shape for k in SHARDED])):
            small_out[k].append(t)

    big_of = dict(e_w_in="e_in", e_w_out="e_out", o_w_in="o_in", o_c_w="o_cw", o_w_out="o_out")
    stacked = ("e_a_conv", "e_b_conv", "o_c_b")

    def leaf(name, which):
        if name in big_of:
            return big_out[big_of[name]][which][None]
        t = small_out[name][which]
        return t[None] if name in stacked else t

    order = ("e_norm_pre", "e_norm_post", "e_w_in", "e_a_conv", "e_b_conv", "e_b_conv_bias", "e_b_ln_g", "e_b_ln_b",
             "e_w_out", "o_norm_pre", "o_norm_post", "o_w_in", "o_c_w", "o_c_b", "o_c_scale", "o_w_out")
    outs = [loss, grad_x[None]]
    for which in range(4):
        outs += [leaf(nm, which) for nm in order]
    return tuple(outs)
```

```python
import jax
import jax.numpy as jnp
from jax import lax
from jax.experimental import pallas as pl
from jax.experimental.pallas import tpu as pltpu

F32 = jnp.float32
BF16 = jnp.bfloat16
EPS = 1e-6
MESH = pl.DeviceIdType.MESH
ANY = pl.BlockSpec(memory_space=pl.ANY)

N_DEV = 8
HALO = 32
PHALO = 16
CONV_A = 3
CONV_B = 31
POOL_WINDOWS = (2, 4, 8, 16)
LANES = 128
MIB = 1024 * 1024

ADAM_LR = 0.001
ADAM_B1 = 0.9
ADAM_B2 = 0.999
ADAM_EPS = 1e-08
ADAM_WD = 0.01
ADAM_STEP = 10

TM_NT = 1024
TM_MIX = 256
TM_WIDE = 512


def _sds(shape, dtype):
    return jax.ShapeDtypeStruct(tuple(shape), dtype)


def _params(sem, vmem_mib):
    return pltpu.CompilerParams(dimension_semantics=sem, vmem_limit_bytes=vmem_mib * MIB)


def _const(shape, single=False):
    n = len(shape)
    if single:
        return pl.BlockSpec(shape, lambda *_: (0,) * n, pipeline_mode=pl.Buffered(1))
    return pl.BlockSpec(shape, lambda *_: (0,) * n)


def _sig(v):
    return jax.nn.sigmoid(v)


def _dsilu(v, s):
    return s * (1.0 + v * (1.0 - s))


def _rms(v):
    return lax.rsqrt(jnp.mean(v * v, axis=-1, keepdims=True) + EPS)


def _norm_bwd(dn, n, r):
    return r * (dn - n * jnp.mean(dn * n, axis=-1, keepdims=True))


def _colsum(v):
    return jnp.sum(v, axis=0, keepdims=True)


class _Comm:
    def __init__(self, inputs, out_shapes, sems, start, finish, aliases=None, middle=None):
        self.inputs, self.out_shapes, self.sems = list(inputs), list(out_shapes), list(sems)
        self.start, self.finish, self.middle = start, finish, middle
        self.aliases = dict(aliases or {})


def _merge(*comms):
    comms = [c for c in comms if c is not None]
    if len(comms) <= 1:
        return comms[0] if comms else None
    spans, i0, o0, s0, aliases = [], 0, 0, 0, {}
    for c in comms:
        spans.append((i0, o0, s0))
        aliases.update({i0 + k: o0 + v for k, v in c.aliases.items()})
        i0, o0, s0 = i0 + len(c.inputs), o0 + len(c.out_shapes), s0 + len(c.sems)

    def run(which):
        def fn(ins, outs, sems):
            for c, (i, o, s) in zip(comms, spans):
                hook = getattr(c, which)
                if hook is not None:
                    hook(ins[i:i + len(c.inputs)], outs[o:o + len(c.out_shapes)], sems[s:s + len(c.sems)])
        return fn

    return _Comm([a for c in comms for a in c.inputs], [a for c in comms for a in c.out_shapes],
                 [a for c in comms for a in c.sems], run("start"), run("finish"), aliases,
                 run("middle") if any(c.middle is not None for c in comms) else None)


def _call(body, *, grid, in_specs, out_specs, out_shape, operands, name, params, scratch_shapes=(), comm=None,
          prefetch=None, own_copies_first=False):
    n_p = 0 if prefetch is None else 1
    n_i, n_o, n_s = len(in_specs), len(out_specs), len(scratch_shapes)
    if comm is None:
        comm = _Comm([], [], [], None, None)
    c_i, c_o = len(comm.inputs), len(comm.out_shapes)

    def carrier(*refs):
        pre, refs = refs[:n_p], refs[n_p:]
        ins, cins = refs[:n_i], refs[n_i:n_i + c_i]
        outs = refs[n_i + c_i:n_i + c_i + n_o]
        couts = refs[n_i + c_i + n_o:n_i + c_i + n_o + c_o]
        scr = refs[n_i + c_i + n_o + c_o:n_i + c_i + n_o + c_o + n_s]
        csems = refs[n_i + c_i + n_o + c_o + n_s:]
        ids = [pl.program_id(d) for d in range(len(grid))]
        first = ids[0] == 0
        half = ids[0] == grid[0] // 2
        last = ids[0] == grid[0] - 1
        for d in range(1, len(grid)):
            first = first & (ids[d] == 0)
            half = half & (ids[d] == 0)
            last = last & (ids[d] == grid[d] - 1)

        def start():
            if comm.start is not None:
                @pl.when(first)
                def _():
                    comm.start(cins, couts, csems)

        if not own_copies_first:
            start()
        if comm.middle is not None:
            assert grid[0] >= 2

            @pl.when(half)
            def _():
                comm.middle(cins, couts, csems)

        body(*pre, *ins, *outs, *scr)
        if own_copies_first:
            start()

        if comm.finish is not None:
            @pl.when(last)
            def _():
                comm.finish(cins, couts, csems)

    specs = dict(grid=grid, in_specs=list(in_specs) + [ANY] * c_i, out_specs=list(out_specs) + [ANY] * c_o,
                 scratch_shapes=list(scratch_shapes) + comm.sems)
    if n_p:
        specs = dict(grid_spec=pltpu.PrefetchScalarGridSpec(num_scalar_prefetch=1, **specs))
    res = pl.pallas_call(
        carrier, out_shape=list(out_shape) + comm.out_shapes,
        input_output_aliases={n_p + n_i + k: n_o + v for k, v in comm.aliases.items()},
        name=name, compiler_params=params, **specs)(*(() if prefetch is None else (prefetch,)), *operands, *comm.inputs)
    return list(res[:n_o]), list(res[n_o:])


def _run_comm(comm, name):
    c_i, c_o = len(comm.inputs), len(comm.out_shapes)

    def body(*refs):
        ins, outs, sems = refs[:c_i], refs[c_i:c_i + c_o], refs[c_i + c_o:]
        comm.start(ins, outs, sems)
        comm.finish(ins, outs, sems)

    res = pl.pallas_call(
        body, in_specs=[ANY] * c_i, out_specs=[ANY] * c_o, out_shape=comm.out_shapes, scratch_shapes=comm.sems,
        input_output_aliases=comm.aliases, name=name)(*comm.inputs)
    return list(res)


def _gather_matmul(order, x, g, shard, *, tm, name, comm=None):
    S, K = x.shape
    nb = shard.shape[1]
    n_i = S // tm

    def body(order_ref, x_ref, g_ref, shard_ref, p_ref, h_ref, full_ref, hbuf, wbuf, stage, send_sems, recv_sems,
             dma_sems):
        j, i = pl.program_id(0), pl.program_id(1)
        px, py, pc = _place()
        cps = _gather_copies(stage, full_ref, 1, nb, send_sems, recv_sems, 0)
        own = pltpu.make_async_copy(stage, _piece(full_ref, 1, nb, 4 * px + 2 * py + pc), dma_sems.at[0])
        keep_h = pltpu.make_async_copy(hbuf, h_ref, dma_sems.at[2])

        def load(src, dst):
            cp = pltpu.make_async_copy(src, dst, dma_sems.at[1])
            cp.start()
            cp.wait()

        def load_pair(qx, qy):
            load(_piece(full_ref, 1, 2 * nb, 2 * qx + qy), wbuf)

        @pl.when((j == 0) & (i == 0))
        def _():
            load(shard_ref, stage)
            own.start()
            for k in (0, 1, 2):
                cps[k].start()

        @pl.when(j == 0)
        def _():
            xx = x_ref[...]
            hbuf[i] = ((xx * _rms(xx)) * g_ref[...]).astype(BF16)

        @pl.when((j == 0) & (i == 0))
        def _():
            own.wait()
            cps[0].wait_recv()
            load_pair(px, py)

        @pl.when((j == 1) & (i == 0))
        def _():
            keep_h.start()
            cps[1].wait_recv()
            cps[3].start()
            cps[5].start()
            cps[2].wait_recv()
            cps[4].start()
            cps[6].start()
            cps[5].wait_recv()
            load_pair(1 - px, py)

        @pl.when((j == 2) & (i == 0))
        def _():
            cps[6].wait_recv()
            load_pair(px, 1 - py)

        @pl.when((j == 2) & (i == n_i // 2))
        def _():
            cps[3].wait_recv()
            cps[4].wait_recv()
            cps[7].start()

        @pl.when((j == 3) & (i == 0))
        def _():
            cps[7].wait_recv()
            load_pair(1 - px, 1 - py)

        p_ref[...] = jnp.dot(hbuf[i], wbuf[...], preferred_element_type=F32).astype(BF16)

        @pl.when((j == 3) & (i == n_i - 1))
        def _():
            for cp in cps:
                cp.wait_send()
            keep_h.wait()

    first_pass = lambda j, i, o: (jnp.where(j == 0, i, n_i - 1), 0)
    outs, extra = _call(
        body, grid=(4, n_i), prefetch=order,
        in_specs=[pl.BlockSpec((tm, K), first_pass), pl.BlockSpec((1, K), lambda j, i, o: (0, 0)), ANY],
        out_specs=[pl.BlockSpec((tm, 2 * nb), lambda j, i, o: (i, o[j])), ANY, ANY],
        out_shape=[_sds((S, N_DEV * nb), BF16), _sds((n_i, tm, K), BF16), _sds((K, N_DEV * nb), BF16)],
        operands=(x, g, shard),
        scratch_shapes=[pltpu.VMEM((n_i, tm, K), BF16), pltpu.VMEM((K, 2 * nb), BF16), pltpu.VMEM((K, nb), BF16),
                        pltpu.SemaphoreType.DMA((N_GATHER,)), pltpu.SemaphoreType.DMA((N_GATHER,)),
                        pltpu.SemaphoreType.DMA((3,))],
        name=name, params=_params(("arbitrary", "arbitrary"), 58), comm=comm, own_copies_first=True)
    return outs[0], outs[1].reshape(S, K), outs[2], extra


def _out_norm_res(u, w, x, g, *, tm, name, comm=None):
    S, K = u.shape
    D = w.shape[1]

    def body(u_ref, w_ref, x_ref, g_ref, x1_ref, y_ref):
        y = jnp.dot(u_ref[...], w_ref[...], preferred_element_type=F32)
        y_ref[...] = y.astype(BF16)
        x1_ref[...] = x_ref[...] + (y * _rms(y)) * g_ref[...]

    return _call(
        body, grid=(S // tm,),
        in_specs=[pl.BlockSpec((tm, K), lambda i: (i, 0)), _const((K, D), single=True),
                  pl.BlockSpec((tm, D), lambda i: (i, 0)), _const((1, D))],
        out_specs=[pl.BlockSpec((tm, D), lambda i: (i, 0)), pl.BlockSpec((tm, D), lambda i: (i, 0))],
        out_shape=[_sds((S, D), F32), _sds((S, D), BF16)], operands=(u, w, x, g),
        name=name, params=_params(("arbitrary",), 56), comm=comm)


def _out_loss(yy, w, x1, g, tgt, *, tm, name):
    S, K = yy.shape
    D = w.shape[1]

    def body(yy_ref, w_ref, x1_ref, g_ref, t_ref, dout_ref, dx2_ref, dyy_ref, lcol_ref, dg_ref):
        out = jnp.dot(yy_ref[...], w_ref[...], preferred_element_type=F32)
        r = _rms(out)
        n = out * r
        gg = g_ref[...]
        e = x1_ref[...] + n * gg - t_ref[...]
        dx2 = e * (1.0 / D)
        dx2_ref[...] = dx2
        dout = _norm_bwd(dx2 * gg, n, r).astype(BF16)
        dout_ref[...] = dout
        dyy_ref[...] = lax.dot_general(dout, w_ref[...], (((1,), (1,)), ((), ())),
                                       preferred_element_type=F32).astype(BF16)

        @pl.when(pl.program_id(0) == 0)
        def _():
            lcol_ref[...] = jnp.zeros_like(lcol_ref)
            dg_ref[...] = jnp.zeros_like(dg_ref)

        lcol_ref[...] += _colsum(e * e)
        dg_ref[...] += _colsum(dx2 * n)

    return _call(
        body, grid=(S // tm,),
        in_specs=[pl.BlockSpec((tm, K), lambda i: (i, 0)), _const((K, D), single=True),
                  pl.BlockSpec((tm, D), lambda i: (i, 0)), _const((1, D)),
                  pl.BlockSpec((tm, D), lambda i: (i, 0))],
        out_specs=[pl.BlockSpec((tm, D), lambda i: (i, 0)), pl.BlockSpec((tm, D), lambda i: (i, 0)),
                   pl.BlockSpec((tm, K), lambda i: (i, 0)), _const((1, D)), _const((1, D))],
        out_shape=[_sds((S, D), BF16), _sds((S, D), F32), _sds((S, K), BF16), _sds((1, D), F32), _sds((1, D), F32)],
        operands=(yy, w, x1, g, tgt), name=name, params=_params(("arbitrary",), 52))[0]


def _mm_nt(a, w, *, tm, tk, name, comm=None):
    S, N = a.shape
    D = w.shape[0]
    n_k = N // tk

    def body(a_ref, w_ref, o_ref, acc_ref):
        k = pl.program_id(1)

        @pl.when(k == 0)
        def _():
            acc_ref[...] = jnp.zeros_like(acc_ref)

        acc_ref[...] = lax.dot_general(a_ref[...], w_ref[...], (((1,), (1,)), ((), ())),
                                       preferred_element_type=F32) + acc_ref[...]

        @pl.when(k == n_k - 1)
        def _():
            o_ref[...] = acc_ref[...].astype(BF16)

    outs, extra = _call(
        body, grid=(S // tm, n_k),
        in_specs=[pl.BlockSpec((tm, tk), lambda i, k: (i, k)), pl.BlockSpec((D, tk), lambda i, k: (0, k))],
        out_specs=[pl.BlockSpec((tm, D), lambda i, k: (i, 0))],
        out_shape=[_sds((S, D), BF16)], operands=(a, w),
        scratch_shapes=[pltpu.VMEM((tm, D), F32)],
        name=name, params=_params(("arbitrary", "arbitrary"), 48), comm=comm)
    return outs[0], extra


def _mm_tn(a, b, *, ts, tn, name, comm=None):
    S, M = a.shape
    N = b.shape[1]
    n_s = S // ts

    def body(a_ref, b_ref, o_ref, acc_ref):
        s = pl.program_id(1)

        @pl.when(s == 0)
        def _():
            acc_ref[...] = jnp.zeros_like(acc_ref)

        acc_ref[...] = lax.dot_general(a_ref[...], b_ref[...], (((0,), (0,)), ((), ())),
                                       preferred_element_type=F32) + acc_ref[...]

        @pl.when(s == n_s - 1)
        def _():
            o_ref[...] = acc_ref[...].astype(BF16)

    outs, extra = _call(
        body, grid=(N // tn, n_s),
        in_specs=[pl.BlockSpec((ts, M), lambda j, s: (s, 0)), pl.BlockSpec((ts, tn), lambda j, s: (s, j))],
        out_specs=[pl.BlockSpec((M, tn), lambda j, s: (0, j))],
        out_shape=[_sds((M, N), BF16)], operands=(a, b),
        scratch_shapes=[pltpu.VMEM((M, tn), F32)],
        name=name, params=_params(("arbitrary", "arbitrary"), 48), comm=comm)
    return outs[0], extra


def _dw_reduce(order, a, b, nb, *, ts, name, comm=None):
    S, M = a.shape
    n_s = S // ts
    rows = 512

    def body(order_ref, a_ref, b_ref, sums_ref, from_sib_ref, from_chip_ref, acc, send_buf, mine_buf, recv_buf,
             sib_send, sib_recv, chip_send, chip_recv, dma_sems):
        t, s = pl.program_id(0), pl.program_id(1)
        x, y, c = _place()
        targets = [(1 - x, y, c), (x, 1 - y, c)]

        def to_sibling(k):
            return pltpu.make_async_remote_copy(
                src_ref=send_buf, dst_ref=from_sib_ref.at[k], send_sem=sib_send.at[k], recv_sem=sib_recv.at[k],
                device_id=(x, y, 1 - c), device_id_type=MESH)

        def to_chip(k):
            return pltpu.make_async_remote_copy(
                src_ref=sums_ref.at[k], dst_ref=from_chip_ref.at[k], send_sem=chip_send.at[k],
                recv_sem=chip_recv.at[k], device_id=targets[k], device_id_type=MESH)

        def finish(k):
            to_sibling(k).wait()
            get = pltpu.make_async_copy(from_sib_ref.at[k], recv_buf, dma_sems.at[0])
            get.start()
            get.wait()
            for r in range(0, M, rows):
                recv_buf[r:r + rows, :] = (mine_buf[r:r + rows, :].astype(F32)
                                           + recv_buf[r:r + rows, :].astype(F32)).astype(BF16)
            put = pltpu.make_async_copy(recv_buf, sums_ref.at[k], dma_sems.at[1])
            put.start()
            put.wait()
            if k < 2:
                to_chip(k).start()

        for k in range(3):
            @pl.when((t == k + 1) & (s == min(1, n_s - 1)))
            def _(k=k):
                finish(k)

        @pl.when(s == 0)
        def _():
            acc[...] = jnp.zeros_like(acc)

        acc[...] = lax.dot_general(a_ref[...], b_ref[...], (((0,), (0,)), ((), ())),
                                   preferred_element_type=F32) + acc[...]

        @pl.when(s == n_s - 1)
        def _():
            for r in range(0, M, rows):
                lo, hi = acc[r:r + rows, :nb], acc[r:r + rows, nb:]
                send_buf[r:r + rows, :] = jnp.where(c == 0, hi, lo).astype(BF16)
                mine_buf[r:r + rows, :] = jnp.where(c == 0, lo, hi).astype(BF16)
            to_sibling(t).start()

        @pl.when((t == 3) & (s == n_s - 1))
        def _():
            finish(3)
            to_chip(0).wait()
            to_chip(1).wait()

    piece = _sds((4, M, nb), BF16)
    outs, extra = _call(
        body, grid=(4, n_s), prefetch=order,
        in_specs=[pl.BlockSpec((ts, M), lambda t, s, o: (s, 0)), pl.BlockSpec((ts, 2 * nb), lambda t, s, o: (s, o[t]))],
        out_specs=[ANY, ANY, ANY], out_shape=[piece, piece, _sds((3, M, nb), BF16)], operands=(a, b),
        scratch_shapes=[pltpu.VMEM((M, 2 * nb), F32), pltpu.VMEM((M, nb), BF16), pltpu.VMEM((M, nb), BF16),
                        pltpu.VMEM((M, nb), BF16), pltpu.SemaphoreType.DMA((4,)), pltpu.SemaphoreType.DMA((4,)),
                        pltpu.SemaphoreType.DMA((2,)), pltpu.SemaphoreType.DMA((2,)), pltpu.SemaphoreType.DMA((2,))],
        name=name, params=_params(("arbitrary", "arbitrary"), 56), comm=comm)
    return outs[0], outs[2], extra


def _diag_comm(sums, from_chip):
    def copy(ins, outs, sems):
        x, y, c = _place()
        return pltpu.make_async_remote_copy(
            src_ref=ins[0].at[2], dst_ref=outs[0].at[2], send_sem=sems[0].at[0], recv_sem=sems[1].at[0],
            device_id=(1 - x, 1 - y, c), device_id_type=MESH)

    def start(ins, outs, sems):
        copy(ins, outs, sems).start()

    def finish(ins, outs, sems):
        copy(ins, outs, sems).wait()

    sems = [pltpu.SemaphoreType.DMA((1,)), pltpu.SemaphoreType.DMA((1,))]
    return _Comm([sums, from_chip], [_sds(from_chip.shape, from_chip.dtype)], sems, start, finish, aliases={1: 0})


def _pre_bwd_o(dh, x1, dx2, y0, g_pre, g_post, *, tm, name, comm=None):
    S, D = x1.shape

    def body(dh_ref, x1_ref, dx2_ref, y0_ref, gpre_ref, gpost_ref, dx1_ref, dy0_ref, dgpre_ref, dgpost_ref):
        @pl.when(pl.program_id(0) == 0)
        def _():
            dgpre_ref[...] = jnp.zeros_like(dgpre_ref)
            dgpost_ref[...] = jnp.zeros_like(dgpost_ref)

        dh = dh_ref[...].astype(F32)
        x1 = x1_ref[...]
        r2 = _rms(x1)
        xn = x1 * r2
        dgpre_ref[...] += _colsum(dh * xn)
        dx1 = dx2_ref[...] + _norm_bwd(dh * gpre_ref[...], xn, r2)
        dx1_ref[...] = dx1
        y = y0_ref[...].astype(F32)
        r1 = _rms(y)
        n1 = y * r1
        dgpost_ref[...] += _colsum(dx1 * n1)
        dy0_ref[...] = _norm_bwd(dx1 * gpost_ref[...], n1, r1).astype(BF16)

    row = pl.BlockSpec((tm, D), lambda i: (i, 0))
    return _call(
        body, grid=(S // tm,),
        in_specs=[row, row, row, row, _const((1, D)), _const((1, D))],
        out_specs=[row, row, _const((1, D)), _const((1, D))],
        out_shape=[_sds((S, D), F32), _sds((S, D), BF16), _sds((1, D), F32), _sds((1, D), F32)],
        operands=(dh, x1, dx2, y0, g_pre, g_post),
        name=name, params=_params(("arbitrary",), 48), comm=comm)


def _pre_bwd_e(dh, x, dx1, g_pre, *, tm, name):
    S, D = x.shape

    def body(dh_ref, x_ref, dx1_ref, gpre_ref, gx_ref, dgpre_ref):
        @pl.when(pl.program_id(0) == 0)
        def _():
            dgpre_ref[...] = jnp.zeros_like(dgpre_ref)

        dh = dh_ref[...].astype(F32)
        xx = x_ref[...]
        r0 = _rms(xx)
        xn = xx * r0
        dgpre_ref[...] += _colsum(dh * xn)
        gx_ref[...] = dx1_ref[...] + _norm_bwd(dh * gpre_ref[...], xn, r0)

    row = pl.BlockSpec((tm, D), lambda i: (i, 0))
    return _call(
        body, grid=(S // tm,),
        in_specs=[row, row, row, _const((1, D))],
        out_specs=[row, _const((1, D))],
        out_shape=[_sds((S, D), F32), _sds((1, D), F32)],
        operands=(dh, x, dx1, g_pre), name=name, params=_params(("arbitrary",), 56))[0]


SUBLANES = 8


def _shift_copies(sh_ref, ext_ref, cs):
    for b in range(1, SUBLANES):
        sh_ref[b - 1] = ext_ref[pl.ds(b, sh_ref.shape[1]), cs]


def _rows_at(ext_ref, sh_ref, off, cs, tm):
    b = off % SUBLANES
    if b == 0 or sh_ref is None:
        return ext_ref[pl.ds(off, tm), cs]
    return sh_ref[b - 1, pl.ds(off - b, tm), :]


def _taps(ext_ref, w_ref, n_taps, base, cs, tm, sh_ref=None):
    acc = _rows_at(ext_ref, sh_ref, base, cs, tm) * w_ref[0:1, cs]
    for k in range(1, n_taps):
        acc = acc + _rows_at(ext_ref, sh_ref, base + k, cs, tm) * w_ref[k:k + 1, cs]
    return acc


def _taps_rev(ext_ref, w_ref, n_taps, cs, tm, sh_ref=None):
    acc = _rows_at(ext_ref, sh_ref, n_taps - 1, cs, tm) * w_ref[0:1, cs]
    for k in range(1, n_taps):
        acc = acc + _rows_at(ext_ref, sh_ref, n_taps - 1 - k, cs, tm) * w_ref[k:k + 1, cs]
    return acc


def _e_mix_fwd(p, wa, wb, bias, ln_g, ln_b, *, tm, name, comm=None):
    S = p.shape[0]
    W = p.shape[1] // 7
    nb = tm // HALO
    chunks = [slice(c * LANES, (c + 1) * LANES) for c in range(W // LANES)]

    def body(p_ref, hax_ref, hac_ref, hbv_ref, hbg_ref, wa_ref, wb_ref, bias_ref, lg_ref, lb_ref,
             u_ref, cb_ref, ext_ref, sh_ref):
        keep = (pl.program_id(0) > 0).astype(F32)
        col = lambda j, cs: p_ref[:, j * W + cs.start:j * W + cs.stop].astype(F32)

        ext_ref[0:HALO, :] = hax_ref[...].astype(F32) * hac_ref[...].astype(F32) * keep
        ext_ref[HALO:, :] = p_ref[:, 2 * W:3 * W].astype(F32) * p_ref[:, 0:W].astype(F32)
        for cs in chunks:
            conv = _taps(ext_ref, wa_ref, CONV_A, HALO - (CONV_A - 1), cs, tm)
            az = col(3, cs)
            u_ref[:, cs] = (col(1, cs) * conv * (az * _sig(az))).astype(BF16)

        ext_ref[0:HALO, :] = hbv_ref[...].astype(F32) * _sig(hbg_ref[...].astype(F32)) * keep
        ext_ref[HALO:, :] = p_ref[:, 4 * W:5 * W].astype(F32) * _sig(p_ref[:, 5 * W:6 * W].astype(F32))
        s1 = jnp.zeros((tm, LANES), F32)
        for cs in chunks:
            _shift_copies(sh_ref, ext_ref, cs)
            cb = _taps(ext_ref, wb_ref, CONV_B, HALO - (CONV_B - 1), cs, tm, sh_ref) + bias_ref[:, cs]
            cb_ref[:, cs] = cb
            s1 = s1 + cb
        mu = jnp.sum(s1, axis=-1, keepdims=True) * (1.0 / W)
        s2 = jnp.zeros((tm, LANES), F32)
        for cs in chunks:
            xc = cb_ref[:, cs] - mu
            s2 = s2 + xc * xc
        rs = lax.rsqrt(jnp.sum(s2, axis=-1, keepdims=True) * (1.0 / W) + EPS)
        for cs in chunks:
            lb = (cb_ref[:, cs] - mu) * rs * lg_ref[:, cs] + lb_ref[:, cs]
            bz = col(6, cs)
            u_ref[:, W + cs.start:W + cs.stop] = (lb * _sig(lb) * (bz * _sig(bz))).astype(BF16)

    prev = lambda j: pl.BlockSpec((HALO, W), lambda i: (jnp.maximum(i * nb - 1, 0), j))
    return _call(
        body, grid=(S // tm,),
        in_specs=[pl.BlockSpec((tm, 7 * W), lambda i: (i, 0)), prev(0), prev(2), prev(4), prev(5),
                  _const((CONV_A, W)), _const((CONV_B, W)), _const((1, W)), _const((1, W)), _const((1, W))],
        out_specs=[pl.BlockSpec((tm, 2 * W), lambda i: (i, 0)), pl.BlockSpec((tm, W), lambda i: (i, 0))],
        out_shape=[_sds((S, 2 * W), BF16), _sds((S, W), F32)],
        operands=(p, p, p, p, p, wa, wb, bias, ln_g, ln_b),
        scratch_shapes=[pltpu.VMEM((HALO + tm, W), F32),
                        pltpu.VMEM((SUBLANES - 1, HALO + tm - SUBLANES, LANES), F32)],
        name=name, params=_params(("arbitrary",), 48), comm=comm)


def _e_mix_bwd(du, p, cb, wa, wb, ln_g, ln_b, *, tm, name, comm=None):
    S = p.shape[0]
    W = p.shape[1] // 7
    nb = tm // HALO
    n_t = S // tm
    last_blk = S // HALO - 1
    chunks = [slice(c * LANES, (c + 1) * LANES) for c in range(W // LANES)]

    def body(du_ref, duf_ref, p_ref, fab_ref, faz_ref, fbz_ref, hax_ref, hac_ref, hbv_ref, hbg_ref,
             cb_ref, cbf_ref, wa_ref, wb_ref, lg_ref, lb_ref,
             dp_ref, dwa_ref, dwb_ref, dbias_ref, dlg_ref, dlb_ref, extd_ref, extg_ref, shd_ref, shg_ref):
        i = pl.program_id(0)
        keep_prev = (i > 0).astype(F32)
        keep_next = (i < n_t - 1).astype(F32)
        col = lambda j, cs: p_ref[:, j * W + cs.start:j * W + cs.stop].astype(F32)

        @pl.when(i == 0)
        def _():
            dwa_ref[...] = jnp.zeros_like(dwa_ref)
            dwb_ref[...] = jnp.zeros_like(dwb_ref)
            dbias_ref[...] = jnp.zeros_like(dbias_ref)
            dlg_ref[...] = jnp.zeros_like(dlg_ref)
            dlb_ref[...] = jnp.zeros_like(dlb_ref)

        def dcb_rows(rows, cb_rows_ref, dub, bz_of, dst0, scale, main):
            cbv = cb_rows_ref[...]
            mu = jnp.mean(cbv, axis=-1, keepdims=True)
            xc = cbv - mu
            rs = lax.rsqrt(jnp.mean(xc * xc, axis=-1, keepdims=True) + EPS)
            m1 = jnp.zeros((rows, LANES), F32)
            m2 = jnp.zeros((rows, LANES), F32)
            for cs in chunks:
                nbv = (cb_rows_ref[:, cs] - mu) * rs
                lb = nbv * lg_ref[:, cs] + lb_ref[:, cs]
                sl = _sig(lb)
                bz = bz_of(cs)
                sz = _sig(bz)
                dub_c = dub(cs)
                dlb = dub_c * (bz * sz) * _dsilu(lb, sl)
                if main:
                    dlg_ref[:, cs] += _colsum(dlb * nbv)
                    dlb_ref[:, cs] += _colsum(dlb)
                    dp_ref[:, 6 * W + cs.start:6 * W + cs.stop] = (dub_c * (lb * sl) * _dsilu(bz, sz)).astype(BF16)
                dnb = dlb * lg_ref[:, cs]
                extd_ref[dst0:dst0 + rows, cs] = dnb
                m1 = m1 + dnb
                m2 = m2 + dnb * nbv
            m1 = jnp.sum(m1, axis=-1, keepdims=True) * (1.0 / W)
            m2 = jnp.sum(m2, axis=-1, keepdims=True) * (1.0 / W)
            for cs in chunks:
                nbv = (cb_rows_ref[:, cs] - mu) * rs
                dcb = rs * (extd_ref[dst0:dst0 + rows, cs] - m1 - nbv * m2) * scale
                extd_ref[dst0:dst0 + rows, cs] = dcb
                if main:
                    dbias_ref[:, cs] += _colsum(dcb)

        dcb_rows(tm, cb_ref, lambda cs: du_ref[:, W + cs.start:W + cs.stop].astype(F32),
                 lambda cs: col(6, cs), 0, 1.0, True)
        dcb_rows(HALO, cbf_ref, lambda cs: duf_ref[:, W + cs.start:W + cs.stop].astype(F32),
                 lambda cs: fbz_ref[:, cs].astype(F32), tm, keep_next, False)

        extg_ref[0:HALO, :] = hbv_ref[...].astype(F32) * _sig(hbg_ref[...].astype(F32)) * keep_prev
        extg_ref[HALO:, :] = p_ref[:, 4 * W:5 * W].astype(F32) * _sig(p_ref[:, 5 * W:6 * W].astype(F32))
        base_b = HALO - (CONV_B - 1)
        for cs in chunks:
            _shift_copies(shd_ref, extd_ref, cs)
            _shift_copies(shg_ref, extg_ref, cs)
            dgb = _taps_rev(extd_ref, wb_ref, CONV_B, cs, tm, shd_ref)
            bv = col(4, cs)
            sg = _sig(col(5, cs))
            dp_ref[:, 4 * W + cs.start:4 * W + cs.stop] = (dgb * sg).astype(BF16)
            dp_ref[:, 5 * W + cs.start:5 * W + cs.stop] = (dgb * bv * sg * (1.0 - sg)).astype(BF16)
            dcb = extd_ref[0:tm, cs]
            for k in range(CONV_B):
                dwb_ref[k:k + 1, cs] += _colsum(dcb * _rows_at(extg_ref, shg_ref, base_b + k, cs, tm))

        extg_ref[0:HALO, :] = hax_ref[...].astype(F32) * hac_ref[...].astype(F32) * keep_prev
        extg_ref[HALO:, :] = p_ref[:, 2 * W:3 * W].astype(F32) * p_ref[:, 0:W].astype(F32)
        base_a = HALO - (CONV_A - 1)
        for cs in chunks:
            conv = _taps(extg_ref, wa_ref, CONV_A, base_a, cs, tm)
            az = col(3, cs)
            sz = _sig(az)
            ab = col(1, cs)
            dua = du_ref[:, cs].astype(F32)
            dya = dua * (az * sz)
            dp_ref[:, W + cs.start:W + cs.stop] = (dya * conv).astype(BF16)
            dp_ref[:, 3 * W + cs.start:3 * W + cs.stop] = (dua * (ab * conv) * _dsilu(az, sz)).astype(BF16)
            extd_ref[0:tm, cs] = dya * ab
            azf = faz_ref[:, cs].astype(F32)
            extd_ref[tm:tm + HALO, cs] = (duf_ref[:, cs].astype(F32) * (azf * _sig(azf))
                                          * fab_ref[:, cs].astype(F32) * keep_next)
        for cs in chunks:
            dca = _taps_rev(extd_ref, wa_ref, CONV_A, cs, tm)
            dp_ref[:, cs] = (dca * col(2, cs)).astype(BF16)
            dp_ref[:, 2 * W + cs.start:2 * W + cs.stop] = (dca * col(0, cs)).astype(BF16)
            dconv = extd_ref[0:tm, cs]
            for k in range(CONV_A):
                dwa_ref[k:k + 1, cs] += _colsum(dconv * extg_ref[pl.ds(base_a + k, tm), cs])

    prev = lambda j: pl.BlockSpec((HALO, W), lambda i: (jnp.maximum(i * nb - 1, 0), j))
    nxt = lambda j, w: pl.BlockSpec((HALO, w), lambda i: (jnp.minimum((i + 1) * nb, last_blk), j))
    row = lambda w: pl.BlockSpec((tm, w), lambda i: (i, 0))
    return _call(
        body, grid=(n_t,),
        in_specs=[row(2 * W), nxt(0, 2 * W), row(7 * W), nxt(1, W), nxt(3, W), nxt(6, W),
                  prev(0), prev(2), prev(4), prev(5), row(W), nxt(0, W),
                  _const((CONV_A, W)), _const((CONV_B, W)), _const((1, W)), _const((1, W))],
        out_specs=[row(7 * W), _const((CONV_A, W)), _const((CONV_B, W)), _const((1, W)), _const((1, W)), _const((1, W))],
        out_shape=[_sds((S, 7 * W), BF16), _sds((CONV_A, W), F32), _sds((CONV_B, W), F32),
                   _sds((1, W), F32), _sds((1, W), F32), _sds((1, W), F32)],
        operands=(du, du, p, p, p, p, p, p, p, p, cb, cb, wa, wb, ln_g, ln_b),
        scratch_shapes=[pltpu.VMEM((tm + HALO, W), F32), pltpu.VMEM((HALO + tm, W), F32),
                        pltpu.VMEM((SUBLANES - 1, HALO + tm - SUBLANES, LANES), F32),
                        pltpu.VMEM((SUBLANES - 1, HALO + tm - SUBLANES, LANES), F32)],
        name=name, params=_params(("arbitrary",), 52), comm=comm)


def _counts(i, tm, rows, off, win):
    t = i * tm + off + lax.broadcasted_iota(jnp.int32, (rows, 1), 0)
    return jnp.minimum(t + 1, win).astype(F32)


def _o_mix_fwd(q, cw, cb, cscale, *, tm, name):
    S = q.shape[0]
    WC = q.shape[1] // 2
    NG = len(POOL_WINDOWS)
    G = WC // NG
    nb = tm // PHALO

    def body(v_ref, z_ref, hv_ref, cw_ref, cb_ref, sc_ref, yy_ref, pooled_ref, gg_ref, ext_ref):
        i = pl.program_id(0)
        keep = (i > 0).astype(F32)
        for g, win in enumerate(POOL_WINDOWS):
            cs = slice(g * G, (g + 1) * G)
            v = v_ref[:, cs].astype(F32)
            ext_ref[0:PHALO, :] = hv_ref[:, cs].astype(F32) * keep
            ext_ref[PHALO:, :] = v
            s = v
            for j in range(1, win):
                s = s + ext_ref[pl.ds(PHALO - j, tm), :]
            pooled = (s / _counts(i, tm, tm, 0, win) - v).astype(BF16)
            pooled_ref[:, cs] = pooled
            gg = jnp.dot(pooled, cw_ref[g], preferred_element_type=F32) + cb_ref[:, cs]
            gg_ref[:, cs] = gg.astype(BF16)
            z = z_ref[:, cs].astype(F32)
            yy_ref[:, cs] = (gg * sc_ref[:, cs] * (z * _sig(z))).astype(BF16)

    row = lambda j: pl.BlockSpec((tm, WC), lambda i: (i, j))
    out = pl.BlockSpec((tm, WC), lambda i: (i, 0))
    return _call(
        body, grid=(S // tm,),
        in_specs=[row(0), row(1), pl.BlockSpec((PHALO, WC), lambda i: (jnp.maximum(i * nb - 1, 0), 0)),
                  _const((NG, G, G)), _const((1, WC)), _const((1, WC))],
        out_specs=[out, out, out],
        out_shape=[_sds((S, WC), BF16)] * 3, operands=(q, q, q, cw, cb, cscale),
        scratch_shapes=[pltpu.VMEM((PHALO + tm, G), F32)],
        name=name, params=_params(("arbitrary",), 40))[0]


def _o_mix_bwd(dyy, q, gg, pooled, cw, cscale, *, tm, name):
    S = q.shape[0]
    WC = q.shape[1] // 2
    NG = len(POOL_WINDOWS)
    G = WC // NG
    nb = tm // PHALO
    n_t = S // tm
    last_blk = S // PHALO - 1
    nt = (((1,), (1,)), ((), ()))
    tn = (((0,), (0,)), ((), ()))

    def body(dyy_ref, dyyf_ref, z_ref, zf_ref, gg_ref, pooled_ref, cw_ref, sc_ref,
             dq_ref, dcw_ref, dcb_ref, dsc_ref, ext_ref):
        i = pl.program_id(0)
        keep_next = (i < n_t - 1).astype(F32)

        @pl.when(i == 0)
        def _():
            dcw_ref[...] = jnp.zeros_like(dcw_ref)
            dcb_ref[...] = jnp.zeros_like(dcb_ref)
            dsc_ref[...] = jnp.zeros_like(dsc_ref)

        for g, win in enumerate(POOL_WINDOWS):
            cs = slice(g * G, (g + 1) * G)
            sc = sc_ref[:, cs]
            z = z_ref[:, cs].astype(F32)
            sz = _sig(z)
            dyy_c = dyy_ref[:, cs].astype(F32)
            ggv = gg_ref[:, cs].astype(F32)
            dyy0 = dyy_c * (z * sz)
            dq_ref[:, WC + cs.start:WC + cs.stop] = (dyy_c * (ggv * sc) * _dsilu(z, sz)).astype(BF16)
            dgg = dyy0 * sc
            dsc_ref[:, cs] += _colsum(dyy0 * ggv)
            dcb_ref[:, cs] += _colsum(dgg)
            dgg_b = dgg.astype(BF16)
            dcw_ref[g] += lax.dot_general(pooled_ref[:, cs], dgg_b, tn, preferred_element_type=F32)
            dpool = lax.dot_general(dgg_b, cw_ref[g], nt, preferred_element_type=F32)
            zf = zf_ref[:, cs].astype(F32)
            dgg_f = (dyyf_ref[:, cs].astype(F32) * (zf * _sig(zf)) * sc * keep_next).astype(BF16)
            dpool_f = lax.dot_general(dgg_f, cw_ref[g], nt, preferred_element_type=F32)
            ext_ref[0:tm, :] = dpool / _counts(i, tm, tm, 0, win)
            ext_ref[tm:tm + PHALO, :] = dpool_f / _counts(i, tm, PHALO, tm, win)
            dv = ext_ref[0:tm, :] - dpool
            for j in range(1, win):
                dv = dv + ext_ref[pl.ds(j, tm), :]
            dq_ref[:, cs] = dv.astype(BF16)

    row = lambda: pl.BlockSpec((tm, WC), lambda i: (i, 0))
    nxt = lambda j: pl.BlockSpec((PHALO, WC), lambda i: (jnp.minimum((i + 1) * nb, last_blk), j))
    return _call(
        body, grid=(n_t,),
        in_specs=[row(), nxt(0), pl.BlockSpec((tm, WC), lambda i: (i, 1)), nxt(1), row(), row(),
                  _const((NG, G, G)), _const((1, WC))],
        out_specs=[pl.BlockSpec((tm, 2 * WC), lambda i: (i, 0)), _const((NG, G, G)), _const((1, WC)), _const((1, WC))],
        out_shape=[_sds((S, 2 * WC), BF16), _sds((NG, G, G), F32), _sds((1, WC), F32), _sds((1, WC), F32)],
        operands=(dyy, dyy, q, q, gg, pooled, cw, cscale),
        scratch_shapes=[pltpu.VMEM((tm + PHALO, G), F32)],
        name=name, params=_params(("arbitrary",), 48))[0]


def _place():
    return lax.axis_index("x"), lax.axis_index("y"), lax.axis_index("c")


def _piece(ref, axis, size, index):
    start = index * size
    if axis == len(ref.shape) - 1:
        start = pl.multiple_of(start, LANES)
    idx = [slice(None)] * len(ref.shape)
    idx[axis] = pl.ds(start, size)
    return ref.at[tuple(idx)]


def _gather_copies(src, out, axis, size, send_sems, recv_sems, base, held=None):
    x, y, c = _place()
    sib, xn, yn = (x, y, 1 - c), (1 - x, y, c), (x, 1 - y, c)

    def blk(px, py, of=out):
        return _piece(of, axis, size, 4 * px + 2 * py + c)

    def half(ref, h):
        n = ref.shape[0] // 2
        return ref.at[pl.ds(h * n, n)]

    def rc(k, s, d, to):
        return pltpu.make_async_remote_copy(src_ref=s, dst_ref=d, send_sem=send_sems.at[base + k],
                                            recv_sem=recv_sems.at[base + k], device_id=to, device_id_type=MESH)

    own, xb, yb, db = blk(x, y), blk(1 - x, y), blk(x, 1 - y), blk(1 - x, 1 - y)
    got = out if held is None else held
    xs, ys, ds = blk(1 - x, y, got), blk(x, 1 - y, got), blk(1 - x, 1 - y, got)
    return [rc(0, src, own, sib), rc(1, src, own, xn), rc(2, src, own, yn),
            rc(3, half(xs, 0), half(xb, 0), yn), rc(4, half(ys, 1), half(yb, 1), xn),
            rc(5, xs, xb, sib), rc(6, ys, yb, sib), rc(7, ds, db, sib)]


N_GATHER = 8


def _gather_comm(shards, axes, phases):
    n = len(shards)
    if phases == "second":
        sizes = [s.shape[a] // N_DEV for s, a in zip(shards, axes)]
        full = [_sds(s.shape, s.dtype) for s in shards]
    else:
        sizes = [s.shape[a] for s, a in zip(shards, axes)]
        full = [_sds(s.shape[:a] + (N_DEV * s.shape[a],) + s.shape[a + 1:], s.dtype) for s, a in zip(shards, axes)]

    def plan(ins, outs, sems):
        x, y, c = _place()
        me = 4 * x + 2 * y + c
        if phases == "second":
            cps = [_gather_copies(_piece(ins[t], axes[t], sizes[t], me), outs[t], axes[t], sizes[t], sems[0], sems[1],
                                  N_GATHER * t, ins[t]) for t in range(n)]
        else:
            cps = [_gather_copies(sems[3 + t], outs[t], axes[t], sizes[t], sems[0], sems[1], N_GATHER * t)
                   for t in range(n)]
        mine = [pltpu.make_async_copy(sems[3 + t], _piece(outs[t], axes[t], sizes[t], me), sems[2].at[t])
                for t in range(n)] if phases != "second" else []
        return cps, mine

    def send_own(ins, outs, sems):
        cps, mine = plan(ins, outs, sems)
        for t in range(n):
            stage = pltpu.make_async_copy(ins[t], sems[3 + t], sems[2].at[t])
            stage.start()
            stage.wait()
            mine[t].start()
            for k in (0, 1, 2):
                cps[t][k].start()

    def pass_on(ins, outs, sems):
        cps, _ = plan(ins, outs, sems)
        for t in range(n):
            if phases == "all":
                cps[t][1].wait_recv()
            cps[t][3].start()
            cps[t][5].start()
        for t in range(n):
            if phases == "all":
                cps[t][2].wait_recv()
            cps[t][4].start()
            cps[t][6].start()

    def own_landed(ins, outs, sems):
        cps, mine = plan(ins, outs, sems)
        for t in range(n):
            for k in (0, 1, 2):
                cps[t][k].wait()
            mine[t].wait()

    def all_landed(ins, outs, sems):
        cps, mine = plan(ins, outs, sems)
        for t in range(n):
            cps[t][3].wait_recv()
            cps[t][4].wait_recv()
            cps[t][7].start()
        for t in range(n):
            for k in ((0, 5, 6, 7) if phases == "all" else (5, 6, 7)):
                cps[t][k].wait_recv()
            for k in (range(N_GATHER) if phases == "all" else range(3, N_GATHER)):
                cps[t][k].wait_send()
            if phases == "all":
                mine[t].wait()

    sems = [pltpu.SemaphoreType.DMA((N_GATHER * n,)), pltpu.SemaphoreType.DMA((N_GATHER * n,))]
    if phases != "second":
        sems.append(pltpu.SemaphoreType.DMA((n,)))
        sems += [pltpu.VMEM(s.shape, s.dtype) for s in shards]
    if phases == "all":
        return _Comm(shards, full, sems, send_own, all_landed, middle=pass_on)
    if phases == "first":
        return _Comm(shards, full, sems, send_own, own_landed)
    return _Comm(shards, full, sems, pass_on, all_landed, aliases={t: t for t in range(n)})


def _pair_comm(grads, axes, sizes):
    n = len(grads)
    outs_sds = [_sds((4,) + g.shape[:a] + (s,) + g.shape[a + 1:], g.dtype) for g, a, s in zip(grads, axes, sizes)]

    def copies(ins, outs, sems):
        send_sems, recv_sems = sems
        x, y, c = _place()
        return [pltpu.make_async_remote_copy(
            src_ref=_piece(ins[t], axes[t], sizes[t], 2 * qi + (1 - c)), dst_ref=outs[t].at[qi],
            send_sem=send_sems.at[4 * t + qi], recv_sem=recv_sems.at[4 * t + qi],
            device_id=(x, y, 1 - c), device_id_type=MESH) for t in range(n) for qi in range(4)]

    def start(ins, outs, sems):
        for cp in copies(ins, outs, sems):
            cp.start()

    def finish(ins, outs, sems):
        for cp in copies(ins, outs, sems):
            cp.wait()

    sems = [pltpu.SemaphoreType.DMA((4 * n,)), pltpu.SemaphoreType.DMA((4 * n,))]
    return _Comm(grads, outs_sds, sems, start, finish)


def _chip_comm(sums):
    n = len(sums)
    outs_sds = [_sds((3,) + s.shape[1:], s.dtype) for s in sums]

    def copies(ins, outs, sems):
        send_sems, recv_sems = sems
        x, y, c = _place()
        return [pltpu.make_async_remote_copy(
            src_ref=ins[t].at[2 * qx + qy], dst_ref=outs[t].at[j],
            send_sem=send_sems.at[3 * t + j], recv_sem=recv_sems.at[3 * t + j],
            device_id=(qx, qy, c), device_id_type=MESH)
            for t in range(n) for j, (qx, qy) in enumerate([(1 - x, y), (x, 1 - y), (1 - x, 1 - y)])]

    def start(ins, outs, sems):
        for cp in copies(ins, outs, sems):
            cp.start()

    def finish(ins, outs, sems):
        for cp in copies(ins, outs, sems):
            cp.wait()

    sems = [pltpu.SemaphoreType.DMA((3 * n,)), pltpu.SemaphoreType.DMA((3 * n,))]
    return _Comm(sums, outs_sds, sems, start, finish)


def _small_comm(small):
    def copies(ins, outs, sems):
        send_sems, recv_sems, local_sem = sems
        x, y, c = _place()
        mine = outs[0].at[4 * x + 2 * y + c]
        out = [pltpu.make_async_copy(ins[0], mine, local_sem.at[0])]
        for k in range(1, N_DEV):
            peer = (1 - x if k & 4 else x, 1 - y if k & 2 else y, 1 - c if k & 1 else c)
            out.append(pltpu.make_async_remote_copy(
                src_ref=ins[0], dst_ref=mine, send_sem=send_sems.at[k - 1], recv_sem=recv_sems.at[k - 1],
                device_id=peer, device_id_type=MESH))
        return out

    def start(ins, outs, sems):
        for cp in copies(ins, outs, sems):
            cp.start()

    def finish(ins, outs, sems):
        for cp in copies(ins, outs, sems):
            cp.wait()

    sems = [pltpu.SemaphoreType.DMA((N_DEV - 1,)), pltpu.SemaphoreType.DMA((N_DEV - 1,)), pltpu.SemaphoreType.DMA((1,))]
    return _Comm([small], [_sds((N_DEV,) + small.shape, small.dtype)], sems, start, finish)


def _small_scatter_comm(send):
    def copies(ins, outs, sems):
        send_sems, recv_sems, local_sem = sems
        x, y, c = _place()
        me = 4 * x + 2 * y + c
        out = [pltpu.make_async_copy(ins[0].at[me], outs[0].at[me], local_sem.at[0])]
        for k in range(1, N_DEV):
            px, py, pc = (1 - x if k & 4 else x, 1 - y if k & 2 else y, 1 - c if k & 1 else c)
            out.append(pltpu.make_async_remote_copy(
                src_ref=ins[0].at[4 * px + 2 * py + pc], dst_ref=outs[0].at[me], send_sem=send_sems.at[k - 1],
                recv_sem=recv_sems.at[k - 1], device_id=(px, py, pc), device_id_type=MESH))
        return out

    def start(ins, outs, sems):
        for cp in copies(ins, outs, sems):
            cp.start()

    def finish(ins, outs, sems):
        for cp in copies(ins, outs, sems):
            cp.wait()

    sems = [pltpu.SemaphoreType.DMA((N_DEV - 1,)), pltpu.SemaphoreType.DMA((N_DEV - 1,)), pltpu.SemaphoreType.DMA((1,))]
    return _Comm([send], [_sds(send.shape, send.dtype)], sems, start, finish)


def _pair_sum(c_idx, grad, recv, axis, size, split, *, name):
    nd = len(grad.shape)
    piece = grad.shape[:axis] + (size,) + grad.shape[axis + 1:]
    blk = (piece[0] // split,) + piece[1:]

    def g_map(q, r, c_ref):
        idx = [0] * nd
        idx[axis] = 2 * q + c_ref[0]
        idx[0] = idx[0] * split + r if axis == 0 else r
        return tuple(idx)

    def r_map(q, r, c_ref):
        return (q, r) + (0,) * (nd - 1)

    def body(c_ref, g_ref, r_ref, o_ref):
        o_ref[0] = (g_ref[...].astype(F32) + r_ref[0].astype(F32)).astype(BF16)

    return _call(
        body, grid=(4, split), prefetch=c_idx,
        in_specs=[pl.BlockSpec(blk, g_map), pl.BlockSpec((1,) + blk, r_map)],
        out_specs=[pl.BlockSpec((1,) + blk, r_map)], out_shape=[_sds((4,) + piece, BF16)],
        operands=(grad, recv), name=name, params=_params(("arbitrary", "arbitrary"), 32))[0][0]


def _adam_math(w, g, m, v):
    m = ADAM_B1 * m + (1.0 - ADAM_B1) * g
    v = ADAM_B2 * v + (1.0 - ADAM_B2) * (g * g)
    m_hat = m / (1.0 - ADAM_B1 ** ADAM_STEP)
    v_hat = v / (1.0 - ADAM_B2 ** ADAM_STEP)
    delta = -ADAM_LR * (m_hat / (jnp.sqrt(v_hat) + ADAM_EPS) + ADAM_WD * w)
    return delta, m, v


def _adam_big(q_idx, sums, recv, w, m, v, split, *, name, comm=None):
    shape = w.shape
    nd = len(shape)
    blk = (shape[0] // split,) + shape[1:]
    w_map = lambda r, q_ref: (r,) + (0,) * (nd - 1)
    s_map = lambda r, q_ref: (q_ref[0], r) + (0,) * (nd - 1)
    r_map = lambda r, q_ref: (0, r) + (0,) * (nd - 1)

    def body(q_ref, s_ref, r_ref, w_ref, m_ref, v_ref, g_ref, d_ref, nm_ref, nv_ref):
        g = s_ref[0].astype(F32) + r_ref[0].astype(F32) + r_ref[1].astype(F32) + r_ref[2].astype(F32)
        g_ref[...] = g
        d_ref[...], nm_ref[...], nv_ref[...] = _adam_math(w_ref[...], g, m_ref[...], v_ref[...])

    wspec = pl.BlockSpec(blk, w_map)
    return _call(
        body, grid=(split,), prefetch=q_idx,
        in_specs=[pl.BlockSpec((1,) + blk, s_map), pl.BlockSpec((3,) + blk, r_map), wspec, wspec, wspec],
        out_specs=[wspec] * 4, out_shape=[_sds(shape, F32)] * 4, operands=(sums, recv, w, m, v),
        name=name, params=_params(("arbitrary",), 32), comm=comm)


def _adam_small(parts, w, m, v, *, name):
    R = w.shape[0]

    def body(p_ref, w_ref, m_ref, v_ref, g_ref, d_ref, nm_ref, nv_ref):
        g = p_ref[0]
        for d in range(1, N_DEV):
            g = g + p_ref[d]
        g_ref[...] = g
        d_ref[...], nm_ref[...], nv_ref[...] = _adam_math(w_ref[...], g, m_ref[...], v_ref[...])

    whole = _const((R, LANES))
    return _call(
        body, grid=(1,), in_specs=[_const((N_DEV, R, LANES)), whole, whole, whole], out_specs=[whole] * 4,
        out_shape=[_sds((R, LANES), F32)] * 4, operands=(parts, w, m, v), name=name,
        params=_params(("arbitrary",), 32))[0]


def _pack(arrs):
    return jnp.concatenate([a.reshape(-1) for a in arrs]).reshape(-1, LANES)


def _unpack(packed, shapes):
    flat = packed.reshape(-1)
    out, off = [], 0
    for s in shapes:
        n = 1
        for d in s:
            n *= d
        out.append(flat[off:off + n].reshape(s))
        off += n
    return out


BIG = ("e_in", "e_out", "o_in", "o_cw", "o_out")
BIG_AXIS = dict(e_in=1, e_out=0, o_in=1, o_cw=1, o_out=0)
BIG_SPLIT = dict(e_in=8, e_out=4, o_in=4, o_cw=4, o_out=4)
REPLICATED = ("e_norm_pre", "e_norm_post", "e_b_conv_bias", "e_b_ln_g", "e_b_ln_b")
SHARDED = ("e_a_conv", "e_b_conv", "o_norm_pre", "o_norm_post", "o_c_b", "o_c_scale")
SMALL = REPLICATED + SHARDED


class _Exchange:
    def __init__(self, shards, small, order, c_idx):
        self.q_idx = order[:1]
        self.shards = shards
        self.small = small
        self.order = order
        self.c_idx = c_idx
        self.reduced = {}

    def gather(self, keys):
        return _gather_comm([self.shards[k] for k in keys], [BIG_AXIS[k] for k in keys], "all")

    def gather1(self, keys):
        return _gather_comm([self.shards[k] for k in keys], [BIG_AXIS[k] for k in keys], "first")

    def gather2(self, keys, firsts):
        return _gather_comm(firsts, [BIG_AXIS[k] for k in keys], "second")

    def pair(self, grads):
        keys = list(grads)
        return _pair_comm([grads[k] for k in keys], [BIG_AXIS[k] for k in keys],
                          [grads[k].shape[BIG_AXIS[k]] // N_DEV for k in keys])

    def pair_sums(self, grads, received):
        return {k: _pair_sum(self.c_idx, grads[k], r, BIG_AXIS[k], grads[k].shape[BIG_AXIS[k]] // N_DEV,
                             BIG_SPLIT[k], name="pair_sum_" + k) for k, r in zip(grads, received)}

    def chips(self, sums):
        return _chip_comm([sums[k] for k in sums])

    def done(self, sums, received):
        self.reduced.update({k: (sums[k], r, self.q_idx) for k, r in zip(sums, received)})


def _local_step(x, tgt, w_small, ex):
    S, D = x.shape
    tnt, tx, tw = min(TM_NT, S), min(TM_MIX, S), min(TM_WIDE, S)

    wt = {}
    p, h0, wt["e_in"], got = _gather_matmul(ex.order, x, w_small["e_norm_pre"], ex.shards["e_in"], tm=tw,
                                            name="e_in_fwd", comm=_small_comm(ex.small))
    per_dev = [_unpack(got[0][d], [w_small[k].shape for k in SHARDED]) for d in range(N_DEV)]
    sm = {k: w_small[k] for k in REPLICATED}
    for j, k in enumerate(SHARDED):
        sm[k] = jnp.concatenate([per_dev[d][j] for d in range(N_DEV)], axis=-1)
    n_groups = sm["o_c_b"].shape[0]
    sm["o_c_b"] = sm["o_c_b"].reshape(1, -1)

    W = p.shape[1] // 7
    (u, cb), got = _e_mix_fwd(p, sm["e_a_conv"], sm["e_b_conv"], sm["e_b_conv_bias"], sm["e_b_ln_g"],
                              sm["e_b_ln_b"], tm=tx, name="e_mix_fwd", comm=ex.gather(["e_out"]))
    wt["e_out"] = got[0]
    late = ["o_out", "o_cw"]
    (x1, y0), part = _out_norm_res(u, wt["e_out"], x, sm["e_norm_post"], tm=tw, name="e_out_fwd",
                                   comm=ex.gather1(late))
    q, h1, wt["o_in"], got = _gather_matmul(ex.order, x1, sm["o_norm_pre"], ex.shards["o_in"], tm=tw,
                                            name="o_in_fwd", comm=ex.gather2(late, part))
    wt.update(zip(late, got))
    yy, pooled, gg = _o_mix_fwd(q, wt["o_cw"], sm["o_c_b"], sm["o_c_scale"], tm=tw, name="o_mix_fwd")
    dout, dx2, dyy, lcol, dg_o_post = _out_loss(yy, wt["o_out"], x1, sm["o_norm_post"], tgt, tm=tx, name="o_out_loss")
    loss = (0.5 / D) * jnp.sum(lcol)

    dq, d_cw, d_cb, d_cscale = _o_mix_bwd(dyy, q, gg, pooled, wt["o_cw"], sm["o_c_scale"], tm=tw, name="o_mix_bwd")
    g_o_out, _ = _mm_tn(yy, dout, ts=tnt, tn=W, name="o_out_dw")
    ga = dict(o_out=g_o_out, o_cw=d_cw.astype(BF16))
    dh1, ra = _mm_nt(dq, wt["o_in"], tm=tnt, tk=W, name="o_in_bwd", comm=ex.pair(ga))
    sa = ex.pair_sums(ga, ra)
    (dx1, dy0, dg_o_pre, dg_e_post), ra = _pre_bwd_o(dh1, x1, dx2, y0, sm["o_norm_pre"], sm["e_norm_post"],
                                                     tm=tx, name="o_pre_bwd", comm=ex.chips(sa))
    ex.done(sa, ra)
    g_o_in, _ = _mm_tn(h1, dq, ts=tnt, tn=W, name="o_in_dw")
    gb = dict(o_in=g_o_in)
    du, rb = _mm_nt(dy0, wt["e_out"], tm=tnt, tk=W, name="e_out_bwd", comm=ex.pair(gb))
    sb = ex.pair_sums(gb, rb)
    g_e_out, _ = _mm_tn(u, dy0, ts=tnt, tn=W, name="e_out_dw")
    gc = dict(e_out=g_e_out)
    (dp, d_wa, d_wb, d_bias, d_lg, d_lb), rbc = _e_mix_bwd(
        du, p, cb, sm["e_a_conv"], sm["e_b_conv"], sm["e_b_ln_g"], sm["e_b_ln_b"], tm=tx, name="e_mix_bwd",
        comm=_merge(ex.chips(sb), ex.pair(gc)))
    ex.done(sb, rbc[:1])
    sc = ex.pair_sums(gc, rbc[1:])
    order_out = jnp.concatenate([ex.order[1:], ex.order[:1]])
    sd, from_chip, rc = _dw_reduce(order_out, h0, dp, ex.shards["e_in"].shape[1], ts=tnt, name="e_in_dw",
                                   comm=ex.chips(sc))
    ex.done(sc, rc)
    dh0, rd = _mm_nt(dp, wt["e_in"], tm=tnt, tk=W, name="e_in_bwd", comm=_diag_comm(sd, from_chip))
    ex.reduced["e_in"] = (sd, rd[0], jnp.full((1,), 3, jnp.int32))
    grad_x, dg_e_pre = _pre_bwd_e(dh0, x, dx1, sm["e_norm_pre"], tm=tw, name="e_pre_bwd")

    small = dict(e_norm_pre=dg_e_pre, e_norm_post=dg_e_post, e_a_conv=d_wa, e_b_conv=d_wb, e_b_conv_bias=d_bias,
                 e_b_ln_g=d_lg, e_b_ln_b=d_lb, o_norm_pre=dg_o_pre, o_norm_post=dg_o_post,
                 o_c_b=d_cb.reshape(n_groups, -1), o_c_scale=d_cscale)
    return loss, grad_x, small


def kernel(x, e_norm_pre, e_norm_post, e_w_in, e_a_conv, e_b_conv, e_b_conv_bias, e_b_ln_g, e_b_ln_b, e_w_out, o_norm_pre, o_norm_post, o_w_in, o_c_w, o_c_b, o_c_scale, o_w_out, loss_target, m_e_norm_pre, m_e_norm_post, m_e_w_in, m_e_a_conv, m_e_b_conv, m_e_b_conv_bias, m_e_b_ln_g, m_e_b_ln_b, m_e_w_out, m_o_norm_pre, m_o_norm_post, m_o_w_in, m_o_c_w, m_o_c_b, m_o_c_scale, m_o_w_out, v_e_norm_pre, v_e_norm_post, v_e_w_in, v_e_a_conv, v_e_b_conv, v_e_b_conv_bias, v_e_b_ln_g, v_e_b_ln_b, v_e_w_out, v_o_norm_pre, v_o_norm_post, v_o_w_in, v_o_c_w, v_o_c_b, v_o_c_scale, v_o_w_out):
    xi, yi, ci = _place()
    w_big = dict(e_in=e_w_in[0], e_out=e_w_out[0], o_in=o_w_in[0], o_cw=o_c_w[0], o_out=o_w_out[0])
    m_big = dict(e_in=m_e_w_in[0], e_out=m_e_w_out[0], o_in=m_o_w_in[0], o_cw=m_o_c_w[0], o_out=m_o_w_out[0])
    v_big = dict(e_in=v_e_w_in[0], e_out=v_e_w_out[0], o_in=v_o_w_in[0], o_cw=v_o_c_w[0], o_out=v_o_w_out[0])
    w_small = dict(e_norm_pre=e_norm_pre, e_norm_post=e_norm_post, e_b_conv_bias=e_b_conv_bias, e_b_ln_g=e_b_ln_g,
                   e_b_ln_b=e_b_ln_b, e_a_conv=e_a_conv[0], e_b_conv=e_b_conv[0], o_norm_pre=o_norm_pre,
                   o_norm_post=o_norm_post, o_c_b=o_c_b[0], o_c_scale=o_c_scale)
    m_small = dict(e_norm_pre=m_e_norm_pre, e_norm_post=m_e_norm_post, e_b_conv_bias=m_e_b_conv_bias,
                   e_b_ln_g=m_e_b_ln_g, e_b_ln_b=m_e_b_ln_b, e_a_conv=m_e_a_conv[0], e_b_conv=m_e_b_conv[0],
                   o_norm_pre=m_o_norm_pre, o_norm_post=m_o_norm_post, o_c_b=m_o_c_b[0], o_c_scale=m_o_c_scale)
    v_small = dict(e_norm_pre=v_e_norm_pre, e_norm_post=v_e_norm_post, e_b_conv_bias=v_e_b_conv_bias,
                   e_b_ln_g=v_e_b_ln_g, e_b_ln_b=v_e_b_ln_b, e_a_conv=v_e_a_conv[0], e_b_conv=v_e_b_conv[0],
                   o_norm_pre=v_o_norm_pre, o_norm_post=v_o_norm_post, o_c_b=v_o_c_b[0], o_c_scale=v_o_c_scale)

    c_idx = jnp.reshape(ci, (1,)).astype(jnp.int32)
    order = jnp.stack([2 * xi + yi, 2 * (1 - xi) + yi, 2 * xi + (1 - yi), 2 * (1 - xi) + (1 - yi)]).astype(jnp.int32)
    ex = _Exchange({k: w_big[k].astype(BF16) for k in BIG}, _pack([w_small[k] for k in SHARDED]), order, c_idx)
    loss, grad_x, g_small = _local_step(x[0], loss_target[0], w_small, ex)

    big_out = {}
    for k in BIG:
        sums, received, q_idx = ex.reduced[k]
        big_out[k] = _adam_big(q_idx, sums, received, w_big[k], m_big[k], v_big[k], BIG_SPLIT[k], name="adam_" + k)[0]

    rep = _pack([g_small[k] for k in REPLICATED])
    loss_row = jnp.pad(jnp.reshape(loss, (1, 1)), ((0, 0), (0, LANES - 1)))
    blocks = []
    for k in SHARDED:
        r, n = w_small[k].shape
        blocks.append(g_small[k].reshape(r, N_DEV, n).transpose(1, 0, 2).reshape(N_DEV, r * n))
    blocks = jnp.concatenate(blocks, axis=1).reshape(N_DEV, -1, LANES)
    head = jnp.concatenate([rep, loss_row], axis=0)
    send = jnp.concatenate([jnp.broadcast_to(head[None], (N_DEV,) + head.shape), blocks], axis=1)
    parts = _run_comm(_small_scatter_comm(send), "small_grad_exchange")[0]

    def own_rows(d):
        return jnp.concatenate([_pack([d[k] for k in REPLICATED]), jnp.ones((1, LANES), F32),
                                _pack([d[k] for k in SHARDED])], axis=0)

    res_small = _adam_small(parts, own_rows(w_small), own_rows(m_small), own_rows(v_small), name="adam_small")
    n_rep = rep.shape[0]
    loss = res_small[0][n_rep, 0]
    small_out = {k: [] for k in SMALL}
    for packed in res_small:
        for k, t in zip(REPLICATED, _unpack(packed[:n_rep], [w_small[k].shape for k in REPLICATED])):
            small_out[k].append(t)
        for k, t in zip(SHARDED, _unpack(packed[n_rep + 1:], [w_small[k].shape for k in SHARDED])):
            small_out[k].append(t)

    big_of = dict(e_w_in="e_in", e_w_out="e_out", o_w_in="o_in", o_c_w="o_cw", o_w_out="o_out")
    stacked = ("e_a_conv", "e_b_conv", "o_c_b")

    def leaf(name, which):
        if name in big_of:
            return big_out[big_of[name]][which][None]
        t = small_out[name][which]
        return t[None] if name in stacked else t

    order = ("e_norm_pre", "e_norm_post", "e_w_in", "e_a_conv", "e_b_conv", "e_b_conv_bias", "e_b_ln_g", "e_b_ln_b",
             "e_w_out", "o_norm_pre", "o_norm_post", "o_w_in", "o_c_w", "o_c_b", "o_c_scale", "o_w_out")
    outs = [loss, grad_x[None]]
    for which in range(4):
        outs += [leaf(nm, which) for nm in order]
    return tuple(outs)
```

```python
import jax
import jax.numpy as jnp
from jax import lax
from jax.experimental import pallas as pl
from jax.experimental.pallas import tpu as pltpu

F32 = jnp.float32
BF16 = jnp.bfloat16
EPS = 1e-6
MESH = pl.DeviceIdType.MESH
ANY = pl.BlockSpec(memory_space=pl.ANY)

N_DEV = 8
HALO = 32
PHALO = 16
CONV_A = 3
CONV_B = 31
POOL_WINDOWS = (2, 4, 8, 16)
LANES = 128
MIB = 1024 * 1024
LOCAL = 1

ADAM_LR = 0.001
ADAM_B1 = 0.9
ADAM_B2 = 0.999
ADAM_EPS = 1e-08
ADAM_WD = 0.01
ADAM_STEP = 10

TM_NT = 1024
TM_MIX = 256
TM_WIDE = 512


def _sds(shape, dtype):
    return jax.ShapeDtypeStruct(tuple(shape), dtype)


def _params(sem, vmem_mib):
    return pltpu.CompilerParams(dimension_semantics=sem, vmem_limit_bytes=vmem_mib * MIB)


def _const(shape, single=False):
    n = len(shape)
    if single:
        return pl.BlockSpec(shape, lambda *_: (0,) * n, pipeline_mode=pl.Buffered(1))
    return pl.BlockSpec(shape, lambda *_: (0,) * n)


def _sig(v):
    return jax.nn.sigmoid(v)


def _dsilu(v, s):
    return s * (1.0 + v * (1.0 - s))


def _rms(v):
    return lax.rsqrt(jnp.mean(v * v, axis=-1, keepdims=True) + EPS)


def _norm_bwd(dn, n, r):
    return r * (dn - n * jnp.mean(dn * n, axis=-1, keepdims=True))


def _colsum(v):
    return jnp.sum(v, axis=0, keepdims=True)


class _Comm:
    def __init__(self, inputs, out_shapes, sems, start, finish, aliases=None, middle=None):
        self.inputs, self.out_shapes, self.sems = list(inputs), list(out_shapes), list(sems)
        self.start, self.finish, self.middle = start, finish, middle
        self.aliases = dict(aliases or {})


def _merge(*comms):
    comms = [c for c in comms if c is not None]
    if len(comms) <= 1:
        return comms[0] if comms else None
    spans, i0, o0, s0, aliases = [], 0, 0, 0, {}
    for c in comms:
        spans.append((i0, o0, s0))
        aliases.update({i0 + k: o0 + v for k, v in c.aliases.items()})
        i0, o0, s0 = i0 + len(c.inputs), o0 + len(c.out_shapes), s0 + len(c.sems)

    def run(which):
        def fn(ins, outs, sems):
            for c, (i, o, s) in zip(comms, spans):
                hook = getattr(c, which)
                if hook is not None:
                    hook(ins[i:i + len(c.inputs)], outs[o:o + len(c.out_shapes)], sems[s:s + len(c.sems)])
        return fn

    return _Comm([a for c in comms for a in c.inputs], [a for c in comms for a in c.out_shapes],
                 [a for c in comms for a in c.sems], run("start"), run("finish"), aliases,
                 run("middle") if any(c.middle is not None for c in comms) else None)


def _call(body, *, grid, in_specs, out_specs, out_shape, operands, name, params, scratch_shapes=(), comm=None,
          prefetch=None, own_copies_first=False):
    n_p = 0 if prefetch is None else 1
    n_i, n_o, n_s = len(in_specs), len(out_specs), len(scratch_shapes)
    if comm is None:
        comm = _Comm([], [], [], None, None)
    c_i, c_o = len(comm.inputs), len(comm.out_shapes)

    def carrier(*refs):
        pre, refs = refs[:n_p], refs[n_p:]
        ins, cins = refs[:n_i], refs[n_i:n_i + c_i]
        outs = refs[n_i + c_i:n_i + c_i + n_o]
        couts = refs[n_i + c_i + n_o:n_i + c_i + n_o + c_o]
        scr = refs[n_i + c_i + n_o + c_o:n_i + c_i + n_o + c_o + n_s]
        csems = refs[n_i + c_i + n_o + c_o + n_s:]
        ids = [pl.program_id(d) for d in range(len(grid))]
        first = ids[0] == 0
        half = ids[0] == grid[0] // 2
        last = ids[0] == grid[0] - 1
        for d in range(1, len(grid)):
            first = first & (ids[d] == 0)
            half = half & (ids[d] == 0)
            last = last & (ids[d] == grid[d] - 1)

        def start():
            if comm.start is not None:
                @pl.when(first)
                def _():
                    comm.start(cins, couts, csems)

        if not own_copies_first:
            start()
        if comm.middle is not None:
            assert grid[0] >= 2

            @pl.when(half)
            def _():
                comm.middle(cins, couts, csems)

        body(*pre, *ins, *outs, *scr)
        if own_copies_first:
            start()

        if comm.finish is not None:
            @pl.when(last)
            def _():
                comm.finish(cins, couts, csems)

    specs = dict(grid=grid, in_specs=list(in_specs) + [ANY] * c_i, out_specs=list(out_specs) + [ANY] * c_o,
                 scratch_shapes=list(scratch_shapes) + comm.sems)
    if n_p:
        specs = dict(grid_spec=pltpu.PrefetchScalarGridSpec(num_scalar_prefetch=1, **specs))
    res = pl.pallas_call(
        carrier, out_shape=list(out_shape) + comm.out_shapes,
        input_output_aliases={n_p + n_i + k: n_o + v for k, v in comm.aliases.items()},
        name=name, compiler_params=params, **specs)(*(() if prefetch is None else (prefetch,)), *operands, *comm.inputs)
    return list(res[:n_o]), list(res[n_o:])


def _run_comm(comm, name):
    c_i, c_o = len(comm.inputs), len(comm.out_shapes)

    def body(*refs):
        ins, outs, sems = refs[:c_i], refs[c_i:c_i + c_o], refs[c_i + c_o:]
        comm.start(ins, outs, sems)
        comm.finish(ins, outs, sems)

    res = pl.pallas_call(
        body, in_specs=[ANY] * c_i, out_specs=[ANY] * c_o, out_shape=comm.out_shapes, scratch_shapes=comm.sems,
        input_output_aliases=comm.aliases, name=name)(*comm.inputs)
    return list(res)


def _gather_matmul(order, x, g, shard, *, tm, name, comm=None):
    S, K = x.shape
    nb = shard.shape[1]
    n_i = S // tm

    def body(order_ref, x_ref, g_ref, shard_ref, p_ref, h_ref, full_ref, hbuf, wbuf, stage, send_sems, recv_sems,
             dma_sems):
        j, i = pl.program_id(0), pl.program_id(1)
        px, py, pc = _place()
        cps = _gather_copies(stage, full_ref, 1, nb, send_sems, recv_sems, 0)
        own = pltpu.make_async_copy(stage, _piece(full_ref, 1, nb, 4 * px + 2 * py + pc), dma_sems.at[0])
        keep_h = pltpu.make_async_copy(hbuf, h_ref, dma_sems.at[2])

        def load(src, dst):
            cp = pltpu.make_async_copy(src, dst, dma_sems.at[1])
            cp.start(priority=LOCAL)
            cp.wait()

        def load_pair(qx, qy):
            load(_piece(full_ref, 1, 2 * nb, 2 * qx + qy), wbuf)

        @pl.when((j == 0) & (i == 0))
        def _():
            load(shard_ref, stage)
            own.start(priority=LOCAL)
            for k in (0, 1, 2):
                cps[k].start()

        @pl.when(j == 0)
        def _():
            xx = x_ref[...]
            hbuf[i] = ((xx * _rms(xx)) * g_ref[...]).astype(BF16)

        @pl.when((j == 0) & (i == 0))
        def _():
            own.wait()
            cps[0].wait_recv()
            load_pair(px, py)

        @pl.when((j == 1) & (i == 0))
        def _():
            keep_h.start(priority=LOCAL)
            cps[1].wait_recv()
            cps[3].start()
            cps[5].start()
            cps[2].wait_recv()
            cps[4].start()
            cps[6].start()
            cps[5].wait_recv()
            load_pair(1 - px, py)

        @pl.when((j == 2) & (i == 0))
        def _():
            cps[6].wait_recv()
            load_pair(px, 1 - py)

        @pl.when((j == 2) & (i == n_i // 2))
        def _():
            cps[3].wait_recv()
            cps[4].wait_recv()
            cps[7].start()

        @pl.when((j == 3) & (i == 0))
        def _():
            cps[7].wait_recv()
            load_pair(1 - px, 1 - py)

        p_ref[...] = jnp.dot(hbuf[i], wbuf[...], preferred_element_type=F32).astype(BF16)

        @pl.when((j == 3) & (i == n_i - 1))
        def _():
            for cp in cps:
                cp.wait_send()
            keep_h.wait()

    first_pass = lambda j, i, o: (jnp.where(j == 0, i, n_i - 1), 0)
    outs, extra = _call(
        body, grid=(4, n_i), prefetch=order,
        in_specs=[pl.BlockSpec((tm, K), first_pass), pl.BlockSpec((1, K), lambda j, i, o: (0, 0)), ANY],
        out_specs=[pl.BlockSpec((tm, 2 * nb), lambda j, i, o: (i, o[j])), ANY, ANY],
        out_shape=[_sds((S, N_DEV * nb), BF16), _sds((n_i, tm, K), BF16), _sds((K, N_DEV * nb), BF16)],
        operands=(x, g, shard),
        scratch_shapes=[pltpu.VMEM((n_i, tm, K), BF16), pltpu.VMEM((K, 2 * nb), BF16), pltpu.VMEM((K, nb), BF16),
                        pltpu.SemaphoreType.DMA((N_GATHER,)), pltpu.SemaphoreType.DMA((N_GATHER,)),
                        pltpu.SemaphoreType.DMA((3,))],
        name=name, params=_params(("arbitrary", "arbitrary"), 58), comm=comm, own_copies_first=True)
    return outs[0], outs[1].reshape(S, K), outs[2], extra


def _out_norm_res(u, w, x, g, *, tm, name, comm=None):
    S, K = u.shape
    D = w.shape[1]

    def body(u_ref, w_ref, x_ref, g_ref, x1_ref, y_ref):
        y = jnp.dot(u_ref[...], w_ref[...], preferred_element_type=F32)
        y_ref[...] = y.astype(BF16)
        x1_ref[...] = x_ref[...] + (y * _rms(y)) * g_ref[...]

    return _call(
        body, grid=(S // tm,),
        in_specs=[pl.BlockSpec((tm, K), lambda i: (i, 0)), _const((K, D), single=True),
                  pl.BlockSpec((tm, D), lambda i: (i, 0)), _const((1, D))],
        out_specs=[pl.BlockSpec((tm, D), lambda i: (i, 0)), pl.BlockSpec((tm, D), lambda i: (i, 0))],
        out_shape=[_sds((S, D), F32), _sds((S, D), BF16)], operands=(u, w, x, g),
        name=name, params=_params(("arbitrary",), 56), comm=comm)


def _out_loss(yy, w, x1, g, tgt, *, tm, name):
    S, K = yy.shape
    D = w.shape[1]

    def body(yy_ref, w_ref, x1_ref, g_ref, t_ref, dout_ref, dx2_ref, dyy_ref, lcol_ref, dg_ref):
        out = jnp.dot(yy_ref[...], w_ref[...], preferred_element_type=F32)
        r = _rms(out)
        n = out * r
        gg = g_ref[...]
        e = x1_ref[...] + n * gg - t_ref[...]
        dx2 = e * (1.0 / D)
        dx2_ref[...] = dx2
        dout = _norm_bwd(dx2 * gg, n, r).astype(BF16)
        dout_ref[...] = dout
        dyy_ref[...] = lax.dot_general(dout, w_ref[...], (((1,), (1,)), ((), ())),
                                       preferred_element_type=F32).astype(BF16)

        @pl.when(pl.program_id(0) == 0)
        def _():
            lcol_ref[...] = jnp.zeros_like(lcol_ref)
            dg_ref[...] = jnp.zeros_like(dg_ref)

        lcol_ref[...] += _colsum(e * e)
        dg_ref[...] += _colsum(dx2 * n)

    return _call(
        body, grid=(S // tm,),
        in_specs=[pl.BlockSpec((tm, K), lambda i: (i, 0)), _const((K, D), single=True),
                  pl.BlockSpec((tm, D), lambda i: (i, 0)), _const((1, D)),
                  pl.BlockSpec((tm, D), lambda i: (i, 0))],
        out_specs=[pl.BlockSpec((tm, D), lambda i: (i, 0)), pl.BlockSpec((tm, D), lambda i: (i, 0)),
                   pl.BlockSpec((tm, K), lambda i: (i, 0)), _const((1, D)), _const((1, D))],
        out_shape=[_sds((S, D), BF16), _sds((S, D), F32), _sds((S, K), BF16), _sds((1, D), F32), _sds((1, D), F32)],
        operands=(yy, w, x1, g, tgt), name=name, params=_params(("arbitrary",), 52))[0]


def _mm_nt(a, w, *, tm, tk, name, comm=None):
    S, N = a.shape
    D = w.shape[0]
    n_k = N // tk

    def body(a_ref, w_ref, o_ref, acc_ref):
        k = pl.program_id(1)

        @pl.when(k == 0)
        def _():
            acc_ref[...] = jnp.zeros_like(acc_ref)

        acc_ref[...] = lax.dot_general(a_ref[...], w_ref[...], (((1,), (1,)), ((), ())),
                                       preferred_element_type=F32) + acc_ref[...]

        @pl.when(k == n_k - 1)
        def _():
            o_ref[...] = acc_ref[...].astype(BF16)

    outs, extra = _call(
        body, grid=(S // tm, n_k),
        in_specs=[pl.BlockSpec((tm, tk), lambda i, k: (i, k)), pl.BlockSpec((D, tk), lambda i, k: (0, k))],
        out_specs=[pl.BlockSpec((tm, D), lambda i, k: (i, 0))],
        out_shape=[_sds((S, D), BF16)], operands=(a, w),
        scratch_shapes=[pltpu.VMEM((tm, D), F32)],
        name=name, params=_params(("arbitrary", "arbitrary"), 48), comm=comm)
    return outs[0], extra


def _mm_tn(a, b, *, ts, tn, name, comm=None):
    S, M = a.shape
    N = b.shape[1]
    n_s = S // ts

    def body(a_ref, b_ref, o_ref, acc_ref):
        s = pl.program_id(1)

        @pl.when(s == 0)
        def _():
            acc_ref[...] = jnp.zeros_like(acc_ref)

        acc_ref[...] = lax.dot_general(a_ref[...], b_ref[...], (((0,), (0,)), ((), ())),
                                       preferred_element_type=F32) + acc_ref[...]

        @pl.when(s == n_s - 1)
        def _():
            o_ref[...] = acc_ref[...].astype(BF16)

    outs, extra = _call(
        body, grid=(N // tn, n_s),
        in_specs=[pl.BlockSpec((ts, M), lambda j, s: (s, 0)), pl.BlockSpec((ts, tn), lambda j, s: (s, j))],
        out_specs=[pl.BlockSpec((M, tn), lambda j, s: (0, j))],
        out_shape=[_sds((M, N), BF16)], operands=(a, b),
        scratch_shapes=[pltpu.VMEM((M, tn), F32)],
        name=name, params=_params(("arbitrary", "arbitrary"), 48), comm=comm)
    return outs[0], extra


def _dw_reduce(order, a, b, nb, *, ts, name, comm=None):
    S, M = a.shape
    n_s = S // ts
    rows = 512

    def body(order_ref, a_ref, b_ref, sums_ref, from_sib_ref, from_chip_ref, acc, send_buf, mine_buf, recv_buf,
             sib_send, sib_recv, chip_send, chip_recv, dma_sems):
        t, s = pl.program_id(0), pl.program_id(1)
        x, y, c = _place()
        targets = [(1 - x, y, c), (x, 1 - y, c)]

        def to_sibling(k):
            return pltpu.make_async_remote_copy(
                src_ref=send_buf, dst_ref=from_sib_ref.at[k], send_sem=sib_send.at[k], recv_sem=sib_recv.at[k],
                device_id=(x, y, 1 - c), device_id_type=MESH)

        def to_chip(k):
            return pltpu.make_async_remote_copy(
                src_ref=sums_ref.at[k], dst_ref=from_chip_ref.at[k], send_sem=chip_send.at[k],
                recv_sem=chip_recv.at[k], device_id=targets[k], device_id_type=MESH)

        def finish(k):
            to_sibling(k).wait()
            get = pltpu.make_async_copy(from_sib_ref.at[k], recv_buf, dma_sems.at[0])
            get.start(priority=LOCAL)
            get.wait()
            for r in range(0, M, rows):
                recv_buf[r:r + rows, :] = (mine_buf[r:r + rows, :].astype(F32)
                                           + recv_buf[r:r + rows, :].astype(F32)).astype(BF16)
            put = pltpu.make_async_copy(recv_buf, sums_ref.at[k], dma_sems.at[1])
            put.start(priority=LOCAL)
            put.wait()
            if k < 2:
                to_chip(k).start()

        for k in range(3):
            @pl.when((t == k + 1) & (s == min(1, n_s - 1)))
            def _(k=k):
                finish(k)

        @pl.when(s == 0)
        def _():
            acc[...] = jnp.zeros_like(acc)

        acc[...] = lax.dot_general(a_ref[...], b_ref[...], (((0,), (0,)), ((), ())),
                                   preferred_element_type=F32) + acc[...]

        @pl.when(s == n_s - 1)
        def _():
            for r in range(0, M, rows):
                lo, hi = acc[r:r + rows, :nb], acc[r:r + rows, nb:]
                send_buf[r:r + rows, :] = jnp.where(c == 0, hi, lo).astype(BF16)
                mine_buf[r:r + rows, :] = jnp.where(c == 0, lo, hi).astype(BF16)
            to_sibling(t).start()

        @pl.when((t == 3) & (s == n_s - 1))
        def _():
            finish(3)
            to_chip(0).wait()
            to_chip(1).wait()

    piece = _sds((4, M, nb), BF16)
    outs, extra = _call(
        body, grid=(4, n_s), prefetch=order,
        in_specs=[pl.BlockSpec((ts, M), lambda t, s, o: (s, 0)), pl.BlockSpec((ts, 2 * nb), lambda t, s, o: (s, o[t]))],
        out_specs=[ANY, ANY, ANY], out_shape=[piece, piece, _sds((3, M, nb), BF16)], operands=(a, b),
        scratch_shapes=[pltpu.VMEM((M, 2 * nb), F32), pltpu.VMEM((M, nb), BF16), pltpu.VMEM((M, nb), BF16),
                        pltpu.VMEM((M, nb), BF16), pltpu.SemaphoreType.DMA((4,)), pltpu.SemaphoreType.DMA((4,)),
                        pltpu.SemaphoreType.DMA((2,)), pltpu.SemaphoreType.DMA((2,)), pltpu.SemaphoreType.DMA((2,))],
        name=name, params=_params(("arbitrary", "arbitrary"), 56), comm=comm)
    return outs[0], outs[2], extra


def _diag_comm(sums, from_chip):
    def copy(ins, outs, sems):
        x, y, c = _place()
        return pltpu.make_async_remote_copy(
            src_ref=ins[0].at[2], dst_ref=outs[0].at[2], send_sem=sems[0].at[0], recv_sem=sems[1].at[0],
            device_id=(1 - x, 1 - y, c), device_id_type=MESH)

    def start(ins, outs, sems):
        copy(ins, outs, sems).start()

    def finish(ins, outs, sems):
        copy(ins, outs, sems).wait()

    sems = [pltpu.SemaphoreType.DMA((1,)), pltpu.SemaphoreType.DMA((1,))]
    return _Comm([sums, from_chip], [_sds(from_chip.shape, from_chip.dtype)], sems, start, finish, aliases={1: 0})


def _pre_bwd_o(dh, x1, dx2, y0, g_pre, g_post, *, tm, name, comm=None):
    S, D = x1.shape

    def body(dh_ref, x1_ref, dx2_ref, y0_ref, gpre_ref, gpost_ref, dx1_ref, dy0_ref, dgpre_ref, dgpost_ref):
        @pl.when(pl.program_id(0) == 0)
        def _():
            dgpre_ref[...] = jnp.zeros_like(dgpre_ref)
            dgpost_ref[...] = jnp.zeros_like(dgpost_ref)

        dh = dh_ref[...].astype(F32)
        x1 = x1_ref[...]
        r2 = _rms(x1)
        xn = x1 * r2
        dgpre_ref[...] += _colsum(dh * xn)
        dx1 = dx2_ref[...] + _norm_bwd(dh * gpre_ref[...], xn, r2)
        dx1_ref[...] = dx1
        y = y0_ref[...].astype(F32)
        r1 = _rms(y)
        n1 = y * r1
        dgpost_ref[...] += _colsum(dx1 * n1)
        dy0_ref[...] = _norm_bwd(dx1 * gpost_ref[...], n1, r1).astype(BF16)

    row = pl.BlockSpec((tm, D), lambda i: (i, 0))
    return _call(
        body, grid=(S // tm,),
        in_specs=[row, row, row, row, _const((1, D)), _const((1, D))],
        out_specs=[row, row, _const((1, D)), _const((1, D))],
        out_shape=[_sds((S, D), F32), _sds((S, D), BF16), _sds((1, D), F32), _sds((1, D), F32)],
        operands=(dh, x1, dx2, y0, g_pre, g_post),
        name=name, params=_params(("arbitrary",), 48), comm=comm)


def _pre_bwd_e(dh, x, dx1, g_pre, *, tm, name):
    S, D = x.shape

    def body(dh_ref, x_ref, dx1_ref, gpre_ref, gx_ref, dgpre_ref):
        @pl.when(pl.program_id(0) == 0)
        def _():
            dgpre_ref[...] = jnp.zeros_like(dgpre_ref)

        dh = dh_ref[...].astype(F32)
        xx = x_ref[...]
        r0 = _rms(xx)
        xn = xx * r0
        dgpre_ref[...] += _colsum(dh * xn)
        gx_ref[...] = dx1_ref[...] + _norm_bwd(dh * gpre_ref[...], xn, r0)

    row = pl.BlockSpec((tm, D), lambda i: (i, 0))
    return _call(
        body, grid=(S // tm,),
        in_specs=[row, row, row, _const((1, D))],
        out_specs=[row, _const((1, D))],
        out_shape=[_sds((S, D), F32), _sds((1, D), F32)],
        operands=(dh, x, dx1, g_pre), name=name, params=_params(("arbitrary",), 56))[0]


SUBLANES = 8


def _shift_copies(sh_ref, ext_ref, cs):
    for b in range(1, SUBLANES):
        sh_ref[b - 1] = ext_ref[pl.ds(b, sh_ref.shape[1]), cs]


def _rows_at(ext_ref, sh_ref, off, cs, tm):
    b = off % SUBLANES
    if b == 0 or sh_ref is None:
        return ext_ref[pl.ds(off, tm), cs]
    return sh_ref[b - 1, pl.ds(off - b, tm), :]


def _taps(ext_ref, w_ref, n_taps, base, cs, tm, sh_ref=None):
    acc = _rows_at(ext_ref, sh_ref, base, cs, tm) * w_ref[0:1, cs]
    for k in range(1, n_taps):
        acc = acc + _rows_at(ext_ref, sh_ref, base + k, cs, tm) * w_ref[k:k + 1, cs]
    return acc


def _taps_rev(ext_ref, w_ref, n_taps, cs, tm, sh_ref=None):
    acc = _rows_at(ext_ref, sh_ref, n_taps - 1, cs, tm) * w_ref[0:1, cs]
    for k in range(1, n_taps):
        acc = acc + _rows_at(ext_ref, sh_ref, n_taps - 1 - k, cs, tm) * w_ref[k:k + 1, cs]
    return acc


def _e_mix_fwd(p, wa, wb, bias, ln_g, ln_b, *, tm, name, comm=None):
    S = p.shape[0]
    W = p.shape[1] // 7
    nb = tm // HALO
    chunks = [slice(c * LANES, (c + 1) * LANES) for c in range(W // LANES)]

    def body(p_ref, hax_ref, hac_ref, hbv_ref, hbg_ref, wa_ref, wb_ref, bias_ref, lg_ref, lb_ref,
             u_ref, cb_ref, ext_ref, sh_ref):
        keep = (pl.program_id(0) > 0).astype(F32)
        col = lambda j, cs: p_ref[:, j * W + cs.start:j * W + cs.stop].astype(F32)

        ext_ref[0:HALO, :] = hax_ref[...].astype(F32) * hac_ref[...].astype(F32) * keep
        ext_ref[HALO:, :] = p_ref[:, 2 * W:3 * W].astype(F32) * p_ref[:, 0:W].astype(F32)
        for cs in chunks:
            conv = _taps(ext_ref, wa_ref, CONV_A, HALO - (CONV_A - 1), cs, tm)
            az = col(3, cs)
            u_ref[:, cs] = (col(1, cs) * conv * (az * _sig(az))).astype(BF16)

        ext_ref[0:HALO, :] = hbv_ref[...].astype(F32) * _sig(hbg_ref[...].astype(F32)) * keep
        ext_ref[HALO:, :] = p_ref[:, 4 * W:5 * W].astype(F32) * _sig(p_ref[:, 5 * W:6 * W].astype(F32))
        s1 = jnp.zeros((tm, LANES), F32)
        for cs in chunks:
            _shift_copies(sh_ref, ext_ref, cs)
            cb = _taps(ext_ref, wb_ref, CONV_B, HALO - (CONV_B - 1), cs, tm, sh_ref) + bias_ref[:, cs]
            cb_ref[:, cs] = cb
            s1 = s1 + cb
        mu = jnp.sum(s1, axis=-1, keepdims=True) * (1.0 / W)
        s2 = jnp.zeros((tm, LANES), F32)
        for cs in chunks:
            xc = cb_ref[:, cs] - mu
            s2 = s2 + xc * xc
        rs = lax.rsqrt(jnp.sum(s2, axis=-1, keepdims=True) * (1.0 / W) + EPS)
        for cs in chunks:
            lb = (cb_ref[:, cs] - mu) * rs * lg_ref[:, cs] + lb_ref[:, cs]
            bz = col(6, cs)
            u_ref[:, W + cs.start:W + cs.stop] = (lb * _sig(lb) * (bz * _sig(bz))).astype(BF16)

    prev = lambda j: pl.BlockSpec((HALO, W), lambda i: (jnp.maximum(i * nb - 1, 0), j))
    return _call(
        body, grid=(S // tm,),
        in_specs=[pl.BlockSpec((tm, 7 * W), lambda i: (i, 0)), prev(0), prev(2), prev(4), prev(5),
                  _const((CONV_A, W)), _const((CONV_B, W)), _const((1, W)), _const((1, W)), _const((1, W))],
        out_specs=[pl.BlockSpec((tm, 2 * W), lambda i: (i, 0)), pl.BlockSpec((tm, W), lambda i: (i, 0))],
        out_shape=[_sds((S, 2 * W), BF16), _sds((S, W), F32)],
        operands=(p, p, p, p, p, wa, wb, bias, ln_g, ln_b),
        scratch_shapes=[pltpu.VMEM((HALO + tm, W), F32),
                        pltpu.VMEM((SUBLANES - 1, HALO + tm - SUBLANES, LANES), F32)],
        name=name, params=_params(("arbitrary",), 48), comm=comm)


def _e_mix_bwd(du, p, cb, wa, wb, ln_g, ln_b, *, tm, name, comm=None):
    S = p.shape[0]
    W = p.shape[1] // 7
    nb = tm // HALO
    n_t = S // tm
    last_blk = S // HALO - 1
    chunks = [slice(c * LANES, (c + 1) * LANES) for c in range(W // LANES)]

    def body(du_ref, duf_ref, p_ref, fab_ref, faz_ref, fbz_ref, hax_ref, hac_ref, hbv_ref, hbg_ref,
             cb_ref, cbf_ref, wa_ref, wb_ref, lg_ref, lb_ref,
             dp_ref, dwa_ref, dwb_ref, dbias_ref, dlg_ref, dlb_ref, extd_ref, extg_ref, shd_ref, shg_ref):
        i = pl.program_id(0)
        keep_prev = (i > 0).astype(F32)
        keep_next = (i < n_t - 1).astype(F32)
        col = lambda j, cs: p_ref[:, j * W + cs.start:j * W + cs.stop].astype(F32)

        @pl.when(i == 0)
        def _():
            dwa_ref[...] = jnp.zeros_like(dwa_ref)
            dwb_ref[...] = jnp.zeros_like(dwb_ref)
            dbias_ref[...] = jnp.zeros_like(dbias_ref)
            dlg_ref[...] = jnp.zeros_like(dlg_ref)
            dlb_ref[...] = jnp.zeros_like(dlb_ref)

        def dcb_rows(rows, cb_rows_ref, dub, bz_of, dst0, scale, main):
            cbv = cb_rows_ref[...]
            mu = jnp.mean(cbv, axis=-1, keepdims=True)
            xc = cbv - mu
            rs = lax.rsqrt(jnp.mean(xc * xc, axis=-1, keepdims=True) + EPS)
            m1 = jnp.zeros((rows, LANES), F32)
            m2 = jnp.zeros((rows, LANES), F32)
            for cs in chunks:
                nbv = (cb_rows_ref[:, cs] - mu) * rs
                lb = nbv * lg_ref[:, cs] + lb_ref[:, cs]
                sl = _sig(lb)
                bz = bz_of(cs)
                sz = _sig(bz)
                dub_c = dub(cs)
                dlb = dub_c * (bz * sz) * _dsilu(lb, sl)
                if main:
                    dlg_ref[:, cs] += _colsum(dlb * nbv)
                    dlb_ref[:, cs] += _colsum(dlb)
                    dp_ref[:, 6 * W + cs.start:6 * W + cs.stop] = (dub_c * (lb * sl) * _dsilu(bz, sz)).astype(BF16)
                dnb = dlb * lg_ref[:, cs]
                extd_ref[dst0:dst0 + rows, cs] = dnb
                m1 = m1 + dnb
                m2 = m2 + dnb * nbv
            m1 = jnp.sum(m1, axis=-1, keepdims=True) * (1.0 / W)
            m2 = jnp.sum(m2, axis=-1, keepdims=True) * (1.0 / W)
            for cs in chunks:
                nbv = (cb_rows_ref[:, cs] - mu) * rs
                dcb = rs * (extd_ref[dst0:dst0 + rows, cs] - m1 - nbv * m2) * scale
                extd_ref[dst0:dst0 + rows, cs] = dcb
                if main:
                    dbias_ref[:, cs] += _colsum(dcb)

        dcb_rows(tm, cb_ref, lambda cs: du_ref[:, W + cs.start:W + cs.stop].astype(F32),
                 lambda cs: col(6, cs), 0, 1.0, True)
        dcb_rows(HALO, cbf_ref, lambda cs: duf_ref[:, W + cs.start:W + cs.stop].astype(F32),
                 lambda cs: fbz_ref[:, cs].astype(F32), tm, keep_next, False)

        extg_ref[0:HALO, :] = hbv_ref[...].astype(F32) * _sig(hbg_ref[...].astype(F32)) * keep_prev
        extg_ref[HALO:, :] = p_ref[:, 4 * W:5 * W].astype(F32) * _sig(p_ref[:, 5 * W:6 * W].astype(F32))
        base_b = HALO - (CONV_B - 1)
        for cs in chunks:
            _shift_copies(shd_ref, extd_ref, cs)
            _shift_copies(shg_ref, extg_ref, cs)
            dgb = _taps_rev(extd_ref, wb_ref, CONV_B, cs, tm, shd_ref)
            bv = col(4, cs)
            sg = _sig(col(5, cs))
            dp_ref[:, 4 * W + cs.start:4 * W + cs.stop] = (dgb * sg).astype(BF16)
            dp_ref[:, 5 * W + cs.start:5 * W + cs.stop] = (dgb * bv * sg * (1.0 - sg)).astype(BF16)
            dcb = extd_ref[0:tm, cs]
            for k in range(CONV_B):
                dwb_ref[k:k + 1, cs] += _colsum(dcb * _rows_at(extg_ref, shg_ref, base_b + k, cs, tm))

        extg_ref[0:HALO, :] = hax_ref[...].astype(F32) * hac_ref[...].astype(F32) * keep_prev
        extg_ref[HALO:, :] = p_ref[:, 2 * W:3 * W].astype(F32) * p_ref[:, 0:W].astype(F32)
        base_a = HALO - (CONV_A - 1)
        for cs in chunks:
            conv = _taps(extg_ref, wa_ref, CONV_A, base_a, cs, tm)
            az = col(3, cs)
            sz = _sig(az)
            ab = col(1, cs)
            dua = du_ref[:, cs].astype(F32)
            dya = dua * (az * sz)
            dp_ref[:, W + cs.start:W + cs.stop] = (dya * conv).astype(BF16)
            dp_ref[:, 3 * W + cs.start:3 * W + cs.stop] = (dua * (ab * conv) * _dsilu(az, sz)).astype(BF16)
            extd_ref[0:tm, cs] = dya * ab
            azf = faz_ref[:, cs].astype(F32)
            extd_ref[tm:tm + HALO, cs] = (duf_ref[:, cs].astype(F32) * (azf * _sig(azf))
                                          * fab_ref[:, cs].astype(F32) * keep_next)
        for cs in chunks:
            dca = _taps_rev(extd_ref, wa_ref, CONV_A, cs, tm)
            dp_ref[:, cs] = (dca * col(2, cs)).astype(BF16)
            dp_ref[:, 2 * W + cs.start:2 * W + cs.stop] = (dca * col(0, cs)).astype(BF16)
            dconv = extd_ref[0:tm, cs]
            for k in range(CONV_A):
                dwa_ref[k:k + 1, cs] += _colsum(dconv * extg_ref[pl.ds(base_a + k, tm), cs])

    prev = lambda j: pl.BlockSpec((HALO, W), lambda i: (jnp.maximum(i * nb - 1, 0), j))
    nxt = lambda j, w: pl.BlockSpec((HALO, w), lambda i: (jnp.minimum((i + 1) * nb, last_blk), j))
    row = lambda w: pl.BlockSpec((tm, w), lambda i: (i, 0))
    return _call(
        body, grid=(n_t,),
        in_specs=[row(2 * W), nxt(0, 2 * W), row(7 * W), nxt(1, W), nxt(3, W), nxt(6, W),
                  prev(0), prev(2), prev(4), prev(5), row(W), nxt(0, W),
                  _const((CONV_A, W)), _const((CONV_B, W)), _const((1, W)), _const((1, W))],
        out_specs=[row(7 * W), _const((CONV_A, W)), _const((CONV_B, W)), _const((1, W)), _const((1, W)), _const((1, W))],
        out_shape=[_sds((S, 7 * W), BF16), _sds((CONV_A, W), F32), _sds((CONV_B, W), F32),
                   _sds((1, W), F32), _sds((1, W), F32), _sds((1, W), F32)],
        operands=(du, du, p, p, p, p, p, p, p, p, cb, cb, wa, wb, ln_g, ln_b),
        scratch_shapes=[pltpu.VMEM((tm + HALO, W), F32), pltpu.VMEM((HALO + tm, W), F32),
                        pltpu.VMEM((SUBLANES - 1, HALO + tm - SUBLANES, LANES), F32),
                        pltpu.VMEM((SUBLANES - 1, HALO + tm - SUBLANES, LANES), F32)],
        name=name, params=_params(("arbitrary",), 52), comm=comm)


def _counts(i, tm, rows, off, win):
    t = i * tm + off + lax.broadcasted_iota(jnp.int32, (rows, 1), 0)
    return jnp.minimum(t + 1, win).astype(F32)


def _o_mix_fwd(q, cw, cb, cscale, *, tm, name):
    S = q.shape[0]
    WC = q.shape[1] // 2
    NG = len(POOL_WINDOWS)
    G = WC // NG
    nb = tm // PHALO

    def body(v_ref, z_ref, hv_ref, cw_ref, cb_ref, sc_ref, yy_ref, pooled_ref, gg_ref, ext_ref):
        i = pl.program_id(0)
        keep = (i > 0).astype(F32)
        for g, win in enumerate(POOL_WINDOWS):
            cs = slice(g * G, (g + 1) * G)
            v = v_ref[:, cs].astype(F32)
            ext_ref[0:PHALO, :] = hv_ref[:, cs].astype(F32) * keep
            ext_ref[PHALO:, :] = v
            s = v
            for j in range(1, win):
                s = s + ext_ref[pl.ds(PHALO - j, tm), :]
            pooled = (s / _counts(i, tm, tm, 0, win) - v).astype(BF16)
            pooled_ref[:, cs] = pooled
            gg = jnp.dot(pooled, cw_ref[g], preferred_element_type=F32) + cb_ref[:, cs]
            gg_ref[:, cs] = gg.astype(BF16)
            z = z_ref[:, cs].astype(F32)
            yy_ref[:, cs] = (gg * sc_ref[:, cs] * (z * _sig(z))).astype(BF16)

    row = lambda j: pl.BlockSpec((tm, WC), lambda i: (i, j))
    out = pl.BlockSpec((tm, WC), lambda i: (i, 0))
    return _call(
        body, grid=(S // tm,),
        in_specs=[row(0), row(1), pl.BlockSpec((PHALO, WC), lambda i: (jnp.maximum(i * nb - 1, 0), 0)),
                  _const((NG, G, G)), _const((1, WC)), _const((1, WC))],
        out_specs=[out, out, out],
        out_shape=[_sds((S, WC), BF16)] * 3, operands=(q, q, q, cw, cb, cscale),
        scratch_shapes=[pltpu.VMEM((PHALO + tm, G), F32)],
        name=name, params=_params(("arbitrary",), 40))[0]


def _o_mix_bwd(dyy, q, gg, pooled, cw, cscale, *, tm, name):
    S = q.shape[0]
    WC = q.shape[1] // 2
    NG = len(POOL_WINDOWS)
    G = WC // NG
    nb = tm // PHALO
    n_t = S // tm
    last_blk = S // PHALO - 1
    nt = (((1,), (1,)), ((), ()))
    tn = (((0,), (0,)), ((), ()))

    def body(dyy_ref, dyyf_ref, z_ref, zf_ref, gg_ref, pooled_ref, cw_ref, sc_ref,
             dq_ref, dcw_ref, dcb_ref, dsc_ref, ext_ref):
        i = pl.program_id(0)
        keep_next = (i < n_t - 1).astype(F32)

        @pl.when(i == 0)
        def _():
            dcw_ref[...] = jnp.zeros_like(dcw_ref)
            dcb_ref[...] = jnp.zeros_like(dcb_ref)
            dsc_ref[...] = jnp.zeros_like(dsc_ref)

        for g, win in enumerate(POOL_WINDOWS):
            cs = slice(g * G, (g + 1) * G)
            sc = sc_ref[:, cs]
            z = z_ref[:, cs].astype(F32)
            sz = _sig(z)
            dyy_c = dyy_ref[:, cs].astype(F32)
            ggv = gg_ref[:, cs].astype(F32)
            dyy0 = dyy_c * (z * sz)
            dq_ref[:, WC + cs.start:WC + cs.stop] = (dyy_c * (ggv * sc) * _dsilu(z, sz)).astype(BF16)
            dgg = dyy0 * sc
            dsc_ref[:, cs] += _colsum(dyy0 * ggv)
            dcb_ref[:, cs] += _colsum(dgg)
            dgg_b = dgg.astype(BF16)
            dcw_ref[g] += lax.dot_general(pooled_ref[:, cs], dgg_b, tn, preferred_element_type=F32)
            dpool = lax.dot_general(dgg_b, cw_ref[g], nt, preferred_element_type=F32)
            zf = zf_ref[:, cs].astype(F32)
            dgg_f = (dyyf_ref[:, cs].astype(F32) * (zf * _sig(zf)) * sc * keep_next).astype(BF16)
            dpool_f = lax.dot_general(dgg_f, cw_ref[g], nt, preferred_element_type=F32)
            ext_ref[0:tm, :] = dpool / _counts(i, tm, tm, 0, win)
            ext_ref[tm:tm + PHALO, :] = dpool_f / _counts(i, tm, PHALO, tm, win)
            dv = ext_ref[0:tm, :] - dpool
            for j in range(1, win):
                dv = dv + ext_ref[pl.ds(j, tm), :]
            dq_ref[:, cs] = dv.astype(BF16)

    row = lambda: pl.BlockSpec((tm, WC), lambda i: (i, 0))
    nxt = lambda j: pl.BlockSpec((PHALO, WC), lambda i: (jnp.minimum((i + 1) * nb, last_blk), j))
    return _call(
        body, grid=(n_t,),
        in_specs=[row(), nxt(0), pl.BlockSpec((tm, WC), lambda i: (i, 1)), nxt(1), row(), row(),
                  _const((NG, G, G)), _const((1, WC))],
        out_specs=[pl.BlockSpec((tm, 2 * WC), lambda i: (i, 0)), _const((NG, G, G)), _const((1, WC)), _const((1, WC))],
        out_shape=[_sds((S, 2 * WC), BF16), _sds((NG, G, G), F32), _sds((1, WC), F32), _sds((1, WC), F32)],
        operands=(dyy, dyy, q, q, gg, pooled, cw, cscale),
        scratch_shapes=[pltpu.VMEM((tm + PHALO, G), F32)],
        name=name, params=_params(("arbitrary",), 48))[0]


def _place():
    return lax.axis_index("x"), lax.axis_index("y"), lax.axis_index("c")


def _piece(ref, axis, size, index):
    start = index * size
    if axis == len(ref.shape) - 1:
        start = pl.multiple_of(start, LANES)
    idx = [slice(None)] * len(ref.shape)
    idx[axis] = pl.ds(start, size)
    return ref.at[tuple(idx)]


def _gather_copies(src, out, axis, size, send_sems, recv_sems, base, held=None):
    x, y, c = _place()
    sib, xn, yn = (x, y, 1 - c), (1 - x, y, c), (x, 1 - y, c)

    def blk(px, py, of=out):
        return _piece(of, axis, size, 4 * px + 2 * py + c)

    def half(ref, h):
        n = ref.shape[0] // 2
        return ref.at[pl.ds(h * n, n)]

    def rc(k, s, d, to):
        return pltpu.make_async_remote_copy(src_ref=s, dst_ref=d, send_sem=send_sems.at[base + k],
                                            recv_sem=recv_sems.at[base + k], device_id=to, device_id_type=MESH)

    own, xb, yb, db = blk(x, y), blk(1 - x, y), blk(x, 1 - y), blk(1 - x, 1 - y)
    got = out if held is None else held
    xs, ys, ds = blk(1 - x, y, got), blk(x, 1 - y, got), blk(1 - x, 1 - y, got)
    return [rc(0, src, own, sib), rc(1, src, own, xn), rc(2, src, own, yn),
            rc(3, half(xs, 0), half(xb, 0), yn), rc(4, half(ys, 1), half(yb, 1), xn),
            rc(5, xs, xb, sib), rc(6, ys, yb, sib), rc(7, ds, db, sib)]


N_GATHER = 8


def _gather_comm(shards, axes, phases):
    n = len(shards)
    if phases == "second":
        sizes = [s.shape[a] // N_DEV for s, a in zip(shards, axes)]
        full = [_sds(s.shape, s.dtype) for s in shards]
    else:
        sizes = [s.shape[a] for s, a in zip(shards, axes)]
        full = [_sds(s.shape[:a] + (N_DEV * s.shape[a],) + s.shape[a + 1:], s.dtype) for s, a in zip(shards, axes)]

    def plan(ins, outs, sems):
        x, y, c = _place()
        me = 4 * x + 2 * y + c
        if phases == "second":
            cps = [_gather_copies(_piece(ins[t], axes[t], sizes[t], me), outs[t], axes[t], sizes[t], sems[0], sems[1],
                                  N_GATHER * t, ins[t]) for t in range(n)]
        else:
            cps = [_gather_copies(sems[3 + t], outs[t], axes[t], sizes[t], sems[0], sems[1], N_GATHER * t)
                   for t in range(n)]
        mine = [pltpu.make_async_copy(sems[3 + t], _piece(outs[t], axes[t], sizes[t], me), sems[2].at[t])
                for t in range(n)] if phases != "second" else []
        return cps, mine

    def send_own(ins, outs, sems):
        cps, mine = plan(ins, outs, sems)
        for t in range(n):
            stage = pltpu.make_async_copy(ins[t], sems[3 + t], sems[2].at[t])
            stage.start()
            stage.wait()
            mine[t].start()
            for k in (0, 1, 2):
                cps[t][k].start()

    def pass_on(ins, outs, sems):
        cps, _ = plan(ins, outs, sems)
        for t in range(n):
            if phases == "all":
                cps[t][1].wait_recv()
            cps[t][3].start()
            cps[t][5].start()
        for t in range(n):
            if phases == "all":
                cps[t][2].wait_recv()
            cps[t][4].start()
            cps[t][6].start()

    def own_landed(ins, outs, sems):
        cps, mine = plan(ins, outs, sems)
        for t in range(n):
            for k in (0, 1, 2):
                cps[t][k].wait()
            mine[t].wait()

    def all_landed(ins, outs, sems):
        cps, mine = plan(ins, outs, sems)
        for t in range(n):
            cps[t][3].wait_recv()
            cps[t][4].wait_recv()
            cps[t][7].start()
        for t in range(n):
            for k in ((0, 5, 6, 7) if phases == "all" else (5, 6, 7)):
                cps[t][k].wait_recv()
            for k in (range(N_GATHER) if phases == "all" else range(3, N_GATHER)):
                cps[t][k].wait_send()
            if phases == "all":
                mine[t].wait()

    sems = [pltpu.SemaphoreType.DMA((N_GATHER * n,)), pltpu.SemaphoreType.DMA((N_GATHER * n,))]
    if phases != "second":
        sems.append(pltpu.SemaphoreType.DMA((n,)))
        sems += [pltpu.VMEM(s.shape, s.dtype) for s in shards]
    if phases == "all":
        return _Comm(shards, full, sems, send_own, all_landed, middle=pass_on)
    if phases == "first":
        return _Comm(shards, full, sems, send_own, own_landed)
    return _Comm(shards, full, sems, pass_on, all_landed, aliases={t: t for t in range(n)})


def _pair_comm(grads, axes, sizes):
    n = len(grads)
    outs_sds = [_sds((4,) + g.shape[:a] + (s,) + g.shape[a + 1:], g.dtype) for g, a, s in zip(grads, axes, sizes)]

    def copies(ins, outs, sems):
        send_sems, recv_sems = sems
        x, y, c = _place()
        return [pltpu.make_async_remote_copy(
            src_ref=_piece(ins[t], axes[t], sizes[t], 2 * qi + (1 - c)), dst_ref=outs[t].at[qi],
            send_sem=send_sems.at[4 * t + qi], recv_sem=recv_sems.at[4 * t + qi],
            device_id=(x, y, 1 - c), device_id_type=MESH) for t in range(n) for qi in range(4)]

    def start(ins, outs, sems):
        for cp in copies(ins, outs, sems):
            cp.start()

    def finish(ins, outs, sems):
        for cp in copies(ins, outs, sems):
            cp.wait()

    sems = [pltpu.SemaphoreType.DMA((4 * n,)), pltpu.SemaphoreType.DMA((4 * n,))]
    return _Comm(grads, outs_sds, sems, start, finish)


def _chip_comm(sums):
    n = len(sums)
    outs_sds = [_sds((3,) + s.shape[1:], s.dtype) for s in sums]

    def copies(ins, outs, sems):
        send_sems, recv_sems = sems
        x, y, c = _place()
        return [pltpu.make_async_remote_copy(
            src_ref=ins[t].at[2 * qx + qy], dst_ref=outs[t].at[j],
            send_sem=send_sems.at[3 * t + j], recv_sem=recv_sems.at[3 * t + j],
            device_id=(qx, qy, c), device_id_type=MESH)
            for t in range(n) for j, (qx, qy) in enumerate([(1 - x, y), (x, 1 - y), (1 - x, 1 - y)])]

    def start(ins, outs, sems):
        for cp in copies(ins, outs, sems):
            cp.start()

    def finish(ins, outs, sems):
        for cp in copies(ins, outs, sems):
            cp.wait()

    sems = [pltpu.SemaphoreType.DMA((3 * n,)), pltpu.SemaphoreType.DMA((3 * n,))]
    return _Comm(sums, outs_sds, sems, start, finish)


def _small_comm(small):
    def copies(ins, outs, sems):
        send_sems, recv_sems, local_sem = sems
        x, y, c = _place()
        mine = outs[0].at[4 * x + 2 * y + c]
        out = [pltpu.make_async_copy(ins[0], mine, local_sem.at[0])]
        for k in range(1, N_DEV):
            peer = (1 - x if k & 4 else x, 1 - y if k & 2 else y, 1 - c if k & 1 else c)
            out.append(pltpu.make_async_remote_copy(
                src_ref=ins[0], dst_ref=mine, send_sem=send_sems.at[k - 1], recv_sem=recv_sems.at[k - 1],
                device_id=peer, device_id_type=MESH))
        return out

    def start(ins, outs, sems):
        for cp in copies(ins, outs, sems):
            cp.start()

    def finish(ins, outs, sems):
        for cp in copies(ins, outs, sems):
            cp.wait()

    sems = [pltpu.SemaphoreType.DMA((N_DEV - 1,)), pltpu.SemaphoreType.DMA((N_DEV - 1,)), pltpu.SemaphoreType.DMA((1,))]
    return _Comm([small], [_sds((N_DEV,) + small.shape, small.dtype)], sems, start, finish)


def _small_scatter_comm(send):
    def copies(ins, outs, sems):
        send_sems, recv_sems, local_sem = sems
        x, y, c = _place()
        me = 4 * x + 2 * y + c
        out = [pltpu.make_async_copy(ins[0].at[me], outs[0].at[me], local_sem.at[0])]
        for k in range(1, N_DEV):
            px, py, pc = (1 - x if k & 4 else x, 1 - y if k & 2 else y, 1 - c if k & 1 else c)
            out.append(pltpu.make_async_remote_copy(
                src_ref=ins[0].at[4 * px + 2 * py + pc], dst_ref=outs[0].at[me], send_sem=send_sems.at[k - 1],
                recv_sem=recv_sems.at[k - 1], device_id=(px, py, pc), device_id_type=MESH))
        return out

    def start(ins, outs, sems):
        for cp in copies(ins, outs, sems):
            cp.start()

    def finish(ins, outs, sems):
        for cp in copies(ins, outs, sems):
            cp.wait()

    sems = [pltpu.SemaphoreType.DMA((N_DEV - 1,)), pltpu.SemaphoreType.DMA((N_DEV - 1,)), pltpu.SemaphoreType.DMA((1,))]
    return _Comm([send], [_sds(send.shape, send.dtype)], sems, start, finish)


def _pair_sum(c_idx, grad, recv, axis, size, split, *, name):
    nd = len(grad.shape)
    piece = grad.shape[:axis] + (size,) + grad.shape[axis + 1:]
    blk = (piece[0] // split,) + piece[1:]

    def g_map(q, r, c_ref):
        idx = [0] * nd
        idx[axis] = 2 * q + c_ref[0]
        idx[0] = idx[0] * split + r if axis == 0 else r
        return tuple(idx)

    def r_map(q, r, c_ref):
        return (q, r) + (0,) * (nd - 1)

    def body(c_ref, g_ref, r_ref, o_ref):
        o_ref[0] = (g_ref[...].astype(F32) + r_ref[0].astype(F32)).astype(BF16)

    return _call(
        body, grid=(4, split), prefetch=c_idx,
        in_specs=[pl.BlockSpec(blk, g_map), pl.BlockSpec((1,) + blk, r_map)],
        out_specs=[pl.BlockSpec((1,) + blk, r_map)], out_shape=[_sds((4,) + piece, BF16)],
        operands=(grad, recv), name=name, params=_params(("arbitrary", "arbitrary"), 32))[0][0]


def _adam_math(w, g, m, v):
    m = ADAM_B1 * m + (1.0 - ADAM_B1) * g
    v = ADAM_B2 * v + (1.0 - ADAM_B2) * (g * g)
    m_hat = m / (1.0 - ADAM_B1 ** ADAM_STEP)
    v_hat = v / (1.0 - ADAM_B2 ** ADAM_STEP)
    delta = -ADAM_LR * (m_hat / (jnp.sqrt(v_hat) + ADAM_EPS) + ADAM_WD * w)
    return delta, m, v


def _adam_big(q_idx, sums, recv, w, m, v, split, *, name, comm=None):
    shape = w.shape
    nd = len(shape)
    blk = (shape[0] // split,) + shape[1:]
    w_map = lambda r, q_ref: (r,) + (0,) * (nd - 1)
    s_map = lambda r, q_ref: (q_ref[0], r) + (0,) * (nd - 1)
    r_map = lambda r, q_ref: (0, r) + (0,) * (nd - 1)

    def body(q_ref, s_ref, r_ref, w_ref, m_ref, v_ref, g_ref, d_ref, nm_ref, nv_ref):
        g = s_ref[0].astype(F32) + r_ref[0].astype(F32) + r_ref[1].astype(F32) + r_ref[2].astype(F32)
        g_ref[...] = g
        d_ref[...], nm_ref[...], nv_ref[...] = _adam_math(w_ref[...], g, m_ref[...], v_ref[...])

    wspec = pl.BlockSpec(blk, w_map)
    return _call(
        body, grid=(split,), prefetch=q_idx,
        in_specs=[pl.BlockSpec((1,) + blk, s_map), pl.BlockSpec((3,) + blk, r_map), wspec, wspec, wspec],
        out_specs=[wspec] * 4, out_shape=[_sds(shape, F32)] * 4, operands=(sums, recv, w, m, v),
        name=name, params=_params(("arbitrary",), 32), comm=comm)


def _adam_small(parts, w, m, v, *, name):
    R = w.shape[0]

    def body(p_ref, w_ref, m_ref, v_ref, g_ref, d_ref, nm_ref, nv_ref):
        g = p_ref[0]
        for d in range(1, N_DEV):
            g = g + p_ref[d]
        g_ref[...] = g
        d_ref[...], nm_ref[...], nv_ref[...] = _adam_math(w_ref[...], g, m_ref[...], v_ref[...])

    whole = _const((R, LANES))
    return _call(
        body, grid=(1,), in_specs=[_const((N_DEV, R, LANES)), whole, whole, whole], out_specs=[whole] * 4,
        out_shape=[_sds((R, LANES), F32)] * 4, operands=(parts, w, m, v), name=name,
        params=_params(("arbitrary",), 32))[0]


def _pack(arrs):
    return jnp.concatenate([a.reshape(-1) for a in arrs]).reshape(-1, LANES)


def _unpack(packed, shapes):
    flat = packed.reshape(-1)
    out, off = [], 0
    for s in shapes:
        n = 1
        for d in s:
            n *= d
        out.append(flat[off:off + n].reshape(s))
        off += n
    return out


BIG = ("e_in", "e_out", "o_in", "o_cw", "o_out")
BIG_AXIS = dict(e_in=1, e_out=0, o_in=1, o_cw=1, o_out=0)
BIG_SPLIT = dict(e_in=8, e_out=4, o_in=4, o_cw=4, o_out=4)
REPLICATED = ("e_norm_pre", "e_norm_post", "e_b_conv_bias", "e_b_ln_g", "e_b_ln_b")
SHARDED = ("e_a_conv", "e_b_conv", "o_norm_pre", "o_norm_post", "o_c_b", "o_c_scale")
SMALL = REPLICATED + SHARDED


class _Exchange:
    def __init__(self, shards, small, order, c_idx):
        self.q_idx = order[:1]
        self.shards = shards
        self.small = small
        self.order = order
        self.c_idx = c_idx
        self.reduced = {}

    def gather(self, keys):
        return _gather_comm([self.shards[k] for k in keys], [BIG_AXIS[k] for k in keys], "all")

    def gather1(self, keys):
        return _gather_comm([self.shards[k] for k in keys], [BIG_AXIS[k] for k in keys], "first")

    def gather2(self, keys, firsts):
        return _gather_comm(firsts, [BIG_AXIS[k] for k in keys], "second")

    def pair(self, grads):
        keys = list(grads)
        return _pair_comm([grads[k] for k in keys], [BIG_AXIS[k] for k in keys],
                          [grads[k].shape[BIG_AXIS[k]] // N_DEV for k in keys])

    def pair_sums(self, grads, received):
        return {k: _pair_sum(self.c_idx, grads[k], r, BIG_AXIS[k], grads[k].shape[BIG_AXIS[k]] // N_DEV,
                             BIG_SPLIT[k], name="pair_sum_" + k) for k, r in zip(grads, received)}

    def chips(self, sums):
        return _chip_comm([sums[k] for k in sums])

    def done(self, sums, received):
        self.reduced.update({k: (sums[k], r, self.q_idx) for k, r in zip(sums, received)})


def _local_step(x, tgt, w_small, ex):
    S, D = x.shape
    tnt, tx, tw = min(TM_NT, S), min(TM_MIX, S), min(TM_WIDE, S)

    wt = {}
    p, h0, wt["e_in"], got = _gather_matmul(ex.order, x, w_small["e_norm_pre"], ex.shards["e_in"], tm=tw,
                                            name="e_in_fwd", comm=_small_comm(ex.small))
    per_dev = [_unpack(got[0][d], [w_small[k].shape for k in SHARDED]) for d in range(N_DEV)]
    sm = {k: w_small[k] for k in REPLICATED}
    for j, k in enumerate(SHARDED):
        sm[k] = jnp.concatenate([per_dev[d][j] for d in range(N_DEV)], axis=-1)
    n_groups = sm["o_c_b"].shape[0]
    sm["o_c_b"] = sm["o_c_b"].reshape(1, -1)

    W = p.shape[1] // 7
    (u, cb), got = _e_mix_fwd(p, sm["e_a_conv"], sm["e_b_conv"], sm["e_b_conv_bias"], sm["e_b_ln_g"],
                              sm["e_b_ln_b"], tm=tx, name="e_mix_fwd", comm=ex.gather(["e_out"]))
    wt["e_out"] = got[0]
    late = ["o_out", "o_cw"]
    (x1, y0), part = _out_norm_res(u, wt["e_out"], x, sm["e_norm_post"], tm=tw, name="e_out_fwd",
                                   comm=ex.gather1(late))
    q, h1, wt["o_in"], got = _gather_matmul(ex.order, x1, sm["o_norm_pre"], ex.shards["o_in"], tm=tw,
                                            name="o_in_fwd", comm=ex.gather2(late, part))
    wt.update(zip(late, got))
    yy, pooled, gg = _o_mix_fwd(q, wt["o_cw"], sm["o_c_b"], sm["o_c_scale"], tm=tw, name="o_mix_fwd")
    dout, dx2, dyy, lcol, dg_o_post = _out_loss(yy, wt["o_out"], x1, sm["o_norm_post"], tgt, tm=tx, name="o_out_loss")
    loss = (0.5 / D) * jnp.sum(lcol)

    dq, d_cw, d_cb, d_cscale = _o_mix_bwd(dyy, q, gg, pooled, wt["o_cw"], sm["o_c_scale"], tm=tw, name="o_mix_bwd")
    g_o_out, _ = _mm_tn(yy, dout, ts=tnt, tn=W, name="o_out_dw")
    ga = dict(o_out=g_o_out, o_cw=d_cw.astype(BF16))
    dh1, ra = _mm_nt(dq, wt["o_in"], tm=tnt, tk=W, name="o_in_bwd", comm=ex.pair(ga))
    sa = ex.pair_sums(ga, ra)
    (dx1, dy0, dg_o_pre, dg_e_post), ra = _pre_bwd_o(dh1, x1, dx2, y0, sm["o_norm_pre"], sm["e_norm_post"],
                                                     tm=tx, name="o_pre_bwd", comm=ex.chips(sa))
    ex.done(sa, ra)
    g_o_in, _ = _mm_tn(h1, dq, ts=tnt, tn=W, name="o_in_dw")
    gb = dict(o_in=g_o_in)
    du, rb = _mm_nt(dy0, wt["e_out"], tm=tnt, tk=W, name="e_out_bwd", comm=ex.pair(gb))
    sb = ex.pair_sums(gb, rb)
    g_e_out, _ = _mm_tn(u, dy0, ts=tnt, tn=W, name="e_out_dw")
    gc = dict(e_out=g_e_out)
    (dp, d_wa, d_wb, d_bias, d_lg, d_lb), rbc = _e_mix_bwd(
        du, p, cb, sm["e_a_conv"], sm["e_b_conv"], sm["e_b_ln_g"], sm["e_b_ln_b"], tm=tx, name="e_mix_bwd",
        comm=_merge(ex.chips(sb), ex.pair(gc)))
    ex.done(sb, rbc[:1])
    sc = ex.pair_sums(gc, rbc[1:])
    order_out = jnp.concatenate([ex.order[1:], ex.order[:1]])
    sd, from_chip, rc = _dw_reduce(order_out, h0, dp, ex.shards["e_in"].shape[1], ts=tnt, name="e_in_dw",
                                   comm=ex.chips(sc))
    ex.done(sc, rc)
    dh0, rd = _mm_nt(dp, wt["e_in"], tm=tnt, tk=W, name="e_in_bwd", comm=_diag_comm(sd, from_chip))
    ex.reduced["e_in"] = (sd, rd[0], jnp.full((1,), 3, jnp.int32))
    grad_x, dg_e_pre = _pre_bwd_e(dh0, x, dx1, sm["e_norm_pre"], tm=tw, name="e_pre_bwd")

    small = dict(e_norm_pre=dg_e_pre, e_norm_post=dg_e_post, e_a_conv=d_wa, e_b_conv=d_wb, e_b_conv_bias=d_bias,
                 e_b_ln_g=d_lg, e_b_ln_b=d_lb, o_norm_pre=dg_o_pre, o_norm_post=dg_o_post,
                 o_c_b=d_cb.reshape(n_groups, -1), o_c_scale=d_cscale)
    return loss, grad_x, small


def kernel(x, e_norm_pre, e_norm_post, e_w_in, e_a_conv, e_b_conv, e_b_conv_bias, e_b_ln_g, e_b_ln_b, e_w_out, o_norm_pre, o_norm_post, o_w_in, o_c_w, o_c_b, o_c_scale, o_w_out, loss_target, m_e_norm_pre, m_e_norm_post, m_e_w_in, m_e_a_conv, m_e_b_conv, m_e_b_conv_bias, m_e_b_ln_g, m_e_b_ln_b, m_e_w_out, m_o_norm_pre, m_o_norm_post, m_o_w_in, m_o_c_w, m_o_c_b, m_o_c_scale, m_o_w_out, v_e_norm_pre, v_e_norm_post, v_e_w_in, v_e_a_conv, v_e_b_conv, v_e_b_conv_bias, v_e_b_ln_g, v_e_b_ln_b, v_e_w_out, v_o_norm_pre, v_o_norm_post, v_o_w_in, v_o_c_w, v_o_c_b, v_o_c_scale, v_o_w_out):
    xi, yi, ci = _place()
    w_big = dict(e_in=e_w_in[0], e_out=e_w_out[0], o_in=o_w_in[0], o_cw=o_c_w[0], o_out=o_w_out[0])
    m_big = dict(e_in=m_e_w_in[0], e_out=m_e_w_out[0], o_in=m_o_w_in[0], o_cw=m_o_c_w[0], o_out=m_o_w_out[0])
    v_big = dict(e_in=v_e_w_in[0], e_out=v_e_w_out[0], o_in=v_o_w_in[0], o_cw=v_o_c_w[0], o_out=v_o_w_out[0])
    w_small = dict(e_norm_pre=e_norm_pre, e_norm_post=e_norm_post, e_b_conv_bias=e_b_conv_bias, e_b_ln_g=e_b_ln_g,
                   e_b_ln_b=e_b_ln_b, e_a_conv=e_a_conv[0], e_b_conv=e_b_conv[0], o_norm_pre=o_norm_pre,
                   o_norm_post=o_norm_post, o_c_b=o_c_b[0], o_c_scale=o_c_scale)
    m_small = dict(e_norm_pre=m_e_norm_pre, e_norm_post=m_e_norm_post, e_b_conv_bias=m_e_b_conv_bias,
                   e_b_ln_g=m_e_b_ln_g, e_b_ln_b=m_e_b_ln_b, e_a_conv=m_e_a_conv[0], e_b_conv=m_e_b_conv[0],
                   o_norm_pre=m_o_norm_pre, o_norm_post=m_o_norm_post, o_c_b=m_o_c_b[0], o_c_scale=m_o_c_scale)
    v_small = dict(e_norm_pre=v_e_norm_pre, e_norm_post=v_e_norm_post, e_b_conv_bias=v_e_b_conv_bias,
                   e_b_ln_g=v_e_b_ln_g, e_b_ln_b=v_e_b_ln_b, e_a_conv=v_e_a_conv[0], e_b_conv=v_e_b_conv[0],
                   o_norm_pre=v_o_norm_pre, o_norm_post=v_o_norm_post, o_c_b=v_o_c_b[0], o_c_scale=v_o_c_scale)

    c_idx = jnp.reshape(ci, (1,)).astype(jnp.int32)
    order = jnp.stack([2 * xi + yi, 2 * (1 - xi) + yi, 2 * xi + (1 - yi), 2 * (1 - xi) + (1 - yi)]).astype(jnp.int32)
    ex = _Exchange({k: w_big[k].astype(BF16) for k in BIG}, _pack([w_small[k] for k in SHARDED]), order, c_idx)
    loss, grad_x, g_small = _local_step(x[0], loss_target[0], w_small, ex)

    big_out = {}
    for k in BIG:
        sums, received, q_idx = ex.reduced[k]
        big_out[k] = _adam_big(q_idx, sums, received, w_big[k], m_big[k], v_big[k], BIG_SPLIT[k], name="adam_" + k)[0]

    rep = _pack([g_small[k] for k in REPLICATED])
    loss_row = jnp.pad(jnp.reshape(loss, (1, 1)), ((0, 0), (0, LANES - 1)))
    blocks = []
    for k in SHARDED:
        r, n = w_small[k].shape
        blocks.append(g_small[k].reshape(r, N_DEV, n).transpose(1, 0, 2).reshape(N_DEV, r * n))
    blocks = jnp.concatenate(blocks, axis=1).reshape(N_DEV, -1, LANES)
    head = jnp.concatenate([rep, loss_row], axis=0)
    send = jnp.concatenate([jnp.broadcast_to(head[None], (N_DEV,) + head.shape), blocks], axis=1)
    parts = _run_comm(_small_scatter_comm(send), "small_grad_exchange")[0]

    def own_rows(d):
        return jnp.concatenate([_pack([d[k] for k in REPLICATED]), jnp.ones((1, LANES), F32),
                                _pack([d[k] for k in SHARDED])], axis=0)

    res_small = _adam_small(parts, own_rows(w_small), own_rows(m_small), own_rows(v_small), name="adam_small")
    n_rep = rep.shape[0]
    loss = res_small[0][n_rep, 0]
    small_out = {k: [] for k in SMALL}
    for packed in res_small:
        for k, t in zip(REPLICATED, _unpack(packed[:n_rep], [w_small[k].shape for k in REPLICATED])):
            small_out[k].append(t)
        for k, t in zip(SHARDED, _unpack(packed[n_rep + 1:], [w_small[k].shape for k in SHARDED])):
            small_out[k].append(t)

    big_of = dict(e_w_in="e_in", e_w_out="e_out", o_w_in="o_in", o_c_w="o_cw", o_w_out="o_out")
    stacked = ("e_a_conv", "e_b_conv", "o_c_b")

    def leaf(name, which):
        if name in big_of:
            return big_out[big_of[name]][which][None]
        t = small_out[name][which]
        return t[None] if name in stacked else t

    order = ("e_norm_pre", "e_norm_post", "e_w_in", "e_a_conv", "e_b_conv", "e_b_conv_bias", "e_b_ln_g", "e_b_ln_b",
             "e_w_out", "o_norm_pre", "o_norm_post", "o_w_in", "o_c_w", "o_c_b", "o_c_scale", "o_w_out")
    outs = [loss, grad_x[None]]
    for which in range(4):
        outs += [leaf(nm, which) for nm in order]
    return tuple(outs)
```

```python
import jax
import jax.numpy as jnp
from jax import lax
from jax.experimental import pallas as pl
from jax.experimental.pallas import tpu as pltpu

F32 = jnp.float32
BF16 = jnp.bfloat16
EPS = 1e-6
MESH = pl.DeviceIdType.MESH
ANY = pl.BlockSpec(memory_space=pl.ANY)

N_DEV = 8
HALO = 32
PHALO = 16
CONV_A = 3
CONV_B = 31
POOL_WINDOWS = (2, 4, 8, 16)
LANES = 128
MIB = 1024 * 1024
LOCAL = 1

ADAM_LR = 0.001
ADAM_B1 = 0.9
ADAM_B2 = 0.999
ADAM_EPS = 1e-08
ADAM_WD = 0.01
ADAM_STEP = 10

TM_NT = 1024
TM_MIX = 256
TM_WIDE = 512


def _sds(shape, dtype):
    return jax.ShapeDtypeStruct(tuple(shape), dtype)


def _params(sem, vmem_mib):
    return pltpu.CompilerParams(dimension_semantics=sem, vmem_limit_bytes=vmem_mib * MIB)


def _const(shape, single=False):
    n = len(shape)
    if single:
        return pl.BlockSpec(shape, lambda *_: (0,) * n, pipeline_mode=pl.Buffered(1))
    return pl.BlockSpec(shape, lambda *_: (0,) * n)


def _sig(v):
    return jax.nn.sigmoid(v)


def _dsilu(v, s):
    return s * (1.0 + v * (1.0 - s))


def _rms(v):
    return lax.rsqrt(jnp.mean(v * v, axis=-1, keepdims=True) + EPS)


def _norm_bwd(dn, n, r):
    return r * (dn - n * jnp.mean(dn * n, axis=-1, keepdims=True))


def _colsum(v):
    return jnp.sum(v, axis=0, keepdims=True)


class _Comm:
    def __init__(self, inputs, out_shapes, sems, start, finish, aliases=None, middle=None):
        self.inputs, self.out_shapes, self.sems = list(inputs), list(out_shapes), list(sems)
        self.start, self.finish, self.middle = start, finish, middle
        self.aliases = dict(aliases or {})


def _merge(*comms):
    comms = [c for c in comms if c is not None]
    if len(comms) <= 1:
        return comms[0] if comms else None
    spans, i0, o0, s0, aliases = [], 0, 0, 0, {}
    for c in comms:
        spans.append((i0, o0, s0))
        aliases.update({i0 + k: o0 + v for k, v in c.aliases.items()})
        i0, o0, s0 = i0 + len(c.inputs), o0 + len(c.out_shapes), s0 + len(c.sems)

    def run(which):
        def fn(ins, outs, sems):
            for c, (i, o, s) in zip(comms, spans):
                hook = getattr(c, which)
                if hook is not None:
                    hook(ins[i:i + len(c.inputs)], outs[o:o + len(c.out_shapes)], sems[s:s + len(c.sems)])
        return fn

    return _Comm([a for c in comms for a in c.inputs], [a for c in comms for a in c.out_shapes],
                 [a for c in comms for a in c.sems], run("start"), run("finish"), aliases,
                 run("middle") if any(c.middle is not None for c in comms) else None)


def _call(body, *, grid, in_specs, out_specs, out_shape, operands, name, params, scratch_shapes=(), comm=None,
          prefetch=None, own_copies_first=False):
    n_p = 0 if prefetch is None else 1
    n_i, n_o, n_s = len(in_specs), len(out_specs), len(scratch_shapes)
    if comm is None:
        comm = _Comm([], [], [], None, None)
    c_i, c_o = len(comm.inputs), len(comm.out_shapes)

    def carrier(*refs):
        pre, refs = refs[:n_p], refs[n_p:]
        ins, cins = refs[:n_i], refs[n_i:n_i + c_i]
        outs = refs[n_i + c_i:n_i + c_i + n_o]
        couts = refs[n_i + c_i + n_o:n_i + c_i + n_o + c_o]
        scr = refs[n_i + c_i + n_o + c_o:n_i + c_i + n_o + c_o + n_s]
        csems = refs[n_i + c_i + n_o + c_o + n_s:]
        ids = [pl.program_id(d) for d in range(len(grid))]
        first = ids[0] == 0
        half = ids[0] == grid[0] // 2
        last = ids[0] == grid[0] - 1
        for d in range(1, len(grid)):
            first = first & (ids[d] == 0)
            half = half & (ids[d] == 0)
            last = last & (ids[d] == grid[d] - 1)

        def start():
            if comm.start is not None:
                @pl.when(first)
                def _():
                    comm.start(cins, couts, csems)

        if not own_copies_first:
            start()
        if comm.middle is not None:
            assert grid[0] >= 2

            @pl.when(half)
            def _():
                comm.middle(cins, couts, csems)

        body(*pre, *ins, *outs, *scr)
        if own_copies_first:
            start()

        if comm.finish is not None:
            @pl.when(last)
            def _():
                comm.finish(cins, couts, csems)

    specs = dict(grid=grid, in_specs=list(in_specs) + [ANY] * c_i, out_specs=list(out_specs) + [ANY] * c_o,
                 scratch_shapes=list(scratch_shapes) + comm.sems)
    if n_p:
        specs = dict(grid_spec=pltpu.PrefetchScalarGridSpec(num_scalar_prefetch=1, **specs))
    res = pl.pallas_call(
        carrier, out_shape=list(out_shape) + comm.out_shapes,
        input_output_aliases={n_p + n_i + k: n_o + v for k, v in comm.aliases.items()},
        name=name, compiler_params=params, **specs)(*(() if prefetch is None else (prefetch,)), *operands, *comm.inputs)
    return list(res[:n_o]), list(res[n_o:])


def _run_comm(comm, name):
    c_i, c_o = len(comm.inputs), len(comm.out_shapes)

    def body(*refs):
        ins, outs, sems = refs[:c_i], refs[c_i:c_i + c_o], refs[c_i + c_o:]
        comm.start(ins, outs, sems)
        comm.finish(ins, outs, sems)

    res = pl.pallas_call(
        body, in_specs=[ANY] * c_i, out_specs=[ANY] * c_o, out_shape=comm.out_shapes, scratch_shapes=comm.sems,
        input_output_aliases=comm.aliases, name=name)(*comm.inputs)
    return list(res)


def _gather_matmul(order, x, g, shard, *, tm, name, comm=None):
    S, K = x.shape
    nb = shard.shape[1]
    n_i = S // tm

    def body(order_ref, x_ref, g_ref, shard_ref, p_ref, h_ref, full_ref, hbuf, wbuf, stage, send_sems, recv_sems,
             dma_sems):
        j, i = pl.program_id(0), pl.program_id(1)
        px, py, pc = _place()
        cps = _gather_copies(stage, full_ref, 1, nb, send_sems, recv_sems, 0)
        own = pltpu.make_async_copy(stage, _piece(full_ref, 1, nb, 4 * px + 2 * py + pc), dma_sems.at[0])
        keep_h = pltpu.make_async_copy(hbuf, h_ref, dma_sems.at[2])

        def load(src, dst):
            cp = pltpu.make_async_copy(src, dst, dma_sems.at[1])
            cp.start(priority=LOCAL)
            cp.wait()

        def load_pair(qx, qy):
            load(_piece(full_ref, 1, 2 * nb, 2 * qx + qy), wbuf)

        @pl.when((j == 0) & (i == 0))
        def _():
            load(shard_ref, stage)
            own.start(priority=LOCAL)
            for k in (0, 1, 2):
                cps[k].start()

        @pl.when(j == 0)
        def _():
            xx = x_ref[...]
            hbuf[i] = ((xx * _rms(xx)) * g_ref[...]).astype(BF16)

        @pl.when((j == 0) & (i == 0))
        def _():
            own.wait()
            cps[0].wait_recv()
            load_pair(px, py)

        @pl.when((j == 1) & (i == 0))
        def _():
            keep_h.start(priority=LOCAL)
            cps[1].wait_recv()
            cps[3].start()
            cps[5].start()
            cps[2].wait_recv()
            cps[4].start()
            cps[6].start()
            cps[5].wait_recv()
            load_pair(1 - px, py)

        @pl.when((j == 2) & (i == 0))
        def _():
            cps[6].wait_recv()
            load_pair(px, 1 - py)

        @pl.when((j == 2) & (i == n_i // 2))
        def _():
            cps[3].wait_recv()
            cps[4].wait_recv()
            cps[7].start()

        @pl.when((j == 3) & (i == 0))
        def _():
            cps[7].wait_recv()
            load_pair(1 - px, 1 - py)

        p_ref[...] = jnp.dot(hbuf[i], wbuf[...], preferred_element_type=F32).astype(BF16)

        @pl.when((j == 3) & (i == n_i - 1))
        def _():
            for cp in cps:
                cp.wait_send()
            keep_h.wait()

    first_pass = lambda j, i, o: (jnp.where(j == 0, i, n_i - 1), 0)
    outs, extra = _call(
        body, grid=(4, n_i), prefetch=order,
        in_specs=[pl.BlockSpec((tm, K), first_pass), pl.BlockSpec((1, K), lambda j, i, o: (0, 0)), ANY],
        out_specs=[pl.BlockSpec((tm, 2 * nb), lambda j, i, o: (i, o[j])), ANY, ANY],
        out_shape=[_sds((S, N_DEV * nb), BF16), _sds((n_i, tm, K), BF16), _sds((K, N_DEV * nb), BF16)],
        operands=(x, g, shard),
        scratch_shapes=[pltpu.VMEM((n_i, tm, K), BF16), pltpu.VMEM((K, 2 * nb), BF16), pltpu.VMEM((K, nb), BF16),
                        pltpu.SemaphoreType.DMA((N_GATHER,)), pltpu.SemaphoreType.DMA((N_GATHER,)),
                        pltpu.SemaphoreType.DMA((3,))],
        name=name, params=_params(("arbitrary", "arbitrary"), 58), comm=comm, own_copies_first=True)
    return outs[0], outs[1].reshape(S, K), outs[2], extra


def _out_norm_res(u, w, x, g, *, tm, name, comm=None):
    S, K = u.shape
    D = w.shape[1]

    def body(u_ref, w_ref, x_ref, g_ref, x1_ref, y_ref):
        y = jnp.dot(u_ref[...], w_ref[...], preferred_element_type=F32)
        y_ref[...] = y.astype(BF16)
        x1_ref[...] = x_ref[...] + (y * _rms(y)) * g_ref[...]

    return _call(
        body, grid=(S // tm,),
        in_specs=[pl.BlockSpec((tm, K), lambda i: (i, 0)), _const((K, D), single=True),
                  pl.BlockSpec((tm, D), lambda i: (i, 0)), _const((1, D))],
        out_specs=[pl.BlockSpec((tm, D), lambda i: (i, 0)), pl.BlockSpec((tm, D), lambda i: (i, 0))],
        out_shape=[_sds((S, D), F32), _sds((S, D), BF16)], operands=(u, w, x, g),
        name=name, params=_params(("arbitrary",), 56), comm=comm)


def _out_loss(yy, w, x1, g, tgt, *, tm, name):
    S, K = yy.shape
    D = w.shape[1]

    def body(yy_ref, w_ref, x1_ref, g_ref, t_ref, dout_ref, dx2_ref, dyy_ref, lcol_ref, dg_ref):
        out = jnp.dot(yy_ref[...], w_ref[...], preferred_element_type=F32)
        r = _rms(out)
        n = out * r
        gg = g_ref[...]
        e = x1_ref[...] + n * gg - t_ref[...]
        dx2 = e * (1.0 / D)
        dx2_ref[...] = dx2
        dout = _norm_bwd(dx2 * gg, n, r).astype(BF16)
        dout_ref[...] = dout
        dyy_ref[...] = lax.dot_general(dout, w_ref[...], (((1,), (1,)), ((), ())),
                                       preferred_element_type=F32).astype(BF16)

        @pl.when(pl.program_id(0) == 0)
        def _():
            lcol_ref[...] = jnp.zeros_like(lcol_ref)
            dg_ref[...] = jnp.zeros_like(dg_ref)

        lcol_ref[...] += _colsum(e * e)
        dg_ref[...] += _colsum(dx2 * n)

    return _call(
        body, grid=(S // tm,),
        in_specs=[pl.BlockSpec((tm, K), lambda i: (i, 0)), _const((K, D), single=True),
                  pl.BlockSpec((tm, D), lambda i: (i, 0)), _const((1, D)),
                  pl.BlockSpec((tm, D), lambda i: (i, 0))],
        out_specs=[pl.BlockSpec((tm, D), lambda i: (i, 0)), pl.BlockSpec((tm, D), lambda i: (i, 0)),
                   pl.BlockSpec((tm, K), lambda i: (i, 0)), _const((1, D)), _const((1, D))],
        out_shape=[_sds((S, D), BF16), _sds((S, D), F32), _sds((S, K), BF16), _sds((1, D), F32), _sds((1, D), F32)],
        operands=(yy, w, x1, g, tgt), name=name, params=_params(("arbitrary",), 52))[0]


def _mm_nt(a, w, *, tm, tk, name, comm=None):
    S, N = a.shape
    D = w.shape[0]
    n_k = N // tk

    def body(a_ref, w_ref, o_ref, acc_ref):
        k = pl.program_id(1)

        @pl.when(k == 0)
        def _():
            acc_ref[...] = jnp.zeros_like(acc_ref)

        acc_ref[...] = lax.dot_general(a_ref[...], w_ref[...], (((1,), (1,)), ((), ())),
                                       preferred_element_type=F32) + acc_ref[...]

        @pl.when(k == n_k - 1)
        def _():
            o_ref[...] = acc_ref[...].astype(BF16)

    outs, extra = _call(
        body, grid=(S // tm, n_k),
        in_specs=[pl.BlockSpec((tm, tk), lambda i, k: (i, k)), pl.BlockSpec((D, tk), lambda i, k: (0, k))],
        out_specs=[pl.BlockSpec((tm, D), lambda i, k: (i, 0))],
        out_shape=[_sds((S, D), BF16)], operands=(a, w),
        scratch_shapes=[pltpu.VMEM((tm, D), F32)],
        name=name, params=_params(("arbitrary", "arbitrary"), 48), comm=comm)
    return outs[0], extra


def _mm_tn(a, b, *, ts, tn, name, comm=None):
    S, M = a.shape
    N = b.shape[1]
    n_s = S // ts

    def body(a_ref, b_ref, o_ref, acc_ref):
        s = pl.program_id(1)

        @pl.when(s == 0)
        def _():
            acc_ref[...] = jnp.zeros_like(acc_ref)

        acc_ref[...] = lax.dot_general(a_ref[...], b_ref[...], (((0,), (0,)), ((), ())),
                                       preferred_element_type=F32) + acc_ref[...]

        @pl.when(s == n_s - 1)
        def _():
            o_ref[...] = acc_ref[...].astype(BF16)

    outs, extra = _call(
        body, grid=(N // tn, n_s),
        in_specs=[pl.BlockSpec((ts, M), lambda j, s: (s, 0)), pl.BlockSpec((ts, tn), lambda j, s: (s, j))],
        out_specs=[pl.BlockSpec((M, tn), lambda j, s: (0, j))],
        out_shape=[_sds((M, N), BF16)], operands=(a, b),
        scratch_shapes=[pltpu.VMEM((M, tn), F32)],
        name=name, params=_params(("arbitrary", "arbitrary"), 48), comm=comm)
    return outs[0], extra


def _dw_reduce(order, a, b, nb, *, ts, name, comm=None):
    S, M = a.shape
    n_s = S // ts
    rows = 512

    def body(order_ref, a_ref, b_ref, sums_ref, from_sib_ref, from_chip_ref, acc, send_buf, mine_buf, recv_buf,
             sib_send, sib_recv, chip_send, chip_recv, dma_sems):
        t, s = pl.program_id(0), pl.program_id(1)
        x, y, c = _place()
        targets = [(1 - x, y, c), (x, 1 - y, c)]

        def to_sibling(k):
            return pltpu.make_async_remote_copy(
                src_ref=send_buf, dst_ref=from_sib_ref.at[k], send_sem=sib_send.at[k], recv_sem=sib_recv.at[k],
                device_id=(x, y, 1 - c), device_id_type=MESH)

        def to_chip(k):
            return pltpu.make_async_remote_copy(
                src_ref=sums_ref.at[k], dst_ref=from_chip_ref.at[k], send_sem=chip_send.at[k],
                recv_sem=chip_recv.at[k], device_id=targets[k], device_id_type=MESH)

        def finish(k):
            to_sibling(k).wait()
            get = pltpu.make_async_copy(from_sib_ref.at[k], recv_buf, dma_sems.at[0])
            get.start(priority=LOCAL)
            get.wait()
            for r in range(0, M, rows):
                recv_buf[r:r + rows, :] = (mine_buf[r:r + rows, :].astype(F32)
                                           + recv_buf[r:r + rows, :].astype(F32)).astype(BF16)
            put = pltpu.make_async_copy(recv_buf, sums_ref.at[k], dma_sems.at[1])
            put.start(priority=LOCAL)
            put.wait()
            if k < 2:
                to_chip(k).start()

        for k in range(3):
            @pl.when((t == k + 1) & (s == min(1, n_s - 1)))
            def _(k=k):
                finish(k)

        @pl.when(s == 0)
        def _():
            acc[...] = jnp.zeros_like(acc)

        acc[...] = lax.dot_general(a_ref[...], b_ref[...], (((0,), (0,)), ((), ())),
                                   preferred_element_type=F32) + acc[...]

        @pl.when(s == n_s - 1)
        def _():
            for r in range(0, M, rows):
                lo, hi = acc[r:r + rows, :nb], acc[r:r + rows, nb:]
                send_buf[r:r + rows, :] = jnp.where(c == 0, hi, lo).astype(BF16)
                mine_buf[r:r + rows, :] = jnp.where(c == 0, lo, hi).astype(BF16)
            to_sibling(t).start()

        @pl.when((t == 3) & (s == n_s - 1))
        def _():
            finish(3)
            to_chip(0).wait()
            to_chip(1).wait()

    piece = _sds((4, M, nb), BF16)
    outs, extra = _call(
        body, grid=(4, n_s), prefetch=order,
        in_specs=[pl.BlockSpec((ts, M), lambda t, s, o: (s, 0)), pl.BlockSpec((ts, 2 * nb), lambda t, s, o: (s, o[t]))],
        out_specs=[ANY, ANY, ANY], out_shape=[piece, piece, _sds((3, M, nb), BF16)], operands=(a, b),
        scratch_shapes=[pltpu.VMEM((M, 2 * nb), F32), pltpu.VMEM((M, nb), BF16), pltpu.VMEM((M, nb), BF16),
                        pltpu.VMEM((M, nb), BF16), pltpu.SemaphoreType.DMA((4,)), pltpu.SemaphoreType.DMA((4,)),
                        pltpu.SemaphoreType.DMA((2,)), pltpu.SemaphoreType.DMA((2,)), pltpu.SemaphoreType.DMA((2,))],
        name=name, params=_params(("arbitrary", "arbitrary"), 56), comm=comm)
    return outs[0], outs[2], extra


def _diag_comm(sums, from_chip):
    def copy(ins, outs, sems):
        x, y, c = _place()
        return pltpu.make_async_remote_copy(
            src_ref=ins[0].at[2], dst_ref=outs[0].at[2], send_sem=sems[0].at[0], recv_sem=sems[1].at[0],
            device_id=(1 - x, 1 - y, c), device_id_type=MESH)

    def start(ins, outs, sems):
        copy(ins, outs, sems).start()

    def finish(ins, outs, sems):
        copy(ins, outs, sems).wait()

    sems = [pltpu.SemaphoreType.DMA((1,)), pltpu.SemaphoreType.DMA((1,))]
    return _Comm([sums, from_chip], [_sds(from_chip.shape, from_chip.dtype)], sems, start, finish, aliases={1: 0})


def _pre_bwd_o(dh, x1, dx2, y0, g_pre, g_post, *, tm, name, comm=None):
    S, D = x1.shape

    def body(dh_ref, x1_ref, dx2_ref, y0_ref, gpre_ref, gpost_ref, dx1_ref, dy0_ref, dgpre_ref, dgpost_ref):
        @pl.when(pl.program_id(0) == 0)
        def _():
            dgpre_ref[...] = jnp.zeros_like(dgpre_ref)
            dgpost_ref[...] = jnp.zeros_like(dgpost_ref)

        dh = dh_ref[...].astype(F32)
        x1 = x1_ref[...]
        r2 = _rms(x1)
        xn = x1 * r2
        dgpre_ref[...] += _colsum(dh * xn)
        dx1 = dx2_ref[...] + _norm_bwd(dh * gpre_ref[...], xn, r2)
        dx1_ref[...] = dx1
        y = y0_ref[...].astype(F32)
        r1 = _rms(y)
        n1 = y * r1
        dgpost_ref[...] += _colsum(dx1 * n1)
        dy0_ref[...] = _norm_bwd(dx1 * gpost_ref[...], n1, r1).astype(BF16)

    row = pl.BlockSpec((tm, D), lambda i: (i, 0))
    return _call(
        body, grid=(S // tm,),
        in_specs=[row, row, row, row, _const((1, D)), _const((1, D))],
        out_specs=[row, row, _const((1, D)), _const((1, D))],
        out_shape=[_sds((S, D), F32), _sds((S, D), BF16), _sds((1, D), F32), _sds((1, D), F32)],
        operands=(dh, x1, dx2, y0, g_pre, g_post),
        name=name, params=_params(("arbitrary",), 48), comm=comm)


def _pre_bwd_e(dh, x, dx1, g_pre, *, tm, name):
    S, D = x.shape

    def body(dh_ref, x_ref, dx1_ref, gpre_ref, gx_ref, dgpre_ref):
        @pl.when(pl.program_id(0) == 0)
        def _():
            dgpre_ref[...] = jnp.zeros_like(dgpre_ref)

        dh = dh_ref[...].astype(F32)
        xx = x_ref[...]
        r0 = _rms(xx)
        xn = xx * r0
        dgpre_ref[...] += _colsum(dh * xn)
        gx_ref[...] = dx1_ref[...] + _norm_bwd(dh * gpre_ref[...], xn, r0)

    row = pl.BlockSpec((tm, D), lambda i: (i, 0))
    return _call(
        body, grid=(S // tm,),
        in_specs=[row, row, row, _const((1, D))],
        out_specs=[row, _const((1, D))],
        out_shape=[_sds((S, D), F32), _sds((1, D), F32)],
        operands=(dh, x, dx1, g_pre), name=name, params=_params(("arbitrary",), 56))[0]


SUBLANES = 8


def _shift_copies(sh_ref, ext_ref, cs):
    for b in range(1, SUBLANES):
        sh_ref[b - 1] = ext_ref[pl.ds(b, sh_ref.shape[1]), cs]


def _rows_at(ext_ref, sh_ref, off, cs, tm):
    b = off % SUBLANES
    if b == 0 or sh_ref is None:
        return ext_ref[pl.ds(off, tm), cs]
    return sh_ref[b - 1, pl.ds(off - b, tm), :]


def _taps(ext_ref, w_ref, n_taps, base, cs, tm, sh_ref=None):
    acc = _rows_at(ext_ref, sh_ref, base, cs, tm) * w_ref[0:1, cs]
    for k in range(1, n_taps):
        acc = acc + _rows_at(ext_ref, sh_ref, base + k, cs, tm) * w_ref[k:k + 1, cs]
    return acc


def _taps_rev(ext_ref, w_ref, n_taps, cs, tm, sh_ref=None):
    acc = _rows_at(ext_ref, sh_ref, n_taps - 1, cs, tm) * w_ref[0:1, cs]
    for k in range(1, n_taps):
        acc = acc + _rows_at(ext_ref, sh_ref, n_taps - 1 - k, cs, tm) * w_ref[k:k + 1, cs]
    return acc


def _e_mix_fwd(p, wa, wb, bias, ln_g, ln_b, *, tm, name, comm=None):
    S = p.shape[0]
    W = p.shape[1] // 7
    nb = tm // HALO
    chunks = [slice(c * LANES, (c + 1) * LANES) for c in range(W // LANES)]

    def body(p_ref, hax_ref, hac_ref, hbv_ref, hbg_ref, wa_ref, wb_ref, bias_ref, lg_ref, lb_ref,
             u_ref, cb_ref, ext_ref, sh_ref):
        keep = (pl.program_id(0) > 0).astype(F32)
        col = lambda j, cs: p_ref[:, j * W + cs.start:j * W + cs.stop].astype(F32)

        ext_ref[0:HALO, :] = hax_ref[...].astype(F32) * hac_ref[...].astype(F32) * keep
        ext_ref[HALO:, :] = p_ref[:, 2 * W:3 * W].astype(F32) * p_ref[:, 0:W].astype(F32)
        for cs in chunks:
            conv = _taps(ext_ref, wa_ref, CONV_A, HALO - (CONV_A - 1), cs, tm)
            az = col(3, cs)
            u_ref[:, cs] = (col(1, cs) * conv * (az * _sig(az))).astype(BF16)

        ext_ref[0:HALO, :] = hbv_ref[...].astype(F32) * _sig(hbg_ref[...].astype(F32)) * keep
        ext_ref[HALO:, :] = p_ref[:, 4 * W:5 * W].astype(F32) * _sig(p_ref[:, 5 * W:6 * W].astype(F32))
        s1 = jnp.zeros((tm, LANES), F32)
        for cs in chunks:
            _shift_copies(sh_ref, ext_ref, cs)
            cb = _taps(ext_ref, wb_ref, CONV_B, HALO - (CONV_B - 1), cs, tm, sh_ref) + bias_ref[:, cs]
            cb_ref[:, cs] = cb
            s1 = s1 + cb
        mu = jnp.sum(s1, axis=-1, keepdims=True) * (1.0 / W)
        s2 = jnp.zeros((tm, LANES), F32)
        for cs in chunks:
            xc = cb_ref[:, cs] - mu
            s2 = s2 + xc * xc
        rs = lax.rsqrt(jnp.sum(s2, axis=-1, keepdims=True) * (1.0 / W) + EPS)
        for cs in chunks:
            lb = (cb_ref[:, cs] - mu) * rs * lg_ref[:, cs] + lb_ref[:, cs]
            bz = col(6, cs)
            u_ref[:, W + cs.start:W + cs.stop] = (lb * _sig(lb) * (bz * _sig(bz))).astype(BF16)

    prev = lambda j: pl.BlockSpec((HALO, W), lambda i: (jnp.maximum(i * nb - 1, 0), j))
    return _call(
        body, grid=(S // tm,),
        in_specs=[pl.BlockSpec((tm, 7 * W), lambda i: (i, 0)), prev(0), prev(2), prev(4), prev(5),
                  _const((CONV_A, W)), _const((CONV_B, W)), _const((1, W)), _const((1, W)), _const((1, W))],
        out_specs=[pl.BlockSpec((tm, 2 * W), lambda i: (i, 0)), pl.BlockSpec((tm, W), lambda i: (i, 0))],
        out_shape=[_sds((S, 2 * W), BF16), _sds((S, W), F32)],
        operands=(p, p, p, p, p, wa, wb, bias, ln_g, ln_b),
        scratch_shapes=[pltpu.VMEM((HALO + tm, W), F32),
                        pltpu.VMEM((SUBLANES - 1, HALO + tm - SUBLANES, LANES), F32)],
        name=name, params=_params(("arbitrary",), 48), comm=comm)


def _e_mix_bwd(du, p, cb, wa, wb, ln_g, ln_b, *, tm, name, comm=None):
    S = p.shape[0]
    W = p.shape[1] // 7
    nb = tm // HALO
    n_t = S // tm
    last_blk = S // HALO - 1
    chunks = [slice(c * LANES, (c + 1) * LANES) for c in range(W // LANES)]

    def body(du_ref, duf_ref, p_ref, fab_ref, faz_ref, fbz_ref, hax_ref, hac_ref, hbv_ref, hbg_ref,
             cb_ref, cbf_ref, wa_ref, wb_ref, lg_ref, lb_ref,
             dp_ref, dwa_ref, dwb_ref, dbias_ref, dlg_ref, dlb_ref, extd_ref, extg_ref, shd_ref, shg_ref):
        i = pl.program_id(0)
        keep_prev = (i > 0).astype(F32)
        keep_next = (i < n_t - 1).astype(F32)
        col = lambda j, cs: p_ref[:, j * W + cs.start:j * W + cs.stop].astype(F32)

        @pl.when(i == 0)
        def _():
            dwa_ref[...] = jnp.zeros_like(dwa_ref)
            dwb_ref[...] = jnp.zeros_like(dwb_ref)
            dbias_ref[...] = jnp.zeros_like(dbias_ref)
            dlg_ref[...] = jnp.zeros_like(dlg_ref)
            dlb_ref[...] = jnp.zeros_like(dlb_ref)

        def dcb_rows(rows, cb_rows_ref, dub, bz_of, dst0, scale, main):
            cbv = cb_rows_ref[...]
            mu = jnp.mean(cbv, axis=-1, keepdims=True)
            xc = cbv - mu
            rs = lax.rsqrt(jnp.mean(xc * xc, axis=-1, keepdims=True) + EPS)
            m1 = jnp.zeros((rows, LANES), F32)
            m2 = jnp.zeros((rows, LANES), F32)
            for cs in chunks:
                nbv = (cb_rows_ref[:, cs] - mu) * rs
                lb = nbv * lg_ref[:, cs] + lb_ref[:, cs]
                sl = _sig(lb)
                bz = bz_of(cs)
                sz = _sig(bz)
                dub_c = dub(cs)
                dlb = dub_c * (bz * sz) * _dsilu(lb, sl)
                if main:
                    dlg_ref[:, cs] += _colsum(dlb * nbv)
                    dlb_ref[:, cs] += _colsum(dlb)
                    dp_ref[:, 6 * W + cs.start:6 * W + cs.stop] = (dub_c * (lb * sl) * _dsilu(bz, sz)).astype(BF16)
                dnb = dlb * lg_ref[:, cs]
                extd_ref[dst0:dst0 + rows, cs] = dnb
                m1 = m1 + dnb
                m2 = m2 + dnb * nbv
            m1 = jnp.sum(m1, axis=-1, keepdims=True) * (1.0 / W)
            m2 = jnp.sum(m2, axis=-1, keepdims=True) * (1.0 / W)
            for cs in chunks:
                nbv = (cb_rows_ref[:, cs] - mu) * rs
                dcb = rs * (extd_ref[dst0:dst0 + rows, cs] - m1 - nbv * m2) * scale
                extd_ref[dst0:dst0 + rows, cs] = dcb
                if main:
                    dbias_ref[:, cs] += _colsum(dcb)

        dcb_rows(tm, cb_ref, lambda cs: du_ref[:, W + cs.start:W + cs.stop].astype(F32),
                 lambda cs: col(6, cs), 0, 1.0, True)
        dcb_rows(HALO, cbf_ref, lambda cs: duf_ref[:, W + cs.start:W + cs.stop].astype(F32),
                 lambda cs: fbz_ref[:, cs].astype(F32), tm, keep_next, False)

        extg_ref[0:HALO, :] = hbv_ref[...].astype(F32) * _sig(hbg_ref[...].astype(F32)) * keep_prev
        extg_ref[HALO:, :] = p_ref[:, 4 * W:5 * W].astype(F32) * _sig(p_ref[:, 5 * W:6 * W].astype(F32))
        base_b = HALO - (CONV_B - 1)
        for cs in chunks:
            _shift_copies(shd_ref, extd_ref, cs)
            _shift_copies(shg_ref, extg_ref, cs)
            dgb = _taps_rev(extd_ref, wb_ref, CONV_B, cs, tm, shd_ref)
            bv = col(4, cs)
            sg = _sig(col(5, cs))
            dp_ref[:, 4 * W + cs.start:4 * W + cs.stop] = (dgb * sg).astype(BF16)
            dp_ref[:, 5 * W + cs.start:5 * W + cs.stop] = (dgb * bv * sg * (1.0 - sg)).astype(BF16)
            dcb = extd_ref[0:tm, cs]
            for k in range(CONV_B):
                prod = (dcb * _rows_at(extg_ref, shg_ref, base_b + k, cs, tm)).astype(BF16)
                dwb_ref[k:k + 1, cs] += jnp.dot(jnp.ones((SUBLANES, tm), BF16), prod,
                                                preferred_element_type=F32)[0:1]

        extg_ref[0:HALO, :] = hax_ref[...].astype(F32) * hac_ref[...].astype(F32) * keep_prev
        extg_ref[HALO:, :] = p_ref[:, 2 * W:3 * W].astype(F32) * p_ref[:, 0:W].astype(F32)
        base_a = HALO - (CONV_A - 1)
        for cs in chunks:
            conv = _taps(extg_ref, wa_ref, CONV_A, base_a, cs, tm)
            az = col(3, cs)
            sz = _sig(az)
            ab = col(1, cs)
            dua = du_ref[:, cs].astype(F32)
            dya = dua * (az * sz)
            dp_ref[:, W + cs.start:W + cs.stop] = (dya * conv).astype(BF16)
            dp_ref[:, 3 * W + cs.start:3 * W + cs.stop] = (dua * (ab * conv) * _dsilu(az, sz)).astype(BF16)
            extd_ref[0:tm, cs] = dya * ab
            azf = faz_ref[:, cs].astype(F32)
            extd_ref[tm:tm + HALO, cs] = (duf_ref[:, cs].astype(F32) * (azf * _sig(azf))
                                          * fab_ref[:, cs].astype(F32) * keep_next)
        for cs in chunks:
            dca = _taps_rev(extd_ref, wa_ref, CONV_A, cs, tm)
            dp_ref[:, cs] = (dca * col(2, cs)).astype(BF16)
            dp_ref[:, 2 * W + cs.start:2 * W + cs.stop] = (dca * col(0, cs)).astype(BF16)
            dconv = extd_ref[0:tm, cs]
            for k in range(CONV_A):
                dwa_ref[k:k + 1, cs] += _colsum(dconv * extg_ref[pl.ds(base_a + k, tm), cs])

    prev = lambda j: pl.BlockSpec((HALO, W), lambda i: (jnp.maximum(i * nb - 1, 0), j))
    nxt = lambda j, w: pl.BlockSpec((HALO, w), lambda i: (jnp.minimum((i + 1) * nb, last_blk), j))
    row = lambda w: pl.BlockSpec((tm, w), lambda i: (i, 0))
    return _call(
        body, grid=(n_t,),
        in_specs=[row(2 * W), nxt(0, 2 * W), row(7 * W), nxt(1, W), nxt(3, W), nxt(6, W),
                  prev(0), prev(2), prev(4), prev(5), row(W), nxt(0, W),
                  _const((CONV_A, W)), _const((CONV_B, W)), _const((1, W)), _const((1, W))],
        out_specs=[row(7 * W), _const((CONV_A, W)), _const((CONV_B, W)), _const((1, W)), _const((1, W)), _const((1, W))],
        out_shape=[_sds((S, 7 * W), BF16), _sds((CONV_A, W), F32), _sds((CONV_B, W), F32),
                   _sds((1, W), F32), _sds((1, W), F32), _sds((1, W), F32)],
        operands=(du, du, p, p, p, p, p, p, p, p, cb, cb, wa, wb, ln_g, ln_b),
        scratch_shapes=[pltpu.VMEM((tm + HALO, W), F32), pltpu.VMEM((HALO + tm, W), F32),
                        pltpu.VMEM((SUBLANES - 1, HALO + tm - SUBLANES, LANES), F32),
                        pltpu.VMEM((SUBLANES - 1, HALO + tm - SUBLANES, LANES), F32)],
        name=name, params=_params(("arbitrary",), 52), comm=comm)


def _counts(i, tm, rows, off, win):
    t = i * tm + off + lax.broadcasted_iota(jnp.int32, (rows, 1), 0)
    return jnp.minimum(t + 1, win).astype(F32)


def _o_mix_fwd(q, cw, cb, cscale, *, tm, name):
    S = q.shape[0]
    WC = q.shape[1] // 2
    NG = len(POOL_WINDOWS)
    G = WC // NG
    nb = tm // PHALO

    def body(v_ref, z_ref, hv_ref, cw_ref, cb_ref, sc_ref, yy_ref, pooled_ref, gg_ref, ext_ref):
        i = pl.program_id(0)
        keep = (i > 0).astype(F32)
        for g, win in enumerate(POOL_WINDOWS):
            cs = slice(g * G, (g + 1) * G)
            v = v_ref[:, cs].astype(F32)
            ext_ref[0:PHALO, :] = hv_ref[:, cs].astype(F32) * keep
            ext_ref[PHALO:, :] = v
            s = v
            for j in range(1, win):
                s = s + ext_ref[pl.ds(PHALO - j, tm), :]
            pooled = (s / _counts(i, tm, tm, 0, win) - v).astype(BF16)
            pooled_ref[:, cs] = pooled
            gg = jnp.dot(pooled, cw_ref[g], preferred_element_type=F32) + cb_ref[:, cs]
            gg_ref[:, cs] = gg.astype(BF16)
            z = z_ref[:, cs].astype(F32)
            yy_ref[:, cs] = (gg * sc_ref[:, cs] * (z * _sig(z))).astype(BF16)

    row = lambda j: pl.BlockSpec((tm, WC), lambda i: (i, j))
    out = pl.BlockSpec((tm, WC), lambda i: (i, 0))
    return _call(
        body, grid=(S // tm,),
        in_specs=[row(0), row(1), pl.BlockSpec((PHALO, WC), lambda i: (jnp.maximum(i * nb - 1, 0), 0)),
                  _const((NG, G, G)), _const((1, WC)), _const((1, WC))],
        out_specs=[out, out, out],
        out_shape=[_sds((S, WC), BF16)] * 3, operands=(q, q, q, cw, cb, cscale),
        scratch_shapes=[pltpu.VMEM((PHALO + tm, G), F32)],
        name=name, params=_params(("arbitrary",), 40))[0]


def _o_mix_bwd(dyy, q, gg, pooled, cw, cscale, *, tm, name):
    S = q.shape[0]
    WC = q.shape[1] // 2
    NG = len(POOL_WINDOWS)
    G = WC // NG
    nb = tm // PHALO
    n_t = S // tm
    last_blk = S // PHALO - 1
    nt = (((1,), (1,)), ((), ()))
    tn = (((0,), (0,)), ((), ()))

    def body(dyy_ref, dyyf_ref, z_ref, zf_ref, gg_ref, pooled_ref, cw_ref, sc_ref,
             dq_ref, dcw_ref, dcb_ref, dsc_ref, ext_ref):
        i = pl.program_id(0)
        keep_next = (i < n_t - 1).astype(F32)

        @pl.when(i == 0)
        def _():
            dcw_ref[...] = jnp.zeros_like(dcw_ref)
            dcb_ref[...] = jnp.zeros_like(dcb_ref)
            dsc_ref[...] = jnp.zeros_like(dsc_ref)

        for g, win in enumerate(POOL_WINDOWS):
            cs = slice(g * G, (g + 1) * G)
            sc = sc_ref[:, cs]
            z = z_ref[:, cs].astype(F32)
            sz = _sig(z)
            dyy_c = dyy_ref[:, cs].astype(F32)
            ggv = gg_ref[:, cs].astype(F32)
            dyy0 = dyy_c * (z * sz)
            dq_ref[:, WC + cs.start:WC + cs.stop] = (dyy_c * (ggv * sc) * _dsilu(z, sz)).astype(BF16)
            dgg = dyy0 * sc
            dsc_ref[:, cs] += _colsum(dyy0 * ggv)
            dcb_ref[:, cs] += _colsum(dgg)
            dgg_b = dgg.astype(BF16)
            dcw_ref[g] += lax.dot_general(pooled_ref[:, cs], dgg_b, tn, preferred_element_type=F32)
            dpool = lax.dot_general(dgg_b, cw_ref[g], nt, preferred_element_type=F32)
            zf = zf_ref[:, cs].astype(F32)
            dgg_f = (dyyf_ref[:, cs].astype(F32) * (zf * _sig(zf)) * sc * keep_next).astype(BF16)
            dpool_f = lax.dot_general(dgg_f, cw_ref[g], nt, preferred_element_type=F32)
            ext_ref[0:tm, :] = dpool / _counts(i, tm, tm, 0, win)
            ext_ref[tm:tm + PHALO, :] = dpool_f / _counts(i, tm, PHALO, tm, win)
            dv = ext_ref[0:tm, :] - dpool
            for j in range(1, win):
                dv = dv + ext_ref[pl.ds(j, tm), :]
            dq_ref[:, cs] = dv.astype(BF16)

    row = lambda: pl.BlockSpec((tm, WC), lambda i: (i, 0))
    nxt = lambda j: pl.BlockSpec((PHALO, WC), lambda i: (jnp.minimum((i + 1) * nb, last_blk), j))
    return _call(
        body, grid=(n_t,),
        in_specs=[row(), nxt(0), pl.BlockSpec((tm, WC), lambda i: (i, 1)), nxt(1), row(), row(),
                  _const((NG, G, G)), _const((1, WC))],
        out_specs=[pl.BlockSpec((tm, 2 * WC), lambda i: (i, 0)), _const((NG, G, G)), _const((1, WC)), _const((1, WC))],
        out_shape=[_sds((S, 2 * WC), BF16), _sds((NG, G, G), F32), _sds((1, WC), F32), _sds((1, WC), F32)],
        operands=(dyy, dyy, q, q, gg, pooled, cw, cscale),
        scratch_shapes=[pltpu.VMEM((tm + PHALO, G), F32)],
        name=name, params=_params(("arbitrary",), 48))[0]


def _place():
    return lax.axis_index("x"), lax.axis_index("y"), lax.axis_index("c")


def _piece(ref, axis, size, index):
    start = index * size
    if axis == len(ref.shape) - 1:
        start = pl.multiple_of(start, LANES)
    idx = [slice(None)] * len(ref.shape)
    idx[axis] = pl.ds(start, size)
    return ref.at[tuple(idx)]


def _gather_copies(src, out, axis, size, send_sems, recv_sems, base, held=None):
    x, y, c = _place()
    sib, xn, yn = (x, y, 1 - c), (1 - x, y, c), (x, 1 - y, c)

    def blk(px, py, of=out):
        return _piece(of, axis, size, 4 * px + 2 * py + c)

    def half(ref, h):
        n = ref.shape[0] // 2
        return ref.at[pl.ds(h * n, n)]

    def rc(k, s, d, to):
        return pltpu.make_async_remote_copy(src_ref=s, dst_ref=d, send_sem=send_sems.at[base + k],
                                            recv_sem=recv_sems.at[base + k], device_id=to, device_id_type=MESH)

    own, xb, yb, db = blk(x, y), blk(1 - x, y), blk(x, 1 - y), blk(1 - x, 1 - y)
    got = out if held is None else held
    xs, ys, ds = blk(1 - x, y, got), blk(x, 1 - y, got), blk(1 - x, 1 - y, got)
    return [rc(0, src, own, sib), rc(1, src, own, xn), rc(2, src, own, yn),
            rc(3, half(xs, 0), half(xb, 0), yn), rc(4, half(ys, 1), half(yb, 1), xn),
            rc(5, xs, xb, sib), rc(6, ys, yb, sib), rc(7, ds, db, sib)]


N_GATHER = 8


def _gather_comm(shards, axes, phases):
    n = len(shards)
    if phases == "second":
        sizes = [s.shape[a] // N_DEV for s, a in zip(shards, axes)]
        full = [_sds(s.shape, s.dtype) for s in shards]
    else:
        sizes = [s.shape[a] for s, a in zip(shards, axes)]
        full = [_sds(s.shape[:a] + (N_DEV * s.shape[a],) + s.shape[a + 1:], s.dtype) for s, a in zip(shards, axes)]

    def plan(ins, outs, sems):
        x, y, c = _place()
        me = 4 * x + 2 * y + c
        if phases == "second":
            cps = [_gather_copies(_piece(ins[t], axes[t], sizes[t], me), outs[t], axes[t], sizes[t], sems[0], sems[1],
                                  N_GATHER * t, ins[t]) for t in range(n)]
        else:
            cps = [_gather_copies(sems[3 + t], outs[t], axes[t], sizes[t], sems[0], sems[1], N_GATHER * t)
                   for t in range(n)]
        mine = [pltpu.make_async_copy(sems[3 + t], _piece(outs[t], axes[t], sizes[t], me), sems[2].at[t])
                for t in range(n)] if phases != "second" else []
        return cps, mine

    def send_own(ins, outs, sems):
        cps, mine = plan(ins, outs, sems)
        for t in range(n):
            stage = pltpu.make_async_copy(ins[t], sems[3 + t], sems[2].at[t])
            stage.start()
            stage.wait()
            mine[t].start()
            for k in (0, 1, 2):
                cps[t][k].start()

    def pass_on(ins, outs, sems):
        cps, _ = plan(ins, outs, sems)
        for t in range(n):
            if phases == "all":
                cps[t][1].wait_recv()
            cps[t][3].start()
            cps[t][5].start()
        for t in range(n):
            if phases == "all":
                cps[t][2].wait_recv()
            cps[t][4].start()
            cps[t][6].start()

    def own_landed(ins, outs, sems):
        cps, mine = plan(ins, outs, sems)
        for t in range(n):
            for k in (0, 1, 2):
                cps[t][k].wait()
            mine[t].wait()

    def all_landed(ins, outs, sems):
        cps, mine = plan(ins, outs, sems)
        for t in range(n):
            cps[t][3].wait_recv()
            cps[t][4].wait_recv()
            cps[t][7].start()
        for t in range(n):
            for k in ((0, 5, 6, 7) if phases == "all" else (5, 6, 7)):
                cps[t][k].wait_recv()
            for k in (range(N_GATHER) if phases == "all" else range(3, N_GATHER)):
                cps[t][k].wait_send()
            if phases == "all":
                mine[t].wait()

    sems = [pltpu.SemaphoreType.DMA((N_GATHER * n,)), pltpu.SemaphoreType.DMA((N_GATHER * n,))]
    if phases != "second":
        sems.append(pltpu.SemaphoreType.DMA((n,)))
        sems += [pltpu.VMEM(s.shape, s.dtype) for s in shards]
    if phases == "all":
        return _Comm(shards, full, sems, send_own, all_landed, middle=pass_on)
    if phases == "first":
        return _Comm(shards, full, sems, send_own, own_landed)
    return _Comm(shards, full, sems, pass_on, all_landed, aliases={t: t for t in range(n)})


def _pair_comm(grads, axes, sizes):
    n = len(grads)
    outs_sds = [_sds((4,) + g.shape[:a] + (s,) + g.shape[a + 1:], g.dtype) for g, a, s in zip(grads, axes, sizes)]

    def copies(ins, outs, sems):
        send_sems, recv_sems = sems
        x, y, c = _place()
        return [pltpu.make_async_remote_copy(
            src_ref=_piece(ins[t], axes[t], sizes[t], 2 * qi + (1 - c)), dst_ref=outs[t].at[qi],
            send_sem=send_sems.at[4 * t + qi], recv_sem=recv_sems.at[4 * t + qi],
            device_id=(x, y, 1 - c), device_id_type=MESH) for t in range(n) for qi in range(4)]

    def start(ins, outs, sems):
        for cp in copies(ins, outs, sems):
            cp.start()

    def finish(ins, outs, sems):
        for cp in copies(ins, outs, sems):
            cp.wait()

    sems = [pltpu.SemaphoreType.DMA((4 * n,)), pltpu.SemaphoreType.DMA((4 * n,))]
    return _Comm(grads, outs_sds, sems, start, finish)


def _chip_comm(sums):
    n = len(sums)
    outs_sds = [_sds((3,) + s.shape[1:], s.dtype) for s in sums]

    def copies(ins, outs, sems):
        send_sems, recv_sems = sems
        x, y, c = _place()
        return [pltpu.make_async_remote_copy(
            src_ref=ins[t].at[2 * qx + qy], dst_ref=outs[t].at[j],
            send_sem=send_sems.at[3 * t + j], recv_sem=recv_sems.at[3 * t + j],
            device_id=(qx, qy, c), device_id_type=MESH)
            for t in range(n) for j, (qx, qy) in enumerate([(1 - x, y), (x, 1 - y), (1 - x, 1 - y)])]

    def start(ins, outs, sems):
        for cp in copies(ins, outs, sems):
            cp.start()

    def finish(ins, outs, sems):
        for cp in copies(ins, outs, sems):
            cp.wait()

    sems = [pltpu.SemaphoreType.DMA((3 * n,)), pltpu.SemaphoreType.DMA((3 * n,))]
    return _Comm(sums, outs_sds, sems, start, finish)


def _small_comm(small):
    def copies(ins, outs, sems):
        send_sems, recv_sems, local_sem = sems
        x, y, c = _place()
        mine = outs[0].at[4 * x + 2 * y + c]
        out = [pltpu.make_async_copy(ins[0], mine, local_sem.at[0])]
        for k in range(1, N_DEV):
            peer = (1 - x if k & 4 else x, 1 - y if k & 2 else y, 1 - c if k & 1 else c)
            out.append(pltpu.make_async_remote_copy(
                src_ref=ins[0], dst_ref=mine, send_sem=send_sems.at[k - 1], recv_sem=recv_sems.at[k - 1],
                device_id=peer, device_id_type=MESH))
        return out

    def start(ins, outs, sems):
        for cp in copies(ins, outs, sems):
            cp.start()

    def finish(ins, outs, sems):
        for cp in copies(ins, outs, sems):
            cp.wait()

    sems = [pltpu.SemaphoreType.DMA((N_DEV - 1,)), pltpu.SemaphoreType.DMA((N_DEV - 1,)), pltpu.SemaphoreType.DMA((1,))]
    return _Comm([small], [_sds((N_DEV,) + small.shape, small.dtype)], sems, start, finish)


def _small_scatter_comm(send):
    def copies(ins, outs, sems):
        send_sems, recv_sems, local_sem = sems
        x, y, c = _place()
        me = 4 * x + 2 * y + c
        out = [pltpu.make_async_copy(ins[0].at[me], outs[0].at[me], local_sem.at[0])]
        for k in range(1, N_DEV):
            px, py, pc = (1 - x if k & 4 else x, 1 - y if k & 2 else y, 1 - c if k & 1 else c)
            out.append(pltpu.make_async_remote_copy(
                src_ref=ins[0].at[4 * px + 2 * py + pc], dst_ref=outs[0].at[me], send_sem=send_sems.at[k - 1],
                recv_sem=recv_sems.at[k - 1], device_id=(px, py, pc), device_id_type=MESH))
        return out

    def start(ins, outs, sems):
        for cp in copies(ins, outs, sems):
            cp.start()

    def finish(ins, outs, sems):
        for cp in copies(ins, outs, sems):
            cp.wait()

    sems = [pltpu.SemaphoreType.DMA((N_DEV - 1,)), pltpu.SemaphoreType.DMA((N_DEV - 1,)), pltpu.SemaphoreType.DMA((1,))]
    return _Comm([send], [_sds(send.shape, send.dtype)], sems, start, finish)


def _pair_sum(c_idx, grad, recv, axis, size, split, *, name):
    nd = len(grad.shape)
    piece = grad.shape[:axis] + (size,) + grad.shape[axis + 1:]
    blk = (piece[0] // split,) + piece[1:]

    def g_map(q, r, c_ref):
        idx = [0] * nd
        idx[axis] = 2 * q + c_ref[0]
        idx[0] = idx[0] * split + r if axis == 0 else r
        return tuple(idx)

    def r_map(q, r, c_ref):
        return (q, r) + (0,) * (nd - 1)

    def body(c_ref, g_ref, r_ref, o_ref):
        o_ref[0] = (g_ref[...].astype(F32) + r_ref[0].astype(F32)).astype(BF16)

    return _call(
        body, grid=(4, split), prefetch=c_idx,
        in_specs=[pl.BlockSpec(blk, g_map), pl.BlockSpec((1,) + blk, r_map)],
        out_specs=[pl.BlockSpec((1,) + blk, r_map)], out_shape=[_sds((4,) + piece, BF16)],
        operands=(grad, recv), name=name, params=_params(("arbitrary", "arbitrary"), 32))[0][0]


def _adam_math(w, g, m, v):
    m = ADAM_B1 * m + (1.0 - ADAM_B1) * g
    v = ADAM_B2 * v + (1.0 - ADAM_B2) * (g * g)
    m_hat = m / (1.0 - ADAM_B1 ** ADAM_STEP)
    v_hat = v / (1.0 - ADAM_B2 ** ADAM_STEP)
    delta = -ADAM_LR * (m_hat / (jnp.sqrt(v_hat) + ADAM_EPS) + ADAM_WD * w)
    return delta, m, v


def _adam_big(q_idx, sums, recv, w, m, v, split, *, name, comm=None):
    shape = w.shape
    nd = len(shape)
    blk = (shape[0] // split,) + shape[1:]
    w_map = lambda r, q_ref: (r,) + (0,) * (nd - 1)
    s_map = lambda r, q_ref: (q_ref[0], r) + (0,) * (nd - 1)
    r_map = lambda r, q_ref: (0, r) + (0,) * (nd - 1)

    def body(q_ref, s_ref, r_ref, w_ref, m_ref, v_ref, g_ref, d_ref, nm_ref, nv_ref):
        g = s_ref[0].astype(F32) + r_ref[0].astype(F32) + r_ref[1].astype(F32) + r_ref[2].astype(F32)
        g_ref[...] = g
        d_ref[...], nm_ref[...], nv_ref[...] = _adam_math(w_ref[...], g, m_ref[...], v_ref[...])

    wspec = pl.BlockSpec(blk, w_map)
    return _call(
        body, grid=(split,), prefetch=q_idx,
        in_specs=[pl.BlockSpec((1,) + blk, s_map), pl.BlockSpec((3,) + blk, r_map), wspec, wspec, wspec],
        out_specs=[wspec] * 4, out_shape=[_sds(shape, F32)] * 4, operands=(sums, recv, w, m, v),
        name=name, params=_params(("arbitrary",), 32), comm=comm)


def _adam_small(parts, w, m, v, *, name):
    R = w.shape[0]

    def body(p_ref, w_ref, m_ref, v_ref, g_ref, d_ref, nm_ref, nv_ref):
        g = p_ref[0]
        for d in range(1, N_DEV):
            g = g + p_ref[d]
        g_ref[...] = g
        d_ref[...], nm_ref[...], nv_ref[...] = _adam_math(w_ref[...], g, m_ref[...], v_ref[...])

    whole = _const((R, LANES))
    return _call(
        body, grid=(1,), in_specs=[_const((N_DEV, R, LANES)), whole, whole, whole], out_specs=[whole] * 4,
        out_shape=[_sds((R, LANES), F32)] * 4, operands=(parts, w, m, v), name=name,
        params=_params(("arbitrary",), 32))[0]


def _pack(arrs):
    return jnp.concatenate([a.reshape(-1) for a in arrs]).reshape(-1, LANES)


def _unpack(packed, shapes):
    flat = packed.reshape(-1)
    out, off = [], 0
    for s in shapes:
        n = 1
        for d in s:
            n *= d
        out.append(flat[off:off + n].reshape(s))
        off += n
    return out


BIG = ("e_in", "e_out", "o_in", "o_cw", "o_out")
BIG_AXIS = dict(e_in=1, e_out=0, o_in=1, o_cw=1, o_out=0)
BIG_SPLIT = dict(e_in=8, e_out=4, o_in=4, o_cw=4, o_out=4)
REPLICATED = ("e_norm_pre", "e_norm_post", "e_b_conv_bias", "e_b_ln_g", "e_b_ln_b")
SHARDED = ("e_a_conv", "e_b_conv", "o_norm_pre", "o_norm_post", "o_c_b", "o_c_scale")
SMALL = REPLICATED + SHARDED


class _Exchange:
    def __init__(self, shards, small, order, c_idx):
        self.q_idx = order[:1]
        self.shards = shards
        self.small = small
        self.order = order
        self.c_idx = c_idx
        self.reduced = {}

    def gather(self, keys):
        return _gather_comm([self.shards[k] for k in keys], [BIG_AXIS[k] for k in keys], "all")

    def gather1(self, keys):
        return _gather_comm([self.shards[k] for k in keys], [BIG_AXIS[k] for k in keys], "first")

    def gather2(self, keys, firsts):
        return _gather_comm(firsts, [BIG_AXIS[k] for k in keys], "second")

    def pair(self, grads):
        keys = list(grads)
        return _pair_comm([grads[k] for k in keys], [BIG_AXIS[k] for k in keys],
                          [grads[k].shape[BIG_AXIS[k]] // N_DEV for k in keys])

    def pair_sums(self, grads, received):
        return {k: _pair_sum(self.c_idx, grads[k], r, BIG_AXIS[k], grads[k].shape[BIG_AXIS[k]] // N_DEV,
                             BIG_SPLIT[k], name="pair_sum_" + k) for k, r in zip(grads, received)}

    def chips(self, sums):
        return _chip_comm([sums[k] for k in sums])

    def done(self, sums, received):
        self.reduced.update({k: (sums[k], r, self.q_idx) for k, r in zip(sums, received)})


def _local_step(x, tgt, w_small, ex):
    S, D = x.shape
    tnt, tx, tw = min(TM_NT, S), min(TM_MIX, S), min(TM_WIDE, S)

    wt = {}
    p, h0, wt["e_in"], got = _gather_matmul(ex.order, x, w_small["e_norm_pre"], ex.shards["e_in"], tm=tw,
                                            name="e_in_fwd", comm=_small_comm(ex.small))
    per_dev = [_unpack(got[0][d], [w_small[k].shape for k in SHARDED]) for d in range(N_DEV)]
    sm = {k: w_small[k] for k in REPLICATED}
    for j, k in enumerate(SHARDED):
        sm[k] = jnp.concatenate([per_dev[d][j] for d in range(N_DEV)], axis=-1)
    n_groups = sm["o_c_b"].shape[0]
    sm["o_c_b"] = sm["o_c_b"].reshape(1, -1)

    W = p.shape[1] // 7
    (u, cb), got = _e_mix_fwd(p, sm["e_a_conv"], sm["e_b_conv"], sm["e_b_conv_bias"], sm["e_b_ln_g"],
                              sm["e_b_ln_b"], tm=tx, name="e_mix_fwd", comm=ex.gather(["e_out"]))
    wt["e_out"] = got[0]
    late = ["o_out", "o_cw"]
    (x1, y0), part = _out_norm_res(u, wt["e_out"], x, sm["e_norm_post"], tm=tw, name="e_out_fwd",
                                   comm=ex.gather1(late))
    q, h1, wt["o_in"], got = _gather_matmul(ex.order, x1, sm["o_norm_pre"], ex.shards["o_in"], tm=tw,
                                            name="o_in_fwd", comm=ex.gather2(late, part))
    wt.update(zip(late, got))
    yy, pooled, gg = _o_mix_fwd(q, wt["o_cw"], sm["o_c_b"], sm["o_c_scale"], tm=tw, name="o_mix_fwd")
    dout, dx2, dyy, lcol, dg_o_post = _out_loss(yy, wt["o_out"], x1, sm["o_norm_post"], tgt, tm=tx, name="o_out_loss")
    loss = (0.5 / D) * jnp.sum(lcol)

    dq, d_cw, d_cb, d_cscale = _o_mix_bwd(dyy, q, gg, pooled, wt["o_cw"], sm["o_c_scale"], tm=tw, name="o_mix_bwd")
    g_o_out, _ = _mm_tn(yy, dout, ts=tnt, tn=W, name="o_out_dw")
    ga = dict(o_out=g_o_out, o_cw=d_cw.astype(BF16))
    dh1, ra = _mm_nt(dq, wt["o_in"], tm=tnt, tk=W, name="o_in_bwd", comm=ex.pair(ga))
    sa = ex.pair_sums(ga, ra)
    (dx1, dy0, dg_o_pre, dg_e_post), ra = _pre_bwd_o(dh1, x1, dx2, y0, sm["o_norm_pre"], sm["e_norm_post"],
                                                     tm=tx, name="o_pre_bwd", comm=ex.chips(sa))
    ex.done(sa, ra)
    g_o_in, _ = _mm_tn(h1, dq, ts=tnt, tn=W, name="o_in_dw")
    gb = dict(o_in=g_o_in)
    du, rb = _mm_nt(dy0, wt["e_out"], tm=tnt, tk=W, name="e_out_bwd", comm=ex.pair(gb))
    sb = ex.pair_sums(gb, rb)
    g_e_out, _ = _mm_tn(u, dy0, ts=tnt, tn=W, name="e_out_dw")
    gc = dict(e_out=g_e_out)
    (dp, d_wa, d_wb, d_bias, d_lg, d_lb), rbc = _e_mix_bwd(
        du, p, cb, sm["e_a_conv"], sm["e_b_conv"], sm["e_b_ln_g"], sm["e_b_ln_b"], tm=tx, name="e_mix_bwd",
        comm=_merge(ex.chips(sb), ex.pair(gc)))
    ex.done(sb, rbc[:1])
    sc = ex.pair_sums(gc, rbc[1:])
    order_out = jnp.concatenate([ex.order[1:], ex.order[:1]])
    sd, from_chip, rc = _dw_reduce(order_out, h0, dp, ex.shards["e_in"].shape[1], ts=tnt, name="e_in_dw",
                                   comm=ex.chips(sc))
    ex.done(sc, rc)
    dh0, rd = _mm_nt(dp, wt["e_in"], tm=tnt, tk=W, name="e_in_bwd", comm=_diag_comm(sd, from_chip))
    ex.reduced["e_in"] = (sd, rd[0], jnp.full((1,), 3, jnp.int32))
    grad_x, dg_e_pre = _pre_bwd_e(dh0, x, dx1, sm["e_norm_pre"], tm=tw, name="e_pre_bwd")

    small = dict(e_norm_pre=dg_e_pre, e_norm_post=dg_e_post, e_a_conv=d_wa, e_b_conv=d_wb, e_b_conv_bias=d_bias,
                 e_b_ln_g=d_lg, e_b_ln_b=d_lb, o_norm_pre=dg_o_pre, o_norm_post=dg_o_post,
                 o_c_b=d_cb.reshape(n_groups, -1), o_c_scale=d_cscale)
    return loss, grad_x, small


def kernel(x, e_norm_pre, e_norm_post, e_w_in, e_a_conv, e_b_conv, e_b_conv_bias, e_b_ln_g, e_b_ln_b, e_w_out, o_norm_pre, o_norm_post, o_w_in, o_c_w, o_c_b, o_c_scale, o_w_out, loss_target, m_e_norm_pre, m_e_norm_post, m_e_w_in, m_e_a_conv, m_e_b_conv, m_e_b_conv_bias, m_e_b_ln_g, m_e_b_ln_b, m_e_w_out, m_o_norm_pre, m_o_norm_post, m_o_w_in, m_o_c_w, m_o_c_b, m_o_c_scale, m_o_w_out, v_e_norm_pre, v_e_norm_post, v_e_w_in, v_e_a_conv, v_e_b_conv, v_e_b_conv_bias, v_e_b_ln_g, v_e_b_ln_b, v_e_w_out, v_o_norm_pre, v_o_norm_post, v_o_w_in, v_o_c_w, v_o_c_b, v_o_c_scale, v_o_w_out):
    xi, yi, ci = _place()
    w_big = dict(e_in=e_w_in[0], e_out=e_w_out[0], o_in=o_w_in[0], o_cw=o_c_w[0], o_out=o_w_out[0])
    m_big = dict(e_in=m_e_w_in[0], e_out=m_e_w_out[0], o_in=m_o_w_in[0], o_cw=m_o_c_w[0], o_out=m_o_w_out[0])
    v_big = dict(e_in=v_e_w_in[0], e_out=v_e_w_out[0], o_in=v_o_w_in[0], o_cw=v_o_c_w[0], o_out=v_o_w_out[0])
    w_small = dict(e_norm_pre=e_norm_pre, e_norm_post=e_norm_post, e_b_conv_bias=e_b_conv_bias, e_b_ln_g=e_b_ln_g,
                   e_b_ln_b=e_b_ln_b, e_a_conv=e_a_conv[0], e_b_conv=e_b_conv[0], o_norm_pre=o_norm_pre,
                   o_norm_post=o_norm_post, o_c_b=o_c_b[0], o_c_scale=o_c_scale)
    m_small = dict(e_norm_pre=m_e_norm_pre, e_norm_post=m_e_norm_post, e_b_conv_bias=m_e_b_conv_bias,
                   e_b_ln_g=m_e_b_ln_g, e_b_ln_b=m_e_b_ln_b, e_a_conv=m_e_a_conv[0], e_b_conv=m_e_b_conv[0],
                   o_norm_pre=m_o_norm_pre, o_norm_post=m_o_norm_post, o_c_b=m_o_c_b[0], o_c_scale=m_o_c_scale)
    v_small = dict(e_norm_pre=v_e_norm_pre, e_norm_post=v_e_norm_post, e_b_conv_bias=v_e_b_conv_bias,
                   e_b_ln_g=v_e_b_ln_g, e_b_ln_b=v_e_b_ln_b, e_a_conv=v_e_a_conv[0], e_b_conv=v_e_b_conv[0],
                   o_norm_pre=v_o_norm_pre, o_norm_post=v_o_norm_post, o_c_b=v_o_c_b[0], o_c_scale=v_o_c_scale)

    c_idx = jnp.reshape(ci, (1,)).astype(jnp.int32)
    order = jnp.stack([2 * xi + yi, 2 * (1 - xi) + yi, 2 * xi + (1 - yi), 2 * (1 - xi) + (1 - yi)]).astype(jnp.int32)
    ex = _Exchange({k: w_big[k].astype(BF16) for k in BIG}, _pack([w_small[k] for k in SHARDED]), order, c_idx)
    loss, grad_x, g_small = _local_step(x[0], loss_target[0], w_small, ex)

    big_out = {}
    for k in BIG:
        sums, received, q_idx = ex.reduced[k]
        big_out[k] = _adam_big(q_idx, sums, received, w_big[k], m_big[k], v_big[k], BIG_SPLIT[k], name="adam_" + k)[0]

    rep = _pack([g_small[k] for k in REPLICATED])
    loss_row = jnp.pad(jnp.reshape(loss, (1, 1)), ((0, 0), (0, LANES - 1)))
    blocks = []
    for k in SHARDED:
        r, n = w_small[k].shape
        blocks.append(g_small[k].reshape(r, N_DEV, n).transpose(1, 0, 2).reshape(N_DEV, r * n))
    blocks = jnp.concatenate(blocks, axis=1).reshape(N_DEV, -1, LANES)
    head = jnp.concatenate([rep, loss_row], axis=0)
    send = jnp.concatenate([jnp.broadcast_to(head[None], (N_DEV,) + head.shape), blocks], axis=1)
    parts = _run_comm(_small_scatter_comm(send), "small_grad_exchange")[0]

    def own_rows(d):
        return jnp.concatenate([_pack([d[k] for k in REPLICATED]), jnp.ones((1, LANES), F32),
                                _pack([d[k] for k in SHARDED])], axis=0)

    res_small = _adam_small(parts, own_rows(w_small), own_rows(m_small), own_rows(v_small), name="adam_small")
    n_rep = rep.shape[0]
    loss = res_small[0][n_rep, 0]
    small_out = {k: [] for k in SMALL}
    for packed in res_small:
        for k, t in zip(REPLICATED, _unpack(packed[:n_rep], [w_small[k].shape for k in REPLICATED])):
            small_out[k].append(t)
        for k, t in zip(SHARDED, _unpack(packed[n_rep + 1:], [w_small[k].shape for k in SHARDED])):
            small_out[k].append(t)

    big_of = dict(e_w_in="e_in", e_w_out="e_out", o_w_in="o_in", o_c_w="o_cw", o_w_out="o_out")
    stacked = ("e_a_conv", "e_b_conv", "o_c_b")

    def leaf(name, which):
        if name in big_of:
            return big_out[big_of[name]][which][None]
        t = small_out[name][which]
        return t[None] if name in stacked else t

    order = ("e_norm_pre", "e_norm_post", "e_w_in", "e_a_conv", "e_b_conv", "e_b_conv_bias", "e_b_ln_g", "e_b_ln_b",
             "e_w_out", "o_norm_pre", "o_norm_post", "o_w_in", "o_c_w", "o_c_b", "o_c_scale", "o_w_out")
    outs = [loss, grad_x[None]]
    for which in range(4):
        outs += [leaf(nm, which) for nm in order]
    return tuple(outs)
```

```python
import jax
import jax.numpy as jnp
from jax import lax
from jax.experimental import pallas as pl
from jax.experimental.pallas import tpu as pltpu

F32 = jnp.float32
BF16 = jnp.bfloat16
EPS = 1e-6
MESH = pl.DeviceIdType.MESH
ANY = pl.BlockSpec(memory_space=pl.ANY)

N_DEV = 8
HALO = 32
PHALO = 16
CONV_A = 3
CONV_B = 31
POOL_WINDOWS = (2, 4, 8, 16)
LANES = 128
MIB = 1024 * 1024
LOCAL = 1

ADAM_LR = 0.001
ADAM_B1 = 0.9
ADAM_B2 = 0.999
ADAM_EPS = 1e-08
ADAM_WD = 0.01
ADAM_STEP = 10

TM_NT = 1024
TM_MIX = 256
TM_WIDE = 512


def _sds(shape, dtype):
    return jax.ShapeDtypeStruct(tuple(shape), dtype)


def _params(sem, vmem_mib):
    return pltpu.CompilerParams(dimension_semantics=sem, vmem_limit_bytes=vmem_mib * MIB)


def _const(shape, single=False):
    n = len(shape)
    if single:
        return pl.BlockSpec(shape, lambda *_: (0,) * n, pipeline_mode=pl.Buffered(1))
    return pl.BlockSpec(shape, lambda *_: (0,) * n)


def _sig(v):
    return jax.nn.sigmoid(v)


def _dsilu(v, s):
    return s * (1.0 + v * (1.0 - s))


def _rms(v):
    return lax.rsqrt(jnp.mean(v * v, axis=-1, keepdims=True) + EPS)


def _norm_bwd(dn, n, r):
    return r * (dn - n * jnp.mean(dn * n, axis=-1, keepdims=True))


def _colsum(v):
    return jnp.sum(v, axis=0, keepdims=True)


def _colsum_mxu(v):
    return jnp.dot(jnp.ones((8, v.shape[0]), BF16), v.astype(BF16), preferred_element_type=F32)[0:1]


class _Comm:
    def __init__(self, inputs, out_shapes, sems, start, finish, aliases=None, middle=None):
        self.inputs, self.out_shapes, self.sems = list(inputs), list(out_shapes), list(sems)
        self.start, self.finish, self.middle = start, finish, middle
        self.aliases = dict(aliases or {})


def _merge(*comms):
    comms = [c for c in comms if c is not None]
    if len(comms) <= 1:
        return comms[0] if comms else None
    spans, i0, o0, s0, aliases = [], 0, 0, 0, {}
    for c in comms:
        spans.append((i0, o0, s0))
        aliases.update({i0 + k: o0 + v for k, v in c.aliases.items()})
        i0, o0, s0 = i0 + len(c.inputs), o0 + len(c.out_shapes), s0 + len(c.sems)

    def run(which):
        def fn(ins, outs, sems):
            for c, (i, o, s) in zip(comms, spans):
                hook = getattr(c, which)
                if hook is not None:
                    hook(ins[i:i + len(c.inputs)], outs[o:o + len(c.out_shapes)], sems[s:s + len(c.sems)])
        return fn

    return _Comm([a for c in comms for a in c.inputs], [a for c in comms for a in c.out_shapes],
                 [a for c in comms for a in c.sems], run("start"), run("finish"), aliases,
                 run("middle") if any(c.middle is not None for c in comms) else None)


def _call(body, *, grid, in_specs, out_specs, out_shape, operands, name, params, scratch_shapes=(), comm=None,
          prefetch=None, own_copies_first=False):
    n_p = 0 if prefetch is None else 1
    n_i, n_o, n_s = len(in_specs), len(out_specs), len(scratch_shapes)
    if comm is None:
        comm = _Comm([], [], [], None, None)
    c_i, c_o = len(comm.inputs), len(comm.out_shapes)

    def carrier(*refs):
        pre, refs = refs[:n_p], refs[n_p:]
        ins, cins = refs[:n_i], refs[n_i:n_i + c_i]
        outs = refs[n_i + c_i:n_i + c_i + n_o]
        couts = refs[n_i + c_i + n_o:n_i + c_i + n_o + c_o]
        scr = refs[n_i + c_i + n_o + c_o:n_i + c_i + n_o + c_o + n_s]
        csems = refs[n_i + c_i + n_o + c_o + n_s:]
        ids = [pl.program_id(d) for d in range(len(grid))]
        first = ids[0] == 0
        half = ids[0] == grid[0] // 2
        last = ids[0] == grid[0] - 1
        for d in range(1, len(grid)):
            first = first & (ids[d] == 0)
            half = half & (ids[d] == 0)
            last = last & (ids[d] == grid[d] - 1)

        def start():
            if comm.start is not None:
                @pl.when(first)
                def _():
                    comm.start(cins, couts, csems)

        if not own_copies_first:
            start()
        if comm.middle is not None:
            assert grid[0] >= 2

            @pl.when(half)
            def _():
                comm.middle(cins, couts, csems)

        body(*pre, *ins, *outs, *scr)
        if own_copies_first:
            start()

        if comm.finish is not None:
            @pl.when(last)
            def _():
                comm.finish(cins, couts, csems)

    specs = dict(grid=grid, in_specs=list(in_specs) + [ANY] * c_i, out_specs=list(out_specs) + [ANY] * c_o,
                 scratch_shapes=list(scratch_shapes) + comm.sems)
    if n_p:
        specs = dict(grid_spec=pltpu.PrefetchScalarGridSpec(num_scalar_prefetch=1, **specs))
    res = pl.pallas_call(
        carrier, out_shape=list(out_shape) + comm.out_shapes,
        input_output_aliases={n_p + n_i + k: n_o + v for k, v in comm.aliases.items()},
        name=name, compiler_params=params, **specs)(*(() if prefetch is None else (prefetch,)), *operands, *comm.inputs)
    return list(res[:n_o]), list(res[n_o:])


def _run_comm(comm, name):
    c_i, c_o = len(comm.inputs), len(comm.out_shapes)

    def body(*refs):
        ins, outs, sems = refs[:c_i], refs[c_i:c_i + c_o], refs[c_i + c_o:]
        comm.start(ins, outs, sems)
        comm.finish(ins, outs, sems)

    res = pl.pallas_call(
        body, in_specs=[ANY] * c_i, out_specs=[ANY] * c_o, out_shape=comm.out_shapes, scratch_shapes=comm.sems,
        input_output_aliases=comm.aliases, name=name)(*comm.inputs)
    return list(res)


def _gather_matmul(order, x, g, shard, *, tm, name, comm=None):
    S, K = x.shape
    nb = shard.shape[1]
    n_i = S // tm

    def body(order_ref, x_ref, g_ref, shard_ref, p_ref, h_ref, full_ref, hbuf, wbuf, stage, send_sems, recv_sems,
             dma_sems):
        j, i = pl.program_id(0), pl.program_id(1)
        px, py, pc = _place()
        cps = _gather_copies(stage, full_ref, 1, nb, send_sems, recv_sems, 0)
        own = pltpu.make_async_copy(stage, _piece(full_ref, 1, nb, 4 * px + 2 * py + pc), dma_sems.at[0])
        keep_h = pltpu.make_async_copy(hbuf, h_ref, dma_sems.at[2])

        def load(src, dst):
            cp = pltpu.make_async_copy(src, dst, dma_sems.at[1])
            cp.start(priority=LOCAL)
            cp.wait()

        def load_pair(qx, qy):
            load(_piece(full_ref, 1, 2 * nb, 2 * qx + qy), wbuf)

        @pl.when((j == 0) & (i == 0))
        def _():
            load(shard_ref, stage)
            own.start(priority=LOCAL)
            for k in (0, 1, 2):
                cps[k].start()

        @pl.when(j == 0)
        def _():
            xx = x_ref[...]
            hbuf[i] = ((xx * _rms(xx)) * g_ref[...]).astype(BF16)

        @pl.when((j == 0) & (i == 0))
        def _():
            own.wait()
            cps[0].wait_recv()
            load_pair(px, py)

        @pl.when((j == 1) & (i == 0))
        def _():
            keep_h.start(priority=LOCAL)
            cps[1].wait_recv()
            cps[3].start()
            cps[5].start()
            cps[2].wait_recv()
            cps[4].start()
            cps[6].start()
            cps[5].wait_recv()
            load_pair(1 - px, py)

        @pl.when((j == 2) & (i == 0))
        def _():
            cps[6].wait_recv()
            load_pair(px, 1 - py)

        @pl.when((j == 2) & (i == n_i // 2))
        def _():
            cps[3].wait_recv()
            cps[4].wait_recv()
            cps[7].start()

        @pl.when((j == 3) & (i == 0))
        def _():
            cps[7].wait_recv()
            load_pair(1 - px, 1 - py)

        p_ref[...] = jnp.dot(hbuf[i], wbuf[...], preferred_element_type=F32).astype(BF16)

        @pl.when((j == 3) & (i == n_i - 1))
        def _():
            for cp in cps:
                cp.wait_send()
            keep_h.wait()

    first_pass = lambda j, i, o: (jnp.where(j == 0, i, n_i - 1), 0)
    outs, extra = _call(
        body, grid=(4, n_i), prefetch=order,
        in_specs=[pl.BlockSpec((tm, K), first_pass), pl.BlockSpec((1, K), lambda j, i, o: (0, 0)), ANY],
        out_specs=[pl.BlockSpec((tm, 2 * nb), lambda j, i, o: (i, o[j])), ANY, ANY],
        out_shape=[_sds((S, N_DEV * nb), BF16), _sds((n_i, tm, K), BF16), _sds((K, N_DEV * nb), BF16)],
        operands=(x, g, shard),
        scratch_shapes=[pltpu.VMEM((n_i, tm, K), BF16), pltpu.VMEM((K, 2 * nb), BF16), pltpu.VMEM((K, nb), BF16),
                        pltpu.SemaphoreType.DMA((N_GATHER,)), pltpu.SemaphoreType.DMA((N_GATHER,)),
                        pltpu.SemaphoreType.DMA((3,))],
        name=name, params=_params(("arbitrary", "arbitrary"), 58), comm=comm, own_copies_first=True)
    return outs[0], outs[1].reshape(S, K), outs[2], extra


def _out_norm_res(u, w, x, g, *, tm, name, comm=None):
    S, K = u.shape
    D = w.shape[1]

    def body(u_ref, w_ref, x_ref, g_ref, x1_ref, y_ref):
        y = jnp.dot(u_ref[...], w_ref[...], preferred_element_type=F32)
        y_ref[...] = y.astype(BF16)
        x1_ref[...] = x_ref[...] + (y * _rms(y)) * g_ref[...]

    return _call(
        body, grid=(S // tm,),
        in_specs=[pl.BlockSpec((tm, K), lambda i: (i, 0)), _const((K, D), single=True),
                  pl.BlockSpec((tm, D), lambda i: (i, 0)), _const((1, D))],
        out_specs=[pl.BlockSpec((tm, D), lambda i: (i, 0)), pl.BlockSpec((tm, D), lambda i: (i, 0))],
        out_shape=[_sds((S, D), F32), _sds((S, D), BF16)], operands=(u, w, x, g),
        name=name, params=_params(("arbitrary",), 56), comm=comm)


def _out_loss(yy, w, x1, g, tgt, *, tm, name):
    S, K = yy.shape
    D = w.shape[1]

    def body(yy_ref, w_ref, x1_ref, g_ref, t_ref, dout_ref, dx2_ref, dyy_ref, lcol_ref, dg_ref):
        out = jnp.dot(yy_ref[...], w_ref[...], preferred_element_type=F32)
        r = _rms(out)
        n = out * r
        gg = g_ref[...]
        e = x1_ref[...] + n * gg - t_ref[...]
        dx2 = e * (1.0 / D)
        dx2_ref[...] = dx2
        dout = _norm_bwd(dx2 * gg, n, r).astype(BF16)
        dout_ref[...] = dout
        dyy_ref[...] = lax.dot_general(dout, w_ref[...], (((1,), (1,)), ((), ())),
                                       preferred_element_type=F32).astype(BF16)

        @pl.when(pl.program_id(0) == 0)
        def _():
            lcol_ref[...] = jnp.zeros_like(lcol_ref)
            dg_ref[...] = jnp.zeros_like(dg_ref)

        lcol_ref[...] += _colsum(e * e)
        dg_ref[...] += _colsum(dx2 * n)

    return _call(
        body, grid=(S // tm,),
        in_specs=[pl.BlockSpec((tm, K), lambda i: (i, 0)), _const((K, D), single=True),
                  pl.BlockSpec((tm, D), lambda i: (i, 0)), _const((1, D)),
                  pl.BlockSpec((tm, D), lambda i: (i, 0))],
        out_specs=[pl.BlockSpec((tm, D), lambda i: (i, 0)), pl.BlockSpec((tm, D), lambda i: (i, 0)),
                   pl.BlockSpec((tm, K), lambda i: (i, 0)), _const((1, D)), _const((1, D))],
        out_shape=[_sds((S, D), BF16), _sds((S, D), F32), _sds((S, K), BF16), _sds((1, D), F32), _sds((1, D), F32)],
        operands=(yy, w, x1, g, tgt), name=name, params=_params(("arbitrary",), 52))[0]


def _mm_nt(a, w, *, tm, tk, name, comm=None):
    S, N = a.shape
    D = w.shape[0]
    n_k = N // tk

    def body(a_ref, w_ref, o_ref, acc_ref):
        k = pl.program_id(1)

        @pl.when(k == 0)
        def _():
            acc_ref[...] = jnp.zeros_like(acc_ref)

        acc_ref[...] = lax.dot_general(a_ref[...], w_ref[...], (((1,), (1,)), ((), ())),
                                       preferred_element_type=F32) + acc_ref[...]

        @pl.when(k == n_k - 1)
        def _():
            o_ref[...] = acc_ref[...].astype(BF16)

    outs, extra = _call(
        body, grid=(S // tm, n_k),
        in_specs=[pl.BlockSpec((tm, tk), lambda i, k: (i, k)), pl.BlockSpec((D, tk), lambda i, k: (0, k))],
        out_specs=[pl.BlockSpec((tm, D), lambda i, k: (i, 0))],
        out_shape=[_sds((S, D), BF16)], operands=(a, w),
        scratch_shapes=[pltpu.VMEM((tm, D), F32)],
        name=name, params=_params(("arbitrary", "arbitrary"), 48), comm=comm)
    return outs[0], extra


def _mm_tn(a, b, *, ts, tn, name, comm=None):
    S, M = a.shape
    N = b.shape[1]
    n_s = S // ts

    def body(a_ref, b_ref, o_ref, acc_ref):
        s = pl.program_id(1)

        @pl.when(s == 0)
        def _():
            acc_ref[...] = jnp.zeros_like(acc_ref)

        acc_ref[...] = lax.dot_general(a_ref[...], b_ref[...], (((0,), (0,)), ((), ())),
                                       preferred_element_type=F32) + acc_ref[...]

        @pl.when(s == n_s - 1)
        def _():
            o_ref[...] = acc_ref[...].astype(BF16)

    outs, extra = _call(
        body, grid=(N // tn, n_s),
        in_specs=[pl.BlockSpec((ts, M), lambda j, s: (s, 0)), pl.BlockSpec((ts, tn), lambda j, s: (s, j))],
        out_specs=[pl.BlockSpec((M, tn), lambda j, s: (0, j))],
        out_shape=[_sds((M, N), BF16)], operands=(a, b),
        scratch_shapes=[pltpu.VMEM((M, tn), F32)],
        name=name, params=_params(("arbitrary", "arbitrary"), 48), comm=comm)
    return outs[0], extra


def _dw_reduce(order, a, b, nb, *, ts, name, comm=None):
    S, M = a.shape
    n_s = S // ts
    rows = 512

    def body(order_ref, a_ref, b_ref, sums_ref, from_sib_ref, from_chip_ref, acc, send_buf, mine_buf, recv_buf,
             sib_send, sib_recv, chip_send, chip_recv, dma_sems):
        t, s = pl.program_id(0), pl.program_id(1)
        x, y, c = _place()
        targets = [(1 - x, y, c), (x, 1 - y, c)]

        def to_sibling(k):
            return pltpu.make_async_remote_copy(
                src_ref=send_buf, dst_ref=from_sib_ref.at[k], send_sem=sib_send.at[k], recv_sem=sib_recv.at[k],
                device_id=(x, y, 1 - c), device_id_type=MESH)

        def to_chip(k):
            return pltpu.make_async_remote_copy(
                src_ref=sums_ref.at[k], dst_ref=from_chip_ref.at[k], send_sem=chip_send.at[k],
                recv_sem=chip_recv.at[k], device_id=targets[k], device_id_type=MESH)

        def finish(k):
            to_sibling(k).wait()
            get = pltpu.make_async_copy(from_sib_ref.at[k], recv_buf, dma_sems.at[0])
            get.start(priority=LOCAL)
            get.wait()
            for r in range(0, M, rows):
                recv_buf[r:r + rows, :] = (mine_buf[r:r + rows, :].astype(F32)
                                           + recv_buf[r:r + rows, :].astype(F32)).astype(BF16)
            put = pltpu.make_async_copy(recv_buf, sums_ref.at[k], dma_sems.at[1])
            put.start(priority=LOCAL)
            put.wait()
            if k < 2:
                to_chip(k).start()

        for k in range(3):
            @pl.when((t == k + 1) & (s == min(1, n_s - 1)))
            def _(k=k):
                finish(k)

        @pl.when(s == 0)
        def _():
            acc[...] = jnp.zeros_like(acc)

        acc[...] = lax.dot_general(a_ref[...], b_ref[...], (((0,), (0,)), ((), ())),
                                   preferred_element_type=F32) + acc[...]

        @pl.when(s == n_s - 1)
        def _():
            for r in range(0, M, rows):
                lo, hi = acc[r:r + rows, :nb], acc[r:r + rows, nb:]
                send_buf[r:r + rows, :] = jnp.where(c == 0, hi, lo).astype(BF16)
                mine_buf[r:r + rows, :] = jnp.where(c == 0, lo, hi).astype(BF16)
            to_sibling(t).start()

        @pl.when((t == 3) & (s == n_s - 1))
        def _():
            finish(3)
            to_chip(0).wait()
            to_chip(1).wait()

    piece = _sds((4, M, nb), BF16)
    outs, extra = _call(
        body, grid=(4, n_s), prefetch=order,
        in_specs=[pl.BlockSpec((ts, M), lambda t, s, o: (s, 0)), pl.BlockSpec((ts, 2 * nb), lambda t, s, o: (s, o[t]))],
        out_specs=[ANY, ANY, ANY], out_shape=[piece, piece, _sds((3, M, nb), BF16)], operands=(a, b),
        scratch_shapes=[pltpu.VMEM((M, 2 * nb), F32), pltpu.VMEM((M, nb), BF16), pltpu.VMEM((M, nb), BF16),
                        pltpu.VMEM((M, nb), BF16), pltpu.SemaphoreType.DMA((4,)), pltpu.SemaphoreType.DMA((4,)),
                        pltpu.SemaphoreType.DMA((2,)), pltpu.SemaphoreType.DMA((2,)), pltpu.SemaphoreType.DMA((2,))],
        name=name, params=_params(("arbitrary", "arbitrary"), 56), comm=comm)
    return outs[0], outs[2], extra


def _diag_comm(sums, from_chip):
    def copy(ins, outs, sems):
        x, y, c = _place()
        return pltpu.make_async_remote_copy(
            src_ref=ins[0].at[2], dst_ref=outs[0].at[2], send_sem=sems[0].at[0], recv_sem=sems[1].at[0],
            device_id=(1 - x, 1 - y, c), device_id_type=MESH)

    def start(ins, outs, sems):
        copy(ins, outs, sems).start()

    def finish(ins, outs, sems):
        copy(ins, outs, sems).wait()

    sems = [pltpu.SemaphoreType.DMA((1,)), pltpu.SemaphoreType.DMA((1,))]
    return _Comm([sums, from_chip], [_sds(from_chip.shape, from_chip.dtype)], sems, start, finish, aliases={1: 0})


def _pre_bwd_o(dh, x1, dx2, y0, g_pre, g_post, *, tm, name, comm=None):
    S, D = x1.shape

    def body(dh_ref, x1_ref, dx2_ref, y0_ref, gpre_ref, gpost_ref, dx1_ref, dy0_ref, dgpre_ref, dgpost_ref):
        @pl.when(pl.program_id(0) == 0)
        def _():
            dgpre_ref[...] = jnp.zeros_like(dgpre_ref)
            dgpost_ref[...] = jnp.zeros_like(dgpost_ref)

        dh = dh_ref[...].astype(F32)
        x1 = x1_ref[...]
        r2 = _rms(x1)
        xn = x1 * r2
        dgpre_ref[...] += _colsum(dh * xn)
        dx1 = dx2_ref[...] + _norm_bwd(dh * gpre_ref[...], xn, r2)
        dx1_ref[...] = dx1
        y = y0_ref[...].astype(F32)
        r1 = _rms(y)
        n1 = y * r1
        dgpost_ref[...] += _colsum(dx1 * n1)
        dy0_ref[...] = _norm_bwd(dx1 * gpost_ref[...], n1, r1).astype(BF16)

    row = pl.BlockSpec((tm, D), lambda i: (i, 0))
    return _call(
        body, grid=(S // tm,),
        in_specs=[row, row, row, row, _const((1, D)), _const((1, D))],
        out_specs=[row, row, _const((1, D)), _const((1, D))],
        out_shape=[_sds((S, D), F32), _sds((S, D), BF16), _sds((1, D), F32), _sds((1, D), F32)],
        operands=(dh, x1, dx2, y0, g_pre, g_post),
        name=name, params=_params(("arbitrary",), 48), comm=comm)


def _pre_bwd_e(dh, x, dx1, g_pre, *, tm, name):
    S, D = x.shape

    def body(dh_ref, x_ref, dx1_ref, gpre_ref, gx_ref, dgpre_ref):
        @pl.when(pl.program_id(0) == 0)
        def _():
            dgpre_ref[...] = jnp.zeros_like(dgpre_ref)

        dh = dh_ref[...].astype(F32)
        xx = x_ref[...]
        r0 = _rms(xx)
        xn = xx * r0
        dgpre_ref[...] += _colsum(dh * xn)
        gx_ref[...] = dx1_ref[...] + _norm_bwd(dh * gpre_ref[...], xn, r0)

    row = pl.BlockSpec((tm, D), lambda i: (i, 0))
    return _call(
        body, grid=(S // tm,),
        in_specs=[row, row, row, _const((1, D))],
        out_specs=[row, _const((1, D))],
        out_shape=[_sds((S, D), F32), _sds((1, D), F32)],
        operands=(dh, x, dx1, g_pre), name=name, params=_params(("arbitrary",), 56))[0]


SUBLANES = 8


def _shift_copies(sh_ref, ext_ref, cs):
    for b in range(1, SUBLANES):
        sh_ref[b - 1] = ext_ref[pl.ds(b, sh_ref.shape[1]), cs]


def _rows_at(ext_ref, sh_ref, off, cs, tm):
    b = off % SUBLANES
    if b == 0 or sh_ref is None:
        return ext_ref[pl.ds(off, tm), cs]
    return sh_ref[b - 1, pl.ds(off - b, tm), :]


def _taps(ext_ref, w_ref, n_taps, base, cs, tm, sh_ref=None):
    acc = _rows_at(ext_ref, sh_ref, base, cs, tm) * w_ref[0:1, cs]
    for k in range(1, n_taps):
        acc = acc + _rows_at(ext_ref, sh_ref, base + k, cs, tm) * w_ref[k:k + 1, cs]
    return acc


def _taps_rev(ext_ref, w_ref, n_taps, cs, tm, sh_ref=None):
    acc = _rows_at(ext_ref, sh_ref, n_taps - 1, cs, tm) * w_ref[0:1, cs]
    for k in range(1, n_taps):
        acc = acc + _rows_at(ext_ref, sh_ref, n_taps - 1 - k, cs, tm) * w_ref[k:k + 1, cs]
    return acc


def _e_mix_fwd(p, wa, wb, bias, ln_g, ln_b, *, tm, name, comm=None):
    S = p.shape[0]
    W = p.shape[1] // 7
    nb = tm // HALO
    chunks = [slice(c * LANES, (c + 1) * LANES) for c in range(W // LANES)]

    def body(p_ref, hax_ref, hac_ref, hbv_ref, hbg_ref, wa_ref, wb_ref, bias_ref, lg_ref, lb_ref,
             u_ref, cb_ref, ext_ref, sh_ref):
        keep = (pl.program_id(0) > 0).astype(F32)
        col = lambda j, cs: p_ref[:, j * W + cs.start:j * W + cs.stop].astype(F32)

        ext_ref[0:HALO, :] = hax_ref[...].astype(F32) * hac_ref[...].astype(F32) * keep
        ext_ref[HALO:, :] = p_ref[:, 2 * W:3 * W].astype(F32) * p_ref[:, 0:W].astype(F32)
        for cs in chunks:
            conv = _taps(ext_ref, wa_ref, CONV_A, HALO - (CONV_A - 1), cs, tm)
            az = col(3, cs)
            u_ref[:, cs] = (col(1, cs) * conv * (az * _sig(az))).astype(BF16)

        ext_ref[0:HALO, :] = hbv_ref[...].astype(F32) * _sig(hbg_ref[...].astype(F32)) * keep
        ext_ref[HALO:, :] = p_ref[:, 4 * W:5 * W].astype(F32) * _sig(p_ref[:, 5 * W:6 * W].astype(F32))
        s1 = jnp.zeros((tm, LANES), F32)
        for cs in chunks:
            _shift_copies(sh_ref, ext_ref, cs)
            cb = _taps(ext_ref, wb_ref, CONV_B, HALO - (CONV_B - 1), cs, tm, sh_ref) + bias_ref[:, cs]
            cb_ref[:, cs] = cb
            s1 = s1 + cb
        mu = jnp.sum(s1, axis=-1, keepdims=True) * (1.0 / W)
        s2 = jnp.zeros((tm, LANES), F32)
        for cs in chunks:
            xc = cb_ref[:, cs] - mu
            s2 = s2 + xc * xc
        rs = lax.rsqrt(jnp.sum(s2, axis=-1, keepdims=True) * (1.0 / W) + EPS)
        for cs in chunks:
            lb = (cb_ref[:, cs] - mu) * rs * lg_ref[:, cs] + lb_ref[:, cs]
            bz = col(6, cs)
            u_ref[:, W + cs.start:W + cs.stop] = (lb * _sig(lb) * (bz * _sig(bz))).astype(BF16)

    prev = lambda j: pl.BlockSpec((HALO, W), lambda i: (jnp.maximum(i * nb - 1, 0), j))
    return _call(
        body, grid=(S // tm,),
        in_specs=[pl.BlockSpec((tm, 7 * W), lambda i: (i, 0)), prev(0), prev(2), prev(4), prev(5),
                  _const((CONV_A, W)), _const((CONV_B, W)), _const((1, W)), _const((1, W)), _const((1, W))],
        out_specs=[pl.BlockSpec((tm, 2 * W), lambda i: (i, 0)), pl.BlockSpec((tm, W), lambda i: (i, 0))],
        out_shape=[_sds((S, 2 * W), BF16), _sds((S, W), F32)],
        operands=(p, p, p, p, p, wa, wb, bias, ln_g, ln_b),
        scratch_shapes=[pltpu.VMEM((HALO + tm, W), F32),
                        pltpu.VMEM((SUBLANES - 1, HALO + tm - SUBLANES, LANES), F32)],
        name=name, params=_params(("arbitrary",), 48), comm=comm)


def _e_mix_bwd(du, p, cb, wa, wb, ln_g, ln_b, *, tm, name, comm=None):
    S = p.shape[0]
    W = p.shape[1] // 7
    nb = tm // HALO
    n_t = S // tm
    last_blk = S // HALO - 1
    chunks = [slice(c * LANES, (c + 1) * LANES) for c in range(W // LANES)]

    def body(du_ref, duf_ref, p_ref, fab_ref, faz_ref, fbz_ref, hax_ref, hac_ref, hbv_ref, hbg_ref,
             cb_ref, cbf_ref, wa_ref, wb_ref, lg_ref, lb_ref,
             dp_ref, dwa_ref, dwb_ref, dbias_ref, dlg_ref, dlb_ref, extd_ref, extg_ref, shd_ref, shg_ref):
        i = pl.program_id(0)
        keep_prev = (i > 0).astype(F32)
        keep_next = (i < n_t - 1).astype(F32)
        col = lambda j, cs: p_ref[:, j * W + cs.start:j * W + cs.stop].astype(F32)

        @pl.when(i == 0)
        def _():
            dwa_ref[...] = jnp.zeros_like(dwa_ref)
            dwb_ref[...] = jnp.zeros_like(dwb_ref)
            dbias_ref[...] = jnp.zeros_like(dbias_ref)
            dlg_ref[...] = jnp.zeros_like(dlg_ref)
            dlb_ref[...] = jnp.zeros_like(dlb_ref)

        def dcb_rows(rows, cb_rows_ref, dub, bz_of, dst0, scale, main):
            cbv = cb_rows_ref[...]
            mu = jnp.mean(cbv, axis=-1, keepdims=True)
            xc = cbv - mu
            rs = lax.rsqrt(jnp.mean(xc * xc, axis=-1, keepdims=True) + EPS)
            m1 = jnp.zeros((rows, LANES), F32)
            m2 = jnp.zeros((rows, LANES), F32)
            for cs in chunks:
                nbv = (cb_rows_ref[:, cs] - mu) * rs
                lb = nbv * lg_ref[:, cs] + lb_ref[:, cs]
                sl = _sig(lb)
                bz = bz_of(cs)
                sz = _sig(bz)
                dub_c = dub(cs)
                dlb = dub_c * (bz * sz) * _dsilu(lb, sl)
                if main:
                    dlg_ref[:, cs] += _colsum_mxu(dlb * nbv)
                    dlb_ref[:, cs] += _colsum_mxu(dlb)
                    dp_ref[:, 6 * W + cs.start:6 * W + cs.stop] = (dub_c * (lb * sl) * _dsilu(bz, sz)).astype(BF16)
                dnb = dlb * lg_ref[:, cs]
                extd_ref[dst0:dst0 + rows, cs] = dnb
                m1 = m1 + dnb
                m2 = m2 + dnb * nbv
            m1 = jnp.sum(m1, axis=-1, keepdims=True) * (1.0 / W)
            m2 = jnp.sum(m2, axis=-1, keepdims=True) * (1.0 / W)
            for cs in chunks:
                nbv = (cb_rows_ref[:, cs] - mu) * rs
                dcb = rs * (extd_ref[dst0:dst0 + rows, cs] - m1 - nbv * m2) * scale
                extd_ref[dst0:dst0 + rows, cs] = dcb
                if main:
                    dbias_ref[:, cs] += _colsum_mxu(dcb)

        dcb_rows(tm, cb_ref, lambda cs: du_ref[:, W + cs.start:W + cs.stop].astype(F32),
                 lambda cs: col(6, cs), 0, 1.0, True)
        dcb_rows(HALO, cbf_ref, lambda cs: duf_ref[:, W + cs.start:W + cs.stop].astype(F32),
                 lambda cs: fbz_ref[:, cs].astype(F32), tm, keep_next, False)

        extg_ref[0:HALO, :] = hbv_ref[...].astype(F32) * _sig(hbg_ref[...].astype(F32)) * keep_prev
        extg_ref[HALO:, :] = p_ref[:, 4 * W:5 * W].astype(F32) * _sig(p_ref[:, 5 * W:6 * W].astype(F32))
        base_b = HALO - (CONV_B - 1)
        for cs in chunks:
            _shift_copies(shd_ref, extd_ref, cs)
            _shift_copies(shg_ref, extg_ref, cs)
            dgb = _taps_rev(extd_ref, wb_ref, CONV_B, cs, tm, shd_ref)
            bv = col(4, cs)
            sg = _sig(col(5, cs))
            dp_ref[:, 4 * W + cs.start:4 * W + cs.stop] = (dgb * sg).astype(BF16)
            dp_ref[:, 5 * W + cs.start:5 * W + cs.stop] = (dgb * bv * sg * (1.0 - sg)).astype(BF16)
            dcb = extd_ref[0:tm, cs]
            for k in range(CONV_B):
                prod = (dcb * _rows_at(extg_ref, shg_ref, base_b + k, cs, tm)).astype(BF16)
                dwb_ref[k:k + 1, cs] += jnp.dot(jnp.ones((SUBLANES, tm), BF16), prod,
                                                preferred_element_type=F32)[0:1]

        extg_ref[0:HALO, :] = hax_ref[...].astype(F32) * hac_ref[...].astype(F32) * keep_prev
        extg_ref[HALO:, :] = p_ref[:, 2 * W:3 * W].astype(F32) * p_ref[:, 0:W].astype(F32)
        base_a = HALO - (CONV_A - 1)
        for cs in chunks:
            conv = _taps(extg_ref, wa_ref, CONV_A, base_a, cs, tm)
            az = col(3, cs)
            sz = _sig(az)
            ab = col(1, cs)
            dua = du_ref[:, cs].astype(F32)
            dya = dua * (az * sz)
            dp_ref[:, W + cs.start:W + cs.stop] = (dya * conv).astype(BF16)
            dp_ref[:, 3 * W + cs.start:3 * W + cs.stop] = (dua * (ab * conv) * _dsilu(az, sz)).astype(BF16)
            extd_ref[0:tm, cs] = dya * ab
            azf = faz_ref[:, cs].astype(F32)
            extd_ref[tm:tm + HALO, cs] = (duf_ref[:, cs].astype(F32) * (azf * _sig(azf))
                                          * fab_ref[:, cs].astype(F32) * keep_next)
        for cs in chunks:
            dca = _taps_rev(extd_ref, wa_ref, CONV_A, cs, tm)
            dp_ref[:, cs] = (dca * col(2, cs)).astype(BF16)
            dp_ref[:, 2 * W + cs.start:2 * W + cs.stop] = (dca * col(0, cs)).astype(BF16)
            dconv = extd_ref[0:tm, cs]
            for k in range(CONV_A):
                dwa_ref[k:k + 1, cs] += _colsum_mxu(dconv * extg_ref[pl.ds(base_a + k, tm), cs])

    prev = lambda j: pl.BlockSpec((HALO, W), lambda i: (jnp.maximum(i * nb - 1, 0), j))
    nxt = lambda j, w: pl.BlockSpec((HALO, w), lambda i: (jnp.minimum((i + 1) * nb, last_blk), j))
    row = lambda w: pl.BlockSpec((tm, w), lambda i: (i, 0))
    return _call(
        body, grid=(n_t,),
        in_specs=[row(2 * W), nxt(0, 2 * W), row(7 * W), nxt(1, W), nxt(3, W), nxt(6, W),
                  prev(0), prev(2), prev(4), prev(5), row(W), nxt(0, W),
                  _const((CONV_A, W)), _const((CONV_B, W)), _const((1, W)), _const((1, W))],
        out_specs=[row(7 * W), _const((CONV_A, W)), _const((CONV_B, W)), _const((1, W)), _const((1, W)), _const((1, W))],
        out_shape=[_sds((S, 7 * W), BF16), _sds((CONV_A, W), F32), _sds((CONV_B, W), F32),
                   _sds((1, W), F32), _sds((1, W), F32), _sds((1, W), F32)],
        operands=(du, du, p, p, p, p, p, p, p, p, cb, cb, wa, wb, ln_g, ln_b),
        scratch_shapes=[pltpu.VMEM((tm + HALO, W), F32), pltpu.VMEM((HALO + tm, W), F32),
                        pltpu.VMEM((SUBLANES - 1, HALO + tm - SUBLANES, LANES), F32),
                        pltpu.VMEM((SUBLANES - 1, HALO + tm - SUBLANES, LANES), F32)],
        name=name, params=_params(("arbitrary",), 52), comm=comm)


def _counts(i, tm, rows, off, win):
    t = i * tm + off + lax.broadcasted_iota(jnp.int32, (rows, 1), 0)
    return jnp.minimum(t + 1, win).astype(F32)


def _o_mix_fwd(q, cw, cb, cscale, *, tm, name):
    S = q.shape[0]
    WC = q.shape[1] // 2
    NG = len(POOL_WINDOWS)
    G = WC // NG
    nb = tm // PHALO

    def body(v_ref, z_ref, hv_ref, cw_ref, cb_ref, sc_ref, yy_ref, pooled_ref, gg_ref, ext_ref):
        i = pl.program_id(0)
        keep = (i > 0).astype(F32)
        for g, win in enumerate(POOL_WINDOWS):
            cs = slice(g * G, (g + 1) * G)
            v = v_ref[:, cs].astype(F32)
            ext_ref[0:PHALO, :] = hv_ref[:, cs].astype(F32) * keep
            ext_ref[PHALO:, :] = v
            s = v
            for j in range(1, win):
                s = s + ext_ref[pl.ds(PHALO - j, tm), :]
            pooled = (s / _counts(i, tm, tm, 0, win) - v).astype(BF16)
            pooled_ref[:, cs] = pooled
            gg = jnp.dot(pooled, cw_ref[g], preferred_element_type=F32) + cb_ref[:, cs]
            gg_ref[:, cs] = gg.astype(BF16)
            z = z_ref[:, cs].astype(F32)
            yy_ref[:, cs] = (gg * sc_ref[:, cs] * (z * _sig(z))).astype(BF16)

    row = lambda j: pl.BlockSpec((tm, WC), lambda i: (i, j))
    out = pl.BlockSpec((tm, WC), lambda i: (i, 0))
    return _call(
        body, grid=(S // tm,),
        in_specs=[row(0), row(1), pl.BlockSpec((PHALO, WC), lambda i: (jnp.maximum(i * nb - 1, 0), 0)),
                  _const((NG, G, G)), _const((1, WC)), _const((1, WC))],
        out_specs=[out, out, out],
        out_shape=[_sds((S, WC), BF16)] * 3, operands=(q, q, q, cw, cb, cscale),
        scratch_shapes=[pltpu.VMEM((PHALO + tm, G), F32)],
        name=name, params=_params(("arbitrary",), 40))[0]


def _o_mix_bwd(dyy, q, gg, pooled, cw, cscale, *, tm, name):
    S = q.shape[0]
    WC = q.shape[1] // 2
    NG = len(POOL_WINDOWS)
    G = WC // NG
    nb = tm // PHALO
    n_t = S // tm
    last_blk = S // PHALO - 1
    nt = (((1,), (1,)), ((), ()))
    tn = (((0,), (0,)), ((), ()))

    def body(dyy_ref, dyyf_ref, z_ref, zf_ref, gg_ref, pooled_ref, cw_ref, sc_ref,
             dq_ref, dcw_ref, dcb_ref, dsc_ref, ext_ref):
        i = pl.program_id(0)
        keep_next = (i < n_t - 1).astype(F32)

        @pl.when(i == 0)
        def _():
            dcw_ref[...] = jnp.zeros_like(dcw_ref)
            dcb_ref[...] = jnp.zeros_like(dcb_ref)
            dsc_ref[...] = jnp.zeros_like(dsc_ref)

        for g, win in enumerate(POOL_WINDOWS):
            cs = slice(g * G, (g + 1) * G)
            sc = sc_ref[:, cs]
            z = z_ref[:, cs].astype(F32)
            sz = _sig(z)
            dyy_c = dyy_ref[:, cs].astype(F32)
            ggv = gg_ref[:, cs].astype(F32)
            dyy0 = dyy_c * (z * sz)
            dq_ref[:, WC + cs.start:WC + cs.stop] = (dyy_c * (ggv * sc) * _dsilu(z, sz)).astype(BF16)
            dgg = dyy0 * sc
            dsc_ref[:, cs] += _colsum(dyy0 * ggv)
            dcb_ref[:, cs] += _colsum(dgg)
            dgg_b = dgg.astype(BF16)
            dcw_ref[g] += lax.dot_general(pooled_ref[:, cs], dgg_b, tn, preferred_element_type=F32)
            dpool = lax.dot_general(dgg_b, cw_ref[g], nt, preferred_element_type=F32)
            zf = zf_ref[:, cs].astype(F32)
            dgg_f = (dyyf_ref[:, cs].astype(F32) * (zf * _sig(zf)) * sc * keep_next).astype(BF16)
            dpool_f = lax.dot_general(dgg_f, cw_ref[g], nt, preferred_element_type=F32)
            ext_ref[0:tm, :] = dpool / _counts(i, tm, tm, 0, win)
            ext_ref[tm:tm + PHALO, :] = dpool_f / _counts(i, tm, PHALO, tm, win)
            dv = ext_ref[0:tm, :] - dpool
            for j in range(1, win):
                dv = dv + ext_ref[pl.ds(j, tm), :]
            dq_ref[:, cs] = dv.astype(BF16)

    row = lambda: pl.BlockSpec((tm, WC), lambda i: (i, 0))
    nxt = lambda j: pl.BlockSpec((PHALO, WC), lambda i: (jnp.minimum((i + 1) * nb, last_blk), j))
    return _call(
        body, grid=(n_t,),
        in_specs=[row(), nxt(0), pl.BlockSpec((tm, WC), lambda i: (i, 1)), nxt(1), row(), row(),
                  _const((NG, G, G)), _const((1, WC))],
        out_specs=[pl.BlockSpec((tm, 2 * WC), lambda i: (i, 0)), _const((NG, G, G)), _const((1, WC)), _const((1, WC))],
        out_shape=[_sds((S, 2 * WC), BF16), _sds((NG, G, G), F32), _sds((1, WC), F32), _sds((1, WC), F32)],
        operands=(dyy, dyy, q, q, gg, pooled, cw, cscale),
        scratch_shapes=[pltpu.VMEM((tm + PHALO, G), F32)],
        name=name, params=_params(("arbitrary",), 48))[0]


def _place():
    return lax.axis_index("x"), lax.axis_index("y"), lax.axis_index("c")


def _piece(ref, axis, size, index):
    start = index * size
    if axis == len(ref.shape) - 1:
        start = pl.multiple_of(start, LANES)
    idx = [slice(None)] * len(ref.shape)
    idx[axis] = pl.ds(start, size)
    return ref.at[tuple(idx)]


def _gather_copies(src, out, axis, size, send_sems, recv_sems, base, held=None):
    x, y, c = _place()
    sib, xn, yn = (x, y, 1 - c), (1 - x, y, c), (x, 1 - y, c)

    def blk(px, py, of=out):
        return _piece(of, axis, size, 4 * px + 2 * py + c)

    def half(ref, h):
        n = ref.shape[0] // 2
        return ref.at[pl.ds(h * n, n)]

    def rc(k, s, d, to):
        return pltpu.make_async_remote_copy(src_ref=s, dst_ref=d, send_sem=send_sems.at[base + k],
                                            recv_sem=recv_sems.at[base + k], device_id=to, device_id_type=MESH)

    own, xb, yb, db = blk(x, y), blk(1 - x, y), blk(x, 1 - y), blk(1 - x, 1 - y)
    got = out if held is None else held
    xs, ys, ds = blk(1 - x, y, got), blk(x, 1 - y, got), blk(1 - x, 1 - y, got)
    return [rc(0, src, own, sib), rc(1, src, own, xn), rc(2, src, own, yn),
            rc(3, half(xs, 0), half(xb, 0), yn), rc(4, half(ys, 1), half(yb, 1), xn),
            rc(5, xs, xb, sib), rc(6, ys, yb, sib), rc(7, ds, db, sib)]


N_GATHER = 8


def _gather_comm(shards, axes, phases):
    n = len(shards)
    if phases == "second":
        sizes = [s.shape[a] // N_DEV for s, a in zip(shards, axes)]
        full = [_sds(s.shape, s.dtype) for s in shards]
    else:
        sizes = [s.shape[a] for s, a in zip(shards, axes)]
        full = [_sds(s.shape[:a] + (N_DEV * s.shape[a],) + s.shape[a + 1:], s.dtype) for s, a in zip(shards, axes)]

    def plan(ins, outs, sems):
        x, y, c = _place()
        me = 4 * x + 2 * y + c
        if phases == "second":
            cps = [_gather_copies(_piece(ins[t], axes[t], sizes[t], me), outs[t], axes[t], sizes[t], sems[0], sems[1],
                                  N_GATHER * t, ins[t]) for t in range(n)]
        else:
            cps = [_gather_copies(sems[3 + t], outs[t], axes[t], sizes[t], sems[0], sems[1], N_GATHER * t)
                   for t in range(n)]
        mine = [pltpu.make_async_copy(sems[3 + t], _piece(outs[t], axes[t], sizes[t], me), sems[2].at[t])
                for t in range(n)] if phases != "second" else []
        return cps, mine

    def send_own(ins, outs, sems):
        cps, mine = plan(ins, outs, sems)
        for t in range(n):
            stage = pltpu.make_async_copy(ins[t], sems[3 + t], sems[2].at[t])
            stage.start()
            stage.wait()
            mine[t].start()
            for k in (0, 1, 2):
                cps[t][k].start()

    def pass_on(ins, outs, sems):
        cps, _ = plan(ins, outs, sems)
        for t in range(n):
            if phases == "all":
                cps[t][1].wait_recv()
            cps[t][3].start()
            cps[t][5].start()
        for t in range(n):
            if phases == "all":
                cps[t][2].wait_recv()
            cps[t][4].start()
            cps[t][6].start()

    def own_landed(ins, outs, sems):
        cps, mine = plan(ins, outs, sems)
        for t in range(n):
            for k in (0, 1, 2):
                cps[t][k].wait()
            mine[t].wait()

    def all_landed(ins, outs, sems):
        cps, mine = plan(ins, outs, sems)
        for t in range(n):
            cps[t][3].wait_recv()
            cps[t][4].wait_recv()
            cps[t][7].start()
        for t in range(n):
            for k in ((0, 5, 6, 7) if phases == "all" else (5, 6, 7)):
                cps[t][k].wait_recv()
            for k in (range(N_GATHER) if phases == "all" else range(3, N_GATHER)):
                cps[t][k].wait_send()
            if phases == "all":
                mine[t].wait()

    sems = [pltpu.SemaphoreType.DMA((N_GATHER * n,)), pltpu.SemaphoreType.DMA((N_GATHER * n,))]
    if phases != "second":
        sems.append(pltpu.SemaphoreType.DMA((n,)))
        sems += [pltpu.VMEM(s.shape, s.dtype) for s in shards]
    if phases == "all":
        return _Comm(shards, full, sems, send_own, all_landed, middle=pass_on)
    if phases == "first":
        return _Comm(shards, full, sems, send_own, own_landed)
    return _Comm(shards, full, sems, pass_on, all_landed, aliases={t: t for t in range(n)})


def _pair_comm(grads, axes, sizes):
    n = len(grads)
    outs_sds = [_sds((4,) + g.shape[:a] + (s,) + g.shape[a + 1:], g.dtype) for g, a, s in zip(grads, axes, sizes)]

    def copies(ins, outs, sems):
        send_sems, recv_sems = sems
        x, y, c = _place()
        return [pltpu.make_async_remote_copy(
            src_ref=_piece(ins[t], axes[t], sizes[t], 2 * qi + (1 - c)), dst_ref=outs[t].at[qi],
            send_sem=send_sems.at[4 * t + qi], recv_sem=recv_sems.at[4 * t + qi],
            device_id=(x, y, 1 - c), device_id_type=MESH) for t in range(n) for qi in range(4)]

    def start(ins, outs, sems):
        for cp in copies(ins, outs, sems):
            cp.start()

    def finish(ins, outs, sems):
        for cp in copies(ins, outs, sems):
            cp.wait()

    sems = [pltpu.SemaphoreType.DMA((4 * n,)), pltpu.SemaphoreType.DMA((4 * n,))]
    return _Comm(grads, outs_sds, sems, start, finish)


def _chip_comm(sums):
    n = len(sums)
    outs_sds = [_sds((3,) + s.shape[1:], s.dtype) for s in sums]

    def copies(ins, outs, sems):
        send_sems, recv_sems = sems
        x, y, c = _place()
        return [pltpu.make_async_remote_copy(
            src_ref=ins[t].at[2 * qx + qy], dst_ref=outs[t].at[j],
            send_sem=send_sems.at[3 * t + j], recv_sem=recv_sems.at[3 * t + j],
            device_id=(qx, qy, c), device_id_type=MESH)
            for t in range(n) for j, (qx, qy) in enumerate([(1 - x, y), (x, 1 - y), (1 - x, 1 - y)])]

    def start(ins, outs, sems):
        for cp in copies(ins, outs, sems):
            cp.start()

    def finish(ins, outs, sems):
        for cp in copies(ins, outs, sems):
            cp.wait()

    sems = [pltpu.SemaphoreType.DMA((3 * n,)), pltpu.SemaphoreType.DMA((3 * n,))]
    return _Comm(sums, outs_sds, sems, start, finish)


def _small_comm(small):
    def copies(ins, outs, sems):
        send_sems, recv_sems, local_sem = sems
        x, y, c = _place()
        mine = outs[0].at[4 * x + 2 * y + c]
        out = [pltpu.make_async_copy(ins[0], mine, local_sem.at[0])]
        for k in range(1, N_DEV):
            peer = (1 - x if k & 4 else x, 1 - y if k & 2 else y, 1 - c if k & 1 else c)
            out.append(pltpu.make_async_remote_copy(
                src_ref=ins[0], dst_ref=mine, send_sem=send_sems.at[k - 1], recv_sem=recv_sems.at[k - 1],
                device_id=peer, device_id_type=MESH))
        return out

    def start(ins, outs, sems):
        for cp in copies(ins, outs, sems):
            cp.start()

    def finish(ins, outs, sems):
        for cp in copies(ins, outs, sems):
            cp.wait()

    sems = [pltpu.SemaphoreType.DMA((N_DEV - 1,)), pltpu.SemaphoreType.DMA((N_DEV - 1,)), pltpu.SemaphoreType.DMA((1,))]
    return _Comm([small], [_sds((N_DEV,) + small.shape, small.dtype)], sems, start, finish)


def _small_scatter_comm(send):
    def copies(ins, outs, sems):
        send_sems, recv_sems, local_sem = sems
        x, y, c = _place()
        me = 4 * x + 2 * y + c
        out = [pltpu.make_async_copy(ins[0].at[me], outs[0].at[me], local_sem.at[0])]
        for k in range(1, N_DEV):
            px, py, pc = (1 - x if k & 4 else x, 1 - y if k & 2 else y, 1 - c if k & 1 else c)
            out.append(pltpu.make_async_remote_copy(
                src_ref=ins[0].at[4 * px + 2 * py + pc], dst_ref=outs[0].at[me], send_sem=send_sems.at[k - 1],
                recv_sem=recv_sems.at[k - 1], device_id=(px, py, pc), device_id_type=MESH))
        return out

    def start(ins, outs, sems):
        for cp in copies(ins, outs, sems):
            cp.start()

    def finish(ins, outs, sems):
        for cp in copies(ins, outs, sems):
            cp.wait()

    sems = [pltpu.SemaphoreType.DMA((N_DEV - 1,)), pltpu.SemaphoreType.DMA((N_DEV - 1,)), pltpu.SemaphoreType.DMA((1,))]
    return _Comm([send], [_sds(send.shape, send.dtype)], sems, start, finish)


def _pair_sum(c_idx, grad, recv, axis, size, split, *, name):
    nd = len(grad.shape)
    piece = grad.shape[:axis] + (size,) + grad.shape[axis + 1:]
    blk = (piece[0] // split,) + piece[1:]

    def g_map(q, r, c_ref):
        idx = [0] * nd
        idx[axis] = 2 * q + c_ref[0]
        idx[0] = idx[0] * split + r if axis == 0 else r
        return tuple(idx)

    def r_map(q, r, c_ref):
        return (q, r) + (0,) * (nd - 1)

    def body(c_ref, g_ref, r_ref, o_ref):
        o_ref[0] = (g_ref[...].astype(F32) + r_ref[0].astype(F32)).astype(BF16)

    return _call(
        body, grid=(4, split), prefetch=c_idx,
        in_specs=[pl.BlockSpec(blk, g_map), pl.BlockSpec((1,) + blk, r_map)],
        out_specs=[pl.BlockSpec((1,) + blk, r_map)], out_shape=[_sds((4,) + piece, BF16)],
        operands=(grad, recv), name=name, params=_params(("arbitrary", "arbitrary"), 32))[0][0]


def _adam_math(w, g, m, v):
    m = ADAM_B1 * m + (1.0 - ADAM_B1) * g
    v = ADAM_B2 * v + (1.0 - ADAM_B2) * (g * g)
    m_hat = m / (1.0 - ADAM_B1 ** ADAM_STEP)
    v_hat = v / (1.0 - ADAM_B2 ** ADAM_STEP)
    delta = -ADAM_LR * (m_hat / (jnp.sqrt(v_hat) + ADAM_EPS) + ADAM_WD * w)
    return delta, m, v


def _adam_big(q_idx, sums, recv, w, m, v, split, *, name, comm=None):
    shape = w.shape
    nd = len(shape)
    blk = (shape[0] // split,) + shape[1:]
    w_map = lambda r, q_ref: (r,) + (0,) * (nd - 1)
    s_map = lambda r, q_ref: (q_ref[0], r) + (0,) * (nd - 1)
    r_map = lambda r, q_ref: (0, r) + (0,) * (nd - 1)

    def body(q_ref, s_ref, r_ref, w_ref, m_ref, v_ref, g_ref, d_ref, nm_ref, nv_ref):
        g = s_ref[0].astype(F32) + r_ref[0].astype(F32) + r_ref[1].astype(F32) + r_ref[2].astype(F32)
        g_ref[...] = g
        d_ref[...], nm_ref[...], nv_ref[...] = _adam_math(w_ref[...], g, m_ref[...], v_ref[...])

    wspec = pl.BlockSpec(blk, w_map)
    return _call(
        body, grid=(split,), prefetch=q_idx,
        in_specs=[pl.BlockSpec((1,) + blk, s_map), pl.BlockSpec((3,) + blk, r_map), wspec, wspec, wspec],
        out_specs=[wspec] * 4, out_shape=[_sds(shape, F32)] * 4, operands=(sums, recv, w, m, v),
        name=name, params=_params(("arbitrary",), 32), comm=comm)


def _adam_small(parts, w, m, v, *, name):
    R = w.shape[0]

    def body(p_ref, w_ref, m_ref, v_ref, g_ref, d_ref, nm_ref, nv_ref):
        g = p_ref[0]
        for d in range(1, N_DEV):
            g = g + p_ref[d]
        g_ref[...] = g
        d_ref[...], nm_ref[...], nv_ref[...] = _adam_math(w_ref[...], g, m_ref[...], v_ref[...])

    whole = _const((R, LANES))
    return _call(
        body, grid=(1,), in_specs=[_const((N_DEV, R, LANES)), whole, whole, whole], out_specs=[whole] * 4,
        out_shape=[_sds((R, LANES), F32)] * 4, operands=(parts, w, m, v), name=name,
        params=_params(("arbitrary",), 32))[0]


def _pack(arrs):
    return jnp.concatenate([a.reshape(-1) for a in arrs]).reshape(-1, LANES)


def _unpack(packed, shapes):
    flat = packed.reshape(-1)
    out, off = [], 0
    for s in shapes:
        n = 1
        for d in s:
            n *= d
        out.append(flat[off:off + n].reshape(s))
        off += n
    return out


BIG = ("e_in", "e_out", "o_in", "o_cw", "o_out")
BIG_AXIS = dict(e_in=1, e_out=0, o_in=1, o_cw=1, o_out=0)
BIG_SPLIT = dict(e_in=8, e_out=4, o_in=4, o_cw=4, o_out=4)
REPLICATED = ("e_norm_pre", "e_norm_post", "e_b_conv_bias", "e_b_ln_g", "e_b_ln_b")
SHARDED = ("e_a_conv", "e_b_conv", "o_norm_pre", "o_norm_post", "o_c_b", "o_c_scale")
SMALL = REPLICATED + SHARDED


class _Exchange:
    def __init__(self, shards, small, order, c_idx):
        self.q_idx = order[:1]
        self.shards = shards
        self.small = small
        self.order = order
        self.c_idx = c_idx
        self.reduced = {}

    def gather(self, keys):
        return _gather_comm([self.shards[k] for k in keys], [BIG_AXIS[k] for k in keys], "all")

    def gather1(self, keys):
        return _gather_comm([self.shards[k] for k in keys], [BIG_AXIS[k] for k in keys], "first")

    def gather2(self, keys, firsts):
        return _gather_comm(firsts, [BIG_AXIS[k] for k in keys], "second")

    def pair(self, grads):
        keys = list(grads)
        return _pair_comm([grads[k] for k in keys], [BIG_AXIS[k] for k in keys],
                          [grads[k].shape[BIG_AXIS[k]] // N_DEV for k in keys])

    def pair_sums(self, grads, received):
        return {k: _pair_sum(self.c_idx, grads[k], r, BIG_AXIS[k], grads[k].shape[BIG_AXIS[k]] // N_DEV,
                             BIG_SPLIT[k], name="pair_sum_" + k) for k, r in zip(grads, received)}

    def chips(self, sums):
        return _chip_comm([sums[k] for k in sums])

    def done(self, sums, received):
        self.reduced.update({k: (sums[k], r, self.q_idx) for k, r in zip(sums, received)})


def _local_step(x, tgt, w_small, ex):
    S, D = x.shape
    tnt, tx, tw = min(TM_NT, S), min(TM_MIX, S), min(TM_WIDE, S)

    wt = {}
    p, h0, wt["e_in"], got = _gather_matmul(ex.order, x, w_small["e_norm_pre"], ex.shards["e_in"], tm=tw,
                                            name="e_in_fwd", comm=_small_comm(ex.small))
    per_dev = [_unpack(got[0][d], [w_small[k].shape for k in SHARDED]) for d in range(N_DEV)]
    sm = {k: w_small[k] for k in REPLICATED}
    for j, k in enumerate(SHARDED):
        sm[k] = jnp.concatenate([per_dev[d][j] for d in range(N_DEV)], axis=-1)
    n_groups = sm["o_c_b"].shape[0]
    sm["o_c_b"] = sm["o_c_b"].reshape(1, -1)

    W = p.shape[1] // 7
    (u, cb), got = _e_mix_fwd(p, sm["e_a_conv"], sm["e_b_conv"], sm["e_b_conv_bias"], sm["e_b_ln_g"],
                              sm["e_b_ln_b"], tm=tx, name="e_mix_fwd", comm=ex.gather(["e_out"]))
    wt["e_out"] = got[0]
    late = ["o_out", "o_cw"]
    (x1, y0), part = _out_norm_res(u, wt["e_out"], x, sm["e_norm_post"], tm=tw, name="e_out_fwd",
                                   comm=ex.gather1(late))
    q, h1, wt["o_in"], got = _gather_matmul(ex.order, x1, sm["o_norm_pre"], ex.shards["o_in"], tm=tw,
                                            name="o_in_fwd", comm=ex.gather2(late, part))
    wt.update(zip(late, got))
    yy, pooled, gg = _o_mix_fwd(q, wt["o_cw"], sm["o_c_b"], sm["o_c_scale"], tm=tw, name="o_mix_fwd")
    dout, dx2, dyy, lcol, dg_o_post = _out_loss(yy, wt["o_out"], x1, sm["o_norm_post"], tgt, tm=tx, name="o_out_loss")
    loss = (0.5 / D) * jnp.sum(lcol)

    dq, d_cw, d_cb, d_cscale = _o_mix_bwd(dyy, q, gg, pooled, wt["o_cw"], sm["o_c_scale"], tm=tw, name="o_mix_bwd")
    g_o_out, _ = _mm_tn(yy, dout, ts=tnt, tn=W, name="o_out_dw")
    ga = dict(o_out=g_o_out, o_cw=d_cw.astype(BF16))
    dh1, ra = _mm_nt(dq, wt["o_in"], tm=tnt, tk=W, name="o_in_bwd", comm=ex.pair(ga))
    sa = ex.pair_sums(ga, ra)
    (dx1, dy0, dg_o_pre, dg_e_post), ra = _pre_bwd_o(dh1, x1, dx2, y0, sm["o_norm_pre"], sm["e_norm_post"],
                                                     tm=tx, name="o_pre_bwd", comm=ex.chips(sa))
    ex.done(sa, ra)
    g_o_in, _ = _mm_tn(h1, dq, ts=tnt, tn=W, name="o_in_dw")
    gb = dict(o_in=g_o_in)
    du, rb = _mm_nt(dy0, wt["e_out"], tm=tnt, tk=W, name="e_out_bwd", comm=ex.pair(gb))
    sb = ex.pair_sums(gb, rb)
    g_e_out, _ = _mm_tn(u, dy0, ts=tnt, tn=W, name="e_out_dw")
    gc = dict(e_out=g_e_out)
    (dp, d_wa, d_wb, d_bias, d_lg, d_lb), rbc = _e_mix_bwd(
        du, p, cb, sm["e_a_conv"], sm["e_b_conv"], sm["e_b_ln_g"], sm["e_b_ln_b"], tm=tx, name="e_mix_bwd",
        comm=_merge(ex.chips(sb), ex.pair(gc)))
    ex.done(sb, rbc[:1])
    sc = ex.pair_sums(gc, rbc[1:])
    order_out = jnp.concatenate([ex.order[1:], ex.order[:1]])
    sd, from_chip, rc = _dw_reduce(order_out, h0, dp, ex.shards["e_in"].shape[1], ts=tnt, name="e_in_dw",
                                   comm=ex.chips(sc))
    ex.done(sc, rc)
    dh0, rd = _mm_nt(dp, wt["e_in"], tm=tnt, tk=W, name="e_in_bwd", comm=_diag_comm(sd, from_chip))
    ex.reduced["e_in"] = (sd, rd[0], jnp.full((1,), 3, jnp.int32))
    grad_x, dg_e_pre = _pre_bwd_e(dh0, x, dx1, sm["e_norm_pre"], tm=tw, name="e_pre_bwd")

    small = dict(e_norm_pre=dg_e_pre, e_norm_post=dg_e_post, e_a_conv=d_wa, e_b_conv=d_wb, e_b_conv_bias=d_bias,
                 e_b_ln_g=d_lg, e_b_ln_b=d_lb, o_norm_pre=dg_o_pre, o_norm_post=dg_o_post,
                 o_c_b=d_cb.reshape(n_groups, -1), o_c_scale=d_cscale)
    return loss, grad_x, small


def kernel(x, e_norm_pre, e_norm_post, e_w_in, e_a_conv, e_b_conv, e_b_conv_bias, e_b_ln_g, e_b_ln_b, e_w_out, o_norm_pre, o_norm_post, o_w_in, o_c_w, o_c_b, o_c_scale, o_w_out, loss_target, m_e_norm_pre, m_e_norm_post, m_e_w_in, m_e_a_conv, m_e_b_conv, m_e_b_conv_bias, m_e_b_ln_g, m_e_b_ln_b, m_e_w_out, m_o_norm_pre, m_o_norm_post, m_o_w_in, m_o_c_w, m_o_c_b, m_o_c_scale, m_o_w_out, v_e_norm_pre, v_e_norm_post, v_e_w_in, v_e_a_conv, v_e_b_conv, v_e_b_conv_bias, v_e_b_ln_g, v_e_b_ln_b, v_e_w_out, v_o_norm_pre, v_o_norm_post, v_o_w_in, v_o_c_w, v_o_c_b, v_o_c_scale, v_o_w_out):
    xi, yi, ci = _place()
    w_big = dict(e_in=e_w_in[0], e_out=e_w_out[0], o_in=o_w_in[0], o_cw=o_c_w[0], o_out=o_w_out[0])
    m_big = dict(e_in=m_e_w_in[0], e_out=m_e_w_out[0], o_in=m_o_w_in[0], o_cw=m_o_c_w[0], o_out=m_o_w_out[0])
    v_big = dict(e_in=v_e_w_in[0], e_out=v_e_w_out[0], o_in=v_o_w_in[0], o_cw=v_o_c_w[0], o_out=v_o_w_out[0])
    w_small = dict(e_norm_pre=e_norm_pre, e_norm_post=e_norm_post, e_b_conv_bias=e_b_conv_bias, e_b_ln_g=e_b_ln_g,
                   e_b_ln_b=e_b_ln_b, e_a_conv=e_a_conv[0], e_b_conv=e_b_conv[0], o_norm_pre=o_norm_pre,
                   o_norm_post=o_norm_post, o_c_b=o_c_b[0], o_c_scale=o_c_scale)
    m_small = dict(e_norm_pre=m_e_norm_pre, e_norm_post=m_e_norm_post, e_b_conv_bias=m_e_b_conv_bias,
                   e_b_ln_g=m_e_b_ln_g, e_b_ln_b=m_e_b_ln_b, e_a_conv=m_e_a_conv[0], e_b_conv=m_e_b_conv[0],
                   o_norm_pre=m_o_norm_pre, o_norm_post=m_o_norm_post, o_c_b=m_o_c_b[0], o_c_scale=m_o_c_scale)
    v_small = dict(e_norm_pre=v_e_norm_pre, e_norm_post=v_e_norm_post, e_b_conv_bias=v_e_b_conv_bias,
                   e_b_ln_g=v_e_b_ln_g, e_b_ln_b=v_e_b_ln_b, e_a_conv=v_e_a_conv[0], e_b_conv=v_e_b_conv[0],
                   o_norm_pre=v_o_norm_pre, o_norm_post=v_o_norm_post, o_c_b=v_o_c_b[0], o_c_scale=v_o_c_scale)

    c_idx = jnp.reshape(ci, (1,)).astype(jnp.int32)
    order = jnp.stack([2 * xi + yi, 2 * (1 - xi) + yi, 2 * xi + (1 - yi), 2 * (1 - xi) + (1 - yi)]).astype(jnp.int32)
    ex = _Exchange({k: w_big[k].astype(BF16) for k in BIG}, _pack([w_small[k] for k in SHARDED]), order, c_idx)
    loss, grad_x, g_small = _local_step(x[0], loss_target[0], w_small, ex)

    big_out = {}
    for k in BIG:
        sums, received, q_idx = ex.reduced[k]
        big_out[k] = _adam_big(q_idx, sums, received, w_big[k], m_big[k], v_big[k], BIG_SPLIT[k], name="adam_" + k)[0]

    rep = _pack([g_small[k] for k in REPLICATED])
    loss_row = jnp.pad(jnp.reshape(loss, (1, 1)), ((0, 0), (0, LANES - 1)))
    blocks = []
    for k in SHARDED:
        r, n = w_small[k].shape
        blocks.append(g_small[k].reshape(r, N_DEV, n).transpose(1, 0, 2).reshape(N_DEV, r * n))
    blocks = jnp.concatenate(blocks, axis=1).reshape(N_DEV, -1, LANES)
    head = jnp.concatenate([rep, loss_row], axis=0)
    send = jnp.concatenate([jnp.broadcast_to(head[None], (N_DEV,) + head.shape), blocks], axis=1)
    parts = _run_comm(_small_scatter_comm(send), "small_grad_exchange")[0]

    def own_rows(d):
        return jnp.concatenate([_pack([d[k] for k in REPLICATED]), jnp.ones((1, LANES), F32),
                                _pack([d[k] for k in SHARDED])], axis=0)

    res_small = _adam_small(parts, own_rows(w_small), own_rows(m_small), own_rows(v_small), name="adam_small")
    n_rep = rep.shape[0]
    loss = res_small[0][n_rep, 0]
    small_out = {k: [] for k in SMALL}
    for packed in res_small:
        for k, t in zip(REPLICATED, _unpack(packed[:n_rep], [w_small[k].shape for k in REPLICATED])):
            small_out[k].append(t)
        for k, t in zip(SHARDED, _unpack(packed[n_rep + 1:], [w_small[k].shape for k in SHARDED])):
            small_out[k].append(t)

    big_of = dict(e_w_in="e_in", e_w_out="e_out", o_w_in="o_in", o_c_w="o_cw", o_w_out="o_out")
    stacked = ("e_a_conv", "e_b_conv", "o_c_b")

    def leaf(name, which):
        if name in big_of:
            return big_out[big_of[name]][which][None]
        t = small_out[name][which]
        return t[None] if name in stacked else t

    order = ("e_norm_pre", "e_norm_post", "e_w_in", "e_a_conv", "e_b_conv", "e_b_conv_bias", "e_b_ln_g", "e_b_ln_b",
             "e_w_out", "o_norm_pre", "o_norm_post", "o_w_in", "o_c_w", "o_c_b", "o_c_scale", "o_w_out")
    outs = [loss, grad_x[None]]
    for which in range(4):
        outs += [leaf(nm, which) for nm in order]
    return tuple(outs)
```
